```python
import jax, jax.numpy as jnp
from jax import lax
import numpy as np

D_MODEL = 1024
BATCH = 8
SEQ = 8192
DEPTH = 4

CHUNK = 64
N_MIXERS = 2
GLA_HEADS = 4
GLA_DK = D_MODEL // 2
GLA_DV = D_MODEL
GLA_DK_HEAD = GLA_DK // GLA_HEADS
GLA_DV_HEAD = GLA_DV // GLA_HEADS
GLA_GATE_RANK = 16
GLA_GATE_TAU = 16.0
GLA_IN_WIDTH = 2 * GLA_DK + 2 * GLA_DV + GLA_GATE_RANK
ATT_HEADS = 16
ATT_HEAD_DIM = D_MODEL // ATT_HEADS
LEFT_CHUNKS = 8
BAND = (LEFT_CHUNKS + 1) * CHUNK
MAX_REL = 128
N_REL = 2 * MAX_REL + 1
D_FF = 4 * D_MODEL
DEEPNORM_ALPHA = (2.0 * DEPTH) ** 0.25
DEEPNORM_BETA = (8.0 * DEPTH) ** -0.25
LN_EPS = 1e-5
RMS_EPS = 1e-6
NEG_INF = -1e30
N_GLA_LAYERS = (DEPTH + 1) // 2
N_ATT_LAYERS = DEPTH // 2

kernel_name = "hybrid_gla_chunkattn_deepnorm_adaln"


def layer_norm(x, g, b):
    xf = x.astype(jnp.float32)
    mu = jnp.mean(xf, -1, keepdims=True)
    var = jnp.mean(jnp.square(xf - mu), -1, keepdims=True)
    return ((xf - mu) * lax.rsqrt(var + LN_EPS)).astype(x.dtype) * g + b


def gla_mixer(u, w_in, w_gk2, b_gk, g_norm, w_out):
    B_, S_, _ = u.shape
    nC = S_ // CHUNK
    proj = u @ w_in
    q, k, v, g, gk_lr = jnp.split(
        proj, [GLA_DK, 2 * GLA_DK, 2 * GLA_DK + GLA_DV, 2 * GLA_DK + 2 * GLA_DV], axis=-1)
    log_a = jax.nn.log_sigmoid((gk_lr @ w_gk2 + b_gk).astype(jnp.float32)) / GLA_GATE_TAU

    def to_chunks(t, hd):
        return t.reshape(B_, nC, CHUNK, GLA_HEADS, hd).transpose(0, 3, 1, 2, 4)

    qc = to_chunks(q.astype(jnp.float32), GLA_DK_HEAD) * (GLA_DK_HEAD ** -0.5)
    kc = to_chunks(k.astype(jnp.float32), GLA_DK_HEAD)
    vc = to_chunks(v.astype(jnp.float32), GLA_DV_HEAD)
    cum = jnp.cumsum(to_chunks(log_a, GLA_DK_HEAD), axis=3)
    e_pos = jnp.exp(cum)
    e_neg = jnp.exp(-cum)
    q_fwd = qc * e_pos
    a_fwd = jnp.einsum('bhntd,bhnsd->bhnts', q_fwd, kc * e_neg)
    a_bwd = jnp.einsum('bhntd,bhnsd->bhnts', qc * e_neg, kc * e_pos)
    lower = jnp.tril(jnp.ones((CHUNK, CHUNK), dtype=bool))
    o_intra = jnp.einsum('bhnts,bhnsv->bhntv', jnp.where(lower, a_fwd, a_bwd), vc)
    k_to_end = kc * jnp.exp(cum[:, :, :, -1:, :] - cum)
    chunk_decay = jnp.exp(cum[:, :, :, -1, :])

    def step(state, inp):
        qs, ke, vv, dec = inp
        o = jnp.einsum('bhtk,bhkv->bhtv', qs, state)
        state = state * dec[..., None] + jnp.einsum('bhtk,bhtv->bhkv', ke, vv)
        return state, o

    s0 = jnp.zeros((B_, GLA_HEADS, GLA_DK_HEAD, GLA_DV_HEAD), jnp.float32)
    _, o_inter = lax.scan(step, s0, (jnp.moveaxis(q_fwd, 2, 0), jnp.moveaxis(k_to_end, 2, 0),
                                     jnp.moveaxis(vc, 2, 0), jnp.moveaxis(chunk_decay, 2, 0)))
    o = o_intra + jnp.moveaxis(o_inter, 0, 2)
    o = o * lax.rsqrt(jnp.mean(jnp.square(o), -1, keepdims=True) + RMS_EPS)
    o = o * g_norm.astype(jnp.float32)[None, :, None, None, :]
    o = o.transpose(0, 2, 3, 1, 4).reshape(B_, S_, GLA_DV).astype(u.dtype)
    return (o * jax.nn.silu(g)) @ w_out


def chunk_attention(u, w_in, b_in, rel_bias, w_out):
    B_, S_, _ = u.shape
    nC = S_ // CHUNK
    q, k, v = jnp.split(u @ w_in + b_in, 3, axis=-1)

    def heads(t):
        return t.reshape(B_, S_, ATT_HEADS, ATT_HEAD_DIM).transpose(0, 2, 1, 3)

    pad = LEFT_CHUNKS * CHUNK
    qc = (heads(q) * (ATT_HEAD_DIM ** -0.5)).reshape(B_, ATT_HEADS, nC, CHUNK, ATT_HEAD_DIM)
    kp = jnp.pad(heads(k), ((0, 0), (0, 0), (pad, 0), (0, 0)))
    vp = jnp.pad(heads(v), ((0, 0), (0, 0), (pad, 0), (0, 0)))
    key_valid = jnp.arange(S_ + pad) >= pad
    rel = jnp.clip(pad + jnp.arange(CHUNK)[:, None] - jnp.arange(BAND)[None, :], -MAX_REL, MAX_REL) + MAX_REL
    bias = rel_bias.astype(jnp.float32)[:, rel]

    def one_chunk(inp):
        qi, start = inp
        kb = lax.dynamic_slice_in_dim(kp, start, BAND, axis=2)
        vb = lax.dynamic_slice_in_dim(vp, start, BAND, axis=2)
        mb = lax.dynamic_slice_in_dim(key_valid, start, BAND)
        s = jnp.einsum('bhtd,bhjd->bhtj', qi, kb).astype(jnp.float32) + bias
        p = jax.nn.softmax(jnp.where(mb, s, NEG_INF), axis=-1).astype(vb.dtype)
        return jnp.einsum('bhtj,bhjd->bhtd', p, vb)

    o = lax.map(one_chunk, (jnp.moveaxis(qc, 2, 0), jnp.arange(nC) * CHUNK))
    o = o.transpose(1, 0, 3, 2, 4).reshape(B_, S_, D_MODEL)
    return o @ w_out


def squared_relu_mlp(u, w1, w2):
    return jnp.square(jax.nn.relu(u @ w1)) @ w2


def _fwd_setup_inputs(seed: int = 0) -> dict:
    key = jax.random.key(seed)
    ks = jax.random.split(key, 16)
    nrm = lambda k, shape, s: jax.random.normal(k, shape, jnp.float32) * s
    D = D_MODEL
    return {
        "x": nrm(ks[0], (BATCH, SEQ, D), 1.0),
        "c": nrm(ks[1], (BATCH, D), 1.0),
        "w_ada": nrm(ks[2], (DEPTH, D, 6 * D), 0.1 * D ** -0.5),
        "b_ada": nrm(ks[3], (DEPTH, 6 * D), 0.02),
        "ln_g": 1.0 + nrm(ks[4], (DEPTH, 2, D), 0.02),
        "ln_b": nrm(ks[5], (DEPTH, 2, D), 0.02),
        "gla_w_in": nrm(ks[6], (N_GLA_LAYERS, D, GLA_IN_WIDTH), D ** -0.5),
        "gla_w_gk2": nrm(ks[7], (N_GLA_LAYERS, GLA_GATE_RANK, GLA_DK), GLA_GATE_RANK ** -0.5),
        "gla_b_gk": nrm(ks[8], (N_GLA_LAYERS, GLA_DK), 0.1),
        "gla_g_norm": 1.0 + nrm(ks[9], (N_GLA_LAYERS, GLA_HEADS, GLA_DV_HEAD), 0.02),
        "gla_w_out": nrm(ks[10], (N_GLA_LAYERS, GLA_DV, D), DEEPNORM_BETA * GLA_DV ** -0.5),
        "att_w_in": nrm(ks[11], (N_ATT_LAYERS, D, 3 * D), D ** -0.5),
        "att_b_in": nrm(ks[12], (N_ATT_LAYERS, 3 * D), 0.02),
        "att_rel_bias": nrm(ks[13], (N_ATT_LAYERS, ATT_HEADS, N_REL), 0.2),
        "att_w_out": nrm(ks[14], (N_ATT_LAYERS, D, D), DEEPNORM_BETA * D ** -0.5),
        "ff_w1": nrm(jax.random.fold_in(ks[15], 0), (DEPTH, D, D_FF), D ** -0.5),
        "ff_w2": nrm(jax.random.fold_in(ks[15], 1), (DEPTH, D_FF, D), DEEPNORM_BETA * D_FF ** -0.5),
    }


def _fwd_reference(x, c, w_ada, b_ada, ln_g, ln_b, gla_w_in, gla_w_gk2, gla_b_gk, gla_g_norm, gla_w_out,
              att_w_in, att_b_in, att_rel_bias, att_w_out, ff_w1, ff_w2):
    c_act = jax.nn.silu(c)
    for i in range(DEPTH):
        mods = jnp.split(c_act @ w_ada[i] + b_ada[i], 6, axis=-1)
        sh1, sc1, g1, sh2, sc2, g2 = [m[:, None, :] for m in mods]
        u = x * (1.0 + sc1) + sh1
        j = i // N_MIXERS
        if i % N_MIXERS == 0:
            y = gla_mixer(u, gla_w_in[j], gla_w_gk2[j], gla_b_gk[j], gla_g_norm[j], gla_w_out[j])
        else:
            y = chunk_attention(u, att_w_in[j], att_b_in[j], att_rel_bias[j], att_w_out[j])
        x = layer_norm(DEEPNORM_ALPHA * x + (1.0 + g1) * y, ln_g[i, 0], ln_b[i, 0])
        u = x * (1.0 + sc2) + sh2
        y = squared_relu_mlp(u, ff_w1[i], ff_w2[i])
        x = layer_norm(DEEPNORM_ALPHA * x + (1.0 + g2) * y, ln_g[i, 1], ln_b[i, 1])
    return x


import jax as _jax
import jax.numpy as _jnp

TWIN_FORMAT = 'train_step'
FWD_PARAMS = ['x', 'c', 'w_ada', 'b_ada', 'ln_g', 'ln_b', 'gla_w_in', 'gla_w_gk2', 'gla_b_gk', 'gla_g_norm', 'gla_w_out', 'att_w_in', 'att_b_in', 'att_rel_bias', 'att_w_out', 'ff_w1', 'ff_w2']
TWIN_WEIGHTS = ['w_ada', 'b_ada', 'ln_g', 'ln_b', 'gla_w_in', 'gla_w_gk2', 'gla_b_gk', 'gla_g_norm', 'gla_w_out', 'att_w_in', 'att_b_in', 'att_rel_bias', 'att_w_out', 'ff_w1', 'ff_w2']
TWIN_DIFF_INPUT = 'x'
TWIN_INPUTS = ['x', 'c', 'w_ada', 'b_ada', 'ln_g', 'ln_b', 'gla_w_in', 'gla_w_gk2', 'gla_b_gk', 'gla_g_norm', 'gla_w_out', 'att_w_in', 'att_b_in', 'att_rel_bias', 'att_w_out', 'ff_w1', 'ff_w2', 'loss_target', 'm_w_ada', 'm_b_ada', 'm_ln_g', 'm_ln_b', 'm_gla_w_in', 'm_gla_w_gk2', 'm_gla_b_gk', 'm_gla_g_norm', 'm_gla_w_out', 'm_att_w_in', 'm_att_b_in', 'm_att_rel_bias', 'm_att_w_out', 'm_ff_w1', 'm_ff_w2', 'v_w_ada', 'v_b_ada', 'v_ln_g', 'v_ln_b', 'v_gla_w_in', 'v_gla_w_gk2', 'v_gla_b_gk', 'v_gla_g_norm', 'v_gla_w_out', 'v_att_w_in', 'v_att_b_in', 'v_att_rel_bias', 'v_att_w_out', 'v_ff_w1', 'v_ff_w2']
TWIN_OUTPUTS = ['loss', 'grad_x', 'grad_w_ada', 'grad_b_ada', 'grad_ln_g', 'grad_ln_b', 'grad_gla_w_in', 'grad_gla_w_gk2', 'grad_gla_b_gk', 'grad_gla_g_norm', 'grad_gla_w_out', 'grad_att_w_in', 'grad_att_b_in', 'grad_att_rel_bias', 'grad_att_w_out', 'grad_ff_w1', 'grad_ff_w2', 'delta_w_ada', 'delta_b_ada', 'delta_ln_g', 'delta_ln_b', 'delta_gla_w_in', 'delta_gla_w_gk2', 'delta_gla_b_gk', 'delta_gla_g_norm', 'delta_gla_w_out', 'delta_att_w_in', 'delta_att_b_in', 'delta_att_rel_bias', 'delta_att_w_out', 'delta_ff_w1', 'delta_ff_w2', 'new_m_w_ada', 'new_m_b_ada', 'new_m_ln_g', 'new_m_ln_b', 'new_m_gla_w_in', 'new_m_gla_w_gk2', 'new_m_gla_b_gk', 'new_m_gla_g_norm', 'new_m_gla_w_out', 'new_m_att_w_in', 'new_m_att_b_in', 'new_m_att_rel_bias', 'new_m_att_w_out', 'new_m_ff_w1', 'new_m_ff_w2', 'new_v_w_ada', 'new_v_b_ada', 'new_v_ln_g', 'new_v_ln_b', 'new_v_gla_w_in', 'new_v_gla_w_gk2', 'new_v_gla_b_gk', 'new_v_gla_g_norm', 'new_v_gla_w_out', 'new_v_att_w_in', 'new_v_att_b_in', 'new_v_att_rel_bias', 'new_v_att_w_out', 'new_v_ff_w1', 'new_v_ff_w2']
TWIN_LEAF_KINDS = {'loss': 'loss', 'grad_x': 'grad_x', 'grad_w_ada': 'grad_w', 'grad_b_ada': 'grad_w', 'grad_ln_g': 'grad_w', 'grad_ln_b': 'grad_w', 'grad_gla_w_in': 'grad_w', 'grad_gla_w_gk2': 'grad_w', 'grad_gla_b_gk': 'grad_w', 'grad_gla_g_norm': 'grad_w', 'grad_gla_w_out': 'grad_w', 'grad_att_w_in': 'grad_w', 'grad_att_b_in': 'grad_w', 'grad_att_rel_bias': 'grad_w', 'grad_att_w_out': 'grad_w', 'grad_ff_w1': 'grad_w', 'grad_ff_w2': 'grad_w', 'delta_w_ada': 'delta_w', 'delta_b_ada': 'delta_w', 'delta_ln_g': 'delta_w', 'delta_ln_b': 'delta_w', 'delta_gla_w_in': 'delta_w', 'delta_gla_w_gk2': 'delta_w', 'delta_gla_b_gk': 'delta_w', 'delta_gla_g_norm': 'delta_w', 'delta_gla_w_out': 'delta_w', 'delta_att_w_in': 'delta_w', 'delta_att_b_in': 'delta_w', 'delta_att_rel_bias': 'delta_w', 'delta_att_w_out': 'delta_w', 'delta_ff_w1': 'delta_w', 'delta_ff_w2': 'delta_w', 'new_m_w_ada': 'new_m', 'new_m_b_ada': 'new_m', 'new_m_ln_g': 'new_m', 'new_m_ln_b': 'new_m', 'new_m_gla_w_in': 'new_m', 'new_m_gla_w_gk2': 'new_m', 'new_m_gla_b_gk': 'new_m', 'new_m_gla_g_norm': 'new_m', 'new_m_gla_w_out': 'new_m', 'new_m_att_w_in': 'new_m', 'new_m_att_b_in': 'new_m', 'new_m_att_rel_bias': 'new_m', 'new_m_att_w_out': 'new_m', 'new_m_ff_w1': 'new_m', 'new_m_ff_w2': 'new_m', 'new_v_w_ada': 'new_v', 'new_v_b_ada': 'new_v', 'new_v_ln_g': 'new_v', 'new_v_ln_b': 'new_v', 'new_v_gla_w_in': 'new_v', 'new_v_gla_w_gk2': 'new_v', 'new_v_gla_b_gk': 'new_v', 'new_v_gla_g_norm': 'new_v', 'new_v_gla_w_out': 'new_v', 'new_v_att_w_in': 'new_v', 'new_v_att_b_in': 'new_v', 'new_v_att_rel_bias': 'new_v', 'new_v_att_w_out': 'new_v', 'new_v_ff_w1': 'new_v', 'new_v_ff_w2': 'new_v'}


def _forward(args):
    return _fwd_reference(*[args[k] for k in FWD_PARAMS])


def _output_shape():
    def fwd():
        inp = _fwd_setup_inputs(0)
        return _fwd_reference(*[inp[k] for k in FWD_PARAMS])
    out = _jax.eval_shape(fwd)
    return out.shape, out.dtype

N_MICROBATCH = 1
ADAM_LR = 0.001
ADAM_B1 = 0.9
ADAM_B2 = 0.999
ADAM_EPS = 1e-08
ADAM_WD = 0.01
ADAM_STEP = 10
PER_EXAMPLE_BATCH_AXIS = {'x': 0, 'c': 0, 'loss_target': 0}
SHARED_INPUTS = []
_WEIGHT_DTYPES = {'w_ada': _jnp.float32, 'b_ada': _jnp.float32, 'ln_g': _jnp.float32, 'ln_b': _jnp.float32, 'gla_w_in': _jnp.float32, 'gla_w_gk2': _jnp.float32, 'gla_b_gk': _jnp.float32, 'gla_g_norm': _jnp.float32, 'gla_w_out': _jnp.float32, 'att_w_in': _jnp.float32, 'att_b_in': _jnp.float32, 'att_rel_bias': _jnp.float32, 'att_w_out': _jnp.float32, 'ff_w1': _jnp.float32, 'ff_w2': _jnp.float32}
MOMENT_SCALE = {'w_ada': 5.328714e-02, 'b_ada': 1.352178e-01, 'ln_g': 2.285562e+01, 'ln_b': 5.424982e+00, 'gla_w_in': 4.636264e-02, 'gla_w_gk2': 7.063162e-03, 'gla_b_gk': 2.398975e-02, 'gla_g_norm': 3.987625e-02, 'gla_w_out': 9.364165e-02, 'att_w_in': 2.066739e-02, 'att_b_in': 1.594686e-01, 'att_rel_bias': 4.404597e-03, 'att_w_out': 9.333087e-02, 'ff_w1': 4.594834e-02, 'ff_w2': 3.027847e-01}


def _to_microbatches(a, axis):
    t = _jnp.moveaxis(a, axis, 0)
    t = t.reshape((N_MICROBATCH, t.shape[0] // N_MICROBATCH) + t.shape[1:])
    return _jnp.moveaxis(t, 1, axis + 1)


def setup_inputs(seed: int = 0) -> dict:
    inp = _fwd_setup_inputs(seed)
    key = _jax.random.fold_in(_jax.random.key(seed), 7919)
    shape, _ = _output_shape()
    out = dict(inp)
    out["loss_target"] = _jax.random.normal(_jax.random.fold_in(key, 0), shape, _jnp.float32)
    for i, name in enumerate(TWIN_WEIGHTS):
        w = inp[name].astype(_jnp.float32)
        if MOMENT_SCALE is None:
            s = _jnp.sqrt(_jnp.mean(_jnp.square(w)) + 1e-30)
        else:
            s = MOMENT_SCALE[name]
        km, kv = _jax.random.split(_jax.random.fold_in(key, i + 1))
        out[name] = w
        out["m_" + name] = s * _jax.random.normal(km, w.shape, _jnp.float32)
        out["v_" + name] = (s * s) * _jax.random.uniform(kv, w.shape, _jnp.float32, 0.5, 1.5)
    if N_MICROBATCH > 1:
        for name, axis in PER_EXAMPLE_BATCH_AXIS.items():
            out[name] = _to_microbatches(out[name], axis)
    return {'x': out['x'], 'c': out['c'], 'w_ada': out['w_ada'], 'b_ada': out['b_ada'], 'ln_g': out['ln_g'], 'ln_b': out['ln_b'], 'gla_w_in': out['gla_w_in'], 'gla_w_gk2': out['gla_w_gk2'], 'gla_b_gk': out['gla_b_gk'], 'gla_g_norm': out['gla_g_norm'], 'gla_w_out': out['gla_w_out'], 'att_w_in': out['att_w_in'], 'att_b_in': out['att_b_in'], 'att_rel_bias': out['att_rel_bias'], 'att_w_out': out['att_w_out'], 'ff_w1': out['ff_w1'], 'ff_w2': out['ff_w2'], 'loss_target': out['loss_target'], 'm_w_ada': out['m_w_ada'], 'm_b_ada': out['m_b_ada'], 'm_ln_g': out['m_ln_g'], 'm_ln_b': out['m_ln_b'], 'm_gla_w_in': out['m_gla_w_in'], 'm_gla_w_gk2': out['m_gla_w_gk2'], 'm_gla_b_gk': out['m_gla_b_gk'], 'm_gla_g_norm': out['m_gla_g_norm'], 'm_gla_w_out': out['m_gla_w_out'], 'm_att_w_in': out['m_att_w_in'], 'm_att_b_in': out['m_att_b_in'], 'm_att_rel_bias': out['m_att_rel_bias'], 'm_att_w_out': out['m_att_w_out'], 'm_ff_w1': out['m_ff_w1'], 'm_ff_w2': out['m_ff_w2'], 'v_w_ada': out['v_w_ada'], 'v_b_ada': out['v_b_ada'], 'v_ln_g': out['v_ln_g'], 'v_ln_b': out['v_ln_b'], 'v_gla_w_in': out['v_gla_w_in'], 'v_gla_w_gk2': out['v_gla_w_gk2'], 'v_gla_b_gk': out['v_gla_b_gk'], 'v_gla_g_norm': out['v_gla_g_norm'], 'v_gla_w_out': out['v_gla_w_out'], 'v_att_w_in': out['v_att_w_in'], 'v_att_b_in': out['v_att_b_in'], 'v_att_rel_bias': out['v_att_rel_bias'], 'v_att_w_out': out['v_att_w_out'], 'v_ff_w1': out['v_ff_w1'], 'v_ff_w2': out['v_ff_w2']}


def _loss(weights, diff, rest, loss_target):
    with _jax.named_scope("forward"):
        args = {**rest, TWIN_DIFF_INPUT: diff, **{k: w.astype(_WEIGHT_DTYPES[k]) for k, w in weights.items()}}
        y = _forward(args)
    with _jax.named_scope("loss_head"):
        err = _jnp.square(y.astype(_jnp.float32) - loss_target)
        return 0.5 * _jnp.sum(_jnp.mean(err, axis=-1)) if err.ndim else 0.5 * err


def _adamw(w, g, m, v):
    m = ADAM_B1 * m + (1.0 - ADAM_B1) * g
    v = ADAM_B2 * v + (1.0 - ADAM_B2) * _jnp.square(g)
    m_hat = m / (1.0 - ADAM_B1 ** ADAM_STEP)
    v_hat = v / (1.0 - ADAM_B2 ** ADAM_STEP)
    delta = -ADAM_LR * (m_hat / (_jnp.sqrt(v_hat) + ADAM_EPS) + ADAM_WD * w)
    return delta, m, v


def reference(x, c, w_ada, b_ada, ln_g, ln_b, gla_w_in, gla_w_gk2, gla_b_gk, gla_g_norm, gla_w_out, att_w_in, att_b_in, att_rel_bias, att_w_out, ff_w1, ff_w2, loss_target, m_w_ada, m_b_ada, m_ln_g, m_ln_b, m_gla_w_in, m_gla_w_gk2, m_gla_b_gk, m_gla_g_norm, m_gla_w_out, m_att_w_in, m_att_b_in, m_att_rel_bias, m_att_w_out, m_ff_w1, m_ff_w2, v_w_ada, v_b_ada, v_ln_g, v_ln_b, v_gla_w_in, v_gla_w_gk2, v_gla_b_gk, v_gla_g_norm, v_gla_w_out, v_att_w_in, v_att_b_in, v_att_rel_bias, v_att_w_out, v_ff_w1, v_ff_w2):
    given = dict(x=x, c=c, w_ada=w_ada, b_ada=b_ada, ln_g=ln_g, ln_b=ln_b, gla_w_in=gla_w_in, gla_w_gk2=gla_w_gk2, gla_b_gk=gla_b_gk, gla_g_norm=gla_g_norm, gla_w_out=gla_w_out, att_w_in=att_w_in, att_b_in=att_b_in, att_rel_bias=att_rel_bias, att_w_out=att_w_out, ff_w1=ff_w1, ff_w2=ff_w2, loss_target=loss_target, m_w_ada=m_w_ada, m_b_ada=m_b_ada, m_ln_g=m_ln_g, m_ln_b=m_ln_b, m_gla_w_in=m_gla_w_in, m_gla_w_gk2=m_gla_w_gk2, m_gla_b_gk=m_gla_b_gk, m_gla_g_norm=m_gla_g_norm, m_gla_w_out=m_gla_w_out, m_att_w_in=m_att_w_in, m_att_b_in=m_att_b_in, m_att_rel_bias=m_att_rel_bias, m_att_w_out=m_att_w_out, m_ff_w1=m_ff_w1, m_ff_w2=m_ff_w2, v_w_ada=v_w_ada, v_b_ada=v_b_ada, v_ln_g=v_ln_g, v_ln_b=v_ln_b, v_gla_w_in=v_gla_w_in, v_gla_w_gk2=v_gla_w_gk2, v_gla_b_gk=v_gla_b_gk, v_gla_g_norm=v_gla_g_norm, v_gla_w_out=v_gla_w_out, v_att_w_in=v_att_w_in, v_att_b_in=v_att_b_in, v_att_rel_bias=v_att_rel_bias, v_att_w_out=v_att_w_out, v_ff_w1=v_ff_w1, v_ff_w2=v_ff_w2)
    weights = {n: given[n] for n in TWIN_WEIGHTS}
    shared = {n: given[n] for n in SHARED_INPUTS}
    per_example = {n: given[n] for n in ['x', 'c']}
    grad_fn = _jax.value_and_grad(_loss, argnums=(0, 1))

    def one_microbatch(ex, loss_target):
        ex = dict(ex)
        diff = ex.pop(TWIN_DIFF_INPUT)
        return grad_fn(weights, diff, {**shared, **ex}, loss_target)

    if N_MICROBATCH == 1:
        loss, (grad_w, grad_x) = one_microbatch(per_example, given["loss_target"])
    else:
        def body(carry, xs):
            loss_sum, grad_sum = carry
            l_k, (gw_k, gx_k) = one_microbatch(xs[0], xs[1])
            with _jax.named_scope("update"):
                return (loss_sum + l_k, _jax.tree.map(_jnp.add, grad_sum, gw_k)), gx_k

        init = (_jnp.zeros((), _jnp.float32), _jax.tree.map(_jnp.zeros_like, weights))
        (loss, grad_w), grad_x = _jax.lax.scan(body, init, (per_example, given["loss_target"]))
    with _jax.named_scope("update"):
        delta_w, new_m, new_v = {}, {}, {}
        for n in TWIN_WEIGHTS:
            delta_w[n], new_m[n], new_v[n] = _adamw(weights[n], grad_w[n], given["m_" + n], given["v_" + n])
    return (loss, grad_x, *[grad_w[n] for n in TWIN_WEIGHTS], *[delta_w[n] for n in TWIN_WEIGHTS],
            *[new_m[n] for n in TWIN_WEIGHTS], *[new_v[n] for n in TWIN_WEIGHTS])
```

```python
import functools

import numpy as np
import jax
import jax.numpy as jnp
from jax import lax
from jax.experimental import pallas as pl
from jax.experimental.pallas import tpu as pltpu

F32 = jnp.float32
BF16 = jnp.bfloat16
HIGHEST = lax.Precision.HIGHEST
MESH = pl.DeviceIdType.MESH

N_DEV = 8
D = 1024
DEPTH = 4
CHUNK = 64
ALPHA = (2.0 * DEPTH) ** 0.25
LN_EPS = 1e-5
RMS_EPS = 1e-6
NEG_INF = -1e30

GLA_H = 4
GLA_DKH = 128
GLA_DVH = 256
GLA_DK = GLA_H * GLA_DKH
GLA_DV = GLA_H * GLA_DVH
GLA_RANK = 16
GLA_IN = 2 * GLA_DK + 2 * GLA_DV + GLA_RANK
GLA_INP = 3200
GLA_TAU_INV = 1.0 / 16.0

ATT_H = 16
ATT_HD = 64
ATT_QB = 256
ATT_KB = 3 * ATT_QB
LEFT = 8 * CHUNK
MAX_REL = 128
N_REL = 2 * MAX_REL + 1
REL_PAD = 384
D_FF = 4 * D

ADAM_LR = 0.001
ADAM_B1 = 0.9
ADAM_B2 = 0.999
ADAM_EPS = 1e-08
ADAM_WD = 0.01
ADAM_STEP = 10

VMEM_LIMIT = 48 * 1024 * 1024
PACK_COLS = 1024
PACK_BLOCK_ROWS = 448


def _params(n_axes):
    return pltpu.CompilerParams(dimension_semantics=("arbitrary",) * n_axes, vmem_limit_bytes=VMEM_LIMIT)


def _dot(a, b):
    return jnp.dot(a, b, preferred_element_type=F32)


def _dot_nt(a, b):
    return lax.dot_general(a, b, (((1,), (1,)), ((), ())), preferred_element_type=F32)


def _dot_tn(a, b):
    return lax.dot_general(a, b, (((0,), (0,)), ((), ())), preferred_element_type=F32)


def _bf(a):
    return a.astype(BF16)


def _prologue(kind, a, p1=None, p2=None):
    if kind == "mod":
        return a * (1.0 + p1) + p2
    if kind == "scale":
        return a * (1.0 + p1)
    if kind == "relu2":
        r = jnp.maximum(a, 0.0)
        return r * r
    return a


def mm_nn(a, b, *, pro=None, p1=None, p2=None, bias=None, tm, tn, name):
    M, K = a.shape
    N = b.shape[1]
    tm = min(tm, M)
    n_p = {"mod": 2, "scale": 1}.get(pro, 0)
    has_bias = bias is not None

    def body(*refs):
        a_ref, b_ref = refs[0], refs[1]
        p_refs = refs[2:2 + n_p]
        bias_ref = refs[2 + n_p] if has_bias else None
        o_ref, abf = refs[-2], refs[-1]

        @pl.when(pl.program_id(1) == 0)
        def _():
            abf[...] = _bf(_prologue(pro, a_ref[...].astype(F32), *[r[...] for r in p_refs]))

        acc = _dot(abf[...], b_ref[...])
        if has_bias:
            acc = acc + bias_ref[...]
        o_ref[...] = acc

    in_specs = [pl.BlockSpec((tm, K), lambda i, j: (i, 0)), pl.BlockSpec((K, tn), lambda i, j: (0, j))]
    args = [a, b]
    for p in (p1, p2)[:n_p]:
        in_specs.append(pl.BlockSpec((1, K), lambda i, j: (0, 0)))
        args.append(p)
    if has_bias:
        in_specs.append(pl.BlockSpec((1, tn), lambda i, j: (0, j)))
        args.append(bias)
    return pl.pallas_call(
        body, name=name, grid=(M // tm, N // tn), in_specs=in_specs,
        out_specs=pl.BlockSpec((tm, tn), lambda i, j: (i, j)),
        out_shape=jax.ShapeDtypeStruct((M, N), F32),
        scratch_shapes=[pltpu.VMEM((tm, K), BF16)], compiler_params=_params(2),
    )(*args)


def mm_nt(a_parts, w, *, pro=None, p1=None, epi_h=None, out_dtype=F32, tm, tn, name):
    M = a_parts[0].shape[0]
    tm = min(tm, M)
    widths = [p.shape[1] for p in a_parts]
    Nw = sum(widths)
    Kw = w.shape[0]
    n_a = len(a_parts)
    has_p = pro == "scale"
    has_h = epi_h is not None

    def body(*refs):
        a_refs = refs[:n_a]
        w_ref = refs[n_a]
        k = n_a + 1
        p_ref = refs[k] if has_p else None
        k += int(has_p)
        h_ref = refs[k] if has_h else None
        o_ref, abf = refs[-2], refs[-1]

        @pl.when(pl.program_id(1) == 0)
        def _():
            off = 0
            for r, wd in zip(a_refs, widths):
                av = r[...]
                if has_p:
                    av = av.astype(F32) * (1.0 + p_ref[...])
                abf[:, off:off + wd] = _bf(av)
                off += wd

        acc = _dot_nt(abf[...], w_ref[...])
        if has_h:
            acc = acc * (2.0 * jnp.maximum(h_ref[...], 0.0))
        o_ref[...] = acc.astype(out_dtype)

    in_specs = [pl.BlockSpec((tm, wd), lambda i, j: (i, 0)) for wd in widths]
    in_specs.append(pl.BlockSpec((tn, Nw), lambda i, j: (j, 0)))
    args = list(a_parts) + [w]
    if has_p:
        in_specs.append(pl.BlockSpec((1, Nw), lambda i, j: (0, 0)))
        args.append(p1)
    if has_h:
        in_specs.append(pl.BlockSpec((tm, tn), lambda i, j: (i, j)))
        args.append(epi_h)
    return pl.pallas_call(
        body, name=name, grid=(M // tm, Kw // tn), in_specs=in_specs,
        out_specs=pl.BlockSpec((tm, tn), lambda i, j: (i, j)),
        out_shape=jax.ShapeDtypeStruct((M, Kw), out_dtype),
        scratch_shapes=[pltpu.VMEM((tm, Nw), BF16)], compiler_params=_params(2),
    )(*args)


def mm_tn(a, d, *, pro=None, p1=None, p2=None, dscale=None, tk, tn, tm, name):
    M, Kf = a.shape
    N = d.shape[1]
    n_p = {"mod": 2}.get(pro, 0)
    has_ds = dscale is not None
    n_m = M // tm

    def body(*refs):
        a_ref, d_ref = refs[0], refs[1]
        p_refs = refs[2:2 + n_p]
        ds_ref = refs[2 + n_p] if has_ds else None
        o_ref, acc = refs[-2], refs[-1]
        m = pl.program_id(2)

        @pl.when(m == 0)
        def _():
            acc[...] = jnp.zeros_like(acc)

        av = _bf(_prologue(pro, a_ref[...].astype(F32), *[r[...] for r in p_refs]))
        dv = d_ref[...]
        if has_ds:
            dv = dv.astype(F32) * (1.0 + ds_ref[...])
        acc[...] += _dot_tn(av, _bf(dv))

        @pl.when(m == n_m - 1)
        def _():
            o_ref[...] = acc[...]

    in_specs = [pl.BlockSpec((tm, tk), lambda i, j, m: (m, i)), pl.BlockSpec((tm, tn), lambda i, j, m: (m, j))]
    args = [a, d]
    for p in (p1, p2)[:n_p]:
        in_specs.append(pl.BlockSpec((1, tk), lambda i, j, m: (0, i)))
        args.append(p)
    if has_ds:
        in_specs.append(pl.BlockSpec((1, tn), lambda i, j, m: (0, j)))
        args.append(dscale)
    return pl.pallas_call(
        body, name=name, grid=(Kf // tk, N // tn, n_m), in_specs=in_specs,
        out_specs=pl.BlockSpec((tk, tn), lambda i, j, m: (i, j)),
        out_shape=jax.ShapeDtypeStruct((Kf, N), F32),
        scratch_shapes=[pltpu.VMEM((tk, tn), F32)], compiler_params=_params(3),
    )(*args)


ROW_BLOCK = 512
ACC_ROWS = 8


def _ln_stats(z):
    mu = jnp.mean(z, axis=-1, keepdims=True)
    zc = z - mu
    var = jnp.mean(zc * zc, axis=-1, keepdims=True)
    return zc, lax.rsqrt(var + LN_EPS)


def ln_fwd(x, y, gate, lng, lnb, *, name):
    S = x.shape[0]

    def body(x_ref, y_ref, gt_ref, g_ref, b_ref, o_ref):
        z = ALPHA * x_ref[...] + (1.0 + gt_ref[...]) * y_ref[...]
        zc, rstd = _ln_stats(z)
        o_ref[...] = (zc * rstd) * g_ref[...] + b_ref[...]

    row = pl.BlockSpec((ROW_BLOCK, D), lambda i: (i, 0))
    vec = pl.BlockSpec((1, D), lambda i: (0, 0))
    return pl.pallas_call(
        body, name=name, grid=(S // ROW_BLOCK,), in_specs=[row, row, vec, vec, vec], out_specs=row,
        out_shape=jax.ShapeDtypeStruct((S, D), F32), compiler_params=_params(1),
    )(x, y, gate, lng, lnb)


def _add_colsum(acc_ref, r, val):
    acc_ref[r:r + 1, :] += jnp.sum(val, axis=0, keepdims=True)


def ln_bwd(x_in, y, gate, lng, *, dout=None, nxt=None, name):
    S = x_in.shape[0]
    has_next = nxt is not None

    def body(*refs):
        if has_next:
            dzn_ref, dun_ref, scn_ref, xo_ref = refs[:4]
            k = 4
        else:
            do_ref = refs[0]
            k = 1
        x_ref, y_ref, gt_ref, g_ref = refs[k:k + 4]
        dz_ref, acc_ref = refs[k + 4:]

        @pl.when(pl.program_id(0) == 0)
        def _():
            acc_ref[...] = jnp.zeros_like(acc_ref)

        if has_next:
            du = dun_ref[...]
            dout_v = ALPHA * dzn_ref[...] + du * (1.0 + scn_ref[...])
            _add_colsum(acc_ref, 3, du * xo_ref[...])
            _add_colsum(acc_ref, 4, du)
        else:
            dout_v = do_ref[...]
        yv = y_ref[...]
        z = ALPHA * x_ref[...] + (1.0 + gt_ref[...]) * yv
        zc, rstd = _ln_stats(z)
        xhat = zc * rstd
        _add_colsum(acc_ref, 0, dout_v * xhat)
        _add_colsum(acc_ref, 1, dout_v)
        dxh = dout_v * g_ref[...]
        m1 = jnp.mean(dxh, axis=-1, keepdims=True)
        m2 = jnp.mean(dxh * xhat, axis=-1, keepdims=True)
        dz = rstd * (dxh - m1 - xhat * m2)
        _add_colsum(acc_ref, 2, dz * yv)
        dz_ref[...] = dz

    row = pl.BlockSpec((ROW_BLOCK, D), lambda i: (i, 0))
    vec = pl.BlockSpec((1, D), lambda i: (0, 0))
    if has_next:
        in_specs = [row, row, vec, row]
        args = list(nxt)
    else:
        in_specs = [row]
        args = [dout]
    in_specs += [row, row, vec, vec]
    args += [x_in, y, gate, lng]
    return pl.pallas_call(
        body, name=name, grid=(S // ROW_BLOCK,), in_specs=in_specs,
        out_specs=[row, pl.BlockSpec((ACC_ROWS, D), lambda i: (0, 0))],
        out_shape=[jax.ShapeDtypeStruct((S, D), F32), jax.ShapeDtypeStruct((ACC_ROWS, D), F32)],
        compiler_params=_params(1),
    )(*args)


def combine_final(dz, du, sc, x_in, *, name):
    S = dz.shape[0]

    def body(dz_ref, du_ref, sc_ref, x_ref, dx_ref, acc_ref):
        @pl.when(pl.program_id(0) == 0)
        def _():
            acc_ref[...] = jnp.zeros_like(acc_ref)

        du_v = du_ref[...]
        dx_ref[...] = ALPHA * dz_ref[...] + du_v * (1.0 + sc_ref[...])
        _add_colsum(acc_ref, 3, du_v * x_ref[...])
        _add_colsum(acc_ref, 4, du_v)

    row = pl.BlockSpec((ROW_BLOCK, D), lambda i: (i, 0))
    vec = pl.BlockSpec((1, D), lambda i: (0, 0))
    return pl.pallas_call(
        body, name=name, grid=(S // ROW_BLOCK,), in_specs=[row, row, vec, row],
        out_specs=[row, pl.BlockSpec((ACC_ROWS, D), lambda i: (0, 0))],
        out_shape=[jax.ShapeDtypeStruct((S, D), F32), jax.ShapeDtypeStruct((ACC_ROWS, D), F32)],
        compiler_params=_params(1),
    )(dz, du, sc, x_in)


def colsum(a, *, name):
    S, N = a.shape

    def body(a_ref, o_ref):
        @pl.when(pl.program_id(0) == 0)
        def _():
            o_ref[...] = jnp.zeros_like(o_ref)

        o_ref[...] += jnp.sum(a_ref[...], axis=0, keepdims=True)

    return pl.pallas_call(
        body, name=name, grid=(S // ROW_BLOCK,), in_specs=[pl.BlockSpec((ROW_BLOCK, N), lambda i: (i, 0))],
        out_specs=pl.BlockSpec((1, N), lambda i: (0, 0)), out_shape=jax.ShapeDtypeStruct((1, N), F32),
        compiler_params=_params(1),
    )(a)


def loss_head(y, t, *, name):
    S = y.shape[0]

    def body(y_ref, t_ref, dy_ref, l_ref):
        @pl.when(pl.program_id(0) == 0)
        def _():
            l_ref[...] = jnp.zeros_like(l_ref)

        e = y_ref[...] - t_ref[...]
        dy_ref[...] = e * (1.0 / D)
        per_tok = jnp.sum(e * e, axis=1, keepdims=True) * (1.0 / D)
        l_ref[...] += 0.5 * jnp.sum(per_tok, axis=0, keepdims=True)

    row = pl.BlockSpec((ROW_BLOCK, D), lambda i: (i, 0))
    return pl.pallas_call(
        body, name=name, grid=(S // ROW_BLOCK,), in_specs=[row, row],
        out_specs=[row, pl.BlockSpec((8, 128), lambda i: (0, 0))],
        out_shape=[jax.ShapeDtypeStruct((S, D), F32), jax.ShapeDtypeStruct((8, 128), F32)],
        compiler_params=_params(1),
    )(y, t)


def _log_sigmoid(x):
    return jnp.minimum(x, 0.0) - jnp.log(1.0 + jnp.exp(-jnp.abs(x)))


def _silu(x):
    return x * (1.0 / (1.0 + jnp.exp(-x)))


def _gla_chunk(q, k, v, g, gk, s0t, w2p, bgk, gn):
    C = q.shape[0]
    row = lax.broadcasted_iota(jnp.int32, (C, C), 0)
    col = lax.broadcasted_iota(jnp.int32, (C, C), 1)
    lower = row >= col
    tri = lower.astype(F32)
    la = _log_sigmoid(_dot(_bf(gk), _bf(w2p)) + bgk) * GLA_TAU_INV
    outs, states = [], []
    for h in range(GLA_H):
        ks = slice(h * GLA_DKH, (h + 1) * GLA_DKH)
        vs = slice(h * GLA_DVH, (h + 1) * GLA_DVH)
        qh = q[:, ks] * (GLA_DKH ** -0.5)
        kh, vh, gh, lah, s0 = k[:, ks], v[:, vs], g[:, vs], la[:, ks], s0t[h]
        cum = jnp.dot(tri, lah, precision=HIGHEST, preferred_element_type=F32)
        e_pos = jnp.exp(cum)
        e_neg = jnp.exp(-cum)
        q_f = qh * e_pos
        a_f = _dot_nt(_bf(q_f), _bf(kh * e_neg))
        a_b = _dot_nt(_bf(qh * e_neg), _bf(kh * e_pos))
        att = jnp.where(lower, a_f, a_b)
        o = _dot(_bf(att), _bf(vh)) + _dot_nt(_bf(q_f), _bf(s0))
        tot = jnp.sum(lah, axis=0, keepdims=True)
        k_end = kh * jnp.exp(tot - cum)
        states.append(s0 * jnp.exp(tot) + _dot_tn(_bf(vh), _bf(k_end)))
        on = o * lax.rsqrt(jnp.mean(o * o, axis=-1, keepdims=True) + RMS_EPS) * gn[:, vs]
        outs.append(on * _silu(gh))
    return jnp.concatenate(outs, axis=1), jnp.stack(states)


def _gla_split(p):
    return (p[:, 0:GLA_DK], p[:, GLA_DK:2 * GLA_DK], p[:, 2 * GLA_DK:2 * GLA_DK + GLA_DV],
            p[:, 2 * GLA_DK + GLA_DV:2 * GLA_DK + 2 * GLA_DV], p[:, 2 * GLA_DK + 2 * GLA_DV:GLA_INP])


def gla_fwd(proj, w2p, bgk, gn, *, name):
    S = proj.shape[0]
    n_c = S // CHUNK

    def body(p_ref, w_ref, b_ref, gn_ref, o_ref, st_ref, st):
        @pl.when(pl.program_id(0) == 0)
        def _():
            st[...] = jnp.zeros_like(st)

        s0 = st[...]
        st_ref[...] = s0
        og, s1 = _gla_chunk(*_gla_split(p_ref[...]), s0, w_ref[...], b_ref[...], gn_ref[...])
        o_ref[...] = og
        st[...] = s1

    full = lambda shape: pl.BlockSpec(shape, lambda i: (0,) * len(shape))
    return pl.pallas_call(
        body, name=name, grid=(n_c,),
        in_specs=[pl.BlockSpec((CHUNK, GLA_INP), lambda i: (i, 0)), full((128, GLA_DK)), full((1, GLA_DK)),
                  full((1, GLA_DV))],
        out_specs=[pl.BlockSpec((CHUNK, GLA_DV), lambda i: (i, 0)),
                   pl.BlockSpec((None, GLA_H, GLA_DVH, GLA_DKH), lambda i: (i, 0, 0, 0))],
        out_shape=[jax.ShapeDtypeStruct((S, GLA_DV), F32),
                   jax.ShapeDtypeStruct((n_c, GLA_H, GLA_DVH, GLA_DKH), F32)],
        scratch_shapes=[pltpu.VMEM((GLA_H, GLA_DVH, GLA_DKH), F32)], compiler_params=_params(1),
    )(proj, w2p, bgk, gn)


def gla_bwd(proj, dog, states, w2p, bgk, gn, *, name):
    S = proj.shape[0]
    n_c = S // CHUNK

    def body(p_ref, dog_ref, st_ref, w_ref, b_ref, gn_ref, dp_ref, dw_ref, db_ref, dgn_ref, ds):
        @pl.when(pl.program_id(0) == 0)
        def _():
            ds[...] = jnp.zeros_like(ds)
            dw_ref[...] = jnp.zeros_like(dw_ref)
            db_ref[...] = jnp.zeros_like(db_ref)
            dgn_ref[...] = jnp.zeros_like(dgn_ref)

        q, k, v, g, gk = _gla_split(p_ref[...])
        _, vjp = jax.vjp(_gla_chunk, q, k, v, g, gk, st_ref[...], w_ref[...], b_ref[...], gn_ref[...])
        dq, dk, dv, dg, dgk, ds0, dw, db, dgn = vjp((dog_ref[...], ds[...]))
        dp_ref[:, 0:GLA_DK] = dq
        dp_ref[:, GLA_DK:2 * GLA_DK] = dk
        dp_ref[:, 2 * GLA_DK:2 * GLA_DK + GLA_DV] = dv
        dp_ref[:, 2 * GLA_DK + GLA_DV:2 * GLA_DK + 2 * GLA_DV] = dg
        dp_ref[:, 2 * GLA_DK + 2 * GLA_DV:GLA_INP] = dgk
        ds[...] = ds0
        dw_ref[...] += dw
        db_ref[...] += db
        dgn_ref[...] += dgn

    full = lambda shape: pl.BlockSpec(shape, lambda i: (0,) * len(shape))
    rev = lambda i: (n_c - 1 - i, 0)
    return pl.pallas_call(
        body, name=name, grid=(n_c,),
        in_specs=[pl.BlockSpec((CHUNK, GLA_INP), rev), pl.BlockSpec((CHUNK, GLA_DV), rev),
                  pl.BlockSpec((None, GLA_H, GLA_DVH, GLA_DKH), lambda i: (n_c - 1 - i, 0, 0, 0)),
                  full((128, GLA_DK)), full((1, GLA_DK)), full((1, GLA_DV))],
        out_specs=[pl.BlockSpec((CHUNK, GLA_INP), rev), full((128, GLA_DK)), full((1, GLA_DK)), full((1, GLA_DV))],
        out_shape=[jax.ShapeDtypeStruct((S, GLA_INP), F32), jax.ShapeDtypeStruct((128, GLA_DK), F32),
                   jax.ShapeDtypeStruct((1, GLA_DK), F32), jax.ShapeDtypeStruct((1, GLA_DV), F32)],
        scratch_shapes=[pltpu.VMEM((GLA_H, GLA_DVH, GLA_DKH), F32)], compiler_params=_params(1),
    )(proj, dog, states, w2p, bgk, gn)


def _rel_index():
    tq = np.arange(ATT_QB)[:, None]
    j = np.arange(ATT_KB)[None, :]
    band = (j // CHUNK - tq // CHUNK >= 0) & (j // CHUNK - tq // CHUNK <= LEFT // CHUNK)
    dist = LEFT + tq - j
    idx = np.where(band, np.minimum(dist, MAX_REL) + MAX_REL, N_REL)
    return jnp.asarray(idx.reshape(1, ATT_QB * ATT_KB).astype(np.int32))


REL_BLOCK = 2048


def _one_hot(idx_row):
    return (lax.broadcasted_iota(jnp.int32, (REL_PAD, idx_row.shape[1]), 0) == idx_row).astype(F32)


def rel_bias_tile(rel_pad, idx, *, name):
    E = idx.shape[1]

    def body(r_ref, i_ref, o_ref):
        o_ref[...] = jnp.dot(r_ref[...], _one_hot(i_ref[...]), precision=HIGHEST, preferred_element_type=F32)

    return pl.pallas_call(
        body, name=name, grid=(E // REL_BLOCK,),
        in_specs=[pl.BlockSpec((ATT_H, REL_PAD), lambda i: (0, 0)), pl.BlockSpec((1, REL_BLOCK), lambda i: (0, i))],
        out_specs=pl.BlockSpec((ATT_H, REL_BLOCK), lambda i: (0, i)),
        out_shape=jax.ShapeDtypeStruct((ATT_H, E), F32), compiler_params=_params(1),
    )(rel_pad, idx)


def rel_bias_grad(dbias_flat, idx, *, name):
    E = idx.shape[1]

    def body(d_ref, i_ref, o_ref):
        @pl.when(pl.program_id(0) == 0)
        def _():
            o_ref[...] = jnp.zeros_like(o_ref)

        o_ref[...] += lax.dot_general(d_ref[...], _one_hot(i_ref[...]), (((1,), (1,)), ((), ())),
                                      precision=HIGHEST, preferred_element_type=F32)

    return pl.pallas_call(
        body, name=name, grid=(E // REL_BLOCK,),
        in_specs=[pl.BlockSpec((ATT_H, REL_BLOCK), lambda i: (0, i)), pl.BlockSpec((1, REL_BLOCK), lambda i: (0, i))],
        out_specs=pl.BlockSpec((ATT_H, REL_PAD), lambda i: (0, 0)),
        out_shape=jax.ShapeDtypeStruct((ATT_H, REL_PAD), F32), compiler_params=_params(1),
    )(dbias_flat, idx)


def _attn_pair(q, kw, vw, bias2, kvalid):
    lane = lax.broadcasted_iota(jnp.int32, (1, 2 * ATT_HD), 1)
    kb, vb = _bf(kw), _bf(vw)
    out = jnp.zeros_like(q)
    for a in range(2):
        m = ((lane >= a * ATT_HD) & (lane < (a + 1) * ATT_HD)).astype(F32)
        s = _dot_nt(_bf(q * (m * ATT_HD ** -0.5)), kb) + bias2[a]
        s = jnp.where(kvalid, s, NEG_INF)
        e = jnp.exp(s - jnp.max(s, axis=-1, keepdims=True))
        p = e / jnp.sum(e, axis=-1, keepdims=True)
        out = out + _dot(_bf(p), vb) * m
    return out


def _attn_specs():
    n_hp = ATT_H // 2
    q_spec = pl.BlockSpec((ATT_QB, 128), lambda hp, g: (g, hp))

    def win(col0, back):
        return pl.BlockSpec((ATT_QB, 128), lambda hp, g: (jnp.maximum(g - back, 0), col0 + hp))

    kv_specs = [win(n_hp, 2), win(n_hp, 1), win(n_hp, 0), win(2 * n_hp, 2), win(2 * n_hp, 1), win(2 * n_hp, 0)]
    bias_spec = pl.BlockSpec((2, ATT_QB, ATT_KB), lambda hp, g: (hp, 0, 0))
    return q_spec, kv_specs, bias_spec


def _attn_window(refs, g):
    kw = jnp.concatenate([r[...] for r in refs[0:3]], axis=0)
    vw = jnp.concatenate([r[...] for r in refs[3:6]], axis=0)
    j = lax.broadcasted_iota(jnp.int32, (1, ATT_KB), 1)
    return kw, vw, (j + (g - 2) * ATT_QB) >= 0


def attn_fwd(qkv, bias, *, name):
    S = qkv.shape[0]
    q_spec, kv_specs, bias_spec = _attn_specs()

    def body(q_ref, *rest):
        kv_refs, b_ref, o_ref = rest[:6], rest[6], rest[7]
        kw, vw, kvalid = _attn_window(kv_refs, pl.program_id(1))
        o_ref[...] = _attn_pair(q_ref[...], kw, vw, b_ref[...], kvalid)

    return pl.pallas_call(
        body, name=name, grid=(ATT_H // 2, S // ATT_QB), in_specs=[q_spec] + kv_specs + [bias_spec],
        out_specs=q_spec, out_shape=jax.ShapeDtypeStruct((S, D), F32), compiler_params=_params(2),
    )(*([qkv] * 7), bias)


def attn_bwd(qkv, do, bias, *, name):
    S = qkv.shape[0]
    q_spec, kv_specs, bias_spec = _attn_specs()
    col_spec = pl.BlockSpec((S, 128), lambda hp, g: (0, hp))

    def body(q_ref, *rest):
        kv_refs, b_ref, do_ref = rest[:6], rest[6], rest[7]
        dq_ref, dk_ref, dv_ref, db_ref = rest[8:]
        g = pl.program_id(1)

        @pl.when(g == 0)
        def _():
            dk_ref[...] = jnp.zeros_like(dk_ref)
            dv_ref[...] = jnp.zeros_like(dv_ref)
            db_ref[...] = jnp.zeros_like(db_ref)

        kw, vw, kvalid = _attn_window(kv_refs, g)
        _, vjp = jax.vjp(lambda q, k, v, b: _attn_pair(q, k, v, b, kvalid), q_ref[...], kw, vw, b_ref[...])
        dq, dkw, dvw, db = vjp(do_ref[...])
        dq_ref[...] = dq
        db_ref[...] += db
        for blk in range(3):
            src = g - 2 + blk

            @pl.when(src >= 0)
            def _(blk=blk, src=src):
                rows = pl.ds(pl.multiple_of(src * ATT_QB, ATT_QB), ATT_QB)
                dk_ref[rows, :] += dkw[blk * ATT_QB:(blk + 1) * ATT_QB]
                dv_ref[rows, :] += dvw[blk * ATT_QB:(blk + 1) * ATT_QB]

    return pl.pallas_call(
        body, name=name, grid=(ATT_H // 2, S // ATT_QB), in_specs=[q_spec] + kv_specs + [bias_spec, q_spec],
        out_specs=[q_spec, col_spec, col_spec, bias_spec],
        out_shape=[jax.ShapeDtypeStruct((S, D), F32)] * 3 + [jax.ShapeDtypeStruct((ATT_H, ATT_QB, ATT_KB), F32)],
        compiler_params=_params(2),
    )(*([qkv] * 7), bias, do)


def mods_partial(c_all, w_ada, *, name):
    n_l, _, n_c = w_ada.shape

    def body(c_ref, w_ref, o_ref):
        o_ref[...] = _dot(_bf(_silu(c_ref[...])), _bf(w_ref[...]))

    return pl.pallas_call(
        body, name=name, grid=(n_l,),
        in_specs=[pl.BlockSpec((N_DEV, D), lambda l: (0, 0)), pl.BlockSpec((None, D, n_c), lambda l: (l, 0, 0))],
        out_specs=pl.BlockSpec((None, N_DEV, n_c), lambda l: (l, 0, 0)),
        out_shape=jax.ShapeDtypeStruct((n_l, N_DEV, n_c), F32), compiler_params=_params(1),
    )(c_all, w_ada)


def w_ada_grad(c_all, dm, *, name):
    n_l, _, n_c = dm.shape

    def body(c_ref, d_ref, o_ref):
        o_ref[...] = lax.dot_general(_silu(c_ref[...]), d_ref[...], (((0,), (0,)), ((), ())),
                                     precision=HIGHEST, preferred_element_type=F32)

    return pl.pallas_call(
        body, name=name, grid=(n_l,),
        in_specs=[pl.BlockSpec((N_DEV, D), lambda l: (0, 0)), pl.BlockSpec((None, N_DEV, n_c), lambda l: (l, 0, 0))],
        out_specs=pl.BlockSpec((None, D, n_c), lambda l: (l, 0, 0)),
        out_shape=jax.ShapeDtypeStruct((n_l, D, n_c), F32), compiler_params=_params(1),
    )(c_all, dm)


def adamw(w, m, v, gparts, *, block_rows, name):
    R, C = w.shape
    n = gparts.shape[0]

    def body(w_ref, m_ref, v_ref, g_ref, go_ref, d_ref, mo_ref, vo_ref):
        g = g_ref[0].astype(F32)
        for k in range(1, n):
            g = g + g_ref[k].astype(F32)
        m_new = ADAM_B1 * m_ref[...] + (1.0 - ADAM_B1) * g
        v_new = ADAM_B2 * v_ref[...] + (1.0 - ADAM_B2) * (g * g)
        m_hat = m_new / (1.0 - ADAM_B1 ** ADAM_STEP)
        v_hat = v_new / (1.0 - ADAM_B2 ** ADAM_STEP)
        go_ref[...] = g
        d_ref[...] = -ADAM_LR * (m_hat / (jnp.sqrt(v_hat) + ADAM_EPS) + ADAM_WD * w_ref[...])
        mo_ref[...] = m_new
        vo_ref[...] = v_new

    blk = pl.BlockSpec((block_rows, C), lambda i: (i, 0))
    return pl.pallas_call(
        body, name=name, grid=(R // block_rows,),
        in_specs=[blk, blk, blk, pl.BlockSpec((n, block_rows, C), lambda i: (0, i, 0))],
        out_specs=[blk] * 4, out_shape=[jax.ShapeDtypeStruct((R, C), F32)] * 4, compiler_params=_params(1),
    )(w, m, v, gparts)


def adamw_nd(w, m, v, g, *, name):
    shape = w.shape
    two = (int(np.prod(shape[:-1])), shape[-1])
    rows = two[0]
    block_rows = rows
    for cand in (512, 256):
        if rows > cand and rows % cand == 0:
            block_rows = cand
            break
    outs = adamw(w.reshape(two), m.reshape(two), v.reshape(two), g.reshape((1,) + two), block_rows=block_rows, name=name)
    return [o.reshape(shape) for o in outs]


def sum_parts(parts, *, name):
    n, R, C = parts.shape

    def body(p_ref, o_ref):
        acc = p_ref[0]
        for k in range(1, n):
            acc = acc + p_ref[k]
        o_ref[...] = acc

    return pl.pallas_call(
        body, name=name, in_specs=[pl.BlockSpec((n, R, C), lambda: (0, 0, 0))],
        out_specs=pl.BlockSpec((R, C), lambda: (0, 0)), out_shape=jax.ShapeDtypeStruct((R, C), F32),
        compiler_params=pltpu.CompilerParams(vmem_limit_bytes=VMEM_LIMIT),
    )(parts)


def pair_sum_bf16(a, b, *, name):
    n, R, C = a.shape

    def body(a_ref, b_ref, o_ref):
        o_ref[...] = _bf(a_ref[...] + b_ref[...])

    blk = pl.BlockSpec((None, PACK_BLOCK_ROWS, C), lambda k, i: (k, i, 0))
    return pl.pallas_call(
        body, name=name, grid=(n, R // PACK_BLOCK_ROWS), in_specs=[blk, blk], out_specs=blk,
        out_shape=jax.ShapeDtypeStruct((n, R, C), BF16), compiler_params=_params(2),
    )(a, b)


def _my_place():
    return lax.axis_index("x"), lax.axis_index("y"), lax.axis_index("c")


def all_gather(x_shard, *, in_vmem, name):
    m_per, n = x_shard.shape
    space = pltpu.VMEM if in_vmem else pltpu.HBM

    def body(x_ref, out_ref, send_sems, recv_sems, local_sem):
        x, y, c = _my_place()
        me, sibling = (x, y, c), (x, y, 1 - c)
        chips = [(1 - x, y), (x, 1 - y), (1 - x, 1 - y)]

        def rows(px, py, pc):
            return out_ref.at[pl.ds((4 * px + 2 * py + pc) * m_per, m_per), :]

        def copy(k, block, to, src=None):
            return pltpu.make_async_remote_copy(
                src_ref=rows(*block) if src is None else src, dst_ref=rows(*block),
                send_sem=send_sems.at[k], recv_sem=recv_sems.at[k], device_id=to, device_id_type=MESH)

        mine = pltpu.make_async_copy(x_ref, rows(*me), local_sem)
        mine.start()
        first = [copy(0, me, sibling, src=x_ref)]
        first += [copy(1 + j, me, (*chip, c), src=x_ref) for j, chip in enumerate(chips)]
        for cp in first:
            cp.start()
        passed = [copy(4 + j, (*chip, c), sibling) for j, chip in enumerate(chips)]
        for j, chip in enumerate(chips):
            copy(1 + j, (*chip, c), me).wait_recv()
            passed[j].start()
        copy(0, sibling, me).wait_recv()
        for j, chip in enumerate(chips):
            copy(4 + j, (*chip, 1 - c), me).wait_recv()
        for cp in first + passed:
            cp.wait_send()
        mine.wait()

    return pl.pallas_call(
        body, name=name, out_shape=jax.ShapeDtypeStruct((N_DEV * m_per, n), x_shard.dtype),
        in_specs=[pl.BlockSpec(memory_space=space)], out_specs=pl.BlockSpec(memory_space=space),
        scratch_shapes=[pltpu.SemaphoreType.DMA((7,)), pltpu.SemaphoreType.DMA((7,)), pltpu.SemaphoreType.DMA],
        compiler_params=pltpu.CompilerParams(vmem_limit_bytes=VMEM_LIMIT),
    )(x_shard)


def sibling_exchange(send, *, name):
    def body(s_ref, o_ref, send_sems, recv_sems):
        x, y, c = _my_place()
        copies = [pltpu.make_async_remote_copy(
            src_ref=s_ref.at[k], dst_ref=o_ref.at[k], send_sem=send_sems.at[k], recv_sem=recv_sems.at[k],
            device_id=(x, y, 1 - c), device_id_type=MESH) for k in range(4)]
        for cp in copies:
            cp.start()
        for cp in copies:
            cp.wait()

    return pl.pallas_call(
        body, name=name, out_shape=jax.ShapeDtypeStruct(send.shape, send.dtype),
        in_specs=[pl.BlockSpec(memory_space=pltpu.HBM)], out_specs=pl.BlockSpec(memory_space=pltpu.HBM),
        scratch_shapes=[pltpu.SemaphoreType.DMA((4,)), pltpu.SemaphoreType.DMA((4,))],
    )(send)


def chip_exchange(parts, *, name):
    def body(p_ref, o_ref, send_sems, recv_sems, local_sem):
        x, y, c = _my_place()
        mine = 2 * x + y
        local = pltpu.make_async_copy(p_ref.at[mine], o_ref.at[mine], local_sem)
        local.start()
        chips = [(1 - x, y), (x, 1 - y), (1 - x, 1 - y)]
        copies = [pltpu.make_async_remote_copy(
            src_ref=p_ref.at[2 * cx + cy], dst_ref=o_ref.at[mine], send_sem=send_sems.at[k],
            recv_sem=recv_sems.at[k], device_id=(cx, cy, c), device_id_type=MESH)
            for k, (cx, cy) in enumerate(chips)]
        for cp in copies:
            cp.start()
        for k, (cx, cy) in enumerate(chips):
            pltpu.make_async_remote_copy(
                src_ref=p_ref.at[mine], dst_ref=o_ref.at[2 * cx + cy], send_sem=send_sems.at[k],
                recv_sem=recv_sems.at[k], device_id=(cx, cy, c), device_id_type=MESH).wait_recv()
        for cp in copies:
            cp.wait_send()
        local.wait()

    return pl.pallas_call(
        body, name=name, out_shape=jax.ShapeDtypeStruct(parts.shape, parts.dtype),
        in_specs=[pl.BlockSpec(memory_space=pltpu.HBM)], out_specs=pl.BlockSpec(memory_space=pltpu.HBM),
        scratch_shapes=[pltpu.SemaphoreType.DMA((3,)), pltpu.SemaphoreType.DMA((3,)), pltpu.SemaphoreType.DMA],
    )(parts)


BIG = ("gla_w_in", "gla_w_out", "att_w_in", "att_w_out", "ff_w1", "ff_w2")
COL_SHARDED = {"gla_w_in": True, "gla_w_out": False, "att_w_in": True, "att_w_out": False, "ff_w1": True,
               "ff_w2": False}


def _pack_rows(n_elems):
    rows = -(-n_elems // PACK_COLS)
    return -(-rows // PACK_BLOCK_ROWS) * PACK_BLOCK_ROWS


def _pack_local(shards):
    flat = jnp.concatenate([s.reshape(-1) for s in shards])
    rows = _pack_rows(flat.shape[0])
    return jnp.pad(flat, (0, rows * PACK_COLS - flat.shape[0])).reshape(rows, PACK_COLS)


def _unpack_local(packed, shapes):
    flat = packed.reshape(-1)
    out, off = [], 0
    for shp in shapes:
        n = int(np.prod(shp))
        out.append(flat[off:off + n].reshape(shp))
        off += n
    return out


def _unpack_gathered(gathered, names, shapes):
    out, off = {}, 0
    for name, shp in zip(names, shapes):
        n = int(np.prod(shp))
        part = gathered[:, off:off + n].reshape((N_DEV,) + tuple(shp))
        off += n
        n_l, k, nn = shp
        if COL_SHARDED[name]:
            out[name] = part.transpose(1, 2, 0, 3).reshape(n_l, k, N_DEV * nn)
        else:
            out[name] = part.transpose(1, 0, 2, 3).reshape(n_l, N_DEV * k, nn)
    return out


def _pack_grads(grads, names, shapes, rows):
    parts = []
    for name, shp in zip(names, shapes):
        g = grads[name]
        n_l, k, nn = shp
        if COL_SHARDED[name]:
            p = g.reshape(n_l, k, N_DEV, nn).transpose(2, 0, 1, 3)
        else:
            p = g.reshape(n_l, N_DEV, k, nn).transpose(1, 0, 2, 3)
        parts.append(p.reshape(N_DEV, -1))
    flat = jnp.concatenate(parts, axis=1)
    flat = jnp.pad(flat, ((0, 0), (0, rows * PACK_COLS - flat.shape[1])))
    return flat.reshape(N_DEV, rows, PACK_COLS)


def _pack_small(arrs):
    parts = []
    for a in arrs:
        f = a.reshape(-1)
        parts.append(jnp.pad(f, (0, -f.shape[0] % 128)))
    flat = jnp.concatenate(parts)
    flat = jnp.pad(flat, (0, -flat.shape[0] % 1024))
    return flat.reshape(-1, 128)


def _unpack_small(packed, shapes):
    flat = packed.reshape(packed.shape[:-2] + (-1,))
    out, off = [], 0
    for shp in shapes:
        n = int(np.prod(shp))
        out.append(flat[..., off:off + n].reshape(packed.shape[:-2] + tuple(shp)))
        off += n + (-n % 128)
    return out


def _vec(a):
    return a.reshape(1, -1)


def _trunk(x, target, mods, wts, sm):
    n_gla = 0
    n_att = 0
    rel_idx = _rel_index()
    saved = []
    for i in range(DEPTH):
        sh1, sc1, g1, sh2, sc2, g2 = [mods[i, k:k + 1] for k in range(6)]
        rec = {"x0": x}
        if i % 2 == 0:
            j = n_gla
            n_gla += 1
            w_in = jnp.pad(wts["gla_w_in"][j], ((0, 0), (0, GLA_INP - GLA_IN)))
            w2p = jnp.pad(sm["gla_w_gk2"][j], ((0, 128 - GLA_RANK), (0, 0)))
            bgk, gn = _vec(sm["gla_b_gk"][j]), _vec(sm["gla_g_norm"][j])
            proj = mm_nn(x, w_in, pro="mod", p1=sc1, p2=sh1, tm=1024, tn=640, name=f"gla_proj_{i}")
            og, states = gla_fwd(proj, w2p, bgk, gn, name=f"gla_core_{i}")
            y = mm_nn(og, wts["gla_w_out"][j], tm=1024, tn=512, name=f"gla_out_{i}")
            rec.update(kind="gla", j=j, w_in=w_in, w2p=w2p, bgk=bgk, gn=gn, proj=proj, og=og, states=states)
        else:
            j = n_att
            n_att += 1
            rel_pad = jnp.pad(sm["att_rel_bias"][j], ((0, 0), (0, REL_PAD - N_REL)), constant_values=NEG_INF)
            bias = rel_bias_tile(rel_pad, rel_idx, name=f"att_bias_{i}").reshape(ATT_H, ATT_QB, ATT_KB)
            qkv = mm_nn(x, wts["att_w_in"][j], pro="mod", p1=sc1, p2=sh1, bias=_vec(sm["att_b_in"][j]),
                        tm=1024, tn=512, name=f"att_proj_{i}")
            o = attn_fwd(qkv, bias, name=f"att_core_{i}")
            y = mm_nn(o, wts["att_w_out"][j], tm=1024, tn=512, name=f"att_out_{i}")
            rec.update(kind="att", j=j, bias=bias, qkv=qkv, o=o)
        x1 = ln_fwd(x, y, g1, _vec(sm["ln_g"][i, 0]), _vec(sm["ln_b"][i, 0]), name=f"ln_mix_{i}")
        h = mm_nn(x1, wts["ff_w1"][i], pro="mod", p1=sc2, p2=sh2, tm=1024, tn=512, name=f"ff_up_{i}")
        y2 = mm_nn(h, wts["ff_w2"][i], pro="relu2", tm=256, tn=512, name=f"ff_down_{i}")
        x2 = ln_fwd(x1, y2, g2, _vec(sm["ln_g"][i, 1]), _vec(sm["ln_b"][i, 1]), name=f"ln_ff_{i}")
        rec.update(y=y, x1=x1, h=h, y2=y2)
        saved.append(rec)
        x = x2

    dy, loss = loss_head(x, target, name="loss_head")

    gw = {"gla_w_in": [None] * 2, "gla_w_out": [None] * 2, "att_w_in": [None] * 2, "att_w_out": [None] * 2,
          "ff_w1": [None] * DEPTH, "ff_w2": [None] * DEPTH}
    gs = {"ln_g": [[None, None] for _ in range(DEPTH)], "ln_b": [[None, None] for _ in range(DEPTH)],
          "gla_w_gk2": [None] * 2, "gla_b_gk": [None] * 2, "gla_g_norm": [None] * 2, "att_b_in": [None] * 2,
          "att_rel_bias": [None] * 2}
    dmods = [[None] * 6 for _ in range(DEPTH)]
    nxt = None
    nxt_slot = None
    for i in reversed(range(DEPTH)):
        rec = saved[i]
        sh1, sc1, g1, sh2, sc2, g2 = [mods[i, k:k + 1] for k in range(6)]
        x0, x1 = rec["x0"], rec["x1"]
        if nxt is None:
            dz2, acc = ln_bwd(x1, rec["y2"], g2, _vec(sm["ln_g"][i, 1]), dout=dy, name=f"ln_ff_bwd_{i}")
        else:
            dz2, acc = ln_bwd(x1, rec["y2"], g2, _vec(sm["ln_g"][i, 1]), nxt=nxt, name=f"ln_ff_bwd_{i}")
            dmods[nxt_slot[0]][nxt_slot[1]] = acc[3]
            dmods[nxt_slot[0]][nxt_slot[2]] = acc[4]
        gs["ln_g"][i][1], gs["ln_b"][i][1], dmods[i][5] = acc[0], acc[1], acc[2]
        dh = mm_nt([dz2], wts["ff_w2"][i], pro="scale", p1=g2, epi_h=rec["h"], out_dtype=BF16, tm=1024, tn=512,
                   name=f"ff_down_bwd_{i}")
        gw["ff_w2"][i] = mm_tn(rec["h"], dz2, pro="relu2", dscale=g2, tk=1024, tn=1024, tm=512,
                               name=f"ff_w2_grad_{i}")
        du2 = mm_nt([dh], wts["ff_w1"][i], tm=512, tn=512, name=f"ff_up_bwd_{i}")
        gw["ff_w1"][i] = mm_tn(x1, dh, pro="mod", p1=sc2, p2=sh2, tk=1024, tn=1024, tm=512, name=f"ff_w1_grad_{i}")
        dz1, acc = ln_bwd(x0, rec["y"], g1, _vec(sm["ln_g"][i, 0]), nxt=(dz2, du2, sc2, x1), name=f"ln_mix_bwd_{i}")
        dmods[i][4], dmods[i][3] = acc[3], acc[4]
        gs["ln_g"][i][0], gs["ln_b"][i][0], dmods[i][2] = acc[0], acc[1], acc[2]
        j = rec["j"]
        if rec["kind"] == "gla":
            dog = mm_nt([dz1], wts["gla_w_out"][j], pro="scale", p1=g1, tm=1024, tn=512, name=f"gla_out_bwd_{i}")
            gw["gla_w_out"][j] = mm_tn(rec["og"], dz1, dscale=g1, tk=1024, tn=1024, tm=512, name=f"gla_wout_grad_{i}")
            dproj, dw2p, dbgk, dgn = gla_bwd(rec["proj"], dog, rec["states"], rec["w2p"], rec["bgk"], rec["gn"],
                                             name=f"gla_core_bwd_{i}")
            gs["gla_w_gk2"][j], gs["gla_b_gk"][j], gs["gla_g_norm"][j] = dw2p[:GLA_RANK], dbgk[0], dgn[0]
            du1 = mm_nt([dproj], rec["w_in"], tm=512, tn=512, name=f"gla_proj_bwd_{i}")
            gw["gla_w_in"][j] = mm_tn(x0, dproj, pro="mod", p1=sc1, p2=sh1, tk=1024, tn=640, tm=512,
                                      name=f"gla_win_grad_{i}")[:, :GLA_IN]
        else:
            do = mm_nt([dz1], wts["att_w_out"][j], pro="scale", p1=g1, tm=1024, tn=512, name=f"att_out_bwd_{i}")
            gw["att_w_out"][j] = mm_tn(rec["o"], dz1, dscale=g1, tk=1024, tn=1024, tm=512, name=f"att_wout_grad_{i}")
            dq, dk, dv, dbias = attn_bwd(rec["qkv"], do, rec["bias"], name=f"att_core_bwd_{i}")
            drel = rel_bias_grad(dbias.reshape(ATT_H, ATT_QB * ATT_KB), rel_idx, name=f"att_bias_grad_{i}")
            gs["att_rel_bias"][j] = drel[:, :N_REL]
            gs["att_b_in"][j] = jnp.concatenate(
                [colsum(t, name=f"att_bin_grad_{i}_{n}")[0] for n, t in enumerate((dq, dk, dv))])
            du1 = mm_nt([dq, dk, dv], wts["att_w_in"][j], tm=512, tn=512, name=f"att_proj_bwd_{i}")
            gw["att_w_in"][j] = jnp.concatenate(
                [mm_tn(x0, t, pro="mod", p1=sc1, p2=sh1, tk=1024, tn=1024, tm=512, name=f"att_win_grad_{i}_{n}")
                 for n, t in enumerate((dq, dk, dv))], axis=1)
        nxt = (dz1, du1, sc1, x0)
        nxt_slot = (i, 1, 0)
    dx, acc = combine_final(nxt[0], nxt[1], nxt[2], nxt[3], name="grad_x")
    dmods[0][1], dmods[0][0] = acc[3], acc[4]
    dmods = jnp.stack([jnp.stack(r) for r in dmods])
    gw = {k: jnp.stack(v) for k, v in gw.items()}
    gs = {k: jnp.stack([jnp.stack(r) if isinstance(r, list) else r for r in v]) for k, v in gs.items()}
    return loss, dx, dmods, gw, gs


WEIGHTS = ("w_ada", "b_ada", "ln_g", "ln_b", "gla_w_in", "gla_w_gk2", "gla_b_gk", "gla_g_norm", "gla_w_out",
           "att_w_in", "att_b_in", "att_rel_bias", "att_w_out", "ff_w1", "ff_w2")
SMALL_SHARDED = {"ln_g": 2, "ln_b": 2, "gla_w_gk2": 2, "gla_g_norm": 2, "att_b_in": 1}
SMALL_GRADS = ("ln_g", "ln_b", "gla_w_gk2", "gla_b_gk", "gla_g_norm", "att_b_in", "att_rel_bias")


def kernel(x, c, w_ada, b_ada, ln_g, ln_b, gla_w_in, gla_w_gk2, gla_b_gk, gla_g_norm, gla_w_out, att_w_in, att_b_in, att_rel_bias, att_w_out, ff_w1, ff_w2, loss_target, m_w_ada, m_b_ada, m_ln_g, m_ln_b, m_gla_w_in, m_gla_w_gk2, m_gla_b_gk, m_gla_g_norm, m_gla_w_out, m_att_w_in, m_att_b_in, m_att_rel_bias, m_att_w_out, m_ff_w1, m_ff_w2, v_w_ada, v_b_ada, v_ln_g, v_ln_b, v_gla_w_in, v_gla_w_gk2, v_gla_b_gk, v_gla_g_norm, v_gla_w_out, v_att_w_in, v_att_b_in, v_att_rel_bias, v_att_w_out, v_ff_w1, v_ff_w2):
    w = dict(w_ada=w_ada, b_ada=b_ada, ln_g=ln_g, ln_b=ln_b, gla_w_in=gla_w_in, gla_w_gk2=gla_w_gk2,
             gla_b_gk=gla_b_gk, gla_g_norm=gla_g_norm, gla_w_out=gla_w_out, att_w_in=att_w_in, att_b_in=att_b_in,
             att_rel_bias=att_rel_bias, att_w_out=att_w_out, ff_w1=ff_w1, ff_w2=ff_w2)
    m = dict(w_ada=m_w_ada, b_ada=m_b_ada, ln_g=m_ln_g, ln_b=m_ln_b, gla_w_in=m_gla_w_in, gla_w_gk2=m_gla_w_gk2,
             gla_b_gk=m_gla_b_gk, gla_g_norm=m_gla_g_norm, gla_w_out=m_gla_w_out, att_w_in=m_att_w_in,
             att_b_in=m_att_b_in, att_rel_bias=m_att_rel_bias, att_w_out=m_att_w_out, ff_w1=m_ff_w1, ff_w2=m_ff_w2)
    v = dict(w_ada=v_w_ada, b_ada=v_b_ada, ln_g=v_ln_g, ln_b=v_ln_b, gla_w_in=v_gla_w_in, gla_w_gk2=v_gla_w_gk2,
             gla_b_gk=v_gla_b_gk, gla_g_norm=v_gla_g_norm, gla_w_out=v_gla_w_out, att_w_in=v_att_w_in,
             att_b_in=v_att_b_in, att_rel_bias=v_att_rel_bias, att_w_out=v_att_w_out, ff_w1=v_ff_w1, ff_w2=v_ff_w2)
    xi, yi, ci = _my_place()
    me = 4 * xi + 2 * yi + ci

    small_names = tuple(SMALL_SHARDED)
    small_in = _pack_small([c] + [w[n] for n in small_names])
    small_all = all_gather(small_in, in_vmem=True, name="gather_small").reshape(N_DEV, -1, 128)
    parts = _unpack_small(small_all, [c.shape] + [w[n].shape for n in small_names])
    c_all = parts[0].reshape(N_DEV, D)
    sm = {"gla_b_gk": gla_b_gk, "att_rel_bias": att_rel_bias}
    for n, p in zip(small_names, parts[1:]):
        ax = SMALL_SHARDED[n]
        sm[n] = jnp.moveaxis(p, 0, ax).reshape(p.shape[1:ax + 1] + (N_DEV * p.shape[ax + 1],) + p.shape[ax + 2:])

    n_ada = w_ada.shape[2]
    mp = mods_partial(c_all, w_ada, name="mods_partial")
    mp_all = all_gather(mp.reshape(DEPTH * N_DEV, n_ada), in_vmem=True, name="gather_mods")
    mp_all = mp_all.reshape(N_DEV, DEPTH, N_DEV, n_ada)
    mods = lax.dynamic_index_in_dim(mp_all, me, axis=2, keepdims=False)
    mods = mods.transpose(1, 0, 2).reshape(DEPTH, 6 * D) + b_ada
    mods = mods.reshape(DEPTH, 6, D)

    big_shapes = [w[n].shape for n in BIG]
    packed_w = _pack_local([w[n].astype(BF16) for n in BIG])
    rows = packed_w.shape[0]
    gathered = all_gather(packed_w, in_vmem=False, name="gather_weights").reshape(N_DEV, rows * PACK_COLS)
    wts = _unpack_gathered(gathered, BIG, big_shapes)

    loss, dx, dmods, gw, gs = _trunk(x[0], loss_target[0], mods, wts, sm)
    loss = lax.psum(loss[0, 0], ("x", "y", "c"))

    gp = _pack_grads(gw, BIG, big_shapes, rows).reshape(4, 2, rows, PACK_COLS)
    keep = lax.dynamic_index_in_dim(gp, ci, axis=1, keepdims=False)
    give = lax.dynamic_index_in_dim(gp, 1 - ci, axis=1, keepdims=False)
    got = sibling_exchange(give, name="grads_sibling")
    chip_parts = chip_exchange(pair_sum_bf16(keep, got, name="grads_pair_sum"), name="grads_chips")
    pw, pm, pv = [_pack_local([d[n] for n in BIG]) for d in (w, m, v)]
    big_out = adamw(pw, pm, pv, chip_parts, block_rows=PACK_BLOCK_ROWS, name="adamw_packed")
    big_out = [dict(zip(BIG, _unpack_local(o, big_shapes))) for o in big_out]

    dm_flat = dmods.reshape(DEPTH, 6 * D)
    small_g = [dm_flat] + [gs[n].reshape(sm[n].shape) for n in SMALL_GRADS]
    small_shapes = [a.shape for a in small_g]
    sg_all = all_gather(_pack_small(small_g), in_vmem=True, name="gather_small_grads").reshape(N_DEV, -1, 128)
    summed = _unpack_small(sum_parts(sg_all, name="sum_small_grads"), small_shapes)
    g_full = dict(zip(("b_ada",) + SMALL_GRADS, summed))
    dm_all = _unpack_small(sg_all, small_shapes)[0]
    dm_mine = lax.dynamic_slice_in_dim(dm_all, me * n_ada, n_ada, axis=2).transpose(1, 0, 2)
    g_w_ada = w_ada_grad(c_all, dm_mine, name="w_ada_grad")

    results = {}
    for n in BIG:
        results[n] = [o[n] for o in big_out]
    results["w_ada"] = adamw_nd(w_ada, m_w_ada, v_w_ada, g_w_ada, name="adamw_w_ada")
    for n in ("b_ada",) + SMALL_GRADS:
        g = g_full[n]
        if n in SMALL_SHARDED:
            ax = SMALL_SHARDED[n]
            width = w[n].shape[ax]
            g = lax.dynamic_slice_in_dim(g, me * width, width, axis=ax)
        results[n] = adamw_nd(w[n], m[n], v[n], g, name=f"adamw_{n}")

    out = [loss, dx[None]]
    for k in range(4):
        out += [results[n][k] for n in WEIGHTS]
    return tuple(out)
```

```python
import numpy as np
import jax
import jax.numpy as jnp
from jax import lax
from jax.experimental import pallas as pl
from jax.experimental.pallas import tpu as pltpu

F32 = jnp.float32
BF16 = jnp.bfloat16
HIGHEST = lax.Precision.HIGHEST
MESH = pl.DeviceIdType.MESH

N_DEV = 8
D = 1024
DEPTH = 4
CHUNK = 64
ALPHA = (2.0 * DEPTH) ** 0.25
LN_EPS = 1e-5
RMS_EPS = 1e-6
NEG_INF = -1e30

GLA_H = 4
GLA_DKH = 128
GLA_DVH = 256
GLA_DK = GLA_H * GLA_DKH
GLA_DV = GLA_H * GLA_DVH
GLA_RANK = 16
GLA_IN = 2 * GLA_DK + 2 * GLA_DV + GLA_RANK
GLA_INP = 3200
GLA_TAU_INV = 1.0 / 16.0

ATT_H = 16
ATT_HD = 64
ATT_QB = 256
ATT_KB = 3 * ATT_QB
LEFT = 8 * CHUNK
MAX_REL = 128
N_REL = 2 * MAX_REL + 1
REL_PAD = 384
REL_TILE = 128
REL_TILES = (3, 4)
D_FF = 4 * D

ADAM_LR = 0.001
ADAM_B1 = 0.9
ADAM_B2 = 0.999
ADAM_EPS = 1e-08
ADAM_WD = 0.01
ADAM_STEP = 10

VMEM_LIMIT = 48 * 1024 * 1024


def _params(n_axes):
    return pltpu.CompilerParams(dimension_semantics=("arbitrary",) * n_axes, vmem_limit_bytes=VMEM_LIMIT)


def _dot(a, b):
    return jnp.dot(a, b, preferred_element_type=F32)


def _dot_nt(a, b):
    return lax.dot_general(a, b, (((1,), (1,)), ((), ())), preferred_element_type=F32)


def _dot_tn(a, b):
    return lax.dot_general(a, b, (((0,), (0,)), ((), ())), preferred_element_type=F32)


def _bf(a):
    return a.astype(BF16)


def _prologue(kind, a, p1=None, p2=None):
    if kind == "mod":
        return a * (1.0 + p1) + p2
    if kind == "scale":
        return a * (1.0 + p1)
    if kind == "relu2":
        r = jnp.maximum(a, 0.0)
        return r * r
    return a


def mm_nn(a, b, layer, *, pro=None, p1=None, p2=None, bias=None, tm, tn, name):
    M, K = a.shape
    N = b.shape[2]
    tm = min(tm, M)
    n_p = {"mod": 2, "scale": 1}.get(pro, 0)
    has_bias = bias is not None

    def body(*refs):
        a_ref, b_ref = refs[0], refs[1]
        p_refs = refs[2:2 + n_p]
        bias_ref = refs[2 + n_p] if has_bias else None
        o_ref, abf = refs[-2], refs[-1]

        @pl.when(pl.program_id(1) == 0)
        def _():
            abf[...] = _bf(_prologue(pro, a_ref[...].astype(F32), *[r[...] for r in p_refs]))

        acc = _dot(abf[...], b_ref[...])
        if has_bias:
            acc = acc + bias_ref[...]
        o_ref[...] = acc

    in_specs = [pl.BlockSpec((tm, K), lambda i, j: (i, 0)), pl.BlockSpec((None, K, tn), lambda i, j: (layer, 0, j))]
    args = [a, b]
    for p in (p1, p2)[:n_p]:
        in_specs.append(pl.BlockSpec((1, K), lambda i, j: (0, 0)))
        args.append(p)
    if has_bias:
        in_specs.append(pl.BlockSpec((1, tn), lambda i, j: (0, j)))
        args.append(bias)
    return pl.pallas_call(
        body, name=name, grid=(M // tm, N // tn), in_specs=in_specs,
        out_specs=pl.BlockSpec((tm, tn), lambda i, j: (i, j)),
        out_shape=jax.ShapeDtypeStruct((M, N), F32),
        scratch_shapes=[pltpu.VMEM((tm, K), BF16)], compiler_params=_params(2),
    )(*args)


def mm_nt(a_parts, w, layer, *, pro=None, p1=None, epi_h=None, out_dtype=F32, tm, tn, name):
    M = a_parts[0].shape[0]
    tm = min(tm, M)
    widths = [p.shape[1] for p in a_parts]
    Nw = sum(widths)
    Kw = w.shape[1]
    n_a = len(a_parts)
    has_p = pro == "scale"
    has_h = epi_h is not None

    def body(*refs):
        a_refs = refs[:n_a]
        w_ref = refs[n_a]
        k = n_a + 1
        p_ref = refs[k] if has_p else None
        k += int(has_p)
        h_ref = refs[k] if has_h else None
        o_ref, abf = refs[-2], refs[-1]

        @pl.when(pl.program_id(1) == 0)
        def _():
            off = 0
            for r, wd in zip(a_refs, widths):
                av = r[...]
                if has_p:
                    av = av.astype(F32) * (1.0 + p_ref[...])
                abf[:, off:off + wd] = _bf(av)
                off += wd

        acc = _dot_nt(abf[...], w_ref[...])
        if has_h:
            acc = acc * (2.0 * jnp.maximum(h_ref[...], 0.0))
        o_ref[...] = acc.astype(out_dtype)

    in_specs = [pl.BlockSpec((tm, wd), lambda i, j: (i, 0)) for wd in widths]
    in_specs.append(pl.BlockSpec((None, tn, Nw), lambda i, j: (layer, j, 0)))
    args = list(a_parts) + [w]
    if has_p:
        in_specs.append(pl.BlockSpec((1, Nw), lambda i, j: (0, 0)))
        args.append(p1)
    if has_h:
        in_specs.append(pl.BlockSpec((tm, tn), lambda i, j: (i, j)))
        args.append(epi_h)
    return pl.pallas_call(
        body, name=name, grid=(M // tm, Kw // tn), in_specs=in_specs,
        out_specs=pl.BlockSpec((tm, tn), lambda i, j: (i, j)),
        out_shape=jax.ShapeDtypeStruct((M, Kw), out_dtype),
        scratch_shapes=[pltpu.VMEM((tm, Nw), BF16)], compiler_params=_params(2),
    )(*args)


def mm_tn(a, d, *, pro=None, p1=None, p2=None, dscale=None, tk, tn, tm, out_buf, out_shape, layer, col_block0=0,
          name):
    M, Kf = a.shape
    N = d.shape[1]
    n_p = {"mod": 2}.get(pro, 0)
    has_ds = dscale is not None
    has_buf = out_buf is not None
    n_m = M // tm

    def body(*refs):
        a_ref, d_ref = refs[0], refs[1]
        p_refs = refs[2:2 + n_p]
        ds_ref = refs[2 + n_p] if has_ds else None
        o_ref, acc = refs[-2], refs[-1]
        m = pl.program_id(2)

        @pl.when(m == 0)
        def _():
            acc[...] = jnp.zeros_like(acc)

        av = _bf(_prologue(pro, a_ref[...].astype(F32), *[r[...] for r in p_refs]))
        dv = d_ref[...]
        if has_ds:
            dv = dv.astype(F32) * (1.0 + ds_ref[...])
        acc[...] += _dot_tn(av, _bf(dv))

        @pl.when(m == n_m - 1)
        def _():
            o_ref[...] = acc[...]

    in_specs = [pl.BlockSpec((tm, tk), lambda i, j, m: (m, i)), pl.BlockSpec((tm, tn), lambda i, j, m: (m, j))]
    args = [a, d]
    for p in (p1, p2)[:n_p]:
        in_specs.append(pl.BlockSpec((1, tk), lambda i, j, m: (0, i)))
        args.append(p)
    if has_ds:
        in_specs.append(pl.BlockSpec((1, tn), lambda i, j, m: (0, j)))
        args.append(dscale)
    aliases = {}
    if has_buf:
        in_specs.append(pl.BlockSpec(memory_space=pl.ANY))
        args.append(out_buf)
        aliases = {len(args) - 1: 0}
    return pl.pallas_call(
        body, name=name, grid=(Kf // tk, N // tn, n_m), in_specs=in_specs,
        out_specs=pl.BlockSpec((None, tk, tn), lambda i, j, m: (layer, i, col_block0 + j)),
        out_shape=jax.ShapeDtypeStruct(out_shape, F32), input_output_aliases=aliases,
        scratch_shapes=[pltpu.VMEM((tk, tn), F32)], compiler_params=_params(3),
    )(*args)


ROW_BLOCK = 512
ACC_ROWS = 8


def _ln_stats(z):
    mu = jnp.mean(z, axis=-1, keepdims=True)
    zc = z - mu
    var = jnp.mean(zc * zc, axis=-1, keepdims=True)
    return zc, lax.rsqrt(var + LN_EPS)


def ln_fwd(x, y, gate, lng, lnb, *, name):
    S = x.shape[0]

    def body(x_ref, y_ref, gt_ref, g_ref, b_ref, o_ref):
        z = ALPHA * x_ref[...] + (1.0 + gt_ref[...]) * y_ref[...]
        zc, rstd = _ln_stats(z)
        o_ref[...] = (zc * rstd) * g_ref[...] + b_ref[...]

    row = pl.BlockSpec((ROW_BLOCK, D), lambda i: (i, 0))
    vec = pl.BlockSpec((1, D), lambda i: (0, 0))
    return pl.pallas_call(
        body, name=name, grid=(S // ROW_BLOCK,), in_specs=[row, row, vec, vec, vec], out_specs=row,
        out_shape=jax.ShapeDtypeStruct((S, D), F32), compiler_params=_params(1),
    )(x, y, gate, lng, lnb)


def _add_colsum(acc_ref, r, val):
    acc_ref[r:r + 1, :] += jnp.sum(val, axis=0, keepdims=True)


def ln_bwd(x_in, y, gate, lng, *, dout=None, nxt=None, name):
    S = x_in.shape[0]
    has_next = nxt is not None

    def body(*refs):
        if has_next:
            dzn_ref, dun_ref, scn_ref, xo_ref = refs[:4]
            k = 4
        else:
            do_ref = refs[0]
            k = 1
        x_ref, y_ref, gt_ref, g_ref = refs[k:k + 4]
        dz_ref, acc_ref = refs[k + 4:]

        @pl.when(pl.program_id(0) == 0)
        def _():
            acc_ref[...] = jnp.zeros_like(acc_ref)

        if has_next:
            du = dun_ref[...]
            dout_v = ALPHA * dzn_ref[...] + du * (1.0 + scn_ref[...])
            _add_colsum(acc_ref, 3, du * xo_ref[...])
            _add_colsum(acc_ref, 4, du)
        else:
            dout_v = do_ref[...]
        yv = y_ref[...]
        z = ALPHA * x_ref[...] + (1.0 + gt_ref[...]) * yv
        zc, rstd = _ln_stats(z)
        xhat = zc * rstd
        _add_colsum(acc_ref, 0, dout_v * xhat)
        _add_colsum(acc_ref, 1, dout_v)
        dxh = dout_v * g_ref[...]
        m1 = jnp.mean(dxh, axis=-1, keepdims=True)
        m2 = jnp.mean(dxh * xhat, axis=-1, keepdims=True)
        dz = rstd * (dxh - m1 - xhat * m2)
        _add_colsum(acc_ref, 2, dz * yv)
        dz_ref[...] = dz

    row = pl.BlockSpec((ROW_BLOCK, D), lambda i: (i, 0))
    vec = pl.BlockSpec((1, D), lambda i: (0, 0))
    if has_next:
        in_specs = [row, row, vec, row]
        args = list(nxt)
    else:
        in_specs = [row]
        args = [dout]
    in_specs += [row, row, vec, vec]
    args += [x_in, y, gate, lng]
    return pl.pallas_call(
        body, name=name, grid=(S // ROW_BLOCK,), in_specs=in_specs,
        out_specs=[row, pl.BlockSpec((ACC_ROWS, D), lambda i: (0, 0))],
        out_shape=[jax.ShapeDtypeStruct((S, D), F32), jax.ShapeDtypeStruct((ACC_ROWS, D), F32)],
        compiler_params=_params(1),
    )(*args)


def combine_final(dz, du, sc, x_in, *, name):
    S = dz.shape[0]

    def body(dz_ref, du_ref, sc_ref, x_ref, dx_ref, acc_ref):
        @pl.when(pl.program_id(0) == 0)
        def _():
            acc_ref[...] = jnp.zeros_like(acc_ref)

        du_v = du_ref[...]
        dx_ref[...] = ALPHA * dz_ref[...] + du_v * (1.0 + sc_ref[...])
        _add_colsum(acc_ref, 3, du_v * x_ref[...])
        _add_colsum(acc_ref, 4, du_v)

    row = pl.BlockSpec((ROW_BLOCK, D), lambda i: (i, 0))
    vec = pl.BlockSpec((1, D), lambda i: (0, 0))
    return pl.pallas_call(
        body, name=name, grid=(S // ROW_BLOCK,), in_specs=[row, row, vec, row],
        out_specs=[row, pl.BlockSpec((ACC_ROWS, D), lambda i: (0, 0))],
        out_shape=[jax.ShapeDtypeStruct((S, D), F32), jax.ShapeDtypeStruct((ACC_ROWS, D), F32)],
        compiler_params=_params(1),
    )(dz, du, sc, x_in)


def colsum(a, *, name):
    S, N = a.shape

    def body(a_ref, o_ref):
        @pl.when(pl.program_id(0) == 0)
        def _():
            o_ref[...] = jnp.zeros_like(o_ref)

        o_ref[...] += jnp.sum(a_ref[...], axis=0, keepdims=True)

    return pl.pallas_call(
        body, name=name, grid=(S // ROW_BLOCK,), in_specs=[pl.BlockSpec((ROW_BLOCK, N), lambda i: (i, 0))],
        out_specs=pl.BlockSpec((1, N), lambda i: (0, 0)), out_shape=jax.ShapeDtypeStruct((1, N), F32),
        compiler_params=_params(1),
    )(a)


def loss_head(y, t, *, name):
    S = y.shape[0]

    def body(y_ref, t_ref, dy_ref, l_ref):
        @pl.when(pl.program_id(0) == 0)
        def _():
            l_ref[...] = jnp.zeros_like(l_ref)

        e = y_ref[...] - t_ref[...]
        dy_ref[...] = e * (1.0 / D)
        per_tok = jnp.sum(e * e, axis=1, keepdims=True) * (1.0 / D)
        l_ref[...] += 0.5 * jnp.sum(per_tok, axis=0, keepdims=True)

    row = pl.BlockSpec((ROW_BLOCK, D), lambda i: (i, 0))
    return pl.pallas_call(
        body, name=name, grid=(S // ROW_BLOCK,), in_specs=[row, row],
        out_specs=[row, pl.BlockSpec((8, 128), lambda i: (0, 0))],
        out_shape=[jax.ShapeDtypeStruct((S, D), F32), jax.ShapeDtypeStruct((8, 128), F32)],
        compiler_params=_params(1),
    )(y, t)


def _log_sigmoid(x):
    return jnp.minimum(x, 0.0) - jnp.log(1.0 + jnp.exp(-jnp.abs(x)))


def _silu(x):
    return x * (1.0 / (1.0 + jnp.exp(-x)))


def _gla_chunk(q, k, v, g, gk, s0t, w2p, bgk, gn):
    C = q.shape[0]
    row = lax.broadcasted_iota(jnp.int32, (C, C), 0)
    col = lax.broadcasted_iota(jnp.int32, (C, C), 1)
    lower = row >= col
    tri = lower.astype(F32)
    la = _log_sigmoid(_dot(_bf(gk), _bf(w2p)) + bgk) * GLA_TAU_INV
    outs, states = [], []
    for h in range(GLA_H):
        ks = slice(h * GLA_DKH, (h + 1) * GLA_DKH)
        vs = slice(h * GLA_DVH, (h + 1) * GLA_DVH)
        qh = q[:, ks] * (GLA_DKH ** -0.5)
        kh, vh, gh, lah, s0 = k[:, ks], v[:, vs], g[:, vs], la[:, ks], s0t[h]
        cum = jnp.dot(tri, lah, precision=HIGHEST, preferred_element_type=F32)
        e_pos = jnp.exp(cum)
        e_neg = jnp.exp(-cum)
        q_f = qh * e_pos
        a_f = _dot_nt(_bf(q_f), _bf(kh * e_neg))
        a_b = _dot_nt(_bf(qh * e_neg), _bf(kh * e_pos))
        att = jnp.where(lower, a_f, a_b)
        o = _dot(_bf(att), _bf(vh)) + _dot_nt(_bf(q_f), _bf(s0))
        tot = jnp.sum(lah, axis=0, keepdims=True)
        k_end = kh * jnp.exp(tot - cum)
        states.append(s0 * jnp.exp(tot) + _dot_tn(_bf(vh), _bf(k_end)))
        on = o * lax.rsqrt(jnp.mean(o * o, axis=-1, keepdims=True) + RMS_EPS) * gn[:, vs]
        outs.append(on * _silu(gh))
    return jnp.concatenate(outs, axis=1), tuple(states)


def _gla_split(p):
    return (p[:, 0:GLA_DK], p[:, GLA_DK:2 * GLA_DK], p[:, 2 * GLA_DK:2 * GLA_DK + GLA_DV],
            p[:, 2 * GLA_DK + GLA_DV:2 * GLA_DK + 2 * GLA_DV], p[:, 2 * GLA_DK + 2 * GLA_DV:GLA_INP])


def gla_fwd(proj, w2p, bgk, gn, *, name):
    S = proj.shape[0]
    n_c = S // CHUNK

    def body(p_ref, w_ref, b_ref, gn_ref, o_ref, st_ref, st):
        @pl.when(pl.program_id(0) == 0)
        def _():
            st[...] = jnp.zeros_like(st)

        s0 = tuple(st[h] for h in range(GLA_H))
        for h in range(GLA_H):
            st_ref[h] = s0[h]
        og, s1 = _gla_chunk(*_gla_split(p_ref[...]), s0, w_ref[...], b_ref[...], gn_ref[...])
        o_ref[...] = og
        for h in range(GLA_H):
            st[h] = s1[h]

    full = lambda shape: pl.BlockSpec(shape, lambda i: (0,) * len(shape))
    return pl.pallas_call(
        body, name=name, grid=(n_c,),
        in_specs=[pl.BlockSpec((CHUNK, GLA_INP), lambda i: (i, 0)), full((128, GLA_DK)), full((1, GLA_DK)),
                  full((1, GLA_DV))],
        out_specs=[pl.BlockSpec((CHUNK, GLA_DV), lambda i: (i, 0)),
                   pl.BlockSpec((None, GLA_H, GLA_DVH, GLA_DKH), lambda i: (i, 0, 0, 0))],
        out_shape=[jax.ShapeDtypeStruct((S, GLA_DV), F32),
                   jax.ShapeDtypeStruct((n_c, GLA_H, GLA_DVH, GLA_DKH), F32)],
        scratch_shapes=[pltpu.VMEM((GLA_H, GLA_DVH, GLA_DKH), F32)], compiler_params=_params(1),
    )(proj, w2p, bgk, gn)


def gla_bwd(proj, dog, states, w2p, bgk, gn, *, name):
    S = proj.shape[0]
    n_c = S // CHUNK

    def body(p_ref, dog_ref, st_ref, w_ref, b_ref, gn_ref, dp_ref, dw_ref, db_ref, dgn_ref, ds):
        @pl.when(pl.program_id(0) == 0)
        def _():
            ds[...] = jnp.zeros_like(ds)
            dw_ref[...] = jnp.zeros_like(dw_ref)
            db_ref[...] = jnp.zeros_like(db_ref)
            dgn_ref[...] = jnp.zeros_like(dgn_ref)

        q, k, v, g, gk = _gla_split(p_ref[...])
        s0 = tuple(st_ref[h] for h in range(GLA_H))
        _, vjp = jax.vjp(_gla_chunk, q, k, v, g, gk, s0, w_ref[...], b_ref[...], gn_ref[...])
        dq, dk, dv, dg, dgk, ds0, dw, db, dgn = vjp((dog_ref[...], tuple(ds[h] for h in range(GLA_H))))
        dp_ref[:, 0:GLA_DK] = dq
        dp_ref[:, GLA_DK:2 * GLA_DK] = dk
        dp_ref[:, 2 * GLA_DK:2 * GLA_DK + GLA_DV] = dv
        dp_ref[:, 2 * GLA_DK + GLA_DV:2 * GLA_DK + 2 * GLA_DV] = dg
        dp_ref[:, 2 * GLA_DK + 2 * GLA_DV:GLA_INP] = dgk
        for h in range(GLA_H):
            ds[h] = ds0[h]
        dw_ref[...] += dw
        db_ref[...] += db
        dgn_ref[...] += dgn

    full = lambda shape: pl.BlockSpec(shape, lambda i: (0,) * len(shape))
    rev = lambda i: (n_c - 1 - i, 0)
    return pl.pallas_call(
        body, name=name, grid=(n_c,),
        in_specs=[pl.BlockSpec((CHUNK, GLA_INP), rev), pl.BlockSpec((CHUNK, GLA_DV), rev),
                  pl.BlockSpec((None, GLA_H, GLA_DVH, GLA_DKH), lambda i: (n_c - 1 - i, 0, 0, 0)),
                  full((128, GLA_DK)), full((1, GLA_DK)), full((1, GLA_DV))],
        out_specs=[pl.BlockSpec((CHUNK, GLA_INP), rev), full((128, GLA_DK)), full((1, GLA_DK)), full((1, GLA_DV))],
        out_shape=[jax.ShapeDtypeStruct((S, GLA_INP), F32), jax.ShapeDtypeStruct((128, GLA_DK), F32),
                   jax.ShapeDtypeStruct((1, GLA_DK), F32), jax.ShapeDtypeStruct((1, GLA_DV), F32)],
        scratch_shapes=[pltpu.VMEM((GLA_H, GLA_DVH, GLA_DKH), F32)], compiler_params=_params(1),
    )(proj, dog, states, w2p, bgk, gn)


def _rel_index():
    t = np.arange(REL_TILE)[:, None]
    j = np.arange(REL_TILE)[None, :]
    tiles = []
    for m in REL_TILES:
        chunks = (REL_TILE // CHUNK) * m + j // CHUNK - t // CHUNK
        band = (chunks >= 0) & (chunks <= LEFT // CHUNK)
        dist = LEFT - REL_TILE * m + t - j
        tiles.append(np.where(band, np.minimum(dist, MAX_REL) + MAX_REL, N_REL))
    return jnp.asarray(np.stack(tiles).reshape(1, -1).astype(np.int32))


REL_BLOCK = 2048


def _one_hot(idx_row):
    return (lax.broadcasted_iota(jnp.int32, (REL_PAD, idx_row.shape[1]), 0) == idx_row).astype(F32)


def rel_bias_tiles(rel_pad, idx, *, name):
    E = idx.shape[1]

    def body(r_ref, i_ref, o_ref):
        o_ref[...] = jnp.dot(r_ref[...], _one_hot(i_ref[...]), precision=HIGHEST, preferred_element_type=F32)

    return pl.pallas_call(
        body, name=name, grid=(E // REL_BLOCK,),
        in_specs=[pl.BlockSpec((ATT_H, REL_PAD), lambda i: (0, 0)), pl.BlockSpec((1, REL_BLOCK), lambda i: (0, i))],
        out_specs=pl.BlockSpec((ATT_H, REL_BLOCK), lambda i: (0, i)),
        out_shape=jax.ShapeDtypeStruct((ATT_H, E), F32), compiler_params=_params(1),
    )(rel_pad, idx)


def rel_bias_grad(dtiles_flat, dclip, idx, *, name):
    E = idx.shape[1]
    n_steps = E // REL_BLOCK

    def body(d_ref, c_ref, i_ref, o_ref):
        @pl.when(pl.program_id(0) == 0)
        def _():
            o_ref[...] = jnp.zeros_like(o_ref)

        o_ref[...] += lax.dot_general(d_ref[...], _one_hot(i_ref[...]), (((1,), (1,)), ((), ())),
                                      precision=HIGHEST, preferred_element_type=F32)

        @pl.when(pl.program_id(0) == n_steps - 1)
        def _():
            at_clip = lax.broadcasted_iota(jnp.int32, (1, REL_PAD), 1) == 2 * MAX_REL
            o_ref[...] += jnp.where(at_clip, jnp.sum(c_ref[...], axis=1, keepdims=True), 0.0)

    return pl.pallas_call(
        body, name=name, grid=(n_steps,),
        in_specs=[pl.BlockSpec((ATT_H, REL_BLOCK), lambda i: (0, i)), pl.BlockSpec((ATT_H, 128), lambda i: (0, 0)),
                  pl.BlockSpec((1, REL_BLOCK), lambda i: (0, i))],
        out_specs=pl.BlockSpec((ATT_H, REL_PAD), lambda i: (0, 0)),
        out_shape=jax.ShapeDtypeStruct((ATT_H, REL_PAD), F32), compiler_params=_params(1),
    )(dtiles_flat, dclip, idx)


def _attn_bias(tiles, clip):
    const = jnp.broadcast_to(clip, (REL_TILE, REL_TILE))
    zero = jnp.zeros((REL_TILE, REL_TILE), F32)
    rows = []
    for qt in range(ATT_QB // REL_TILE):
        blocks = []
        for kt in range(ATT_KB // REL_TILE):
            m = kt - qt
            if m in REL_TILES:
                blocks.append(tiles[REL_TILES.index(m)])
            elif 0 <= m < REL_TILES[0]:
                blocks.append(const)
            else:
                blocks.append(zero)
        rows.append(jnp.concatenate(blocks, axis=1))
    return jnp.concatenate(rows, axis=0)


def _attn_pair(q, kw, vw, tiles2, clip2, mask):
    lane = lax.broadcasted_iota(jnp.int32, (1, 2 * ATT_HD), 1)
    kb, vb = _bf(kw), _bf(vw)
    out = jnp.zeros_like(q)
    for a in range(2):
        m = ((lane >= a * ATT_HD) & (lane < (a + 1) * ATT_HD)).astype(F32)
        s = _dot_nt(_bf(q * (m * ATT_HD ** -0.5)), kb) + _attn_bias(tiles2[a], clip2[a])
        s = jnp.where(mask, s, NEG_INF)
        e = jnp.exp(s - jnp.max(s, axis=-1, keepdims=True))
        p = e / jnp.sum(e, axis=-1, keepdims=True)
        out = out + _dot(_bf(p), vb) * m
    return out


def _attn_specs():
    n_hp = ATT_H // 2
    q_spec = pl.BlockSpec((ATT_QB, 128), lambda hp, g: (g, hp))

    def win(col0, back):
        return pl.BlockSpec((ATT_QB, 128), lambda hp, g: (jnp.maximum(g - back, 0), col0 + hp))

    kv_specs = [win(n_hp, 2), win(n_hp, 1), win(n_hp, 0), win(2 * n_hp, 2), win(2 * n_hp, 1), win(2 * n_hp, 0)]
    tiles_spec = pl.BlockSpec((2, len(REL_TILES), REL_TILE, REL_TILE), lambda hp, g: (hp, 0, 0, 0))
    clip_spec = pl.BlockSpec((2, 1, 128), lambda hp, g: (hp, 0, 0))
    return q_spec, kv_specs, tiles_spec, clip_spec


def _attn_window(refs, g):
    kw = jnp.concatenate([r[...] for r in refs[0:3]], axis=0)
    vw = jnp.concatenate([r[...] for r in refs[3:6]], axis=0)
    j = lax.broadcasted_iota(jnp.int32, (ATT_QB, ATT_KB), 1)
    t = lax.broadcasted_iota(jnp.int32, (ATT_QB, ATT_KB), 0)
    shift = CHUNK.bit_length() - 1
    chunks = lax.shift_right_logical(j, shift) - lax.shift_right_logical(t, shift)
    mask = (chunks >= 0) & (chunks <= LEFT // CHUNK) & ((j + (g - 2) * ATT_QB) >= 0)
    return kw, vw, mask


def attn_fwd(qkv, tiles, clip, *, name):
    S = qkv.shape[0]
    q_spec, kv_specs, tiles_spec, clip_spec = _attn_specs()

    def body(q_ref, *rest):
        kv_refs, t_ref, c_ref, o_ref = rest[:6], rest[6], rest[7], rest[8]
        kw, vw, mask = _attn_window(kv_refs, pl.program_id(1))
        o_ref[...] = _attn_pair(q_ref[...], kw, vw, t_ref[...], c_ref[...], mask)

    return pl.pallas_call(
        body, name=name, grid=(ATT_H // 2, S // ATT_QB), in_specs=[q_spec] + kv_specs + [tiles_spec, clip_spec],
        out_specs=q_spec, out_shape=jax.ShapeDtypeStruct((S, D), F32), compiler_params=_params(2),
    )(*([qkv] * 7), tiles, clip)


def attn_bwd(qkv, do, tiles, clip, *, name):
    S = qkv.shape[0]
    q_spec, kv_specs, tiles_spec, clip_spec = _attn_specs()
    col_spec = pl.BlockSpec((S, 128), lambda hp, g: (0, hp))

    def body(q_ref, *rest):
        kv_refs, t_ref, c_ref, do_ref = rest[:6], rest[6], rest[7], rest[8]
        dq_ref, dk_ref, dv_ref, dt_ref, dc_ref = rest[9:]
        g = pl.program_id(1)

        @pl.when(g == 0)
        def _():
            dk_ref[...] = jnp.zeros_like(dk_ref)
            dv_ref[...] = jnp.zeros_like(dv_ref)
            dt_ref[...] = jnp.zeros_like(dt_ref)
            dc_ref[...] = jnp.zeros_like(dc_ref)

        kw, vw, mask = _attn_window(kv_refs, g)
        _, vjp = jax.vjp(lambda q, k, v, t, c: _attn_pair(q, k, v, t, c, mask), q_ref[...], kw, vw, t_ref[...],
                         c_ref[...])
        dq, dkw, dvw, dt, dc = vjp(do_ref[...])
        dq_ref[...] = dq
        dt_ref[...] += dt
        dc_ref[...] += dc
        for blk in range(3):
            src = g - 2 + blk

            @pl.when(src >= 0)
            def _(blk=blk, src=src):
                rows = pl.ds(pl.multiple_of(src * ATT_QB, ATT_QB), ATT_QB)
                dk_ref[rows, :] += dkw[blk * ATT_QB:(blk + 1) * ATT_QB]
                dv_ref[rows, :] += dvw[blk * ATT_QB:(blk + 1) * ATT_QB]

    return pl.pallas_call(
        body, name=name, grid=(ATT_H // 2, S // ATT_QB),
        in_specs=[q_spec] + kv_specs + [tiles_spec, clip_spec, q_spec],
        out_specs=[q_spec, col_spec, col_spec, tiles_spec, clip_spec],
        out_shape=[jax.ShapeDtypeStruct((S, D), F32)] * 3
        + [jax.ShapeDtypeStruct((ATT_H, len(REL_TILES), REL_TILE, REL_TILE), F32),
           jax.ShapeDtypeStruct((ATT_H, 1, 128), F32)],
        compiler_params=_params(2),
    )(*([qkv] * 7), tiles, clip, do)


def mods_partial(c_all, w_ada, *, name):
    n_l, _, n_c = w_ada.shape

    def body(c_ref, w_ref, o_ref):
        o_ref[...] = _dot(_bf(_silu(c_ref[...])), _bf(w_ref[...]))

    return pl.pallas_call(
        body, name=name, grid=(n_l,),
        in_specs=[pl.BlockSpec((N_DEV, D), lambda l: (0, 0)), pl.BlockSpec((None, D, n_c), lambda l: (l, 0, 0))],
        out_specs=pl.BlockSpec((None, N_DEV, n_c), lambda l: (l, 0, 0)),
        out_shape=jax.ShapeDtypeStruct((n_l, N_DEV, n_c), F32), compiler_params=_params(1),
    )(c_all, w_ada)


def w_ada_grad(c_all, dm, *, name):
    n_l, _, n_c = dm.shape

    def body(c_ref, d_ref, o_ref):
        o_ref[...] = lax.dot_general(_silu(c_ref[...]), d_ref[...], (((0,), (0,)), ((), ())),
                                     precision=HIGHEST, preferred_element_type=F32)

    return pl.pallas_call(
        body, name=name, grid=(n_l,),
        in_specs=[pl.BlockSpec((N_DEV, D), lambda l: (0, 0)), pl.BlockSpec((None, N_DEV, n_c), lambda l: (l, 0, 0))],
        out_specs=pl.BlockSpec((None, D, n_c), lambda l: (l, 0, 0)),
        out_shape=jax.ShapeDtypeStruct((n_l, D, n_c), F32), compiler_params=_params(1),
    )(c_all, dm)


def adamw(w, m, v, gparts, *, block_rows, name):
    R, C = w.shape
    n = gparts.shape[0]

    def body(w_ref, m_ref, v_ref, g_ref, go_ref, d_ref, mo_ref, vo_ref):
        g = g_ref[0].astype(F32)
        for k in range(1, n):
            g = g + g_ref[k].astype(F32)
        m_new = ADAM_B1 * m_ref[...] + (1.0 - ADAM_B1) * g
        v_new = ADAM_B2 * v_ref[...] + (1.0 - ADAM_B2) * (g * g)
        m_hat = m_new / (1.0 - ADAM_B1 ** ADAM_STEP)
        v_hat = v_new / (1.0 - ADAM_B2 ** ADAM_STEP)
        go_ref[...] = g
        d_ref[...] = -ADAM_LR * (m_hat / (jnp.sqrt(v_hat) + ADAM_EPS) + ADAM_WD * w_ref[...])
        mo_ref[...] = m_new
        vo_ref[...] = v_new

    blk = pl.BlockSpec((block_rows, C), lambda i: (i, 0))
    return pl.pallas_call(
        body, name=name, grid=(R // block_rows,),
        in_specs=[blk, blk, blk, pl.BlockSpec((n, block_rows, C), lambda i: (0, i, 0))],
        out_specs=[blk] * 4, out_shape=[jax.ShapeDtypeStruct((R, C), F32)] * 4, compiler_params=_params(1),
    )(w, m, v, gparts)


def adamw_nd(w, m, v, gparts, *, name):
    shape = w.shape
    two = (int(np.prod(shape[:-1])), shape[-1])
    rows = two[0]
    block_rows = rows
    for cand in (512, 256):
        if rows > cand and rows % cand == 0:
            block_rows = cand
            break
    outs = adamw(w.reshape(two), m.reshape(two), v.reshape(two), gparts.reshape((gparts.shape[0],) + two),
                 block_rows=block_rows, name=name)
    return [o.reshape(shape) for o in outs]


def sum_parts(parts, *, name):
    n, R, C = parts.shape

    def body(p_ref, o_ref):
        acc = p_ref[0]
        for k in range(1, n):
            acc = acc + p_ref[k]
        o_ref[...] = acc

    return pl.pallas_call(
        body, name=name, in_specs=[pl.BlockSpec((n, R, C), lambda: (0, 0, 0))],
        out_specs=pl.BlockSpec((R, C), lambda: (0, 0)), out_shape=jax.ShapeDtypeStruct((R, C), F32),
        compiler_params=pltpu.CompilerParams(vmem_limit_bytes=VMEM_LIMIT),
    )(parts)


def _my_place():
    return lax.axis_index("x"), lax.axis_index("y"), lax.axis_index("c")


def _full_shape(kind, shard):
    n_l, rows, cols = shard
    return {"col": (n_l, rows, N_DEV * cols), "row": (n_l, N_DEV * rows, cols), "stk": (N_DEV, n_l, rows, cols)}[kind]


def _slab(ref, kind, dev, shard):
    _, rows, cols = shard
    if kind == "col":
        return ref.at[:, :, pl.ds(pl.multiple_of(dev * cols, 128), cols)]
    if kind == "row":
        return ref.at[:, pl.ds(pl.multiple_of(dev * rows, 8), rows), :]
    return ref.at[dev]


def _hbm_specs(n):
    return [pl.BlockSpec(memory_space=pltpu.HBM)] * n


def all_gather(x_shard, *, name):
    m_per, n = x_shard.shape

    def body(x_ref, out_ref, send_sems, recv_sems, local_sem):
        x, y, c = _my_place()
        me, sibling = (x, y, c), (x, y, 1 - c)
        chips = [(1 - x, y), (x, 1 - y), (1 - x, 1 - y)]

        def rows(px, py, pc):
            return out_ref.at[pl.ds((4 * px + 2 * py + pc) * m_per, m_per), :]

        def copy(k, block, to, src=None):
            return pltpu.make_async_remote_copy(
                src_ref=rows(*block) if src is None else src, dst_ref=rows(*block),
                send_sem=send_sems.at[k], recv_sem=recv_sems.at[k], device_id=to, device_id_type=MESH)

        mine = pltpu.make_async_copy(x_ref, rows(*me), local_sem)
        mine.start()
        first = [copy(0, me, sibling, src=x_ref)]
        first += [copy(1 + j, me, (*chip, c), src=x_ref) for j, chip in enumerate(chips)]
        for cp in first:
            cp.start()
        passed = [copy(4 + j, (*chip, c), sibling) for j, chip in enumerate(chips)]
        for j, chip in enumerate(chips):
            copy(1 + j, (*chip, c), me).wait_recv()
            passed[j].start()
        copy(0, sibling, me).wait_recv()
        for j, chip in enumerate(chips):
            copy(4 + j, (*chip, 1 - c), me).wait_recv()
        for cp in first + passed:
            cp.wait_send()
        mine.wait()

    return pl.pallas_call(
        body, name=name, out_shape=jax.ShapeDtypeStruct((N_DEV * m_per, n), x_shard.dtype),
        in_specs=[pl.BlockSpec(memory_space=pltpu.VMEM)], out_specs=pl.BlockSpec(memory_space=pltpu.VMEM),
        scratch_shapes=[pltpu.SemaphoreType.DMA((7,)), pltpu.SemaphoreType.DMA((7,)), pltpu.SemaphoreType.DMA],
        compiler_params=pltpu.CompilerParams(vmem_limit_bytes=VMEM_LIMIT),
    )(x_shard)


def gather_weights(shards, kinds, *, name):
    n_t = len(shards)
    shapes = [s.shape for s in shards]

    def body(*refs):
        x_refs, out_refs = refs[:n_t], refs[n_t:2 * n_t]
        send_sems, recv_sems, local_sems = refs[2 * n_t:]
        x, y, c = _my_place()
        me, sibling = (x, y, c), (x, y, 1 - c)
        chips = [(1 - x, y), (x, 1 - y), (1 - x, 1 - y)]

        def slab(t, px, py, pc):
            return _slab(out_refs[t], kinds[t], 4 * px + 2 * py + pc, shapes[t])

        def copy(t, k, block, to, src=None):
            return pltpu.make_async_remote_copy(
                src_ref=slab(t, *block) if src is None else src, dst_ref=slab(t, *block),
                send_sem=send_sems.at[7 * t + k], recv_sem=recv_sems.at[7 * t + k], device_id=to,
                device_id_type=MESH)

        mine = [pltpu.make_async_copy(x_refs[t], slab(t, *me), local_sems.at[t]) for t in range(n_t)]
        first = []
        for t in range(n_t):
            first.append(copy(t, 0, me, sibling, src=x_refs[t]))
            first += [copy(t, 1 + j, me, (*chip, c), src=x_refs[t]) for j, chip in enumerate(chips)]
        for cp in mine + first:
            cp.start()
        passed = []
        for j, chip in enumerate(chips):
            for t in range(n_t):
                copy(t, 1 + j, (*chip, c), me).wait_recv()
                passed.append(copy(t, 4 + j, (*chip, c), sibling))
                passed[-1].start()
        for t in range(n_t):
            copy(t, 0, sibling, me).wait_recv()
        for j, chip in enumerate(chips):
            for t in range(n_t):
                copy(t, 4 + j, (*chip, 1 - c), me).wait_recv()
        for cp in first + passed:
            cp.wait_send()
        for cp in mine:
            cp.wait()

    return pl.pallas_call(
        body, name=name,
        out_shape=[jax.ShapeDtypeStruct(_full_shape(k, s.shape), s.dtype) for k, s in zip(kinds, shards)],
        in_specs=_hbm_specs(n_t), out_specs=_hbm_specs(n_t),
        scratch_shapes=[pltpu.SemaphoreType.DMA((7 * n_t,)), pltpu.SemaphoreType.DMA((7 * n_t,)),
                        pltpu.SemaphoreType.DMA((n_t,))],
    )(*shards)


def sibling_exchange(grads, kinds, shapes, *, name):
    n_t = len(grads)

    def body(*refs):
        g_refs, out_refs = refs[:n_t], refs[n_t:2 * n_t]
        send_sems, recv_sems = refs[2 * n_t:]
        x, y, c = _my_place()
        copies = [pltpu.make_async_remote_copy(
            src_ref=_slab(g_refs[t], kinds[t], 2 * chip + (1 - c), shapes[t]), dst_ref=out_refs[t].at[chip],
            send_sem=send_sems.at[4 * t + chip], recv_sem=recv_sems.at[4 * t + chip], device_id=(x, y, 1 - c),
            device_id_type=MESH) for t in range(n_t) for chip in range(4)]
        for cp in copies:
            cp.start()
        for cp in copies:
            cp.wait()

    return pl.pallas_call(
        body, name=name, out_shape=[jax.ShapeDtypeStruct((4,) + tuple(s), F32) for s in shapes],
        in_specs=_hbm_specs(n_t), out_specs=_hbm_specs(n_t),
        scratch_shapes=[pltpu.SemaphoreType.DMA((4 * n_t,)), pltpu.SemaphoreType.DMA((4 * n_t,))],
    )(*grads)


def pair_sum_bf16(grad, got, kind, core, *, name):
    _, n_l, rows, cols = got.shape
    tr = min(rows, 256)
    n_b = rows // tr

    def body(c_ref, g_ref, got_ref, o_ref):
        o_ref[...] = _bf(g_ref[...] + got_ref[...])

    if kind == "col":
        g_spec = pl.BlockSpec((None, tr, cols), lambda ch, l, i, c: (l, i, 2 * ch + c[0]))
    elif kind == "row":
        g_spec = pl.BlockSpec((None, tr, cols), lambda ch, l, i, c: (l, (2 * ch + c[0]) * n_b + i, 0))
    else:
        g_spec = pl.BlockSpec((None, None, tr, cols), lambda ch, l, i, c: (2 * ch + c[0], l, i, 0))
    part_spec = pl.BlockSpec((None, None, tr, cols), lambda ch, l, i, c: (ch, l, i, 0))
    return pl.pallas_call(
        body, name=name,
        grid_spec=pltpu.PrefetchScalarGridSpec(num_scalar_prefetch=1, grid=(4, n_l, n_b),
                                               in_specs=[g_spec, part_spec], out_specs=part_spec),
        out_shape=jax.ShapeDtypeStruct(got.shape, BF16), compiler_params=_params(3),
    )(core, grad, got)


def chip_exchange(parts, *, name):
    n_t = len(parts)

    def body(*refs):
        p_refs, out_refs = refs[:n_t], refs[n_t:2 * n_t]
        send_sems, recv_sems, local_sems = refs[2 * n_t:]
        x, y, c = _my_place()
        mine = 2 * x + y
        chips = [(1 - x, y), (x, 1 - y), (1 - x, 1 - y)]
        local = [pltpu.make_async_copy(p_refs[t].at[mine], out_refs[t].at[mine], local_sems.at[t])
                 for t in range(n_t)]
        sends = [pltpu.make_async_remote_copy(
            src_ref=p_refs[t].at[2 * cx + cy], dst_ref=out_refs[t].at[mine], send_sem=send_sems.at[3 * t + k],
            recv_sem=recv_sems.at[3 * t + k], device_id=(cx, cy, c), device_id_type=MESH)
            for t in range(n_t) for k, (cx, cy) in enumerate(chips)]
        for cp in local + sends:
            cp.start()
        for t in range(n_t):
            for k, (cx, cy) in enumerate(chips):
                pltpu.make_async_remote_copy(
                    src_ref=p_refs[t].at[mine], dst_ref=out_refs[t].at[2 * cx + cy], send_sem=send_sems.at[3 * t + k],
                    recv_sem=recv_sems.at[3 * t + k], device_id=(cx, cy, c), device_id_type=MESH).wait_recv()
        for cp in sends:
            cp.wait_send()
        for cp in local:
            cp.wait()

    return pl.pallas_call(
        body, name=name, out_shape=[jax.ShapeDtypeStruct(p.shape, p.dtype) for p in parts],
        in_specs=_hbm_specs(n_t), out_specs=_hbm_specs(n_t),
        scratch_shapes=[pltpu.SemaphoreType.DMA((3 * n_t,)), pltpu.SemaphoreType.DMA((3 * n_t,)),
                        pltpu.SemaphoreType.DMA((n_t,))],
    )(*parts)


BIG = ("gla_w_in", "gla_w_out", "att_w_in", "att_w_out", "ff_w1", "ff_w2")
KIND = {"gla_w_in": "stk", "gla_w_out": "row", "att_w_in": "col", "att_w_out": "row", "ff_w1": "col", "ff_w2": "row"}


def _pack_small(arrs):
    parts = []
    for a in arrs:
        f = a.reshape(-1)
        parts.append(jnp.pad(f, (0, -f.shape[0] % 128)))
    flat = jnp.concatenate(parts)
    flat = jnp.pad(flat, (0, -flat.shape[0] % 1024))
    return flat.reshape(-1, 128)


def _unpack_small(packed, shapes):
    flat = packed.reshape(packed.shape[:-2] + (-1,))
    out, off = [], 0
    for shp in shapes:
        n = int(np.prod(shp))
        out.append(flat[..., off:off + n].reshape(packed.shape[:-2] + tuple(shp)))
        off += n + (-n % 128)
    return out


def _vec(a):
    return a.reshape(1, -1)


def _trunk(x, target, mods, wts, sm):
    n_gla = 0
    n_att = 0
    rel_idx = _rel_index()
    saved = []
    for i in range(DEPTH):
        sh1, sc1, g1, sh2, sc2, g2 = [mods[i, k:k + 1] for k in range(6)]
        rec = {"x0": x}
        if i % 2 == 0:
            j = n_gla
            n_gla += 1
            w2p = jnp.pad(sm["gla_w_gk2"][j], ((0, 128 - GLA_RANK), (0, 0)))
            bgk, gn = _vec(sm["gla_b_gk"][j]), _vec(sm["gla_g_norm"][j])
            proj = mm_nn(x, wts["gla_w_in"], j, pro="mod", p1=sc1, p2=sh1, tm=1024, tn=640, name=f"gla_proj_{i}")
            og, states = gla_fwd(proj, w2p, bgk, gn, name=f"gla_core_{i}")
            y = mm_nn(og, wts["gla_w_out"], j, tm=1024, tn=512, name=f"gla_out_{i}")
            rec.update(kind="gla", j=j, w2p=w2p, bgk=bgk, gn=gn, proj=proj, og=og, states=states)
        else:
            j = n_att
            n_att += 1
            rel = sm["att_rel_bias"][j]
            rel_pad = jnp.pad(rel, ((0, 0), (0, REL_PAD - N_REL)), constant_values=NEG_INF)
            tiles = rel_bias_tiles(rel_pad, rel_idx, name=f"att_bias_{i}")
            tiles = tiles.reshape(ATT_H, len(REL_TILES), REL_TILE, REL_TILE)
            clip = jnp.broadcast_to(rel[:, 2 * MAX_REL][:, None, None], (ATT_H, 1, 128))
            qkv = mm_nn(x, wts["att_w_in"], j, pro="mod", p1=sc1, p2=sh1, bias=_vec(sm["att_b_in"][j]),
                        tm=1024, tn=512, name=f"att_proj_{i}")
            o = attn_fwd(qkv, tiles, clip, name=f"att_core_{i}")
            y = mm_nn(o, wts["att_w_out"], j, tm=1024, tn=512, name=f"att_out_{i}")
            rec.update(kind="att", j=j, tiles=tiles, clip=clip, qkv=qkv, o=o)
        x1 = ln_fwd(x, y, g1, _vec(sm["ln_g"][i, 0]), _vec(sm["ln_b"][i, 0]), name=f"ln_mix_{i}")
        h = mm_nn(x1, wts["ff_w1"], i, pro="mod", p1=sc2, p2=sh2, tm=1024, tn=512, name=f"ff_up_{i}")
        y2 = mm_nn(h, wts["ff_w2"], i, pro="relu2", tm=256, tn=512, name=f"ff_down_{i}")
        x2 = ln_fwd(x1, y2, g2, _vec(sm["ln_g"][i, 1]), _vec(sm["ln_b"][i, 1]), name=f"ln_ff_{i}")
        rec.update(y=y, x1=x1, h=h, y2=y2)
        saved.append(rec)
        x = x2

    dy, loss = loss_head(x, target, name="loss_head")

    gw = {n: None for n in BIG}
    gw_shape = {n: wts[n].shape for n in BIG}

    def wgrad(weight, layer, a, d, *, tn, col_block0=0, **kw):
        gw[weight] = mm_tn(a, d, tk=1024, tn=tn, tm=512, out_buf=gw[weight], out_shape=gw_shape[weight],
                           layer=layer, col_block0=col_block0, **kw)

    gs = {"ln_g": [[None, None] for _ in range(DEPTH)], "ln_b": [[None, None] for _ in range(DEPTH)],
          "gla_w_gk2": [None] * 2, "gla_b_gk": [None] * 2, "gla_g_norm": [None] * 2, "att_b_in": [None] * 2,
          "att_rel_bias": [None] * 2}
    dmods = [[None] * 6 for _ in range(DEPTH)]
    nxt = None
    nxt_slot = None
    for i in reversed(range(DEPTH)):
        rec = saved[i]
        sh1, sc1, g1, sh2, sc2, g2 = [mods[i, k:k + 1] for k in range(6)]
        x0, x1 = rec["x0"], rec["x1"]
        if nxt is None:
            dz2, acc = ln_bwd(x1, rec["y2"], g2, _vec(sm["ln_g"][i, 1]), dout=dy, name=f"ln_ff_bwd_{i}")
        else:
            dz2, acc = ln_bwd(x1, rec["y2"], g2, _vec(sm["ln_g"][i, 1]), nxt=nxt, name=f"ln_ff_bwd_{i}")
            dmods[nxt_slot[0]][nxt_slot[1]] = acc[3]
            dmods[nxt_slot[0]][nxt_slot[2]] = acc[4]
        gs["ln_g"][i][1], gs["ln_b"][i][1], dmods[i][5] = acc[0], acc[1], acc[2]
        dh = mm_nt([dz2], wts["ff_w2"], i, pro="scale", p1=g2, epi_h=rec["h"], out_dtype=BF16, tm=1024, tn=512,
                   name=f"ff_down_bwd_{i}")
        wgrad("ff_w2", i, rec["h"], dz2, pro="relu2", dscale=g2, tn=1024, name=f"ff_w2_grad_{i}")
        du2 = mm_nt([dh], wts["ff_w1"], i, tm=512, tn=512, name=f"ff_up_bwd_{i}")
        wgrad("ff_w1", i, x1, dh, pro="mod", p1=sc2, p2=sh2, tn=1024, name=f"ff_w1_grad_{i}")
        dz1, acc = ln_bwd(x0, rec["y"], g1, _vec(sm["ln_g"][i, 0]), nxt=(dz2, du2, sc2, x1), name=f"ln_mix_bwd_{i}")
        dmods[i][4], dmods[i][3] = acc[3], acc[4]
        gs["ln_g"][i][0], gs["ln_b"][i][0], dmods[i][2] = acc[0], acc[1], acc[2]
        j = rec["j"]
        if rec["kind"] == "gla":
            dog = mm_nt([dz1], wts["gla_w_out"], j, pro="scale", p1=g1, tm=1024, tn=512, name=f"gla_out_bwd_{i}")
            wgrad("gla_w_out", j, rec["og"], dz1, dscale=g1, tn=1024, name=f"gla_wout_grad_{i}")
            dproj, dw2p, dbgk, dgn = gla_bwd(rec["proj"], dog, rec["states"], rec["w2p"], rec["bgk"], rec["gn"],
                                             name=f"gla_core_bwd_{i}")
            gs["gla_w_gk2"][j], gs["gla_b_gk"][j], gs["gla_g_norm"][j] = dw2p[:GLA_RANK], dbgk[0], dgn[0]
            du1 = mm_nt([dproj], wts["gla_w_in"], j, tm=512, tn=512, name=f"gla_proj_bwd_{i}")
            wgrad("gla_w_in", j, x0, dproj, pro="mod", p1=sc1, p2=sh1, tn=640, name=f"gla_win_grad_{i}")
        else:
            do = mm_nt([dz1], wts["att_w_out"], j, pro="scale", p1=g1, tm=1024, tn=512, name=f"att_out_bwd_{i}")
            wgrad("att_w_out", j, rec["o"], dz1, dscale=g1, tn=1024, name=f"att_wout_grad_{i}")
            dq, dk, dv, dtiles, dclip = attn_bwd(rec["qkv"], do, rec["tiles"], rec["clip"], name=f"att_core_bwd_{i}")
            drel = rel_bias_grad(dtiles.reshape(ATT_H, -1), dclip.reshape(ATT_H, 128), rel_idx,
                                 name=f"att_bias_grad_{i}")
            gs["att_rel_bias"][j] = drel[:, :N_REL]
            gs["att_b_in"][j] = jnp.concatenate(
                [colsum(t, name=f"att_bin_grad_{i}_{n}")[0] for n, t in enumerate((dq, dk, dv))])
            du1 = mm_nt([dq, dk, dv], wts["att_w_in"], j, tm=512, tn=512, name=f"att_proj_bwd_{i}")
            for n, t in enumerate((dq, dk, dv)):
                wgrad("att_w_in", j, x0, t, pro="mod", p1=sc1, p2=sh1, tn=1024, col_block0=n,
                      name=f"att_win_grad_{i}_{n}")
        nxt = (dz1, du1, sc1, x0)
        nxt_slot = (i, 1, 0)
    dx, acc = combine_final(nxt[0], nxt[1], nxt[2], nxt[3], name="grad_x")
    dmods[0][1], dmods[0][0] = acc[3], acc[4]
    dmods = jnp.stack([jnp.stack(r) for r in dmods])
    gs = {k: jnp.stack([jnp.stack(r) if isinstance(r, list) else r for r in v]) for k, v in gs.items()}
    return loss, dx, dmods, gw, gs


WEIGHTS = ("w_ada", "b_ada", "ln_g", "ln_b", "gla_w_in", "gla_w_gk2", "gla_b_gk", "gla_g_norm", "gla_w_out",
           "att_w_in", "att_b_in", "att_rel_bias", "att_w_out", "ff_w1", "ff_w2")
SMALL_SHARDED = {"ln_g": 2, "ln_b": 2, "gla_w_gk2": 2, "gla_g_norm": 2, "att_b_in": 1}
SMALL_GRADS = ("ln_g", "ln_b", "gla_w_gk2", "gla_b_gk", "gla_g_norm", "att_b_in", "att_rel_bias")


def kernel(x, c, w_ada, b_ada, ln_g, ln_b, gla_w_in, gla_w_gk2, gla_b_gk, gla_g_norm, gla_w_out, att_w_in, att_b_in, att_rel_bias, att_w_out, ff_w1, ff_w2, loss_target, m_w_ada, m_b_ada, m_ln_g, m_ln_b, m_gla_w_in, m_gla_w_gk2, m_gla_b_gk, m_gla_g_norm, m_gla_w_out, m_att_w_in, m_att_b_in, m_att_rel_bias, m_att_w_out, m_ff_w1, m_ff_w2, v_w_ada, v_b_ada, v_ln_g, v_ln_b, v_gla_w_in, v_gla_w_gk2, v_gla_b_gk, v_gla_g_norm, v_gla_w_out, v_att_w_in, v_att_b_in, v_att_rel_bias, v_att_w_out, v_ff_w1, v_ff_w2):
    w = dict(w_ada=w_ada, b_ada=b_ada, ln_g=ln_g, ln_b=ln_b, gla_w_in=gla_w_in, gla_w_gk2=gla_w_gk2,
             gla_b_gk=gla_b_gk, gla_g_norm=gla_g_norm, gla_w_out=gla_w_out, att_w_in=att_w_in, att_b_in=att_b_in,
             att_rel_bias=att_rel_bias, att_w_out=att_w_out, ff_w1=ff_w1, ff_w2=ff_w2)
    m = dict(w_ada=m_w_ada, b_ada=m_b_ada, ln_g=m_ln_g, ln_b=m_ln_b, gla_w_in=m_gla_w_in, gla_w_gk2=m_gla_w_gk2,
             gla_b_gk=m_gla_b_gk, gla_g_norm=m_gla_g_norm, gla_w_out=m_gla_w_out, att_w_in=m_att_w_in,
             att_b_in=m_att_b_in, att_rel_bias=m_att_rel_bias, att_w_out=m_att_w_out, ff_w1=m_ff_w1, ff_w2=m_ff_w2)
    v = dict(w_ada=v_w_ada, b_ada=v_b_ada, ln_g=v_ln_g, ln_b=v_ln_b, gla_w_in=v_gla_w_in, gla_w_gk2=v_gla_w_gk2,
             gla_b_gk=v_gla_b_gk, gla_g_norm=v_gla_g_norm, gla_w_out=v_gla_w_out, att_w_in=v_att_w_in,
             att_b_in=v_att_b_in, att_rel_bias=v_att_rel_bias, att_w_out=v_att_w_out, ff_w1=v_ff_w1, ff_w2=v_ff_w2)
    xi, yi, ci = _my_place()
    me = 4 * xi + 2 * yi + ci

    small_names = tuple(SMALL_SHARDED)
    small_in = _pack_small([c] + [w[n] for n in small_names])
    small_all = all_gather(small_in, name="gather_small").reshape(N_DEV, -1, 128)
    parts = _unpack_small(small_all, [c.shape] + [w[n].shape for n in small_names])
    c_all = parts[0].reshape(N_DEV, D)
    sm = {"gla_b_gk": gla_b_gk, "att_rel_bias": att_rel_bias}
    for n, p in zip(small_names, parts[1:]):
        ax = SMALL_SHARDED[n]
        sm[n] = jnp.moveaxis(p, 0, ax).reshape(p.shape[1:ax + 1] + (N_DEV * p.shape[ax + 1],) + p.shape[ax + 2:])

    n_ada = w_ada.shape[2]
    mp = mods_partial(c_all, w_ada, name="mods_partial")
    mp_all = all_gather(mp.reshape(DEPTH * N_DEV, n_ada), name="gather_mods")
    mp_all = mp_all.reshape(N_DEV, DEPTH, N_DEV, n_ada)
    mods = lax.dynamic_index_in_dim(mp_all, me, axis=2, keepdims=False)
    mods = mods.transpose(1, 0, 2).reshape(DEPTH, 6 * D) + b_ada
    mods = mods.reshape(DEPTH, 6, D)

    kinds = [KIND[n] for n in BIG]
    shard_shapes = [w[n].shape for n in BIG]
    wts = dict(zip(BIG, gather_weights([w[n].astype(BF16) for n in BIG], kinds, name="gather_weights")))
    stacked = wts["gla_w_in"]
    w_in = stacked.transpose(1, 2, 0, 3).reshape(stacked.shape[1], D, GLA_IN)
    wts["gla_w_in"] = jnp.pad(w_in, ((0, 0), (0, 0), (0, GLA_INP - GLA_IN)))

    loss, dx, dmods, gw, gs = _trunk(x.reshape(x.shape[1:]), loss_target.reshape(x.shape[1:]), mods, wts, sm)
    loss = lax.psum(loss[0, 0], ("x", "y", "c"))

    n_l, _, n_in = w["gla_w_in"].shape
    gw["gla_w_in"] = gw["gla_w_in"][:, :, :GLA_IN].reshape(n_l, D, N_DEV, n_in).transpose(2, 0, 1, 3)
    grads = [gw[n] for n in BIG]
    got = sibling_exchange(grads, kinds, shard_shapes, name="grads_sibling")
    core = ci.astype(jnp.int32).reshape(1)
    pair = [pair_sum_bf16(g, r, k, core, name=f"grads_pair_sum_{n}") for n, g, r, k in zip(BIG, grads, got, kinds)]
    chip_parts = chip_exchange(pair, name="grads_chips")
    results = {}
    for n, p in zip(BIG, chip_parts):
        results[n] = adamw_nd(w[n], m[n], v[n], p, name=f"adamw_{n}")

    dm_flat = dmods.reshape(DEPTH, 6 * D)
    small_g = [dm_flat] + [gs[n].reshape(sm[n].shape) for n in SMALL_GRADS]
    small_shapes = [a.shape for a in small_g]
    sg_all = all_gather(_pack_small(small_g), name="gather_small_grads").reshape(N_DEV, -1, 128)
    summed = _unpack_small(sum_parts(sg_all, name="sum_small_grads"), small_shapes)
    g_full = dict(zip(("b_ada",) + SMALL_GRADS, summed))
    dm_all = _unpack_small(sg_all, small_shapes)[0]
    dm_mine = lax.dynamic_slice_in_dim(dm_all, me * n_ada, n_ada, axis=2).transpose(1, 0, 2)
    g_w_ada = w_ada_grad(c_all, dm_mine, name="w_ada_grad")

    results["w_ada"] = adamw_nd(w_ada, m_w_ada, v_w_ada, g_w_ada[None], name="adamw_w_ada")
    for n in ("b_ada",) + SMALL_GRADS:
        g = g_full[n]
        if n in SMALL_SHARDED:
            ax = SMALL_SHARDED[n]
            width = w[n].shape[ax]
            g = lax.dynamic_slice_in_dim(g, me * width, width, axis=ax)
        results[n] = adamw_nd(w[n], m[n], v[n], g[None], name=f"adamw_{n}")

    out = [loss, dx[None]]
    for k in range(4):
        out += [results[n][k] for n in WEIGHTS]
    return tuple(out)
```

```python
import numpy as np
import jax
import jax.numpy as jnp
from jax import lax
from jax.experimental import pallas as pl
from jax.experimental.pallas import tpu as pltpu

F32 = jnp.float32
BF16 = jnp.bfloat16
HIGHEST = lax.Precision.HIGHEST
MESH = pl.DeviceIdType.MESH

N_DEV = 8
D = 1024
DEPTH = 4
CHUNK = 64
ALPHA = (2.0 * DEPTH) ** 0.25
LN_EPS = 1e-5
RMS_EPS = 1e-6
NEG_INF = -1e30

GLA_H = 4
GLA_DKH = 128
GLA_DVH = 256
GLA_DK = GLA_H * GLA_DKH
GLA_DV = GLA_H * GLA_DVH
GLA_RANK = 16
GLA_IN = 2 * GLA_DK + 2 * GLA_DV + GLA_RANK
GLA_INP = 3200
GLA_TAU_INV = 1.0 / 16.0

ATT_H = 16
ATT_HD = 64
ATT_QB = 256
ATT_KB = 3 * ATT_QB
LEFT = 8 * CHUNK
MAX_REL = 128
N_REL = 2 * MAX_REL + 1
REL_PAD = 384
REL_TILE = 128
REL_TILES = (3, 4)
D_FF = 4 * D

ADAM_LR = 0.001
ADAM_B1 = 0.9
ADAM_B2 = 0.999
ADAM_EPS = 1e-08
ADAM_WD = 0.01
ADAM_STEP = 10

VMEM_LIMIT = 48 * 1024 * 1024


def _params(n_axes):
    return pltpu.CompilerParams(dimension_semantics=("arbitrary",) * n_axes, vmem_limit_bytes=VMEM_LIMIT)


def _dot(a, b):
    return jnp.dot(a, b, preferred_element_type=F32)


def _dot_nt(a, b):
    return lax.dot_general(a, b, (((1,), (1,)), ((), ())), preferred_element_type=F32)


def _dot_tn(a, b):
    return lax.dot_general(a, b, (((0,), (0,)), ((), ())), preferred_element_type=F32)


def _bf(a):
    return a.astype(BF16)


def _prologue(kind, a, p1=None, p2=None):
    if kind == "mod":
        return a * (1.0 + p1) + p2
    if kind == "scale":
        return a * (1.0 + p1)
    if kind == "relu2":
        r = jnp.maximum(a, 0.0)
        return r * r
    return a


def mm_nn(a, b, layer, *, pro=None, p1=None, p2=None, bias=None, out_dtype=F32, tm, tn, name):
    M, K = a.shape
    N = b.shape[2]
    tm = min(tm, M)
    n_p = {"mod": 2, "scale": 1}.get(pro, 0)
    has_bias = bias is not None
    direct = pro is None and a.dtype == BF16

    def body(*refs):
        a_ref, b_ref = refs[0], refs[1]
        p_refs = refs[2:2 + n_p]
        bias_ref = refs[2 + n_p] if has_bias else None
        if direct:
            o_ref = refs[-1]
            lhs = a_ref[...]
        else:
            o_ref, abf = refs[-2], refs[-1]

            @pl.when(pl.program_id(1) == 0)
            def _():
                abf[...] = _bf(_prologue(pro, a_ref[...].astype(F32), *[r[...] for r in p_refs]))

            lhs = abf[...]
        acc = _dot(lhs, b_ref[...])
        if has_bias:
            acc = acc + bias_ref[...]
        o_ref[...] = acc.astype(out_dtype)

    in_specs = [pl.BlockSpec((tm, K), lambda i, j: (i, 0)), pl.BlockSpec((None, K, tn), lambda i, j: (layer, 0, j))]
    args = [a, b]
    for p in (p1, p2)[:n_p]:
        in_specs.append(pl.BlockSpec((1, K), lambda i, j: (0, 0)))
        args.append(p)
    if has_bias:
        in_specs.append(pl.BlockSpec((1, tn), lambda i, j: (0, j)))
        args.append(bias)
    return pl.pallas_call(
        body, name=name, grid=(M // tm, N // tn), in_specs=in_specs,
        out_specs=pl.BlockSpec((tm, tn), lambda i, j: (i, j)),
        out_shape=jax.ShapeDtypeStruct((M, N), out_dtype),
        scratch_shapes=[] if direct else [pltpu.VMEM((tm, K), BF16)], compiler_params=_params(2),
    )(*args)


def mm_nn_ksplit(a, b, layer, *, pro=None, tm, tk, name):
    M, K = a.shape
    N = b.shape[2]
    tm = min(tm, M)

    def body(a_ref, b_ref, o_ref):
        part = _dot(_bf(_prologue(pro, a_ref[...].astype(F32))), b_ref[...])

        @pl.when(pl.program_id(1) == 0)
        def _():
            o_ref[...] = part

        @pl.when(pl.program_id(1) > 0)
        def _():
            o_ref[...] += part

    return pl.pallas_call(
        body, name=name, grid=(M // tm, K // tk),
        in_specs=[pl.BlockSpec((tm, tk), lambda i, k: (i, k)), pl.BlockSpec((None, tk, N), lambda i, k: (layer, k, 0))],
        out_specs=pl.BlockSpec((tm, N), lambda i, k: (i, 0)),
        out_shape=jax.ShapeDtypeStruct((M, N), F32), compiler_params=_params(2),
    )(a, b)


def mm_nt(a_parts, w, layer, *, pro=None, p1=None, epi_h=None, out_dtype=F32, tm, tn, name):
    M = a_parts[0].shape[0]
    tm = min(tm, M)
    widths = [p.shape[1] for p in a_parts]
    Nw = sum(widths)
    Kw = w.shape[1]
    n_a = len(a_parts)
    has_p = pro == "scale"
    has_h = epi_h is not None
    direct = n_a == 1 and not has_p and a_parts[0].dtype == BF16

    def body(*refs):
        a_refs = refs[:n_a]
        w_ref = refs[n_a]
        k = n_a + 1
        p_ref = refs[k] if has_p else None
        k += int(has_p)
        h_ref = refs[k] if has_h else None
        if direct:
            o_ref = refs[-1]
            lhs = a_refs[0][...]
        else:
            o_ref, abf = refs[-2], refs[-1]

            @pl.when(pl.program_id(1) == 0)
            def _():
                off = 0
                for r, wd in zip(a_refs, widths):
                    av = r[...]
                    if has_p:
                        av = av.astype(F32) * (1.0 + p_ref[...])
                    abf[:, off:off + wd] = _bf(av)
                    off += wd

            lhs = abf[...]
        acc = _dot_nt(lhs, w_ref[...])
        if has_h:
            acc = acc * (2.0 * jnp.maximum(h_ref[...], 0.0))
        o_ref[...] = acc.astype(out_dtype)

    in_specs = [pl.BlockSpec((tm, wd), lambda i, j: (i, 0)) for wd in widths]
    in_specs.append(pl.BlockSpec((None, tn, Nw), lambda i, j: (layer, j, 0)))
    args = list(a_parts) + [w]
    if has_p:
        in_specs.append(pl.BlockSpec((1, Nw), lambda i, j: (0, 0)))
        args.append(p1)
    if has_h:
        in_specs.append(pl.BlockSpec((tm, tn), lambda i, j: (i, j)))
        args.append(epi_h)
    return pl.pallas_call(
        body, name=name, grid=(M // tm, Kw // tn), in_specs=in_specs,
        out_specs=pl.BlockSpec((tm, tn), lambda i, j: (i, j)),
        out_shape=jax.ShapeDtypeStruct((M, Kw), out_dtype),
        scratch_shapes=[] if direct else [pltpu.VMEM((tm, Nw), BF16)], compiler_params=_params(2),
    )(*args)


def mm_tn(a, d, *, pro=None, p1=None, p2=None, dscale=None, tk, tn, tm, out_buf, out_shape, layer, col_block0=0,
          name):
    M, Kf = a.shape
    N = d.shape[1]
    n_p = {"mod": 2}.get(pro, 0)
    has_ds = dscale is not None
    has_buf = out_buf is not None
    n_m = M // tm

    def body(*refs):
        a_ref, d_ref = refs[0], refs[1]
        p_refs = refs[2:2 + n_p]
        ds_ref = refs[2 + n_p] if has_ds else None
        o_ref = refs[-1]
        m = pl.program_id(2)

        av = a_ref[...]
        if pro is not None:
            av = _prologue(pro, av.astype(F32), *[r[...] for r in p_refs])
        dv = d_ref[...]
        if has_ds:
            dv = dv.astype(F32) * (1.0 + ds_ref[...])
        part = _dot_tn(_bf(av), _bf(dv))

        @pl.when(m == 0)
        def _():
            o_ref[...] = part

        @pl.when(m > 0)
        def _():
            o_ref[...] += part

    in_specs = [pl.BlockSpec((tm, tk), lambda i, j, m: (m, i)), pl.BlockSpec((tm, tn), lambda i, j, m: (m, j))]
    args = [a, d]
    for p in (p1, p2)[:n_p]:
        in_specs.append(pl.BlockSpec((1, tk), lambda i, j, m: (0, i)))
        args.append(p)
    if has_ds:
        in_specs.append(pl.BlockSpec((1, tn), lambda i, j, m: (0, j)))
        args.append(dscale)
    aliases = {}
    if has_buf:
        in_specs.append(pl.BlockSpec(memory_space=pl.ANY))
        args.append(out_buf)
        aliases = {len(args) - 1: 0}
    return pl.pallas_call(
        body, name=name, grid=(Kf // tk, N // tn, n_m), in_specs=in_specs,
        out_specs=pl.BlockSpec((None, tk, tn), lambda i, j, m: (layer, i, col_block0 + j)),
        out_shape=jax.ShapeDtypeStruct(out_shape, F32), input_output_aliases=aliases,
        compiler_params=_params(3),
    )(*args)


ROW_BLOCK = 512
ACC_ROWS = 8


def _ln_stats(z):
    mu = jnp.mean(z, axis=-1, keepdims=True)
    zc = z - mu
    var = jnp.mean(zc * zc, axis=-1, keepdims=True)
    return zc, lax.rsqrt(var + LN_EPS)


def ln_fwd(x, y, gate, lng, lnb, *, name):
    S = x.shape[0]

    def body(x_ref, y_ref, gt_ref, g_ref, b_ref, o_ref):
        z = ALPHA * x_ref[...] + (1.0 + gt_ref[...]) * y_ref[...]
        zc, rstd = _ln_stats(z)
        o_ref[...] = (zc * rstd) * g_ref[...] + b_ref[...]

    row = pl.BlockSpec((ROW_BLOCK, D), lambda i: (i, 0))
    vec = pl.BlockSpec((1, D), lambda i: (0, 0))
    return pl.pallas_call(
        body, name=name, grid=(S // ROW_BLOCK,), in_specs=[row, row, vec, vec, vec], out_specs=row,
        out_shape=jax.ShapeDtypeStruct((S, D), F32), compiler_params=_params(1),
    )(x, y, gate, lng, lnb)


def _add_colsum(acc_ref, r, val):
    acc_ref[r:r + 1, :] += jnp.sum(val, axis=0, keepdims=True)


def ln_bwd(x_in, y, gate, lng, *, dout=None, nxt=None, name):
    S = x_in.shape[0]
    has_next = nxt is not None

    def body(*refs):
        if has_next:
            dzn_ref, dun_ref, scn_ref, xo_ref = refs[:4]
            k = 4
        else:
            do_ref = refs[0]
            k = 1
        x_ref, y_ref, gt_ref, g_ref = refs[k:k + 4]
        dz_ref, acc_ref = refs[k + 4:]

        @pl.when(pl.program_id(0) == 0)
        def _():
            acc_ref[...] = jnp.zeros_like(acc_ref)

        if has_next:
            du = dun_ref[...]
            dout_v = ALPHA * dzn_ref[...] + du * (1.0 + scn_ref[...])
            _add_colsum(acc_ref, 3, du * xo_ref[...])
            _add_colsum(acc_ref, 4, du)
        else:
            dout_v = do_ref[...]
        yv = y_ref[...]
        z = ALPHA * x_ref[...] + (1.0 + gt_ref[...]) * yv
        zc, rstd = _ln_stats(z)
        xhat = zc * rstd
        _add_colsum(acc_ref, 0, dout_v * xhat)
        _add_colsum(acc_ref, 1, dout_v)
        dxh = dout_v * g_ref[...]
        m1 = jnp.mean(dxh, axis=-1, keepdims=True)
        m2 = jnp.mean(dxh * xhat, axis=-1, keepdims=True)
        dz = rstd * (dxh - m1 - xhat * m2)
        _add_colsum(acc_ref, 2, dz * yv)
        dz_ref[...] = dz

    row = pl.BlockSpec((ROW_BLOCK, D), lambda i: (i, 0))
    vec = pl.BlockSpec((1, D), lambda i: (0, 0))
    if has_next:
        in_specs = [row, row, vec, row]
        args = list(nxt)
    else:
        in_specs = [row]
        args = [dout]
    in_specs += [row, row, vec, vec]
    args += [x_in, y, gate, lng]
    return pl.pallas_call(
        body, name=name, grid=(S // ROW_BLOCK,), in_specs=in_specs,
        out_specs=[row, pl.BlockSpec((ACC_ROWS, D), lambda i: (0, 0))],
        out_shape=[jax.ShapeDtypeStruct((S, D), F32), jax.ShapeDtypeStruct((ACC_ROWS, D), F32)],
        compiler_params=_params(1),
    )(*args)


def combine_final(dz, du, sc, x_in, *, name):
    S = dz.shape[0]

    def body(dz_ref, du_ref, sc_ref, x_ref, dx_ref, acc_ref):
        @pl.when(pl.program_id(0) == 0)
        def _():
            acc_ref[...] = jnp.zeros_like(acc_ref)

        du_v = du_ref[...]
        dx_ref[...] = ALPHA * dz_ref[...] + du_v * (1.0 + sc_ref[...])
        _add_colsum(acc_ref, 3, du_v * x_ref[...])
        _add_colsum(acc_ref, 4, du_v)

    row = pl.BlockSpec((ROW_BLOCK, D), lambda i: (i, 0))
    vec = pl.BlockSpec((1, D), lambda i: (0, 0))
    return pl.pallas_call(
        body, name=name, grid=(S // ROW_BLOCK,), in_specs=[row, row, vec, row],
        out_specs=[row, pl.BlockSpec((ACC_ROWS, D), lambda i: (0, 0))],
        out_shape=[jax.ShapeDtypeStruct((S, D), F32), jax.ShapeDtypeStruct((ACC_ROWS, D), F32)],
        compiler_params=_params(1),
    )(dz, du, sc, x_in)


def colsum(a, *, name):
    S, N = a.shape

    def body(a_ref, o_ref):
        @pl.when(pl.program_id(0) == 0)
        def _():
            o_ref[...] = jnp.zeros_like(o_ref)

        o_ref[...] += jnp.sum(a_ref[...].astype(F32), axis=0, keepdims=True)

    return pl.pallas_call(
        body, name=name, grid=(S // ROW_BLOCK,), in_specs=[pl.BlockSpec((ROW_BLOCK, N), lambda i: (i, 0))],
        out_specs=pl.BlockSpec((1, N), lambda i: (0, 0)), out_shape=jax.ShapeDtypeStruct((1, N), F32),
        compiler_params=_params(1),
    )(a)


def loss_head(y, t, *, name):
    S = y.shape[0]

    def body(y_ref, t_ref, dy_ref, l_ref):
        @pl.when(pl.program_id(0) == 0)
        def _():
            l_ref[...] = jnp.zeros_like(l_ref)

        e = y_ref[...] - t_ref[...]
        dy_ref[...] = e * (1.0 / D)
        per_tok = jnp.sum(e * e, axis=1, keepdims=True) * (1.0 / D)
        l_ref[...] += 0.5 * jnp.sum(per_tok, axis=0, keepdims=True)

    row = pl.BlockSpec((ROW_BLOCK, D), lambda i: (i, 0))
    return pl.pallas_call(
        body, name=name, grid=(S // ROW_BLOCK,), in_specs=[row, row],
        out_specs=[row, pl.BlockSpec((8, 128), lambda i: (0, 0))],
        out_shape=[jax.ShapeDtypeStruct((S, D), F32), jax.ShapeDtypeStruct((8, 128), F32)],
        compiler_params=_params(1),
    )(y, t)


def _log_sigmoid(x):
    return jnp.minimum(x, 0.0) - jnp.log(1.0 + jnp.exp(-jnp.abs(x)))


def _silu(x):
    return x * (1.0 / (1.0 + jnp.exp(-x)))


def _gla_chunk(q, k, v, g, gk, s0t, w2p, bgk, gn):
    C = q.shape[0]
    row = lax.broadcasted_iota(jnp.int32, (C, C), 0)
    col = lax.broadcasted_iota(jnp.int32, (C, C), 1)
    lower = row >= col
    tri = lower.astype(F32)
    la = _log_sigmoid(_dot(_bf(gk), _bf(w2p)) + bgk) * GLA_TAU_INV
    outs, states = [], []
    for h in range(GLA_H):
        ks = slice(h * GLA_DKH, (h + 1) * GLA_DKH)
        vs = slice(h * GLA_DVH, (h + 1) * GLA_DVH)
        qh = q[:, ks] * (GLA_DKH ** -0.5)
        kh, vh, gh, lah, s0 = k[:, ks], v[:, vs], g[:, vs], la[:, ks], s0t[h]
        cum = jnp.dot(tri, lah, precision=HIGHEST, preferred_element_type=F32)
        e_pos = jnp.exp(cum)
        e_neg = jnp.exp(-cum)
        q_f = qh * e_pos
        a_f = _dot_nt(_bf(q_f), _bf(kh * e_neg))
        a_b = _dot_nt(_bf(qh * e_neg), _bf(kh * e_pos))
        att = jnp.where(lower, a_f, a_b)
        o = _dot(_bf(att), _bf(vh)) + _dot_nt(_bf(q_f), _bf(s0))
        tot = jnp.sum(lah, axis=0, keepdims=True)
        k_end = kh * jnp.exp(tot - cum)
        states.append(s0 * jnp.exp(tot) + _dot_tn(_bf(vh), _bf(k_end)))
        on = o * lax.rsqrt(jnp.mean(o * o, axis=-1, keepdims=True) + RMS_EPS) * gn[:, vs]
        outs.append(on * _silu(gh))
    return jnp.concatenate(outs, axis=1), tuple(states)


def _gla_split(p):
    return (p[:, 0:GLA_DK], p[:, GLA_DK:2 * GLA_DK], p[:, 2 * GLA_DK:2 * GLA_DK + GLA_DV],
            p[:, 2 * GLA_DK + GLA_DV:2 * GLA_DK + 2 * GLA_DV], p[:, 2 * GLA_DK + 2 * GLA_DV:GLA_INP])


def gla_fwd(proj, w2p, bgk, gn, *, name):
    S = proj.shape[0]
    n_c = S // CHUNK

    def body(p_ref, w_ref, b_ref, gn_ref, o_ref, st_ref, st):
        @pl.when(pl.program_id(0) == 0)
        def _():
            st[...] = jnp.zeros_like(st)

        s0 = tuple(st[h] for h in range(GLA_H))
        for h in range(GLA_H):
            st_ref[h] = s0[h]
        og, s1 = _gla_chunk(*_gla_split(p_ref[...]), s0, w_ref[...], b_ref[...], gn_ref[...])
        o_ref[...] = _bf(og)
        for h in range(GLA_H):
            st[h] = s1[h]

    full = lambda shape: pl.BlockSpec(shape, lambda i: (0,) * len(shape))
    return pl.pallas_call(
        body, name=name, grid=(n_c,),
        in_specs=[pl.BlockSpec((CHUNK, GLA_INP), lambda i: (i, 0)), full((128, GLA_DK)), full((1, GLA_DK)),
                  full((1, GLA_DV))],
        out_specs=[pl.BlockSpec((CHUNK, GLA_DV), lambda i: (i, 0)),
                   pl.BlockSpec((None, GLA_H, GLA_DVH, GLA_DKH), lambda i: (i, 0, 0, 0))],
        out_shape=[jax.ShapeDtypeStruct((S, GLA_DV), BF16),
                   jax.ShapeDtypeStruct((n_c, GLA_H, GLA_DVH, GLA_DKH), F32)],
        scratch_shapes=[pltpu.VMEM((GLA_H, GLA_DVH, GLA_DKH), F32)], compiler_params=_params(1),
    )(proj, w2p, bgk, gn)


def gla_bwd(proj, dog, states, w2p, bgk, gn, *, name):
    S = proj.shape[0]
    n_c = S // CHUNK

    def body(p_ref, dog_ref, st_ref, w_ref, b_ref, gn_ref, dp_ref, dw_ref, db_ref, dgn_ref, ds):
        @pl.when(pl.program_id(0) == 0)
        def _():
            ds[...] = jnp.zeros_like(ds)
            dw_ref[...] = jnp.zeros_like(dw_ref)
            db_ref[...] = jnp.zeros_like(db_ref)
            dgn_ref[...] = jnp.zeros_like(dgn_ref)

        q, k, v, g, gk = _gla_split(p_ref[...])
        s0 = tuple(st_ref[h] for h in range(GLA_H))
        _, vjp = jax.vjp(_gla_chunk, q, k, v, g, gk, s0, w_ref[...], b_ref[...], gn_ref[...])
        dq, dk, dv, dg, dgk, ds0, dw, db, dgn = vjp((dog_ref[...], tuple(ds[h] for h in range(GLA_H))))
        dp_ref[:, 0:GLA_DK] = _bf(dq)
        dp_ref[:, GLA_DK:2 * GLA_DK] = _bf(dk)
        dp_ref[:, 2 * GLA_DK:2 * GLA_DK + GLA_DV] = _bf(dv)
        dp_ref[:, 2 * GLA_DK + GLA_DV:2 * GLA_DK + 2 * GLA_DV] = _bf(dg)
        dp_ref[:, 2 * GLA_DK + 2 * GLA_DV:GLA_INP] = _bf(dgk)
        for h in range(GLA_H):
            ds[h] = ds0[h]
        dw_ref[...] += dw
        db_ref[...] += db
        dgn_ref[...] += dgn

    full = lambda shape: pl.BlockSpec(shape, lambda i: (0,) * len(shape))
    rev = lambda i: (n_c - 1 - i, 0)
    return pl.pallas_call(
        body, name=name, grid=(n_c,),
        in_specs=[pl.BlockSpec((CHUNK, GLA_INP), rev), pl.BlockSpec((CHUNK, GLA_DV), rev),
                  pl.BlockSpec((None, GLA_H, GLA_DVH, GLA_DKH), lambda i: (n_c - 1 - i, 0, 0, 0)),
                  full((128, GLA_DK)), full((1, GLA_DK)), full((1, GLA_DV))],
        out_specs=[pl.BlockSpec((CHUNK, GLA_INP), rev), full((128, GLA_DK)), full((1, GLA_DK)), full((1, GLA_DV))],
        out_shape=[jax.ShapeDtypeStruct((S, GLA_INP), BF16), jax.ShapeDtypeStruct((128, GLA_DK), F32),
                   jax.ShapeDtypeStruct((1, GLA_DK), F32), jax.ShapeDtypeStruct((1, GLA_DV), F32)],
        scratch_shapes=[pltpu.VMEM((GLA_H, GLA_DVH, GLA_DKH), F32)], compiler_params=_params(1),
    )(proj, dog, states, w2p, bgk, gn)


def _rel_index():
    t = np.arange(REL_TILE)[:, None]
    j = np.arange(REL_TILE)[None, :]
    tiles = []
    for m in REL_TILES:
        chunks = (REL_TILE // CHUNK) * m + j // CHUNK - t // CHUNK
        band = (chunks >= 0) & (chunks <= LEFT // CHUNK)
        dist = LEFT - REL_TILE * m + t - j
        tiles.append(np.where(band, np.minimum(dist, MAX_REL) + MAX_REL, N_REL))
    return jnp.asarray(np.stack(tiles).reshape(1, -1).astype(np.int32))


REL_BLOCK = 2048


def _one_hot(idx_row):
    return (lax.broadcasted_iota(jnp.int32, (REL_PAD, idx_row.shape[1]), 0) == idx_row).astype(F32)


def rel_bias_tiles(rel_pad, idx, *, name):
    E = idx.shape[1]

    def body(r_ref, i_ref, o_ref):
        o_ref[...] = jnp.dot(r_ref[...], _one_hot(i_ref[...]), precision=HIGHEST, preferred_element_type=F32)

    return pl.pallas_call(
        body, name=name, grid=(E // REL_BLOCK,),
        in_specs=[pl.BlockSpec((ATT_H, REL_PAD), lambda i: (0, 0)), pl.BlockSpec((1, REL_BLOCK), lambda i: (0, i))],
        out_specs=pl.BlockSpec((ATT_H, REL_BLOCK), lambda i: (0, i)),
        out_shape=jax.ShapeDtypeStruct((ATT_H, E), F32), compiler_params=_params(1),
    )(rel_pad, idx)


def rel_bias_grad(dtiles_flat, dclip, idx, *, name):
    E = idx.shape[1]
    n_steps = E // REL_BLOCK

    def body(d_ref, c_ref, i_ref, o_ref):
        @pl.when(pl.program_id(0) == 0)
        def _():
            o_ref[...] = jnp.zeros_like(o_ref)

        o_ref[...] += lax.dot_general(d_ref[...], _one_hot(i_ref[...]), (((1,), (1,)), ((), ())),
                                      precision=HIGHEST, preferred_element_type=F32)

        @pl.when(pl.program_id(0) == n_steps - 1)
        def _():
            at_clip = lax.broadcasted_iota(jnp.int32, (1, REL_PAD), 1) == 2 * MAX_REL
            o_ref[...] += jnp.where(at_clip, jnp.sum(c_ref[...], axis=1, keepdims=True), 0.0)

    return pl.pallas_call(
        body, name=name, grid=(n_steps,),
        in_specs=[pl.BlockSpec((ATT_H, REL_BLOCK), lambda i: (0, i)), pl.BlockSpec((ATT_H, 128), lambda i: (0, 0)),
                  pl.BlockSpec((1, REL_BLOCK), lambda i: (0, i))],
        out_specs=pl.BlockSpec((ATT_H, REL_PAD), lambda i: (0, 0)),
        out_shape=jax.ShapeDtypeStruct((ATT_H, REL_PAD), F32), compiler_params=_params(1),
    )(dtiles_flat, dclip, idx)


def _attn_bias(tiles, clip):
    const = jnp.broadcast_to(clip, (REL_TILE, REL_TILE))
    zero = jnp.zeros((REL_TILE, REL_TILE), F32)
    rows = []
    for qt in range(ATT_QB // REL_TILE):
        blocks = []
        for kt in range(ATT_KB // REL_TILE):
            m = kt - qt
            if m in REL_TILES:
                blocks.append(tiles[REL_TILES.index(m)])
            elif 0 <= m < REL_TILES[0]:
                blocks.append(const)
            else:
                blocks.append(zero)
        rows.append(jnp.concatenate(blocks, axis=1))
    return jnp.concatenate(rows, axis=0)


def _attn_bias_grad(ds, dt_ref, dc_ref, a):
    tile = lambda qt, kt: ds[qt * REL_TILE:(qt + 1) * REL_TILE, kt * REL_TILE:(kt + 1) * REL_TILE]
    const = None
    sums = [None] * len(REL_TILES)
    for qt in range(ATT_QB // REL_TILE):
        for kt in range(ATT_KB // REL_TILE):
            m = kt - qt
            if m in REL_TILES:
                n = REL_TILES.index(m)
                sums[n] = tile(qt, kt) if sums[n] is None else sums[n] + tile(qt, kt)
            elif 0 <= m < REL_TILES[0]:
                const = tile(qt, kt) if const is None else const + tile(qt, kt)
    for n, v in enumerate(sums):
        dt_ref[a, n] += v
    dc_ref[a] += jnp.sum(const, axis=0, keepdims=True)


def _attn_head_lanes():
    lane = lax.broadcasted_iota(jnp.int32, (1, 2 * ATT_HD), 1)
    return [(lane >= a * ATT_HD) & (lane < (a + 1) * ATT_HD) for a in range(2)]


def _attn_band_bias(tiles, clip):
    j = lax.broadcasted_iota(jnp.int32, (ATT_QB, ATT_KB), 1)
    t = lax.broadcasted_iota(jnp.int32, (ATT_QB, ATT_KB), 0)
    shift = CHUNK.bit_length() - 1
    chunks = lax.shift_right_logical(j, shift) - lax.shift_right_logical(t, shift)
    band = (chunks >= 0) & (chunks <= LEFT // CHUNK)
    return jnp.where(band, _attn_bias(tiles, clip), NEG_INF)


def _attn_exp(qa, kb, bias, key_bias):
    s = _dot_nt(qa, kb) + bias + key_bias
    e = jnp.exp(s - jnp.max(s, axis=-1, keepdims=True))
    return e, jnp.sum(e, axis=-1, keepdims=True)


def _attn_specs():
    n_hp = ATT_H // 2
    q_spec = pl.BlockSpec((ATT_QB, 128), lambda hp, g: (g, hp))

    def win(col0, back):
        return pl.BlockSpec((ATT_QB, 128), lambda hp, g: (jnp.maximum(g - back, 0), col0 + hp))

    kv_specs = [win(n_hp, 2), win(n_hp, 1), win(n_hp, 0), win(2 * n_hp, 2), win(2 * n_hp, 1), win(2 * n_hp, 0)]
    tiles_spec = pl.BlockSpec((2, len(REL_TILES), REL_TILE, REL_TILE), lambda hp, g: (hp, 0, 0, 0))
    clip_spec = pl.BlockSpec((2, 1, 128), lambda hp, g: (hp, 0, 0))
    return q_spec, kv_specs, tiles_spec, clip_spec


def _attn_window(refs, g):
    kb = jnp.concatenate([_bf(r[...]) for r in refs[0:3]], axis=0)
    vb = jnp.concatenate([_bf(r[...]) for r in refs[3:6]], axis=0)
    j = lax.broadcasted_iota(jnp.int32, (1, ATT_KB), 1)
    return kb, vb, jnp.where(j + (g - 2) * ATT_QB >= 0, 0.0, NEG_INF)


def attn_fwd(qkv, tiles, clip, *, name):
    S = qkv.shape[0]
    q_spec, kv_specs, tiles_spec, clip_spec = _attn_specs()

    def body(q_ref, *rest):
        kv_refs, t_ref, c_ref, o_ref, bias = rest[:6], rest[6], rest[7], rest[8], rest[9]
        g = pl.program_id(1)

        @pl.when(g == 0)
        def _():
            for a in range(2):
                bias[a] = _attn_band_bias(t_ref[a], c_ref[a])

        kb, vb, key_bias = _attn_window(kv_refs, g)
        q = q_ref[...].astype(F32)
        out = jnp.zeros((ATT_QB, 2 * ATT_HD), F32)
        for a, lanes in enumerate(_attn_head_lanes()):
            mf = lanes.astype(F32)
            e, l = _attn_exp(_bf(q * (mf * ATT_HD ** -0.5)), kb, bias[a], key_bias)
            out = out + _dot(_bf(e), vb) * (mf * (1.0 / l))
        o_ref[...] = _bf(out)

    return pl.pallas_call(
        body, name=name, grid=(ATT_H // 2, S // ATT_QB), in_specs=[q_spec] + kv_specs + [tiles_spec, clip_spec],
        out_specs=q_spec, out_shape=jax.ShapeDtypeStruct((S, D), BF16),
        scratch_shapes=[pltpu.VMEM((2, ATT_QB, ATT_KB), F32)], compiler_params=_params(2),
    )(*([qkv] * 7), tiles, clip)


def attn_bwd(qkv, do, tiles, clip, *, name):
    S = qkv.shape[0]
    q_spec, kv_specs, tiles_spec, clip_spec = _attn_specs()
    col_spec = pl.BlockSpec((S, 128), lambda hp, g: (0, hp))

    def body(q_ref, *rest):
        kv_refs, t_ref, c_ref, do_ref = rest[:6], rest[6], rest[7], rest[8]
        dq_ref, dk_ref, dv_ref, dt_ref, dc_ref, bias = rest[9:]
        g = pl.program_id(1)

        @pl.when(g == 0)
        def _():
            for a in range(2):
                bias[a] = _attn_band_bias(t_ref[a], c_ref[a])
            dk_ref[...] = jnp.zeros_like(dk_ref)
            dv_ref[...] = jnp.zeros_like(dv_ref)
            dt_ref[...] = jnp.zeros_like(dt_ref)
            dc_ref[...] = jnp.zeros_like(dc_ref)

        kb, vb, key_bias = _attn_window(kv_refs, g)
        q = q_ref[...].astype(F32)
        do = do_ref[...]
        dq = jnp.zeros((ATT_QB, 2 * ATT_HD), F32)
        dkw = jnp.zeros((ATT_KB, 2 * ATT_HD), F32)
        dvw = jnp.zeros((ATT_KB, 2 * ATT_HD), F32)
        for a, lanes in enumerate(_attn_head_lanes()):
            mf = lanes.astype(F32) * ATT_HD ** -0.5
            qa = _bf(q * mf)
            e, l = _attn_exp(qa, kb, bias[a], key_bias)
            p = e * (1.0 / l)
            do_a = jnp.where(lanes, do, jnp.zeros_like(do))
            dp = _dot_nt(do_a, vb)
            ds = p * (dp - jnp.sum(p * dp, axis=-1, keepdims=True))
            ds_b = _bf(ds)
            dq = dq + _dot(ds_b, kb) * mf
            dkw = dkw + _dot_tn(ds_b, qa)
            dvw = dvw + _dot_tn(_bf(p), do_a)
            _attn_bias_grad(ds, dt_ref, dc_ref, a)
        dq_ref[...] = _bf(dq)
        for blk in range(3):
            src = g - 2 + blk

            @pl.when(src >= 0)
            def _(blk=blk, src=src):
                rows = pl.ds(pl.multiple_of(src * ATT_QB, ATT_QB), ATT_QB)
                dk_ref[rows, :] += dkw[blk * ATT_QB:(blk + 1) * ATT_QB]
                dv_ref[rows, :] += dvw[blk * ATT_QB:(blk + 1) * ATT_QB]

    return pl.pallas_call(
        body, name=name, grid=(ATT_H // 2, S // ATT_QB),
        in_specs=[q_spec] + kv_specs + [tiles_spec, clip_spec, q_spec],
        out_specs=[q_spec, col_spec, col_spec, tiles_spec, clip_spec],
        out_shape=[jax.ShapeDtypeStruct((S, D), BF16)] + [jax.ShapeDtypeStruct((S, D), F32)] * 2
        + [jax.ShapeDtypeStruct((ATT_H, len(REL_TILES), REL_TILE, REL_TILE), F32),
           jax.ShapeDtypeStruct((ATT_H, 1, 128), F32)],
        scratch_shapes=[pltpu.VMEM((2, ATT_QB, ATT_KB), F32)], compiler_params=_params(2),
    )(*([qkv] * 7), tiles, clip, do)


def mods_partial(c_all, w_ada, *, name):
    n_l, _, n_c = w_ada.shape

    def body(c_ref, w_ref, o_ref):
        o_ref[...] = _dot(_bf(_silu(c_ref[...])), _bf(w_ref[...]))

    return pl.pallas_call(
        body, name=name, grid=(n_l,),
        in_specs=[pl.BlockSpec((N_DEV, D), lambda l: (0, 0)), pl.BlockSpec((None, D, n_c), lambda l: (l, 0, 0))],
        out_specs=pl.BlockSpec((None, N_DEV, n_c), lambda l: (l, 0, 0)),
        out_shape=jax.ShapeDtypeStruct((n_l, N_DEV, n_c), F32), compiler_params=_params(1),
    )(c_all, w_ada)


def w_ada_grad(c_all, dm, *, name):
    n_l, _, n_c = dm.shape

    def body(c_ref, d_ref, o_ref):
        o_ref[...] = lax.dot_general(_silu(c_ref[...]), d_ref[...], (((0,), (0,)), ((), ())),
                                     precision=HIGHEST, preferred_element_type=F32)

    return pl.pallas_call(
        body, name=name, grid=(n_l,),
        in_specs=[pl.BlockSpec((N_DEV, D), lambda l: (0, 0)), pl.BlockSpec((None, N_DEV, n_c), lambda l: (l, 0, 0))],
        out_specs=pl.BlockSpec((None, D, n_c), lambda l: (l, 0, 0)),
        out_shape=jax.ShapeDtypeStruct((n_l, D, n_c), F32), compiler_params=_params(1),
    )(c_all, dm)


def adamw(w, m, v, gparts, *, block_rows, name):
    R, C = w.shape
    n = gparts.shape[0]

    def body(w_ref, m_ref, v_ref, g_ref, go_ref, d_ref, mo_ref, vo_ref):
        g = g_ref[0].astype(F32)
        for k in range(1, n):
            g = g + g_ref[k].astype(F32)
        m_new = ADAM_B1 * m_ref[...] + (1.0 - ADAM_B1) * g
        v_new = ADAM_B2 * v_ref[...] + (1.0 - ADAM_B2) * (g * g)
        m_hat = m_new / (1.0 - ADAM_B1 ** ADAM_STEP)
        v_hat = v_new / (1.0 - ADAM_B2 ** ADAM_STEP)
        go_ref[...] = g
        d_ref[...] = -ADAM_LR * (m_hat / (jnp.sqrt(v_hat) + ADAM_EPS) + ADAM_WD * w_ref[...])
        mo_ref[...] = m_new
        vo_ref[...] = v_new

    blk = pl.BlockSpec((block_rows, C), lambda i: (i, 0))
    return pl.pallas_call(
        body, name=name, grid=(R // block_rows,),
        in_specs=[blk, blk, blk, pl.BlockSpec((n, block_rows, C), lambda i: (0, i, 0))],
        out_specs=[blk] * 4, out_shape=[jax.ShapeDtypeStruct((R, C), F32)] * 4, compiler_params=_params(1),
    )(w, m, v, gparts)


def adamw_nd(w, m, v, gparts, *, name):
    shape = w.shape
    two = (int(np.prod(shape[:-1])), shape[-1])
    rows = two[0]
    block_rows = rows
    for cand in (512, 256):
        if rows > cand and rows % cand == 0:
            block_rows = cand
            break
    outs = adamw(w.reshape(two), m.reshape(two), v.reshape(two), gparts.reshape((gparts.shape[0],) + two),
                 block_rows=block_rows, name=name)
    return [o.reshape(shape) for o in outs]


def sum_parts(parts, *, name):
    n, R, C = parts.shape

    def body(p_ref, o_ref):
        acc = p_ref[0]
        for k in range(1, n):
            acc = acc + p_ref[k]
        o_ref[...] = acc

    return pl.pallas_call(
        body, name=name, in_specs=[pl.BlockSpec((n, R, C), lambda: (0, 0, 0))],
        out_specs=pl.BlockSpec((R, C), lambda: (0, 0)), out_shape=jax.ShapeDtypeStruct((R, C), F32),
        compiler_params=pltpu.CompilerParams(vmem_limit_bytes=VMEM_LIMIT),
    )(parts)


def _my_place():
    return lax.axis_index("x"), lax.axis_index("y"), lax.axis_index("c")


def _full_shape(kind, shard):
    n_l, rows, cols = shard
    return {"col": (n_l, rows, N_DEV * cols), "row": (n_l, N_DEV * rows, cols), "stk": (N_DEV, n_l, rows, cols)}[kind]


def _slab(ref, kind, dev, shard):
    _, rows, cols = shard
    if kind == "col":
        return ref.at[:, :, pl.ds(pl.multiple_of(dev * cols, 128), cols)]
    if kind == "row":
        return ref.at[:, pl.ds(pl.multiple_of(dev * rows, 8), rows), :]
    return ref.at[dev]


def _hbm_specs(n):
    return [pl.BlockSpec(memory_space=pltpu.HBM)] * n


def all_gather(x_shard, *, name):
    m_per, n = x_shard.shape

    def body(x_ref, out_ref, send_sems, recv_sems, local_sem):
        x, y, c = _my_place()
        me, sibling = (x, y, c), (x, y, 1 - c)
        chips = [(1 - x, y), (x, 1 - y), (1 - x, 1 - y)]

        def rows(px, py, pc):
            return out_ref.at[pl.ds((4 * px + 2 * py + pc) * m_per, m_per), :]

        def copy(k, block, to, src=None):
            return pltpu.make_async_remote_copy(
                src_ref=rows(*block) if src is None else src, dst_ref=rows(*block),
                send_sem=send_sems.at[k], recv_sem=recv_sems.at[k], device_id=to, device_id_type=MESH)

        mine = pltpu.make_async_copy(x_ref, rows(*me), local_sem)
        mine.start()
        first = [copy(0, me, sibling, src=x_ref)]
        first += [copy(1 + j, me, (*chip, c), src=x_ref) for j, chip in enumerate(chips)]
        for cp in first:
            cp.start()
        passed = [copy(4 + j, (*chip, c), sibling) for j, chip in enumerate(chips)]
        for j, chip in enumerate(chips):
            copy(1 + j, (*chip, c), me).wait_recv()
            passed[j].start()
        copy(0, sibling, me).wait_recv()
        for j, chip in enumerate(chips):
            copy(4 + j, (*chip, 1 - c), me).wait_recv()
        for cp in first + passed:
            cp.wait_send()
        mine.wait()

    return pl.pallas_call(
        body, name=name, out_shape=jax.ShapeDtypeStruct((N_DEV * m_per, n), x_shard.dtype),
        in_specs=[pl.BlockSpec(memory_space=pltpu.VMEM)], out_specs=pl.BlockSpec(memory_space=pltpu.VMEM),
        scratch_shapes=[pltpu.SemaphoreType.DMA((7,)), pltpu.SemaphoreType.DMA((7,)), pltpu.SemaphoreType.DMA],
        compiler_params=pltpu.CompilerParams(vmem_limit_bytes=VMEM_LIMIT),
    )(x_shard)


def gather_weights(shards, kinds, *, name):
    n_t = len(shards)
    shapes = [s.shape for s in shards]

    def body(*refs):
        x_refs, out_refs = refs[:n_t], refs[n_t:2 * n_t]
        send_sems, recv_sems, local_sems = refs[2 * n_t:]
        x, y, c = _my_place()
        me, sibling = (x, y, c), (x, y, 1 - c)
        chips = [(1 - x, y), (x, 1 - y), (1 - x, 1 - y)]

        def slab(t, px, py, pc):
            return _slab(out_refs[t], kinds[t], 4 * px + 2 * py + pc, shapes[t])

        def copy(t, k, block, to, src=None):
            return pltpu.make_async_remote_copy(
                src_ref=slab(t, *block) if src is None else src, dst_ref=slab(t, *block),
                send_sem=send_sems.at[7 * t + k], recv_sem=recv_sems.at[7 * t + k], device_id=to,
                device_id_type=MESH)

        mine = [pltpu.make_async_copy(x_refs[t], slab(t, *me), local_sems.at[t]) for t in range(n_t)]
        first = []
        for t in range(n_t):
            first.append(copy(t, 0, me, sibling, src=x_refs[t]))
            first += [copy(t, 1 + j, me, (*chip, c), src=x_refs[t]) for j, chip in enumerate(chips)]
        for cp in mine + first:
            cp.start()
        passed = []
        for j, chip in enumerate(chips):
            for t in range(n_t):
                copy(t, 1 + j, (*chip, c), me).wait_recv()
                passed.append(copy(t, 4 + j, (*chip, c), sibling))
                passed[-1].start()
        for t in range(n_t):
            copy(t, 0, sibling, me).wait_recv()
        for j, chip in enumerate(chips):
            for t in range(n_t):
                copy(t, 4 + j, (*chip, 1 - c), me).wait_recv()
        for cp in first + passed:
            cp.wait_send()
        for cp in mine:
            cp.wait()

    return pl.pallas_call(
        body, name=name,
        out_shape=[jax.ShapeDtypeStruct(_full_shape(k, s.shape), s.dtype) for k, s in zip(kinds, shards)],
        in_specs=_hbm_specs(n_t), out_specs=_hbm_specs(n_t),
        scratch_shapes=[pltpu.SemaphoreType.DMA((7 * n_t,)), pltpu.SemaphoreType.DMA((7 * n_t,)),
                        pltpu.SemaphoreType.DMA((n_t,))],
    )(*shards)


def sibling_exchange(grads, kinds, shapes, *, name):
    n_t = len(grads)

    def body(*refs):
        g_refs, out_refs = refs[:n_t], refs[n_t:2 * n_t]
        send_sems, recv_sems = refs[2 * n_t:]
        x, y, c = _my_place()
        copies = [pltpu.make_async_remote_copy(
            src_ref=_slab(g_refs[t], kinds[t], 2 * chip + (1 - c), shapes[t]), dst_ref=out_refs[t].at[chip],
            send_sem=send_sems.at[4 * t + chip], recv_sem=recv_sems.at[4 * t + chip], device_id=(x, y, 1 - c),
            device_id_type=MESH) for t in range(n_t) for chip in range(4)]
        for cp in copies:
            cp.start()
        for cp in copies:
            cp.wait()

    return pl.pallas_call(
        body, name=name, out_shape=[jax.ShapeDtypeStruct((4,) + tuple(s), F32) for s in shapes],
        in_specs=_hbm_specs(n_t), out_specs=_hbm_specs(n_t),
        scratch_shapes=[pltpu.SemaphoreType.DMA((4 * n_t,)), pltpu.SemaphoreType.DMA((4 * n_t,))],
    )(*grads)


def pair_sum_bf16(grad, got, kind, core, *, name):
    _, n_l, rows, cols = got.shape
    tr = min(rows, 256)
    n_b = rows // tr

    def body(c_ref, g_ref, got_ref, o_ref):
        o_ref[...] = _bf(g_ref[...] + got_ref[...])

    if kind == "col":
        g_spec = pl.BlockSpec((None, tr, cols), lambda ch, l, i, c: (l, i, 2 * ch + c[0]))
    elif kind == "row":
        g_spec = pl.BlockSpec((None, tr, cols), lambda ch, l, i, c: (l, (2 * ch + c[0]) * n_b + i, 0))
    else:
        g_spec = pl.BlockSpec((None, None, tr, cols), lambda ch, l, i, c: (2 * ch + c[0], l, i, 0))
    part_spec = pl.BlockSpec((None, None, tr, cols), lambda ch, l, i, c: (ch, l, i, 0))
    return pl.pallas_call(
        body, name=name,
        grid_spec=pltpu.PrefetchScalarGridSpec(num_scalar_prefetch=1, grid=(4, n_l, n_b),
                                               in_specs=[g_spec, part_spec], out_specs=part_spec),
        out_shape=jax.ShapeDtypeStruct(got.shape, BF16), compiler_params=_params(3),
    )(core, grad, got)


def chip_exchange(parts, *, name):
    n_t = len(parts)

    def body(*refs):
        p_refs, out_refs = refs[:n_t], refs[n_t:2 * n_t]
        send_sems, recv_sems, local_sems = refs[2 * n_t:]
        x, y, c = _my_place()
        mine = 2 * x + y
        chips = [(1 - x, y), (x, 1 - y), (1 - x, 1 - y)]
        local = [pltpu.make_async_copy(p_refs[t].at[mine], out_refs[t].at[mine], local_sems.at[t])
                 for t in range(n_t)]
        sends = [pltpu.make_async_remote_copy(
            src_ref=p_refs[t].at[2 * cx + cy], dst_ref=out_refs[t].at[mine], send_sem=send_sems.at[3 * t + k],
            recv_sem=recv_sems.at[3 * t + k], device_id=(cx, cy, c), device_id_type=MESH)
            for t in range(n_t) for k, (cx, cy) in enumerate(chips)]
        for cp in local + sends:
            cp.start()
        for t in range(n_t):
            for k, (cx, cy) in enumerate(chips):
                pltpu.make_async_remote_copy(
                    src_ref=p_refs[t].at[mine], dst_ref=out_refs[t].at[2 * cx + cy], send_sem=send_sems.at[3 * t + k],
                    recv_sem=recv_sems.at[3 * t + k], device_id=(cx, cy, c), device_id_type=MESH).wait_recv()
        for cp in sends:
            cp.wait_send()
        for cp in local:
            cp.wait()

    return pl.pallas_call(
        body, name=name, out_shape=[jax.ShapeDtypeStruct(p.shape, p.dtype) for p in parts],
        in_specs=_hbm_specs(n_t), out_specs=_hbm_specs(n_t),
        scratch_shapes=[pltpu.SemaphoreType.DMA((3 * n_t,)), pltpu.SemaphoreType.DMA((3 * n_t,)),
                        pltpu.SemaphoreType.DMA((n_t,))],
    )(*parts)


BIG = ("gla_w_in", "gla_w_out", "att_w_in", "att_w_out", "ff_w1", "ff_w2")
KIND = {"gla_w_in": "stk", "gla_w_out": "row", "att_w_in": "col", "att_w_out": "row", "ff_w1": "col", "ff_w2": "row"}


def _pack_small(arrs):
    parts = []
    for a in arrs:
        f = a.reshape(-1)
        parts.append(jnp.pad(f, (0, -f.shape[0] % 128)))
    flat = jnp.concatenate(parts)
    flat = jnp.pad(flat, (0, -flat.shape[0] % 1024))
    return flat.reshape(-1, 128)


def _unpack_small(packed, shapes):
    flat = packed.reshape(packed.shape[:-2] + (-1,))
    out, off = [], 0
    for shp in shapes:
        n = int(np.prod(shp))
        out.append(flat[..., off:off + n].reshape(packed.shape[:-2] + tuple(shp)))
        off += n + (-n % 128)
    return out


def _vec(a):
    return a.reshape(1, -1)


def _trunk(x, target, mods, wts, sm):
    n_gla = 0
    n_att = 0
    rel_idx = _rel_index()
    saved = []
    for i in range(DEPTH):
        sh1, sc1, g1, sh2, sc2, g2 = [mods[i, k:k + 1] for k in range(6)]
        rec = {"x0": x}
        if i % 2 == 0:
            j = n_gla
            n_gla += 1
            w2p = jnp.pad(sm["gla_w_gk2"][j], ((0, 128 - GLA_RANK), (0, 0)))
            bgk, gn = _vec(sm["gla_b_gk"][j]), _vec(sm["gla_g_norm"][j])
            proj = mm_nn(x, wts["gla_w_in"], j, pro="mod", p1=sc1, p2=sh1, tm=1024, tn=640, name=f"gla_proj_{i}")
            og, states = gla_fwd(proj, w2p, bgk, gn, name=f"gla_core_{i}")
            y = mm_nn(og, wts["gla_w_out"], j, tm=1024, tn=1024, name=f"gla_out_{i}")
            rec.update(kind="gla", j=j, w2p=w2p, bgk=bgk, gn=gn, proj=proj, og=og, states=states)
        else:
            j = n_att
            n_att += 1
            rel = sm["att_rel_bias"][j]
            rel_pad = jnp.pad(rel, ((0, 0), (0, REL_PAD - N_REL)), constant_values=NEG_INF)
            tiles = rel_bias_tiles(rel_pad, rel_idx, name=f"att_bias_{i}")
            tiles = tiles.reshape(ATT_H, len(REL_TILES), REL_TILE, REL_TILE)
            clip = jnp.broadcast_to(rel[:, 2 * MAX_REL][:, None, None], (ATT_H, 1, 128))
            qkv = mm_nn(x, wts["att_w_in"], j, pro="mod", p1=sc1, p2=sh1, bias=_vec(sm["att_b_in"][j]),
                        out_dtype=BF16, tm=1024, tn=1024, name=f"att_proj_{i}")
            o = attn_fwd(qkv, tiles, clip, name=f"att_core_{i}")
            y = mm_nn(o, wts["att_w_out"], j, tm=1024, tn=1024, name=f"att_out_{i}")
            rec.update(kind="att", j=j, tiles=tiles, clip=clip, qkv=qkv, o=o)
        x1 = ln_fwd(x, y, g1, _vec(sm["ln_g"][i, 0]), _vec(sm["ln_b"][i, 0]), name=f"ln_mix_{i}")
        h = mm_nn(x1, wts["ff_w1"], i, pro="mod", p1=sc2, p2=sh2, tm=1024, tn=1024, name=f"ff_up_{i}")
        y2 = mm_nn_ksplit(h, wts["ff_w2"], i, pro="relu2", tm=1024, tk=1024, name=f"ff_down_{i}")
        x2 = ln_fwd(x1, y2, g2, _vec(sm["ln_g"][i, 1]), _vec(sm["ln_b"][i, 1]), name=f"ln_ff_{i}")
        rec.update(y=y, x1=x1, h=h, y2=y2)
        saved.append(rec)
        x = x2

    dy, loss = loss_head(x, target, name="loss_head")

    gw = {n: None for n in BIG}
    gw_shape = {n: wts[n].shape for n in BIG}

    def wgrad(weight, layer, a, d, *, tn, tk=1024, tm=512, col_block0=0, **kw):
        gw[weight] = mm_tn(a, d, tk=tk, tn=tn, tm=tm, out_buf=gw[weight], out_shape=gw_shape[weight],
                           layer=layer, col_block0=col_block0, **kw)

    gs = {"ln_g": [[None, None] for _ in range(DEPTH)], "ln_b": [[None, None] for _ in range(DEPTH)],
          "gla_w_gk2": [None] * 2, "gla_b_gk": [None] * 2, "gla_g_norm": [None] * 2, "att_b_in": [None] * 2,
          "att_rel_bias": [None] * 2}
    dmods = [[None] * 6 for _ in range(DEPTH)]
    nxt = None
    nxt_slot = None
    for i in reversed(range(DEPTH)):
        rec = saved[i]
        sh1, sc1, g1, sh2, sc2, g2 = [mods[i, k:k + 1] for k in range(6)]
        x0, x1 = rec["x0"], rec["x1"]
        if nxt is None:
            dz2, acc = ln_bwd(x1, rec["y2"], g2, _vec(sm["ln_g"][i, 1]), dout=dy, name=f"ln_ff_bwd_{i}")
        else:
            dz2, acc = ln_bwd(x1, rec["y2"], g2, _vec(sm["ln_g"][i, 1]), nxt=nxt, name=f"ln_ff_bwd_{i}")
            dmods[nxt_slot[0]][nxt_slot[1]] = acc[3]
            dmods[nxt_slot[0]][nxt_slot[2]] = acc[4]
        gs["ln_g"][i][1], gs["ln_b"][i][1], dmods[i][5] = acc[0], acc[1], acc[2]
        dh = mm_nt([dz2], wts["ff_w2"], i, pro="scale", p1=g2, epi_h=rec["h"], out_dtype=BF16, tm=1024, tn=512,
                   name=f"ff_down_bwd_{i}")
        wgrad("ff_w2", i, rec["h"], dz2, pro="relu2", dscale=g2, tk=2048, tn=1024, tm=256, name=f"ff_w2_grad_{i}")
        du2 = mm_nt([dh], wts["ff_w1"], i, tm=1024, tn=1024, name=f"ff_up_bwd_{i}")
        wgrad("ff_w1", i, x1, dh, pro="mod", p1=sc2, p2=sh2, tn=2048, name=f"ff_w1_grad_{i}")
        dz1, acc = ln_bwd(x0, rec["y"], g1, _vec(sm["ln_g"][i, 0]), nxt=(dz2, du2, sc2, x1), name=f"ln_mix_bwd_{i}")
        dmods[i][4], dmods[i][3] = acc[3], acc[4]
        gs["ln_g"][i][0], gs["ln_b"][i][0], dmods[i][2] = acc[0], acc[1], acc[2]
        j = rec["j"]
        if rec["kind"] == "gla":
            dog = mm_nt([dz1], wts["gla_w_out"], j, pro="scale", p1=g1, tm=1024, tn=1024, name=f"gla_out_bwd_{i}")
            wgrad("gla_w_out", j, rec["og"], dz1, dscale=g1, tn=1024, name=f"gla_wout_grad_{i}")
            dproj, dw2p, dbgk, dgn = gla_bwd(rec["proj"], dog, rec["states"], rec["w2p"], rec["bgk"], rec["gn"],
                                             name=f"gla_core_bwd_{i}")
            gs["gla_w_gk2"][j], gs["gla_b_gk"][j], gs["gla_g_norm"][j] = dw2p[:GLA_RANK], dbgk[0], dgn[0]
            du1 = mm_nt([dproj], wts["gla_w_in"], j, tm=1024, tn=1024, name=f"gla_proj_bwd_{i}")
            wgrad("gla_w_in", j, x0, dproj, pro="mod", p1=sc1, p2=sh1, tk=512, tn=GLA_INP, name=f"gla_win_grad_{i}")
        else:
            do = mm_nt([dz1], wts["att_w_out"], j, pro="scale", p1=g1, out_dtype=BF16, tm=1024, tn=1024,
                       name=f"att_out_bwd_{i}")
            wgrad("att_w_out", j, rec["o"], dz1, dscale=g1, tn=1024, name=f"att_wout_grad_{i}")
            dq, dk, dv, dtiles, dclip = attn_bwd(rec["qkv"], do, rec["tiles"], rec["clip"], name=f"att_core_bwd_{i}")
            drel = rel_bias_grad(dtiles.reshape(ATT_H, -1), dclip.reshape(ATT_H, 128), rel_idx,
                                 name=f"att_bias_grad_{i}")
            gs["att_rel_bias"][j] = drel[:, :N_REL]
            gs["att_b_in"][j] = jnp.concatenate(
                [colsum(t, name=f"att_bin_grad_{i}_{n}")[0] for n, t in enumerate((dq, dk, dv))])
            du1 = mm_nt([dq, dk, dv], wts["att_w_in"], j, tm=512, tn=1024, name=f"att_proj_bwd_{i}")
            for n, t in enumerate((dq, dk, dv)):
                wgrad("att_w_in", j, x0, t, pro="mod", p1=sc1, p2=sh1, tn=1024, col_block0=n,
                      name=f"att_win_grad_{i}_{n}")
        nxt = (dz1, du1, sc1, x0)
        nxt_slot = (i, 1, 0)
    dx, acc = combine_final(nxt[0], nxt[1], nxt[2], nxt[3], name="grad_x")
    dmods[0][1], dmods[0][0] = acc[3], acc[4]
    dmods = jnp.stack([jnp.stack(r) for r in dmods])
    gs = {k: jnp.stack([jnp.stack(r) if isinstance(r, list) else r for r in v]) for k, v in gs.items()}
    return loss, dx, dmods, gw, gs


WEIGHTS = ("w_ada", "b_ada", "ln_g", "ln_b", "gla_w_in", "gla_w_gk2", "gla_b_gk", "gla_g_norm", "gla_w_out",
           "att_w_in", "att_b_in", "att_rel_bias", "att_w_out", "ff_w1", "ff_w2")
SMALL_SHARDED = {"ln_g": 2, "ln_b": 2, "gla_w_gk2": 2, "gla_g_norm": 2, "att_b_in": 1}
SMALL_GRADS = ("ln_g", "ln_b", "gla_w_gk2", "gla_b_gk", "gla_g_norm", "att_b_in", "att_rel_bias")


def kernel(x, c, w_ada, b_ada, ln_g, ln_b, gla_w_in, gla_w_gk2, gla_b_gk, gla_g_norm, gla_w_out, att_w_in, att_b_in, att_rel_bias, att_w_out, ff_w1, ff_w2, loss_target, m_w_ada, m_b_ada, m_ln_g, m_ln_b, m_gla_w_in, m_gla_w_gk2, m_gla_b_gk, m_gla_g_norm, m_gla_w_out, m_att_w_in, m_att_b_in, m_att_rel_bias, m_att_w_out, m_ff_w1, m_ff_w2, v_w_ada, v_b_ada, v_ln_g, v_ln_b, v_gla_w_in, v_gla_w_gk2, v_gla_b_gk, v_gla_g_norm, v_gla_w_out, v_att_w_in, v_att_b_in, v_att_rel_bias, v_att_w_out, v_ff_w1, v_ff_w2):
    w = dict(w_ada=w_ada, b_ada=b_ada, ln_g=ln_g, ln_b=ln_b, gla_w_in=gla_w_in, gla_w_gk2=gla_w_gk2,
             gla_b_gk=gla_b_gk, gla_g_norm=gla_g_norm, gla_w_out=gla_w_out, att_w_in=att_w_in, att_b_in=att_b_in,
             att_rel_bias=att_rel_bias, att_w_out=att_w_out, ff_w1=ff_w1, ff_w2=ff_w2)
    m = dict(w_ada=m_w_ada, b_ada=m_b_ada, ln_g=m_ln_g, ln_b=m_ln_b, gla_w_in=m_gla_w_in, gla_w_gk2=m_gla_w_gk2,
             gla_b_gk=m_gla_b_gk, gla_g_norm=m_gla_g_norm, gla_w_out=m_gla_w_out, att_w_in=m_att_w_in,
             att_b_in=m_att_b_in, att_rel_bias=m_att_rel_bias, att_w_out=m_att_w_out, ff_w1=m_ff_w1, ff_w2=m_ff_w2)
    v = dict(w_ada=v_w_ada, b_ada=v_b_ada, ln_g=v_ln_g, ln_b=v_ln_b, gla_w_in=v_gla_w_in, gla_w_gk2=v_gla_w_gk2,
             gla_b_gk=v_gla_b_gk, gla_g_norm=v_gla_g_norm, gla_w_out=v_gla_w_out, att_w_in=v_att_w_in,
             att_b_in=v_att_b_in, att_rel_bias=v_att_rel_bias, att_w_out=v_att_w_out, ff_w1=v_ff_w1, ff_w2=v_ff_w2)
    xi, yi, ci = _my_place()
    me = 4 * xi + 2 * yi + ci

    small_names = tuple(SMALL_SHARDED)
    small_in = _pack_small([c] + [w[n] for n in small_names])
    small_all = all_gather(small_in, name="gather_small").reshape(N_DEV, -1, 128)
    parts = _unpack_small(small_all, [c.shape] + [w[n].shape for n in small_names])
    c_all = parts[0].reshape(N_DEV, D)
    sm = {"gla_b_gk": gla_b_gk, "att_rel_bias": att_rel_bias}
    for n, p in zip(small_names, parts[1:]):
        ax = SMALL_SHARDED[n]
        sm[n] = jnp.moveaxis(p, 0, ax).reshape(p.shape[1:ax + 1] + (N_DEV * p.shape[ax + 1],) + p.shape[ax + 2:])

    n_ada = w_ada.shape[2]
    mp = mods_partial(c_all, w_ada, name="mods_partial")
    mp_all = all_gather(mp.reshape(DEPTH * N_DEV, n_ada), name="gather_mods")
    mp_all = mp_all.reshape(N_DEV, DEPTH, N_DEV, n_ada)
    mods = lax.dynamic_index_in_dim(mp_all, me, axis=2, keepdims=False)
    mods = mods.transpose(1, 0, 2).reshape(DEPTH, 6 * D) + b_ada
    mods = mods.reshape(DEPTH, 6, D)

    kinds = [KIND[n] for n in BIG]
    shard_shapes = [w[n].shape for n in BIG]
    wts = dict(zip(BIG, gather_weights([w[n].astype(BF16) for n in BIG], kinds, name="gather_weights")))
    stacked = wts["gla_w_in"]
    w_in = stacked.transpose(1, 2, 0, 3).reshape(stacked.shape[1], D, GLA_IN)
    wts["gla_w_in"] = jnp.pad(w_in, ((0, 0), (0, 0), (0, GLA_INP - GLA_IN)))

    loss, dx, dmods, gw, gs = _trunk(x.reshape(x.shape[1:]), loss_target.reshape(x.shape[1:]), mods, wts, sm)
    loss = lax.psum(loss[0, 0], ("x", "y", "c"))

    n_l, _, n_in = w["gla_w_in"].shape
    gw["gla_w_in"] = gw["gla_w_in"][:, :, :GLA_IN].reshape(n_l, D, N_DEV, n_in).transpose(2, 0, 1, 3)
    grads = [gw[n] for n in BIG]
    got = sibling_exchange(grads, kinds, shard_shapes, name="grads_sibling")
    core = ci.astype(jnp.int32).reshape(1)
    pair = [pair_sum_bf16(g, r, k, core, name=f"grads_pair_sum_{n}") for n, g, r, k in zip(BIG, grads, got, kinds)]
    chip_parts = chip_exchange(pair, name="grads_chips")
    results = {}
    for n, p in zip(BIG, chip_parts):
        results[n] = adamw_nd(w[n], m[n], v[n], p, name=f"adamw_{n}")

    dm_flat = dmods.reshape(DEPTH, 6 * D)
    small_g = [dm_flat] + [gs[n].reshape(sm[n].shape) for n in SMALL_GRADS]
    small_shapes = [a.shape for a in small_g]
    sg_all = all_gather(_pack_small(small_g), name="gather_small_grads").reshape(N_DEV, -1, 128)
    summed = _unpack_small(sum_parts(sg_all, name="sum_small_grads"), small_shapes)
    g_full = dict(zip(("b_ada",) + SMALL_GRADS, summed))
    dm_all = _unpack_small(sg_all, small_shapes)[0]
    dm_mine = lax.dynamic_slice_in_dim(dm_all, me * n_ada, n_ada, axis=2).transpose(1, 0, 2)
    g_w_ada = w_ada_grad(c_all, dm_mine, name="w_ada_grad")

    results["w_ada"] = adamw_nd(w_ada, m_w_ada, v_w_ada, g_w_ada[None], name="adamw_w_ada")
    for n in ("b_ada",) + SMALL_GRADS:
        g = g_full[n]
        if n in SMALL_SHARDED:
            ax = SMALL_SHARDED[n]
            width = w[n].shape[ax]
            g = lax.dynamic_slice_in_dim(g, me * width, width, axis=ax)
        results[n] = adamw_nd(w[n], m[n], v[n], g[None], name=f"adamw_{n}")

    out = [loss, dx[None]]
    for k in range(4):
        out += [results[n][k] for n in WEIGHTS]
    return tuple(out)
```

```python
import numpy as np
import jax
import jax.numpy as jnp
from jax import lax
from jax.experimental import pallas as pl
from jax.experimental.pallas import tpu as pltpu

F32 = jnp.float32
BF16 = jnp.bfloat16
HIGHEST = lax.Precision.HIGHEST
MESH = pl.DeviceIdType.MESH

N_DEV = 8
D = 1024
DEPTH = 4
CHUNK = 64
ALPHA = (2.0 * DEPTH) ** 0.25
LN_EPS = 1e-5
RMS_EPS = 1e-6
NEG_INF = -1e30

GLA_H = 4
GLA_DKH = 128
GLA_DVH = 256
GLA_DK = GLA_H * GLA_DKH
GLA_DV = GLA_H * GLA_DVH
GLA_RANK = 16
GLA_IN = 2 * GLA_DK + 2 * GLA_DV + GLA_RANK
GLA_INP = 3200
GLA_TAU_INV = 1.0 / 16.0

ATT_H = 16
ATT_HD = 64
ATT_QB = 256
ATT_KB = 3 * ATT_QB
LEFT = 8 * CHUNK
MAX_REL = 128
N_REL = 2 * MAX_REL + 1
REL_PAD = 384
REL_TILE = 128
REL_TILES = (3, 4)
D_FF = 4 * D

ADAM_LR = 0.001
ADAM_B1 = 0.9
ADAM_B2 = 0.999
ADAM_EPS = 1e-08
ADAM_WD = 0.01
ADAM_STEP = 10

VMEM_LIMIT = 48 * 1024 * 1024


def _params(n_axes):
    return pltpu.CompilerParams(dimension_semantics=("arbitrary",) * n_axes, vmem_limit_bytes=VMEM_LIMIT)


def _dot(a, b):
    return jnp.dot(a, b, preferred_element_type=F32)


def _dot_nt(a, b):
    return lax.dot_general(a, b, (((1,), (1,)), ((), ())), preferred_element_type=F32)


def _dot_tn(a, b):
    return lax.dot_general(a, b, (((0,), (0,)), ((), ())), preferred_element_type=F32)


def _bf(a):
    return a.astype(BF16)


def _prologue(kind, a, p1=None, p2=None):
    if kind == "mod":
        return a * (1.0 + p1) + p2
    if kind == "scale":
        return a * (1.0 + p1)
    if kind == "relu2":
        r = jnp.maximum(a, 0.0)
        return r * r
    return a


def mm_nn(a, b, layer, *, pro=None, p1=None, p2=None, bias=None, out_dtype=F32, tm, tn, name):
    M, K = a.shape
    N = b.shape[2]
    tm = min(tm, M)
    n_p = {"mod": 2, "scale": 1}.get(pro, 0)
    has_bias = bias is not None
    direct = pro is None and a.dtype == BF16

    def body(*refs):
        a_ref, b_ref = refs[0], refs[1]
        p_refs = refs[2:2 + n_p]
        bias_ref = refs[2 + n_p] if has_bias else None
        if direct:
            o_ref = refs[-1]
            lhs = a_ref[...]
        else:
            o_ref, abf = refs[-2], refs[-1]

            @pl.when(pl.program_id(1) == 0)
            def _():
                abf[...] = _bf(_prologue(pro, a_ref[...].astype(F32), *[r[...] for r in p_refs]))

            lhs = abf[...]
        acc = _dot(lhs, b_ref[...])
        if has_bias:
            acc = acc + bias_ref[...]
        o_ref[...] = acc.astype(out_dtype)

    in_specs = [pl.BlockSpec((tm, K), lambda i, j: (i, 0)), pl.BlockSpec((None, K, tn), lambda i, j: (layer, 0, j))]
    args = [a, b]
    for p in (p1, p2)[:n_p]:
        in_specs.append(pl.BlockSpec((1, K), lambda i, j: (0, 0)))
        args.append(p)
    if has_bias:
        in_specs.append(pl.BlockSpec((1, tn), lambda i, j: (0, j)))
        args.append(bias)
    return pl.pallas_call(
        body, name=name, grid=(M // tm, N // tn), in_specs=in_specs,
        out_specs=pl.BlockSpec((tm, tn), lambda i, j: (i, j)),
        out_shape=jax.ShapeDtypeStruct((M, N), out_dtype),
        scratch_shapes=[] if direct else [pltpu.VMEM((tm, K), BF16)], compiler_params=_params(2),
    )(*args)


def mm_nn_ksplit(a, b, layer, *, pro=None, tm, tk, name):
    M, K = a.shape
    N = b.shape[2]
    tm = min(tm, M)

    def body(a_ref, b_ref, o_ref):
        part = _dot(_bf(_prologue(pro, a_ref[...].astype(F32))), b_ref[...])

        @pl.when(pl.program_id(1) == 0)
        def _():
            o_ref[...] = part

        @pl.when(pl.program_id(1) > 0)
        def _():
            o_ref[...] += part

    return pl.pallas_call(
        body, name=name, grid=(M // tm, K // tk),
        in_specs=[pl.BlockSpec((tm, tk), lambda i, k: (i, k)), pl.BlockSpec((None, tk, N), lambda i, k: (layer, k, 0))],
        out_specs=pl.BlockSpec((tm, N), lambda i, k: (i, 0)),
        out_shape=jax.ShapeDtypeStruct((M, N), F32), compiler_params=_params(2),
    )(a, b)


def mm_nt(a_parts, w, layer, *, pro=None, p1=None, epi_h=None, out_dtype=F32, tm, tn, name):
    M = a_parts[0].shape[0]
    tm = min(tm, M)
    widths = [p.shape[1] for p in a_parts]
    Nw = sum(widths)
    Kw = w.shape[1]
    n_a = len(a_parts)
    has_p = pro == "scale"
    has_h = epi_h is not None
    direct = n_a == 1 and not has_p and a_parts[0].dtype == BF16

    def body(*refs):
        a_refs = refs[:n_a]
        w_ref = refs[n_a]
        k = n_a + 1
        p_ref = refs[k] if has_p else None
        k += int(has_p)
        h_ref = refs[k] if has_h else None
        if direct:
            o_ref = refs[-1]
            lhs = a_refs[0][...]
        else:
            o_ref, abf = refs[-2], refs[-1]

            @pl.when(pl.program_id(1) == 0)
            def _():
                off = 0
                for r, wd in zip(a_refs, widths):
                    av = r[...]
                    if has_p:
                        av = av.astype(F32) * (1.0 + p_ref[...])
                    abf[:, off:off + wd] = _bf(av)
                    off += wd

            lhs = abf[...]
        acc = _dot_nt(lhs, w_ref[...])
        if has_h:
            acc = acc * (2.0 * jnp.maximum(h_ref[...], 0.0))
        o_ref[...] = acc.astype(out_dtype)

    in_specs = [pl.BlockSpec((tm, wd), lambda i, j: (i, 0)) for wd in widths]
    in_specs.append(pl.BlockSpec((None, tn, Nw), lambda i, j: (layer, j, 0)))
    args = list(a_parts) + [w]
    if has_p:
        in_specs.append(pl.BlockSpec((1, Nw), lambda i, j: (0, 0)))
        args.append(p1)
    if has_h:
        in_specs.append(pl.BlockSpec((tm, tn), lambda i, j: (i, j)))
        args.append(epi_h)
    return pl.pallas_call(
        body, name=name, grid=(M // tm, Kw // tn), in_specs=in_specs,
        out_specs=pl.BlockSpec((tm, tn), lambda i, j: (i, j)),
        out_shape=jax.ShapeDtypeStruct((M, Kw), out_dtype),
        scratch_shapes=[] if direct else [pltpu.VMEM((tm, Nw), BF16)], compiler_params=_params(2),
    )(*args)


def mm_tn(a, d, *, pro=None, p1=None, p2=None, dscale=None, tk, tn, tm, out_buf, out_shape, layer, col_block0=0,
          name):
    M, Kf = a.shape
    N = d.shape[1]
    n_p = {"mod": 2}.get(pro, 0)
    has_ds = dscale is not None
    has_buf = out_buf is not None
    n_m = M // tm

    def body(*refs):
        a_ref, d_ref = refs[0], refs[1]
        p_refs = refs[2:2 + n_p]
        ds_ref = refs[2 + n_p] if has_ds else None
        o_ref = refs[-1]
        m = pl.program_id(2)

        av = a_ref[...]
        if pro is not None:
            av = _prologue(pro, av.astype(F32), *[r[...] for r in p_refs])
        dv = d_ref[...]
        if has_ds:
            dv = dv.astype(F32) * (1.0 + ds_ref[...])
        part = _dot_tn(_bf(av), _bf(dv))

        @pl.when(m == 0)
        def _():
            o_ref[...] = part

        @pl.when(m > 0)
        def _():
            o_ref[...] += part

    in_specs = [pl.BlockSpec((tm, tk), lambda i, j, m: (m, i)), pl.BlockSpec((tm, tn), lambda i, j, m: (m, j))]
    args = [a, d]
    for p in (p1, p2)[:n_p]:
        in_specs.append(pl.BlockSpec((1, tk), lambda i, j, m: (0, i)))
        args.append(p)
    if has_ds:
        in_specs.append(pl.BlockSpec((1, tn), lambda i, j, m: (0, j)))
        args.append(dscale)
    aliases = {}
    if has_buf:
        in_specs.append(pl.BlockSpec(memory_space=pl.ANY))
        args.append(out_buf)
        aliases = {len(args) - 1: 0}
    return pl.pallas_call(
        body, name=name, grid=(Kf // tk, N // tn, n_m), in_specs=in_specs,
        out_specs=pl.BlockSpec((None, tk, tn), lambda i, j, m: (layer, i, col_block0 + j)),
        out_shape=jax.ShapeDtypeStruct(out_shape, F32), input_output_aliases=aliases,
        compiler_params=_params(3),
    )(*args)


ROW_BLOCK = 512
ACC_ROWS = 8


def _ln_stats(z):
    mu = jnp.mean(z, axis=-1, keepdims=True)
    zc = z - mu
    var = jnp.mean(zc * zc, axis=-1, keepdims=True)
    return zc, lax.rsqrt(var + LN_EPS)


def ln_fwd(x, y, gate, lng, lnb, *, name):
    S = x.shape[0]

    def body(x_ref, y_ref, gt_ref, g_ref, b_ref, o_ref):
        z = ALPHA * x_ref[...] + (1.0 + gt_ref[...]) * y_ref[...]
        zc, rstd = _ln_stats(z)
        o_ref[...] = (zc * rstd) * g_ref[...] + b_ref[...]

    row = pl.BlockSpec((ROW_BLOCK, D), lambda i: (i, 0))
    vec = pl.BlockSpec((1, D), lambda i: (0, 0))
    return pl.pallas_call(
        body, name=name, grid=(S // ROW_BLOCK,), in_specs=[row, row, vec, vec, vec], out_specs=row,
        out_shape=jax.ShapeDtypeStruct((S, D), F32), compiler_params=_params(1),
    )(x, y, gate, lng, lnb)


def _add_colsum(acc_ref, r, val):
    acc_ref[r:r + 1, :] += jnp.sum(val, axis=0, keepdims=True)


def ln_bwd(x_in, y, gate, lng, *, dout=None, nxt=None, name):
    S = x_in.shape[0]
    has_next = nxt is not None

    def body(*refs):
        if has_next:
            dzn_ref, dun_ref, scn_ref, xo_ref = refs[:4]
            k = 4
        else:
            do_ref = refs[0]
            k = 1
        x_ref, y_ref, gt_ref, g_ref = refs[k:k + 4]
        dz_ref, acc_ref = refs[k + 4:]

        @pl.when(pl.program_id(0) == 0)
        def _():
            acc_ref[...] = jnp.zeros_like(acc_ref)

        if has_next:
            du = dun_ref[...]
            dout_v = ALPHA * dzn_ref[...] + du * (1.0 + scn_ref[...])
            _add_colsum(acc_ref, 3, du * xo_ref[...])
            _add_colsum(acc_ref, 4, du)
        else:
            dout_v = do_ref[...]
        yv = y_ref[...]
        z = ALPHA * x_ref[...] + (1.0 + gt_ref[...]) * yv
        zc, rstd = _ln_stats(z)
        xhat = zc * rstd
        _add_colsum(acc_ref, 0, dout_v * xhat)
        _add_colsum(acc_ref, 1, dout_v)
        dxh = dout_v * g_ref[...]
        m1 = jnp.mean(dxh, axis=-1, keepdims=True)
        m2 = jnp.mean(dxh * xhat, axis=-1, keepdims=True)
        dz = rstd * (dxh - m1 - xhat * m2)
        _add_colsum(acc_ref, 2, dz * yv)
        dz_ref[...] = dz

    row = pl.BlockSpec((ROW_BLOCK, D), lambda i: (i, 0))
    vec = pl.BlockSpec((1, D), lambda i: (0, 0))
    if has_next:
        in_specs = [row, row, vec, row]
        args = list(nxt)
    else:
        in_specs = [row]
        args = [dout]
    in_specs += [row, row, vec, vec]
    args += [x_in, y, gate, lng]
    return pl.pallas_call(
        body, name=name, grid=(S // ROW_BLOCK,), in_specs=in_specs,
        out_specs=[row, pl.BlockSpec((ACC_ROWS, D), lambda i: (0, 0))],
        out_shape=[jax.ShapeDtypeStruct((S, D), F32), jax.ShapeDtypeStruct((ACC_ROWS, D), F32)],
        compiler_params=_params(1),
    )(*args)


def combine_final(dz, du, sc, x_in, *, name):
    S = dz.shape[0]

    def body(dz_ref, du_ref, sc_ref, x_ref, dx_ref, acc_ref):
        @pl.when(pl.program_id(0) == 0)
        def _():
            acc_ref[...] = jnp.zeros_like(acc_ref)

        du_v = du_ref[...]
        dx_ref[...] = ALPHA * dz_ref[...] + du_v * (1.0 + sc_ref[...])
        _add_colsum(acc_ref, 3, du_v * x_ref[...])
        _add_colsum(acc_ref, 4, du_v)

    row = pl.BlockSpec((ROW_BLOCK, D), lambda i: (i, 0))
    vec = pl.BlockSpec((1, D), lambda i: (0, 0))
    return pl.pallas_call(
        body, name=name, grid=(S // ROW_BLOCK,), in_specs=[row, row, vec, row],
        out_specs=[row, pl.BlockSpec((ACC_ROWS, D), lambda i: (0, 0))],
        out_shape=[jax.ShapeDtypeStruct((S, D), F32), jax.ShapeDtypeStruct((ACC_ROWS, D), F32)],
        compiler_params=_params(1),
    )(dz, du, sc, x_in)


def colsum(a, *, name):
    S, N = a.shape

    def body(a_ref, o_ref):
        @pl.when(pl.program_id(0) == 0)
        def _():
            o_ref[...] = jnp.zeros_like(o_ref)

        o_ref[...] += jnp.sum(a_ref[...].astype(F32), axis=0, keepdims=True)

    return pl.pallas_call(
        body, name=name, grid=(S // ROW_BLOCK,), in_specs=[pl.BlockSpec((ROW_BLOCK, N), lambda i: (i, 0))],
        out_specs=pl.BlockSpec((1, N), lambda i: (0, 0)), out_shape=jax.ShapeDtypeStruct((1, N), F32),
        compiler_params=_params(1),
    )(a)


def loss_head(y, t, *, name):
    S = y.shape[0]

    def body(y_ref, t_ref, dy_ref, l_ref):
        @pl.when(pl.program_id(0) == 0)
        def _():
            l_ref[...] = jnp.zeros_like(l_ref)

        e = y_ref[...] - t_ref[...]
        dy_ref[...] = e * (1.0 / D)
        per_tok = jnp.sum(e * e, axis=1, keepdims=True) * (1.0 / D)
        l_ref[...] += 0.5 * jnp.sum(per_tok, axis=0, keepdims=True)

    row = pl.BlockSpec((ROW_BLOCK, D), lambda i: (i, 0))
    return pl.pallas_call(
        body, name=name, grid=(S // ROW_BLOCK,), in_specs=[row, row],
        out_specs=[row, pl.BlockSpec((8, 128), lambda i: (0, 0))],
        out_shape=[jax.ShapeDtypeStruct((S, D), F32), jax.ShapeDtypeStruct((8, 128), F32)],
        compiler_params=_params(1),
    )(y, t)


def _log_sigmoid(x):
    return jnp.minimum(x, 0.0) - jnp.log(1.0 + jnp.exp(-jnp.abs(x)))


def _silu(x):
    return x * (1.0 / (1.0 + jnp.exp(-x)))


def _cumsum_steps(x):
    row = lax.broadcasted_iota(jnp.int32, x.shape, 0)
    step = 1
    while step < x.shape[0]:
        x = x + jnp.where(row >= step, pltpu.roll(x, step, 0), 0.0)
        step *= 2
    return x


@jax.custom_vjp
def _cumsum_rows(x):
    return _cumsum_steps(x)


def _cumsum_rows_fwd(x):
    return _cumsum_steps(x), None


def _cumsum_rows_bwd(_, g):
    return (jnp.sum(g, axis=0, keepdims=True) - _cumsum_steps(g) + g,)


_cumsum_rows.defvjp(_cumsum_rows_fwd, _cumsum_rows_bwd)


def _gla_chunk(q, k, v, g, gk, s0t, w2p, bgk, gn):
    C = q.shape[0]
    row = lax.broadcasted_iota(jnp.int32, (C, C), 0)
    col = lax.broadcasted_iota(jnp.int32, (C, C), 1)
    lower = row >= col
    la = _log_sigmoid(_dot(_bf(gk), _bf(w2p)) + bgk) * GLA_TAU_INV
    outs, states = [], []
    for h in range(GLA_H):
        ks = slice(h * GLA_DKH, (h + 1) * GLA_DKH)
        vs = slice(h * GLA_DVH, (h + 1) * GLA_DVH)
        qh = q[:, ks] * (GLA_DKH ** -0.5)
        kh, vh, gh, lah, s0 = k[:, ks], v[:, vs], g[:, vs], la[:, ks], s0t[h]
        cum = _cumsum_rows(lah)
        e_pos = jnp.exp(cum)
        e_neg = jnp.exp(-cum)
        q_f = qh * e_pos
        a_f = _dot_nt(_bf(q_f), _bf(kh * e_neg))
        a_b = _dot_nt(_bf(qh * e_neg), _bf(kh * e_pos))
        att = jnp.where(lower, a_f, a_b)
        o = _dot(_bf(att), _bf(vh)) + _dot_nt(_bf(q_f), _bf(s0))
        tot = jnp.sum(lah, axis=0, keepdims=True)
        k_end = kh * jnp.exp(tot - cum)
        states.append(s0 * jnp.exp(tot) + _dot_tn(_bf(vh), _bf(k_end)))
        on = o * lax.rsqrt(jnp.mean(o * o, axis=-1, keepdims=True) + RMS_EPS) * gn[:, vs]
        outs.append(on * _silu(gh))
    return jnp.concatenate(outs, axis=1), tuple(states)


def _gla_split(p):
    return (p[:, 0:GLA_DK], p[:, GLA_DK:2 * GLA_DK], p[:, 2 * GLA_DK:2 * GLA_DK + GLA_DV],
            p[:, 2 * GLA_DK + GLA_DV:2 * GLA_DK + 2 * GLA_DV], p[:, 2 * GLA_DK + 2 * GLA_DV:GLA_INP])


def gla_fwd(proj, w2p, bgk, gn, *, name):
    S = proj.shape[0]
    n_c = S // CHUNK

    def body(p_ref, w_ref, b_ref, gn_ref, o_ref, st_ref, st):
        @pl.when(pl.program_id(0) == 0)
        def _():
            st[...] = jnp.zeros_like(st)

        s0 = tuple(st[h] for h in range(GLA_H))
        for h in range(GLA_H):
            st_ref[h] = s0[h]
        og, s1 = _gla_chunk(*_gla_split(p_ref[...]), s0, w_ref[...], b_ref[...], gn_ref[...])
        o_ref[...] = _bf(og)
        for h in range(GLA_H):
            st[h] = s1[h]

    full = lambda shape: pl.BlockSpec(shape, lambda i: (0,) * len(shape))
    return pl.pallas_call(
        body, name=name, grid=(n_c,),
        in_specs=[pl.BlockSpec((CHUNK, GLA_INP), lambda i: (i, 0)), full((128, GLA_DK)), full((1, GLA_DK)),
                  full((1, GLA_DV))],
        out_specs=[pl.BlockSpec((CHUNK, GLA_DV), lambda i: (i, 0)),
                   pl.BlockSpec((None, GLA_H, GLA_DVH, GLA_DKH), lambda i: (i, 0, 0, 0))],
        out_shape=[jax.ShapeDtypeStruct((S, GLA_DV), BF16),
                   jax.ShapeDtypeStruct((n_c, GLA_H, GLA_DVH, GLA_DKH), F32)],
        scratch_shapes=[pltpu.VMEM((GLA_H, GLA_DVH, GLA_DKH), F32)], compiler_params=_params(1),
    )(proj, w2p, bgk, gn)


def gla_bwd(proj, dog, states, w2p, bgk, gn, *, name):
    S = proj.shape[0]
    n_c = S // CHUNK

    def body(p_ref, dog_ref, st_ref, w_ref, b_ref, gn_ref, dp_ref, dw_ref, db_ref, dgn_ref, ds):
        @pl.when(pl.program_id(0) == 0)
        def _():
            ds[...] = jnp.zeros_like(ds)
            dw_ref[...] = jnp.zeros_like(dw_ref)
            db_ref[...] = jnp.zeros_like(db_ref)
            dgn_ref[...] = jnp.zeros_like(dgn_ref)

        q, k, v, g, gk = _gla_split(p_ref[...])
        s0 = tuple(st_ref[h] for h in range(GLA_H))
        _, vjp = jax.vjp(_gla_chunk, q, k, v, g, gk, s0, w_ref[...], b_ref[...], gn_ref[...])
        dq, dk, dv, dg, dgk, ds0, dw, db, dgn = vjp((dog_ref[...], tuple(ds[h] for h in range(GLA_H))))
        dp_ref[:, 0:GLA_DK] = _bf(dq)
        dp_ref[:, GLA_DK:2 * GLA_DK] = _bf(dk)
        dp_ref[:, 2 * GLA_DK:2 * GLA_DK + GLA_DV] = _bf(dv)
        dp_ref[:, 2 * GLA_DK + GLA_DV:2 * GLA_DK + 2 * GLA_DV] = _bf(dg)
        dp_ref[:, 2 * GLA_DK + 2 * GLA_DV:GLA_INP] = _bf(dgk)
        for h in range(GLA_H):
            ds[h] = ds0[h]
        dw_ref[...] += dw
        db_ref[...] += db
        dgn_ref[...] += dgn

    full = lambda shape: pl.BlockSpec(shape, lambda i: (0,) * len(shape))
    rev = lambda i: (n_c - 1 - i, 0)
    return pl.pallas_call(
        body, name=name, grid=(n_c,),
        in_specs=[pl.BlockSpec((CHUNK, GLA_INP), rev), pl.BlockSpec((CHUNK, GLA_DV), rev),
                  pl.BlockSpec((None, GLA_H, GLA_DVH, GLA_DKH), lambda i: (n_c - 1 - i, 0, 0, 0)),
                  full((128, GLA_DK)), full((1, GLA_DK)), full((1, GLA_DV))],
        out_specs=[pl.BlockSpec((CHUNK, GLA_INP), rev), full((128, GLA_DK)), full((1, GLA_DK)), full((1, GLA_DV))],
        out_shape=[jax.ShapeDtypeStruct((S, GLA_INP), BF16), jax.ShapeDtypeStruct((128, GLA_DK), F32),
                   jax.ShapeDtypeStruct((1, GLA_DK), F32), jax.ShapeDtypeStruct((1, GLA_DV), F32)],
        scratch_shapes=[pltpu.VMEM((GLA_H, GLA_DVH, GLA_DKH), F32)], compiler_params=_params(1),
    )(proj, dog, states, w2p, bgk, gn)


def _rel_index():
    t = np.arange(REL_TILE)[:, None]
    j = np.arange(REL_TILE)[None, :]
    tiles = []
    for m in REL_TILES:
        chunks = (REL_TILE // CHUNK) * m + j // CHUNK - t // CHUNK
        band = (chunks >= 0) & (chunks <= LEFT // CHUNK)
        dist = LEFT - REL_TILE * m + t - j
        tiles.append(np.where(band, np.minimum(dist, MAX_REL) + MAX_REL, N_REL))
    return jnp.asarray(np.stack(tiles).reshape(1, -1).astype(np.int32))


REL_BLOCK = 2048


def _one_hot(idx_row):
    return (lax.broadcasted_iota(jnp.int32, (REL_PAD, idx_row.shape[1]), 0) == idx_row).astype(F32)


def rel_bias_tiles(rel_pad, idx, *, name):
    E = idx.shape[1]

    def body(r_ref, i_ref, o_ref):
        o_ref[...] = jnp.dot(r_ref[...], _one_hot(i_ref[...]), precision=HIGHEST, preferred_element_type=F32)

    return pl.pallas_call(
        body, name=name, grid=(E // REL_BLOCK,),
        in_specs=[pl.BlockSpec((ATT_H, REL_PAD), lambda i: (0, 0)), pl.BlockSpec((1, REL_BLOCK), lambda i: (0, i))],
        out_specs=pl.BlockSpec((ATT_H, REL_BLOCK), lambda i: (0, i)),
        out_shape=jax.ShapeDtypeStruct((ATT_H, E), F32), compiler_params=_params(1),
    )(rel_pad, idx)


def rel_bias_grad(dtiles_flat, dclip, idx, *, name):
    E = idx.shape[1]
    n_steps = E // REL_BLOCK

    def body(d_ref, c_ref, i_ref, o_ref):
        @pl.when(pl.program_id(0) == 0)
        def _():
            o_ref[...] = jnp.zeros_like(o_ref)

        o_ref[...] += lax.dot_general(d_ref[...], _one_hot(i_ref[...]), (((1,), (1,)), ((), ())),
                                      precision=HIGHEST, preferred_element_type=F32)

        @pl.when(pl.program_id(0) == n_steps - 1)
        def _():
            at_clip = lax.broadcasted_iota(jnp.int32, (1, REL_PAD), 1) == 2 * MAX_REL
            o_ref[...] += jnp.where(at_clip, jnp.sum(c_ref[...], axis=1, keepdims=True), 0.0)

    return pl.pallas_call(
        body, name=name, grid=(n_steps,),
        in_specs=[pl.BlockSpec((ATT_H, REL_BLOCK), lambda i: (0, i)), pl.BlockSpec((ATT_H, 128), lambda i: (0, 0)),
                  pl.BlockSpec((1, REL_BLOCK), lambda i: (0, i))],
        out_specs=pl.BlockSpec((ATT_H, REL_PAD), lambda i: (0, 0)),
        out_shape=jax.ShapeDtypeStruct((ATT_H, REL_PAD), F32), compiler_params=_params(1),
    )(dtiles_flat, dclip, idx)


def _attn_bias(tiles, clip):
    const = jnp.broadcast_to(clip, (REL_TILE, REL_TILE))
    zero = jnp.zeros((REL_TILE, REL_TILE), F32)
    rows = []
    for qt in range(ATT_QB // REL_TILE):
        blocks = []
        for kt in range(ATT_KB // REL_TILE):
            m = kt - qt
            if m in REL_TILES:
                blocks.append(tiles[REL_TILES.index(m)])
            elif 0 <= m < REL_TILES[0]:
                blocks.append(const)
            else:
                blocks.append(zero)
        rows.append(jnp.concatenate(blocks, axis=1))
    return jnp.concatenate(rows, axis=0)


def _attn_bias_grad(ds, dt_ref, dc_ref, a):
    tile = lambda qt, kt: ds[qt * REL_TILE:(qt + 1) * REL_TILE, kt * REL_TILE:(kt + 1) * REL_TILE]
    const = None
    sums = [None] * len(REL_TILES)
    for qt in range(ATT_QB // REL_TILE):
        for kt in range(ATT_KB // REL_TILE):
            m = kt - qt
            if m in REL_TILES:
                n = REL_TILES.index(m)
                sums[n] = tile(qt, kt) if sums[n] is None else sums[n] + tile(qt, kt)
            elif 0 <= m < REL_TILES[0]:
                const = tile(qt, kt) if const is None else const + tile(qt, kt)
    for n, v in enumerate(sums):
        dt_ref[a, n] += v
    dc_ref[a] += jnp.sum(const, axis=0, keepdims=True)


def _attn_head_lanes():
    lane = lax.broadcasted_iota(jnp.int32, (1, 2 * ATT_HD), 1)
    return [(lane >= a * ATT_HD) & (lane < (a + 1) * ATT_HD) for a in range(2)]


def _attn_band_bias(tiles, clip):
    j = lax.broadcasted_iota(jnp.int32, (ATT_QB, ATT_KB), 1)
    t = lax.broadcasted_iota(jnp.int32, (ATT_QB, ATT_KB), 0)
    shift = CHUNK.bit_length() - 1
    chunks = lax.shift_right_logical(j, shift) - lax.shift_right_logical(t, shift)
    band = (chunks >= 0) & (chunks <= LEFT // CHUNK)
    return jnp.where(band, _attn_bias(tiles, clip), NEG_INF)


def _attn_exp(qa, kb, bias, key_bias):
    s = _dot_nt(qa, kb) + bias + key_bias
    e = jnp.exp(s - jnp.max(s, axis=-1, keepdims=True))
    return e, jnp.sum(e, axis=-1, keepdims=True)


def _attn_specs():
    n_hp = ATT_H // 2
    q_spec = pl.BlockSpec((ATT_QB, 128), lambda hp, g: (g, hp))

    def win(col0, back):
        return pl.BlockSpec((ATT_QB, 128), lambda hp, g: (jnp.maximum(g - back, 0), col0 + hp))

    kv_specs = [win(n_hp, 2), win(n_hp, 1), win(n_hp, 0), win(2 * n_hp, 2), win(2 * n_hp, 1), win(2 * n_hp, 0)]
    tiles_spec = pl.BlockSpec((2, len(REL_TILES), REL_TILE, REL_TILE), lambda hp, g: (hp, 0, 0, 0))
    clip_spec = pl.BlockSpec((2, 1, 128), lambda hp, g: (hp, 0, 0))
    return q_spec, kv_specs, tiles_spec, clip_spec


def _attn_window(refs, g):
    kb = jnp.concatenate([_bf(r[...]) for r in refs[0:3]], axis=0)
    vb = jnp.concatenate([_bf(r[...]) for r in refs[3:6]], axis=0)
    j = lax.broadcasted_iota(jnp.int32, (1, ATT_KB), 1)
    return kb, vb, jnp.where(j + (g - 2) * ATT_QB >= 0, 0.0, NEG_INF)


def attn_fwd(qkv, tiles, clip, *, name):
    S = qkv.shape[0]
    q_spec, kv_specs, tiles_spec, clip_spec = _attn_specs()

    def body(q_ref, *rest):
        kv_refs, t_ref, c_ref, o_ref, bias = rest[:6], rest[6], rest[7], rest[8], rest[9]
        g = pl.program_id(1)

        @pl.when(g == 0)
        def _():
            for a in range(2):
                bias[a] = _attn_band_bias(t_ref[a], c_ref[a])

        kb, vb, key_bias = _attn_window(kv_refs, g)
        q = q_ref[...].astype(F32)
        out = jnp.zeros((ATT_QB, 2 * ATT_HD), F32)
        for a, lanes in enumerate(_attn_head_lanes()):
            mf = lanes.astype(F32)
            e, l = _attn_exp(_bf(q * (mf * ATT_HD ** -0.5)), kb, bias[a], key_bias)
            out = out + _dot(_bf(e), vb) * (mf * (1.0 / l))
        o_ref[...] = _bf(out)

    return pl.pallas_call(
        body, name=name, grid=(ATT_H // 2, S // ATT_QB), in_specs=[q_spec] + kv_specs + [tiles_spec, clip_spec],
        out_specs=q_spec, out_shape=jax.ShapeDtypeStruct((S, D), BF16),
        scratch_shapes=[pltpu.VMEM((2, ATT_QB, ATT_KB), F32)], compiler_params=_params(2),
    )(*([qkv] * 7), tiles, clip)


def attn_bwd(qkv, do, tiles, clip, *, name):
    S = qkv.shape[0]
    q_spec, kv_specs, tiles_spec, clip_spec = _attn_specs()
    col_spec = pl.BlockSpec((S, 128), lambda hp, g: (0, hp))

    def body(q_ref, *rest):
        kv_refs, t_ref, c_ref, do_ref = rest[:6], rest[6], rest[7], rest[8]
        dq_ref, dk_ref, dv_ref, dt_ref, dc_ref, bias = rest[9:]
        g = pl.program_id(1)

        @pl.when(g == 0)
        def _():
            for a in range(2):
                bias[a] = _attn_band_bias(t_ref[a], c_ref[a])
            dk_ref[...] = jnp.zeros_like(dk_ref)
            dv_ref[...] = jnp.zeros_like(dv_ref)
            dt_ref[...] = jnp.zeros_like(dt_ref)
            dc_ref[...] = jnp.zeros_like(dc_ref)

        kb, vb, key_bias = _attn_window(kv_refs, g)
        q = q_ref[...].astype(F32)
        do = do_ref[...]
        dq = jnp.zeros((ATT_QB, 2 * ATT_HD), F32)
        dkw = jnp.zeros((ATT_KB, 2 * ATT_HD), F32)
        dvw = jnp.zeros((ATT_KB, 2 * ATT_HD), F32)
        for a, lanes in enumerate(_attn_head_lanes()):
            mf = lanes.astype(F32) * ATT_HD ** -0.5
            qa = _bf(q * mf)
            e, l = _attn_exp(qa, kb, bias[a], key_bias)
            p = e * (1.0 / l)
            do_a = jnp.where(lanes, do, jnp.zeros_like(do))
            dp = _dot_nt(do_a, vb)
            ds = p * (dp - jnp.sum(p * dp, axis=-1, keepdims=True))
            ds_b = _bf(ds)
            dq = dq + _dot(ds_b, kb) * mf
            dkw = dkw + _dot_tn(ds_b, qa)
            dvw = dvw + _dot_tn(_bf(p), do_a)
            _attn_bias_grad(ds, dt_ref, dc_ref, a)
        dq_ref[...] = _bf(dq)
        for blk in range(3):
            src = g - 2 + blk

            @pl.when(src >= 0)
            def _(blk=blk, src=src):
                rows = pl.ds(pl.multiple_of(src * ATT_QB, ATT_QB), ATT_QB)
                dk_ref[rows, :] += dkw[blk * ATT_QB:(blk + 1) * ATT_QB]
                dv_ref[rows, :] += dvw[blk * ATT_QB:(blk + 1) * ATT_QB]

    return pl.pallas_call(
        body, name=name, grid=(ATT_H // 2, S // ATT_QB),
        in_specs=[q_spec] + kv_specs + [tiles_spec, clip_spec, q_spec],
        out_specs=[q_spec, col_spec, col_spec, tiles_spec, clip_spec],
        out_shape=[jax.ShapeDtypeStruct((S, D), BF16)] + [jax.ShapeDtypeStruct((S, D), F32)] * 2
        + [jax.ShapeDtypeStruct((ATT_H, len(REL_TILES), REL_TILE, REL_TILE), F32),
           jax.ShapeDtypeStruct((ATT_H, 1, 128), F32)],
        scratch_shapes=[pltpu.VMEM((2, ATT_QB, ATT_KB), F32)], compiler_params=_params(2),
    )(*([qkv] * 7), tiles, clip, do)


def mods_partial(c_all, w_ada, *, name):
    n_l, _, n_c = w_ada.shape

    def body(c_ref, w_ref, o_ref):
        o_ref[...] = _dot(_bf(_silu(c_ref[...])), _bf(w_ref[...]))

    return pl.pallas_call(
        body, name=name, grid=(n_l,),
        in_specs=[pl.BlockSpec((N_DEV, D), lambda l: (0, 0)), pl.BlockSpec((None, D, n_c), lambda l: (l, 0, 0))],
        out_specs=pl.BlockSpec((None, N_DEV, n_c), lambda l: (l, 0, 0)),
        out_shape=jax.ShapeDtypeStruct((n_l, N_DEV, n_c), F32), compiler_params=_params(1),
    )(c_all, w_ada)


def w_ada_grad(c_all, dm, *, name):
    n_l, _, n_c = dm.shape

    def body(c_ref, d_ref, o_ref):
        o_ref[...] = lax.dot_general(_silu(c_ref[...]), d_ref[...], (((0,), (0,)), ((), ())),
                                     precision=HIGHEST, preferred_element_type=F32)

    return pl.pallas_call(
        body, name=name, grid=(n_l,),
        in_specs=[pl.BlockSpec((N_DEV, D), lambda l: (0, 0)), pl.BlockSpec((None, N_DEV, n_c), lambda l: (l, 0, 0))],
        out_specs=pl.BlockSpec((None, D, n_c), lambda l: (l, 0, 0)),
        out_shape=jax.ShapeDtypeStruct((n_l, D, n_c), F32), compiler_params=_params(1),
    )(c_all, dm)


def adamw(w, m, v, gparts, *, block_rows, name):
    R, C = w.shape
    n = gparts.shape[0]

    def body(w_ref, m_ref, v_ref, g_ref, go_ref, d_ref, mo_ref, vo_ref):
        g = g_ref[0].astype(F32)
        for k in range(1, n):
            g = g + g_ref[k].astype(F32)
        m_new = ADAM_B1 * m_ref[...] + (1.0 - ADAM_B1) * g
        v_new = ADAM_B2 * v_ref[...] + (1.0 - ADAM_B2) * (g * g)
        m_hat = m_new / (1.0 - ADAM_B1 ** ADAM_STEP)
        v_hat = v_new / (1.0 - ADAM_B2 ** ADAM_STEP)
        go_ref[...] = g
        d_ref[...] = -ADAM_LR * (m_hat / (jnp.sqrt(v_hat) + ADAM_EPS) + ADAM_WD * w_ref[...])
        mo_ref[...] = m_new
        vo_ref[...] = v_new

    blk = pl.BlockSpec((block_rows, C), lambda i: (i, 0))
    return pl.pallas_call(
        body, name=name, grid=(R // block_rows,),
        in_specs=[blk, blk, blk, pl.BlockSpec((n, block_rows, C), lambda i: (0, i, 0))],
        out_specs=[blk] * 4, out_shape=[jax.ShapeDtypeStruct((R, C), F32)] * 4, compiler_params=_params(1),
    )(w, m, v, gparts)


def adamw_nd(w, m, v, gparts, *, name):
    shape = w.shape
    two = (int(np.prod(shape[:-1])), shape[-1])
    rows = two[0]
    block_rows = rows
    for cand in (512, 256):
        if rows > cand and rows % cand == 0:
            block_rows = cand
            break
    outs = adamw(w.reshape(two), m.reshape(two), v.reshape(two), gparts.reshape((gparts.shape[0],) + two),
                 block_rows=block_rows, name=name)
    return [o.reshape(shape) for o in outs]


def sum_parts(parts, *, name):
    n, R, C = parts.shape

    def body(p_ref, o_ref):
        acc = p_ref[0]
        for k in range(1, n):
            acc = acc + p_ref[k]
        o_ref[...] = acc

    return pl.pallas_call(
        body, name=name, in_specs=[pl.BlockSpec((n, R, C), lambda: (0, 0, 0))],
        out_specs=pl.BlockSpec((R, C), lambda: (0, 0)), out_shape=jax.ShapeDtypeStruct((R, C), F32),
        compiler_params=pltpu.CompilerParams(vmem_limit_bytes=VMEM_LIMIT),
    )(parts)


def _my_place():
    return lax.axis_index("x"), lax.axis_index("y"), lax.axis_index("c")


def _full_shape(kind, shard):
    n_l, rows, cols = shard
    return {"col": (n_l, rows, N_DEV * cols), "row": (n_l, N_DEV * rows, cols), "stk": (N_DEV, n_l, rows, cols)}[kind]


def _slab(ref, kind, dev, shard):
    _, rows, cols = shard
    if kind == "col":
        return ref.at[:, :, pl.ds(pl.multiple_of(dev * cols, 128), cols)]
    if kind == "row":
        return ref.at[:, pl.ds(pl.multiple_of(dev * rows, 8), rows), :]
    return ref.at[dev]


def _hbm_specs(n):
    return [pl.BlockSpec(memory_space=pltpu.HBM)] * n


def all_gather(x_shard, *, name):
    m_per, n = x_shard.shape

    def body(x_ref, out_ref, send_sems, recv_sems, local_sem):
        x, y, c = _my_place()
        me, sibling = (x, y, c), (x, y, 1 - c)
        chips = [(1 - x, y), (x, 1 - y), (1 - x, 1 - y)]

        def rows(px, py, pc):
            return out_ref.at[pl.ds((4 * px + 2 * py + pc) * m_per, m_per), :]

        def copy(k, block, to, src=None):
            return pltpu.make_async_remote_copy(
                src_ref=rows(*block) if src is None else src, dst_ref=rows(*block),
                send_sem=send_sems.at[k], recv_sem=recv_sems.at[k], device_id=to, device_id_type=MESH)

        mine = pltpu.make_async_copy(x_ref, rows(*me), local_sem)
        mine.start()
        first = [copy(0, me, sibling, src=x_ref)]
        first += [copy(1 + j, me, (*chip, c), src=x_ref) for j, chip in enumerate(chips)]
        for cp in first:
            cp.start()
        passed = [copy(4 + j, (*chip, c), sibling) for j, chip in enumerate(chips)]
        for j, chip in enumerate(chips):
            copy(1 + j, (*chip, c), me).wait_recv()
            passed[j].start()
        copy(0, sibling, me).wait_recv()
        for j, chip in enumerate(chips):
            copy(4 + j, (*chip, 1 - c), me).wait_recv()
        for cp in first + passed:
            cp.wait_send()
        mine.wait()

    return pl.pallas_call(
        body, name=name, out_shape=jax.ShapeDtypeStruct((N_DEV * m_per, n), x_shard.dtype),
        in_specs=[pl.BlockSpec(memory_space=pltpu.VMEM)], out_specs=pl.BlockSpec(memory_space=pltpu.VMEM),
        scratch_shapes=[pltpu.SemaphoreType.DMA((7,)), pltpu.SemaphoreType.DMA((7,)), pltpu.SemaphoreType.DMA],
        compiler_params=pltpu.CompilerParams(vmem_limit_bytes=VMEM_LIMIT),
    )(x_shard)


def gather_weights(shards, kinds, *, name):
    n_t = len(shards)
    shapes = [s.shape for s in shards]

    def body(*refs):
        x_refs, out_refs = refs[:n_t], refs[n_t:2 * n_t]
        send_sems, recv_sems, local_sems = refs[2 * n_t:]
        x, y, c = _my_place()
        me, sibling = (x, y, c), (x, y, 1 - c)
        chips = [(1 - x, y), (x, 1 - y), (1 - x, 1 - y)]

        def slab(t, px, py, pc):
            return _slab(out_refs[t], kinds[t], 4 * px + 2 * py + pc, shapes[t])

        def copy(t, k, block, to, src=None):
            return pltpu.make_async_remote_copy(
                src_ref=slab(t, *block) if src is None else src, dst_ref=slab(t, *block),
                send_sem=send_sems.at[7 * t + k], recv_sem=recv_sems.at[7 * t + k], device_id=to,
                device_id_type=MESH)

        mine = [pltpu.make_async_copy(x_refs[t], slab(t, *me), local_sems.at[t]) for t in range(n_t)]
        first = []
        for t in range(n_t):
            first.append(copy(t, 0, me, sibling, src=x_refs[t]))
            first += [copy(t, 1 + j, me, (*chip, c), src=x_refs[t]) for j, chip in enumerate(chips)]
        for cp in mine + first:
            cp.start()
        passed = []
        for j, chip in enumerate(chips):
            for t in range(n_t):
                copy(t, 1 + j, (*chip, c), me).wait_recv()
                passed.append(copy(t, 4 + j, (*chip, c), sibling))
                passed[-1].start()
        for t in range(n_t):
            copy(t, 0, sibling, me).wait_recv()
        for j, chip in enumerate(chips):
            for t in range(n_t):
                copy(t, 4 + j, (*chip, 1 - c), me).wait_recv()
        for cp in first + passed:
            cp.wait_send()
        for cp in mine:
            cp.wait()

    return pl.pallas_call(
        body, name=name,
        out_shape=[jax.ShapeDtypeStruct(_full_shape(k, s.shape), s.dtype) for k, s in zip(kinds, shards)],
        in_specs=_hbm_specs(n_t), out_specs=_hbm_specs(n_t),
        scratch_shapes=[pltpu.SemaphoreType.DMA((7 * n_t,)), pltpu.SemaphoreType.DMA((7 * n_t,)),
                        pltpu.SemaphoreType.DMA((n_t,))],
    )(*shards)


def sibling_exchange(grads, kinds, shapes, *, name):
    n_t = len(grads)

    def body(*refs):
        g_refs, out_refs = refs[:n_t], refs[n_t:2 * n_t]
        send_sems, recv_sems = refs[2 * n_t:]
        x, y, c = _my_place()
        copies = [pltpu.make_async_remote_copy(
            src_ref=_slab(g_refs[t], kinds[t], 2 * chip + (1 - c), shapes[t]), dst_ref=out_refs[t].at[chip],
            send_sem=send_sems.at[4 * t + chip], recv_sem=recv_sems.at[4 * t + chip], device_id=(x, y, 1 - c),
            device_id_type=MESH) for t in range(n_t) for chip in range(4)]
        for cp in copies:
            cp.start()
        for cp in copies:
            cp.wait()

    return pl.pallas_call(
        body, name=name, out_shape=[jax.ShapeDtypeStruct((4,) + tuple(s), F32) for s in shapes],
        in_specs=_hbm_specs(n_t), out_specs=_hbm_specs(n_t),
        scratch_shapes=[pltpu.SemaphoreType.DMA((4 * n_t,)), pltpu.SemaphoreType.DMA((4 * n_t,))],
    )(*grads)


def pair_sum_bf16(grad, got, kind, core, *, name):
    _, n_l, rows, cols = got.shape
    tr = min(rows, 256)
    n_b = rows // tr

    def body(c_ref, g_ref, got_ref, o_ref):
        o_ref[...] = _bf(g_ref[...] + got_ref[...])

    if kind == "col":
        g_spec = pl.BlockSpec((None, tr, cols), lambda ch, l, i, c: (l, i, 2 * ch + c[0]))
    elif kind == "row":
        g_spec = pl.BlockSpec((None, tr, cols), lambda ch, l, i, c: (l, (2 * ch + c[0]) * n_b + i, 0))
    else:
        g_spec = pl.BlockSpec((None, None, tr, cols), lambda ch, l, i, c: (2 * ch + c[0], l, i, 0))
    part_spec = pl.BlockSpec((None, None, tr, cols), lambda ch, l, i, c: (ch, l, i, 0))
    return pl.pallas_call(
        body, name=name,
        grid_spec=pltpu.PrefetchScalarGridSpec(num_scalar_prefetch=1, grid=(4, n_l, n_b),
                                               in_specs=[g_spec, part_spec], out_specs=part_spec),
        out_shape=jax.ShapeDtypeStruct(got.shape, BF16), compiler_params=_params(3),
    )(core, grad, got)


def chip_exchange(parts, *, name):
    n_t = len(parts)

    def body(*refs):
        p_refs, out_refs = refs[:n_t], refs[n_t:2 * n_t]
        send_sems, recv_sems, local_sems = refs[2 * n_t:]
        x, y, c = _my_place()
        mine = 2 * x + y
        chips = [(1 - x, y), (x, 1 - y), (1 - x, 1 - y)]
        local = [pltpu.make_async_copy(p_refs[t].at[mine], out_refs[t].at[mine], local_sems.at[t])
                 for t in range(n_t)]
        sends = [pltpu.make_async_remote_copy(
            src_ref=p_refs[t].at[2 * cx + cy], dst_ref=out_refs[t].at[mine], send_sem=send_sems.at[3 * t + k],
            recv_sem=recv_sems.at[3 * t + k], device_id=(cx, cy, c), device_id_type=MESH)
            for t in range(n_t) for k, (cx, cy) in enumerate(chips)]
        for cp in local + sends:
            cp.start()
        for t in range(n_t):
            for k, (cx, cy) in enumerate(chips):
                pltpu.make_async_remote_copy(
                    src_ref=p_refs[t].at[mine], dst_ref=out_refs[t].at[2 * cx + cy], send_sem=send_sems.at[3 * t + k],
                    recv_sem=recv_sems.at[3 * t + k], device_id=(cx, cy, c), device_id_type=MESH).wait_recv()
        for cp in sends:
            cp.wait_send()
        for cp in local:
            cp.wait()

    return pl.pallas_call(
        body, name=name, out_shape=[jax.ShapeDtypeStruct(p.shape, p.dtype) for p in parts],
        in_specs=_hbm_specs(n_t), out_specs=_hbm_specs(n_t),
        scratch_shapes=[pltpu.SemaphoreType.DMA((3 * n_t,)), pltpu.SemaphoreType.DMA((3 * n_t,)),
                        pltpu.SemaphoreType.DMA((n_t,))],
    )(*parts)


BIG = ("gla_w_in", "gla_w_out", "att_w_in", "att_w_out", "ff_w1", "ff_w2")
KIND = {"gla_w_in": "stk", "gla_w_out": "row", "att_w_in": "col", "att_w_out": "row", "ff_w1": "col", "ff_w2": "row"}


def _pack_small(arrs):
    parts = []
    for a in arrs:
        f = a.reshape(-1)
        parts.append(jnp.pad(f, (0, -f.shape[0] % 128)))
    flat = jnp.concatenate(parts)
    flat = jnp.pad(flat, (0, -flat.shape[0] % 1024))
    return flat.reshape(-1, 128)


def _unpack_small(packed, shapes):
    flat = packed.reshape(packed.shape[:-2] + (-1,))
    out, off = [], 0
    for shp in shapes:
        n = int(np.prod(shp))
        out.append(flat[..., off:off + n].reshape(packed.shape[:-2] + tuple(shp)))
        off += n + (-n % 128)
    return out


def _vec(a):
    return a.reshape(1, -1)


def _trunk(x, target, mods, wts, sm):
    n_gla = 0
    n_att = 0
    rel_idx = _rel_index()
    saved = []
    for i in range(DEPTH):
        sh1, sc1, g1, sh2, sc2, g2 = [mods[i, k:k + 1] for k in range(6)]
        rec = {"x0": x}
        if i % 2 == 0:
            j = n_gla
            n_gla += 1
            w2p = jnp.pad(sm["gla_w_gk2"][j], ((0, 128 - GLA_RANK), (0, 0)))
            bgk, gn = _vec(sm["gla_b_gk"][j]), _vec(sm["gla_g_norm"][j])
            proj = mm_nn(x, wts["gla_w_in"], j, pro="mod", p1=sc1, p2=sh1, tm=1024, tn=640, name=f"gla_proj_{i}")
            og, states = gla_fwd(proj, w2p, bgk, gn, name=f"gla_core_{i}")
            y = mm_nn(og, wts["gla_w_out"], j, tm=1024, tn=1024, name=f"gla_out_{i}")
            rec.update(kind="gla", j=j, w2p=w2p, bgk=bgk, gn=gn, proj=proj, og=og, states=states)
        else:
            j = n_att
            n_att += 1
            rel = sm["att_rel_bias"][j]
            rel_pad = jnp.pad(rel, ((0, 0), (0, REL_PAD - N_REL)), constant_values=NEG_INF)
            tiles = rel_bias_tiles(rel_pad, rel_idx, name=f"att_bias_{i}")
            tiles = tiles.reshape(ATT_H, len(REL_TILES), REL_TILE, REL_TILE)
            clip = jnp.broadcast_to(rel[:, 2 * MAX_REL][:, None, None], (ATT_H, 1, 128))
            qkv = mm_nn(x, wts["att_w_in"], j, pro="mod", p1=sc1, p2=sh1, bias=_vec(sm["att_b_in"][j]),
                        out_dtype=BF16, tm=1024, tn=1024, name=f"att_proj_{i}")
            o = attn_fwd(qkv, tiles, clip, name=f"att_core_{i}")
            y = mm_nn(o, wts["att_w_out"], j, tm=1024, tn=1024, name=f"att_out_{i}")
            rec.update(kind="att", j=j, tiles=tiles, clip=clip, qkv=qkv, o=o)
        x1 = ln_fwd(x, y, g1, _vec(sm["ln_g"][i, 0]), _vec(sm["ln_b"][i, 0]), name=f"ln_mix_{i}")
        h = mm_nn(x1, wts["ff_w1"], i, pro="mod", p1=sc2, p2=sh2, out_dtype=BF16, tm=2048, tn=1024,
                  name=f"ff_up_{i}")
        y2 = mm_nn_ksplit(h, wts["ff_w2"], i, pro="relu2", tm=1024, tk=1024, name=f"ff_down_{i}")
        x2 = ln_fwd(x1, y2, g2, _vec(sm["ln_g"][i, 1]), _vec(sm["ln_b"][i, 1]), name=f"ln_ff_{i}")
        rec.update(y=y, x1=x1, h=h, y2=y2)
        saved.append(rec)
        x = x2

    dy, loss = loss_head(x, target, name="loss_head")

    gw = {n: None for n in BIG}
    gw_shape = {n: wts[n].shape for n in BIG}

    def wgrad(weight, layer, a, d, *, tn, tk=1024, tm=512, col_block0=0, **kw):
        gw[weight] = mm_tn(a, d, tk=tk, tn=tn, tm=tm, out_buf=gw[weight], out_shape=gw_shape[weight],
                           layer=layer, col_block0=col_block0, **kw)

    gs = {"ln_g": [[None, None] for _ in range(DEPTH)], "ln_b": [[None, None] for _ in range(DEPTH)],
          "gla_w_gk2": [None] * 2, "gla_b_gk": [None] * 2, "gla_g_norm": [None] * 2, "att_b_in": [None] * 2,
          "att_rel_bias": [None] * 2}
    dmods = [[None] * 6 for _ in range(DEPTH)]
    nxt = None
    nxt_slot = None
    for i in reversed(range(DEPTH)):
        rec = saved[i]
        sh1, sc1, g1, sh2, sc2, g2 = [mods[i, k:k + 1] for k in range(6)]
        x0, x1 = rec["x0"], rec["x1"]
        if nxt is None:
            dz2, acc = ln_bwd(x1, rec["y2"], g2, _vec(sm["ln_g"][i, 1]), dout=dy, name=f"ln_ff_bwd_{i}")
        else:
            dz2, acc = ln_bwd(x1, rec["y2"], g2, _vec(sm["ln_g"][i, 1]), nxt=nxt, name=f"ln_ff_bwd_{i}")
            dmods[nxt_slot[0]][nxt_slot[1]] = acc[3]
            dmods[nxt_slot[0]][nxt_slot[2]] = acc[4]
        gs["ln_g"][i][1], gs["ln_b"][i][1], dmods[i][5] = acc[0], acc[1], acc[2]
        dh = mm_nt([dz2], wts["ff_w2"], i, pro="scale", p1=g2, epi_h=rec["h"], out_dtype=BF16, tm=2048, tn=512,
                   name=f"ff_down_bwd_{i}")
        wgrad("ff_w2", i, rec["h"], dz2, pro="relu2", dscale=g2, tk=2048, tn=1024, name=f"ff_w2_grad_{i}")
        du2 = mm_nt([dh], wts["ff_w1"], i, tm=1024, tn=1024, name=f"ff_up_bwd_{i}")
        wgrad("ff_w1", i, x1, dh, pro="mod", p1=sc2, p2=sh2, tn=2048, name=f"ff_w1_grad_{i}")
        dz1, acc = ln_bwd(x0, rec["y"], g1, _vec(sm["ln_g"][i, 0]), nxt=(dz2, du2, sc2, x1), name=f"ln_mix_bwd_{i}")
        dmods[i][4], dmods[i][3] = acc[3], acc[4]
        gs["ln_g"][i][0], gs["ln_b"][i][0], dmods[i][2] = acc[0], acc[1], acc[2]
        j = rec["j"]
        if rec["kind"] == "gla":
            dog = mm_nt([dz1], wts["gla_w_out"], j, pro="scale", p1=g1, tm=1024, tn=1024, name=f"gla_out_bwd_{i}")
            wgrad("gla_w_out", j, rec["og"], dz1, dscale=g1, tn=1024, name=f"gla_wout_grad_{i}")
            dproj, dw2p, dbgk, dgn = gla_bwd(rec["proj"], dog, rec["states"], rec["w2p"], rec["bgk"], rec["gn"],
                                             name=f"gla_core_bwd_{i}")
            gs["gla_w_gk2"][j], gs["gla_b_gk"][j], gs["gla_g_norm"][j] = dw2p[:GLA_RANK], dbgk[0], dgn[0]
            du1 = mm_nt([dproj], wts["gla_w_in"], j, tm=1024, tn=1024, name=f"gla_proj_bwd_{i}")
            wgrad("gla_w_in", j, x0, dproj, pro="mod", p1=sc1, p2=sh1, tk=512, tn=GLA_INP, name=f"gla_win_grad_{i}")
        else:
            do = mm_nt([dz1], wts["att_w_out"], j, pro="scale", p1=g1, out_dtype=BF16, tm=1024, tn=1024,
                       name=f"att_out_bwd_{i}")
            wgrad("att_w_out", j, rec["o"], dz1, dscale=g1, tn=1024, name=f"att_wout_grad_{i}")
            dq, dk, dv, dtiles, dclip = attn_bwd(rec["qkv"], do, rec["tiles"], rec["clip"], name=f"att_core_bwd_{i}")
            drel = rel_bias_grad(dtiles.reshape(ATT_H, -1), dclip.reshape(ATT_H, 128), rel_idx,
                                 name=f"att_bias_grad_{i}")
            gs["att_rel_bias"][j] = drel[:, :N_REL]
            gs["att_b_in"][j] = jnp.concatenate(
                [colsum(t, name=f"att_bin_grad_{i}_{n}")[0] for n, t in enumerate((dq, dk, dv))])
            du1 = mm_nt([dq, dk, dv], wts["att_w_in"], j, tm=512, tn=1024, name=f"att_proj_bwd_{i}")
            for n, t in enumerate((dq, dk, dv)):
                wgrad("att_w_in", j, x0, t, pro="mod", p1=sc1, p2=sh1, tn=1024, col_block0=n,
                      name=f"att_win_grad_{i}_{n}")
        nxt = (dz1, du1, sc1, x0)
        nxt_slot = (i, 1, 0)
    dx, acc = combine_final(nxt[0], nxt[1], nxt[2], nxt[3], name="grad_x")
    dmods[0][1], dmods[0][0] = acc[3], acc[4]
    dmods = jnp.stack([jnp.stack(r) for r in dmods])
    gs = {k: jnp.stack([jnp.stack(r) if isinstance(r, list) else r for r in v]) for k, v in gs.items()}
    return loss, dx, dmods, gw, gs


WEIGHTS = ("w_ada", "b_ada", "ln_g", "ln_b", "gla_w_in", "gla_w_gk2", "gla_b_gk", "gla_g_norm", "gla_w_out",
           "att_w_in", "att_b_in", "att_rel_bias", "att_w_out", "ff_w1", "ff_w2")
SMALL_SHARDED = {"ln_g": 2, "ln_b": 2, "gla_w_gk2": 2, "gla_g_norm": 2, "att_b_in": 1}
SMALL_GRADS = ("ln_g", "ln_b", "gla_w_gk2", "gla_b_gk", "gla_g_norm", "att_b_in", "att_rel_bias")


def kernel(x, c, w_ada, b_ada, ln_g, ln_b, gla_w_in, gla_w_gk2, gla_b_gk, gla_g_norm, gla_w_out, att_w_in, att_b_in, att_rel_bias, att_w_out, ff_w1, ff_w2, loss_target, m_w_ada, m_b_ada, m_ln_g, m_ln_b, m_gla_w_in, m_gla_w_gk2, m_gla_b_gk, m_gla_g_norm, m_gla_w_out, m_att_w_in, m_att_b_in, m_att_rel_bias, m_att_w_out, m_ff_w1, m_ff_w2, v_w_ada, v_b_ada, v_ln_g, v_ln_b, v_gla_w_in, v_gla_w_gk2, v_gla_b_gk, v_gla_g_norm, v_gla_w_out, v_att_w_in, v_att_b_in, v_att_rel_bias, v_att_w_out, v_ff_w1, v_ff_w2):
    w = dict(w_ada=w_ada, b_ada=b_ada, ln_g=ln_g, ln_b=ln_b, gla_w_in=gla_w_in, gla_w_gk2=gla_w_gk2,
             gla_b_gk=gla_b_gk, gla_g_norm=gla_g_norm, gla_w_out=gla_w_out, att_w_in=att_w_in, att_b_in=att_b_in,
             att_rel_bias=att_rel_bias, att_w_out=att_w_out, ff_w1=ff_w1, ff_w2=ff_w2)
    m = dict(w_ada=m_w_ada, b_ada=m_b_ada, ln_g=m_ln_g, ln_b=m_ln_b, gla_w_in=m_gla_w_in, gla_w_gk2=m_gla_w_gk2,
             gla_b_gk=m_gla_b_gk, gla_g_norm=m_gla_g_norm, gla_w_out=m_gla_w_out, att_w_in=m_att_w_in,
             att_b_in=m_att_b_in, att_rel_bias=m_att_rel_bias, att_w_out=m_att_w_out, ff_w1=m_ff_w1, ff_w2=m_ff_w2)
    v = dict(w_ada=v_w_ada, b_ada=v_b_ada, ln_g=v_ln_g, ln_b=v_ln_b, gla_w_in=v_gla_w_in, gla_w_gk2=v_gla_w_gk2,
             gla_b_gk=v_gla_b_gk, gla_g_norm=v_gla_g_norm, gla_w_out=v_gla_w_out, att_w_in=v_att_w_in,
             att_b_in=v_att_b_in, att_rel_bias=v_att_rel_bias, att_w_out=v_att_w_out, ff_w1=v_ff_w1, ff_w2=v_ff_w2)
    xi, yi, ci = _my_place()
    me = 4 * xi + 2 * yi + ci

    small_names = tuple(SMALL_SHARDED)
    small_in = _pack_small([c] + [w[n] for n in small_names])
    small_all = all_gather(small_in, name="gather_small").reshape(N_DEV, -1, 128)
    parts = _unpack_small(small_all, [c.shape] + [w[n].shape for n in small_names])
    c_all = parts[0].reshape(N_DEV, D)
    sm = {"gla_b_gk": gla_b_gk, "att_rel_bias": att_rel_bias}
    for n, p in zip(small_names, parts[1:]):
        ax = SMALL_SHARDED[n]
        sm[n] = jnp.moveaxis(p, 0, ax).reshape(p.shape[1:ax + 1] + (N_DEV * p.shape[ax + 1],) + p.shape[ax + 2:])

    n_ada = w_ada.shape[2]
    mp = mods_partial(c_all, w_ada, name="mods_partial")
    mp_all = all_gather(mp.reshape(DEPTH * N_DEV, n_ada), name="gather_mods")
    mp_all = mp_all.reshape(N_DEV, DEPTH, N_DEV, n_ada)
    mods = lax.dynamic_index_in_dim(mp_all, me, axis=2, keepdims=False)
    mods = mods.transpose(1, 0, 2).reshape(DEPTH, 6 * D) + b_ada
    mods = mods.reshape(DEPTH, 6, D)

    kinds = [KIND[n] for n in BIG]
    shard_shapes = [w[n].shape for n in BIG]
    wts = dict(zip(BIG, gather_weights([w[n].astype(BF16) for n in BIG], kinds, name="gather_weights")))
    stacked = wts["gla_w_in"]
    w_in = stacked.transpose(1, 2, 0, 3).reshape(stacked.shape[1], D, GLA_IN)
    wts["gla_w_in"] = jnp.pad(w_in, ((0, 0), (0, 0), (0, GLA_INP - GLA_IN)))

    loss, dx, dmods, gw, gs = _trunk(x.reshape(x.shape[1:]), loss_target.reshape(x.shape[1:]), mods, wts, sm)
    loss = lax.psum(loss[0, 0], ("x", "y", "c"))

    n_l, _, n_in = w["gla_w_in"].shape
    gw["gla_w_in"] = gw["gla_w_in"][:, :, :GLA_IN].reshape(n_l, D, N_DEV, n_in).transpose(2, 0, 1, 3)
    grads = [gw[n] for n in BIG]
    got = sibling_exchange(grads, kinds, shard_shapes, name="grads_sibling")
    core = ci.astype(jnp.int32).reshape(1)
    pair = [pair_sum_bf16(g, r, k, core, name=f"grads_pair_sum_{n}") for n, g, r, k in zip(BIG, grads, got, kinds)]
    chip_parts = chip_exchange(pair, name="grads_chips")
    results = {}
    for n, p in zip(BIG, chip_parts):
        results[n] = adamw_nd(w[n], m[n], v[n], p, name=f"adamw_{n}")

    dm_flat = dmods.reshape(DEPTH, 6 * D)
    small_g = [dm_flat] + [gs[n].reshape(sm[n].shape) for n in SMALL_GRADS]
    small_shapes = [a.shape for a in small_g]
    sg_all = all_gather(_pack_small(small_g), name="gather_small_grads").reshape(N_DEV, -1, 128)
    summed = _unpack_small(sum_parts(sg_all, name="sum_small_grads"), small_shapes)
    g_full = dict(zip(("b_ada",) + SMALL_GRADS, summed))
    dm_all = _unpack_small(sg_all, small_shapes)[0]
    dm_mine = lax.dynamic_slice_in_dim(dm_all, me * n_ada, n_ada, axis=2).transpose(1, 0, 2)
    g_w_ada = w_ada_grad(c_all, dm_mine, name="w_ada_grad")

    results["w_ada"] = adamw_nd(w_ada, m_w_ada, v_w_ada, g_w_ada[None], name="adamw_w_ada")
    for n in ("b_ada",) + SMALL_GRADS:
        g = g_full[n]
        if n in SMALL_SHARDED:
            ax = SMALL_SHARDED[n]
            width = w[n].shape[ax]
            g = lax.dynamic_slice_in_dim(g, me * width, width, axis=ax)
        results[n] = adamw_nd(w[n], m[n], v[n], g[None], name=f"adamw_{n}")

    out = [loss, dx[None]]
    for k in range(4):
        out += [results[n][k] for n in WEIGHTS]
    return tuple(out)
```

```python
import numpy as np
import jax
import jax.numpy as jnp
from jax import lax
from jax.experimental import pallas as pl
from jax.experimental.pallas import tpu as pltpu

F32 = jnp.float32
BF16 = jnp.bfloat16
HIGHEST = lax.Precision.HIGHEST
MESH = pl.DeviceIdType.MESH

N_DEV = 8
D = 1024
DEPTH = 4
CHUNK = 64
ALPHA = (2.0 * DEPTH) ** 0.25
LN_EPS = 1e-5
RMS_EPS = 1e-6
NEG_INF = -1e30

GLA_H = 4
GLA_DKH = 128
GLA_DVH = 256
GLA_DK = GLA_H * GLA_DKH
GLA_DV = GLA_H * GLA_DVH
GLA_RANK = 16
GLA_IN = 2 * GLA_DK + 2 * GLA_DV + GLA_RANK
GLA_INP = 3200
GLA_TAU_INV = 1.0 / 16.0

ATT_H = 16
ATT_HD = 64
ATT_QB = 256
ATT_KB = 3 * ATT_QB
LEFT = 8 * CHUNK
MAX_REL = 128
N_REL = 2 * MAX_REL + 1
REL_PAD = 384
REL_TILE = 128
REL_TILES = (3, 4)
D_FF = 4 * D

ADAM_LR = 0.001
ADAM_B1 = 0.9
ADAM_B2 = 0.999
ADAM_EPS = 1e-08
ADAM_WD = 0.01
ADAM_STEP = 10

VMEM_LIMIT = 48 * 1024 * 1024


def _params(n_axes):
    return pltpu.CompilerParams(dimension_semantics=("arbitrary",) * n_axes, vmem_limit_bytes=VMEM_LIMIT)


def _dot(a, b):
    return jnp.dot(a, b, preferred_element_type=F32)


def _dot_nt(a, b):
    return lax.dot_general(a, b, (((1,), (1,)), ((), ())), preferred_element_type=F32)


def _dot_tn(a, b):
    return lax.dot_general(a, b, (((0,), (0,)), ((), ())), preferred_element_type=F32)


def _bf(a):
    return a.astype(BF16)


def _prologue(kind, a, p1=None, p2=None):
    if kind == "mod":
        return a * (1.0 + p1) + p2
    if kind == "scale":
        return a * (1.0 + p1)
    if kind == "relu2":
        r = jnp.maximum(a, 0.0)
        return r * r
    return a


class Hosted:
    def __init__(self, inputs, out_shapes, sems, first, last):
        self.inputs, self.out_shapes, self.sems, self.first, self.last = inputs, out_shapes, sems, first, last


def _hbm_specs(n):
    return [pl.BlockSpec(memory_space=pltpu.HBM)] * n


def _call_hosting(body, comm, *, first, last, in_specs, out_specs, out_shape, scratch_shapes, args, **kw):
    if comm is None:
        return pl.pallas_call(body, in_specs=in_specs, out_specs=out_specs, out_shape=out_shape,
                              scratch_shapes=scratch_shapes, **kw)(*args), []
    n_in, n_out, n_scr = len(in_specs), len(out_specs), len(scratch_shapes)
    n_ci, n_co = len(comm.inputs), len(comm.out_shapes)

    def hosting(*refs):
        ins, ci = refs[:n_in], refs[n_in:n_in + n_ci]
        k = n_in + n_ci
        outs, co = refs[k:k + n_out], refs[k + n_out:k + n_out + n_co]
        k += n_out + n_co
        scr, cs = refs[k:k + n_scr], refs[k + n_scr:]

        @pl.when(first())
        def _():
            comm.first(ci, co, cs)

        body(*ins, *outs, *scr)

        @pl.when(last())
        def _():
            comm.last(ci, co, cs)

    res = pl.pallas_call(
        hosting, in_specs=list(in_specs) + _hbm_specs(n_ci), out_specs=list(out_specs) + _hbm_specs(n_co),
        out_shape=list(out_shape) + list(comm.out_shapes), scratch_shapes=list(scratch_shapes) + list(comm.sems),
        **kw)(*args, *comm.inputs)
    return res[:n_out], res[n_out:]


def run_hosted(comm, *, name):
    n_i, n_o = len(comm.inputs), len(comm.out_shapes)

    def body(*refs):
        ins, outs, sems = refs[:n_i], refs[n_i:n_i + n_o], refs[n_i + n_o:]
        comm.first(ins, outs, sems)
        comm.last(ins, outs, sems)

    return pl.pallas_call(body, name=name, out_shape=list(comm.out_shapes), in_specs=_hbm_specs(n_i),
                          out_specs=_hbm_specs(n_o), scratch_shapes=list(comm.sems))(*comm.inputs)


def mm_nn(a, b, layer, *, pro=None, p1=None, p2=None, bias=None, out_dtype=F32, tm, tn, name):
    M, K = a.shape
    N = b.shape[2]
    tm = min(tm, M)
    n_p = {"mod": 2, "scale": 1}.get(pro, 0)
    has_bias = bias is not None
    direct = pro is None and a.dtype == BF16

    def body(*refs):
        a_ref, b_ref = refs[0], refs[1]
        p_refs = refs[2:2 + n_p]
        bias_ref = refs[2 + n_p] if has_bias else None
        if direct:
            o_ref = refs[-1]
            lhs = a_ref[...]
        else:
            o_ref, abf = refs[-2], refs[-1]

            @pl.when(pl.program_id(1) == 0)
            def _():
                abf[...] = _bf(_prologue(pro, a_ref[...].astype(F32), *[r[...] for r in p_refs]))

            lhs = abf[...]
        acc = _dot(lhs, b_ref[...])
        if has_bias:
            acc = acc + bias_ref[...]
        o_ref[...] = acc.astype(out_dtype)

    in_specs = [pl.BlockSpec((tm, K), lambda i, j: (i, 0)), pl.BlockSpec((None, K, tn), lambda i, j: (layer, 0, j))]
    args = [a, b]
    for p in (p1, p2)[:n_p]:
        in_specs.append(pl.BlockSpec((1, K), lambda i, j: (0, 0)))
        args.append(p)
    if has_bias:
        in_specs.append(pl.BlockSpec((1, tn), lambda i, j: (0, j)))
        args.append(bias)
    return pl.pallas_call(
        body, name=name, grid=(M // tm, N // tn), in_specs=in_specs,
        out_specs=pl.BlockSpec((tm, tn), lambda i, j: (i, j)),
        out_shape=jax.ShapeDtypeStruct((M, N), out_dtype),
        scratch_shapes=[] if direct else [pltpu.VMEM((tm, K), BF16)], compiler_params=_params(2),
    )(*args)


def mm_nn_ksplit(a, b, layer, *, pro=None, tm, tk, name):
    M, K = a.shape
    N = b.shape[2]
    tm = min(tm, M)

    def body(a_ref, b_ref, o_ref):
        part = _dot(_bf(_prologue(pro, a_ref[...].astype(F32))), b_ref[...])

        @pl.when(pl.program_id(1) == 0)
        def _():
            o_ref[...] = part

        @pl.when(pl.program_id(1) > 0)
        def _():
            o_ref[...] += part

    return pl.pallas_call(
        body, name=name, grid=(M // tm, K // tk),
        in_specs=[pl.BlockSpec((tm, tk), lambda i, k: (i, k)), pl.BlockSpec((None, tk, N), lambda i, k: (layer, k, 0))],
        out_specs=pl.BlockSpec((tm, N), lambda i, k: (i, 0)),
        out_shape=jax.ShapeDtypeStruct((M, N), F32), compiler_params=_params(2),
    )(a, b)


def mm_nt(a_parts, w, layer, *, pro=None, p1=None, epi_h=None, out_dtype=F32, tm, tn, name):
    M = a_parts[0].shape[0]
    tm = min(tm, M)
    widths = [p.shape[1] for p in a_parts]
    Nw = sum(widths)
    Kw = w.shape[1]
    n_a = len(a_parts)
    has_p = pro == "scale"
    has_h = epi_h is not None
    direct = n_a == 1 and not has_p and a_parts[0].dtype == BF16

    def body(*refs):
        a_refs = refs[:n_a]
        w_ref = refs[n_a]
        k = n_a + 1
        p_ref = refs[k] if has_p else None
        k += int(has_p)
        h_ref = refs[k] if has_h else None
        if direct:
            o_ref = refs[-1]
            lhs = a_refs[0][...]
        else:
            o_ref, abf = refs[-2], refs[-1]

            @pl.when(pl.program_id(1) == 0)
            def _():
                off = 0
                for r, wd in zip(a_refs, widths):
                    av = r[...]
                    if has_p:
                        av = av.astype(F32) * (1.0 + p_ref[...])
                    abf[:, off:off + wd] = _bf(av)
                    off += wd

            lhs = abf[...]
        acc = _dot_nt(lhs, w_ref[...])
        if has_h:
            acc = acc * (2.0 * jnp.maximum(h_ref[...], 0.0))
        o_ref[...] = acc.astype(out_dtype)

    in_specs = [pl.BlockSpec((tm, wd), lambda i, j: (i, 0)) for wd in widths]
    in_specs.append(pl.BlockSpec((None, tn, Nw), lambda i, j: (layer, j, 0)))
    args = list(a_parts) + [w]
    if has_p:
        in_specs.append(pl.BlockSpec((1, Nw), lambda i, j: (0, 0)))
        args.append(p1)
    if has_h:
        in_specs.append(pl.BlockSpec((tm, tn), lambda i, j: (i, j)))
        args.append(epi_h)
    return pl.pallas_call(
        body, name=name, grid=(M // tm, Kw // tn), in_specs=in_specs,
        out_specs=pl.BlockSpec((tm, tn), lambda i, j: (i, j)),
        out_shape=jax.ShapeDtypeStruct((M, Kw), out_dtype),
        scratch_shapes=[] if direct else [pltpu.VMEM((tm, Nw), BF16)], compiler_params=_params(2),
    )(*args)


def mm_tn(a, d, *, pro=None, p1=None, p2=None, dscale=None, tk, tn, tm, out_buf, out_shape, col_block0=0, name):
    M, Kf = a.shape
    N = d.shape[1]
    n_p = {"mod": 2}.get(pro, 0)
    has_ds = dscale is not None
    has_buf = out_buf is not None
    n_m = M // tm

    def body(*refs):
        a_ref, d_ref = refs[0], refs[1]
        p_refs = refs[2:2 + n_p]
        ds_ref = refs[2 + n_p] if has_ds else None
        o_ref, acc = refs[-2], refs[-1]
        m = pl.program_id(2)

        av = a_ref[...]
        if pro is not None:
            av = _prologue(pro, av.astype(F32), *[r[...] for r in p_refs])
        dv = d_ref[...]
        if has_ds:
            dv = dv.astype(F32) * (1.0 + ds_ref[...])
        part = _dot_tn(_bf(av), _bf(dv))

        @pl.when(m == 0)
        def _():
            acc[...] = part

        @pl.when(m > 0)
        def _():
            acc[...] += part

        @pl.when(m == n_m - 1)
        def _():
            o_ref[...] = _bf(acc[...])

    in_specs = [pl.BlockSpec((tm, tk), lambda i, j, m: (m, i)), pl.BlockSpec((tm, tn), lambda i, j, m: (m, j))]
    args = [a, d]
    for p in (p1, p2)[:n_p]:
        in_specs.append(pl.BlockSpec((1, tk), lambda i, j, m: (0, i)))
        args.append(p)
    if has_ds:
        in_specs.append(pl.BlockSpec((1, tn), lambda i, j, m: (0, j)))
        args.append(dscale)
    aliases = {}
    if has_buf:
        in_specs.append(pl.BlockSpec(memory_space=pl.ANY))
        args.append(out_buf)
        aliases = {len(args) - 1: 0}
    return pl.pallas_call(
        body, name=name, grid=(Kf // tk, N // tn, n_m), in_specs=in_specs,
        out_specs=pl.BlockSpec((None, tk, tn), lambda i, j, m: (0, i, col_block0 + j)),
        out_shape=jax.ShapeDtypeStruct(out_shape, BF16), input_output_aliases=aliases,
        scratch_shapes=[pltpu.VMEM((tk, tn), F32)], compiler_params=_params(3),
    )(*args)


ROW_BLOCK = 512
ACC_ROWS = 8


def _ln_stats(z):
    mu = jnp.mean(z, axis=-1, keepdims=True)
    zc = z - mu
    var = jnp.mean(zc * zc, axis=-1, keepdims=True)
    return zc, lax.rsqrt(var + LN_EPS)


def ln_fwd(x, y, gate, lng, lnb, *, name):
    S = x.shape[0]

    def body(x_ref, y_ref, gt_ref, g_ref, b_ref, o_ref):
        z = ALPHA * x_ref[...] + (1.0 + gt_ref[...]) * y_ref[...]
        zc, rstd = _ln_stats(z)
        o_ref[...] = (zc * rstd) * g_ref[...] + b_ref[...]

    row = pl.BlockSpec((ROW_BLOCK, D), lambda i: (i, 0))
    vec = pl.BlockSpec((1, D), lambda i: (0, 0))
    return pl.pallas_call(
        body, name=name, grid=(S // ROW_BLOCK,), in_specs=[row, row, vec, vec, vec], out_specs=row,
        out_shape=jax.ShapeDtypeStruct((S, D), F32), compiler_params=_params(1),
    )(x, y, gate, lng, lnb)


def _add_colsum(acc_ref, r, val):
    acc_ref[r:r + 1, :] += jnp.sum(val, axis=0, keepdims=True)


def ln_bwd(x_in, y, gate, lng, *, dout=None, nxt=None, name):
    S = x_in.shape[0]
    has_next = nxt is not None

    def body(*refs):
        if has_next:
            dzn_ref, dun_ref, scn_ref, xo_ref = refs[:4]
            k = 4
        else:
            do_ref = refs[0]
            k = 1
        x_ref, y_ref, gt_ref, g_ref = refs[k:k + 4]
        dz_ref, acc_ref = refs[k + 4:]

        @pl.when(pl.program_id(0) == 0)
        def _():
            acc_ref[...] = jnp.zeros_like(acc_ref)

        if has_next:
            du = dun_ref[...]
            dout_v = ALPHA * dzn_ref[...] + du * (1.0 + scn_ref[...])
            _add_colsum(acc_ref, 3, du * xo_ref[...])
            _add_colsum(acc_ref, 4, du)
        else:
            dout_v = do_ref[...]
        yv = y_ref[...]
        z = ALPHA * x_ref[...] + (1.0 + gt_ref[...]) * yv
        zc, rstd = _ln_stats(z)
        xhat = zc * rstd
        _add_colsum(acc_ref, 0, dout_v * xhat)
        _add_colsum(acc_ref, 1, dout_v)
        dxh = dout_v * g_ref[...]
        m1 = jnp.mean(dxh, axis=-1, keepdims=True)
        m2 = jnp.mean(dxh * xhat, axis=-1, keepdims=True)
        dz = rstd * (dxh - m1 - xhat * m2)
        _add_colsum(acc_ref, 2, dz * yv)
        dz_ref[...] = dz

    row = pl.BlockSpec((ROW_BLOCK, D), lambda i: (i, 0))
    vec = pl.BlockSpec((1, D), lambda i: (0, 0))
    if has_next:
        in_specs = [row, row, vec, row]
        args = list(nxt)
    else:
        in_specs = [row]
        args = [dout]
    in_specs += [row, row, vec, vec]
    args += [x_in, y, gate, lng]
    return pl.pallas_call(
        body, name=name, grid=(S // ROW_BLOCK,), in_specs=in_specs,
        out_specs=[row, pl.BlockSpec((ACC_ROWS, D), lambda i: (0, 0))],
        out_shape=[jax.ShapeDtypeStruct((S, D), F32), jax.ShapeDtypeStruct((ACC_ROWS, D), F32)],
        compiler_params=_params(1),
    )(*args)


def combine_final(dz, du, sc, x_in, *, name):
    S = dz.shape[0]

    def body(dz_ref, du_ref, sc_ref, x_ref, dx_ref, acc_ref):
        @pl.when(pl.program_id(0) == 0)
        def _():
            acc_ref[...] = jnp.zeros_like(acc_ref)

        du_v = du_ref[...]
        dx_ref[...] = ALPHA * dz_ref[...] + du_v * (1.0 + sc_ref[...])
        _add_colsum(acc_ref, 3, du_v * x_ref[...])
        _add_colsum(acc_ref, 4, du_v)

    row = pl.BlockSpec((ROW_BLOCK, D), lambda i: (i, 0))
    vec = pl.BlockSpec((1, D), lambda i: (0, 0))
    return pl.pallas_call(
        body, name=name, grid=(S // ROW_BLOCK,), in_specs=[row, row, vec, row],
        out_specs=[row, pl.BlockSpec((ACC_ROWS, D), lambda i: (0, 0))],
        out_shape=[jax.ShapeDtypeStruct((S, D), F32), jax.ShapeDtypeStruct((ACC_ROWS, D), F32)],
        compiler_params=_params(1),
    )(dz, du, sc, x_in)


def colsum(a, *, name):
    S, N = a.shape

    def body(a_ref, o_ref):
        @pl.when(pl.program_id(0) == 0)
        def _():
            o_ref[...] = jnp.zeros_like(o_ref)

        o_ref[...] += jnp.sum(a_ref[...].astype(F32), axis=0, keepdims=True)

    return pl.pallas_call(
        body, name=name, grid=(S // ROW_BLOCK,), in_specs=[pl.BlockSpec((ROW_BLOCK, N), lambda i: (i, 0))],
        out_specs=pl.BlockSpec((1, N), lambda i: (0, 0)), out_shape=jax.ShapeDtypeStruct((1, N), F32),
        compiler_params=_params(1),
    )(a)


def loss_head(y, t, *, name):
    S = y.shape[0]

    def body(y_ref, t_ref, dy_ref, l_ref):
        @pl.when(pl.program_id(0) == 0)
        def _():
            l_ref[...] = jnp.zeros_like(l_ref)

        e = y_ref[...] - t_ref[...]
        dy_ref[...] = e * (1.0 / D)
        per_tok = jnp.sum(e * e, axis=1, keepdims=True) * (1.0 / D)
        l_ref[...] += 0.5 * jnp.sum(per_tok, axis=0, keepdims=True)

    row = pl.BlockSpec((ROW_BLOCK, D), lambda i: (i, 0))
    return pl.pallas_call(
        body, name=name, grid=(S // ROW_BLOCK,), in_specs=[row, row],
        out_specs=[row, pl.BlockSpec((8, 128), lambda i: (0, 0))],
        out_shape=[jax.ShapeDtypeStruct((S, D), F32), jax.ShapeDtypeStruct((8, 128), F32)],
        compiler_params=_params(1),
    )(y, t)


def _log_sigmoid(x):
    return jnp.minimum(x, 0.0) - jnp.log(1.0 + jnp.exp(-jnp.abs(x)))


def _silu(x):
    return x * (1.0 / (1.0 + jnp.exp(-x)))


def _cumsum_steps(x):
    row = lax.broadcasted_iota(jnp.int32, x.shape, 0)
    step = 1
    while step < x.shape[0]:
        x = x + jnp.where(row >= step, pltpu.roll(x, step, 0), 0.0)
        step *= 2
    return x


@jax.custom_vjp
def _cumsum_rows(x):
    return _cumsum_steps(x)


def _cumsum_rows_fwd(x):
    return _cumsum_steps(x), None


def _cumsum_rows_bwd(_, g):
    return (jnp.sum(g, axis=0, keepdims=True) - _cumsum_steps(g) + g,)


_cumsum_rows.defvjp(_cumsum_rows_fwd, _cumsum_rows_bwd)


def _gla_chunk(q, k, v, g, gk, s0t, w2p, bgk, gn):
    C = q.shape[0]
    row = lax.broadcasted_iota(jnp.int32, (C, C), 0)
    col = lax.broadcasted_iota(jnp.int32, (C, C), 1)
    lower = row >= col
    la = _log_sigmoid(_dot(_bf(gk), _bf(w2p)) + bgk) * GLA_TAU_INV
    outs, states = [], []
    for h in range(GLA_H):
        ks = slice(h * GLA_DKH, (h + 1) * GLA_DKH)
        vs = slice(h * GLA_DVH, (h + 1) * GLA_DVH)
        qh = q[:, ks] * (GLA_DKH ** -0.5)
        kh, vh, gh, lah, s0 = k[:, ks], v[:, vs], g[:, vs], la[:, ks], s0t[h]
        cum = _cumsum_rows(lah)
        e_pos = jnp.exp(cum)
        e_neg = jnp.exp(-cum)
        q_f = qh * e_pos
        a_f = _dot_nt(_bf(q_f), _bf(kh * e_neg))
        a_b = _dot_nt(_bf(qh * e_neg), _bf(kh * e_pos))
        att = jnp.where(lower, a_f, a_b)
        o = _dot(_bf(att), _bf(vh)) + _dot_nt(_bf(q_f), _bf(s0))
        tot = jnp.sum(lah, axis=0, keepdims=True)
        k_end = kh * jnp.exp(tot - cum)
        states.append(s0 * jnp.exp(tot) + _dot_tn(_bf(vh), _bf(k_end)))
        on = o * lax.rsqrt(jnp.mean(o * o, axis=-1, keepdims=True) + RMS_EPS) * gn[:, vs]
        outs.append(on * _silu(gh))
    return jnp.concatenate(outs, axis=1), tuple(states)


def _gla_split(p):
    return (p[:, 0:GLA_DK], p[:, GLA_DK:2 * GLA_DK], p[:, 2 * GLA_DK:2 * GLA_DK + GLA_DV],
            p[:, 2 * GLA_DK + GLA_DV:2 * GLA_DK + 2 * GLA_DV], p[:, 2 * GLA_DK + 2 * GLA_DV:GLA_INP])


def gla_fwd(proj, w2p, bgk, gn, *, comm=None, name):
    S = proj.shape[0]
    n_c = S // CHUNK

    def body(p_ref, w_ref, b_ref, gn_ref, o_ref, st_ref, st):
        @pl.when(pl.program_id(0) == 0)
        def _():
            st[...] = jnp.zeros_like(st)

        s0 = tuple(st[h] for h in range(GLA_H))
        for h in range(GLA_H):
            st_ref[h] = s0[h]
        og, s1 = _gla_chunk(*_gla_split(p_ref[...]), s0, w_ref[...], b_ref[...], gn_ref[...])
        o_ref[...] = _bf(og)
        for h in range(GLA_H):
            st[h] = s1[h]

    full = lambda shape: pl.BlockSpec(shape, lambda i: (0,) * len(shape))
    return _call_hosting(
        body, comm, first=lambda: pl.program_id(0) == 0, last=lambda: pl.program_id(0) == n_c - 1,
        name=name, grid=(n_c,),
        in_specs=[pl.BlockSpec((CHUNK, GLA_INP), lambda i: (i, 0)), full((128, GLA_DK)), full((1, GLA_DK)),
                  full((1, GLA_DV))],
        out_specs=[pl.BlockSpec((CHUNK, GLA_DV), lambda i: (i, 0)),
                   pl.BlockSpec((None, GLA_H, GLA_DVH, GLA_DKH), lambda i: (i, 0, 0, 0))],
        out_shape=[jax.ShapeDtypeStruct((S, GLA_DV), BF16),
                   jax.ShapeDtypeStruct((n_c, GLA_H, GLA_DVH, GLA_DKH), F32)],
        scratch_shapes=[pltpu.VMEM((GLA_H, GLA_DVH, GLA_DKH), F32)], compiler_params=_params(1),
        args=(proj, w2p, bgk, gn))


def gla_bwd(proj, dog, states, w2p, bgk, gn, *, comm=None, name):
    S = proj.shape[0]
    n_c = S // CHUNK

    def body(p_ref, dog_ref, st_ref, w_ref, b_ref, gn_ref, dp_ref, dw_ref, db_ref, dgn_ref, ds):
        @pl.when(pl.program_id(0) == 0)
        def _():
            ds[...] = jnp.zeros_like(ds)
            dw_ref[...] = jnp.zeros_like(dw_ref)
            db_ref[...] = jnp.zeros_like(db_ref)
            dgn_ref[...] = jnp.zeros_like(dgn_ref)

        q, k, v, g, gk = _gla_split(p_ref[...])
        s0 = tuple(st_ref[h] for h in range(GLA_H))
        _, vjp = jax.vjp(_gla_chunk, q, k, v, g, gk, s0, w_ref[...], b_ref[...], gn_ref[...])
        dq, dk, dv, dg, dgk, ds0, dw, db, dgn = vjp((dog_ref[...], tuple(ds[h] for h in range(GLA_H))))
        dp_ref[:, 0:GLA_DK] = _bf(dq)
        dp_ref[:, GLA_DK:2 * GLA_DK] = _bf(dk)
        dp_ref[:, 2 * GLA_DK:2 * GLA_DK + GLA_DV] = _bf(dv)
        dp_ref[:, 2 * GLA_DK + GLA_DV:2 * GLA_DK + 2 * GLA_DV] = _bf(dg)
        dp_ref[:, 2 * GLA_DK + 2 * GLA_DV:GLA_INP] = _bf(dgk)
        for h in range(GLA_H):
            ds[h] = ds0[h]
        dw_ref[...] += dw
        db_ref[...] += db
        dgn_ref[...] += dgn

    full = lambda shape: pl.BlockSpec(shape, lambda i: (0,) * len(shape))
    rev = lambda i: (n_c - 1 - i, 0)
    return _call_hosting(
        body, comm, first=lambda: pl.program_id(0) == 0, last=lambda: pl.program_id(0) == n_c - 1,
        name=name, grid=(n_c,),
        in_specs=[pl.BlockSpec((CHUNK, GLA_INP), rev), pl.BlockSpec((CHUNK, GLA_DV), rev),
                  pl.BlockSpec((None, GLA_H, GLA_DVH, GLA_DKH), lambda i: (n_c - 1 - i, 0, 0, 0)),
                  full((128, GLA_DK)), full((1, GLA_DK)), full((1, GLA_DV))],
        out_specs=[pl.BlockSpec((CHUNK, GLA_INP), rev), full((128, GLA_DK)), full((1, GLA_DK)), full((1, GLA_DV))],
        out_shape=[jax.ShapeDtypeStruct((S, GLA_INP), BF16), jax.ShapeDtypeStruct((128, GLA_DK), F32),
                   jax.ShapeDtypeStruct((1, GLA_DK), F32), jax.ShapeDtypeStruct((1, GLA_DV), F32)],
        scratch_shapes=[pltpu.VMEM((GLA_H, GLA_DVH, GLA_DKH), F32)], compiler_params=_params(1),
        args=(proj, dog, states, w2p, bgk, gn))


def _rel_index():
    t = np.arange(REL_TILE)[:, None]
    j = np.arange(REL_TILE)[None, :]
    tiles = []
    for m in REL_TILES:
        chunks = (REL_TILE // CHUNK) * m + j // CHUNK - t // CHUNK
        band = (chunks >= 0) & (chunks <= LEFT // CHUNK)
        dist = LEFT - REL_TILE * m + t - j
        tiles.append(np.where(band, np.minimum(dist, MAX_REL) + MAX_REL, N_REL))
    return jnp.asarray(np.stack(tiles).reshape(1, -1).astype(np.int32))


REL_BLOCK = 2048


def _one_hot(idx_row):
    return (lax.broadcasted_iota(jnp.int32, (REL_PAD, idx_row.shape[1]), 0) == idx_row).astype(F32)


def rel_bias_tiles(rel_pad, idx, *, name):
    E = idx.shape[1]

    def body(r_ref, i_ref, o_ref):
        o_ref[...] = jnp.dot(r_ref[...], _one_hot(i_ref[...]), precision=HIGHEST, preferred_element_type=F32)

    return pl.pallas_call(
        body, name=name, grid=(E // REL_BLOCK,),
        in_specs=[pl.BlockSpec((ATT_H, REL_PAD), lambda i: (0, 0)), pl.BlockSpec((1, REL_BLOCK), lambda i: (0, i))],
        out_specs=pl.BlockSpec((ATT_H, REL_BLOCK), lambda i: (0, i)),
        out_shape=jax.ShapeDtypeStruct((ATT_H, E), F32), compiler_params=_params(1),
    )(rel_pad, idx)


def rel_bias_grad(dtiles_flat, dclip, idx, *, name):
    E = idx.shape[1]
    n_steps = E // REL_BLOCK

    def body(d_ref, c_ref, i_ref, o_ref):
        @pl.when(pl.program_id(0) == 0)
        def _():
            o_ref[...] = jnp.zeros_like(o_ref)

        o_ref[...] += lax.dot_general(d_ref[...], _one_hot(i_ref[...]), (((1,), (1,)), ((), ())),
                                      precision=HIGHEST, preferred_element_type=F32)

        @pl.when(pl.program_id(0) == n_steps - 1)
        def _():
            at_clip = lax.broadcasted_iota(jnp.int32, (1, REL_PAD), 1) == 2 * MAX_REL
            o_ref[...] += jnp.where(at_clip, jnp.sum(c_ref[...], axis=1, keepdims=True), 0.0)

    return pl.pallas_call(
        body, name=name, grid=(n_steps,),
        in_specs=[pl.BlockSpec((ATT_H, REL_BLOCK), lambda i: (0, i)), pl.BlockSpec((ATT_H, 128), lambda i: (0, 0)),
                  pl.BlockSpec((1, REL_BLOCK), lambda i: (0, i))],
        out_specs=pl.BlockSpec((ATT_H, REL_PAD), lambda i: (0, 0)),
        out_shape=jax.ShapeDtypeStruct((ATT_H, REL_PAD), F32), compiler_params=_params(1),
    )(dtiles_flat, dclip, idx)


def _attn_bias(tiles, clip):
    const = jnp.broadcast_to(clip, (REL_TILE, REL_TILE))
    zero = jnp.zeros((REL_TILE, REL_TILE), F32)
    rows = []
    for qt in range(ATT_QB // REL_TILE):
        blocks = []
        for kt in range(ATT_KB // REL_TILE):
            m = kt - qt
            if m in REL_TILES:
                blocks.append(tiles[REL_TILES.index(m)])
            elif 0 <= m < REL_TILES[0]:
                blocks.append(const)
            else:
                blocks.append(zero)
        rows.append(jnp.concatenate(blocks, axis=1))
    return jnp.concatenate(rows, axis=0)


def _attn_bias_grad(ds, dt_ref, dc_ref, a):
    tile = lambda qt, kt: ds[qt * REL_TILE:(qt + 1) * REL_TILE, kt * REL_TILE:(kt + 1) * REL_TILE]
    const = None
    sums = [None] * len(REL_TILES)
    for qt in range(ATT_QB // REL_TILE):
        for kt in range(ATT_KB // REL_TILE):
            m = kt - qt
            if m in REL_TILES:
                n = REL_TILES.index(m)
                sums[n] = tile(qt, kt) if sums[n] is None else sums[n] + tile(qt, kt)
            elif 0 <= m < REL_TILES[0]:
                const = tile(qt, kt) if const is None else const + tile(qt, kt)
    for n, v in enumerate(sums):
        dt_ref[a, n] += v
    dc_ref[a] += jnp.sum(const, axis=0, keepdims=True)


def _attn_head_lanes():
    lane = lax.broadcasted_iota(jnp.int32, (1, 2 * ATT_HD), 1)
    return [(lane >= a * ATT_HD) & (lane < (a + 1) * ATT_HD) for a in range(2)]


def _attn_band_bias(tiles, clip):
    j = lax.broadcasted_iota(jnp.int32, (ATT_QB, ATT_KB), 1)
    t = lax.broadcasted_iota(jnp.int32, (ATT_QB, ATT_KB), 0)
    shift = CHUNK.bit_length() - 1
    chunks = lax.shift_right_logical(j, shift) - lax.shift_right_logical(t, shift)
    band = (chunks >= 0) & (chunks <= LEFT // CHUNK)
    return jnp.where(band, _attn_bias(tiles, clip), NEG_INF)


def _attn_exp(qa, kb, bias, key_bias):
    s = _dot_nt(qa, kb) + bias + key_bias
    e = jnp.exp(s - jnp.max(s, axis=-1, keepdims=True))
    return e, jnp.sum(e, axis=-1, keepdims=True)


def _attn_specs():
    n_hp = ATT_H // 2
    q_spec = pl.BlockSpec((ATT_QB, 128), lambda hp, g: (g, hp))

    def win(col0, back):
        return pl.BlockSpec((ATT_QB, 128), lambda hp, g: (jnp.maximum(g - back, 0), col0 + hp))

    kv_specs = [win(n_hp, 2), win(n_hp, 1), win(n_hp, 0), win(2 * n_hp, 2), win(2 * n_hp, 1), win(2 * n_hp, 0)]
    tiles_spec = pl.BlockSpec((2, len(REL_TILES), REL_TILE, REL_TILE), lambda hp, g: (hp, 0, 0, 0))
    clip_spec = pl.BlockSpec((2, 1, 128), lambda hp, g: (hp, 0, 0))
    return q_spec, kv_specs, tiles_spec, clip_spec


def _attn_window(refs, g):
    kb = jnp.concatenate([_bf(r[...]) for r in refs[0:3]], axis=0)
    vb = jnp.concatenate([_bf(r[...]) for r in refs[3:6]], axis=0)
    j = lax.broadcasted_iota(jnp.int32, (1, ATT_KB), 1)
    return kb, vb, jnp.where(j + (g - 2) * ATT_QB >= 0, 0.0, NEG_INF)


def attn_fwd(qkv, tiles, clip, *, comm=None, name):
    S = qkv.shape[0]
    q_spec, kv_specs, tiles_spec, clip_spec = _attn_specs()

    def body(q_ref, *rest):
        kv_refs, t_ref, c_ref, o_ref, bias = rest[:6], rest[6], rest[7], rest[8], rest[9]
        g = pl.program_id(1)

        @pl.when(g == 0)
        def _():
            for a in range(2):
                bias[a] = _attn_band_bias(t_ref[a], c_ref[a])

        kb, vb, key_bias = _attn_window(kv_refs, g)
        q = q_ref[...].astype(F32)
        out = jnp.zeros((ATT_QB, 2 * ATT_HD), F32)
        for a, lanes in enumerate(_attn_head_lanes()):
            mf = lanes.astype(F32)
            e, l = _attn_exp(_bf(q * (mf * ATT_HD ** -0.5)), kb, bias[a], key_bias)
            out = out + _dot(_bf(e), vb) * (mf * (1.0 / l))
        o_ref[...] = _bf(out)

    n_hp, n_g = ATT_H // 2, S // ATT_QB
    return _call_hosting(
        body, comm, first=lambda: (pl.program_id(0) == 0) & (pl.program_id(1) == 0),
        last=lambda: (pl.program_id(0) == n_hp - 1) & (pl.program_id(1) == n_g - 1),
        name=name, grid=(n_hp, n_g), in_specs=[q_spec] + kv_specs + [tiles_spec, clip_spec],
        out_specs=[q_spec], out_shape=[jax.ShapeDtypeStruct((S, D), BF16)],
        scratch_shapes=[pltpu.VMEM((2, ATT_QB, ATT_KB), F32)], compiler_params=_params(2),
        args=(*([qkv] * 7), tiles, clip))


def attn_bwd(qkv, do, tiles, clip, *, comm=None, name):
    S = qkv.shape[0]
    q_spec, kv_specs, tiles_spec, clip_spec = _attn_specs()
    col_spec = pl.BlockSpec((S, 128), lambda hp, g: (0, hp))

    def body(q_ref, *rest):
        kv_refs, t_ref, c_ref, do_ref = rest[:6], rest[6], rest[7], rest[8]
        dq_ref, dk_ref, dv_ref, dt_ref, dc_ref, bias = rest[9:]
        g = pl.program_id(1)

        @pl.when(g == 0)
        def _():
            for a in range(2):
                bias[a] = _attn_band_bias(t_ref[a], c_ref[a])
            dk_ref[...] = jnp.zeros_like(dk_ref)
            dv_ref[...] = jnp.zeros_like(dv_ref)
            dt_ref[...] = jnp.zeros_like(dt_ref)
            dc_ref[...] = jnp.zeros_like(dc_ref)

        kb, vb, key_bias = _attn_window(kv_refs, g)
        q = q_ref[...].astype(F32)
        do = do_ref[...]
        dq = jnp.zeros((ATT_QB, 2 * ATT_HD), F32)
        dkw = jnp.zeros((ATT_KB, 2 * ATT_HD), F32)
        dvw = jnp.zeros((ATT_KB, 2 * ATT_HD), F32)
        for a, lanes in enumerate(_attn_head_lanes()):
            mf = lanes.astype(F32) * ATT_HD ** -0.5
            qa = _bf(q * mf)
            e, l = _attn_exp(qa, kb, bias[a], key_bias)
            p = e * (1.0 / l)
            do_a = jnp.where(lanes, do, jnp.zeros_like(do))
            dp = _dot_nt(do_a, vb)
            ds = p * (dp - jnp.sum(p * dp, axis=-1, keepdims=True))
            ds_b = _bf(ds)
            dq = dq + _dot(ds_b, kb) * mf
            dkw = dkw + _dot_tn(ds_b, qa)
            dvw = dvw + _dot_tn(_bf(p), do_a)
            _attn_bias_grad(ds, dt_ref, dc_ref, a)
        dq_ref[...] = _bf(dq)
        for blk in range(3):
            src = g - 2 + blk

            @pl.when(src >= 0)
            def _(blk=blk, src=src):
                rows = pl.ds(pl.multiple_of(src * ATT_QB, ATT_QB), ATT_QB)
                dk_ref[rows, :] += dkw[blk * ATT_QB:(blk + 1) * ATT_QB]
                dv_ref[rows, :] += dvw[blk * ATT_QB:(blk + 1) * ATT_QB]

    n_hp, n_g = ATT_H // 2, S // ATT_QB
    return _call_hosting(
        body, comm, first=lambda: (pl.program_id(0) == 0) & (pl.program_id(1) == 0),
        last=lambda: (pl.program_id(0) == n_hp - 1) & (pl.program_id(1) == n_g - 1),
        name=name, grid=(n_hp, n_g),
        in_specs=[q_spec] + kv_specs + [tiles_spec, clip_spec, q_spec],
        out_specs=[q_spec, col_spec, col_spec, tiles_spec, clip_spec],
        out_shape=[jax.ShapeDtypeStruct((S, D), BF16)] + [jax.ShapeDtypeStruct((S, D), F32)] * 2
        + [jax.ShapeDtypeStruct((ATT_H, len(REL_TILES), REL_TILE, REL_TILE), F32),
           jax.ShapeDtypeStruct((ATT_H, 1, 128), F32)],
        scratch_shapes=[pltpu.VMEM((2, ATT_QB, ATT_KB), F32)], compiler_params=_params(2),
        args=(*([qkv] * 7), tiles, clip, do))


def mods_partial(c_all, w_ada, *, name):
    n_l, _, n_c = w_ada.shape

    def body(c_ref, w_ref, o_ref):
        o_ref[...] = _dot(_bf(_silu(c_ref[...])), _bf(w_ref[...]))

    return pl.pallas_call(
        body, name=name, grid=(n_l,),
        in_specs=[pl.BlockSpec((N_DEV, D), lambda l: (0, 0)), pl.BlockSpec((None, D, n_c), lambda l: (l, 0, 0))],
        out_specs=pl.BlockSpec((None, N_DEV, n_c), lambda l: (l, 0, 0)),
        out_shape=jax.ShapeDtypeStruct((n_l, N_DEV, n_c), F32), compiler_params=_params(1),
    )(c_all, w_ada)


def w_ada_grad(c_all, dm, *, name):
    n_l, _, n_c = dm.shape

    def body(c_ref, d_ref, o_ref):
        o_ref[...] = lax.dot_general(_silu(c_ref[...]), d_ref[...], (((0,), (0,)), ((), ())),
                                     precision=HIGHEST, preferred_element_type=F32)

    return pl.pallas_call(
        body, name=name, grid=(n_l,),
        in_specs=[pl.BlockSpec((N_DEV, D), lambda l: (0, 0)), pl.BlockSpec((None, N_DEV, n_c), lambda l: (l, 0, 0))],
        out_specs=pl.BlockSpec((None, D, n_c), lambda l: (l, 0, 0)),
        out_shape=jax.ShapeDtypeStruct((n_l, D, n_c), F32), compiler_params=_params(1),
    )(c_all, dm)


def adamw(w, m, v, gparts, *, block_rows, name):
    R, C = w.shape
    n = gparts.shape[0]

    def body(w_ref, m_ref, v_ref, g_ref, go_ref, d_ref, mo_ref, vo_ref):
        g = g_ref[0].astype(F32)
        for k in range(1, n):
            g = g + g_ref[k].astype(F32)
        m_new = ADAM_B1 * m_ref[...] + (1.0 - ADAM_B1) * g
        v_new = ADAM_B2 * v_ref[...] + (1.0 - ADAM_B2) * (g * g)
        m_hat = m_new / (1.0 - ADAM_B1 ** ADAM_STEP)
        v_hat = v_new / (1.0 - ADAM_B2 ** ADAM_STEP)
        go_ref[...] = g
        d_ref[...] = -ADAM_LR * (m_hat / (jnp.sqrt(v_hat) + ADAM_EPS) + ADAM_WD * w_ref[...])
        mo_ref[...] = m_new
        vo_ref[...] = v_new

    blk = pl.BlockSpec((block_rows, C), lambda i: (i, 0))
    return pl.pallas_call(
        body, name=name, grid=(R // block_rows,),
        in_specs=[blk, blk, blk, pl.BlockSpec((n, block_rows, C), lambda i: (0, i, 0))],
        out_specs=[blk] * 4, out_shape=[jax.ShapeDtypeStruct((R, C), F32)] * 4, compiler_params=_params(1),
    )(w, m, v, gparts)


def adamw_nd(w, m, v, gparts, *, name):
    shape = w.shape
    two = (int(np.prod(shape[:-1])), shape[-1])
    rows = two[0]
    block_rows = rows
    for cand in (512, 256):
        if rows > cand and rows % cand == 0:
            block_rows = cand
            break
    outs = adamw(w.reshape(two), m.reshape(two), v.reshape(two), gparts.reshape((gparts.shape[0],) + two),
                 block_rows=block_rows, name=name)
    return [o.reshape(shape) for o in outs]


def sum_parts(parts, *, name):
    n, R, C = parts.shape

    def body(p_ref, o_ref):
        acc = p_ref[0]
        for k in range(1, n):
            acc = acc + p_ref[k]
        o_ref[...] = acc

    return pl.pallas_call(
        body, name=name, in_specs=[pl.BlockSpec((n, R, C), lambda: (0, 0, 0))],
        out_specs=pl.BlockSpec((R, C), lambda: (0, 0)), out_shape=jax.ShapeDtypeStruct((R, C), F32),
        compiler_params=pltpu.CompilerParams(vmem_limit_bytes=VMEM_LIMIT),
    )(parts)


def _my_place():
    return lax.axis_index("x"), lax.axis_index("y"), lax.axis_index("c")


def _full_shape(kind, shard):
    n_l, rows, cols = shard
    return {"col": (n_l, rows, N_DEV * cols), "row": (n_l, N_DEV * rows, cols), "stk": (N_DEV, n_l, rows, cols)}[kind]


def _slab(ref, kind, dev, shard):
    _, rows, cols = shard
    if kind == "col":
        return ref.at[:, :, pl.ds(pl.multiple_of(dev * cols, 128), cols)]
    if kind == "row":
        return ref.at[:, pl.ds(pl.multiple_of(dev * rows, 8), rows), :]
    return ref.at[dev]


def _hbm_specs(n):
    return [pl.BlockSpec(memory_space=pltpu.HBM)] * n


def all_gather(x_shard, *, name):
    m_per, n = x_shard.shape

    def body(x_ref, out_ref, send_sems, recv_sems, local_sem):
        x, y, c = _my_place()
        me, sibling = (x, y, c), (x, y, 1 - c)
        chips = [(1 - x, y), (x, 1 - y), (1 - x, 1 - y)]

        def rows(px, py, pc):
            return out_ref.at[pl.ds((4 * px + 2 * py + pc) * m_per, m_per), :]

        def copy(k, block, to, src=None):
            return pltpu.make_async_remote_copy(
                src_ref=rows(*block) if src is None else src, dst_ref=rows(*block),
                send_sem=send_sems.at[k], recv_sem=recv_sems.at[k], device_id=to, device_id_type=MESH)

        mine = pltpu.make_async_copy(x_ref, rows(*me), local_sem)
        mine.start()
        first = [copy(0, me, sibling, src=x_ref)]
        first += [copy(1 + j, me, (*chip, c), src=x_ref) for j, chip in enumerate(chips)]
        for cp in first:
            cp.start()
        passed = [copy(4 + j, (*chip, c), sibling) for j, chip in enumerate(chips)]
        for j, chip in enumerate(chips):
            copy(1 + j, (*chip, c), me).wait_recv()
            passed[j].start()
        copy(0, sibling, me).wait_recv()
        for j, chip in enumerate(chips):
            copy(4 + j, (*chip, 1 - c), me).wait_recv()
        for cp in first + passed:
            cp.wait_send()
        mine.wait()

    return pl.pallas_call(
        body, name=name, out_shape=jax.ShapeDtypeStruct((N_DEV * m_per, n), x_shard.dtype),
        in_specs=[pl.BlockSpec(memory_space=pltpu.VMEM)], out_specs=pl.BlockSpec(memory_space=pltpu.VMEM),
        scratch_shapes=[pltpu.SemaphoreType.DMA((7,)), pltpu.SemaphoreType.DMA((7,)), pltpu.SemaphoreType.DMA],
        compiler_params=pltpu.CompilerParams(vmem_limit_bytes=VMEM_LIMIT),
    )(x_shard)


def gather_plan(shards, kinds, layers):
    n_t = len(shards)
    shapes = [(1,) + tuple(s.shape[1:]) for s in shards]

    def copies(x_refs, out_refs, sems):
        send_sems, recv_sems, local_sems = sems
        x, y, c = _my_place()
        me, sibling = (x, y, c), (x, y, 1 - c)
        chips = [(1 - x, y), (x, 1 - y), (1 - x, 1 - y)]
        own = [x_refs[t].at[pl.ds(layers[t], 1)] for t in range(n_t)]

        def slab(t, px, py, pc):
            return _slab(out_refs[t], kinds[t], 4 * px + 2 * py + pc, shapes[t])

        def copy(t, k, block, to, src=None):
            return pltpu.make_async_remote_copy(
                src_ref=slab(t, *block) if src is None else src, dst_ref=slab(t, *block),
                send_sem=send_sems.at[7 * t + k], recv_sem=recv_sems.at[7 * t + k], device_id=to,
                device_id_type=MESH)

        mine = [pltpu.make_async_copy(own[t], slab(t, *me), local_sems.at[t]) for t in range(n_t)]
        sends = []
        for t in range(n_t):
            sends.append(copy(t, 0, me, sibling, src=own[t]))
            sends += [copy(t, 1 + j, me, (*chip, c), src=own[t]) for j, chip in enumerate(chips)]
        return mine, sends, copy, me, sibling, chips, c

    def first(x_refs, out_refs, sems):
        mine, sends = copies(x_refs, out_refs, sems)[:2]
        for cp in mine + sends:
            cp.start()

    def last(x_refs, out_refs, sems):
        mine, sends, copy, me, sibling, chips, c = copies(x_refs, out_refs, sems)
        passed = []
        for j, chip in enumerate(chips):
            for t in range(n_t):
                copy(t, 1 + j, (*chip, c), me).wait_recv()
                passed.append(copy(t, 4 + j, (*chip, c), sibling))
                passed[-1].start()
        for t in range(n_t):
            copy(t, 0, sibling, me).wait_recv()
        for j, chip in enumerate(chips):
            for t in range(n_t):
                copy(t, 4 + j, (*chip, 1 - c), me).wait_recv()
        for cp in sends + passed:
            cp.wait_send()
        for cp in mine:
            cp.wait()

    return Hosted(
        list(shards), [jax.ShapeDtypeStruct(_full_shape(k, shp), s.dtype) for k, shp, s in zip(kinds, shapes, shards)],
        [pltpu.SemaphoreType.DMA((7 * n_t,)), pltpu.SemaphoreType.DMA((7 * n_t,)), pltpu.SemaphoreType.DMA((n_t,))],
        first, last)


def scatter_plan(grads, kinds, shapes):
    n_t = len(grads)

    def copies(g_refs, out_refs, sems):
        send_sems, recv_sems, local_sems = sems
        x, y, c = _my_place()
        me = 4 * x + 2 * y + c
        local = [pltpu.make_async_copy(_slab(g_refs[t], kinds[t], me, shapes[t]), out_refs[t].at[me],
                                       local_sems.at[t]) for t in range(n_t)]
        remote = []
        for t in range(n_t):
            for r in range(1, N_DEV):
                px = 1 - x if r & 4 else x
                py = 1 - y if r & 2 else y
                pc = 1 - c if r & 1 else c
                remote.append(pltpu.make_async_remote_copy(
                    src_ref=_slab(g_refs[t], kinds[t], 4 * px + 2 * py + pc, shapes[t]), dst_ref=out_refs[t].at[me],
                    send_sem=send_sems.at[7 * t + r - 1], recv_sem=recv_sems.at[7 * t + r - 1],
                    device_id=(px, py, pc), device_id_type=MESH))
        return local, remote

    def first(g_refs, out_refs, sems):
        local, remote = copies(g_refs, out_refs, sems)
        for cp in local + remote:
            cp.start()

    def last(g_refs, out_refs, sems):
        local, remote = copies(g_refs, out_refs, sems)
        for cp in remote + local:
            cp.wait()

    return Hosted(
        list(grads), [jax.ShapeDtypeStruct((N_DEV,) + tuple(s), BF16) for s in shapes],
        [pltpu.SemaphoreType.DMA((7 * n_t,)), pltpu.SemaphoreType.DMA((7 * n_t,)), pltpu.SemaphoreType.DMA((n_t,))],
        first, last)


def adamw_layer(w, m, v, parts, layer, bufs, *, name):
    n_l, rows, cols = w.shape
    n = parts.shape[0]
    tr = min(rows, 256)

    def body(w_ref, m_ref, v_ref, g_ref, *rest):
        go_ref, d_ref, mo_ref, vo_ref = rest[-4:]
        g = g_ref[0].astype(F32)
        for k in range(1, n):
            g = g + g_ref[k].astype(F32)
        m_new = ADAM_B1 * m_ref[...] + (1.0 - ADAM_B1) * g
        v_new = ADAM_B2 * v_ref[...] + (1.0 - ADAM_B2) * (g * g)
        m_hat = m_new / (1.0 - ADAM_B1 ** ADAM_STEP)
        v_hat = v_new / (1.0 - ADAM_B2 ** ADAM_STEP)
        go_ref[...] = g
        d_ref[...] = -ADAM_LR * (m_hat / (jnp.sqrt(v_hat) + ADAM_EPS) + ADAM_WD * w_ref[...])
        mo_ref[...] = m_new
        vo_ref[...] = v_new

    blk = pl.BlockSpec((None, tr, cols), lambda i: (layer, i, 0))
    in_specs = [blk, blk, blk, pl.BlockSpec((n, None, tr, cols), lambda i: (0, 0, i, 0))]
    args = [w, m, v, parts]
    aliases = {}
    if bufs is not None:
        in_specs += [pl.BlockSpec(memory_space=pl.ANY)] * 4
        args += list(bufs)
        aliases = {4 + k: k for k in range(4)}
    return pl.pallas_call(
        body, name=name, grid=(rows // tr,), in_specs=in_specs, out_specs=[blk] * 4,
        out_shape=[jax.ShapeDtypeStruct((n_l, rows, cols), F32)] * 4, input_output_aliases=aliases,
        compiler_params=_params(1),
    )(*args)


BIG =("gla_w_in", "gla_w_out", "att_w_in", "att_w_out", "ff_w1", "ff_w2")
KIND = {"gla_w_in": "stk", "gla_w_out": "row", "att_w_in": "col", "att_w_out": "row", "ff_w1": "col", "ff_w2": "row"}


def _pack_small(arrs):
    parts = []
    for a in arrs:
        f = a.reshape(-1)
        parts.append(jnp.pad(f, (0, -f.shape[0] % 128)))
    flat = jnp.concatenate(parts)
    flat = jnp.pad(flat, (0, -flat.shape[0] % 1024))
    return flat.reshape(-1, 128)


def _unpack_small(packed, shapes):
    flat = packed.reshape(packed.shape[:-2] + (-1,))
    out, off = [], 0
    for shp in shapes:
        n = int(np.prod(shp))
        out.append(flat[..., off:off + n].reshape(packed.shape[:-2] + tuple(shp)))
        off += n + (-n % 128)
    return out


def _vec(a):
    return a.reshape(1, -1)


def _layer_weights(i):
    mixer = "gla" if i % 2 == 0 else "att"
    return [(f"{mixer}_w_in", i // 2), (f"{mixer}_w_out", i // 2), ("ff_w1", i), ("ff_w2", i)]


def _trunk(x, target, mods, sm, w, m, v):
    shard_bf = {n: w[n].astype(BF16) for n in BIG}

    def gather_layer(i):
        names = _layer_weights(i)
        return names, gather_plan([shard_bf[n] for n, _ in names], [KIND[n] for n, _ in names], [l for _, l in names])

    wts = {}

    def keep_gathered(names, arrays):
        for (n, l), a in zip(names, arrays):
            if KIND[n] == "stk":
                a = a.transpose(1, 2, 0, 3).reshape(1, D, GLA_IN)
                a = jnp.pad(a, ((0, 0), (0, 0), (0, GLA_INP - GLA_IN)))
            wts[n, l] = a

    names0, plan0 = gather_layer(0)
    keep_gathered(names0, run_hosted(plan0, name="gather_layer_0"))

    rel_idx = _rel_index()
    saved = []
    for i in range(DEPTH):
        sh1, sc1, g1, sh2, sc2, g2 = [mods[i, k:k + 1] for k in range(6)]
        rec = {"x0": x}
        j = i // 2
        nxt_names, nxt_plan = gather_layer(i + 1) if i + 1 < DEPTH else (None, None)
        if i % 2 == 0:
            w2p = jnp.pad(sm["gla_w_gk2"][j], ((0, 128 - GLA_RANK), (0, 0)))
            bgk, gn = _vec(sm["gla_b_gk"][j]), _vec(sm["gla_g_norm"][j])
            proj = mm_nn(x, wts["gla_w_in", j], 0, pro="mod", p1=sc1, p2=sh1, tm=1024, tn=640, name=f"gla_proj_{i}")
            (og, states), got = gla_fwd(proj, w2p, bgk, gn, comm=nxt_plan, name=f"gla_core_{i}")
            y = mm_nn(og, wts["gla_w_out", j], 0, tm=1024, tn=1024, name=f"gla_out_{i}")
            rec.update(kind="gla", j=j, w2p=w2p, bgk=bgk, gn=gn, proj=proj, og=og, states=states)
        else:
            rel = sm["att_rel_bias"][j]
            rel_pad = jnp.pad(rel, ((0, 0), (0, REL_PAD - N_REL)), constant_values=NEG_INF)
            tiles = rel_bias_tiles(rel_pad, rel_idx, name=f"att_bias_{i}")
            tiles = tiles.reshape(ATT_H, len(REL_TILES), REL_TILE, REL_TILE)
            clip = jnp.broadcast_to(rel[:, 2 * MAX_REL][:, None, None], (ATT_H, 1, 128))
            qkv = mm_nn(x, wts["att_w_in", j], 0, pro="mod", p1=sc1, p2=sh1, bias=_vec(sm["att_b_in"][j]),
                        out_dtype=BF16, tm=1024, tn=1024, name=f"att_proj_{i}")
            (o,), got = attn_fwd(qkv, tiles, clip, comm=nxt_plan, name=f"att_core_{i}")
            y = mm_nn(o, wts["att_w_out", j], 0, tm=1024, tn=1024, name=f"att_out_{i}")
            rec.update(kind="att", j=j, tiles=tiles, clip=clip, qkv=qkv, o=o)
        if nxt_plan is not None:
            keep_gathered(nxt_names, got)
        x1 = ln_fwd(x, y, g1, _vec(sm["ln_g"][i, 0]), _vec(sm["ln_b"][i, 0]), name=f"ln_mix_{i}")
        h = mm_nn(x1, wts["ff_w1", i], 0, pro="mod", p1=sc2, p2=sh2, out_dtype=BF16, tm=2048, tn=1024,
                  name=f"ff_up_{i}")
        y2 = mm_nn_ksplit(h, wts["ff_w2", i], 0, pro="relu2", tm=1024, tk=1024, name=f"ff_down_{i}")
        x2 = ln_fwd(x1, y2, g2, _vec(sm["ln_g"][i, 1]), _vec(sm["ln_b"][i, 1]), name=f"ln_ff_{i}")
        rec.update(y=y, x1=x1, h=h, y2=y2)
        saved.append(rec)
        x = x2

    dy, loss = loss_head(x, target, name="loss_head")

    gw = {}

    def wgrad(weight, layer, a, d, *, tn, tk=1024, tm=512, col_block0=0, **kw):
        gw[weight, layer] = mm_tn(a, d, tk=tk, tn=tn, tm=tm, out_buf=gw.get((weight, layer)),
                                  out_shape=wts[weight, layer].shape, col_block0=col_block0, **kw)

    def scatter_layer(units):
        grads = []
        for n, l in units:
            g = gw[n, l]
            if KIND[n] == "stk":
                g = g[:, :, :GLA_IN].reshape(1, D, N_DEV, GLA_IN // N_DEV).transpose(2, 0, 1, 3)
            grads.append(g)
        return scatter_plan(grads, [KIND[n] for n, _ in units], [(1,) + tuple(w[n].shape[1:]) for n, _ in units])

    results = {n: None for n in BIG}

    def update(units, parts):
        for (n, l), p in zip(units, parts):
            results[n] = adamw_layer(w[n], m[n], v[n], p, l, results[n], name=f"adamw_{n}_{l}")

    gs = {"ln_g": [[None, None] for _ in range(DEPTH)], "ln_b": [[None, None] for _ in range(DEPTH)],
          "gla_w_gk2": [None] * 2, "gla_b_gk": [None] * 2, "gla_g_norm": [None] * 2, "att_b_in": [None] * 2,
          "att_rel_bias": [None] * 2}
    dmods = [[None] * 6 for _ in range(DEPTH)]
    nxt = None
    nxt_slot = None
    for i in reversed(range(DEPTH)):
        rec = saved[i]
        sh1, sc1, g1, sh2, sc2, g2 = [mods[i, k:k + 1] for k in range(6)]
        x0, x1 = rec["x0"], rec["x1"]
        if nxt is None:
            dz2, acc = ln_bwd(x1, rec["y2"], g2, _vec(sm["ln_g"][i, 1]), dout=dy, name=f"ln_ff_bwd_{i}")
        else:
            dz2, acc = ln_bwd(x1, rec["y2"], g2, _vec(sm["ln_g"][i, 1]), nxt=nxt, name=f"ln_ff_bwd_{i}")
            dmods[nxt_slot[0]][nxt_slot[1]] = acc[3]
            dmods[nxt_slot[0]][nxt_slot[2]] = acc[4]
        gs["ln_g"][i][1], gs["ln_b"][i][1], dmods[i][5] = acc[0], acc[1], acc[2]
        dh = mm_nt([dz2], wts["ff_w2", i], 0, pro="scale", p1=g2, epi_h=rec["h"], out_dtype=BF16, tm=2048, tn=512,
                   name=f"ff_down_bwd_{i}")
        wgrad("ff_w2", i, rec["h"], dz2, pro="relu2", dscale=g2, tk=2048, tn=1024, name=f"ff_w2_grad_{i}")
        du2 = mm_nt([dh], wts["ff_w1", i], 0, tm=1024, tn=1024, name=f"ff_up_bwd_{i}")
        wgrad("ff_w1", i, x1, dh, pro="mod", p1=sc2, p2=sh2, tn=2048, name=f"ff_w1_grad_{i}")
        dz1, acc = ln_bwd(x0, rec["y"], g1, _vec(sm["ln_g"][i, 0]), nxt=(dz2, du2, sc2, x1), name=f"ln_mix_bwd_{i}")
        dmods[i][4], dmods[i][3] = acc[3], acc[4]
        gs["ln_g"][i][0], gs["ln_b"][i][0], dmods[i][2] = acc[0], acc[1], acc[2]
        j = rec["j"]
        w_in, w_out = _layer_weights(i)[:2]
        if rec["kind"] == "gla":
            dog = mm_nt([dz1], wts[w_out], 0, pro="scale", p1=g1, tm=1024, tn=1024, name=f"gla_out_bwd_{i}")
            wgrad(*w_out, rec["og"], dz1, dscale=g1, tn=1024, name=f"gla_wout_grad_{i}")
        else:
            do = mm_nt([dz1], wts[w_out], 0, pro="scale", p1=g1, out_dtype=BF16, tm=1024, tn=1024,
                       name=f"att_out_bwd_{i}")
            wgrad(*w_out, rec["o"], dz1, dscale=g1, tn=1024, name=f"att_wout_grad_{i}")
        units = [("ff_w1", i), ("ff_w2", i), w_out] + ([_layer_weights(i + 1)[0]] if i + 1 < DEPTH else [])
        plan = scatter_layer(units)
        if rec["kind"] == "gla":
            (dproj, dw2p, dbgk, dgn), parts = gla_bwd(rec["proj"], dog, rec["states"], rec["w2p"], rec["bgk"],
                                                      rec["gn"], comm=plan, name=f"gla_core_bwd_{i}")
            gs["gla_w_gk2"][j], gs["gla_b_gk"][j], gs["gla_g_norm"][j] = dw2p[:GLA_RANK], dbgk[0], dgn[0]
            du1 = mm_nt([dproj], wts[w_in], 0, tm=1024, tn=1024, name=f"gla_proj_bwd_{i}")
            wgrad(*w_in, x0, dproj, pro="mod", p1=sc1, p2=sh1, tk=512, tn=GLA_INP, name=f"gla_win_grad_{i}")
        else:
            (dq, dk, dv, dtiles, dclip), parts = attn_bwd(rec["qkv"], do, rec["tiles"], rec["clip"], comm=plan,
                                                          name=f"att_core_bwd_{i}")
            drel = rel_bias_grad(dtiles.reshape(ATT_H, -1), dclip.reshape(ATT_H, 128), rel_idx,
                                 name=f"att_bias_grad_{i}")
            gs["att_rel_bias"][j] = drel[:, :N_REL]
            gs["att_b_in"][j] = jnp.concatenate(
                [colsum(t, name=f"att_bin_grad_{i}_{n}")[0] for n, t in enumerate((dq, dk, dv))])
            du1 = mm_nt([dq, dk, dv], wts[w_in], 0, tm=512, tn=1024, name=f"att_proj_bwd_{i}")
            for n, t in enumerate((dq, dk, dv)):
                wgrad(*w_in, x0, t, pro="mod", p1=sc1, p2=sh1, tn=1024, col_block0=n, name=f"att_win_grad_{i}_{n}")
        update(units, parts)
        nxt = (dz1, du1, sc1, x0)
        nxt_slot = (i, 1, 0)
    dx, acc = combine_final(nxt[0], nxt[1], nxt[2], nxt[3], name="grad_x")
    last_units = [_layer_weights(0)[0]]
    update(last_units, run_hosted(scatter_layer(last_units), name="scatter_last"))
    dmods[0][1], dmods[0][0] = acc[3], acc[4]
    dmods = jnp.stack([jnp.stack(r) for r in dmods])
    gs = {k: jnp.stack([jnp.stack(r) if isinstance(r, list) else r for r in v]) for k, v in gs.items()}
    return loss, dx, dmods, gs, results


WEIGHTS = ("w_ada", "b_ada", "ln_g", "ln_b", "gla_w_in", "gla_w_gk2", "gla_b_gk", "gla_g_norm", "gla_w_out",
           "att_w_in", "att_b_in", "att_rel_bias", "att_w_out", "ff_w1", "ff_w2")
SMALL_SHARDED = {"ln_g": 2, "ln_b": 2, "gla_w_gk2": 2, "gla_g_norm": 2, "att_b_in": 1}
SMALL_GRADS = ("ln_g", "ln_b", "gla_w_gk2", "gla_b_gk", "gla_g_norm", "att_b_in", "att_rel_bias")


def kernel(x, c, w_ada, b_ada, ln_g, ln_b, gla_w_in, gla_w_gk2, gla_b_gk, gla_g_norm, gla_w_out, att_w_in, att_b_in, att_rel_bias, att_w_out, ff_w1, ff_w2, loss_target, m_w_ada, m_b_ada, m_ln_g, m_ln_b, m_gla_w_in, m_gla_w_gk2, m_gla_b_gk, m_gla_g_norm, m_gla_w_out, m_att_w_in, m_att_b_in, m_att_rel_bias, m_att_w_out, m_ff_w1, m_ff_w2, v_w_ada, v_b_ada, v_ln_g, v_ln_b, v_gla_w_in, v_gla_w_gk2, v_gla_b_gk, v_gla_g_norm, v_gla_w_out, v_att_w_in, v_att_b_in, v_att_rel_bias, v_att_w_out, v_ff_w1, v_ff_w2):
    w = dict(w_ada=w_ada, b_ada=b_ada, ln_g=ln_g, ln_b=ln_b, gla_w_in=gla_w_in, gla_w_gk2=gla_w_gk2,
             gla_b_gk=gla_b_gk, gla_g_norm=gla_g_norm, gla_w_out=gla_w_out, att_w_in=att_w_in, att_b_in=att_b_in,
             att_rel_bias=att_rel_bias, att_w_out=att_w_out, ff_w1=ff_w1, ff_w2=ff_w2)
    m = dict(w_ada=m_w_ada, b_ada=m_b_ada, ln_g=m_ln_g, ln_b=m_ln_b, gla_w_in=m_gla_w_in, gla_w_gk2=m_gla_w_gk2,
             gla_b_gk=m_gla_b_gk, gla_g_norm=m_gla_g_norm, gla_w_out=m_gla_w_out, att_w_in=m_att_w_in,
             att_b_in=m_att_b_in, att_rel_bias=m_att_rel_bias, att_w_out=m_att_w_out, ff_w1=m_ff_w1, ff_w2=m_ff_w2)
    v = dict(w_ada=v_w_ada, b_ada=v_b_ada, ln_g=v_ln_g, ln_b=v_ln_b, gla_w_in=v_gla_w_in, gla_w_gk2=v_gla_w_gk2,
             gla_b_gk=v_gla_b_gk, gla_g_norm=v_gla_g_norm, gla_w_out=v_gla_w_out, att_w_in=v_att_w_in,
             att_b_in=v_att_b_in, att_rel_bias=v_att_rel_bias, att_w_out=v_att_w_out, ff_w1=v_ff_w1, ff_w2=v_ff_w2)
    xi, yi, ci = _my_place()
    me = 4 * xi + 2 * yi + ci

    small_names = tuple(SMALL_SHARDED)
    small_in = _pack_small([c] + [w[n] for n in small_names])
    small_all = all_gather(small_in, name="gather_small").reshape(N_DEV, -1, 128)
    parts = _unpack_small(small_all, [c.shape] + [w[n].shape for n in small_names])
    c_all = parts[0].reshape(N_DEV, D)
    sm = {"gla_b_gk": gla_b_gk, "att_rel_bias": att_rel_bias}
    for n, p in zip(small_names, parts[1:]):
        ax = SMALL_SHARDED[n]
        sm[n] = jnp.moveaxis(p, 0, ax).reshape(p.shape[1:ax + 1] + (N_DEV * p.shape[ax + 1],) + p.shape[ax + 2:])

    n_ada = w_ada.shape[2]
    mp = mods_partial(c_all, w_ada, name="mods_partial")
    mp_all = all_gather(mp.reshape(DEPTH * N_DEV, n_ada), name="gather_mods")
    mp_all = mp_all.reshape(N_DEV, DEPTH, N_DEV, n_ada)
    mods = lax.dynamic_index_in_dim(mp_all, me, axis=2, keepdims=False)
    mods = mods.transpose(1, 0, 2).reshape(DEPTH, 6 * D) + b_ada
    mods = mods.reshape(DEPTH, 6, D)

    loss, dx, dmods, gs, results = _trunk(x.reshape(x.shape[1:]), loss_target.reshape(x.shape[1:]), mods, sm, w, m, v)
    loss = lax.psum(loss[0, 0], ("x", "y", "c"))

    dm_flat = dmods.reshape(DEPTH, 6 * D)
    small_g = [dm_flat] + [gs[n].reshape(sm[n].shape) for n in SMALL_GRADS]
    small_shapes = [a.shape for a in small_g]
    sg_all = all_gather(_pack_small(small_g), name="gather_small_grads").reshape(N_DEV, -1, 128)
    summed = _unpack_small(sum_parts(sg_all, name="sum_small_grads"), small_shapes)
    g_full = dict(zip(("b_ada",) + SMALL_GRADS, summed))
    dm_all = _unpack_small(sg_all, small_shapes)[0]
    dm_mine = lax.dynamic_slice_in_dim(dm_all, me * n_ada, n_ada, axis=2).transpose(1, 0, 2)
    g_w_ada = w_ada_grad(c_all, dm_mine, name="w_ada_grad")

    results["w_ada"] = adamw_nd(w_ada, m_w_ada, v_w_ada, g_w_ada[None], name="adamw_w_ada")
    for n in ("b_ada",) + SMALL_GRADS:
        g = g_full[n]
        if n in SMALL_SHARDED:
            ax = SMALL_SHARDED[n]
            width = w[n].shape[ax]
            g = lax.dynamic_slice_in_dim(g, me * width, width, axis=ax)
        results[n] = adamw_nd(w[n], m[n], v[n], g[None], name=f"adamw_{n}")

    out = [loss, dx[None]]
    for k in range(4):
        out += [results[n][k] for n in WEIGHTS]
    return tuple(out)
```

```python
import numpy as np
import jax
import jax.numpy as jnp
from jax import lax
from jax.experimental import pallas as pl
from jax.experimental.pallas import tpu as pltpu

F32 = jnp.float32
BF16 = jnp.bfloat16
HIGHEST = lax.Precision.HIGHEST
MESH = pl.DeviceIdType.MESH

N_DEV = 8
D = 1024
DEPTH = 4
CHUNK = 64
ALPHA = (2.0 * DEPTH) ** 0.25
LN_EPS = 1e-5
RMS_EPS = 1e-6
NEG_INF = -1e30

GLA_H = 4
GLA_DKH = 128
GLA_DVH = 256
GLA_DK = GLA_H * GLA_DKH
GLA_DV = GLA_H * GLA_DVH
GLA_RANK = 16
GLA_IN = 2 * GLA_DK + 2 * GLA_DV + GLA_RANK
GLA_INP = 3200
GLA_TAU_INV = 1.0 / 16.0
GLA_SUB = 2

ATT_H = 16
ATT_HD = 64
ATT_QB = 256
ATT_KB = 3 * ATT_QB
LEFT = 8 * CHUNK
MAX_REL = 128
N_REL = 2 * MAX_REL + 1
REL_PAD = 384
REL_TILE = 128
REL_TILES = (3, 4)
D_FF = 4 * D

ADAM_LR = 0.001
ADAM_B1 = 0.9
ADAM_B2 = 0.999
ADAM_EPS = 1e-08
ADAM_WD = 0.01
ADAM_STEP = 10

VMEM_LIMIT = 48 * 1024 * 1024


def _params(n_axes):
    return pltpu.CompilerParams(dimension_semantics=("arbitrary",) * n_axes, vmem_limit_bytes=VMEM_LIMIT)


def _dot(a, b):
    return jnp.dot(a, b, preferred_element_type=F32)


def _dot_nt(a, b):
    return lax.dot_general(a, b, (((1,), (1,)), ((), ())), preferred_element_type=F32)


def _dot_tn(a, b):
    return lax.dot_general(a, b, (((0,), (0,)), ((), ())), preferred_element_type=F32)


def _bf(a):
    return a.astype(BF16)


def _prologue(kind, a, p1=None, p2=None):
    if kind == "mod":
        return a.astype(F32) * (1.0 + p1) + p2
    if kind == "scale":
        return a.astype(F32) * (1.0 + p1)
    if kind == "relu2":
        r = jnp.maximum(a, 0.0)
        return r * r
    return a


class Hosted:
    def __init__(self, inputs, out_shapes, sems, first, last):
        self.inputs, self.out_shapes, self.sems, self.first, self.last = inputs, out_shapes, sems, first, last


def _hbm_specs(n):
    return [pl.BlockSpec(memory_space=pltpu.HBM)] * n


def _call_hosting(body, comm, *, first, last, in_specs, out_specs, out_shape, scratch_shapes, args, **kw):
    if comm is None:
        return pl.pallas_call(body, in_specs=in_specs, out_specs=out_specs, out_shape=out_shape,
                              scratch_shapes=scratch_shapes, **kw)(*args), []
    n_in, n_out, n_scr = len(in_specs), len(out_specs), len(scratch_shapes)
    n_ci, n_co = len(comm.inputs), len(comm.out_shapes)

    def hosting(*refs):
        ins, ci = refs[:n_in], refs[n_in:n_in + n_ci]
        k = n_in + n_ci
        outs, co = refs[k:k + n_out], refs[k + n_out:k + n_out + n_co]
        k += n_out + n_co
        scr, cs = refs[k:k + n_scr], refs[k + n_scr:]

        @pl.when(first())
        def _():
            comm.first(ci, co, cs)

        body(*ins, *outs, *scr)

        @pl.when(last())
        def _():
            comm.last(ci, co, cs)

    res = pl.pallas_call(
        hosting, in_specs=list(in_specs) + _hbm_specs(n_ci), out_specs=list(out_specs) + _hbm_specs(n_co),
        out_shape=list(out_shape) + list(comm.out_shapes), scratch_shapes=list(scratch_shapes) + list(comm.sems),
        **kw)(*args, *comm.inputs)
    return res[:n_out], res[n_out:]


def run_hosted(comm, *, name):
    n_i, n_o = len(comm.inputs), len(comm.out_shapes)

    def body(*refs):
        ins, outs, sems = refs[:n_i], refs[n_i:n_i + n_o], refs[n_i + n_o:]
        comm.first(ins, outs, sems)
        comm.last(ins, outs, sems)

    return pl.pallas_call(body, name=name, out_shape=list(comm.out_shapes), in_specs=_hbm_specs(n_i),
                          out_specs=_hbm_specs(n_o), scratch_shapes=list(comm.sems))(*comm.inputs)


def mm_nn(a, b, layer, *, pro=None, p1=None, p2=None, bias=None, out_dtype=F32, tm, tn, comm=None, name):
    M, K = a.shape
    N = b.shape[2]
    tm = min(tm, M)
    n_p = {"mod": 2, "scale": 1}.get(pro, 0)
    has_bias = bias is not None
    direct = pro is None and a.dtype == BF16

    def body(*refs):
        a_ref, b_ref = refs[0], refs[1]
        p_refs = refs[2:2 + n_p]
        bias_ref = refs[2 + n_p] if has_bias else None
        if direct:
            o_ref = refs[-1]
            lhs = a_ref[...]
        else:
            o_ref, abf = refs[-2], refs[-1]

            @pl.when(pl.program_id(1) == 0)
            def _():
                abf[...] = _bf(_prologue(pro, a_ref[...].astype(F32), *[r[...] for r in p_refs]))

            lhs = abf[...]
        acc = _dot(lhs, b_ref[...])
        if has_bias:
            acc = acc + bias_ref[...]
        o_ref[...] = acc.astype(out_dtype)

    in_specs = [pl.BlockSpec((tm, K), lambda i, j: (i, 0)), pl.BlockSpec((None, K, tn), lambda i, j: (layer, 0, j))]
    args = [a, b]
    for p in (p1, p2)[:n_p]:
        in_specs.append(pl.BlockSpec((1, K), lambda i, j: (0, 0)))
        args.append(p)
    if has_bias:
        in_specs.append(pl.BlockSpec((1, tn), lambda i, j: (0, j)))
        args.append(bias)
    n_i, n_j = M // tm, N // tn
    (out,), got = _call_hosting(
        body, comm, first=lambda: (pl.program_id(0) == 0) & (pl.program_id(1) == 0),
        last=lambda: (pl.program_id(0) == n_i - 1) & (pl.program_id(1) == n_j - 1),
        name=name, grid=(n_i, n_j), in_specs=in_specs,
        out_specs=[pl.BlockSpec((tm, tn), lambda i, j: (i, j))],
        out_shape=[jax.ShapeDtypeStruct((M, N), out_dtype)],
        scratch_shapes=[] if direct else [pltpu.VMEM((tm, K), BF16)], compiler_params=_params(2), args=args)
    return out if comm is None else (out, got)


def mm_nn_ksplit(a, b, layer, *, pro=None, tm, tk, name):
    M, K = a.shape
    N = b.shape[2]
    tm = min(tm, M)

    def body(a_ref, b_ref, o_ref):
        part = _dot(_bf(_prologue(pro, a_ref[...])), b_ref[...])

        @pl.when(pl.program_id(1) == 0)
        def _():
            o_ref[...] = part

        @pl.when(pl.program_id(1) > 0)
        def _():
            o_ref[...] += part

    return pl.pallas_call(
        body, name=name, grid=(M // tm, K // tk),
        in_specs=[pl.BlockSpec((tm, tk), lambda i, k: (i, k)), pl.BlockSpec((None, tk, N), lambda i, k: (layer, k, 0))],
        out_specs=pl.BlockSpec((tm, N), lambda i, k: (i, 0)),
        out_shape=jax.ShapeDtypeStruct((M, N), F32), compiler_params=_params(2),
    )(a, b)


def mm_nt(a_parts, w, layer, *, pro=None, p1=None, epi_h=None, out_dtype=F32, tm, tn, comm=None, name):
    M = a_parts[0].shape[0]
    tm = min(tm, M)
    widths = [p.shape[1] for p in a_parts]
    Nw = sum(widths)
    Kw = w.shape[1]
    n_a = len(a_parts)
    has_p = pro == "scale"
    has_h = epi_h is not None
    direct = n_a == 1 and not has_p and a_parts[0].dtype == BF16

    def body(*refs):
        a_refs = refs[:n_a]
        w_ref = refs[n_a]
        k = n_a + 1
        p_ref = refs[k] if has_p else None
        k += int(has_p)
        h_ref = refs[k] if has_h else None
        if direct:
            o_ref = refs[-1]
            lhs = a_refs[0][...]
        else:
            o_ref, abf = refs[-2], refs[-1]

            @pl.when(pl.program_id(1) == 0)
            def _():
                off = 0
                for r, wd in zip(a_refs, widths):
                    av = r[...]
                    if has_p:
                        av = av.astype(F32) * (1.0 + p_ref[...])
                    abf[:, off:off + wd] = _bf(av)
                    off += wd

            lhs = abf[...]
        acc = _dot_nt(lhs, w_ref[...])
        if has_h:
            acc = acc * (2.0 * jnp.maximum(h_ref[...], 0.0))
        o_ref[...] = acc.astype(out_dtype)

    in_specs = [pl.BlockSpec((tm, wd), lambda i, j: (i, 0)) for wd in widths]
    in_specs.append(pl.BlockSpec((None, tn, Nw), lambda i, j: (layer, j, 0)))
    args = list(a_parts) + [w]
    if has_p:
        in_specs.append(pl.BlockSpec((1, Nw), lambda i, j: (0, 0)))
        args.append(p1)
    if has_h:
        in_specs.append(pl.BlockSpec((tm, tn), lambda i, j: (i, j)))
        args.append(epi_h)
    n_i, n_j = M // tm, Kw // tn
    (out,), got = _call_hosting(
        body, comm, first=lambda: (pl.program_id(0) == 0) & (pl.program_id(1) == 0),
        last=lambda: (pl.program_id(0) == n_i - 1) & (pl.program_id(1) == n_j - 1),
        name=name, grid=(n_i, n_j), in_specs=in_specs,
        out_specs=[pl.BlockSpec((tm, tn), lambda i, j: (i, j))],
        out_shape=[jax.ShapeDtypeStruct((M, Kw), out_dtype)],
        scratch_shapes=[] if direct else [pltpu.VMEM((tm, Nw), BF16)], compiler_params=_params(2), args=args)
    return out if comm is None else (out, got)


def mm_tn(a, d, *, pro=None, p1=None, p2=None, dscale=None, tk, tn, tm, out_buf, out_shape, col_block0=0, name):
    M, Kf = a.shape
    N = d.shape[1]
    n_p = {"mod": 2}.get(pro, 0)
    has_ds = dscale is not None
    has_buf = out_buf is not None
    n_m = M // tm

    def body(*refs):
        a_ref, d_ref = refs[0], refs[1]
        p_refs = refs[2:2 + n_p]
        ds_ref = refs[2 + n_p] if has_ds else None
        o_ref, acc = refs[-2], refs[-1]
        m = pl.program_id(2)

        av = _prologue(pro, a_ref[...], *[r[...] for r in p_refs])
        dv = d_ref[...]
        if has_ds:
            dv = dv.astype(F32) * (1.0 + ds_ref[...])
        part = _dot_tn(_bf(av), _bf(dv))

        @pl.when(m == 0)
        def _():
            acc[...] = part

        @pl.when(m > 0)
        def _():
            acc[...] += part

        @pl.when(m == n_m - 1)
        def _():
            o_ref[...] = _bf(acc[...])

    in_specs = [pl.BlockSpec((tm, tk), lambda i, j, m: (m, i)), pl.BlockSpec((tm, tn), lambda i, j, m: (m, j))]
    args = [a, d]
    for p in (p1, p2)[:n_p]:
        in_specs.append(pl.BlockSpec((1, tk), lambda i, j, m: (0, i)))
        args.append(p)
    if has_ds:
        in_specs.append(pl.BlockSpec((1, tn), lambda i, j, m: (0, j)))
        args.append(dscale)
    aliases = {}
    if has_buf:
        in_specs.append(pl.BlockSpec(memory_space=pl.ANY))
        args.append(out_buf)
        aliases = {len(args) - 1: 0}
    return pl.pallas_call(
        body, name=name, grid=(Kf // tk, N // tn, n_m), in_specs=in_specs,
        out_specs=pl.BlockSpec((None, tk, tn), lambda i, j, m: (0, i, col_block0 + j)),
        out_shape=jax.ShapeDtypeStruct(out_shape, BF16), input_output_aliases=aliases,
        scratch_shapes=[pltpu.VMEM((tk, tn), F32)], compiler_params=_params(3),
    )(*args)


ROW_BLOCK = 512
ACC_ROWS = 8


def _ln_stats(z):
    mu = jnp.mean(z, axis=-1, keepdims=True)
    zc = z - mu
    var = jnp.mean(zc * zc, axis=-1, keepdims=True)
    return zc, lax.rsqrt(var + LN_EPS)


def ln_fwd(x, y, gate, lng, lnb, *, name):
    S = x.shape[0]

    def body(x_ref, y_ref, gt_ref, g_ref, b_ref, o_ref):
        z = ALPHA * x_ref[...] + (1.0 + gt_ref[...]) * y_ref[...]
        zc, rstd = _ln_stats(z)
        o_ref[...] = (zc * rstd) * g_ref[...] + b_ref[...]

    row = pl.BlockSpec((ROW_BLOCK, D), lambda i: (i, 0))
    vec = pl.BlockSpec((1, D), lambda i: (0, 0))
    return pl.pallas_call(
        body, name=name, grid=(S // ROW_BLOCK,), in_specs=[row, row, vec, vec, vec], out_specs=row,
        out_shape=jax.ShapeDtypeStruct((S, D), F32), compiler_params=_params(1),
    )(x, y, gate, lng, lnb)


def _add_colsum(acc_ref, r, val):
    acc_ref[r:r + 1, :] += jnp.sum(val, axis=0, keepdims=True)


def ln_bwd(x_in, y, gate, lng, *, dout=None, nxt=None, name):
    S = x_in.shape[0]
    has_next = nxt is not None

    def body(*refs):
        if has_next:
            dzn_ref, dun_ref, scn_ref, xo_ref = refs[:4]
            k = 4
        else:
            do_ref = refs[0]
            k = 1
        x_ref, y_ref, gt_ref, g_ref = refs[k:k + 4]
        dz_ref, acc_ref = refs[k + 4:]

        @pl.when(pl.program_id(0) == 0)
        def _():
            acc_ref[...] = jnp.zeros_like(acc_ref)

        if has_next:
            du = dun_ref[...]
            dout_v = ALPHA * dzn_ref[...] + du * (1.0 + scn_ref[...])
            _add_colsum(acc_ref, 3, du * xo_ref[...])
            _add_colsum(acc_ref, 4, du)
        else:
            dout_v = do_ref[...]
        yv = y_ref[...]
        z = ALPHA * x_ref[...] + (1.0 + gt_ref[...]) * yv
        zc, rstd = _ln_stats(z)
        xhat = zc * rstd
        _add_colsum(acc_ref, 0, dout_v * xhat)
        _add_colsum(acc_ref, 1, dout_v)
        dxh = dout_v * g_ref[...]
        m1 = jnp.mean(dxh, axis=-1, keepdims=True)
        m2 = jnp.mean(dxh * xhat, axis=-1, keepdims=True)
        dz = rstd * (dxh - m1 - xhat * m2)
        _add_colsum(acc_ref, 2, dz * yv)
        dz_ref[...] = dz

    row = pl.BlockSpec((ROW_BLOCK, D), lambda i: (i, 0))
    vec = pl.BlockSpec((1, D), lambda i: (0, 0))
    if has_next:
        in_specs = [row, row, vec, row]
        args = list(nxt)
    else:
        in_specs = [row]
        args = [dout]
    in_specs += [row, row, vec, vec]
    args += [x_in, y, gate, lng]
    return pl.pallas_call(
        body, name=name, grid=(S // ROW_BLOCK,), in_specs=in_specs,
        out_specs=[row, pl.BlockSpec((ACC_ROWS, D), lambda i: (0, 0))],
        out_shape=[jax.ShapeDtypeStruct((S, D), F32), jax.ShapeDtypeStruct((ACC_ROWS, D), F32)],
        compiler_params=_params(1),
    )(*args)


def combine_final(dz, du, sc, x_in, *, name):
    S = dz.shape[0]

    def body(dz_ref, du_ref, sc_ref, x_ref, dx_ref, acc_ref):
        @pl.when(pl.program_id(0) == 0)
        def _():
            acc_ref[...] = jnp.zeros_like(acc_ref)

        du_v = du_ref[...]
        dx_ref[...] = ALPHA * dz_ref[...] + du_v * (1.0 + sc_ref[...])
        _add_colsum(acc_ref, 3, du_v * x_ref[...])
        _add_colsum(acc_ref, 4, du_v)

    row = pl.BlockSpec((ROW_BLOCK, D), lambda i: (i, 0))
    vec = pl.BlockSpec((1, D), lambda i: (0, 0))
    return pl.pallas_call(
        body, name=name, grid=(S // ROW_BLOCK,), in_specs=[row, row, vec, row],
        out_specs=[row, pl.BlockSpec((ACC_ROWS, D), lambda i: (0, 0))],
        out_shape=[jax.ShapeDtypeStruct((S, D), F32), jax.ShapeDtypeStruct((ACC_ROWS, D), F32)],
        compiler_params=_params(1),
    )(dz, du, sc, x_in)


def colsum(a, *, name):
    S, N = a.shape

    def body(a_ref, o_ref):
        @pl.when(pl.program_id(0) == 0)
        def _():
            o_ref[...] = jnp.zeros_like(o_ref)

        o_ref[...] += jnp.sum(a_ref[...].astype(F32), axis=0, keepdims=True)

    return pl.pallas_call(
        body, name=name, grid=(S // ROW_BLOCK,), in_specs=[pl.BlockSpec((ROW_BLOCK, N), lambda i: (i, 0))],
        out_specs=pl.BlockSpec((1, N), lambda i: (0, 0)), out_shape=jax.ShapeDtypeStruct((1, N), F32),
        compiler_params=_params(1),
    )(a)


def loss_head(y, t, *, name):
    S = y.shape[0]

    def body(y_ref, t_ref, dy_ref, l_ref):
        @pl.when(pl.program_id(0) == 0)
        def _():
            l_ref[...] = jnp.zeros_like(l_ref)

        e = y_ref[...] - t_ref[...]
        dy_ref[...] = e * (1.0 / D)
        per_tok = jnp.sum(e * e, axis=1, keepdims=True) * (1.0 / D)
        l_ref[...] += 0.5 * jnp.sum(per_tok, axis=0, keepdims=True)

    row = pl.BlockSpec((ROW_BLOCK, D), lambda i: (i, 0))
    return pl.pallas_call(
        body, name=name, grid=(S // ROW_BLOCK,), in_specs=[row, row],
        out_specs=[row, pl.BlockSpec((8, 128), lambda i: (0, 0))],
        out_shape=[jax.ShapeDtypeStruct((S, D), F32), jax.ShapeDtypeStruct((8, 128), F32)],
        compiler_params=_params(1),
    )(y, t)


def _log_sigmoid(x):
    return jnp.minimum(x, 0.0) - jnp.log(1.0 + jnp.exp(-jnp.abs(x)))


def _silu(x):
    return x * (1.0 / (1.0 + jnp.exp(-x)))


def _cumsum_steps(x):
    row = lax.broadcasted_iota(jnp.int32, x.shape, 0)
    step = 1
    while step < x.shape[0]:
        x = x + jnp.where(row >= step, pltpu.roll(x, step, 0), 0.0)
        step *= 2
    return x


@jax.custom_vjp
def _cumsum_rows(x):
    return _cumsum_steps(x)


def _cumsum_rows_fwd(x):
    return _cumsum_steps(x), None


def _cumsum_rows_bwd(_, g):
    return (jnp.sum(g, axis=0, keepdims=True) - _cumsum_steps(g) + g,)


_cumsum_rows.defvjp(_cumsum_rows_fwd, _cumsum_rows_bwd)


def _gla_chunk(q, k, v, g, gk, s0t, w2p, bgk, gn):
    C = q.shape[0]
    row = lax.broadcasted_iota(jnp.int32, (C, C), 0)
    col = lax.broadcasted_iota(jnp.int32, (C, C), 1)
    lower = row >= col
    la = _log_sigmoid(_dot(_bf(gk), _bf(w2p)) + bgk) * GLA_TAU_INV
    outs, states = [], []
    for h in range(GLA_H):
        ks = slice(h * GLA_DKH, (h + 1) * GLA_DKH)
        vs = slice(h * GLA_DVH, (h + 1) * GLA_DVH)
        qh = q[:, ks] * (GLA_DKH ** -0.5)
        kh, vh, gh, lah, s0 = k[:, ks], v[:, vs], g[:, vs], la[:, ks], s0t[h]
        cum = _cumsum_rows(lah)
        e_pos = jnp.exp(cum)
        e_neg = jnp.exp(-cum)
        q_f = qh * e_pos
        a_f = _dot_nt(_bf(q_f), _bf(kh * e_neg))
        a_b = _dot_nt(_bf(qh * e_neg), _bf(kh * e_pos))
        att = jnp.where(lower, a_f, a_b)
        o = _dot(_bf(att), _bf(vh)) + _dot_nt(_bf(q_f), _bf(s0))
        tot = jnp.sum(lah, axis=0, keepdims=True)
        k_end = kh * jnp.exp(tot - cum)
        states.append(s0 * jnp.exp(tot) + _dot_tn(_bf(vh), _bf(k_end)))
        on = o * lax.rsqrt(jnp.mean(o * o, axis=-1, keepdims=True) + RMS_EPS) * gn[:, vs]
        outs.append(on * _silu(gh))
    return jnp.concatenate(outs, axis=1), tuple(states)


def _gla_split(p):
    return (p[:, 0:GLA_DK], p[:, GLA_DK:2 * GLA_DK], p[:, 2 * GLA_DK:2 * GLA_DK + GLA_DV],
            p[:, 2 * GLA_DK + GLA_DV:2 * GLA_DK + 2 * GLA_DV], p[:, 2 * GLA_DK + 2 * GLA_DV:GLA_INP])


def gla_fwd(proj, w2p, bgk, gn, *, comm=None, name):
    S = proj.shape[0]
    n_c = S // CHUNK
    n_s = n_c // GLA_SUB
    rows = GLA_SUB * CHUNK

    def body(p_ref, w_ref, b_ref, gn_ref, o_ref, st_ref, st):
        @pl.when(pl.program_id(0) == 0)
        def _():
            st[...] = jnp.zeros_like(st)

        s = tuple(st[h] for h in range(GLA_H))
        for u in range(GLA_SUB):
            sub = slice(u * CHUNK, (u + 1) * CHUNK)
            for h in range(GLA_H):
                st_ref[u, h] = s[h]
            og, s = _gla_chunk(*_gla_split(p_ref[sub, :]), s, w_ref[...], b_ref[...], gn_ref[...])
            o_ref[sub, :] = _bf(og)
        for h in range(GLA_H):
            st[h] = s[h]

    full = lambda shape: pl.BlockSpec(shape, lambda i: (0,) * len(shape))
    return _call_hosting(
        body, comm, first=lambda: pl.program_id(0) == 0, last=lambda: pl.program_id(0) == n_s - 1,
        name=name, grid=(n_s,),
        in_specs=[pl.BlockSpec((rows, GLA_INP), lambda i: (i, 0)), full((128, GLA_DK)), full((1, GLA_DK)),
                  full((1, GLA_DV))],
        out_specs=[pl.BlockSpec((rows, GLA_DV), lambda i: (i, 0)),
                   pl.BlockSpec((GLA_SUB, GLA_H, GLA_DVH, GLA_DKH), lambda i: (i, 0, 0, 0))],
        out_shape=[jax.ShapeDtypeStruct((S, GLA_DV), BF16),
                   jax.ShapeDtypeStruct((n_c, GLA_H, GLA_DVH, GLA_DKH), F32)],
        scratch_shapes=[pltpu.VMEM((GLA_H, GLA_DVH, GLA_DKH), F32)], compiler_params=_params(1),
        args=(proj, w2p, bgk, gn))


def gla_bwd(proj, dog, states, w2p, bgk, gn, *, comm=None, name):
    S = proj.shape[0]
    n_c = S // CHUNK
    n_s = n_c // GLA_SUB
    rows = GLA_SUB * CHUNK

    def body(p_ref, dog_ref, st_ref, w_ref, b_ref, gn_ref, dp_ref, dw_ref, db_ref, dgn_ref, ds_ref):
        @pl.when(pl.program_id(0) == 0)
        def _():
            ds_ref[...] = jnp.zeros_like(ds_ref)
            dw_ref[...] = jnp.zeros_like(dw_ref)
            db_ref[...] = jnp.zeros_like(db_ref)
            dgn_ref[...] = jnp.zeros_like(dgn_ref)

        ds = tuple(ds_ref[h] for h in range(GLA_H))
        for u in reversed(range(GLA_SUB)):
            sub = slice(u * CHUNK, (u + 1) * CHUNK)
            q, k, v, g, gk = _gla_split(p_ref[sub, :])
            s0 = tuple(st_ref[u, h] for h in range(GLA_H))
            _, vjp = jax.vjp(_gla_chunk, q, k, v, g, gk, s0, w_ref[...], b_ref[...], gn_ref[...])
            dq, dk, dv, dg, dgk, ds, dw, db, dgn = vjp((dog_ref[sub, :], ds))
            dp_ref[sub, 0:GLA_DK] = _bf(dq)
            dp_ref[sub, GLA_DK:2 * GLA_DK] = _bf(dk)
            dp_ref[sub, 2 * GLA_DK:2 * GLA_DK + GLA_DV] = _bf(dv)
            dp_ref[sub, 2 * GLA_DK + GLA_DV:2 * GLA_DK + 2 * GLA_DV] = _bf(dg)
            dp_ref[sub, 2 * GLA_DK + 2 * GLA_DV:GLA_INP] = _bf(dgk)
            dw_ref[...] += dw
            db_ref[...] += db
            dgn_ref[...] += dgn
        for h in range(GLA_H):
            ds_ref[h] = ds[h]

    full = lambda shape: pl.BlockSpec(shape, lambda i: (0,) * len(shape))
    rev = lambda i: (n_s - 1 - i, 0)
    return _call_hosting(
        body, comm, first=lambda: pl.program_id(0) == 0, last=lambda: pl.program_id(0) == n_s - 1,
        name=name, grid=(n_s,),
        in_specs=[pl.BlockSpec((rows, GLA_INP), rev), pl.BlockSpec((rows, GLA_DV), rev),
                  pl.BlockSpec((GLA_SUB, GLA_H, GLA_DVH, GLA_DKH), lambda i: (n_s - 1 - i, 0, 0, 0)),
                  full((128, GLA_DK)), full((1, GLA_DK)), full((1, GLA_DV))],
        out_specs=[pl.BlockSpec((rows, GLA_INP), rev), full((128, GLA_DK)), full((1, GLA_DK)), full((1, GLA_DV))],
        out_shape=[jax.ShapeDtypeStruct((S, GLA_INP), BF16), jax.ShapeDtypeStruct((128, GLA_DK), F32),
                   jax.ShapeDtypeStruct((1, GLA_DK), F32), jax.ShapeDtypeStruct((1, GLA_DV), F32)],
        scratch_shapes=[pltpu.VMEM((GLA_H, GLA_DVH, GLA_DKH), F32)], compiler_params=_params(1),
        args=(proj, dog, states, w2p, bgk, gn))


def _rel_index():
    t = np.arange(REL_TILE)[:, None]
    j = np.arange(REL_TILE)[None, :]
    tiles = []
    for m in REL_TILES:
        chunks = (REL_TILE // CHUNK) * m + j // CHUNK - t // CHUNK
        band = (chunks >= 0) & (chunks <= LEFT // CHUNK)
        dist = LEFT - REL_TILE * m + t - j
        tiles.append(np.where(band, np.minimum(dist, MAX_REL) + MAX_REL, N_REL))
    return jnp.asarray(np.stack(tiles).reshape(1, -1).astype(np.int32))


REL_BLOCK = 2048


def _one_hot(idx_row):
    return (lax.broadcasted_iota(jnp.int32, (REL_PAD, idx_row.shape[1]), 0) == idx_row).astype(F32)


def rel_bias_tiles(rel_pad, idx, *, name):
    E = idx.shape[1]

    def body(r_ref, i_ref, o_ref):
        o_ref[...] = jnp.dot(r_ref[...], _one_hot(i_ref[...]), precision=HIGHEST, preferred_element_type=F32)

    return pl.pallas_call(
        body, name=name, grid=(E // REL_BLOCK,),
        in_specs=[pl.BlockSpec((ATT_H, REL_PAD), lambda i: (0, 0)), pl.BlockSpec((1, REL_BLOCK), lambda i: (0, i))],
        out_specs=pl.BlockSpec((ATT_H, REL_BLOCK), lambda i: (0, i)),
        out_shape=jax.ShapeDtypeStruct((ATT_H, E), F32), compiler_params=_params(1),
    )(rel_pad, idx)


def rel_bias_grad(dtiles_flat, dclip, idx, *, name):
    E = idx.shape[1]
    n_steps = E // REL_BLOCK

    def body(d_ref, c_ref, i_ref, o_ref):
        @pl.when(pl.program_id(0) == 0)
        def _():
            o_ref[...] = jnp.zeros_like(o_ref)

        o_ref[...] += lax.dot_general(d_ref[...], _one_hot(i_ref[...]), (((1,), (1,)), ((), ())),
                                      precision=HIGHEST, preferred_element_type=F32)

        @pl.when(pl.program_id(0) == n_steps - 1)
        def _():
            at_clip = lax.broadcasted_iota(jnp.int32, (1, REL_PAD), 1) == 2 * MAX_REL
            o_ref[...] += jnp.where(at_clip, jnp.sum(c_ref[...], axis=1, keepdims=True), 0.0)

    return pl.pallas_call(
        body, name=name, grid=(n_steps,),
        in_specs=[pl.BlockSpec((ATT_H, REL_BLOCK), lambda i: (0, i)), pl.BlockSpec((ATT_H, 128), lambda i: (0, 0)),
                  pl.BlockSpec((1, REL_BLOCK), lambda i: (0, i))],
        out_specs=pl.BlockSpec((ATT_H, REL_PAD), lambda i: (0, 0)),
        out_shape=jax.ShapeDtypeStruct((ATT_H, REL_PAD), F32), compiler_params=_params(1),
    )(dtiles_flat, dclip, idx)


def _attn_bias(tiles, clip):
    const = jnp.broadcast_to(clip, (REL_TILE, REL_TILE))
    zero = jnp.zeros((REL_TILE, REL_TILE), F32)
    rows = []
    for qt in range(ATT_QB // REL_TILE):
        blocks = []
        for kt in range(ATT_KB // REL_TILE):
            m = kt - qt
            if m in REL_TILES:
                blocks.append(tiles[REL_TILES.index(m)])
            elif 0 <= m < REL_TILES[0]:
                blocks.append(const)
            else:
                blocks.append(zero)
        rows.append(jnp.concatenate(blocks, axis=1))
    return jnp.concatenate(rows, axis=0)


def _attn_bias_grad(ds, dt_ref, dc_ref, a):
    tile = lambda qt, kt: ds[qt * REL_TILE:(qt + 1) * REL_TILE, kt * REL_TILE:(kt + 1) * REL_TILE]
    const = None
    sums = [None] * len(REL_TILES)
    for qt in range(ATT_QB // REL_TILE):
        for kt in range(ATT_KB // REL_TILE):
            m = kt - qt
            if m in REL_TILES:
                n = REL_TILES.index(m)
                sums[n] = tile(qt, kt) if sums[n] is None else sums[n] + tile(qt, kt)
            elif 0 <= m < REL_TILES[0]:
                const = tile(qt, kt) if const is None else const + tile(qt, kt)
    for n, v in enumerate(sums):
        dt_ref[a, n] += v
    dc_ref[a] += jnp.sum(const, axis=0, keepdims=True)


def _attn_head_lanes():
    lane = lax.broadcasted_iota(jnp.int32, (1, 2 * ATT_HD), 1)
    return [(lane >= a * ATT_HD) & (lane < (a + 1) * ATT_HD) for a in range(2)]


def _attn_band_bias(tiles, clip):
    j = lax.broadcasted_iota(jnp.int32, (ATT_QB, ATT_KB), 1)
    t = lax.broadcasted_iota(jnp.int32, (ATT_QB, ATT_KB), 0)
    shift = CHUNK.bit_length() - 1
    chunks = lax.shift_right_logical(j, shift) - lax.shift_right_logical(t, shift)
    band = (chunks >= 0) & (chunks <= LEFT // CHUNK)
    return jnp.where(band, _attn_bias(tiles, clip), NEG_INF)


def _attn_exp(qa, kb, bias, key_bias):
    s = _dot_nt(qa, kb) + bias + key_bias
    e = jnp.exp(s - jnp.max(s, axis=-1, keepdims=True))
    return e, jnp.sum(e, axis=-1, keepdims=True)


def _attn_specs():
    n_hp = ATT_H // 2
    q_spec = pl.BlockSpec((ATT_QB, 128), lambda hp, g: (g, hp))

    def win(col0, back):
        return pl.BlockSpec((ATT_QB, 128), lambda hp, g: (jnp.maximum(g - back, 0), col0 + hp))

    kv_specs = [win(n_hp, 2), win(n_hp, 1), win(n_hp, 0), win(2 * n_hp, 2), win(2 * n_hp, 1), win(2 * n_hp, 0)]
    tiles_spec = pl.BlockSpec((2, len(REL_TILES), REL_TILE, REL_TILE), lambda hp, g: (hp, 0, 0, 0))
    clip_spec = pl.BlockSpec((2, 1, 128), lambda hp, g: (hp, 0, 0))
    return q_spec, kv_specs, tiles_spec, clip_spec


def _attn_window(refs, g):
    kb = jnp.concatenate([_bf(r[...]) for r in refs[0:3]], axis=0)
    vb = jnp.concatenate([_bf(r[...]) for r in refs[3:6]], axis=0)
    j = lax.broadcasted_iota(jnp.int32, (1, ATT_KB), 1)
    return kb, vb, jnp.where(j + (g - 2) * ATT_QB >= 0, 0.0, NEG_INF)


def attn_fwd(qkv, tiles, clip, *, comm=None, name):
    S = qkv.shape[0]
    q_spec, kv_specs, tiles_spec, clip_spec = _attn_specs()

    def body(q_ref, *rest):
        kv_refs, t_ref, c_ref, o_ref, bias = rest[:6], rest[6], rest[7], rest[8], rest[9]
        g = pl.program_id(1)

        @pl.when(g == 0)
        def _():
            for a in range(2):
                bias[a] = _attn_band_bias(t_ref[a], c_ref[a])

        kb, vb, key_bias = _attn_window(kv_refs, g)
        q = q_ref[...].astype(F32)
        out = jnp.zeros((ATT_QB, 2 * ATT_HD), F32)
        for a, lanes in enumerate(_attn_head_lanes()):
            mf = lanes.astype(F32)
            e, l = _attn_exp(_bf(q * (mf * ATT_HD ** -0.5)), kb, bias[a], key_bias)
            out = out + _dot(_bf(e), vb) * (mf * (1.0 / l))
        o_ref[...] = _bf(out)

    n_hp, n_g = ATT_H // 2, S // ATT_QB
    return _call_hosting(
        body, comm, first=lambda: (pl.program_id(0) == 0) & (pl.program_id(1) == 0),
        last=lambda: (pl.program_id(0) == n_hp - 1) & (pl.program_id(1) == n_g - 1),
        name=name, grid=(n_hp, n_g), in_specs=[q_spec] + kv_specs + [tiles_spec, clip_spec],
        out_specs=[q_spec], out_shape=[jax.ShapeDtypeStruct((S, D), BF16)],
        scratch_shapes=[pltpu.VMEM((2, ATT_QB, ATT_KB), F32)], compiler_params=_params(2),
        args=(*([qkv] * 7), tiles, clip))


def attn_bwd(qkv, do, tiles, clip, *, comm=None, name):
    S = qkv.shape[0]
    q_spec, kv_specs, tiles_spec, clip_spec = _attn_specs()
    col_spec = pl.BlockSpec((S, 128), lambda hp, g: (0, hp))
    sum_spec = pl.BlockSpec((1, 128), lambda hp, g: (0, hp))
    n_g = S // ATT_QB

    def body(q_ref, *rest):
        kv_refs, t_ref, c_ref, do_ref = rest[:6], rest[6], rest[7], rest[8]
        dq_ref, dk_ref, dv_ref, dt_ref, dc_ref, sq_ref, sk_ref, sv_ref, bias = rest[9:]
        g = pl.program_id(1)

        @pl.when(g == 0)
        def _():
            for a in range(2):
                bias[a] = _attn_band_bias(t_ref[a], c_ref[a])
            dk_ref[...] = jnp.zeros_like(dk_ref)
            dv_ref[...] = jnp.zeros_like(dv_ref)
            dt_ref[...] = jnp.zeros_like(dt_ref)
            dc_ref[...] = jnp.zeros_like(dc_ref)
            sq_ref[...] = jnp.zeros_like(sq_ref)

        kb, vb, key_bias = _attn_window(kv_refs, g)
        q = q_ref[...].astype(F32)
        do = do_ref[...]
        dq = jnp.zeros((ATT_QB, 2 * ATT_HD), F32)
        dkw = jnp.zeros((ATT_KB, 2 * ATT_HD), F32)
        dvw = jnp.zeros((ATT_KB, 2 * ATT_HD), F32)
        for a, lanes in enumerate(_attn_head_lanes()):
            mf = lanes.astype(F32) * ATT_HD ** -0.5
            qa = _bf(q * mf)
            e, l = _attn_exp(qa, kb, bias[a], key_bias)
            p = e * (1.0 / l)
            do_a = jnp.where(lanes, do, jnp.zeros_like(do))
            dp = _dot_nt(do_a, vb)
            ds = p * (dp - jnp.sum(p * dp, axis=-1, keepdims=True))
            ds_b = _bf(ds)
            dq = dq + _dot(ds_b, kb) * mf
            dkw = dkw + _dot_tn(ds_b, qa)
            dvw = dvw + _dot_tn(_bf(p), do_a)
            _attn_bias_grad(ds, dt_ref, dc_ref, a)
        dq_ref[...] = _bf(dq)
        sq_ref[...] += jnp.sum(dq, axis=0, keepdims=True)
        for blk in range(3):
            src = g - 2 + blk

            @pl.when(src >= 0)
            def _(blk=blk, src=src):
                rows = pl.ds(pl.multiple_of(src * ATT_QB, ATT_QB), ATT_QB)
                dk_ref[rows, :] += dkw[blk * ATT_QB:(blk + 1) * ATT_QB]
                dv_ref[rows, :] += dvw[blk * ATT_QB:(blk + 1) * ATT_QB]

        @pl.when(g == n_g - 1)
        def _():
            sk_ref[...] = jnp.sum(dk_ref[...], axis=0, keepdims=True)
            sv_ref[...] = jnp.sum(dv_ref[...], axis=0, keepdims=True)

    n_hp, n_g = ATT_H // 2, S // ATT_QB
    return _call_hosting(
        body, comm, first=lambda: (pl.program_id(0) == 0) & (pl.program_id(1) == 0),
        last=lambda: (pl.program_id(0) == n_hp - 1) & (pl.program_id(1) == n_g - 1),
        name=name, grid=(n_hp, n_g),
        in_specs=[q_spec] + kv_specs + [tiles_spec, clip_spec, q_spec],
        out_specs=[q_spec, col_spec, col_spec, tiles_spec, clip_spec] + [sum_spec] * 3,
        out_shape=[jax.ShapeDtypeStruct((S, D), BF16)] + [jax.ShapeDtypeStruct((S, D), F32)] * 2
        + [jax.ShapeDtypeStruct((ATT_H, len(REL_TILES), REL_TILE, REL_TILE), F32),
           jax.ShapeDtypeStruct((ATT_H, 1, 128), F32)] + [jax.ShapeDtypeStruct((1, D), F32)] * 3,
        scratch_shapes=[pltpu.VMEM((2, ATT_QB, ATT_KB), F32)], compiler_params=_params(2),
        args=(*([qkv] * 7), tiles, clip, do))


def mods_partial(c_all, w_ada, *, name):
    n_l, _, n_c = w_ada.shape

    def body(c_ref, w_ref, o_ref):
        o_ref[...] = _dot(_bf(_silu(c_ref[...])), _bf(w_ref[...]))

    return pl.pallas_call(
        body, name=name, grid=(n_l,),
        in_specs=[pl.BlockSpec((N_DEV, D), lambda l: (0, 0)), pl.BlockSpec((None, D, n_c), lambda l: (l, 0, 0))],
        out_specs=pl.BlockSpec((None, N_DEV, n_c), lambda l: (l, 0, 0)),
        out_shape=jax.ShapeDtypeStruct((n_l, N_DEV, n_c), F32), compiler_params=_params(1),
    )(c_all, w_ada)


def w_ada_grad(c_all, dm, *, name):
    n_l, _, n_c = dm.shape

    def body(c_ref, d_ref, o_ref):
        o_ref[...] = lax.dot_general(_silu(c_ref[...]), d_ref[...], (((0,), (0,)), ((), ())),
                                     precision=HIGHEST, preferred_element_type=F32)

    return pl.pallas_call(
        body, name=name, grid=(n_l,),
        in_specs=[pl.BlockSpec((N_DEV, D), lambda l: (0, 0)), pl.BlockSpec((None, N_DEV, n_c), lambda l: (l, 0, 0))],
        out_specs=pl.BlockSpec((None, D, n_c), lambda l: (l, 0, 0)),
        out_shape=jax.ShapeDtypeStruct((n_l, D, n_c), F32), compiler_params=_params(1),
    )(c_all, dm)


def adamw(w, m, v, gparts, *, block_rows, name):
    R, C = w.shape
    n = gparts.shape[0]

    def body(w_ref, m_ref, v_ref, g_ref, go_ref, d_ref, mo_ref, vo_ref):
        g = g_ref[0].astype(F32)
        for k in range(1, n):
            g = g + g_ref[k].astype(F32)
        m_new = ADAM_B1 * m_ref[...] + (1.0 - ADAM_B1) * g
        v_new = ADAM_B2 * v_ref[...] + (1.0 - ADAM_B2) * (g * g)
        m_hat = m_new / (1.0 - ADAM_B1 ** ADAM_STEP)
        v_hat = v_new / (1.0 - ADAM_B2 ** ADAM_STEP)
        go_ref[...] = g
        d_ref[...] = -ADAM_LR * (m_hat / (jnp.sqrt(v_hat) + ADAM_EPS) + ADAM_WD * w_ref[...])
        mo_ref[...] = m_new
        vo_ref[...] = v_new

    blk = pl.BlockSpec((block_rows, C), lambda i: (i, 0))
    return pl.pallas_call(
        body, name=name, grid=(R // block_rows,),
        in_specs=[blk, blk, blk, pl.BlockSpec((n, block_rows, C), lambda i: (0, i, 0))],
        out_specs=[blk] * 4, out_shape=[jax.ShapeDtypeStruct((R, C), F32)] * 4, compiler_params=_params(1),
    )(w, m, v, gparts)


def adamw_nd(w, m, v, gparts, *, name):
    shape = w.shape
    two = (int(np.prod(shape[:-1])), shape[-1])
    rows = two[0]
    block_rows = rows
    for cand in (512, 256):
        if rows > cand and rows % cand == 0:
            block_rows = cand
            break
    outs = adamw(w.reshape(two), m.reshape(two), v.reshape(two), gparts.reshape((gparts.shape[0],) + two),
                 block_rows=block_rows, name=name)
    return [o.reshape(shape) for o in outs]


def sum_parts(parts, *, name):
    n, R, C = parts.shape

    def body(p_ref, o_ref):
        acc = p_ref[0]
        for k in range(1, n):
            acc = acc + p_ref[k]
        o_ref[...] = acc

    return pl.pallas_call(
        body, name=name, in_specs=[pl.BlockSpec((n, R, C), lambda: (0, 0, 0))],
        out_specs=pl.BlockSpec((R, C), lambda: (0, 0)), out_shape=jax.ShapeDtypeStruct((R, C), F32),
        compiler_params=pltpu.CompilerParams(vmem_limit_bytes=VMEM_LIMIT),
    )(parts)


def _my_place():
    return lax.axis_index("x"), lax.axis_index("y"), lax.axis_index("c")


def _full_shape(kind, shard):
    n_l, rows, cols = shard
    return {"col": (n_l, rows, N_DEV * cols), "row": (n_l, N_DEV * rows, cols), "stk": (N_DEV, n_l, rows, cols)}[kind]


def _slab(ref, kind, dev, shard):
    _, rows, cols = shard
    if kind == "col":
        return ref.at[:, :, pl.ds(pl.multiple_of(dev * cols, 128), cols)]
    if kind == "row":
        return ref.at[:, pl.ds(pl.multiple_of(dev * rows, 8), rows), :]
    return ref.at[dev]


def _hbm_specs(n):
    return [pl.BlockSpec(memory_space=pltpu.HBM)] * n


def all_gather(x_shard, *, name):
    m_per, n = x_shard.shape

    def body(x_ref, out_ref, send_sems, recv_sems, local_sem):
        x, y, c = _my_place()
        me, sibling = (x, y, c), (x, y, 1 - c)
        chips = [(1 - x, y), (x, 1 - y), (1 - x, 1 - y)]

        def rows(px, py, pc):
            return out_ref.at[pl.ds((4 * px + 2 * py + pc) * m_per, m_per), :]

        def copy(k, block, to, src=None):
            return pltpu.make_async_remote_copy(
                src_ref=rows(*block) if src is None else src, dst_ref=rows(*block),
                send_sem=send_sems.at[k], recv_sem=recv_sems.at[k], device_id=to, device_id_type=MESH)

        mine = pltpu.make_async_copy(x_ref, rows(*me), local_sem)
        mine.start()
        first = [copy(0, me, sibling, src=x_ref)]
        first += [copy(1 + j, me, (*chip, c), src=x_ref) for j, chip in enumerate(chips)]
        for cp in first:
            cp.start()
        passed = [copy(4 + j, (*chip, c), sibling) for j, chip in enumerate(chips)]
        for j, chip in enumerate(chips):
            copy(1 + j, (*chip, c), me).wait_recv()
            passed[j].start()
        copy(0, sibling, me).wait_recv()
        for j, chip in enumerate(chips):
            copy(4 + j, (*chip, 1 - c), me).wait_recv()
        for cp in first + passed:
            cp.wait_send()
        mine.wait()

    return pl.pallas_call(
        body, name=name, out_shape=jax.ShapeDtypeStruct((N_DEV * m_per, n), x_shard.dtype),
        in_specs=[pl.BlockSpec(memory_space=pltpu.VMEM)], out_specs=pl.BlockSpec(memory_space=pltpu.VMEM),
        scratch_shapes=[pltpu.SemaphoreType.DMA((7,)), pltpu.SemaphoreType.DMA((7,)), pltpu.SemaphoreType.DMA],
        compiler_params=pltpu.CompilerParams(vmem_limit_bytes=VMEM_LIMIT),
    )(x_shard)


def gather_plan(shards, kinds, layers):
    n_t = len(shards)
    shapes = [(1,) + tuple(s.shape[1:]) for s in shards]

    def copies(x_refs, out_refs, sems):
        send_sems, recv_sems, local_sems = sems
        x, y, c = _my_place()
        me, sibling = (x, y, c), (x, y, 1 - c)
        chips = [(1 - x, y), (x, 1 - y), (1 - x, 1 - y)]
        own = [x_refs[t].at[pl.ds(layers[t], 1)] for t in range(n_t)]

        def slab(t, px, py, pc):
            return _slab(out_refs[t], kinds[t], 4 * px + 2 * py + pc, shapes[t])

        def copy(t, k, block, to, src=None):
            return pltpu.make_async_remote_copy(
                src_ref=slab(t, *block) if src is None else src, dst_ref=slab(t, *block),
                send_sem=send_sems.at[7 * t + k], recv_sem=recv_sems.at[7 * t + k], device_id=to,
                device_id_type=MESH)

        mine = [pltpu.make_async_copy(own[t], slab(t, *me), local_sems.at[t]) for t in range(n_t)]
        sends = []
        for t in range(n_t):
            sends.append(copy(t, 0, me, sibling, src=own[t]))
            sends += [copy(t, 1 + j, me, (*chip, c), src=own[t]) for j, chip in enumerate(chips)]
        return mine, sends, copy, me, sibling, chips, c

    def first(x_refs, out_refs, sems):
        mine, sends = copies(x_refs, out_refs, sems)[:2]
        for cp in mine + sends:
            cp.start()

    def last(x_refs, out_refs, sems):
        mine, sends, copy, me, sibling, chips, c = copies(x_refs, out_refs, sems)
        passed = []
        for j, chip in enumerate(chips):
            for t in range(n_t):
                copy(t, 1 + j, (*chip, c), me).wait_recv()
                passed.append(copy(t, 4 + j, (*chip, c), sibling))
                passed[-1].start()
        for t in range(n_t):
            copy(t, 0, sibling, me).wait_recv()
        for j, chip in enumerate(chips):
            for t in range(n_t):
                copy(t, 4 + j, (*chip, 1 - c), me).wait_recv()
        for cp in sends + passed:
            cp.wait_send()
        for cp in mine:
            cp.wait()

    return Hosted(
        list(shards), [jax.ShapeDtypeStruct(_full_shape(k, shp), s.dtype) for k, shp, s in zip(kinds, shapes, shards)],
        [pltpu.SemaphoreType.DMA((7 * n_t,)), pltpu.SemaphoreType.DMA((7 * n_t,)), pltpu.SemaphoreType.DMA((n_t,))],
        first, last)


def scatter_plan(grads, kinds, shapes):
    n_t = len(grads)

    def copies(g_refs, out_refs, sems):
        send_sems, recv_sems, local_sems = sems
        x, y, c = _my_place()
        me = 4 * x + 2 * y + c
        local = [pltpu.make_async_copy(_slab(g_refs[t], kinds[t], me, shapes[t]), out_refs[t].at[me],
                                       local_sems.at[t]) for t in range(n_t)]
        remote = []
        for t in range(n_t):
            for r in range(1, N_DEV):
                px = 1 - x if r & 4 else x
                py = 1 - y if r & 2 else y
                pc = 1 - c if r & 1 else c
                remote.append(pltpu.make_async_remote_copy(
                    src_ref=_slab(g_refs[t], kinds[t], 4 * px + 2 * py + pc, shapes[t]), dst_ref=out_refs[t].at[me],
                    send_sem=send_sems.at[7 * t + r - 1], recv_sem=recv_sems.at[7 * t + r - 1],
                    device_id=(px, py, pc), device_id_type=MESH))
        return local, remote

    def first(g_refs, out_refs, sems):
        local, remote = copies(g_refs, out_refs, sems)
        for cp in local + remote:
            cp.start()

    def last(g_refs, out_refs, sems):
        local, remote = copies(g_refs, out_refs, sems)
        for cp in remote + local:
            cp.wait()

    return Hosted(
        list(grads), [jax.ShapeDtypeStruct((N_DEV,) + tuple(s), BF16) for s in shapes],
        [pltpu.SemaphoreType.DMA((7 * n_t,)), pltpu.SemaphoreType.DMA((7 * n_t,)), pltpu.SemaphoreType.DMA((n_t,))],
        first, last)


def adamw_layer(w, m, v, parts, layer, bufs, *, name):
    n_l, rows, cols = w.shape
    n = parts.shape[0]
    tr = min(rows, 256)

    def body(w_ref, m_ref, v_ref, g_ref, *rest):
        go_ref, d_ref, mo_ref, vo_ref = rest[-4:]
        g = g_ref[0].astype(F32)
        for k in range(1, n):
            g = g + g_ref[k].astype(F32)
        m_new = ADAM_B1 * m_ref[...] + (1.0 - ADAM_B1) * g
        v_new = ADAM_B2 * v_ref[...] + (1.0 - ADAM_B2) * (g * g)
        m_hat = m_new / (1.0 - ADAM_B1 ** ADAM_STEP)
        v_hat = v_new / (1.0 - ADAM_B2 ** ADAM_STEP)
        go_ref[...] = g
        d_ref[...] = -ADAM_LR * (m_hat / (jnp.sqrt(v_hat) + ADAM_EPS) + ADAM_WD * w_ref[...])
        mo_ref[...] = m_new
        vo_ref[...] = v_new

    blk = pl.BlockSpec((None, tr, cols), lambda i: (layer, i, 0))
    in_specs = [blk, blk, blk, pl.BlockSpec((n, None, tr, cols), lambda i: (0, 0, i, 0))]
    args = [w, m, v, parts]
    aliases = {}
    if bufs is not None:
        in_specs += [pl.BlockSpec(memory_space=pl.ANY)] * 4
        args += list(bufs)
        aliases = {4 + k: k for k in range(4)}
    return pl.pallas_call(
        body, name=name, grid=(rows // tr,), in_specs=in_specs, out_specs=[blk] * 4,
        out_shape=[jax.ShapeDtypeStruct((n_l, rows, cols), F32)] * 4, input_output_aliases=aliases,
        compiler_params=_params(1),
    )(*args)


BIG =("gla_w_in", "gla_w_out", "att_w_in", "att_w_out", "ff_w1", "ff_w2")
KIND = {"gla_w_in": "stk", "gla_w_out": "row", "att_w_in": "col", "att_w_out": "row", "ff_w1": "col", "ff_w2": "row"}


def _pack_small(arrs):
    parts = []
    for a in arrs:
        f = a.reshape(-1)
        parts.append(jnp.pad(f, (0, -f.shape[0] % 128)))
    flat = jnp.concatenate(parts)
    flat = jnp.pad(flat, (0, -flat.shape[0] % 1024))
    return flat.reshape(-1, 128)


def _unpack_small(packed, shapes):
    flat = packed.reshape(packed.shape[:-2] + (-1,))
    out, off = [], 0
    for shp in shapes:
        n = int(np.prod(shp))
        out.append(flat[..., off:off + n].reshape(packed.shape[:-2] + tuple(shp)))
        off += n + (-n % 128)
    return out


def _vec(a):
    return a.reshape(1, -1)


def _layer_weights(i):
    mixer = "gla" if i % 2 == 0 else "att"
    return [(f"{mixer}_w_in", i // 2), (f"{mixer}_w_out", i // 2), ("ff_w1", i), ("ff_w2", i)]


def _trunk(x, target, mods, sm, w, m, v):
    shard_bf = {n: w[n].astype(BF16) for n in BIG}

    def gather_of(names):
        return gather_plan([shard_bf[n] for n, _ in names], [KIND[n] for n, _ in names], [l for _, l in names])

    def gather_layer(i):
        return _layer_weights(i), gather_of(_layer_weights(i))

    wts = {}

    def keep_gathered(names, arrays):
        for (n, l), a in zip(names, arrays):
            if KIND[n] == "stk":
                a = a.transpose(1, 2, 0, 3).reshape(1, D, GLA_IN)
                a = jnp.pad(a, ((0, 0), (0, 0), (0, GLA_INP - GLA_IN)))
            wts[n, l] = a

    first_names, rest_names = _layer_weights(0)[:1], _layer_weights(0)[1:]
    keep_gathered(first_names, run_hosted(gather_of(first_names), name="gather_first"))

    rel_idx = _rel_index()
    saved = []
    for i in range(DEPTH):
        sh1, sc1, g1, sh2, sc2, g2 = [mods[i, k:k + 1] for k in range(6)]
        rec = {"x0": x}
        j = i // 2
        nxt_names, nxt_plan = gather_layer(i + 1) if i + 1 < DEPTH else (None, None)
        if i % 2 == 0:
            w2p = jnp.pad(sm["gla_w_gk2"][j], ((0, 128 - GLA_RANK), (0, 0)))
            bgk, gn = _vec(sm["gla_b_gk"][j]), _vec(sm["gla_g_norm"][j])
            if i == 0:
                proj, got0 = mm_nn(x, wts["gla_w_in", j], 0, pro="mod", p1=sc1, p2=sh1, tm=1024, tn=640,
                                   comm=gather_of(rest_names), name=f"gla_proj_{i}")
                keep_gathered(rest_names, got0)
            else:
                proj = mm_nn(x, wts["gla_w_in", j], 0, pro="mod", p1=sc1, p2=sh1, tm=1024, tn=640,
                             name=f"gla_proj_{i}")
            (og, states), got = gla_fwd(proj, w2p, bgk, gn, comm=nxt_plan, name=f"gla_core_{i}")
            y = mm_nn(og, wts["gla_w_out", j], 0, tm=1024, tn=1024, name=f"gla_out_{i}")
            rec.update(kind="gla", j=j, w2p=w2p, bgk=bgk, gn=gn, proj=proj, og=og, states=states)
        else:
            rel = sm["att_rel_bias"][j]
            rel_pad = jnp.pad(rel, ((0, 0), (0, REL_PAD - N_REL)), constant_values=NEG_INF)
            tiles = rel_bias_tiles(rel_pad, rel_idx, name=f"att_bias_{i}")
            tiles = tiles.reshape(ATT_H, len(REL_TILES), REL_TILE, REL_TILE)
            clip = jnp.broadcast_to(rel[:, 2 * MAX_REL][:, None, None], (ATT_H, 1, 128))
            qkv = mm_nn(x, wts["att_w_in", j], 0, pro="mod", p1=sc1, p2=sh1, bias=_vec(sm["att_b_in"][j]),
                        out_dtype=BF16, tm=1024, tn=1024, name=f"att_proj_{i}")
            (o,), got = attn_fwd(qkv, tiles, clip, comm=nxt_plan, name=f"att_core_{i}")
            y = mm_nn(o, wts["att_w_out", j], 0, tm=1024, tn=1024, name=f"att_out_{i}")
            rec.update(kind="att", j=j, tiles=tiles, clip=clip, qkv=qkv, o=o)
        if nxt_plan is not None:
            keep_gathered(nxt_names, got)
        x1 = ln_fwd(x, y, g1, _vec(sm["ln_g"][i, 0]), _vec(sm["ln_b"][i, 0]), name=f"ln_mix_{i}")
        h = mm_nn(x1, wts["ff_w1", i], 0, pro="mod", p1=sc2, p2=sh2, out_dtype=BF16, tm=2048, tn=1024,
                  name=f"ff_up_{i}")
        y2 = mm_nn_ksplit(h, wts["ff_w2", i], 0, pro="relu2", tm=1024, tk=1024, name=f"ff_down_{i}")
        x2 = ln_fwd(x1, y2, g2, _vec(sm["ln_g"][i, 1]), _vec(sm["ln_b"][i, 1]), name=f"ln_ff_{i}")
        rec.update(y=y, x1=x1, h=h, y2=y2)
        saved.append(rec)
        x = x2

    dy, loss = loss_head(x, target, name="loss_head")

    gw = {}

    def wgrad(weight, layer, a, d, *, tn, tk=1024, tm=512, col_block0=0, **kw):
        gw[weight, layer] = mm_tn(a, d, tk=tk, tn=tn, tm=tm, out_buf=gw.get((weight, layer)),
                                  out_shape=wts[weight, layer].shape, col_block0=col_block0, **kw)

    def scatter_layer(units):
        grads = []
        for n, l in units:
            g = gw[n, l]
            if KIND[n] == "stk":
                g = g[:, :, :GLA_IN].reshape(1, D, N_DEV, GLA_IN // N_DEV).transpose(2, 0, 1, 3)
            grads.append(g)
        return scatter_plan(grads, [KIND[n] for n, _ in units], [(1,) + tuple(w[n].shape[1:]) for n, _ in units])

    results = {n: None for n in BIG}

    def update(units, parts):
        for (n, l), p in zip(units, parts):
            results[n] = adamw_layer(w[n], m[n], v[n], p, l, results[n], name=f"adamw_{n}_{l}")

    gs = {"ln_g": [[None, None] for _ in range(DEPTH)], "ln_b": [[None, None] for _ in range(DEPTH)],
          "gla_w_gk2": [None] * 2, "gla_b_gk": [None] * 2, "gla_g_norm": [None] * 2, "att_b_in": [None] * 2,
          "att_rel_bias": [None] * 2}
    dmods = [[None] * 6 for _ in range(DEPTH)]
    nxt = None
    nxt_slot = None
    for i in reversed(range(DEPTH)):
        rec = saved[i]
        sh1, sc1, g1, sh2, sc2, g2 = [mods[i, k:k + 1] for k in range(6)]
        x0, x1 = rec["x0"], rec["x1"]
        if nxt is None:
            dz2, acc = ln_bwd(x1, rec["y2"], g2, _vec(sm["ln_g"][i, 1]), dout=dy, name=f"ln_ff_bwd_{i}")
        else:
            dz2, acc = ln_bwd(x1, rec["y2"], g2, _vec(sm["ln_g"][i, 1]), nxt=nxt, name=f"ln_ff_bwd_{i}")
            dmods[nxt_slot[0]][nxt_slot[1]] = acc[3]
            dmods[nxt_slot[0]][nxt_slot[2]] = acc[4]
        gs["ln_g"][i][1], gs["ln_b"][i][1], dmods[i][5] = acc[0], acc[1], acc[2]
        dh = mm_nt([dz2], wts["ff_w2", i], 0, pro="scale", p1=g2, epi_h=rec["h"], out_dtype=BF16, tm=2048, tn=512,
                   name=f"ff_down_bwd_{i}")
        wgrad("ff_w2", i, rec["h"], dz2, pro="relu2", dscale=g2, tk=2048, tn=1024, name=f"ff_w2_grad_{i}")
        du2 = mm_nt([dh], wts["ff_w1", i], 0, tm=1024, tn=1024, name=f"ff_up_bwd_{i}")
        wgrad("ff_w1", i, x1, dh, pro="mod", p1=sc2, p2=sh2, tn=2048, name=f"ff_w1_grad_{i}")
        dz1, acc = ln_bwd(x0, rec["y"], g1, _vec(sm["ln_g"][i, 0]), nxt=(dz2, du2, sc2, x1), name=f"ln_mix_bwd_{i}")
        dmods[i][4], dmods[i][3] = acc[3], acc[4]
        gs["ln_g"][i][0], gs["ln_b"][i][0], dmods[i][2] = acc[0], acc[1], acc[2]
        j = rec["j"]
        w_in, w_out = _layer_weights(i)[:2]
        if rec["kind"] == "gla":
            dog = mm_nt([dz1], wts[w_out], 0, pro="scale", p1=g1, tm=1024, tn=1024, name=f"gla_out_bwd_{i}")
            wgrad(*w_out, rec["og"], dz1, dscale=g1, tn=1024, name=f"gla_wout_grad_{i}")
        else:
            do = mm_nt([dz1], wts[w_out], 0, pro="scale", p1=g1, out_dtype=BF16, tm=1024, tn=1024,
                       name=f"att_out_bwd_{i}")
            wgrad(*w_out, rec["o"], dz1, dscale=g1, tn=1024, name=f"att_wout_grad_{i}")
        units = [("ff_w1", i), ("ff_w2", i), w_out] + ([_layer_weights(i + 1)[0]] if i + 1 < DEPTH else [])
        plan = scatter_layer(units)
        if rec["kind"] == "gla":
            (dproj, dw2p, dbgk, dgn), parts = gla_bwd(rec["proj"], dog, rec["states"], rec["w2p"], rec["bgk"],
                                                      rec["gn"], comm=plan, name=f"gla_core_bwd_{i}")
            gs["gla_w_gk2"][j], gs["gla_b_gk"][j], gs["gla_g_norm"][j] = dw2p[:GLA_RANK], dbgk[0], dgn[0]
            wgrad(*w_in, x0, dproj, pro="mod", p1=sc1, p2=sh1, tk=512, tn=GLA_INP, name=f"gla_win_grad_{i}")
            if i == 0:
                du1, last_parts = mm_nt([dproj], wts[w_in], 0, tm=1024, tn=1024, comm=scatter_layer([w_in]),
                                        name=f"gla_proj_bwd_{i}")
                update([w_in], last_parts)
            else:
                du1 = mm_nt([dproj], wts[w_in], 0, tm=1024, tn=1024, name=f"gla_proj_bwd_{i}")
        else:
            (dq, dk, dv, dtiles, dclip, sq, sk, sv), parts = attn_bwd(rec["qkv"], do, rec["tiles"], rec["clip"],
                                                                      comm=plan, name=f"att_core_bwd_{i}")
            drel = rel_bias_grad(dtiles.reshape(ATT_H, -1), dclip.reshape(ATT_H, 128), rel_idx,
                                 name=f"att_bias_grad_{i}")
            gs["att_rel_bias"][j] = drel[:, :N_REL]
            gs["att_b_in"][j] = jnp.concatenate([sq[0], sk[0], sv[0]])
            du1 = mm_nt([dq, dk, dv], wts[w_in], 0, tm=512, tn=1024, name=f"att_proj_bwd_{i}")
            for n, t in enumerate((dq, dk, dv)):
                wgrad(*w_in, x0, t, pro="mod", p1=sc1, p2=sh1, tn=1024, col_block0=n, name=f"att_win_grad_{i}_{n}")
        update(units, parts)
        nxt = (dz1, du1, sc1, x0)
        nxt_slot = (i, 1, 0)
    dx, acc = combine_final(nxt[0], nxt[1], nxt[2], nxt[3], name="grad_x")
    dmods[0][1], dmods[0][0] = acc[3], acc[4]
    dmods = jnp.stack([jnp.stack(r) for r in dmods])
    gs = {k: jnp.stack([jnp.stack(r) if isinstance(r, list) else r for r in v]) for k, v in gs.items()}
    return loss, dx, dmods, gs, results


WEIGHTS = ("w_ada", "b_ada", "ln_g", "ln_b", "gla_w_in", "gla_w_gk2", "gla_b_gk", "gla_g_norm", "gla_w_out",
           "att_w_in", "att_b_in", "att_rel_bias", "att_w_out", "ff_w1", "ff_w2")
SMALL_SHARDED = {"ln_g": 2, "ln_b": 2, "gla_w_gk2": 2, "gla_g_norm": 2, "att_b_in": 1}
SMALL_GRADS = ("ln_g", "ln_b", "gla_w_gk2", "gla_b_gk", "gla_g_norm", "att_b_in", "att_rel_bias")


def kernel(x, c, w_ada, b_ada, ln_g, ln_b, gla_w_in, gla_w_gk2, gla_b_gk, gla_g_norm, gla_w_out, att_w_in, att_b_in, att_rel_bias, att_w_out, ff_w1, ff_w2, loss_target, m_w_ada, m_b_ada, m_ln_g, m_ln_b, m_gla_w_in, m_gla_w_gk2, m_gla_b_gk, m_gla_g_norm, m_gla_w_out, m_att_w_in, m_att_b_in, m_att_rel_bias, m_att_w_out, m_ff_w1, m_ff_w2, v_w_ada, v_b_ada, v_ln_g, v_ln_b, v_gla_w_in, v_gla_w_gk2, v_gla_b_gk, v_gla_g_norm, v_gla_w_out, v_att_w_in, v_att_b_in, v_att_rel_bias, v_att_w_out, v_ff_w1, v_ff_w2):
    w = dict(w_ada=w_ada, b_ada=b_ada, ln_g=ln_g, ln_b=ln_b, gla_w_in=gla_w_in, gla_w_gk2=gla_w_gk2,
             gla_b_gk=gla_b_gk, gla_g_norm=gla_g_norm, gla_w_out=gla_w_out, att_w_in=att_w_in, att_b_in=att_b_in,
             att_rel_bias=att_rel_bias, att_w_out=att_w_out, ff_w1=ff_w1, ff_w2=ff_w2)
    m = dict(w_ada=m_w_ada, b_ada=m_b_ada, ln_g=m_ln_g, ln_b=m_ln_b, gla_w_in=m_gla_w_in, gla_w_gk2=m_gla_w_gk2,
             gla_b_gk=m_gla_b_gk, gla_g_norm=m_gla_g_norm, gla_w_out=m_gla_w_out, att_w_in=m_att_w_in,
             att_b_in=m_att_b_in, att_rel_bias=m_att_rel_bias, att_w_out=m_att_w_out, ff_w1=m_ff_w1, ff_w2=m_ff_w2)
    v = dict(w_ada=v_w_ada, b_ada=v_b_ada, ln_g=v_ln_g, ln_b=v_ln_b, gla_w_in=v_gla_w_in, gla_w_gk2=v_gla_w_gk2,
             gla_b_gk=v_gla_b_gk, gla_g_norm=v_gla_g_norm, gla_w_out=v_gla_w_out, att_w_in=v_att_w_in,
             att_b_in=v_att_b_in, att_rel_bias=v_att_rel_bias, att_w_out=v_att_w_out, ff_w1=v_ff_w1, ff_w2=v_ff_w2)
    xi, yi, ci = _my_place()
    me = 4 * xi + 2 * yi + ci

    small_names = tuple(SMALL_SHARDED)
    small_in = _pack_small([c] + [w[n] for n in small_names])
    small_all = all_gather(small_in, name="gather_small").reshape(N_DEV, -1, 128)
    parts = _unpack_small(small_all, [c.shape] + [w[n].shape for n in small_names])
    c_all = parts[0].reshape(N_DEV, D)
    sm = {"gla_b_gk": gla_b_gk, "att_rel_bias": att_rel_bias}
    for n, p in zip(small_names, parts[1:]):
        ax = SMALL_SHARDED[n]
        sm[n] = jnp.moveaxis(p, 0, ax).reshape(p.shape[1:ax + 1] + (N_DEV * p.shape[ax + 1],) + p.shape[ax + 2:])

    n_ada = w_ada.shape[2]
    mp = mods_partial(c_all, w_ada, name="mods_partial")
    mp_all = all_gather(mp.reshape(DEPTH * N_DEV, n_ada), name="gather_mods")
    mp_all = mp_all.reshape(N_DEV, DEPTH, N_DEV, n_ada)
    mods = lax.dynamic_index_in_dim(mp_all, me, axis=2, keepdims=False)
    mods = mods.transpose(1, 0, 2).reshape(DEPTH, 6 * D) + b_ada
    mods = mods.reshape(DEPTH, 6, D)

    loss, dx, dmods, gs, results = _trunk(x.reshape(x.shape[1:]), loss_target.reshape(x.shape[1:]), mods, sm, w, m, v)
    loss = lax.psum(loss[0, 0], ("x", "y", "c"))

    dm_flat = dmods.reshape(DEPTH, 6 * D)
    small_g = [dm_flat] + [gs[n].reshape(sm[n].shape) for n in SMALL_GRADS]
    small_shapes = [a.shape for a in small_g]
    sg_all = all_gather(_pack_small(small_g), name="gather_small_grads").reshape(N_DEV, -1, 128)
    summed = _unpack_small(sum_parts(sg_all, name="sum_small_grads"), small_shapes)
    g_full = dict(zip(("b_ada",) + SMALL_GRADS, summed))
    dm_all = _unpack_small(sg_all, small_shapes)[0]
    dm_mine = lax.dynamic_slice_in_dim(dm_all, me * n_ada, n_ada, axis=2).transpose(1, 0, 2)
    g_w_ada = w_ada_grad(c_all, dm_mine, name="w_ada_grad")

    results["w_ada"] = adamw_nd(w_ada, m_w_ada, v_w_ada, g_w_ada[None], name="adamw_w_ada")
    for n in ("b_ada",) + SMALL_GRADS:
        g = g_full[n]
        if n in SMALL_SHARDED:
            ax = SMALL_SHARDED[n]
            width = w[n].shape[ax]
            g = lax.dynamic_slice_in_dim(g, me * width, width, axis=ax)
        results[n] = adamw_nd(w[n], m[n], v[n], g[None], name=f"adamw_{n}")

    out = [loss, dx[None]]
    for k in range(4):
        out += [results[n][k] for n in WEIGHTS]
    return tuple(out)
```

```python
import numpy as np
import jax
import jax.numpy as jnp
from jax import lax
from jax.experimental import pallas as pl
from jax.experimental.pallas import tpu as pltpu

F32 = jnp.float32
BF16 = jnp.bfloat16
HIGHEST = lax.Precision.HIGHEST
MESH = pl.DeviceIdType.MESH

N_DEV = 8
D = 1024
DEPTH = 4
CHUNK = 64
ALPHA = (2.0 * DEPTH) ** 0.25
LN_EPS = 1e-5
RMS_EPS = 1e-6
NEG_INF = -1e30

GLA_H = 4
GLA_DKH = 128
GLA_DVH = 256
GLA_DK = GLA_H * GLA_DKH
GLA_DV = GLA_H * GLA_DVH
GLA_RANK = 16
GLA_IN = 2 * GLA_DK + 2 * GLA_DV + GLA_RANK
GLA_INP = 3200
GLA_TAU_INV = 1.0 / 16.0
GLA_SUB = 2

ATT_H = 16
ATT_HD = 64
ATT_QB = 256
ATT_KB = 3 * ATT_QB
LEFT = 8 * CHUNK
MAX_REL = 128
N_REL = 2 * MAX_REL + 1
REL_PAD = 384
REL_TILE = 128
REL_TILES = (3, 4)
D_FF = 4 * D

ADAM_LR = 0.001
ADAM_B1 = 0.9
ADAM_B2 = 0.999
ADAM_EPS = 1e-08
ADAM_WD = 0.01
ADAM_STEP = 10

VMEM_LIMIT = 48 * 1024 * 1024


def _params(n_axes):
    return pltpu.CompilerParams(dimension_semantics=("arbitrary",) * n_axes, vmem_limit_bytes=VMEM_LIMIT)


def _dot(a, b):
    return jnp.dot(a, b, preferred_element_type=F32)


def _dot_nt(a, b):
    return lax.dot_general(a, b, (((1,), (1,)), ((), ())), preferred_element_type=F32)


def _dot_tn(a, b):
    return lax.dot_general(a, b, (((0,), (0,)), ((), ())), preferred_element_type=F32)


def _bf(a):
    return a.astype(BF16)


def _prologue(kind, a, p1=None, p2=None):
    if kind == "mod":
        return a.astype(F32) * (1.0 + p1) + p2
    if kind == "scale":
        return a.astype(F32) * (1.0 + p1)
    if kind == "relu2":
        r = jnp.maximum(a, 0.0)
        return r * r
    return a


class Hosted:
    def __init__(self, inputs, out_shapes, sems, first, last):
        self.inputs, self.out_shapes, self.sems, self.first, self.last = inputs, out_shapes, sems, first, last


def _hbm_specs(n):
    return [pl.BlockSpec(memory_space=pltpu.HBM)] * n


def _call_hosting(body, comm, *, first, last, in_specs, out_specs, out_shape, scratch_shapes, args, **kw):
    if comm is None:
        return pl.pallas_call(body, in_specs=in_specs, out_specs=out_specs, out_shape=out_shape,
                              scratch_shapes=scratch_shapes, **kw)(*args), []
    n_in, n_out, n_scr = len(in_specs), len(out_specs), len(scratch_shapes)
    n_ci, n_co = len(comm.inputs), len(comm.out_shapes)

    def hosting(*refs):
        ins, ci = refs[:n_in], refs[n_in:n_in + n_ci]
        k = n_in + n_ci
        outs, co = refs[k:k + n_out], refs[k + n_out:k + n_out + n_co]
        k += n_out + n_co
        scr, cs = refs[k:k + n_scr], refs[k + n_scr:]

        @pl.when(first())
        def _():
            comm.first(ci, co, cs)

        body(*ins, *outs, *scr)

        @pl.when(last())
        def _():
            comm.last(ci, co, cs)

    res = pl.pallas_call(
        hosting, in_specs=list(in_specs) + _hbm_specs(n_ci), out_specs=list(out_specs) + _hbm_specs(n_co),
        out_shape=list(out_shape) + list(comm.out_shapes), scratch_shapes=list(scratch_shapes) + list(comm.sems),
        **kw)(*args, *comm.inputs)
    return res[:n_out], res[n_out:]


def run_hosted(comm, *, name):
    n_i, n_o = len(comm.inputs), len(comm.out_shapes)

    def body(*refs):
        ins, outs, sems = refs[:n_i], refs[n_i:n_i + n_o], refs[n_i + n_o:]
        comm.first(ins, outs, sems)
        comm.last(ins, outs, sems)

    return pl.pallas_call(body, name=name, out_shape=list(comm.out_shapes), in_specs=_hbm_specs(n_i),
                          out_specs=_hbm_specs(n_o), scratch_shapes=list(comm.sems))(*comm.inputs)


def mm_nn(a, b, layer, *, pro=None, p1=None, p2=None, bias=None, out_dtype=F32, tm, tn, comm=None, name):
    M, K = a.shape
    N = b.shape[2]
    tm = min(tm, M)
    n_p = {"mod": 2, "scale": 1}.get(pro, 0)
    has_bias = bias is not None
    direct = pro is None and a.dtype == BF16

    def body(*refs):
        a_ref, b_ref = refs[0], refs[1]
        p_refs = refs[2:2 + n_p]
        bias_ref = refs[2 + n_p] if has_bias else None
        if direct:
            o_ref = refs[-1]
            lhs = a_ref[...]
        else:
            o_ref, abf = refs[-2], refs[-1]

            @pl.when(pl.program_id(1) == 0)
            def _():
                abf[...] = _bf(_prologue(pro, a_ref[...].astype(F32), *[r[...] for r in p_refs]))

            lhs = abf[...]
        acc = _dot(lhs, b_ref[...])
        if has_bias:
            acc = acc + bias_ref[...]
        o_ref[...] = acc.astype(out_dtype)

    in_specs = [pl.BlockSpec((tm, K), lambda i, j: (i, 0)), pl.BlockSpec((None, K, tn), lambda i, j: (layer, 0, j))]
    args = [a, b]
    for p in (p1, p2)[:n_p]:
        in_specs.append(pl.BlockSpec((1, K), lambda i, j: (0, 0)))
        args.append(p)
    if has_bias:
        in_specs.append(pl.BlockSpec((1, tn), lambda i, j: (0, j)))
        args.append(bias)
    n_i, n_j = M // tm, N // tn
    (out,), got = _call_hosting(
        body, comm, first=lambda: (pl.program_id(0) == 0) & (pl.program_id(1) == 0),
        last=lambda: (pl.program_id(0) == n_i - 1) & (pl.program_id(1) == n_j - 1),
        name=name, grid=(n_i, n_j), in_specs=in_specs,
        out_specs=[pl.BlockSpec((tm, tn), lambda i, j: (i, j))],
        out_shape=[jax.ShapeDtypeStruct((M, N), out_dtype)],
        scratch_shapes=[] if direct else [pltpu.VMEM((tm, K), BF16)], compiler_params=_params(2), args=args)
    return out if comm is None else (out, got)


def mm_nn_ksplit(a, b, layer, *, pro=None, tm, tk, name):
    M, K = a.shape
    N = b.shape[2]
    tm = min(tm, M)

    def body(a_ref, b_ref, o_ref):
        @pl.when(pl.program_id(1) == 0)
        def _():
            o_ref[...] = jnp.zeros_like(o_ref)

        o_ref[...] += _dot(_bf(_prologue(pro, a_ref[...])), b_ref[...])

    return pl.pallas_call(
        body, name=name, grid=(M // tm, K // tk),
        in_specs=[pl.BlockSpec((tm, tk), lambda i, k: (i, k)), pl.BlockSpec((None, tk, N), lambda i, k: (layer, k, 0))],
        out_specs=pl.BlockSpec((tm, N), lambda i, k: (i, 0)),
        out_shape=jax.ShapeDtypeStruct((M, N), F32), compiler_params=_params(2),
    )(a, b)


def mm_nt(a_parts, w, layer, *, pro=None, p1=None, epi_h=None, out_dtype=F32, tm, tn, comm=None, name):
    M = a_parts[0].shape[0]
    tm = min(tm, M)
    widths = [p.shape[1] for p in a_parts]
    Nw = sum(widths)
    Kw = w.shape[1]
    n_a = len(a_parts)
    has_p = pro == "scale"
    has_h = epi_h is not None
    direct = n_a == 1 and not has_p and a_parts[0].dtype == BF16

    def body(*refs):
        a_refs = refs[:n_a]
        w_ref = refs[n_a]
        k = n_a + 1
        p_ref = refs[k] if has_p else None
        k += int(has_p)
        h_ref = refs[k] if has_h else None
        if direct:
            o_ref = refs[-1]
            lhs = a_refs[0][...]
        else:
            o_ref, abf = refs[-2], refs[-1]

            @pl.when(pl.program_id(1) == 0)
            def _():
                off = 0
                for r, wd in zip(a_refs, widths):
                    av = r[...]
                    if has_p:
                        av = av.astype(F32) * (1.0 + p_ref[...])
                    abf[:, off:off + wd] = _bf(av)
                    off += wd

            lhs = abf[...]
        acc = _dot_nt(lhs, w_ref[...])
        if has_h:
            acc = acc * (2.0 * jnp.maximum(h_ref[...], 0.0))
        o_ref[...] = acc.astype(out_dtype)

    in_specs = [pl.BlockSpec((tm, wd), lambda i, j: (i, 0)) for wd in widths]
    in_specs.append(pl.BlockSpec((None, tn, Nw), lambda i, j: (layer, j, 0)))
    args = list(a_parts) + [w]
    if has_p:
        in_specs.append(pl.BlockSpec((1, Nw), lambda i, j: (0, 0)))
        args.append(p1)
    if has_h:
        in_specs.append(pl.BlockSpec((tm, tn), lambda i, j: (i, j)))
        args.append(epi_h)
    n_i, n_j = M // tm, Kw // tn
    (out,), got = _call_hosting(
        body, comm, first=lambda: (pl.program_id(0) == 0) & (pl.program_id(1) == 0),
        last=lambda: (pl.program_id(0) == n_i - 1) & (pl.program_id(1) == n_j - 1),
        name=name, grid=(n_i, n_j), in_specs=in_specs,
        out_specs=[pl.BlockSpec((tm, tn), lambda i, j: (i, j))],
        out_shape=[jax.ShapeDtypeStruct((M, Kw), out_dtype)],
        scratch_shapes=[] if direct else [pltpu.VMEM((tm, Nw), BF16)], compiler_params=_params(2), args=args)
    return out if comm is None else (out, got)


def mm_tn(a, d, *, pro=None, p1=None, p2=None, dscale=None, tk, tn, tm, out_buf, out_shape, col_block0=0, name):
    M, Kf = a.shape
    N = d.shape[1]
    n_p = {"mod": 2}.get(pro, 0)
    has_ds = dscale is not None
    has_buf = out_buf is not None
    n_m = M // tm

    def body(*refs):
        a_ref, d_ref = refs[0], refs[1]
        p_refs = refs[2:2 + n_p]
        ds_ref = refs[2 + n_p] if has_ds else None
        o_ref, acc = refs[-2], refs[-1]
        m = pl.program_id(2)

        @pl.when(m == 0)
        def _():
            acc[...] = jnp.zeros_like(acc)

        av = _prologue(pro, a_ref[...], *[r[...] for r in p_refs])
        dv = d_ref[...]
        if has_ds:
            dv = dv.astype(F32) * (1.0 + ds_ref[...])
        acc[...] += _dot_tn(_bf(av), _bf(dv))

        @pl.when(m == n_m - 1)
        def _():
            o_ref[...] = _bf(acc[...])

    in_specs = [pl.BlockSpec((tm, tk), lambda i, j, m: (m, i)), pl.BlockSpec((tm, tn), lambda i, j, m: (m, j))]
    args = [a, d]
    for p in (p1, p2)[:n_p]:
        in_specs.append(pl.BlockSpec((1, tk), lambda i, j, m: (0, i)))
        args.append(p)
    if has_ds:
        in_specs.append(pl.BlockSpec((1, tn), lambda i, j, m: (0, j)))
        args.append(dscale)
    aliases = {}
    if has_buf:
        in_specs.append(pl.BlockSpec(memory_space=pl.ANY))
        args.append(out_buf)
        aliases = {len(args) - 1: 0}
    return pl.pallas_call(
        body, name=name, grid=(Kf // tk, N // tn, n_m), in_specs=in_specs,
        out_specs=pl.BlockSpec((None, tk, tn), lambda i, j, m: (0, i, col_block0 + j)),
        out_shape=jax.ShapeDtypeStruct(out_shape, BF16), input_output_aliases=aliases,
        scratch_shapes=[pltpu.VMEM((tk, tn), F32)], compiler_params=_params(3),
    )(*args)


ROW_BLOCK = 512
ACC_ROWS = 8


def _ln_stats(z):
    mu = jnp.mean(z, axis=-1, keepdims=True)
    zc = z - mu
    var = jnp.mean(zc * zc, axis=-1, keepdims=True)
    return zc, lax.rsqrt(var + LN_EPS)


def ln_fwd(x, y, gate, lng, lnb, *, name):
    S = x.shape[0]

    def body(x_ref, y_ref, gt_ref, g_ref, b_ref, o_ref):
        z = ALPHA * x_ref[...] + (1.0 + gt_ref[...]) * y_ref[...]
        zc, rstd = _ln_stats(z)
        o_ref[...] = (zc * rstd) * g_ref[...] + b_ref[...]

    row = pl.BlockSpec((ROW_BLOCK, D), lambda i: (i, 0))
    vec = pl.BlockSpec((1, D), lambda i: (0, 0))
    return pl.pallas_call(
        body, name=name, grid=(S // ROW_BLOCK,), in_specs=[row, row, vec, vec, vec], out_specs=row,
        out_shape=jax.ShapeDtypeStruct((S, D), F32), compiler_params=_params(1),
    )(x, y, gate, lng, lnb)


def _add_colsum(acc_ref, r, val):
    acc_ref[r:r + 1, :] += jnp.sum(val, axis=0, keepdims=True)


def ln_bwd(x_in, y, gate, lng, *, dout=None, nxt=None, name):
    S = x_in.shape[0]
    has_next = nxt is not None

    def body(*refs):
        if has_next:
            dzn_ref, dun_ref, scn_ref, xo_ref = refs[:4]
            k = 4
        else:
            do_ref = refs[0]
            k = 1
        x_ref, y_ref, gt_ref, g_ref = refs[k:k + 4]
        dz_ref, acc_ref = refs[k + 4:]

        @pl.when(pl.program_id(0) == 0)
        def _():
            acc_ref[...] = jnp.zeros_like(acc_ref)

        if has_next:
            du = dun_ref[...]
            dout_v = ALPHA * dzn_ref[...] + du * (1.0 + scn_ref[...])
            _add_colsum(acc_ref, 3, du * xo_ref[...])
            _add_colsum(acc_ref, 4, du)
        else:
            dout_v = do_ref[...]
        yv = y_ref[...]
        z = ALPHA * x_ref[...] + (1.0 + gt_ref[...]) * yv
        zc, rstd = _ln_stats(z)
        xhat = zc * rstd
        _add_colsum(acc_ref, 0, dout_v * xhat)
        _add_colsum(acc_ref, 1, dout_v)
        dxh = dout_v * g_ref[...]
        m1 = jnp.mean(dxh, axis=-1, keepdims=True)
        m2 = jnp.mean(dxh * xhat, axis=-1, keepdims=True)
        dz = rstd * (dxh - m1 - xhat * m2)
        _add_colsum(acc_ref, 2, dz * yv)
        dz_ref[...] = dz

    row = pl.BlockSpec((ROW_BLOCK, D), lambda i: (i, 0))
    vec = pl.BlockSpec((1, D), lambda i: (0, 0))
    if has_next:
        in_specs = [row, row, vec, row]
        args = list(nxt)
    else:
        in_specs = [row]
        args = [dout]
    in_specs += [row, row, vec, vec]
    args += [x_in, y, gate, lng]
    return pl.pallas_call(
        body, name=name, grid=(S // ROW_BLOCK,), in_specs=in_specs,
        out_specs=[row, pl.BlockSpec((ACC_ROWS, D), lambda i: (0, 0))],
        out_shape=[jax.ShapeDtypeStruct((S, D), F32), jax.ShapeDtypeStruct((ACC_ROWS, D), F32)],
        compiler_params=_params(1),
    )(*args)


def combine_final(dz, du, sc, x_in, *, name):
    S = dz.shape[0]

    def body(dz_ref, du_ref, sc_ref, x_ref, dx_ref, acc_ref):
        @pl.when(pl.program_id(0) == 0)
        def _():
            acc_ref[...] = jnp.zeros_like(acc_ref)

        du_v = du_ref[...]
        dx_ref[...] = ALPHA * dz_ref[...] + du_v * (1.0 + sc_ref[...])
        _add_colsum(acc_ref, 3, du_v * x_ref[...])
        _add_colsum(acc_ref, 4, du_v)

    row = pl.BlockSpec((ROW_BLOCK, D), lambda i: (i, 0))
    vec = pl.BlockSpec((1, D), lambda i: (0, 0))
    return pl.pallas_call(
        body, name=name, grid=(S // ROW_BLOCK,), in_specs=[row, row, vec, row],
        out_specs=[row, pl.BlockSpec((ACC_ROWS, D), lambda i: (0, 0))],
        out_shape=[jax.ShapeDtypeStruct((S, D), F32), jax.ShapeDtypeStruct((ACC_ROWS, D), F32)],
        compiler_params=_params(1),
    )(dz, du, sc, x_in)


def colsum(a, *, name):
    S, N = a.shape

    def body(a_ref, o_ref):
        @pl.when(pl.program_id(0) == 0)
        def _():
            o_ref[...] = jnp.zeros_like(o_ref)

        o_ref[...] += jnp.sum(a_ref[...].astype(F32), axis=0, keepdims=True)

    return pl.pallas_call(
        body, name=name, grid=(S // ROW_BLOCK,), in_specs=[pl.BlockSpec((ROW_BLOCK, N), lambda i: (i, 0))],
        out_specs=pl.BlockSpec((1, N), lambda i: (0, 0)), out_shape=jax.ShapeDtypeStruct((1, N), F32),
        compiler_params=_params(1),
    )(a)


def loss_head(y, t, *, name):
    S = y.shape[0]

    def body(y_ref, t_ref, dy_ref, l_ref):
        @pl.when(pl.program_id(0) == 0)
        def _():
            l_ref[...] = jnp.zeros_like(l_ref)

        e = y_ref[...] - t_ref[...]
        dy_ref[...] = e * (1.0 / D)
        per_tok = jnp.sum(e * e, axis=1, keepdims=True) * (1.0 / D)
        l_ref[...] += 0.5 * jnp.sum(per_tok, axis=0, keepdims=True)

    row = pl.BlockSpec((ROW_BLOCK, D), lambda i: (i, 0))
    return pl.pallas_call(
        body, name=name, grid=(S // ROW_BLOCK,), in_specs=[row, row],
        out_specs=[row, pl.BlockSpec((8, 128), lambda i: (0, 0))],
        out_shape=[jax.ShapeDtypeStruct((S, D), F32), jax.ShapeDtypeStruct((8, 128), F32)],
        compiler_params=_params(1),
    )(y, t)


def _log_sigmoid(x):
    return jnp.minimum(x, 0.0) - jnp.log(1.0 + jnp.exp(-jnp.abs(x)))


def _silu(x):
    return x * (1.0 / (1.0 + jnp.exp(-x)))


def _cumsum_steps(x):
    row = lax.broadcasted_iota(jnp.int32, x.shape, 0)
    step = 1
    while step < x.shape[0]:
        x = x + jnp.where(row >= step, pltpu.roll(x, step, 0), 0.0)
        step *= 2
    return x


@jax.custom_vjp
def _cumsum_rows(x):
    return _cumsum_steps(x)


def _cumsum_rows_fwd(x):
    return _cumsum_steps(x), None


def _cumsum_rows_bwd(_, g):
    return (jnp.sum(g, axis=0, keepdims=True) - _cumsum_steps(g) + g,)


_cumsum_rows.defvjp(_cumsum_rows_fwd, _cumsum_rows_bwd)


def _gla_chunk(q, k, v, g, gk, s0t, w2p, bgk, gn):
    C = q.shape[0]
    row = lax.broadcasted_iota(jnp.int32, (C, C), 0)
    col = lax.broadcasted_iota(jnp.int32, (C, C), 1)
    lower = row >= col
    la = _log_sigmoid(_dot(_bf(gk), _bf(w2p)) + bgk) * GLA_TAU_INV
    outs, states = [], []
    for h in range(GLA_H):
        ks = slice(h * GLA_DKH, (h + 1) * GLA_DKH)
        vs = slice(h * GLA_DVH, (h + 1) * GLA_DVH)
        qh = q[:, ks] * (GLA_DKH ** -0.5)
        kh, vh, gh, lah, s0 = k[:, ks], v[:, vs], g[:, vs], la[:, ks], s0t[h]
        cum = _cumsum_rows(lah)
        e_pos = jnp.exp(cum)
        e_neg = jnp.exp(-cum)
        q_f = qh * e_pos
        a_f = _dot_nt(_bf(q_f), _bf(kh * e_neg))
        a_b = _dot_nt(_bf(qh * e_neg), _bf(kh * e_pos))
        att = jnp.where(lower, a_f, a_b)
        o = _dot(_bf(att), _bf(vh)) + _dot_nt(_bf(q_f), _bf(s0))
        tot = jnp.sum(lah, axis=0, keepdims=True)
        k_end = kh * jnp.exp(tot - cum)
        states.append(s0 * jnp.exp(tot) + _dot_tn(_bf(vh), _bf(k_end)))
        on = o * lax.rsqrt(jnp.mean(o * o, axis=-1, keepdims=True) + RMS_EPS) * gn[:, vs]
        outs.append(on * _silu(gh))
    return jnp.concatenate(outs, axis=1), tuple(states)


def _gla_split(p):
    return (p[:, 0:GLA_DK], p[:, GLA_DK:2 * GLA_DK], p[:, 2 * GLA_DK:2 * GLA_DK + GLA_DV],
            p[:, 2 * GLA_DK + GLA_DV:2 * GLA_DK + 2 * GLA_DV], p[:, 2 * GLA_DK + 2 * GLA_DV:GLA_INP])


def gla_fwd(proj, w2p, bgk, gn, *, comm=None, name):
    S = proj.shape[0]
    n_c = S // CHUNK
    n_s = n_c // GLA_SUB
    rows = GLA_SUB * CHUNK

    def body(p_ref, w_ref, b_ref, gn_ref, o_ref, st_ref, st):
        @pl.when(pl.program_id(0) == 0)
        def _():
            st[...] = jnp.zeros_like(st)

        s = tuple(st[h] for h in range(GLA_H))
        for u in range(GLA_SUB):
            sub = slice(u * CHUNK, (u + 1) * CHUNK)
            for h in range(GLA_H):
                st_ref[u, h] = s[h]
            og, s = _gla_chunk(*_gla_split(p_ref[sub, :]), s, w_ref[...], b_ref[...], gn_ref[...])
            o_ref[sub, :] = _bf(og)
        for h in range(GLA_H):
            st[h] = s[h]

    full = lambda shape: pl.BlockSpec(shape, lambda i: (0,) * len(shape))
    return _call_hosting(
        body, comm, first=lambda: pl.program_id(0) == 0, last=lambda: pl.program_id(0) == n_s - 1,
        name=name, grid=(n_s,),
        in_specs=[pl.BlockSpec((rows, GLA_INP), lambda i: (i, 0)), full((128, GLA_DK)), full((1, GLA_DK)),
                  full((1, GLA_DV))],
        out_specs=[pl.BlockSpec((rows, GLA_DV), lambda i: (i, 0)),
                   pl.BlockSpec((GLA_SUB, GLA_H, GLA_DVH, GLA_DKH), lambda i: (i, 0, 0, 0))],
        out_shape=[jax.ShapeDtypeStruct((S, GLA_DV), BF16),
                   jax.ShapeDtypeStruct((n_c, GLA_H, GLA_DVH, GLA_DKH), F32)],
        scratch_shapes=[pltpu.VMEM((GLA_H, GLA_DVH, GLA_DKH), F32)], compiler_params=_params(1),
        args=(proj, w2p, bgk, gn))


def gla_bwd(proj, dog, states, w2p, bgk, gn, *, comm=None, name):
    S = proj.shape[0]
    n_c = S // CHUNK
    n_s = n_c // GLA_SUB
    rows = GLA_SUB * CHUNK

    def body(p_ref, dog_ref, st_ref, w_ref, b_ref, gn_ref, dp_ref, dw_ref, db_ref, dgn_ref, ds_ref):
        @pl.when(pl.program_id(0) == 0)
        def _():
            ds_ref[...] = jnp.zeros_like(ds_ref)
            dw_ref[...] = jnp.zeros_like(dw_ref)
            db_ref[...] = jnp.zeros_like(db_ref)
            dgn_ref[...] = jnp.zeros_like(dgn_ref)

        ds = tuple(ds_ref[h] for h in range(GLA_H))
        for u in reversed(range(GLA_SUB)):
            sub = slice(u * CHUNK, (u + 1) * CHUNK)
            q, k, v, g, gk = _gla_split(p_ref[sub, :])
            s0 = tuple(st_ref[u, h] for h in range(GLA_H))
            _, vjp = jax.vjp(_gla_chunk, q, k, v, g, gk, s0, w_ref[...], b_ref[...], gn_ref[...])
            dq, dk, dv, dg, dgk, ds, dw, db, dgn = vjp((dog_ref[sub, :], ds))
            dp_ref[sub, 0:GLA_DK] = _bf(dq)
            dp_ref[sub, GLA_DK:2 * GLA_DK] = _bf(dk)
            dp_ref[sub, 2 * GLA_DK:2 * GLA_DK + GLA_DV] = _bf(dv)
            dp_ref[sub, 2 * GLA_DK + GLA_DV:2 * GLA_DK + 2 * GLA_DV] = _bf(dg)
            dp_ref[sub, 2 * GLA_DK + 2 * GLA_DV:GLA_INP] = _bf(dgk)
            dw_ref[...] += dw
            db_ref[...] += db
            dgn_ref[...] += dgn
        for h in range(GLA_H):
            ds_ref[h] = ds[h]

    full = lambda shape: pl.BlockSpec(shape, lambda i: (0,) * len(shape))
    rev = lambda i: (n_s - 1 - i, 0)
    return _call_hosting(
        body, comm, first=lambda: pl.program_id(0) == 0, last=lambda: pl.program_id(0) == n_s - 1,
        name=name, grid=(n_s,),
        in_specs=[pl.BlockSpec((rows, GLA_INP), rev), pl.BlockSpec((rows, GLA_DV), rev),
                  pl.BlockSpec((GLA_SUB, GLA_H, GLA_DVH, GLA_DKH), lambda i: (n_s - 1 - i, 0, 0, 0)),
                  full((128, GLA_DK)), full((1, GLA_DK)), full((1, GLA_DV))],
        out_specs=[pl.BlockSpec((rows, GLA_INP), rev), full((128, GLA_DK)), full((1, GLA_DK)), full((1, GLA_DV))],
        out_shape=[jax.ShapeDtypeStruct((S, GLA_INP), BF16), jax.ShapeDtypeStruct((128, GLA_DK), F32),
                   jax.ShapeDtypeStruct((1, GLA_DK), F32), jax.ShapeDtypeStruct((1, GLA_DV), F32)],
        scratch_shapes=[pltpu.VMEM((GLA_H, GLA_DVH, GLA_DKH), F32)], compiler_params=_params(1),
        args=(proj, dog, states, w2p, bgk, gn))


def _rel_index():
    t = np.arange(REL_TILE)[:, None]
    j = np.arange(REL_TILE)[None, :]
    tiles = []
    for m in REL_TILES:
        chunks = (REL_TILE // CHUNK) * m + j // CHUNK - t // CHUNK
        band = (chunks >= 0) & (chunks <= LEFT // CHUNK)
        dist = LEFT - REL_TILE * m + t - j
        tiles.append(np.where(band, np.minimum(dist, MAX_REL) + MAX_REL, N_REL))
    return jnp.asarray(np.stack(tiles).reshape(1, -1).astype(np.int32))


REL_BLOCK = 2048


def _one_hot(idx_row):
    return (lax.broadcasted_iota(jnp.int32, (REL_PAD, idx_row.shape[1]), 0) == idx_row).astype(F32)


def rel_bias_tiles(rel_pad, idx, *, name):
    E = idx.shape[1]

    def body(r_ref, i_ref, o_ref):
        o_ref[...] = jnp.dot(r_ref[...], _one_hot(i_ref[...]), precision=HIGHEST, preferred_element_type=F32)

    return pl.pallas_call(
        body, name=name, grid=(E // REL_BLOCK,),
        in_specs=[pl.BlockSpec((ATT_H, REL_PAD), lambda i: (0, 0)), pl.BlockSpec((1, REL_BLOCK), lambda i: (0, i))],
        out_specs=pl.BlockSpec((ATT_H, REL_BLOCK), lambda i: (0, i)),
        out_shape=jax.ShapeDtypeStruct((ATT_H, E), F32), compiler_params=_params(1),
    )(rel_pad, idx)


def rel_bias_grad(dtiles_flat, dclip, idx, *, name):
    E = idx.shape[1]
    n_steps = E // REL_BLOCK

    def body(d_ref, c_ref, i_ref, o_ref):
        @pl.when(pl.program_id(0) == 0)
        def _():
            o_ref[...] = jnp.zeros_like(o_ref)

        o_ref[...] += lax.dot_general(d_ref[...], _one_hot(i_ref[...]), (((1,), (1,)), ((), ())),
                                      precision=HIGHEST, preferred_element_type=F32)

        @pl.when(pl.program_id(0) == n_steps - 1)
        def _():
            at_clip = lax.broadcasted_iota(jnp.int32, (1, REL_PAD), 1) == 2 * MAX_REL
            o_ref[...] += jnp.where(at_clip, jnp.sum(c_ref[...], axis=1, keepdims=True), 0.0)

    return pl.pallas_call(
        body, name=name, grid=(n_steps,),
        in_specs=[pl.BlockSpec((ATT_H, REL_BLOCK), lambda i: (0, i)), pl.BlockSpec((ATT_H, 128), lambda i: (0, 0)),
                  pl.BlockSpec((1, REL_BLOCK), lambda i: (0, i))],
        out_specs=pl.BlockSpec((ATT_H, REL_PAD), lambda i: (0, 0)),
        out_shape=jax.ShapeDtypeStruct((ATT_H, REL_PAD), F32), compiler_params=_params(1),
    )(dtiles_flat, dclip, idx)


def _attn_bias(tiles, clip):
    const = jnp.broadcast_to(clip, (REL_TILE, REL_TILE))
    zero = jnp.zeros((REL_TILE, REL_TILE), F32)
    rows = []
    for qt in range(ATT_QB // REL_TILE):
        blocks = []
        for kt in range(ATT_KB // REL_TILE):
            m = kt - qt
            if m in REL_TILES:
                blocks.append(tiles[REL_TILES.index(m)])
            elif 0 <= m < REL_TILES[0]:
                blocks.append(const)
            else:
                blocks.append(zero)
        rows.append(jnp.concatenate(blocks, axis=1))
    return jnp.concatenate(rows, axis=0)


def _attn_bias_grad(ds, dt_ref, dc_ref, a):
    tile = lambda qt, kt: ds[qt * REL_TILE:(qt + 1) * REL_TILE, kt * REL_TILE:(kt + 1) * REL_TILE]
    const = None
    sums = [None] * len(REL_TILES)
    for qt in range(ATT_QB // REL_TILE):
        for kt in range(ATT_KB // REL_TILE):
            m = kt - qt
            if m in REL_TILES:
                n = REL_TILES.index(m)
                sums[n] = tile(qt, kt) if sums[n] is None else sums[n] + tile(qt, kt)
            elif 0 <= m < REL_TILES[0]:
                const = tile(qt, kt) if const is None else const + tile(qt, kt)
    for n, v in enumerate(sums):
        dt_ref[a, n] += v
    dc_ref[a] += jnp.sum(const, axis=0, keepdims=True)


def _attn_head_lanes():
    lane = lax.broadcasted_iota(jnp.int32, (1, 2 * ATT_HD), 1)
    return [(lane >= a * ATT_HD) & (lane < (a + 1) * ATT_HD) for a in range(2)]


def _attn_band_bias(tiles, clip):
    j = lax.broadcasted_iota(jnp.int32, (ATT_QB, ATT_KB), 1)
    t = lax.broadcasted_iota(jnp.int32, (ATT_QB, ATT_KB), 0)
    shift = CHUNK.bit_length() - 1
    chunks = lax.shift_right_logical(j, shift) - lax.shift_right_logical(t, shift)
    band = (chunks >= 0) & (chunks <= LEFT // CHUNK)
    return jnp.where(band, _attn_bias(tiles, clip), NEG_INF)


def _attn_exp(qa, kb, bias, key_bias):
    s = _dot_nt(qa, kb) + bias + key_bias
    e = jnp.exp(s - jnp.max(s, axis=-1, keepdims=True))
    return e, jnp.sum(e, axis=-1, keepdims=True)


def _attn_specs():
    n_hp = ATT_H // 2
    q_spec = pl.BlockSpec((ATT_QB, 128), lambda hp, g: (g, hp))

    def win(col0, back):
        return pl.BlockSpec((ATT_QB, 128), lambda hp, g: (jnp.maximum(g - back, 0), col0 + hp))

    kv_specs = [win(n_hp, 2), win(n_hp, 1), win(n_hp, 0), win(2 * n_hp, 2), win(2 * n_hp, 1), win(2 * n_hp, 0)]
    tiles_spec = pl.BlockSpec((2, len(REL_TILES), REL_TILE, REL_TILE), lambda hp, g: (hp, 0, 0, 0))
    clip_spec = pl.BlockSpec((2, 1, 128), lambda hp, g: (hp, 0, 0))
    return q_spec, kv_specs, tiles_spec, clip_spec


def _attn_window(refs, g):
    kb = jnp.concatenate([_bf(r[...]) for r in refs[0:3]], axis=0)
    vb = jnp.concatenate([_bf(r[...]) for r in refs[3:6]], axis=0)
    j = lax.broadcasted_iota(jnp.int32, (1, ATT_KB), 1)
    return kb, vb, jnp.where(j + (g - 2) * ATT_QB >= 0, 0.0, NEG_INF)


def attn_fwd(qkv, tiles, clip, *, comm=None, name):
    S = qkv.shape[0]
    q_spec, kv_specs, tiles_spec, clip_spec = _attn_specs()

    def body(q_ref, *rest):
        kv_refs, t_ref, c_ref, o_ref, bias = rest[:6], rest[6], rest[7], rest[8], rest[9]
        g = pl.program_id(1)

        @pl.when(g == 0)
        def _():
            for a in range(2):
                bias[a] = _attn_band_bias(t_ref[a], c_ref[a])

        kb, vb, key_bias = _attn_window(kv_refs, g)
        q = q_ref[...].astype(F32)
        out = jnp.zeros((ATT_QB, 2 * ATT_HD), F32)
        for a, lanes in enumerate(_attn_head_lanes()):
            mf = lanes.astype(F32)
            e, l = _attn_exp(_bf(q * (mf * ATT_HD ** -0.5)), kb, bias[a], key_bias)
            out = out + _dot(_bf(e), vb) * (mf * (1.0 / l))
        o_ref[...] = _bf(out)

    n_hp, n_g = ATT_H // 2, S // ATT_QB
    return _call_hosting(
        body, comm, first=lambda: (pl.program_id(0) == 0) & (pl.program_id(1) == 0),
        last=lambda: (pl.program_id(0) == n_hp - 1) & (pl.program_id(1) == n_g - 1),
        name=name, grid=(n_hp, n_g), in_specs=[q_spec] + kv_specs + [tiles_spec, clip_spec],
        out_specs=[q_spec], out_shape=[jax.ShapeDtypeStruct((S, D), BF16)],
        scratch_shapes=[pltpu.VMEM((2, ATT_QB, ATT_KB), F32)], compiler_params=_params(2),
        args=(*([qkv] * 7), tiles, clip))


def attn_bwd(qkv, do, tiles, clip, *, comm=None, name):
    S = qkv.shape[0]
    q_spec, kv_specs, tiles_spec, clip_spec = _attn_specs()
    col_spec = pl.BlockSpec((S, 128), lambda hp, g: (0, hp))
    sum_spec = pl.BlockSpec((1, 128), lambda hp, g: (0, hp))
    n_g = S // ATT_QB

    def body(q_ref, *rest):
        kv_refs, t_ref, c_ref, do_ref = rest[:6], rest[6], rest[7], rest[8]
        dq_ref, dk_ref, dv_ref, dt_ref, dc_ref, sq_ref, sk_ref, sv_ref, bias = rest[9:]
        g = pl.program_id(1)

        @pl.when(g == 0)
        def _():
            for a in range(2):
                bias[a] = _attn_band_bias(t_ref[a], c_ref[a])
            dk_ref[...] = jnp.zeros_like(dk_ref)
            dv_ref[...] = jnp.zeros_like(dv_ref)
            dt_ref[...] = jnp.zeros_like(dt_ref)
            dc_ref[...] = jnp.zeros_like(dc_ref)
            sq_ref[...] = jnp.zeros_like(sq_ref)

        kb, vb, key_bias = _attn_window(kv_refs, g)
        q = q_ref[...].astype(F32)
        do = do_ref[...]
        dq = jnp.zeros((ATT_QB, 2 * ATT_HD), F32)
        dkw = jnp.zeros((ATT_KB, 2 * ATT_HD), F32)
        dvw = jnp.zeros((ATT_KB, 2 * ATT_HD), F32)
        for a, lanes in enumerate(_attn_head_lanes()):
            mf = lanes.astype(F32) * ATT_HD ** -0.5
            qa = _bf(q * mf)
            e, l = _attn_exp(qa, kb, bias[a], key_bias)
            p = e * (1.0 / l)
            do_a = jnp.where(lanes, do, jnp.zeros_like(do))
            dp = _dot_nt(do_a, vb)
            ds = p * (dp - jnp.sum(p * dp, axis=-1, keepdims=True))
            ds_b = _bf(ds)
            dq = dq + _dot(ds_b, kb) * mf
            dkw = dkw + _dot_tn(ds_b, qa)
            dvw = dvw + _dot_tn(_bf(p), do_a)
            _attn_bias_grad(ds, dt_ref, dc_ref, a)
        dq_ref[...] = _bf(dq)
        sq_ref[...] += jnp.sum(dq, axis=0, keepdims=True)
        for blk in range(3):
            src = g - 2 + blk

            @pl.when(src >= 0)
            def _(blk=blk, src=src):
                rows = pl.ds(pl.multiple_of(src * ATT_QB, ATT_QB), ATT_QB)
                dk_ref[rows, :] += dkw[blk * ATT_QB:(blk + 1) * ATT_QB]
                dv_ref[rows, :] += dvw[blk * ATT_QB:(blk + 1) * ATT_QB]

        @pl.when(g == n_g - 1)
        def _():
            sk_ref[...] = jnp.sum(dk_ref[...], axis=0, keepdims=True)
            sv_ref[...] = jnp.sum(dv_ref[...], axis=0, keepdims=True)

    n_hp, n_g = ATT_H // 2, S // ATT_QB
    return _call_hosting(
        body, comm, first=lambda: (pl.program_id(0) == 0) & (pl.program_id(1) == 0),
        last=lambda: (pl.program_id(0) == n_hp - 1) & (pl.program_id(1) == n_g - 1),
        name=name, grid=(n_hp, n_g),
        in_specs=[q_spec] + kv_specs + [tiles_spec, clip_spec, q_spec],
        out_specs=[q_spec, col_spec, col_spec, tiles_spec, clip_spec] + [sum_spec] * 3,
        out_shape=[jax.ShapeDtypeStruct((S, D), BF16)] + [jax.ShapeDtypeStruct((S, D), F32)] * 2
        + [jax.ShapeDtypeStruct((ATT_H, len(REL_TILES), REL_TILE, REL_TILE), F32),
           jax.ShapeDtypeStruct((ATT_H, 1, 128), F32)] + [jax.ShapeDtypeStruct((1, D), F32)] * 3,
        scratch_shapes=[pltpu.VMEM((2, ATT_QB, ATT_KB), F32)], compiler_params=_params(2),
        args=(*([qkv] * 7), tiles, clip, do))


def mods_partial(c_all, w_ada, *, name):
    n_l, _, n_c = w_ada.shape

    def body(c_ref, w_ref, o_ref):
        o_ref[...] = _dot(_bf(_silu(c_ref[...])), _bf(w_ref[...]))

    return pl.pallas_call(
        body, name=name, grid=(n_l,),
        in_specs=[pl.BlockSpec((N_DEV, D), lambda l: (0, 0)), pl.BlockSpec((None, D, n_c), lambda l: (l, 0, 0))],
        out_specs=pl.BlockSpec((None, N_DEV, n_c), lambda l: (l, 0, 0)),
        out_shape=jax.ShapeDtypeStruct((n_l, N_DEV, n_c), F32), compiler_params=_params(1),
    )(c_all, w_ada)


def w_ada_grad(c_all, dm, *, name):
    n_l, _, n_c = dm.shape

    def body(c_ref, d_ref, o_ref):
        o_ref[...] = lax.dot_general(_silu(c_ref[...]), d_ref[...], (((0,), (0,)), ((), ())),
                                     precision=HIGHEST, preferred_element_type=F32)

    return pl.pallas_call(
        body, name=name, grid=(n_l,),
        in_specs=[pl.BlockSpec((N_DEV, D), lambda l: (0, 0)), pl.BlockSpec((None, N_DEV, n_c), lambda l: (l, 0, 0))],
        out_specs=pl.BlockSpec((None, D, n_c), lambda l: (l, 0, 0)),
        out_shape=jax.ShapeDtypeStruct((n_l, D, n_c), F32), compiler_params=_params(1),
    )(c_all, dm)


def adamw(w, m, v, gparts, *, block_rows, name):
    R, C = w.shape
    n = gparts.shape[0]

    def body(w_ref, m_ref, v_ref, g_ref, go_ref, d_ref, mo_ref, vo_ref):
        g = g_ref[0].astype(F32)
        for k in range(1, n):
            g = g + g_ref[k].astype(F32)
        m_new = ADAM_B1 * m_ref[...] + (1.0 - ADAM_B1) * g
        v_new = ADAM_B2 * v_ref[...] + (1.0 - ADAM_B2) * (g * g)
        m_hat = m_new / (1.0 - ADAM_B1 ** ADAM_STEP)
        v_hat = v_new / (1.0 - ADAM_B2 ** ADAM_STEP)
        go_ref[...] = g
        d_ref[...] = -ADAM_LR * (m_hat / (jnp.sqrt(v_hat) + ADAM_EPS) + ADAM_WD * w_ref[...])
        mo_ref[...] = m_new
        vo_ref[...] = v_new

    blk = pl.BlockSpec((block_rows, C), lambda i: (i, 0))
    return pl.pallas_call(
        body, name=name, grid=(R // block_rows,),
        in_specs=[blk, blk, blk, pl.BlockSpec((n, block_rows, C), lambda i: (0, i, 0))],
        out_specs=[blk] * 4, out_shape=[jax.ShapeDtypeStruct((R, C), F32)] * 4, compiler_params=_params(1),
    )(w, m, v, gparts)


def adamw_nd(w, m, v, gparts, *, name):
    shape = w.shape
    two = (int(np.prod(shape[:-1])), shape[-1])
    rows = two[0]
    block_rows = rows
    for cand in (512, 256):
        if rows > cand and rows % cand == 0:
            block_rows = cand
            break
    outs = adamw(w.reshape(two), m.reshape(two), v.reshape(two), gparts.reshape((gparts.shape[0],) + two),
                 block_rows=block_rows, name=name)
    return [o.reshape(shape) for o in outs]


def sum_parts(parts, *, name):
    n, R, C = parts.shape

    def body(p_ref, o_ref):
        acc = p_ref[0]
        for k in range(1, n):
            acc = acc + p_ref[k]
        o_ref[...] = acc

    return pl.pallas_call(
        body, name=name, in_specs=[pl.BlockSpec((n, R, C), lambda: (0, 0, 0))],
        out_specs=pl.BlockSpec((R, C), lambda: (0, 0)), out_shape=jax.ShapeDtypeStruct((R, C), F32),
        compiler_params=pltpu.CompilerParams(vmem_limit_bytes=VMEM_LIMIT),
    )(parts)


def _my_place():
    return lax.axis_index("x"), lax.axis_index("y"), lax.axis_index("c")


def _full_shape(kind, shard):
    n_l, rows, cols = shard
    return {"col": (n_l, rows, N_DEV * cols), "row": (n_l, N_DEV * rows, cols), "stk": (N_DEV, n_l, rows, cols)}[kind]


def _slab(ref, kind, dev, shard):
    _, rows, cols = shard
    if kind == "col":
        return ref.at[:, :, pl.ds(pl.multiple_of(dev * cols, 128), cols)]
    if kind == "row":
        return ref.at[:, pl.ds(pl.multiple_of(dev * rows, 8), rows), :]
    return ref.at[dev]


def _hbm_specs(n):
    return [pl.BlockSpec(memory_space=pltpu.HBM)] * n


def all_gather(x_shard, *, name):
    m_per, n = x_shard.shape

    def body(x_ref, out_ref, send_sems, recv_sems, local_sem):
        x, y, c = _my_place()
        me, sibling = (x, y, c), (x, y, 1 - c)
        chips = [(1 - x, y), (x, 1 - y), (1 - x, 1 - y)]

        def rows(px, py, pc):
            return out_ref.at[pl.ds((4 * px + 2 * py + pc) * m_per, m_per), :]

        def copy(k, block, to, src=None):
            return pltpu.make_async_remote_copy(
                src_ref=rows(*block) if src is None else src, dst_ref=rows(*block),
                send_sem=send_sems.at[k], recv_sem=recv_sems.at[k], device_id=to, device_id_type=MESH)

        mine = pltpu.make_async_copy(x_ref, rows(*me), local_sem)
        mine.start()
        first = [copy(0, me, sibling, src=x_ref)]
        first += [copy(1 + j, me, (*chip, c), src=x_ref) for j, chip in enumerate(chips)]
        for cp in first:
            cp.start()
        passed = [copy(4 + j, (*chip, c), sibling) for j, chip in enumerate(chips)]
        for j, chip in enumerate(chips):
            copy(1 + j, (*chip, c), me).wait_recv()
            passed[j].start()
        copy(0, sibling, me).wait_recv()
        for j, chip in enumerate(chips):
            copy(4 + j, (*chip, 1 - c), me).wait_recv()
        for cp in first + passed:
            cp.wait_send()
        mine.wait()

    return pl.pallas_call(
        body, name=name, out_shape=jax.ShapeDtypeStruct((N_DEV * m_per, n), x_shard.dtype),
        in_specs=[pl.BlockSpec(memory_space=pltpu.VMEM)], out_specs=pl.BlockSpec(memory_space=pltpu.VMEM),
        scratch_shapes=[pltpu.SemaphoreType.DMA((7,)), pltpu.SemaphoreType.DMA((7,)), pltpu.SemaphoreType.DMA],
        compiler_params=pltpu.CompilerParams(vmem_limit_bytes=VMEM_LIMIT),
    )(x_shard)


def gather_plan(shards, kinds, layers):
    n_t = len(shards)
    shapes = [(1,) + tuple(s.shape[1:]) for s in shards]

    def copies(x_refs, out_refs, sems):
        send_sems, recv_sems, local_sems = sems
        x, y, c = _my_place()
        me, sibling = (x, y, c), (x, y, 1 - c)
        chips = [(1 - x, y), (x, 1 - y), (1 - x, 1 - y)]
        own = [x_refs[t].at[pl.ds(layers[t], 1)] for t in range(n_t)]

        def slab(t, px, py, pc):
            return _slab(out_refs[t], kinds[t], 4 * px + 2 * py + pc, shapes[t])

        def copy(t, k, block, to, src=None):
            return pltpu.make_async_remote_copy(
                src_ref=slab(t, *block) if src is None else src, dst_ref=slab(t, *block),
                send_sem=send_sems.at[7 * t + k], recv_sem=recv_sems.at[7 * t + k], device_id=to,
                device_id_type=MESH)

        mine = [pltpu.make_async_copy(own[t], slab(t, *me), local_sems.at[t]) for t in range(n_t)]
        sends = []
        for t in range(n_t):
            sends.append(copy(t, 0, me, sibling, src=own[t]))
            sends += [copy(t, 1 + j, me, (*chip, c), src=own[t]) for j, chip in enumerate(chips)]
        return mine, sends, copy, me, sibling, chips, c

    def first(x_refs, out_refs, sems):
        mine, sends = copies(x_refs, out_refs, sems)[:2]
        for cp in mine + sends:
            cp.start()

    def last(x_refs, out_refs, sems):
        mine, sends, copy, me, sibling, chips, c = copies(x_refs, out_refs, sems)
        passed = []
        for j, chip in enumerate(chips):
            for t in range(n_t):
                copy(t, 1 + j, (*chip, c), me).wait_recv()
                passed.append(copy(t, 4 + j, (*chip, c), sibling))
                passed[-1].start()
        for t in range(n_t):
            copy(t, 0, sibling, me).wait_recv()
        for j, chip in enumerate(chips):
            for t in range(n_t):
                copy(t, 4 + j, (*chip, 1 - c), me).wait_recv()
        for cp in sends + passed:
            cp.wait_send()
        for cp in mine:
            cp.wait()

    return Hosted(
        list(shards), [jax.ShapeDtypeStruct(_full_shape(k, shp), s.dtype) for k, shp, s in zip(kinds, shapes, shards)],
        [pltpu.SemaphoreType.DMA((7 * n_t,)), pltpu.SemaphoreType.DMA((7 * n_t,)), pltpu.SemaphoreType.DMA((n_t,))],
        first, last)


def scatter_plan(grads, kinds, shapes):
    n_t = len(grads)

    def copies(g_refs, out_refs, sems):
        send_sems, recv_sems, local_sems = sems
        x, y, c = _my_place()
        me = 4 * x + 2 * y + c
        local = [pltpu.make_async_copy(_slab(g_refs[t], kinds[t], me, shapes[t]), out_refs[t].at[me],
                                       local_sems.at[t]) for t in range(n_t)]
        remote = []
        for t in range(n_t):
            for r in range(1, N_DEV):
                px = 1 - x if r & 4 else x
                py = 1 - y if r & 2 else y
                pc = 1 - c if r & 1 else c
                remote.append(pltpu.make_async_remote_copy(
                    src_ref=_slab(g_refs[t], kinds[t], 4 * px + 2 * py + pc, shapes[t]), dst_ref=out_refs[t].at[me],
                    send_sem=send_sems.at[7 * t + r - 1], recv_sem=recv_sems.at[7 * t + r - 1],
                    device_id=(px, py, pc), device_id_type=MESH))
        return local, remote

    def first(g_refs, out_refs, sems):
        local, remote = copies(g_refs, out_refs, sems)
        for cp in local + remote:
            cp.start()

    def last(g_refs, out_refs, sems):
        local, remote = copies(g_refs, out_refs, sems)
        for cp in remote + local:
            cp.wait()

    return Hosted(
        list(grads), [jax.ShapeDtypeStruct((N_DEV,) + tuple(s), BF16) for s in shapes],
        [pltpu.SemaphoreType.DMA((7 * n_t,)), pltpu.SemaphoreType.DMA((7 * n_t,)), pltpu.SemaphoreType.DMA((n_t,))],
        first, last)


def adamw_layer(w, m, v, parts, layer, bufs, *, name):
    n_l, rows, cols = w.shape
    n = parts.shape[0]
    tr = min(rows, 256)

    def body(w_ref, m_ref, v_ref, g_ref, *rest):
        go_ref, d_ref, mo_ref, vo_ref = rest[-4:]
        g = g_ref[0].astype(F32)
        for k in range(1, n):
            g = g + g_ref[k].astype(F32)
        m_new = ADAM_B1 * m_ref[...] + (1.0 - ADAM_B1) * g
        v_new = ADAM_B2 * v_ref[...] + (1.0 - ADAM_B2) * (g * g)
        m_hat = m_new / (1.0 - ADAM_B1 ** ADAM_STEP)
        v_hat = v_new / (1.0 - ADAM_B2 ** ADAM_STEP)
        go_ref[...] = g
        d_ref[...] = -ADAM_LR * (m_hat / (jnp.sqrt(v_hat) + ADAM_EPS) + ADAM_WD * w_ref[...])
        mo_ref[...] = m_new
        vo_ref[...] = v_new

    blk = pl.BlockSpec((None, tr, cols), lambda i: (layer, i, 0))
    in_specs = [blk, blk, blk, pl.BlockSpec((n, None, tr, cols), lambda i: (0, 0, i, 0))]
    args = [w, m, v, parts]
    aliases = {}
    if bufs is not None:
        in_specs += [pl.BlockSpec(memory_space=pl.ANY)] * 4
        args += list(bufs)
        aliases = {4 + k: k for k in range(4)}
    return pl.pallas_call(
        body, name=name, grid=(rows // tr,), in_specs=in_specs, out_specs=[blk] * 4,
        out_shape=[jax.ShapeDtypeStruct((n_l, rows, cols), F32)] * 4, input_output_aliases=aliases,
        compiler_params=_params(1),
    )(*args)


BIG =("gla_w_in", "gla_w_out", "att_w_in", "att_w_out", "ff_w1", "ff_w2")
KIND = {"gla_w_in": "stk", "gla_w_out": "row", "att_w_in": "col", "att_w_out": "row", "ff_w1": "col", "ff_w2": "row"}


def _pack_small(arrs):
    parts = []
    for a in arrs:
        f = a.reshape(-1)
        parts.append(jnp.pad(f, (0, -f.shape[0] % 128)))
    flat = jnp.concatenate(parts)
    flat = jnp.pad(flat, (0, -flat.shape[0] % 1024))
    return flat.reshape(-1, 128)


def _unpack_small(packed, shapes):
    flat = packed.reshape(packed.shape[:-2] + (-1,))
    out, off = [], 0
    for shp in shapes:
        n = int(np.prod(shp))
        out.append(flat[..., off:off + n].reshape(packed.shape[:-2] + tuple(shp)))
        off += n + (-n % 128)
    return out


def _vec(a):
    return a.reshape(1, -1)


def _layer_weights(i):
    mixer = "gla" if i % 2 == 0 else "att"
    return [(f"{mixer}_w_in", i // 2), (f"{mixer}_w_out", i // 2), ("ff_w1", i), ("ff_w2", i)]


def _trunk(x, target, mods, sm, w, m, v):
    shard_bf = {n: w[n].astype(BF16) for n in BIG}

    def gather_of(names):
        return gather_plan([shard_bf[n] for n, _ in names], [KIND[n] for n, _ in names], [l for _, l in names])

    def gather_layer(i):
        return _layer_weights(i), gather_of(_layer_weights(i))

    wts = {}

    def keep_gathered(names, arrays):
        for (n, l), a in zip(names, arrays):
            if KIND[n] == "stk":
                a = a.transpose(1, 2, 0, 3).reshape(1, D, GLA_IN)
                a = jnp.pad(a, ((0, 0), (0, 0), (0, GLA_INP - GLA_IN)))
            wts[n, l] = a

    first_names, rest_names = _layer_weights(0)[:1], _layer_weights(0)[1:]
    keep_gathered(first_names, run_hosted(gather_of(first_names), name="gather_first"))

    rel_idx = _rel_index()
    saved = []
    for i in range(DEPTH):
        sh1, sc1, g1, sh2, sc2, g2 = [mods[i, k:k + 1] for k in range(6)]
        rec = {"x0": x}
        j = i // 2
        nxt_names, nxt_plan = gather_layer(i + 1) if i + 1 < DEPTH else (None, None)
        if i % 2 == 0:
            w2p = jnp.pad(sm["gla_w_gk2"][j], ((0, 128 - GLA_RANK), (0, 0)))
            bgk, gn = _vec(sm["gla_b_gk"][j]), _vec(sm["gla_g_norm"][j])
            if i == 0:
                proj, got0 = mm_nn(x, wts["gla_w_in", j], 0, pro="mod", p1=sc1, p2=sh1, tm=1024, tn=640,
                                   comm=gather_of(rest_names), name=f"gla_proj_{i}")
                keep_gathered(rest_names, got0)
            else:
                proj = mm_nn(x, wts["gla_w_in", j], 0, pro="mod", p1=sc1, p2=sh1, tm=1024, tn=640,
                             name=f"gla_proj_{i}")
            (og, states), got = gla_fwd(proj, w2p, bgk, gn, comm=nxt_plan, name=f"gla_core_{i}")
            y = mm_nn(og, wts["gla_w_out", j], 0, tm=1024, tn=1024, name=f"gla_out_{i}")
            rec.update(kind="gla", j=j, w2p=w2p, bgk=bgk, gn=gn, proj=proj, og=og, states=states)
        else:
            rel = sm["att_rel_bias"][j]
            rel_pad = jnp.pad(rel, ((0, 0), (0, REL_PAD - N_REL)), constant_values=NEG_INF)
            tiles = rel_bias_tiles(rel_pad, rel_idx, name=f"att_bias_{i}")
            tiles = tiles.reshape(ATT_H, len(REL_TILES), REL_TILE, REL_TILE)
            clip = jnp.broadcast_to(rel[:, 2 * MAX_REL][:, None, None], (ATT_H, 1, 128))
            qkv = mm_nn(x, wts["att_w_in", j], 0, pro="mod", p1=sc1, p2=sh1, bias=_vec(sm["att_b_in"][j]),
                        out_dtype=BF16, tm=1024, tn=1024, name=f"att_proj_{i}")
            (o,), got = attn_fwd(qkv, tiles, clip, comm=nxt_plan, name=f"att_core_{i}")
            y = mm_nn(o, wts["att_w_out", j], 0, tm=1024, tn=1024, name=f"att_out_{i}")
            rec.update(kind="att", j=j, tiles=tiles, clip=clip, qkv=qkv, o=o)
        if nxt_plan is not None:
            keep_gathered(nxt_names, got)
        x1 = ln_fwd(x, y, g1, _vec(sm["ln_g"][i, 0]), _vec(sm["ln_b"][i, 0]), name=f"ln_mix_{i}")
        h = mm_nn(x1, wts["ff_w1", i], 0, pro="mod", p1=sc2, p2=sh2, out_dtype=BF16, tm=2048, tn=1024,
                  name=f"ff_up_{i}")
        y2 = mm_nn_ksplit(h, wts["ff_w2", i], 0, pro="relu2", tm=1024, tk=2048, name=f"ff_down_{i}")
        x2 = ln_fwd(x1, y2, g2, _vec(sm["ln_g"][i, 1]), _vec(sm["ln_b"][i, 1]), name=f"ln_ff_{i}")
        rec.update(y=y, x1=x1, h=h, y2=y2)
        saved.append(rec)
        x = x2

    dy, loss = loss_head(x, target, name="loss_head")

    gw = {}

    def wgrad(weight, layer, a, d, *, tn, tk=1024, tm=512, col_block0=0, **kw):
        gw[weight, layer] = mm_tn(a, d, tk=tk, tn=tn, tm=tm, out_buf=gw.get((weight, layer)),
                                  out_shape=wts[weight, layer].shape, col_block0=col_block0, **kw)

    def scatter_layer(units):
        grads = []
        for n, l in units:
            g = gw[n, l]
            if KIND[n] == "stk":
                g = g[:, :, :GLA_IN].reshape(1, D, N_DEV, GLA_IN // N_DEV).transpose(2, 0, 1, 3)
            grads.append(g)
        return scatter_plan(grads, [KIND[n] for n, _ in units], [(1,) + tuple(w[n].shape[1:]) for n, _ in units])

    results = {n: None for n in BIG}

    def update(units, parts):
        for (n, l), p in zip(units, parts):
            results[n] = adamw_layer(w[n], m[n], v[n], p, l, results[n], name=f"adamw_{n}_{l}")

    gs = {"ln_g": [[None, None] for _ in range(DEPTH)], "ln_b": [[None, None] for _ in range(DEPTH)],
          "gla_w_gk2": [None] * 2, "gla_b_gk": [None] * 2, "gla_g_norm": [None] * 2, "att_b_in": [None] * 2,
          "att_rel_bias": [None] * 2}
    dmods = [[None] * 6 for _ in range(DEPTH)]
    nxt = None
    nxt_slot = None
    for i in reversed(range(DEPTH)):
        rec = saved[i]
        sh1, sc1, g1, sh2, sc2, g2 = [mods[i, k:k + 1] for k in range(6)]
        x0, x1 = rec["x0"], rec["x1"]
        if nxt is None:
            dz2, acc = ln_bwd(x1, rec["y2"], g2, _vec(sm["ln_g"][i, 1]), dout=dy, name=f"ln_ff_bwd_{i}")
        else:
            dz2, acc = ln_bwd(x1, rec["y2"], g2, _vec(sm["ln_g"][i, 1]), nxt=nxt, name=f"ln_ff_bwd_{i}")
            dmods[nxt_slot[0]][nxt_slot[1]] = acc[3]
            dmods[nxt_slot[0]][nxt_slot[2]] = acc[4]
        gs["ln_g"][i][1], gs["ln_b"][i][1], dmods[i][5] = acc[0], acc[1], acc[2]
        dh = mm_nt([dz2], wts["ff_w2", i], 0, pro="scale", p1=g2, epi_h=rec["h"], out_dtype=BF16, tm=1024, tn=1024,
                   name=f"ff_down_bwd_{i}")
        wgrad("ff_w2", i, rec["h"], dz2, pro="relu2", dscale=g2, tk=2048, tn=1024, name=f"ff_w2_grad_{i}")
        du2 = mm_nt([dh], wts["ff_w1", i], 0, tm=1024, tn=1024, name=f"ff_up_bwd_{i}")
        wgrad("ff_w1", i, x1, dh, pro="mod", p1=sc2, p2=sh2, tn=2048, name=f"ff_w1_grad_{i}")
        dz1, acc = ln_bwd(x0, rec["y"], g1, _vec(sm["ln_g"][i, 0]), nxt=(dz2, du2, sc2, x1), name=f"ln_mix_bwd_{i}")
        dmods[i][4], dmods[i][3] = acc[3], acc[4]
        gs["ln_g"][i][0], gs["ln_b"][i][0], dmods[i][2] = acc[0], acc[1], acc[2]
        j = rec["j"]
        w_in, w_out = _layer_weights(i)[:2]
        if rec["kind"] == "gla":
            dog = mm_nt([dz1], wts[w_out], 0, pro="scale", p1=g1, tm=1024, tn=1024, name=f"gla_out_bwd_{i}")
            wgrad(*w_out, rec["og"], dz1, dscale=g1, tn=1024, name=f"gla_wout_grad_{i}")
        else:
            do = mm_nt([dz1], wts[w_out], 0, pro="scale", p1=g1, out_dtype=BF16, tm=1024, tn=1024,
                       name=f"att_out_bwd_{i}")
            wgrad(*w_out, rec["o"], dz1, dscale=g1, tn=1024, name=f"att_wout_grad_{i}")
        units = [("ff_w1", i), ("ff_w2", i), w_out] + ([_layer_weights(i + 1)[0]] if i + 1 < DEPTH else [])
        plan = scatter_layer(units)
        if rec["kind"] == "gla":
            (dproj, dw2p, dbgk, dgn), parts = gla_bwd(rec["proj"], dog, rec["states"], rec["w2p"], rec["bgk"],
                                                      rec["gn"], comm=plan, name=f"gla_core_bwd_{i}")
            gs["gla_w_gk2"][j], gs["gla_b_gk"][j], gs["gla_g_norm"][j] = dw2p[:GLA_RANK], dbgk[0], dgn[0]
            wgrad(*w_in, x0, dproj, pro="mod", p1=sc1, p2=sh1, tk=512, tn=GLA_INP, name=f"gla_win_grad_{i}")
            if i == 0:
                du1, last_parts = mm_nt([dproj], wts[w_in], 0, tm=1024, tn=1024, comm=scatter_layer([w_in]),
                                        name=f"gla_proj_bwd_{i}")
                update([w_in], last_parts)
            else:
                du1 = mm_nt([dproj], wts[w_in], 0, tm=1024, tn=1024, name=f"gla_proj_bwd_{i}")
        else:
            (dq, dk, dv, dtiles, dclip, sq, sk, sv), parts = attn_bwd(rec["qkv"], do, rec["tiles"], rec["clip"],
                                                                      comm=plan, name=f"att_core_bwd_{i}")
            drel = rel_bias_grad(dtiles.reshape(ATT_H, -1), dclip.reshape(ATT_H, 128), rel_idx,
                                 name=f"att_bias_grad_{i}")
            gs["att_rel_bias"][j] = drel[:, :N_REL]
            gs["att_b_in"][j] = jnp.concatenate([sq[0], sk[0], sv[0]])
            du1 = mm_nt([dq, dk, dv], wts[w_in], 0, tm=512, tn=1024, name=f"att_proj_bwd_{i}")
            for n, t in enumerate((dq, dk, dv)):
                wgrad(*w_in, x0, t, pro="mod", p1=sc1, p2=sh1, tn=1024, col_block0=n, name=f"att_win_grad_{i}_{n}")
        update(units, parts)
        nxt = (dz1, du1, sc1, x0)
        nxt_slot = (i, 1, 0)
    dx, acc = combine_final(nxt[0], nxt[1], nxt[2], nxt[3], name="grad_x")
    dmods[0][1], dmods[0][0] = acc[3], acc[4]
    dmods = jnp.stack([jnp.stack(r) for r in dmods])
    gs = {k: jnp.stack([jnp.stack(r) if isinstance(r, list) else r for r in v]) for k, v in gs.items()}
    return loss, dx, dmods, gs, results


WEIGHTS = ("w_ada", "b_ada", "ln_g", "ln_b", "gla_w_in", "gla_w_gk2", "gla_b_gk", "gla_g_norm", "gla_w_out",
           "att_w_in", "att_b_in", "att_rel_bias", "att_w_out", "ff_w1", "ff_w2")
SMALL_SHARDED = {"ln_g": 2, "ln_b": 2, "gla_w_gk2": 2, "gla_g_norm": 2, "att_b_in": 1}
SMALL_GRADS = ("ln_g", "ln_b", "gla_w_gk2", "gla_b_gk", "gla_g_norm", "att_b_in", "att_rel_bias")


def kernel(x, c, w_ada, b_ada, ln_g, ln_b, gla_w_in, gla_w_gk2, gla_b_gk, gla_g_norm, gla_w_out, att_w_in, att_b_in, att_rel_bias, att_w_out, ff_w1, ff_w2, loss_target, m_w_ada, m_b_ada, m_ln_g, m_ln_b, m_gla_w_in, m_gla_w_gk2, m_gla_b_gk, m_gla_g_norm, m_gla_w_out, m_att_w_in, m_att_b_in, m_att_rel_bias, m_att_w_out, m_ff_w1, m_ff_w2, v_w_ada, v_b_ada, v_ln_g, v_ln_b, v_gla_w_in, v_gla_w_gk2, v_gla_b_gk, v_gla_g_norm, v_gla_w_out, v_att_w_in, v_att_b_in, v_att_rel_bias, v_att_w_out, v_ff_w1, v_ff_w2):
    w = dict(w_ada=w_ada, b_ada=b_ada, ln_g=ln_g, ln_b=ln_b, gla_w_in=gla_w_in, gla_w_gk2=gla_w_gk2,
             gla_b_gk=gla_b_gk, gla_g_norm=gla_g_norm, gla_w_out=gla_w_out, att_w_in=att_w_in, att_b_in=att_b_in,
             att_rel_bias=att_rel_bias, att_w_out=att_w_out, ff_w1=ff_w1, ff_w2=ff_w2)
    m = dict(w_ada=m_w_ada, b_ada=m_b_ada, ln_g=m_ln_g, ln_b=m_ln_b, gla_w_in=m_gla_w_in, gla_w_gk2=m_gla_w_gk2,
             gla_b_gk=m_gla_b_gk, gla_g_norm=m_gla_g_norm, gla_w_out=m_gla_w_out, att_w_in=m_att_w_in,
             att_b_in=m_att_b_in, att_rel_bias=m_att_rel_bias, att_w_out=m_att_w_out, ff_w1=m_ff_w1, ff_w2=m_ff_w2)
    v = dict(w_ada=v_w_ada, b_ada=v_b_ada, ln_g=v_ln_g, ln_b=v_ln_b, gla_w_in=v_gla_w_in, gla_w_gk2=v_gla_w_gk2,
             gla_b_gk=v_gla_b_gk, gla_g_norm=v_gla_g_norm, gla_w_out=v_gla_w_out, att_w_in=v_att_w_in,
             att_b_in=v_att_b_in, att_rel_bias=v_att_rel_bias, att_w_out=v_att_w_out, ff_w1=v_ff_w1, ff_w2=v_ff_w2)
    xi, yi, ci = _my_place()
    me = 4 * xi + 2 * yi + ci

    small_names = tuple(SMALL_SHARDED)
    small_in = _pack_small([c] + [w[n] for n in small_names])
    small_all = all_gather(small_in, name="gather_small").reshape(N_DEV, -1, 128)
    parts = _unpack_small(small_all, [c.shape] + [w[n].shape for n in small_names])
    c_all = parts[0].reshape(N_DEV, D)
    sm = {"gla_b_gk": gla_b_gk, "att_rel_bias": att_rel_bias}
    for n, p in zip(small_names, parts[1:]):
        ax = SMALL_SHARDED[n]
        sm[n] = jnp.moveaxis(p, 0, ax).reshape(p.shape[1:ax + 1] + (N_DEV * p.shape[ax + 1],) + p.shape[ax + 2:])

    n_ada = w_ada.shape[2]
    mp = mods_partial(c_all, w_ada, name="mods_partial")
    mp_all = all_gather(mp.reshape(DEPTH * N_DEV, n_ada), name="gather_mods")
    mp_all = mp_all.reshape(N_DEV, DEPTH, N_DEV, n_ada)
    mods = lax.dynamic_index_in_dim(mp_all, me, axis=2, keepdims=False)
    mods = mods.transpose(1, 0, 2).reshape(DEPTH, 6 * D) + b_ada
    mods = mods.reshape(DEPTH, 6, D)

    loss, dx, dmods, gs, results = _trunk(x.reshape(x.shape[1:]), loss_target.reshape(x.shape[1:]), mods, sm, w, m, v)
    loss = lax.psum(loss[0, 0], ("x", "y", "c"))

    dm_flat = dmods.reshape(DEPTH, 6 * D)
    small_g = [dm_flat] + [gs[n].reshape(sm[n].shape) for n in SMALL_GRADS]
    small_shapes = [a.shape for a in small_g]
    sg_all = all_gather(_pack_small(small_g), name="gather_small_grads").reshape(N_DEV, -1, 128)
    summed = _unpack_small(sum_parts(sg_all, name="sum_small_grads"), small_shapes)
    g_full = dict(zip(("b_ada",) + SMALL_GRADS, summed))
    dm_all = _unpack_small(sg_all, small_shapes)[0]
    dm_mine = lax.dynamic_slice_in_dim(dm_all, me * n_ada, n_ada, axis=2).transpose(1, 0, 2)
    g_w_ada = w_ada_grad(c_all, dm_mine, name="w_ada_grad")

    results["w_ada"] = adamw_nd(w_ada, m_w_ada, v_w_ada, g_w_ada[None], name="adamw_w_ada")
    for n in ("b_ada",) + SMALL_GRADS:
        g = g_full[n]
        if n in SMALL_SHARDED:
            ax = SMALL_SHARDED[n]
            width = w[n].shape[ax]
            g = lax.dynamic_slice_in_dim(g, me * width, width, axis=ax)
        results[n] = adamw_nd(w[n], m[n], v[n], g[None], name=f"adamw_{n}")

    out = [loss, dx[None]]
    for k in range(4):
        out += [results[n][k] for n in WEIGHTS]
    return tuple(out)
```

```python
import numpy as np
import jax
import jax.numpy as jnp
from jax import lax
from jax.experimental import pallas as pl
from jax.experimental.pallas import tpu as pltpu

F32 = jnp.float32
BF16 = jnp.bfloat16
HIGHEST = lax.Precision.HIGHEST
MESH = pl.DeviceIdType.MESH

N_DEV = 8
D = 1024
DEPTH = 4
CHUNK = 64
ALPHA = (2.0 * DEPTH) ** 0.25
LN_EPS = 1e-5
RMS_EPS = 1e-6
NEG_INF = -1e30

GLA_H = 4
GLA_DKH = 128
GLA_DVH = 256
GLA_DK = GLA_H * GLA_DKH
GLA_DV = GLA_H * GLA_DVH
GLA_RANK = 16
GLA_IN = 2 * GLA_DK + 2 * GLA_DV + GLA_RANK
GLA_INP = 3200
GLA_TAU_INV = 1.0 / 16.0
GLA_SUB = 2

ATT_H = 16
ATT_HD = 64
ATT_QB = 256
ATT_KB = 3 * ATT_QB
LEFT = 8 * CHUNK
MAX_REL = 128
N_REL = 2 * MAX_REL + 1
REL_PAD = 384
REL_TILE = 128
REL_TILES = (3, 4)
D_FF = 4 * D

ADAM_LR = 0.001
ADAM_B1 = 0.9
ADAM_B2 = 0.999
ADAM_EPS = 1e-08
ADAM_WD = 0.01
ADAM_STEP = 10

VMEM_LIMIT = 48 * 1024 * 1024


def _params(n_axes):
    return pltpu.CompilerParams(dimension_semantics=("arbitrary",) * n_axes, vmem_limit_bytes=VMEM_LIMIT)


def _dot(a, b):
    return jnp.dot(a, b, preferred_element_type=F32)


def _dot_nt(a, b):
    return lax.dot_general(a, b, (((1,), (1,)), ((), ())), preferred_element_type=F32)


def _dot_tn(a, b):
    return lax.dot_general(a, b, (((0,), (0,)), ((), ())), preferred_element_type=F32)


def _bf(a):
    return a.astype(BF16)


def _prologue(kind, a, p1=None, p2=None):
    if kind == "mod":
        return a.astype(F32) * (1.0 + p1) + p2
    if kind == "scale":
        return a.astype(F32) * (1.0 + p1)
    if kind == "relu2":
        r = jnp.maximum(a, 0.0)
        return r * r
    return a


class Hosted:
    def __init__(self, inputs, out_shapes, sems, first, last):
        self.inputs, self.out_shapes, self.sems, self.first, self.last = inputs, out_shapes, sems, first, last


def _hbm_specs(n):
    return [pl.BlockSpec(memory_space=pltpu.HBM)] * n


def _call_hosting(body, comm, *, first, last, in_specs, out_specs, out_shape, scratch_shapes, args, **kw):
    if comm is None:
        return pl.pallas_call(body, in_specs=in_specs, out_specs=out_specs, out_shape=out_shape,
                              scratch_shapes=scratch_shapes, **kw)(*args), []
    n_in, n_out, n_scr = len(in_specs), len(out_specs), len(scratch_shapes)
    n_ci, n_co = len(comm.inputs), len(comm.out_shapes)

    def hosting(*refs):
        ins, ci = refs[:n_in], refs[n_in:n_in + n_ci]
        k = n_in + n_ci
        outs, co = refs[k:k + n_out], refs[k + n_out:k + n_out + n_co]
        k += n_out + n_co
        scr, cs = refs[k:k + n_scr], refs[k + n_scr:]

        @pl.when(first())
        def _():
            comm.first(ci, co, cs)

        body(*ins, *outs, *scr)

        @pl.when(last())
        def _():
            comm.last(ci, co, cs)

    res = pl.pallas_call(
        hosting, in_specs=list(in_specs) + _hbm_specs(n_ci), out_specs=list(out_specs) + _hbm_specs(n_co),
        out_shape=list(out_shape) + list(comm.out_shapes), scratch_shapes=list(scratch_shapes) + list(comm.sems),
        **kw)(*args, *comm.inputs)
    return res[:n_out], res[n_out:]


def run_hosted(comm, *, name):
    n_i, n_o = len(comm.inputs), len(comm.out_shapes)

    def body(*refs):
        ins, outs, sems = refs[:n_i], refs[n_i:n_i + n_o], refs[n_i + n_o:]
        comm.first(ins, outs, sems)
        comm.last(ins, outs, sems)

    return pl.pallas_call(body, name=name, out_shape=list(comm.out_shapes), in_specs=_hbm_specs(n_i),
                          out_specs=_hbm_specs(n_o), scratch_shapes=list(comm.sems))(*comm.inputs)


def mm_nn(a, b, layer, *, pro=None, p1=None, p2=None, bias=None, out_dtype=F32, tm, tn, comm=None, name):
    M, K = a.shape
    N = b.shape[2]
    tm = min(tm, M)
    n_p = {"mod": 2, "scale": 1}.get(pro, 0)
    has_bias = bias is not None
    direct = pro is None and a.dtype == BF16

    def body(*refs):
        a_ref, b_ref = refs[0], refs[1]
        p_refs = refs[2:2 + n_p]
        bias_ref = refs[2 + n_p] if has_bias else None
        if direct:
            o_ref = refs[-1]
            lhs = a_ref[...]
        else:
            o_ref, abf = refs[-2], refs[-1]

            @pl.when(pl.program_id(1) == 0)
            def _():
                abf[...] = _bf(_prologue(pro, a_ref[...].astype(F32), *[r[...] for r in p_refs]))

            lhs = abf[...]
        acc = _dot(lhs, b_ref[...])
        if has_bias:
            acc = acc + bias_ref[...]
        o_ref[...] = acc.astype(out_dtype)

    in_specs = [pl.BlockSpec((tm, K), lambda i, j: (i, 0)), pl.BlockSpec((None, K, tn), lambda i, j: (layer, 0, j))]
    args = [a, b]
    for p in (p1, p2)[:n_p]:
        in_specs.append(pl.BlockSpec((1, K), lambda i, j: (0, 0)))
        args.append(p)
    if has_bias:
        in_specs.append(pl.BlockSpec((1, tn), lambda i, j: (0, j)))
        args.append(bias)
    n_i, n_j = M // tm, N // tn
    (out,), got = _call_hosting(
        body, comm, first=lambda: (pl.program_id(0) == 0) & (pl.program_id(1) == 0),
        last=lambda: (pl.program_id(0) == n_i - 1) & (pl.program_id(1) == n_j - 1),
        name=name, grid=(n_i, n_j), in_specs=in_specs,
        out_specs=[pl.BlockSpec((tm, tn), lambda i, j: (i, j))],
        out_shape=[jax.ShapeDtypeStruct((M, N), out_dtype)],
        scratch_shapes=[] if direct else [pltpu.VMEM((tm, K), BF16)], compiler_params=_params(2), args=args)
    return out if comm is None else (out, got)


def mm_nn_ksplit(a, b, layer, *, pro=None, tm, tk, name):
    M, K = a.shape
    N = b.shape[2]
    tm = min(tm, M)

    def body(a_ref, b_ref, o_ref):
        @pl.when(pl.program_id(1) == 0)
        def _():
            o_ref[...] = jnp.zeros_like(o_ref)

        o_ref[...] += _dot(_bf(_prologue(pro, a_ref[...])), b_ref[...])

    return pl.pallas_call(
        body, name=name, grid=(M // tm, K // tk),
        in_specs=[pl.BlockSpec((tm, tk), lambda i, k: (i, k)), pl.BlockSpec((None, tk, N), lambda i, k: (layer, k, 0))],
        out_specs=pl.BlockSpec((tm, N), lambda i, k: (i, 0)),
        out_shape=jax.ShapeDtypeStruct((M, N), F32), compiler_params=_params(2),
    )(a, b)


def mm_nt(a_parts, w, layer, *, pro=None, p1=None, epi_h=None, out_dtype=F32, tm, tn, comm=None, name):
    M = a_parts[0].shape[0]
    tm = min(tm, M)
    widths = [p.shape[1] for p in a_parts]
    Nw = sum(widths)
    Kw = w.shape[1]
    n_a = len(a_parts)
    has_p = pro == "scale"
    has_h = epi_h is not None
    direct = n_a == 1 and not has_p and a_parts[0].dtype == BF16

    def body(*refs):
        a_refs = refs[:n_a]
        w_ref = refs[n_a]
        k = n_a + 1
        p_ref = refs[k] if has_p else None
        k += int(has_p)
        h_ref = refs[k] if has_h else None
        if direct:
            o_ref = refs[-1]
            lhs = a_refs[0][...]
        else:
            o_ref, abf = refs[-2], refs[-1]

            @pl.when(pl.program_id(1) == 0)
            def _():
                off = 0
                for r, wd in zip(a_refs, widths):
                    av = r[...]
                    if has_p:
                        av = av.astype(F32) * (1.0 + p_ref[...])
                    abf[:, off:off + wd] = _bf(av)
                    off += wd

            lhs = abf[...]
        acc = _dot_nt(lhs, w_ref[...])
        if has_h:
            acc = acc * (2.0 * jnp.maximum(h_ref[...], 0.0))
        o_ref[...] = acc.astype(out_dtype)

    in_specs = [pl.BlockSpec((tm, wd), lambda i, j: (i, 0)) for wd in widths]
    in_specs.append(pl.BlockSpec((None, tn, Nw), lambda i, j: (layer, j, 0)))
    args = list(a_parts) + [w]
    if has_p:
        in_specs.append(pl.BlockSpec((1, Nw), lambda i, j: (0, 0)))
        args.append(p1)
    if has_h:
        in_specs.append(pl.BlockSpec((tm, tn), lambda i, j: (i, j)))
        args.append(epi_h)
    n_i, n_j = M // tm, Kw // tn
    (out,), got = _call_hosting(
        body, comm, first=lambda: (pl.program_id(0) == 0) & (pl.program_id(1) == 0),
        last=lambda: (pl.program_id(0) == n_i - 1) & (pl.program_id(1) == n_j - 1),
        name=name, grid=(n_i, n_j), in_specs=in_specs,
        out_specs=[pl.BlockSpec((tm, tn), lambda i, j: (i, j))],
        out_shape=[jax.ShapeDtypeStruct((M, Kw), out_dtype)],
        scratch_shapes=[] if direct else [pltpu.VMEM((tm, Nw), BF16)], compiler_params=_params(2), args=args)
    return out if comm is None else (out, got)


def mm_tn(a, d, *, pro=None, p1=None, p2=None, dscale=None, tk, tn, tm, out_buf, out_shape, col_block0=0, name):
    M, Kf = a.shape
    N = d.shape[1]
    n_p = {"mod": 2}.get(pro, 0)
    has_ds = dscale is not None
    has_buf = out_buf is not None
    n_m = M // tm

    def body(*refs):
        a_ref, d_ref = refs[0], refs[1]
        p_refs = refs[2:2 + n_p]
        ds_ref = refs[2 + n_p] if has_ds else None
        o_ref, acc = refs[-2], refs[-1]
        m = pl.program_id(2)

        @pl.when(m == 0)
        def _():
            acc[...] = jnp.zeros_like(acc)

        av = _prologue(pro, a_ref[...], *[r[...] for r in p_refs])
        dv = d_ref[...]
        if has_ds:
            dv = dv.astype(F32) * (1.0 + ds_ref[...])
        acc[...] += _dot_tn(_bf(av), _bf(dv))

        @pl.when(m == n_m - 1)
        def _():
            o_ref[...] = _bf(acc[...])

    in_specs = [pl.BlockSpec((tm, tk), lambda i, j, m: (m, i)), pl.BlockSpec((tm, tn), lambda i, j, m: (m, j))]
    args = [a, d]
    for p in (p1, p2)[:n_p]:
        in_specs.append(pl.BlockSpec((1, tk), lambda i, j, m: (0, i)))
        args.append(p)
    if has_ds:
        in_specs.append(pl.BlockSpec((1, tn), lambda i, j, m: (0, j)))
        args.append(dscale)
    aliases = {}
    if has_buf:
        in_specs.append(pl.BlockSpec(memory_space=pl.ANY))
        args.append(out_buf)
        aliases = {len(args) - 1: 0}
    return pl.pallas_call(
        body, name=name, grid=(Kf // tk, N // tn, n_m), in_specs=in_specs,
        out_specs=pl.BlockSpec((None, tk, tn), lambda i, j, m: (0, i, col_block0 + j)),
        out_shape=jax.ShapeDtypeStruct(out_shape, BF16), input_output_aliases=aliases,
        scratch_shapes=[pltpu.VMEM((tk, tn), F32)], compiler_params=_params(3),
    )(*args)


ROW_BLOCK = 512
ACC_ROWS = 8


def _ln_stats(z):
    mu = jnp.mean(z, axis=-1, keepdims=True)
    zc = z - mu
    var = jnp.mean(zc * zc, axis=-1, keepdims=True)
    return zc, lax.rsqrt(var + LN_EPS)


def ln_fwd(x, y, gate, lng, lnb, *, name):
    S = x.shape[0]

    def body(x_ref, y_ref, gt_ref, g_ref, b_ref, o_ref):
        z = ALPHA * x_ref[...] + (1.0 + gt_ref[...]) * y_ref[...]
        zc, rstd = _ln_stats(z)
        o_ref[...] = (zc * rstd) * g_ref[...] + b_ref[...]

    row = pl.BlockSpec((ROW_BLOCK, D), lambda i: (i, 0))
    vec = pl.BlockSpec((1, D), lambda i: (0, 0))
    return pl.pallas_call(
        body, name=name, grid=(S // ROW_BLOCK,), in_specs=[row, row, vec, vec, vec], out_specs=row,
        out_shape=jax.ShapeDtypeStruct((S, D), F32), compiler_params=_params(1),
    )(x, y, gate, lng, lnb)


def _add_colsum(acc_ref, r, val):
    acc_ref[r:r + 1, :] += jnp.sum(val, axis=0, keepdims=True)


def ln_bwd(x_in, y, gate, lng, *, dout=None, nxt=None, name):
    S = x_in.shape[0]
    has_next = nxt is not None

    def body(*refs):
        if has_next:
            dzn_ref, dun_ref, scn_ref, b_ref = refs[:4]
            k = 4
        else:
            do_ref = refs[0]
            k = 1
        x_ref, y_ref, gt_ref, g_ref = refs[k:k + 4]
        dz_ref, acc_ref = refs[k + 4:]

        @pl.when(pl.program_id(0) == 0)
        def _():
            acc_ref[...] = jnp.zeros_like(acc_ref)

        yv = y_ref[...]
        z = ALPHA * x_ref[...] + (1.0 + gt_ref[...]) * yv
        zc, rstd = _ln_stats(z)
        xhat = zc * rstd
        if has_next:
            du = dun_ref[...]
            dout_v = ALPHA * dzn_ref[...] + du * (1.0 + scn_ref[...])
            _add_colsum(acc_ref, 3, du * (xhat * g_ref[...] + b_ref[...]))
            _add_colsum(acc_ref, 4, du)
        else:
            dout_v = do_ref[...]
        _add_colsum(acc_ref, 0, dout_v * xhat)
        _add_colsum(acc_ref, 1, dout_v)
        dxh = dout_v * g_ref[...]
        m1 = jnp.mean(dxh, axis=-1, keepdims=True)
        m2 = jnp.mean(dxh * xhat, axis=-1, keepdims=True)
        dz = rstd * (dxh - m1 - xhat * m2)
        _add_colsum(acc_ref, 2, dz * yv)
        dz_ref[...] = dz

    row = pl.BlockSpec((ROW_BLOCK, D), lambda i: (i, 0))
    vec = pl.BlockSpec((1, D), lambda i: (0, 0))
    if has_next:
        in_specs = [row, row, vec, vec]
        args = list(nxt)
    else:
        in_specs = [row]
        args = [dout]
    in_specs += [row, row, vec, vec]
    args += [x_in, y, gate, lng]
    return pl.pallas_call(
        body, name=name, grid=(S // ROW_BLOCK,), in_specs=in_specs,
        out_specs=[row, pl.BlockSpec((ACC_ROWS, D), lambda i: (0, 0))],
        out_shape=[jax.ShapeDtypeStruct((S, D), F32), jax.ShapeDtypeStruct((ACC_ROWS, D), F32)],
        compiler_params=_params(1),
    )(*args)


def combine_final(dz, du, sc, x_in, *, name):
    S = dz.shape[0]

    def body(dz_ref, du_ref, sc_ref, x_ref, dx_ref, acc_ref):
        @pl.when(pl.program_id(0) == 0)
        def _():
            acc_ref[...] = jnp.zeros_like(acc_ref)

        du_v = du_ref[...]
        dx_ref[...] = ALPHA * dz_ref[...] + du_v * (1.0 + sc_ref[...])
        _add_colsum(acc_ref, 3, du_v * x_ref[...])
        _add_colsum(acc_ref, 4, du_v)

    row = pl.BlockSpec((ROW_BLOCK, D), lambda i: (i, 0))
    vec = pl.BlockSpec((1, D), lambda i: (0, 0))
    return pl.pallas_call(
        body, name=name, grid=(S // ROW_BLOCK,), in_specs=[row, row, vec, row],
        out_specs=[row, pl.BlockSpec((ACC_ROWS, D), lambda i: (0, 0))],
        out_shape=[jax.ShapeDtypeStruct((S, D), F32), jax.ShapeDtypeStruct((ACC_ROWS, D), F32)],
        compiler_params=_params(1),
    )(dz, du, sc, x_in)


def colsum(a, *, name):
    S, N = a.shape

    def body(a_ref, o_ref):
        @pl.when(pl.program_id(0) == 0)
        def _():
            o_ref[...] = jnp.zeros_like(o_ref)

        o_ref[...] += jnp.sum(a_ref[...].astype(F32), axis=0, keepdims=True)

    return pl.pallas_call(
        body, name=name, grid=(S // ROW_BLOCK,), in_specs=[pl.BlockSpec((ROW_BLOCK, N), lambda i: (i, 0))],
        out_specs=pl.BlockSpec((1, N), lambda i: (0, 0)), out_shape=jax.ShapeDtypeStruct((1, N), F32),
        compiler_params=_params(1),
    )(a)


def loss_head(y, t, *, name):
    S = y.shape[0]

    def body(y_ref, t_ref, dy_ref, l_ref):
        @pl.when(pl.program_id(0) == 0)
        def _():
            l_ref[...] = jnp.zeros_like(l_ref)

        e = y_ref[...] - t_ref[...]
        dy_ref[...] = e * (1.0 / D)
        per_tok = jnp.sum(e * e, axis=1, keepdims=True) * (1.0 / D)
        l_ref[...] += 0.5 * jnp.sum(per_tok, axis=0, keepdims=True)

    row = pl.BlockSpec((ROW_BLOCK, D), lambda i: (i, 0))
    return pl.pallas_call(
        body, name=name, grid=(S // ROW_BLOCK,), in_specs=[row, row],
        out_specs=[row, pl.BlockSpec((8, 128), lambda i: (0, 0))],
        out_shape=[jax.ShapeDtypeStruct((S, D), F32), jax.ShapeDtypeStruct((8, 128), F32)],
        compiler_params=_params(1),
    )(y, t)


def _log_sigmoid(x):
    return jnp.minimum(x, 0.0) - jnp.log(1.0 + jnp.exp(-jnp.abs(x)))


def _silu(x):
    return x * (1.0 / (1.0 + jnp.exp(-x)))


def _cumsum_steps(x):
    row = lax.broadcasted_iota(jnp.int32, x.shape, 0)
    step = 1
    while step < x.shape[0]:
        x = x + jnp.where(row >= step, pltpu.roll(x, step, 0), 0.0)
        step *= 2
    return x


@jax.custom_vjp
def _cumsum_rows(x):
    return _cumsum_steps(x)


def _cumsum_rows_fwd(x):
    return _cumsum_steps(x), None


def _cumsum_rows_bwd(_, g):
    return (jnp.sum(g, axis=0, keepdims=True) - _cumsum_steps(g) + g,)


_cumsum_rows.defvjp(_cumsum_rows_fwd, _cumsum_rows_bwd)


def _gla_chunk(q, k, v, g, gk, s0t, w2p, bgk, gn):
    C = q.shape[0]
    row = lax.broadcasted_iota(jnp.int32, (C, C), 0)
    col = lax.broadcasted_iota(jnp.int32, (C, C), 1)
    lower = row >= col
    la = _log_sigmoid(_dot(_bf(gk), _bf(w2p)) + bgk) * GLA_TAU_INV
    outs, states = [], []
    for h in range(GLA_H):
        ks = slice(h * GLA_DKH, (h + 1) * GLA_DKH)
        vs = slice(h * GLA_DVH, (h + 1) * GLA_DVH)
        qh = q[:, ks] * (GLA_DKH ** -0.5)
        kh, vh, gh, lah, s0 = k[:, ks], v[:, vs], g[:, vs], la[:, ks], s0t[h]
        cum = _cumsum_rows(lah)
        e_pos = jnp.exp(cum)
        e_neg = jnp.exp(-cum)
        q_f = qh * e_pos
        a_f = _dot_nt(_bf(q_f), _bf(kh * e_neg))
        a_b = _dot_nt(_bf(qh * e_neg), _bf(kh * e_pos))
        att = jnp.where(lower, a_f, a_b)
        o = _dot(_bf(att), _bf(vh)) + _dot_nt(_bf(q_f), _bf(s0))
        tot = jnp.sum(lah, axis=0, keepdims=True)
        k_end = kh * jnp.exp(tot - cum)
        states.append(s0 * jnp.exp(tot) + _dot_tn(_bf(vh), _bf(k_end)))
        on = o * lax.rsqrt(jnp.mean(o * o, axis=-1, keepdims=True) + RMS_EPS) * gn[:, vs]
        outs.append(on * _silu(gh))
    return jnp.concatenate(outs, axis=1), tuple(states)


def _gla_split(p):
    return (p[:, 0:GLA_DK], p[:, GLA_DK:2 * GLA_DK], p[:, 2 * GLA_DK:2 * GLA_DK + GLA_DV],
            p[:, 2 * GLA_DK + GLA_DV:2 * GLA_DK + 2 * GLA_DV], p[:, 2 * GLA_DK + 2 * GLA_DV:GLA_INP])


def gla_fwd(proj, w2p, bgk, gn, *, comm=None, name):
    S = proj.shape[0]
    n_c = S // CHUNK
    n_s = n_c // GLA_SUB
    rows = GLA_SUB * CHUNK

    def body(p_ref, w_ref, b_ref, gn_ref, o_ref, st_ref, st):
        @pl.when(pl.program_id(0) == 0)
        def _():
            st[...] = jnp.zeros_like(st)

        s = tuple(st[h] for h in range(GLA_H))
        for u in range(GLA_SUB):
            sub = slice(u * CHUNK, (u + 1) * CHUNK)
            for h in range(GLA_H):
                st_ref[u, h] = s[h]
            og, s = _gla_chunk(*_gla_split(p_ref[sub, :]), s, w_ref[...], b_ref[...], gn_ref[...])
            o_ref[sub, :] = _bf(og)
        for h in range(GLA_H):
            st[h] = s[h]

    full = lambda shape: pl.BlockSpec(shape, lambda i: (0,) * len(shape))
    return _call_hosting(
        body, comm, first=lambda: pl.program_id(0) == 0, last=lambda: pl.program_id(0) == n_s - 1,
        name=name, grid=(n_s,),
        in_specs=[pl.BlockSpec((rows, GLA_INP), lambda i: (i, 0)), full((128, GLA_DK)), full((1, GLA_DK)),
                  full((1, GLA_DV))],
        out_specs=[pl.BlockSpec((rows, GLA_DV), lambda i: (i, 0)),
                   pl.BlockSpec((GLA_SUB, GLA_H, GLA_DVH, GLA_DKH), lambda i: (i, 0, 0, 0))],
        out_shape=[jax.ShapeDtypeStruct((S, GLA_DV), BF16),
                   jax.ShapeDtypeStruct((n_c, GLA_H, GLA_DVH, GLA_DKH), F32)],
        scratch_shapes=[pltpu.VMEM((GLA_H, GLA_DVH, GLA_DKH), F32)], compiler_params=_params(1),
        args=(proj, w2p, bgk, gn))


def gla_bwd(proj, dog, states, w2p, bgk, gn, *, comm=None, name):
    S = proj.shape[0]
    n_c = S // CHUNK
    n_s = n_c // GLA_SUB
    rows = GLA_SUB * CHUNK

    def body(p_ref, dog_ref, st_ref, w_ref, b_ref, gn_ref, dp_ref, dw_ref, db_ref, dgn_ref, ds_ref):
        @pl.when(pl.program_id(0) == 0)
        def _():
            ds_ref[...] = jnp.zeros_like(ds_ref)
            dw_ref[...] = jnp.zeros_like(dw_ref)
            db_ref[...] = jnp.zeros_like(db_ref)
            dgn_ref[...] = jnp.zeros_like(dgn_ref)

        ds = tuple(ds_ref[h] for h in range(GLA_H))
        for u in reversed(range(GLA_SUB)):
            sub = slice(u * CHUNK, (u + 1) * CHUNK)
            q, k, v, g, gk = _gla_split(p_ref[sub, :])
            s0 = tuple(st_ref[u, h] for h in range(GLA_H))
            _, vjp = jax.vjp(_gla_chunk, q, k, v, g, gk, s0, w_ref[...], b_ref[...], gn_ref[...])
            dq, dk, dv, dg, dgk, ds, dw, db, dgn = vjp((dog_ref[sub, :], ds))
            dp_ref[sub, 0:GLA_DK] = _bf(dq)
            dp_ref[sub, GLA_DK:2 * GLA_DK] = _bf(dk)
            dp_ref[sub, 2 * GLA_DK:2 * GLA_DK + GLA_DV] = _bf(dv)
            dp_ref[sub, 2 * GLA_DK + GLA_DV:2 * GLA_DK + 2 * GLA_DV] = _bf(dg)
            dp_ref[sub, 2 * GLA_DK + 2 * GLA_DV:GLA_INP] = _bf(dgk)
            dw_ref[...] += dw
            db_ref[...] += db
            dgn_ref[...] += dgn
        for h in range(GLA_H):
            ds_ref[h] = ds[h]

    full = lambda shape: pl.BlockSpec(shape, lambda i: (0,) * len(shape))
    rev = lambda i: (n_s - 1 - i, 0)
    return _call_hosting(
        body, comm, first=lambda: pl.program_id(0) == 0, last=lambda: pl.program_id(0) == n_s - 1,
        name=name, grid=(n_s,),
        in_specs=[pl.BlockSpec((rows, GLA_INP), rev), pl.BlockSpec((rows, GLA_DV), rev),
                  pl.BlockSpec((GLA_SUB, GLA_H, GLA_DVH, GLA_DKH), lambda i: (n_s - 1 - i, 0, 0, 0)),
                  full((128, GLA_DK)), full((1, GLA_DK)), full((1, GLA_DV))],
        out_specs=[pl.BlockSpec((rows, GLA_INP), rev), full((128, GLA_DK)), full((1, GLA_DK)), full((1, GLA_DV))],
        out_shape=[jax.ShapeDtypeStruct((S, GLA_INP), BF16), jax.ShapeDtypeStruct((128, GLA_DK), F32),
                   jax.ShapeDtypeStruct((1, GLA_DK), F32), jax.ShapeDtypeStruct((1, GLA_DV), F32)],
        scratch_shapes=[pltpu.VMEM((GLA_H, GLA_DVH, GLA_DKH), F32)], compiler_params=_params(1),
        args=(proj, dog, states, w2p, bgk, gn))


def _rel_index():
    t = np.arange(REL_TILE)[:, None]
    j = np.arange(REL_TILE)[None, :]
    tiles = []
    for m in REL_TILES:
        chunks = (REL_TILE // CHUNK) * m + j // CHUNK - t // CHUNK
        band = (chunks >= 0) & (chunks <= LEFT // CHUNK)
        dist = LEFT - REL_TILE * m + t - j
        tiles.append(np.where(band, np.minimum(dist, MAX_REL) + MAX_REL, N_REL))
    return jnp.asarray(np.stack(tiles).reshape(1, -1).astype(np.int32))


REL_BLOCK = 2048


def _one_hot(idx_row):
    return (lax.broadcasted_iota(jnp.int32, (REL_PAD, idx_row.shape[1]), 0) == idx_row).astype(F32)


def rel_bias_tiles(rel_pad, idx, *, name):
    E = idx.shape[1]

    def body(r_ref, i_ref, o_ref):
        o_ref[...] = jnp.dot(r_ref[...], _one_hot(i_ref[...]), precision=HIGHEST, preferred_element_type=F32)

    return pl.pallas_call(
        body, name=name, grid=(E // REL_BLOCK,),
        in_specs=[pl.BlockSpec((ATT_H, REL_PAD), lambda i: (0, 0)), pl.BlockSpec((1, REL_BLOCK), lambda i: (0, i))],
        out_specs=pl.BlockSpec((ATT_H, REL_BLOCK), lambda i: (0, i)),
        out_shape=jax.ShapeDtypeStruct((ATT_H, E), F32), compiler_params=_params(1),
    )(rel_pad, idx)


def rel_bias_grad(dtiles_flat, dclip, idx, *, name):
    E = idx.shape[1]
    n_steps = E // REL_BLOCK

    def body(d_ref, c_ref, i_ref, o_ref):
        @pl.when(pl.program_id(0) == 0)
        def _():
            o_ref[...] = jnp.zeros_like(o_ref)

        o_ref[...] += lax.dot_general(d_ref[...], _one_hot(i_ref[...]), (((1,), (1,)), ((), ())),
                                      precision=HIGHEST, preferred_element_type=F32)

        @pl.when(pl.program_id(0) == n_steps - 1)
        def _():
            at_clip = lax.broadcasted_iota(jnp.int32, (1, REL_PAD), 1) == 2 * MAX_REL
            o_ref[...] += jnp.where(at_clip, jnp.sum(c_ref[...], axis=1, keepdims=True), 0.0)

    return pl.pallas_call(
        body, name=name, grid=(n_steps,),
        in_specs=[pl.BlockSpec((ATT_H, REL_BLOCK), lambda i: (0, i)), pl.BlockSpec((ATT_H, 128), lambda i: (0, 0)),
                  pl.BlockSpec((1, REL_BLOCK), lambda i: (0, i))],
        out_specs=pl.BlockSpec((ATT_H, REL_PAD), lambda i: (0, 0)),
        out_shape=jax.ShapeDtypeStruct((ATT_H, REL_PAD), F32), compiler_params=_params(1),
    )(dtiles_flat, dclip, idx)


def _attn_bias(tiles, clip):
    const = jnp.broadcast_to(clip, (REL_TILE, REL_TILE))
    zero = jnp.zeros((REL_TILE, REL_TILE), F32)
    rows = []
    for qt in range(ATT_QB // REL_TILE):
        blocks = []
        for kt in range(ATT_KB // REL_TILE):
            m = kt - qt
            if m in REL_TILES:
                blocks.append(tiles[REL_TILES.index(m)])
            elif 0 <= m < REL_TILES[0]:
                blocks.append(const)
            else:
                blocks.append(zero)
        rows.append(jnp.concatenate(blocks, axis=1))
    return jnp.concatenate(rows, axis=0)


def _attn_bias_grad(ds, dt_ref, dc_ref, a):
    tile = lambda qt, kt: ds[qt * REL_TILE:(qt + 1) * REL_TILE, kt * REL_TILE:(kt + 1) * REL_TILE]
    const = None
    sums = [None] * len(REL_TILES)
    for qt in range(ATT_QB // REL_TILE):
        for kt in range(ATT_KB // REL_TILE):
            m = kt - qt
            if m in REL_TILES:
                n = REL_TILES.index(m)
                sums[n] = tile(qt, kt) if sums[n] is None else sums[n] + tile(qt, kt)
            elif 0 <= m < REL_TILES[0]:
                const = tile(qt, kt) if const is None else const + tile(qt, kt)
    for n, v in enumerate(sums):
        dt_ref[a, n] += v
    dc_ref[a] += jnp.sum(const, axis=0, keepdims=True)


def _attn_head_lanes():
    lane = lax.broadcasted_iota(jnp.int32, (1, 2 * ATT_HD), 1)
    return [(lane >= a * ATT_HD) & (lane < (a + 1) * ATT_HD) for a in range(2)]


def _attn_band_bias(tiles, clip):
    j = lax.broadcasted_iota(jnp.int32, (ATT_QB, ATT_KB), 1)
    t = lax.broadcasted_iota(jnp.int32, (ATT_QB, ATT_KB), 0)
    shift = CHUNK.bit_length() - 1
    chunks = lax.shift_right_logical(j, shift) - lax.shift_right_logical(t, shift)
    band = (chunks >= 0) & (chunks <= LEFT // CHUNK)
    return jnp.where(band, _attn_bias(tiles, clip), NEG_INF)


def _attn_exp(qa, kb, bias, key_bias):
    s = _dot_nt(qa, kb) + bias + key_bias
    e = jnp.exp(s - jnp.max(s, axis=-1, keepdims=True))
    return e, jnp.sum(e, axis=-1, keepdims=True)


def _attn_specs():
    n_hp = ATT_H // 2
    q_spec = pl.BlockSpec((ATT_QB, 128), lambda hp, g: (g, hp))

    def win(col0, back):
        return pl.BlockSpec((ATT_QB, 128), lambda hp, g: (jnp.maximum(g - back, 0), col0 + hp))

    kv_specs = [win(n_hp, 2), win(n_hp, 1), win(n_hp, 0), win(2 * n_hp, 2), win(2 * n_hp, 1), win(2 * n_hp, 0)]
    tiles_spec = pl.BlockSpec((2, len(REL_TILES), REL_TILE, REL_TILE), lambda hp, g: (hp, 0, 0, 0))
    clip_spec = pl.BlockSpec((2, 1, 128), lambda hp, g: (hp, 0, 0))
    return q_spec, kv_specs, tiles_spec, clip_spec


def _attn_window(refs, g):
    kb = jnp.concatenate([_bf(r[...]) for r in refs[0:3]], axis=0)
    vb = jnp.concatenate([_bf(r[...]) for r in refs[3:6]], axis=0)
    j = lax.broadcasted_iota(jnp.int32, (1, ATT_KB), 1)
    return kb, vb, jnp.where(j + (g - 2) * ATT_QB >= 0, 0.0, NEG_INF)


def attn_fwd(qkv, tiles, clip, *, comm=None, name):
    S = qkv.shape[0]
    q_spec, kv_specs, tiles_spec, clip_spec = _attn_specs()

    def body(q_ref, *rest):
        kv_refs, t_ref, c_ref, o_ref, bias = rest[:6], rest[6], rest[7], rest[8], rest[9]
        g = pl.program_id(1)

        @pl.when(g == 0)
        def _():
            for a in range(2):
                bias[a * ATT_QB:(a + 1) * ATT_QB, :] = _attn_band_bias(t_ref[a], c_ref[a])

        kb, vb, key_bias = _attn_window(kv_refs, g)
        q = q_ref[...].astype(F32)
        out = jnp.zeros((ATT_QB, 2 * ATT_HD), F32)
        for a, lanes in enumerate(_attn_head_lanes()):
            mf = lanes.astype(F32)
            e, l = _attn_exp(_bf(q * (mf * ATT_HD ** -0.5)), kb, bias[a * ATT_QB:(a + 1) * ATT_QB, :], key_bias)
            out = out + _dot(_bf(e), vb) * (mf * (1.0 / l))
        o_ref[...] = _bf(out)

    n_hp, n_g = ATT_H // 2, S // ATT_QB
    return _call_hosting(
        body, comm, first=lambda: (pl.program_id(0) == 0) & (pl.program_id(1) == 0),
        last=lambda: (pl.program_id(0) == n_hp - 1) & (pl.program_id(1) == n_g - 1),
        name=name, grid=(n_hp, n_g), in_specs=[q_spec] + kv_specs + [tiles_spec, clip_spec],
        out_specs=[q_spec], out_shape=[jax.ShapeDtypeStruct((S, D), BF16)],
        scratch_shapes=[pltpu.VMEM((2 * ATT_QB, ATT_KB), F32)], compiler_params=_params(2),
        args=(*([qkv] * 7), tiles, clip))


def attn_bwd(qkv, do, tiles, clip, *, comm=None, name):
    S = qkv.shape[0]
    q_spec, kv_specs, tiles_spec, clip_spec = _attn_specs()
    col_spec = pl.BlockSpec((S, 128), lambda hp, g: (0, hp))
    sum_spec = pl.BlockSpec((1, 128), lambda hp, g: (0, hp))
    n_g = S // ATT_QB

    def body(q_ref, *rest):
        kv_refs, t_ref, c_ref, do_ref = rest[:6], rest[6], rest[7], rest[8]
        dq_ref, dk_ref, dv_ref, dt_ref, dc_ref, sq_ref, sk_ref, sv_ref, bias = rest[9:]
        g = pl.program_id(1)

        @pl.when(g == 0)
        def _():
            for a in range(2):
                bias[a * ATT_QB:(a + 1) * ATT_QB, :] = _attn_band_bias(t_ref[a], c_ref[a])
            dk_ref[...] = jnp.zeros_like(dk_ref)
            dv_ref[...] = jnp.zeros_like(dv_ref)
            dt_ref[...] = jnp.zeros_like(dt_ref)
            dc_ref[...] = jnp.zeros_like(dc_ref)
            sq_ref[...] = jnp.zeros_like(sq_ref)

        kb, vb, key_bias = _attn_window(kv_refs, g)
        q = q_ref[...].astype(F32)
        do = do_ref[...]
        lanes = _attn_head_lanes()
        mf = [m.astype(F32) * ATT_HD ** -0.5 for m in lanes]
        qs = _bf(jnp.concatenate([q * m for m in mf], axis=0))
        dos = jnp.concatenate([jnp.where(m, do, jnp.zeros_like(do)) for m in lanes], axis=0)
        e, l = _attn_exp(qs, kb, bias[...], key_bias)
        p = e * (1.0 / l)
        dp = _dot_nt(dos, vb)
        ds = p * (dp - jnp.sum(p * dp, axis=-1, keepdims=True))
        ds_b = _bf(ds)
        dq2 = _dot(ds_b, kb)
        dq = dq2[:ATT_QB] * mf[0] + dq2[ATT_QB:] * mf[1]
        dkw = _dot_tn(ds_b, qs)
        dvw = _dot_tn(_bf(p), dos)
        for a in range(2):
            _attn_bias_grad(ds[a * ATT_QB:(a + 1) * ATT_QB], dt_ref, dc_ref, a)
        dq_ref[...] = _bf(dq)
        sq_ref[...] += jnp.sum(dq, axis=0, keepdims=True)
        for blk in range(3):
            src = g - 2 + blk

            @pl.when(src >= 0)
            def _(blk=blk, src=src):
                rows = pl.ds(pl.multiple_of(src * ATT_QB, ATT_QB), ATT_QB)
                dk_ref[rows, :] += dkw[blk * ATT_QB:(blk + 1) * ATT_QB]
                dv_ref[rows, :] += dvw[blk * ATT_QB:(blk + 1) * ATT_QB]

        @pl.when(g == n_g - 1)
        def _():
            sk_ref[...] = jnp.sum(dk_ref[...], axis=0, keepdims=True)
            sv_ref[...] = jnp.sum(dv_ref[...], axis=0, keepdims=True)

    n_hp, n_g = ATT_H // 2, S // ATT_QB
    return _call_hosting(
        body, comm, first=lambda: (pl.program_id(0) == 0) & (pl.program_id(1) == 0),
        last=lambda: (pl.program_id(0) == n_hp - 1) & (pl.program_id(1) == n_g - 1),
        name=name, grid=(n_hp, n_g),
        in_specs=[q_spec] + kv_specs + [tiles_spec, clip_spec, q_spec],
        out_specs=[q_spec, col_spec, col_spec, tiles_spec, clip_spec] + [sum_spec] * 3,
        out_shape=[jax.ShapeDtypeStruct((S, D), BF16)] + [jax.ShapeDtypeStruct((S, D), F32)] * 2
        + [jax.ShapeDtypeStruct((ATT_H, len(REL_TILES), REL_TILE, REL_TILE), F32),
           jax.ShapeDtypeStruct((ATT_H, 1, 128), F32)] + [jax.ShapeDtypeStruct((1, D), F32)] * 3,
        scratch_shapes=[pltpu.VMEM((2 * ATT_QB, ATT_KB), F32)], compiler_params=_params(2),
        args=(*([qkv] * 7), tiles, clip, do))


def mods_partial(c_all, w_ada, *, name):
    n_l, _, n_c = w_ada.shape

    def body(c_ref, w_ref, o_ref):
        o_ref[...] = _dot(_bf(_silu(c_ref[...])), _bf(w_ref[...]))

    return pl.pallas_call(
        body, name=name, grid=(n_l,),
        in_specs=[pl.BlockSpec((N_DEV, D), lambda l: (0, 0)), pl.BlockSpec((None, D, n_c), lambda l: (l, 0, 0))],
        out_specs=pl.BlockSpec((None, N_DEV, n_c), lambda l: (l, 0, 0)),
        out_shape=jax.ShapeDtypeStruct((n_l, N_DEV, n_c), F32), compiler_params=_params(1),
    )(c_all, w_ada)


def w_ada_grad(c_all, dm, *, name):
    n_l, _, n_c = dm.shape

    def body(c_ref, d_ref, o_ref):
        o_ref[...] = lax.dot_general(_silu(c_ref[...]), d_ref[...], (((0,), (0,)), ((), ())),
                                     precision=HIGHEST, preferred_element_type=F32)

    return pl.pallas_call(
        body, name=name, grid=(n_l,),
        in_specs=[pl.BlockSpec((N_DEV, D), lambda l: (0, 0)), pl.BlockSpec((None, N_DEV, n_c), lambda l: (l, 0, 0))],
        out_specs=pl.BlockSpec((None, D, n_c), lambda l: (l, 0, 0)),
        out_shape=jax.ShapeDtypeStruct((n_l, D, n_c), F32), compiler_params=_params(1),
    )(c_all, dm)


def adamw(w, m, v, gparts, *, block_rows, name):
    R, C = w.shape
    n = gparts.shape[0]

    def body(w_ref, m_ref, v_ref, g_ref, go_ref, d_ref, mo_ref, vo_ref):
        g = g_ref[0].astype(F32)
        for k in range(1, n):
            g = g + g_ref[k].astype(F32)
        m_new = ADAM_B1 * m_ref[...] + (1.0 - ADAM_B1) * g
        v_new = ADAM_B2 * v_ref[...] + (1.0 - ADAM_B2) * (g * g)
        m_hat = m_new / (1.0 - ADAM_B1 ** ADAM_STEP)
        v_hat = v_new / (1.0 - ADAM_B2 ** ADAM_STEP)
        go_ref[...] = g
        d_ref[...] = -ADAM_LR * (m_hat / (jnp.sqrt(v_hat) + ADAM_EPS) + ADAM_WD * w_ref[...])
        mo_ref[...] = m_new
        vo_ref[...] = v_new

    blk = pl.BlockSpec((block_rows, C), lambda i: (i, 0))
    return pl.pallas_call(
        body, name=name, grid=(R // block_rows,),
        in_specs=[blk, blk, blk, pl.BlockSpec((n, block_rows, C), lambda i: (0, i, 0))],
        out_specs=[blk] * 4, out_shape=[jax.ShapeDtypeStruct((R, C), F32)] * 4, compiler_params=_params(1),
    )(w, m, v, gparts)


def adamw_nd(w, m, v, gparts, *, name):
    shape = w.shape
    two = (int(np.prod(shape[:-1])), shape[-1])
    rows = two[0]
    block_rows = rows
    for cand in (512, 256):
        if rows > cand and rows % cand == 0:
            block_rows = cand
            break
    outs = adamw(w.reshape(two), m.reshape(two), v.reshape(two), gparts.reshape((gparts.shape[0],) + two),
                 block_rows=block_rows, name=name)
    return [o.reshape(shape) for o in outs]


def sum_parts(parts, *, name):
    n, R, C = parts.shape

    def body(p_ref, o_ref):
        acc = p_ref[0]
        for k in range(1, n):
            acc = acc + p_ref[k]
        o_ref[...] = acc

    return pl.pallas_call(
        body, name=name, in_specs=[pl.BlockSpec((n, R, C), lambda: (0, 0, 0))],
        out_specs=pl.BlockSpec((R, C), lambda: (0, 0)), out_shape=jax.ShapeDtypeStruct((R, C), F32),
        compiler_params=pltpu.CompilerParams(vmem_limit_bytes=VMEM_LIMIT),
    )(parts)


def _my_place():
    return lax.axis_index("x"), lax.axis_index("y"), lax.axis_index("c")


def _full_shape(kind, shard):
    n_l, rows, cols = shard
    return {"col": (n_l, rows, N_DEV * cols), "row": (n_l, N_DEV * rows, cols), "stk": (N_DEV, n_l, rows, cols)}[kind]


def _slab(ref, kind, dev, shard):
    _, rows, cols = shard
    if kind == "col":
        return ref.at[:, :, pl.ds(pl.multiple_of(dev * cols, 128), cols)]
    if kind == "row":
        return ref.at[:, pl.ds(pl.multiple_of(dev * rows, 8), rows), :]
    return ref.at[dev]


def _hbm_specs(n):
    return [pl.BlockSpec(memory_space=pltpu.HBM)] * n


def all_gather(x_shard, *, name):
    m_per, n = x_shard.shape

    def body(x_ref, out_ref, send_sems, recv_sems, local_sem):
        x, y, c = _my_place()
        me, sibling = (x, y, c), (x, y, 1 - c)
        chips = [(1 - x, y), (x, 1 - y), (1 - x, 1 - y)]

        def rows(px, py, pc):
            return out_ref.at[pl.ds((4 * px + 2 * py + pc) * m_per, m_per), :]

        def copy(k, block, to, src=None):
            return pltpu.make_async_remote_copy(
                src_ref=rows(*block) if src is None else src, dst_ref=rows(*block),
                send_sem=send_sems.at[k], recv_sem=recv_sems.at[k], device_id=to, device_id_type=MESH)

        mine = pltpu.make_async_copy(x_ref, rows(*me), local_sem)
        mine.start()
        first = [copy(0, me, sibling, src=x_ref)]
        first += [copy(1 + j, me, (*chip, c), src=x_ref) for j, chip in enumerate(chips)]
        for cp in first:
            cp.start()
        passed = [copy(4 + j, (*chip, c), sibling) for j, chip in enumerate(chips)]
        for j, chip in enumerate(chips):
            copy(1 + j, (*chip, c), me).wait_recv()
            passed[j].start()
        copy(0, sibling, me).wait_recv()
        for j, chip in enumerate(chips):
            copy(4 + j, (*chip, 1 - c), me).wait_recv()
        for cp in first + passed:
            cp.wait_send()
        mine.wait()

    return pl.pallas_call(
        body, name=name, out_shape=jax.ShapeDtypeStruct((N_DEV * m_per, n), x_shard.dtype),
        in_specs=[pl.BlockSpec(memory_space=pltpu.VMEM)], out_specs=pl.BlockSpec(memory_space=pltpu.VMEM),
        scratch_shapes=[pltpu.SemaphoreType.DMA((7,)), pltpu.SemaphoreType.DMA((7,)), pltpu.SemaphoreType.DMA],
        compiler_params=pltpu.CompilerParams(vmem_limit_bytes=VMEM_LIMIT),
    )(x_shard)


def gather_plan(shards, kinds, layers):
    n_t = len(shards)
    shapes = [(1,) + tuple(s.shape[1:]) for s in shards]

    def copies(x_refs, out_refs, sems):
        send_sems, recv_sems, local_sems = sems
        x, y, c = _my_place()
        me, sibling = (x, y, c), (x, y, 1 - c)
        chips = [(1 - x, y), (x, 1 - y), (1 - x, 1 - y)]
        own = [x_refs[t].at[pl.ds(layers[t], 1)] for t in range(n_t)]

        def slab(t, px, py, pc):
            return _slab(out_refs[t], kinds[t], 4 * px + 2 * py + pc, shapes[t])

        def copy(t, k, block, to, src=None):
            return pltpu.make_async_remote_copy(
                src_ref=slab(t, *block) if src is None else src, dst_ref=slab(t, *block),
                send_sem=send_sems.at[7 * t + k], recv_sem=recv_sems.at[7 * t + k], device_id=to,
                device_id_type=MESH)

        mine = [pltpu.make_async_copy(own[t], slab(t, *me), local_sems.at[t]) for t in range(n_t)]
        sends = []
        for t in range(n_t):
            sends.append(copy(t, 0, me, sibling, src=own[t]))
            sends += [copy(t, 1 + j, me, (*chip, c), src=own[t]) for j, chip in enumerate(chips)]
        return mine, sends, copy, me, sibling, chips, c

    def first(x_refs, out_refs, sems):
        mine, sends = copies(x_refs, out_refs, sems)[:2]
        for cp in mine + sends:
            cp.start()

    def last(x_refs, out_refs, sems):
        mine, sends, copy, me, sibling, chips, c = copies(x_refs, out_refs, sems)
        passed = []
        for j, chip in enumerate(chips):
            for t in range(n_t):
                copy(t, 1 + j, (*chip, c), me).wait_recv()
                passed.append(copy(t, 4 + j, (*chip, c), sibling))
                passed[-1].start()
        for t in range(n_t):
            copy(t, 0, sibling, me).wait_recv()
        for j, chip in enumerate(chips):
            for t in range(n_t):
                copy(t, 4 + j, (*chip, 1 - c), me).wait_recv()
        for cp in sends + passed:
            cp.wait_send()
        for cp in mine:
            cp.wait()

    return Hosted(
        list(shards), [jax.ShapeDtypeStruct(_full_shape(k, shp), s.dtype) for k, shp, s in zip(kinds, shapes, shards)],
        [pltpu.SemaphoreType.DMA((7 * n_t,)), pltpu.SemaphoreType.DMA((7 * n_t,)), pltpu.SemaphoreType.DMA((n_t,))],
        first, last)


def scatter_plan(grads, kinds, shapes):
    n_t = len(grads)

    def copies(g_refs, out_refs, sems):
        send_sems, recv_sems, local_sems = sems
        x, y, c = _my_place()
        me = 4 * x + 2 * y + c
        local = [pltpu.make_async_copy(_slab(g_refs[t], kinds[t], me, shapes[t]), out_refs[t].at[me],
                                       local_sems.at[t]) for t in range(n_t)]
        remote = []
        for t in range(n_t):
            for r in range(1, N_DEV):
                px = 1 - x if r & 4 else x
                py = 1 - y if r & 2 else y
                pc = 1 - c if r & 1 else c
                remote.append(pltpu.make_async_remote_copy(
                    src_ref=_slab(g_refs[t], kinds[t], 4 * px + 2 * py + pc, shapes[t]), dst_ref=out_refs[t].at[me],
                    send_sem=send_sems.at[7 * t + r - 1], recv_sem=recv_sems.at[7 * t + r - 1],
                    device_id=(px, py, pc), device_id_type=MESH))
        return local, remote

    def first(g_refs, out_refs, sems):
        local, remote = copies(g_refs, out_refs, sems)
        for cp in local + remote:
            cp.start()

    def last(g_refs, out_refs, sems):
        local, remote = copies(g_refs, out_refs, sems)
        for cp in remote + local:
            cp.wait()

    return Hosted(
        list(grads), [jax.ShapeDtypeStruct((N_DEV,) + tuple(s), BF16) for s in shapes],
        [pltpu.SemaphoreType.DMA((7 * n_t,)), pltpu.SemaphoreType.DMA((7 * n_t,)), pltpu.SemaphoreType.DMA((n_t,))],
        first, last)


def adamw_layer(w, m, v, parts, layer, bufs, *, name):
    n_l, rows, cols = w.shape
    n = parts.shape[0]
    tr = min(rows, 256)

    def body(w_ref, m_ref, v_ref, g_ref, *rest):
        go_ref, d_ref, mo_ref, vo_ref = rest[-4:]
        g = g_ref[0].astype(F32)
        for k in range(1, n):
            g = g + g_ref[k].astype(F32)
        m_new = ADAM_B1 * m_ref[...] + (1.0 - ADAM_B1) * g
        v_new = ADAM_B2 * v_ref[...] + (1.0 - ADAM_B2) * (g * g)
        m_hat = m_new / (1.0 - ADAM_B1 ** ADAM_STEP)
        v_hat = v_new / (1.0 - ADAM_B2 ** ADAM_STEP)
        go_ref[...] = g
        d_ref[...] = -ADAM_LR * (m_hat / (jnp.sqrt(v_hat) + ADAM_EPS) + ADAM_WD * w_ref[...])
        mo_ref[...] = m_new
        vo_ref[...] = v_new

    blk = pl.BlockSpec((None, tr, cols), lambda i: (layer, i, 0))
    in_specs = [blk, blk, blk, pl.BlockSpec((n, None, tr, cols), lambda i: (0, 0, i, 0))]
    args = [w, m, v, parts]
    aliases = {}
    if bufs is not None:
        in_specs += [pl.BlockSpec(memory_space=pl.ANY)] * 4
        args += list(bufs)
        aliases = {4 + k: k for k in range(4)}
    return pl.pallas_call(
        body, name=name, grid=(rows // tr,), in_specs=in_specs, out_specs=[blk] * 4,
        out_shape=[jax.ShapeDtypeStruct((n_l, rows, cols), F32)] * 4, input_output_aliases=aliases,
        compiler_params=_params(1),
    )(*args)


BIG =("gla_w_in", "gla_w_out", "att_w_in", "att_w_out", "ff_w1", "ff_w2")
KIND = {"gla_w_in": "stk", "gla_w_out": "row", "att_w_in": "col", "att_w_out": "row", "ff_w1": "col", "ff_w2": "row"}


def _pack_small(arrs):
    parts = []
    for a in arrs:
        f = a.reshape(-1)
        parts.append(jnp.pad(f, (0, -f.shape[0] % 128)))
    flat = jnp.concatenate(parts)
    flat = jnp.pad(flat, (0, -flat.shape[0] % 1024))
    return flat.reshape(-1, 128)


def _unpack_small(packed, shapes):
    flat = packed.reshape(packed.shape[:-2] + (-1,))
    out, off = [], 0
    for shp in shapes:
        n = int(np.prod(shp))
        out.append(flat[..., off:off + n].reshape(packed.shape[:-2] + tuple(shp)))
        off += n + (-n % 128)
    return out


def _vec(a):
    return a.reshape(1, -1)


def _layer_weights(i):
    mixer = "gla" if i % 2 == 0 else "att"
    return [(f"{mixer}_w_in", i // 2), (f"{mixer}_w_out", i // 2), ("ff_w1", i), ("ff_w2", i)]


def _trunk(x, target, mods, sm, w, m, v):
    shard_bf = {n: w[n].astype(BF16) for n in BIG}

    def gather_of(names):
        return gather_plan([shard_bf[n] for n, _ in names], [KIND[n] for n, _ in names], [l for _, l in names])

    def gather_layer(i):
        return _layer_weights(i), gather_of(_layer_weights(i))

    wts = {}

    def keep_gathered(names, arrays):
        for (n, l), a in zip(names, arrays):
            if KIND[n] == "stk":
                a = a.transpose(1, 2, 0, 3).reshape(1, D, GLA_IN)
                a = jnp.pad(a, ((0, 0), (0, 0), (0, GLA_INP - GLA_IN)))
            wts[n, l] = a

    first_names, rest_names = _layer_weights(0)[:1], _layer_weights(0)[1:]
    keep_gathered(first_names, run_hosted(gather_of(first_names), name="gather_first"))

    rel_idx = _rel_index()
    saved = []
    for i in range(DEPTH):
        sh1, sc1, g1, sh2, sc2, g2 = [mods[i, k:k + 1] for k in range(6)]
        rec = {"x0": x}
        j = i // 2
        nxt_names, nxt_plan = gather_layer(i + 1) if i + 1 < DEPTH else (None, None)
        if i % 2 == 0:
            w2p = jnp.pad(sm["gla_w_gk2"][j], ((0, 128 - GLA_RANK), (0, 0)))
            bgk, gn = _vec(sm["gla_b_gk"][j]), _vec(sm["gla_g_norm"][j])
            if i == 0:
                proj, got0 = mm_nn(x, wts["gla_w_in", j], 0, pro="mod", p1=sc1, p2=sh1, tm=1024, tn=640,
                                   comm=gather_of(rest_names), name=f"gla_proj_{i}")
                keep_gathered(rest_names, got0)
            else:
                proj = mm_nn(x, wts["gla_w_in", j], 0, pro="mod", p1=sc1, p2=sh1, tm=1024, tn=640,
                             name=f"gla_proj_{i}")
            (og, states), got = gla_fwd(proj, w2p, bgk, gn, comm=nxt_plan, name=f"gla_core_{i}")
            y = mm_nn(og, wts["gla_w_out", j], 0, tm=1024, tn=1024, name=f"gla_out_{i}")
            rec.update(kind="gla", j=j, w2p=w2p, bgk=bgk, gn=gn, proj=proj, og=og, states=states)
        else:
            rel = sm["att_rel_bias"][j]
            rel_pad = jnp.pad(rel, ((0, 0), (0, REL_PAD - N_REL)), constant_values=NEG_INF)
            tiles = rel_bias_tiles(rel_pad, rel_idx, name=f"att_bias_{i}")
            tiles = tiles.reshape(ATT_H, len(REL_TILES), REL_TILE, REL_TILE)
            clip = jnp.broadcast_to(rel[:, 2 * MAX_REL][:, None, None], (ATT_H, 1, 128))
            qkv = mm_nn(x, wts["att_w_in", j], 0, pro="mod", p1=sc1, p2=sh1, bias=_vec(sm["att_b_in"][j]),
                        out_dtype=BF16, tm=1024, tn=1024, name=f"att_proj_{i}")
            (o,), got = attn_fwd(qkv, tiles, clip, comm=nxt_plan, name=f"att_core_{i}")
            y = mm_nn(o, wts["att_w_out", j], 0, tm=1024, tn=1024, name=f"att_out_{i}")
            rec.update(kind="att", j=j, tiles=tiles, clip=clip, qkv=qkv, o=o)
        if nxt_plan is not None:
            keep_gathered(nxt_names, got)
        x1 = ln_fwd(x, y, g1, _vec(sm["ln_g"][i, 0]), _vec(sm["ln_b"][i, 0]), name=f"ln_mix_{i}")
        h = mm_nn(x1, wts["ff_w1", i], 0, pro="mod", p1=sc2, p2=sh2, out_dtype=BF16, tm=2048, tn=1024,
                  name=f"ff_up_{i}")
        y2 = mm_nn_ksplit(h, wts["ff_w2", i], 0, pro="relu2", tm=1024, tk=2048, name=f"ff_down_{i}")
        x2 = ln_fwd(x1, y2, g2, _vec(sm["ln_g"][i, 1]), _vec(sm["ln_b"][i, 1]), name=f"ln_ff_{i}")
        rec.update(y=y, x1=x1, h=h, y2=y2)
        saved.append(rec)
        x = x2

    dy, loss = loss_head(x, target, name="loss_head")

    gw = {}

    def wgrad(weight, layer, a, d, *, tn, tk=1024, tm=512, col_block0=0, **kw):
        gw[weight, layer] = mm_tn(a, d, tk=tk, tn=tn, tm=tm, out_buf=gw.get((weight, layer)),
                                  out_shape=wts[weight, layer].shape, col_block0=col_block0, **kw)

    def scatter_layer(units):
        grads = []
        for n, l in units:
            g = gw[n, l]
            if KIND[n] == "stk":
                g = g[:, :, :GLA_IN].reshape(1, D, N_DEV, GLA_IN // N_DEV).transpose(2, 0, 1, 3)
            grads.append(g)
        return scatter_plan(grads, [KIND[n] for n, _ in units], [(1,) + tuple(w[n].shape[1:]) for n, _ in units])

    results = {n: None for n in BIG}

    def update(units, parts):
        for (n, l), p in zip(units, parts):
            results[n] = adamw_layer(w[n], m[n], v[n], p, l, results[n], name=f"adamw_{n}_{l}")

    gs = {"ln_g": [[None, None] for _ in range(DEPTH)], "ln_b": [[None, None] for _ in range(DEPTH)],
          "gla_w_gk2": [None] * 2, "gla_b_gk": [None] * 2, "gla_g_norm": [None] * 2, "att_b_in": [None] * 2,
          "att_rel_bias": [None] * 2}
    dmods = [[None] * 6 for _ in range(DEPTH)]
    nxt = None
    nxt_slot = None
    for i in reversed(range(DEPTH)):
        rec = saved[i]
        sh1, sc1, g1, sh2, sc2, g2 = [mods[i, k:k + 1] for k in range(6)]
        x0, x1 = rec["x0"], rec["x1"]
        if nxt is None:
            dz2, acc = ln_bwd(x1, rec["y2"], g2, _vec(sm["ln_g"][i, 1]), dout=dy, name=f"ln_ff_bwd_{i}")
        else:
            dz2, acc = ln_bwd(x1, rec["y2"], g2, _vec(sm["ln_g"][i, 1]), nxt=nxt[:3] + (_vec(sm["ln_b"][i, 1]),),
                              name=f"ln_ff_bwd_{i}")
            dmods[nxt_slot[0]][nxt_slot[1]] = acc[3]
            dmods[nxt_slot[0]][nxt_slot[2]] = acc[4]
        gs["ln_g"][i][1], gs["ln_b"][i][1], dmods[i][5] = acc[0], acc[1], acc[2]
        dh = mm_nt([dz2], wts["ff_w2", i], 0, pro="scale", p1=g2, epi_h=rec["h"], out_dtype=BF16, tm=2048, tn=512,
                   name=f"ff_down_bwd_{i}")
        wgrad("ff_w2", i, rec["h"], dz2, pro="relu2", dscale=g2, tk=2048, tn=1024, name=f"ff_w2_grad_{i}")
        du2 = mm_nt([dh], wts["ff_w1", i], 0, tm=1024, tn=1024, name=f"ff_up_bwd_{i}")
        wgrad("ff_w1", i, x1, dh, pro="mod", p1=sc2, p2=sh2, tn=2048, name=f"ff_w1_grad_{i}")
        dz1, acc = ln_bwd(x0, rec["y"], g1, _vec(sm["ln_g"][i, 0]), nxt=(dz2, du2, sc2, _vec(sm["ln_b"][i, 0])),
                          name=f"ln_mix_bwd_{i}")
        dmods[i][4], dmods[i][3] = acc[3], acc[4]
        gs["ln_g"][i][0], gs["ln_b"][i][0], dmods[i][2] = acc[0], acc[1], acc[2]
        j = rec["j"]
        w_in, w_out = _layer_weights(i)[:2]
        if rec["kind"] == "gla":
            dog = mm_nt([dz1], wts[w_out], 0, pro="scale", p1=g1, tm=1024, tn=1024, name=f"gla_out_bwd_{i}")
            wgrad(*w_out, rec["og"], dz1, dscale=g1, tn=1024, name=f"gla_wout_grad_{i}")
        else:
            do = mm_nt([dz1], wts[w_out], 0, pro="scale", p1=g1, out_dtype=BF16, tm=1024, tn=1024,
                       name=f"att_out_bwd_{i}")
            wgrad(*w_out, rec["o"], dz1, dscale=g1, tn=1024, name=f"att_wout_grad_{i}")
        units = [("ff_w1", i), ("ff_w2", i), w_out] + ([_layer_weights(i + 1)[0]] if i + 1 < DEPTH else [])
        plan = scatter_layer(units)
        if rec["kind"] == "gla":
            (dproj, dw2p, dbgk, dgn), parts = gla_bwd(rec["proj"], dog, rec["states"], rec["w2p"], rec["bgk"],
                                                      rec["gn"], comm=plan, name=f"gla_core_bwd_{i}")
            gs["gla_w_gk2"][j], gs["gla_b_gk"][j], gs["gla_g_norm"][j] = dw2p[:GLA_RANK], dbgk[0], dgn[0]
            wgrad(*w_in, x0, dproj, pro="mod", p1=sc1, p2=sh1, tk=512, tn=GLA_INP, name=f"gla_win_grad_{i}")
            if i == 0:
                du1, last_parts = mm_nt([dproj], wts[w_in], 0, tm=1024, tn=1024, comm=scatter_layer([w_in]),
                                        name=f"gla_proj_bwd_{i}")
                update([w_in], last_parts)
            else:
                du1 = mm_nt([dproj], wts[w_in], 0, tm=1024, tn=1024, name=f"gla_proj_bwd_{i}")
        else:
            (dq, dk, dv, dtiles, dclip, sq, sk, sv), parts = attn_bwd(rec["qkv"], do, rec["tiles"], rec["clip"],
                                                                      comm=plan, name=f"att_core_bwd_{i}")
            drel = rel_bias_grad(dtiles.reshape(ATT_H, -1), dclip.reshape(ATT_H, 128), rel_idx,
                                 name=f"att_bias_grad_{i}")
            gs["att_rel_bias"][j] = drel[:, :N_REL]
            gs["att_b_in"][j] = jnp.concatenate([sq[0], sk[0], sv[0]])
            du1 = mm_nt([dq, dk, dv], wts[w_in], 0, tm=512, tn=1024, name=f"att_proj_bwd_{i}")
            for n, t in enumerate((dq, dk, dv)):
                wgrad(*w_in, x0, t, pro="mod", p1=sc1, p2=sh1, tn=1024, col_block0=n, name=f"att_win_grad_{i}_{n}")
        update(units, parts)
        nxt = (dz1, du1, sc1, x0)
        nxt_slot = (i, 1, 0)
    dx, acc = combine_final(nxt[0], nxt[1], nxt[2], nxt[3], name="grad_x")
    dmods[0][1], dmods[0][0] = acc[3], acc[4]
    dmods = jnp.stack([jnp.stack(r) for r in dmods])
    gs = {k: jnp.stack([jnp.stack(r) if isinstance(r, list) else r for r in v]) for k, v in gs.items()}
    return loss, dx, dmods, gs, results


WEIGHTS = ("w_ada", "b_ada", "ln_g", "ln_b", "gla_w_in", "gla_w_gk2", "gla_b_gk", "gla_g_norm", "gla_w_out",
           "att_w_in", "att_b_in", "att_rel_bias", "att_w_out", "ff_w1", "ff_w2")
SMALL_SHARDED = {"ln_g": 2, "ln_b": 2, "gla_w_gk2": 2, "gla_g_norm": 2, "att_b_in": 1}
SMALL_GRADS = ("ln_g", "ln_b", "gla_w_gk2", "gla_b_gk", "gla_g_norm", "att_b_in", "att_rel_bias")


def kernel(x, c, w_ada, b_ada, ln_g, ln_b, gla_w_in, gla_w_gk2, gla_b_gk, gla_g_norm, gla_w_out, att_w_in, att_b_in, att_rel_bias, att_w_out, ff_w1, ff_w2, loss_target, m_w_ada, m_b_ada, m_ln_g, m_ln_b, m_gla_w_in, m_gla_w_gk2, m_gla_b_gk, m_gla_g_norm, m_gla_w_out, m_att_w_in, m_att_b_in, m_att_rel_bias, m_att_w_out, m_ff_w1, m_ff_w2, v_w_ada, v_b_ada, v_ln_g, v_ln_b, v_gla_w_in, v_gla_w_gk2, v_gla_b_gk, v_gla_g_norm, v_gla_w_out, v_att_w_in, v_att_b_in, v_att_rel_bias, v_att_w_out, v_ff_w1, v_ff_w2):
    w = dict(w_ada=w_ada, b_ada=b_ada, ln_g=ln_g, ln_b=ln_b, gla_w_in=gla_w_in, gla_w_gk2=gla_w_gk2,
             gla_b_gk=gla_b_gk, gla_g_norm=gla_g_norm, gla_w_out=gla_w_out, att_w_in=att_w_in, att_b_in=att_b_in,
             att_rel_bias=att_rel_bias, att_w_out=att_w_out, ff_w1=ff_w1, ff_w2=ff_w2)
    m = dict(w_ada=m_w_ada, b_ada=m_b_ada, ln_g=m_ln_g, ln_b=m_ln_b, gla_w_in=m_gla_w_in, gla_w_gk2=m_gla_w_gk2,
             gla_b_gk=m_gla_b_gk, gla_g_norm=m_gla_g_norm, gla_w_out=m_gla_w_out, att_w_in=m_att_w_in,
             att_b_in=m_att_b_in, att_rel_bias=m_att_rel_bias, att_w_out=m_att_w_out, ff_w1=m_ff_w1, ff_w2=m_ff_w2)
    v = dict(w_ada=v_w_ada, b_ada=v_b_ada, ln_g=v_ln_g, ln_b=v_ln_b, gla_w_in=v_gla_w_in, gla_w_gk2=v_gla_w_gk2,
             gla_b_gk=v_gla_b_gk, gla_g_norm=v_gla_g_norm, gla_w_out=v_gla_w_out, att_w_in=v_att_w_in,
             att_b_in=v_att_b_in, att_rel_bias=v_att_rel_bias, att_w_out=v_att_w_out, ff_w1=v_ff_w1, ff_w2=v_ff_w2)
    xi, yi, ci = _my_place()
    me = 4 * xi + 2 * yi + ci

    small_names = tuple(SMALL_SHARDED)
    small_in = _pack_small([c] + [w[n] for n in small_names])
    small_all = all_gather(small_in, name="gather_small").reshape(N_DEV, -1, 128)
    parts = _unpack_small(small_all, [c.shape] + [w[n].shape for n in small_names])
    c_all = parts[0].reshape(N_DEV, D)
    sm = {"gla_b_gk": gla_b_gk, "att_rel_bias": att_rel_bias}
    for n, p in zip(small_names, parts[1:]):
        ax = SMALL_SHARDED[n]
        sm[n] = jnp.moveaxis(p, 0, ax).reshape(p.shape[1:ax + 1] + (N_DEV * p.shape[ax + 1],) + p.shape[ax + 2:])

    n_ada = w_ada.shape[2]
    mp = mods_partial(c_all, w_ada, name="mods_partial")
    mp_all = all_gather(mp.reshape(DEPTH * N_DEV, n_ada), name="gather_mods")
    mp_all = mp_all.reshape(N_DEV, DEPTH, N_DEV, n_ada)
    mods = lax.dynamic_index_in_dim(mp_all, me, axis=2, keepdims=False)
    mods = mods.transpose(1, 0, 2).reshape(DEPTH, 6 * D) + b_ada
    mods = mods.reshape(DEPTH, 6, D)

    loss, dx, dmods, gs, results = _trunk(x.reshape(x.shape[1:]), loss_target.reshape(x.shape[1:]), mods, sm, w, m, v)
    loss = lax.psum(loss[0, 0], ("x", "y", "c"))

    dm_flat = dmods.reshape(DEPTH, 6 * D)
    small_g = [dm_flat] + [gs[n].reshape(sm[n].shape) for n in SMALL_GRADS]
    small_shapes = [a.shape for a in small_g]
    sg_all = all_gather(_pack_small(small_g), name="gather_small_grads").reshape(N_DEV, -1, 128)
    summed = _unpack_small(sum_parts(sg_all, name="sum_small_grads"), small_shapes)
    g_full = dict(zip(("b_ada",) + SMALL_GRADS, summed))
    dm_all = _unpack_small(sg_all, small_shapes)[0]
    dm_mine = lax.dynamic_slice_in_dim(dm_all, me * n_ada, n_ada, axis=2).transpose(1, 0, 2)
    g_w_ada = w_ada_grad(c_all, dm_mine, name="w_ada_grad")

    results["w_ada"] = adamw_nd(w_ada, m_w_ada, v_w_ada, g_w_ada[None], name="adamw_w_ada")
    for n in ("b_ada",) + SMALL_GRADS:
        g = g_full[n]
        if n in SMALL_SHARDED:
            ax = SMALL_SHARDED[n]
            width = w[n].shape[ax]
            g = lax.dynamic_slice_in_dim(g, me * width, width, axis=ax)
        results[n] = adamw_nd(w[n], m[n], v[n], g[None], name=f"adamw_{n}")

    out = [loss, dx[None]]
    for k in range(4):
        out += [results[n][k] for n in WEIGHTS]
    return tuple(out)
```

```python
import numpy as np
import jax
import jax.numpy as jnp
from jax import lax
from jax.experimental import pallas as pl
from jax.experimental.pallas import tpu as pltpu

F32 = jnp.float32
BF16 = jnp.bfloat16
HIGHEST = lax.Precision.HIGHEST
MESH = pl.DeviceIdType.MESH

N_DEV = 8
D = 1024
DEPTH = 4
CHUNK = 64
ALPHA = (2.0 * DEPTH) ** 0.25
LN_EPS = 1e-5
RMS_EPS = 1e-6
NEG_INF = -1e30

GLA_H = 4
GLA_DKH = 128
GLA_DVH = 256
GLA_DK = GLA_H * GLA_DKH
GLA_DV = GLA_H * GLA_DVH
GLA_RANK = 16
GLA_IN = 2 * GLA_DK + 2 * GLA_DV + GLA_RANK
GLA_INP = 3200
GLA_TAU_INV = 1.0 / 16.0
GLA_SUB = 2

ATT_H = 16
ATT_HD = 64
ATT_QB = 256
ATT_KB = 3 * ATT_QB
LEFT = 8 * CHUNK
MAX_REL = 128
N_REL = 2 * MAX_REL + 1
REL_PAD = 384
REL_TILE = 128
REL_TILES = (3, 4)
D_FF = 4 * D

ADAM_LR = 0.001
ADAM_B1 = 0.9
ADAM_B2 = 0.999
ADAM_EPS = 1e-08
ADAM_WD = 0.01
ADAM_STEP = 10

VMEM_LIMIT = 48 * 1024 * 1024


def _params(n_axes):
    return pltpu.CompilerParams(dimension_semantics=("arbitrary",) * n_axes, vmem_limit_bytes=VMEM_LIMIT)


def _dot(a, b):
    return jnp.dot(a, b, preferred_element_type=F32)


def _dot_nt(a, b):
    return lax.dot_general(a, b, (((1,), (1,)), ((), ())), preferred_element_type=F32)


def _dot_tn(a, b):
    return lax.dot_general(a, b, (((0,), (0,)), ((), ())), preferred_element_type=F32)


def _bf(a):
    return a.astype(BF16)


def _prologue(kind, a, p1=None, p2=None):
    if kind == "mod":
        return a.astype(F32) * (1.0 + p1) + p2
    if kind == "scale":
        return a.astype(F32) * (1.0 + p1)
    if kind == "relu2":
        r = jnp.maximum(a, 0.0)
        return r * r
    return a


class Hosted:
    def __init__(self, inputs, out_shapes, sems, first, last):
        self.inputs, self.out_shapes, self.sems, self.first, self.last = inputs, out_shapes, sems, first, last


def _hbm_specs(n):
    return [pl.BlockSpec(memory_space=pltpu.HBM)] * n


def _call_hosting(body, comm, *, first, last, in_specs, out_specs, out_shape, scratch_shapes, args, **kw):
    if comm is None:
        return pl.pallas_call(body, in_specs=in_specs, out_specs=out_specs, out_shape=out_shape,
                              scratch_shapes=scratch_shapes, **kw)(*args), []
    n_in, n_out, n_scr = len(in_specs), len(out_specs), len(scratch_shapes)
    n_ci, n_co = len(comm.inputs), len(comm.out_shapes)

    def hosting(*refs):
        ins, ci = refs[:n_in], refs[n_in:n_in + n_ci]
        k = n_in + n_ci
        outs, co = refs[k:k + n_out], refs[k + n_out:k + n_out + n_co]
        k += n_out + n_co
        scr, cs = refs[k:k + n_scr], refs[k + n_scr:]

        @pl.when(first())
        def _():
            comm.first(ci, co, cs)

        body(*ins, *outs, *scr)

        @pl.when(last())
        def _():
            comm.last(ci, co, cs)

    res = pl.pallas_call(
        hosting, in_specs=list(in_specs) + _hbm_specs(n_ci), out_specs=list(out_specs) + _hbm_specs(n_co),
        out_shape=list(out_shape) + list(comm.out_shapes), scratch_shapes=list(scratch_shapes) + list(comm.sems),
        **kw)(*args, *comm.inputs)
    return res[:n_out], res[n_out:]


def run_hosted(comm, *, name):
    n_i, n_o = len(comm.inputs), len(comm.out_shapes)

    def body(*refs):
        ins, outs, sems = refs[:n_i], refs[n_i:n_i + n_o], refs[n_i + n_o:]
        comm.first(ins, outs, sems)
        comm.last(ins, outs, sems)

    return pl.pallas_call(body, name=name, out_shape=list(comm.out_shapes), in_specs=_hbm_specs(n_i),
                          out_specs=_hbm_specs(n_o), scratch_shapes=list(comm.sems))(*comm.inputs)


def mm_nn(a, b, layer, *, pro=None, p1=None, p2=None, bias=None, out_dtype=F32, tm, tn, comm=None, name):
    M, K = a.shape
    N = b.shape[2]
    tm = min(tm, M)
    n_p = {"mod": 2, "scale": 1}.get(pro, 0)
    has_bias = bias is not None
    direct = pro is None and a.dtype == BF16

    def body(*refs):
        a_ref, b_ref = refs[0], refs[1]
        p_refs = refs[2:2 + n_p]
        bias_ref = refs[2 + n_p] if has_bias else None
        if direct:
            o_ref = refs[-1]
            lhs = a_ref[...]
        else:
            o_ref, abf = refs[-2], refs[-1]

            @pl.when(pl.program_id(1) == 0)
            def _():
                abf[...] = _bf(_prologue(pro, a_ref[...].astype(F32), *[r[...] for r in p_refs]))

            lhs = abf[...]
        acc = _dot(lhs, b_ref[...])
        if has_bias:
            acc = acc + bias_ref[...]
        o_ref[...] = acc.astype(out_dtype)

    in_specs = [pl.BlockSpec((tm, K), lambda i, j: (i, 0)), pl.BlockSpec((None, K, tn), lambda i, j: (layer, 0, j))]
    args = [a, b]
    for p in (p1, p2)[:n_p]:
        in_specs.append(pl.BlockSpec((1, K), lambda i, j: (0, 0)))
        args.append(p)
    if has_bias:
        in_specs.append(pl.BlockSpec((1, tn), lambda i, j: (0, j)))
        args.append(bias)
    n_i, n_j = M // tm, N // tn
    (out,), got = _call_hosting(
        body, comm, first=lambda: (pl.program_id(0) == 0) & (pl.program_id(1) == 0),
        last=lambda: (pl.program_id(0) == n_i - 1) & (pl.program_id(1) == n_j - 1),
        name=name, grid=(n_i, n_j), in_specs=in_specs,
        out_specs=[pl.BlockSpec((tm, tn), lambda i, j: (i, j))],
        out_shape=[jax.ShapeDtypeStruct((M, N), out_dtype)],
        scratch_shapes=[] if direct else [pltpu.VMEM((tm, K), BF16)], compiler_params=_params(2), args=args)
    return out if comm is None else (out, got)


def mm_nn_ksplit(a, b, layer, *, pro=None, tm, tk, name):
    M, K = a.shape
    N = b.shape[2]
    tm = min(tm, M)

    def body(a_ref, b_ref, o_ref):
        @pl.when(pl.program_id(1) == 0)
        def _():
            o_ref[...] = jnp.zeros_like(o_ref)

        o_ref[...] += _dot(_bf(_prologue(pro, a_ref[...])), b_ref[...])

    return pl.pallas_call(
        body, name=name, grid=(M // tm, K // tk),
        in_specs=[pl.BlockSpec((tm, tk), lambda i, k: (i, k)), pl.BlockSpec((None, tk, N), lambda i, k: (layer, k, 0))],
        out_specs=pl.BlockSpec((tm, N), lambda i, k: (i, 0)),
        out_shape=jax.ShapeDtypeStruct((M, N), F32), compiler_params=_params(2),
    )(a, b)


def mm_nt(a_parts, w, layer, *, pro=None, p1=None, epi_h=None, out_dtype=F32, tm, tn, comm=None, name):
    M = a_parts[0].shape[0]
    tm = min(tm, M)
    widths = [p.shape[1] for p in a_parts]
    Nw = sum(widths)
    Kw = w.shape[1]
    n_a = len(a_parts)
    has_p = pro == "scale"
    has_h = epi_h is not None
    direct = n_a == 1 and not has_p and a_parts[0].dtype == BF16

    def body(*refs):
        a_refs = refs[:n_a]
        w_ref = refs[n_a]
        k = n_a + 1
        p_ref = refs[k] if has_p else None
        k += int(has_p)
        h_ref = refs[k] if has_h else None
        if direct:
            o_ref = refs[-1]
            lhs = a_refs[0][...]
        else:
            o_ref, abf = refs[-2], refs[-1]

            @pl.when(pl.program_id(1) == 0)
            def _():
                off = 0
                for r, wd in zip(a_refs, widths):
                    av = r[...]
                    if has_p:
                        av = av.astype(F32) * (1.0 + p_ref[...])
                    abf[:, off:off + wd] = _bf(av)
                    off += wd

            lhs = abf[...]
        acc = _dot_nt(lhs, w_ref[...])
        if has_h:
            acc = acc * (2.0 * jnp.maximum(h_ref[...], 0.0))
        o_ref[...] = acc.astype(out_dtype)

    in_specs = [pl.BlockSpec((tm, wd), lambda i, j: (i, 0)) for wd in widths]
    in_specs.append(pl.BlockSpec((None, tn, Nw), lambda i, j: (layer, j, 0)))
    args = list(a_parts) + [w]
    if has_p:
        in_specs.append(pl.BlockSpec((1, Nw), lambda i, j: (0, 0)))
        args.append(p1)
    if has_h:
        in_specs.append(pl.BlockSpec((tm, tn), lambda i, j: (i, j)))
        args.append(epi_h)
    n_i, n_j = M // tm, Kw // tn
    (out,), got = _call_hosting(
        body, comm, first=lambda: (pl.program_id(0) == 0) & (pl.program_id(1) == 0),
        last=lambda: (pl.program_id(0) == n_i - 1) & (pl.program_id(1) == n_j - 1),
        name=name, grid=(n_i, n_j), in_specs=in_specs,
        out_specs=[pl.BlockSpec((tm, tn), lambda i, j: (i, j))],
        out_shape=[jax.ShapeDtypeStruct((M, Kw), out_dtype)],
        scratch_shapes=[] if direct else [pltpu.VMEM((tm, Nw), BF16)], compiler_params=_params(2), args=args)
    return out if comm is None else (out, got)


def mm_tn(a, d, *, pro=None, p1=None, p2=None, dscale=None, tk, tn, tm, out_buf, out_shape, col_block0=0, name):
    M, Kf = a.shape
    N = d.shape[1]
    n_p = {"mod": 2}.get(pro, 0)
    has_ds = dscale is not None
    has_buf = out_buf is not None
    n_m = M // tm

    def body(*refs):
        a_ref, d_ref = refs[0], refs[1]
        p_refs = refs[2:2 + n_p]
        ds_ref = refs[2 + n_p] if has_ds else None
        o_ref, acc = refs[-2], refs[-1]
        m = pl.program_id(2)

        @pl.when(m == 0)
        def _():
            acc[...] = jnp.zeros_like(acc)

        av = _prologue(pro, a_ref[...], *[r[...] for r in p_refs])
        dv = d_ref[...]
        if has_ds:
            dv = dv.astype(F32) * (1.0 + ds_ref[...])
        acc[...] += _dot_tn(_bf(av), _bf(dv))

        @pl.when(m == n_m - 1)
        def _():
            o_ref[...] = _bf(acc[...])

    in_specs = [pl.BlockSpec((tm, tk), lambda i, j, m: (m, i)), pl.BlockSpec((tm, tn), lambda i, j, m: (m, j))]
    args = [a, d]
    for p in (p1, p2)[:n_p]:
        in_specs.append(pl.BlockSpec((1, tk), lambda i, j, m: (0, i)))
        args.append(p)
    if has_ds:
        in_specs.append(pl.BlockSpec((1, tn), lambda i, j, m: (0, j)))
        args.append(dscale)
    aliases = {}
    if has_buf:
        in_specs.append(pl.BlockSpec(memory_space=pl.ANY))
        args.append(out_buf)
        aliases = {len(args) - 1: 0}
    return pl.pallas_call(
        body, name=name, grid=(Kf // tk, N // tn, n_m), in_specs=in_specs,
        out_specs=pl.BlockSpec((None, tk, tn), lambda i, j, m: (0, i, col_block0 + j)),
        out_shape=jax.ShapeDtypeStruct(out_shape, BF16), input_output_aliases=aliases,
        scratch_shapes=[pltpu.VMEM((tk, tn), F32)], compiler_params=_params(3),
    )(*args)


ROW_BLOCK = 512
ACC_ROWS = 8


def _ln_stats(z):
    mu = jnp.mean(z, axis=-1, keepdims=True)
    zc = z - mu
    var = jnp.mean(zc * zc, axis=-1, keepdims=True)
    return zc, lax.rsqrt(var + LN_EPS)


def ln_fwd(x, y, gate, lng, lnb, *, name):
    S = x.shape[0]

    def body(x_ref, y_ref, gt_ref, g_ref, b_ref, o_ref):
        z = ALPHA * x_ref[...] + (1.0 + gt_ref[...]) * y_ref[...]
        zc, rstd = _ln_stats(z)
        o_ref[...] = (zc * rstd) * g_ref[...] + b_ref[...]

    row = pl.BlockSpec((ROW_BLOCK, D), lambda i: (i, 0))
    vec = pl.BlockSpec((1, D), lambda i: (0, 0))
    return pl.pallas_call(
        body, name=name, grid=(S // ROW_BLOCK,), in_specs=[row, row, vec, vec, vec], out_specs=row,
        out_shape=jax.ShapeDtypeStruct((S, D), F32), compiler_params=_params(1),
    )(x, y, gate, lng, lnb)


def _add_colsum(acc_ref, r, val):
    acc_ref[r:r + 1, :] += jnp.sum(val, axis=0, keepdims=True)


def ln_bwd(x_in, y, gate, lng, *, dout=None, nxt=None, name):
    S = x_in.shape[0]
    has_next = nxt is not None

    def body(*refs):
        if has_next:
            dzn_ref, dun_ref, scn_ref, b_ref = refs[:4]
            k = 4
        else:
            do_ref = refs[0]
            k = 1
        x_ref, y_ref, gt_ref, g_ref = refs[k:k + 4]
        dz_ref, acc_ref = refs[k + 4:]

        @pl.when(pl.program_id(0) == 0)
        def _():
            acc_ref[...] = jnp.zeros_like(acc_ref)

        yv = y_ref[...]
        z = ALPHA * x_ref[...] + (1.0 + gt_ref[...]) * yv
        zc, rstd = _ln_stats(z)
        xhat = zc * rstd
        if has_next:
            du = dun_ref[...]
            dout_v = ALPHA * dzn_ref[...] + du * (1.0 + scn_ref[...])
            _add_colsum(acc_ref, 3, du * (xhat * g_ref[...] + b_ref[...]))
            _add_colsum(acc_ref, 4, du)
        else:
            dout_v = do_ref[...]
        _add_colsum(acc_ref, 0, dout_v * xhat)
        _add_colsum(acc_ref, 1, dout_v)
        dxh = dout_v * g_ref[...]
        m1 = jnp.mean(dxh, axis=-1, keepdims=True)
        m2 = jnp.mean(dxh * xhat, axis=-1, keepdims=True)
        dz = rstd * (dxh - m1 - xhat * m2)
        _add_colsum(acc_ref, 2, dz * yv)
        dz_ref[...] = dz

    row = pl.BlockSpec((ROW_BLOCK, D), lambda i: (i, 0))
    vec = pl.BlockSpec((1, D), lambda i: (0, 0))
    if has_next:
        in_specs = [row, row, vec, vec]
        args = list(nxt)
    else:
        in_specs = [row]
        args = [dout]
    in_specs += [row, row, vec, vec]
    args += [x_in, y, gate, lng]
    return pl.pallas_call(
        body, name=name, grid=(S // ROW_BLOCK,), in_specs=in_specs,
        out_specs=[row, pl.BlockSpec((ACC_ROWS, D), lambda i: (0, 0))],
        out_shape=[jax.ShapeDtypeStruct((S, D), F32), jax.ShapeDtypeStruct((ACC_ROWS, D), F32)],
        compiler_params=_params(1),
    )(*args)


def combine_final(dz, du, sc, x_in, *, name):
    S = dz.shape[0]

    def body(dz_ref, du_ref, sc_ref, x_ref, dx_ref, acc_ref):
        @pl.when(pl.program_id(0) == 0)
        def _():
            acc_ref[...] = jnp.zeros_like(acc_ref)

        du_v = du_ref[...]
        dx_ref[...] = ALPHA * dz_ref[...] + du_v * (1.0 + sc_ref[...])
        _add_colsum(acc_ref, 3, du_v * x_ref[...])
        _add_colsum(acc_ref, 4, du_v)

    row = pl.BlockSpec((ROW_BLOCK, D), lambda i: (i, 0))
    vec = pl.BlockSpec((1, D), lambda i: (0, 0))
    return pl.pallas_call(
        body, name=name, grid=(S // ROW_BLOCK,), in_specs=[row, row, vec, row],
        out_specs=[row, pl.BlockSpec((ACC_ROWS, D), lambda i: (0, 0))],
        out_shape=[jax.ShapeDtypeStruct((S, D), F32), jax.ShapeDtypeStruct((ACC_ROWS, D), F32)],
        compiler_params=_params(1),
    )(dz, du, sc, x_in)


def loss_head(y, t, *, name):
    S = y.shape[0]

    def body(y_ref, t_ref, dy_ref, l_ref):
        @pl.when(pl.program_id(0) == 0)
        def _():
            l_ref[...] = jnp.zeros_like(l_ref)

        e = y_ref[...] - t_ref[...]
        dy_ref[...] = e * (1.0 / D)
        per_tok = jnp.sum(e * e, axis=1, keepdims=True) * (1.0 / D)
        l_ref[...] += 0.5 * jnp.sum(per_tok, axis=0, keepdims=True)

    row = pl.BlockSpec((ROW_BLOCK, D), lambda i: (i, 0))
    return pl.pallas_call(
        body, name=name, grid=(S // ROW_BLOCK,), in_specs=[row, row],
        out_specs=[row, pl.BlockSpec((8, 128), lambda i: (0, 0))],
        out_shape=[jax.ShapeDtypeStruct((S, D), F32), jax.ShapeDtypeStruct((8, 128), F32)],
        compiler_params=_params(1),
    )(y, t)


def _log_sigmoid(x):
    return jnp.minimum(x, 0.0) - jnp.log(1.0 + jnp.exp(-jnp.abs(x)))


def _silu(x):
    return x * (1.0 / (1.0 + jnp.exp(-x)))


def _cumsum_steps(x):
    row = lax.broadcasted_iota(jnp.int32, x.shape, 0)
    step = 1
    while step < x.shape[0]:
        x = x + jnp.where(row >= step, pltpu.roll(x, step, 0), 0.0)
        step *= 2
    return x


@jax.custom_vjp
def _cumsum_rows(x):
    return _cumsum_steps(x)


def _cumsum_rows_fwd(x):
    return _cumsum_steps(x), None


def _cumsum_rows_bwd(_, g):
    return (jnp.sum(g, axis=0, keepdims=True) - _cumsum_steps(g) + g,)


_cumsum_rows.defvjp(_cumsum_rows_fwd, _cumsum_rows_bwd)


def _gla_chunk(q, k, v, g, gk, s0t, w2p, bgk, gn):
    C = q.shape[0]
    row = lax.broadcasted_iota(jnp.int32, (C, C), 0)
    col = lax.broadcasted_iota(jnp.int32, (C, C), 1)
    lower = row >= col
    la = _log_sigmoid(_dot(_bf(gk), _bf(w2p)) + bgk) * GLA_TAU_INV
    outs, states = [], []
    for h in range(GLA_H):
        ks = slice(h * GLA_DKH, (h + 1) * GLA_DKH)
        vs = slice(h * GLA_DVH, (h + 1) * GLA_DVH)
        qh = q[:, ks] * (GLA_DKH ** -0.5)
        kh, vh, gh, lah, s0 = k[:, ks], v[:, vs], g[:, vs], la[:, ks], s0t[h]
        cum = _cumsum_rows(lah)
        e_pos = jnp.exp(cum)
        e_neg = jnp.exp(-cum)
        q_f = qh * e_pos
        a_f = _dot_nt(_bf(q_f), _bf(kh * e_neg))
        a_b = _dot_nt(_bf(qh * e_neg), _bf(kh * e_pos))
        att = jnp.where(lower, a_f, a_b)
        o = _dot(_bf(att), _bf(vh)) + _dot_nt(_bf(q_f), _bf(s0))
        tot = jnp.sum(lah, axis=0, keepdims=True)
        k_end = kh * jnp.exp(tot - cum)
        states.append(s0 * jnp.exp(tot) + _dot_tn(_bf(vh), _bf(k_end)))
        on = o * lax.rsqrt(jnp.mean(o * o, axis=-1, keepdims=True) + RMS_EPS) * gn[:, vs]
        outs.append(on * _silu(gh))
    return jnp.concatenate(outs, axis=1), tuple(states)


def _gla_split(p):
    return (p[:, 0:GLA_DK], p[:, GLA_DK:2 * GLA_DK], p[:, 2 * GLA_DK:2 * GLA_DK + GLA_DV],
            p[:, 2 * GLA_DK + GLA_DV:2 * GLA_DK + 2 * GLA_DV], p[:, 2 * GLA_DK + 2 * GLA_DV:GLA_INP])


def gla_fwd(proj, w2p, bgk, gn, *, comm=None, name):
    S = proj.shape[0]
    n_c = S // CHUNK
    n_s = n_c // GLA_SUB
    rows = GLA_SUB * CHUNK

    def body(p_ref, w_ref, b_ref, gn_ref, o_ref, st_ref, st):
        @pl.when(pl.program_id(0) == 0)
        def _():
            st[...] = jnp.zeros_like(st)

        s = tuple(st[h] for h in range(GLA_H))
        for u in range(GLA_SUB):
            sub = slice(u * CHUNK, (u + 1) * CHUNK)
            for h in range(GLA_H):
                st_ref[u, h] = s[h]
            og, s = _gla_chunk(*_gla_split(p_ref[sub, :]), s, w_ref[...], b_ref[...], gn_ref[...])
            o_ref[sub, :] = _bf(og)
        for h in range(GLA_H):
            st[h] = s[h]

    full = lambda shape: pl.BlockSpec(shape, lambda i: (0,) * len(shape))
    return _call_hosting(
        body, comm, first=lambda: pl.program_id(0) == 0, last=lambda: pl.program_id(0) == n_s - 1,
        name=name, grid=(n_s,),
        in_specs=[pl.BlockSpec((rows, GLA_INP), lambda i: (i, 0)), full((128, GLA_DK)), full((1, GLA_DK)),
                  full((1, GLA_DV))],
        out_specs=[pl.BlockSpec((rows, GLA_DV), lambda i: (i, 0)),
                   pl.BlockSpec((GLA_SUB, GLA_H, GLA_DVH, GLA_DKH), lambda i: (i, 0, 0, 0))],
        out_shape=[jax.ShapeDtypeStruct((S, GLA_DV), BF16),
                   jax.ShapeDtypeStruct((n_c, GLA_H, GLA_DVH, GLA_DKH), F32)],
        scratch_shapes=[pltpu.VMEM((GLA_H, GLA_DVH, GLA_DKH), F32)], compiler_params=_params(1),
        args=(proj, w2p, bgk, gn))


def gla_bwd(proj, dog, states, w2p, bgk, gn, *, comm=None, name):
    S = proj.shape[0]
    n_c = S // CHUNK
    n_s = n_c // GLA_SUB
    rows = GLA_SUB * CHUNK

    def body(p_ref, dog_ref, st_ref, w_ref, b_ref, gn_ref, dp_ref, dw_ref, db_ref, dgn_ref, ds_ref):
        @pl.when(pl.program_id(0) == 0)
        def _():
            ds_ref[...] = jnp.zeros_like(ds_ref)
            dw_ref[...] = jnp.zeros_like(dw_ref)
            db_ref[...] = jnp.zeros_like(db_ref)
            dgn_ref[...] = jnp.zeros_like(dgn_ref)

        ds = tuple(ds_ref[h] for h in range(GLA_H))
        for u in reversed(range(GLA_SUB)):
            sub = slice(u * CHUNK, (u + 1) * CHUNK)
            q, k, v, g, gk = _gla_split(p_ref[sub, :])
            s0 = tuple(st_ref[u, h] for h in range(GLA_H))
            _, vjp = jax.vjp(_gla_chunk, q, k, v, g, gk, s0, w_ref[...], b_ref[...], gn_ref[...])
            dq, dk, dv, dg, dgk, ds, dw, db, dgn = vjp((dog_ref[sub, :], ds))
            dp_ref[sub, 0:GLA_DK] = _bf(dq)
            dp_ref[sub, GLA_DK:2 * GLA_DK] = _bf(dk)
            dp_ref[sub, 2 * GLA_DK:2 * GLA_DK + GLA_DV] = _bf(dv)
            dp_ref[sub, 2 * GLA_DK + GLA_DV:2 * GLA_DK + 2 * GLA_DV] = _bf(dg)
            dp_ref[sub, 2 * GLA_DK + 2 * GLA_DV:GLA_INP] = _bf(dgk)
            dw_ref[...] += dw
            db_ref[...] += db
            dgn_ref[...] += dgn
        for h in range(GLA_H):
            ds_ref[h] = ds[h]

    full = lambda shape: pl.BlockSpec(shape, lambda i: (0,) * len(shape))
    rev = lambda i: (n_s - 1 - i, 0)
    return _call_hosting(
        body, comm, first=lambda: pl.program_id(0) == 0, last=lambda: pl.program_id(0) == n_s - 1,
        name=name, grid=(n_s,),
        in_specs=[pl.BlockSpec((rows, GLA_INP), rev), pl.BlockSpec((rows, GLA_DV), rev),
                  pl.BlockSpec((GLA_SUB, GLA_H, GLA_DVH, GLA_DKH), lambda i: (n_s - 1 - i, 0, 0, 0)),
                  full((128, GLA_DK)), full((1, GLA_DK)), full((1, GLA_DV))],
        out_specs=[pl.BlockSpec((rows, GLA_INP), rev), full((128, GLA_DK)), full((1, GLA_DK)), full((1, GLA_DV))],
        out_shape=[jax.ShapeDtypeStruct((S, GLA_INP), BF16), jax.ShapeDtypeStruct((128, GLA_DK), F32),
                   jax.ShapeDtypeStruct((1, GLA_DK), F32), jax.ShapeDtypeStruct((1, GLA_DV), F32)],
        scratch_shapes=[pltpu.VMEM((GLA_H, GLA_DVH, GLA_DKH), F32)], compiler_params=_params(1),
        args=(proj, dog, states, w2p, bgk, gn))


def _rel_index():
    t = np.arange(REL_TILE)[:, None]
    j = np.arange(REL_TILE)[None, :]
    tiles = []
    for m in REL_TILES:
        chunks = (REL_TILE // CHUNK) * m + j // CHUNK - t // CHUNK
        band = (chunks >= 0) & (chunks <= LEFT // CHUNK)
        dist = LEFT - REL_TILE * m + t - j
        tiles.append(np.where(band, np.minimum(dist, MAX_REL) + MAX_REL, N_REL))
    return jnp.asarray(np.stack(tiles).reshape(1, -1).astype(np.int32))


REL_BLOCK = 2048


def _one_hot(idx_row):
    return (lax.broadcasted_iota(jnp.int32, (REL_PAD, idx_row.shape[1]), 0) == idx_row).astype(F32)


def rel_bias_tiles(rel_pad, idx, *, name):
    E = idx.shape[1]

    def body(r_ref, i_ref, o_ref):
        o_ref[...] = jnp.dot(r_ref[...], _one_hot(i_ref[...]), precision=HIGHEST, preferred_element_type=F32)

    return pl.pallas_call(
        body, name=name, grid=(E // REL_BLOCK,),
        in_specs=[pl.BlockSpec((ATT_H, REL_PAD), lambda i: (0, 0)), pl.BlockSpec((1, REL_BLOCK), lambda i: (0, i))],
        out_specs=pl.BlockSpec((ATT_H, REL_BLOCK), lambda i: (0, i)),
        out_shape=jax.ShapeDtypeStruct((ATT_H, E), F32), compiler_params=_params(1),
    )(rel_pad, idx)


def rel_bias_grad(dtiles_flat, dclip, idx, *, name):
    E = idx.shape[1]
    n_steps = E // REL_BLOCK

    def body(d_ref, c_ref, i_ref, o_ref):
        @pl.when(pl.program_id(0) == 0)
        def _():
            o_ref[...] = jnp.zeros_like(o_ref)

        o_ref[...] += lax.dot_general(d_ref[...], _one_hot(i_ref[...]), (((1,), (1,)), ((), ())),
                                      precision=HIGHEST, preferred_element_type=F32)

        @pl.when(pl.program_id(0) == n_steps - 1)
        def _():
            at_clip = lax.broadcasted_iota(jnp.int32, (1, REL_PAD), 1) == 2 * MAX_REL
            o_ref[...] += jnp.where(at_clip, jnp.sum(c_ref[...], axis=1, keepdims=True), 0.0)

    return pl.pallas_call(
        body, name=name, grid=(n_steps,),
        in_specs=[pl.BlockSpec((ATT_H, REL_BLOCK), lambda i: (0, i)), pl.BlockSpec((ATT_H, 128), lambda i: (0, 0)),
                  pl.BlockSpec((1, REL_BLOCK), lambda i: (0, i))],
        out_specs=pl.BlockSpec((ATT_H, REL_PAD), lambda i: (0, 0)),
        out_shape=jax.ShapeDtypeStruct((ATT_H, REL_PAD), F32), compiler_params=_params(1),
    )(dtiles_flat, dclip, idx)


def _attn_bias(tiles, clip):
    const = jnp.broadcast_to(clip, (REL_TILE, REL_TILE))
    zero = jnp.zeros((REL_TILE, REL_TILE), F32)
    rows = []
    for qt in range(ATT_QB // REL_TILE):
        blocks = []
        for kt in range(ATT_KB // REL_TILE):
            m = kt - qt
            if m in REL_TILES:
                blocks.append(tiles[REL_TILES.index(m)])
            elif 0 <= m < REL_TILES[0]:
                blocks.append(const)
            else:
                blocks.append(zero)
        rows.append(jnp.concatenate(blocks, axis=1))
    return jnp.concatenate(rows, axis=0)


def _attn_bias_grad(ds, dt_ref, dc_ref, a):
    tile = lambda qt, kt: ds[qt * REL_TILE:(qt + 1) * REL_TILE, kt * REL_TILE:(kt + 1) * REL_TILE]
    const = None
    sums = [None] * len(REL_TILES)
    for qt in range(ATT_QB // REL_TILE):
        for kt in range(ATT_KB // REL_TILE):
            m = kt - qt
            if m in REL_TILES:
                n = REL_TILES.index(m)
                sums[n] = tile(qt, kt) if sums[n] is None else sums[n] + tile(qt, kt)
            elif 0 <= m < REL_TILES[0]:
                const = tile(qt, kt) if const is None else const + tile(qt, kt)
    for n, v in enumerate(sums):
        dt_ref[a, n] += v
    dc_ref[a] += jnp.sum(const, axis=0, keepdims=True)


def _attn_head_lanes():
    lane = lax.broadcasted_iota(jnp.int32, (1, 2 * ATT_HD), 1)
    return [(lane >= a * ATT_HD) & (lane < (a + 1) * ATT_HD) for a in range(2)]


def _attn_band_bias(tiles, clip):
    j = lax.broadcasted_iota(jnp.int32, (ATT_QB, ATT_KB), 1)
    t = lax.broadcasted_iota(jnp.int32, (ATT_QB, ATT_KB), 0)
    shift = CHUNK.bit_length() - 1
    chunks = lax.shift_right_logical(j, shift) - lax.shift_right_logical(t, shift)
    band = (chunks >= 0) & (chunks <= LEFT // CHUNK)
    return jnp.where(band, _attn_bias(tiles, clip), NEG_INF)


def _attn_exp(qa, kb, bias, key_bias):
    s = _dot_nt(qa, kb) + bias + key_bias
    e = jnp.exp(s - jnp.max(s, axis=-1, keepdims=True))
    return e, jnp.sum(e, axis=-1, keepdims=True)


def _attn_specs():
    n_hp = ATT_H // 2
    q_spec = pl.BlockSpec((ATT_QB, 128), lambda hp, g: (g, hp))

    def win(col0, back):
        return pl.BlockSpec((ATT_QB, 128), lambda hp, g: (jnp.maximum(g - back, 0), col0 + hp))

    kv_specs = [win(n_hp, 2), win(n_hp, 1), win(n_hp, 0), win(2 * n_hp, 2), win(2 * n_hp, 1), win(2 * n_hp, 0)]
    tiles_spec = pl.BlockSpec((2, len(REL_TILES), REL_TILE, REL_TILE), lambda hp, g: (hp, 0, 0, 0))
    clip_spec = pl.BlockSpec((2, 1, 128), lambda hp, g: (hp, 0, 0))
    return q_spec, kv_specs, tiles_spec, clip_spec


def _attn_window(refs, g):
    kb = jnp.concatenate([_bf(r[...]) for r in refs[0:3]], axis=0)
    vb = jnp.concatenate([_bf(r[...]) for r in refs[3:6]], axis=0)
    j = lax.broadcasted_iota(jnp.int32, (1, ATT_KB), 1)
    return kb, vb, jnp.where(j + (g - 2) * ATT_QB >= 0, 0.0, NEG_INF)


def attn_fwd(qkv, tiles, clip, *, comm=None, name):
    S = qkv.shape[0]
    q_spec, kv_specs, tiles_spec, clip_spec = _attn_specs()

    def body(q_ref, *rest):
        kv_refs, t_ref, c_ref, o_ref, bias = rest[:6], rest[6], rest[7], rest[8], rest[9]
        g = pl.program_id(1)

        @pl.when(g == 0)
        def _():
            for a in range(2):
                bias[a * ATT_QB:(a + 1) * ATT_QB, :] = _attn_band_bias(t_ref[a], c_ref[a])

        kb, vb, key_bias = _attn_window(kv_refs, g)
        q = q_ref[...].astype(F32)
        out = jnp.zeros((ATT_QB, 2 * ATT_HD), F32)
        for a, lanes in enumerate(_attn_head_lanes()):
            mf = lanes.astype(F32)
            e, l = _attn_exp(_bf(q * (mf * ATT_HD ** -0.5)), kb, bias[a * ATT_QB:(a + 1) * ATT_QB, :], key_bias)
            out = out + _dot(_bf(e), vb) * (mf * (1.0 / l))
        o_ref[...] = _bf(out)

    n_hp, n_g = ATT_H // 2, S // ATT_QB
    return _call_hosting(
        body, comm, first=lambda: (pl.program_id(0) == 0) & (pl.program_id(1) == 0),
        last=lambda: (pl.program_id(0) == n_hp - 1) & (pl.program_id(1) == n_g - 1),
        name=name, grid=(n_hp, n_g), in_specs=[q_spec] + kv_specs + [tiles_spec, clip_spec],
        out_specs=[q_spec], out_shape=[jax.ShapeDtypeStruct((S, D), BF16)],
        scratch_shapes=[pltpu.VMEM((2 * ATT_QB, ATT_KB), F32)], compiler_params=_params(2),
        args=(*([qkv] * 7), tiles, clip))


def attn_bwd(qkv, do, tiles, clip, *, comm=None, name):
    S = qkv.shape[0]
    q_spec, kv_specs, tiles_spec, clip_spec = _attn_specs()
    col_spec = pl.BlockSpec((S, 128), lambda hp, g: (0, hp))
    sum_spec = pl.BlockSpec((1, 128), lambda hp, g: (0, hp))
    n_g = S // ATT_QB

    def body(q_ref, *rest):
        kv_refs, t_ref, c_ref, do_ref = rest[:6], rest[6], rest[7], rest[8]
        dq_ref, dk_ref, dv_ref, dt_ref, dc_ref, sq_ref, sk_ref, sv_ref, bias = rest[9:]
        g = pl.program_id(1)

        @pl.when(g == 0)
        def _():
            for a in range(2):
                bias[a * ATT_QB:(a + 1) * ATT_QB, :] = _attn_band_bias(t_ref[a], c_ref[a])
            dk_ref[...] = jnp.zeros_like(dk_ref)
            dv_ref[...] = jnp.zeros_like(dv_ref)
            dt_ref[...] = jnp.zeros_like(dt_ref)
            dc_ref[...] = jnp.zeros_like(dc_ref)
            sq_ref[...] = jnp.zeros_like(sq_ref)

        kb, vb, key_bias = _attn_window(kv_refs, g)
        q = q_ref[...].astype(F32)
        do = do_ref[...]
        lanes = _attn_head_lanes()
        mf = [m.astype(F32) * ATT_HD ** -0.5 for m in lanes]
        qs = _bf(jnp.concatenate([q * m for m in mf], axis=0))
        dos = jnp.concatenate([jnp.where(m, do, jnp.zeros_like(do)) for m in lanes], axis=0)
        e, l = _attn_exp(qs, kb, bias[...], key_bias)
        p = e * (1.0 / l)
        dp = _dot_nt(dos, vb)
        ds = p * (dp - jnp.sum(p * dp, axis=-1, keepdims=True))
        ds_b = _bf(ds)
        dq2 = _dot(ds_b, kb)
        dq = dq2[:ATT_QB] * mf[0] + dq2[ATT_QB:] * mf[1]
        dkw = _dot_tn(ds_b, qs)
        dvw = _dot_tn(_bf(p), dos)
        for a in range(2):
            _attn_bias_grad(ds[a * ATT_QB:(a + 1) * ATT_QB], dt_ref, dc_ref, a)
        dq_ref[...] = _bf(dq)
        sq_ref[...] += jnp.sum(dq, axis=0, keepdims=True)
        for blk in range(3):
            src = g - 2 + blk

            @pl.when(src >= 0)
            def _(blk=blk, src=src):
                rows = pl.ds(pl.multiple_of(src * ATT_QB, ATT_QB), ATT_QB)
                dk_ref[rows, :] += dkw[blk * ATT_QB:(blk + 1) * ATT_QB]
                dv_ref[rows, :] += dvw[blk * ATT_QB:(blk + 1) * ATT_QB]

        @pl.when(g == n_g - 1)
        def _():
            sk_ref[...] = jnp.sum(dk_ref[...], axis=0, keepdims=True)
            sv_ref[...] = jnp.sum(dv_ref[...], axis=0, keepdims=True)

    n_hp, n_g = ATT_H // 2, S // ATT_QB
    return _call_hosting(
        body, comm, first=lambda: (pl.program_id(0) == 0) & (pl.program_id(1) == 0),
        last=lambda: (pl.program_id(0) == n_hp - 1) & (pl.program_id(1) == n_g - 1),
        name=name, grid=(n_hp, n_g),
        in_specs=[q_spec] + kv_specs + [tiles_spec, clip_spec, q_spec],
        out_specs=[q_spec, col_spec, col_spec, tiles_spec, clip_spec] + [sum_spec] * 3,
        out_shape=[jax.ShapeDtypeStruct((S, D), BF16)] + [jax.ShapeDtypeStruct((S, D), F32)] * 2
        + [jax.ShapeDtypeStruct((ATT_H, len(REL_TILES), REL_TILE, REL_TILE), F32),
           jax.ShapeDtypeStruct((ATT_H, 1, 128), F32)] + [jax.ShapeDtypeStruct((1, D), F32)] * 3,
        scratch_shapes=[pltpu.VMEM((2 * ATT_QB, ATT_KB), F32)], compiler_params=_params(2),
        args=(*([qkv] * 7), tiles, clip, do))


def mods_partial(c_all, w_ada, *, name):
    n_l, _, n_c = w_ada.shape

    def body(c_ref, w_ref, o_ref):
        o_ref[...] = _dot(_bf(_silu(c_ref[...])), _bf(w_ref[...]))

    return pl.pallas_call(
        body, name=name, grid=(n_l,),
        in_specs=[pl.BlockSpec((N_DEV, D), lambda l: (0, 0)), pl.BlockSpec((None, D, n_c), lambda l: (l, 0, 0))],
        out_specs=pl.BlockSpec((None, N_DEV, n_c), lambda l: (l, 0, 0)),
        out_shape=jax.ShapeDtypeStruct((n_l, N_DEV, n_c), F32), compiler_params=_params(1),
    )(c_all, w_ada)


def w_ada_grad(c_all, dm, *, name):
    n_l, _, n_c = dm.shape

    def body(c_ref, d_ref, o_ref):
        o_ref[...] = lax.dot_general(_silu(c_ref[...]), d_ref[...], (((0,), (0,)), ((), ())),
                                     precision=HIGHEST, preferred_element_type=F32)

    return pl.pallas_call(
        body, name=name, grid=(n_l,),
        in_specs=[pl.BlockSpec((N_DEV, D), lambda l: (0, 0)), pl.BlockSpec((None, N_DEV, n_c), lambda l: (l, 0, 0))],
        out_specs=pl.BlockSpec((None, D, n_c), lambda l: (l, 0, 0)),
        out_shape=jax.ShapeDtypeStruct((n_l, D, n_c), F32), compiler_params=_params(1),
    )(c_all, dm)


def adamw(w, m, v, gparts, *, block_rows, name):
    R, C = w.shape
    n = gparts.shape[0]

    def body(w_ref, m_ref, v_ref, g_ref, go_ref, d_ref, mo_ref, vo_ref):
        g = g_ref[0].astype(F32)
        for k in range(1, n):
            g = g + g_ref[k].astype(F32)
        m_new = ADAM_B1 * m_ref[...] + (1.0 - ADAM_B1) * g
        v_new = ADAM_B2 * v_ref[...] + (1.0 - ADAM_B2) * (g * g)
        m_hat = m_new / (1.0 - ADAM_B1 ** ADAM_STEP)
        v_hat = v_new / (1.0 - ADAM_B2 ** ADAM_STEP)
        go_ref[...] = g
        d_ref[...] = -ADAM_LR * (m_hat / (jnp.sqrt(v_hat) + ADAM_EPS) + ADAM_WD * w_ref[...])
        mo_ref[...] = m_new
        vo_ref[...] = v_new

    blk = pl.BlockSpec((block_rows, C), lambda i: (i, 0))
    return pl.pallas_call(
        body, name=name, grid=(R // block_rows,),
        in_specs=[blk, blk, blk, pl.BlockSpec((n, block_rows, C), lambda i: (0, i, 0))],
        out_specs=[blk] * 4, out_shape=[jax.ShapeDtypeStruct((R, C), F32)] * 4, compiler_params=_params(1),
    )(w, m, v, gparts)


def adamw_nd(w, m, v, gparts, *, name):
    shape = w.shape
    two = (int(np.prod(shape[:-1])), shape[-1])
    rows = two[0]
    block_rows = rows
    for cand in (512, 256):
        if rows > cand and rows % cand == 0:
            block_rows = cand
            break
    outs = adamw(w.reshape(two), m.reshape(two), v.reshape(two), gparts.reshape((gparts.shape[0],) + two),
                 block_rows=block_rows, name=name)
    return [o.reshape(shape) for o in outs]


def sum_parts(parts, *, name):
    n, R, C = parts.shape

    def body(p_ref, o_ref):
        acc = p_ref[0]
        for k in range(1, n):
            acc = acc + p_ref[k]
        o_ref[...] = acc

    return pl.pallas_call(
        body, name=name, in_specs=[pl.BlockSpec((n, R, C), lambda: (0, 0, 0))],
        out_specs=pl.BlockSpec((R, C), lambda: (0, 0)), out_shape=jax.ShapeDtypeStruct((R, C), F32),
        compiler_params=pltpu.CompilerParams(vmem_limit_bytes=VMEM_LIMIT),
    )(parts)


def _my_place():
    return lax.axis_index("x"), lax.axis_index("y"), lax.axis_index("c")


def _full_shape(kind, shard):
    n_l, rows, cols = shard
    return {"col": (n_l, rows, N_DEV * cols), "row": (n_l, N_DEV * rows, cols), "stk": (N_DEV, n_l, rows, cols)}[kind]


def _slab(ref, kind, dev, shard):
    _, rows, cols = shard
    if kind == "col":
        return ref.at[:, :, pl.ds(pl.multiple_of(dev * cols, 128), cols)]
    if kind == "row":
        return ref.at[:, pl.ds(pl.multiple_of(dev * rows, 8), rows), :]
    return ref.at[dev]


def all_gather(x_shard, *, name):
    m_per, n = x_shard.shape

    def body(x_ref, out_ref, send_sems, recv_sems, local_sem):
        x, y, c = _my_place()
        me, sibling = (x, y, c), (x, y, 1 - c)
        chips = [(1 - x, y), (x, 1 - y), (1 - x, 1 - y)]

        def rows(px, py, pc):
            return out_ref.at[pl.ds((4 * px + 2 * py + pc) * m_per, m_per), :]

        def copy(k, block, to, src=None):
            return pltpu.make_async_remote_copy(
                src_ref=rows(*block) if src is None else src, dst_ref=rows(*block),
                send_sem=send_sems.at[k], recv_sem=recv_sems.at[k], device_id=to, device_id_type=MESH)

        mine = pltpu.make_async_copy(x_ref, rows(*me), local_sem)
        mine.start()
        first = [copy(0, me, sibling, src=x_ref)]
        first += [copy(1 + j, me, (*chip, c), src=x_ref) for j, chip in enumerate(chips)]
        for cp in first:
            cp.start()
        passed = [copy(4 + j, (*chip, c), sibling) for j, chip in enumerate(chips)]
        for j, chip in enumerate(chips):
            copy(1 + j, (*chip, c), me).wait_recv()
            passed[j].start()
        copy(0, sibling, me).wait_recv()
        for j, chip in enumerate(chips):
            copy(4 + j, (*chip, 1 - c), me).wait_recv()
        for cp in first + passed:
            cp.wait_send()
        mine.wait()

    return pl.pallas_call(
        body, name=name, out_shape=jax.ShapeDtypeStruct((N_DEV * m_per, n), x_shard.dtype),
        in_specs=[pl.BlockSpec(memory_space=pltpu.VMEM)], out_specs=pl.BlockSpec(memory_space=pltpu.VMEM),
        scratch_shapes=[pltpu.SemaphoreType.DMA((7,)), pltpu.SemaphoreType.DMA((7,)), pltpu.SemaphoreType.DMA],
        compiler_params=pltpu.CompilerParams(vmem_limit_bytes=VMEM_LIMIT),
    )(x_shard)


def gather_plan(shards, kinds, layers):
    n_t = len(shards)
    shapes = [(1,) + tuple(s.shape[1:]) for s in shards]

    def copies(x_refs, out_refs, sems):
        send_sems, recv_sems, local_sems = sems
        x, y, c = _my_place()
        me, sibling = (x, y, c), (x, y, 1 - c)
        chips = [(1 - x, y), (x, 1 - y), (1 - x, 1 - y)]
        own = [x_refs[t].at[pl.ds(layers[t], 1)] for t in range(n_t)]

        def slab(t, px, py, pc):
            return _slab(out_refs[t], kinds[t], 4 * px + 2 * py + pc, shapes[t])

        def copy(t, k, block, to, src=None):
            return pltpu.make_async_remote_copy(
                src_ref=slab(t, *block) if src is None else src, dst_ref=slab(t, *block),
                send_sem=send_sems.at[7 * t + k], recv_sem=recv_sems.at[7 * t + k], device_id=to,
                device_id_type=MESH)

        mine = [pltpu.make_async_copy(own[t], slab(t, *me), local_sems.at[t]) for t in range(n_t)]
        sends = []
        for t in range(n_t):
            sends.append(copy(t, 0, me, sibling, src=own[t]))
            sends += [copy(t, 1 + j, me, (*chip, c), src=own[t]) for j, chip in enumerate(chips)]
        return mine, sends, copy, me, sibling, chips, c

    def first(x_refs, out_refs, sems):
        mine, sends = copies(x_refs, out_refs, sems)[:2]
        for cp in mine + sends:
            cp.start()

    def last(x_refs, out_refs, sems):
        mine, sends, copy, me, sibling, chips, c = copies(x_refs, out_refs, sems)
        passed = []
        for j, chip in enumerate(chips):
            for t in range(n_t):
                copy(t, 1 + j, (*chip, c), me).wait_recv()
                passed.append(copy(t, 4 + j, (*chip, c), sibling))
                passed[-1].start()
        for t in range(n_t):
            copy(t, 0, sibling, me).wait_recv()
        for j, chip in enumerate(chips):
            for t in range(n_t):
                copy(t, 4 + j, (*chip, 1 - c), me).wait_recv()
        for cp in sends + passed:
            cp.wait_send()
        for cp in mine:
            cp.wait()

    return Hosted(
        list(shards), [jax.ShapeDtypeStruct(_full_shape(k, shp), s.dtype) for k, shp, s in zip(kinds, shapes, shards)],
        [pltpu.SemaphoreType.DMA((7 * n_t,)), pltpu.SemaphoreType.DMA((7 * n_t,)), pltpu.SemaphoreType.DMA((n_t,))],
        first, last)


def scatter_plan(grads, kinds, shapes):
    n_t = len(grads)

    def copies(g_refs, out_refs, sems):
        send_sems, recv_sems, local_sems = sems
        x, y, c = _my_place()
        me = 4 * x + 2 * y + c
        local = [pltpu.make_async_copy(_slab(g_refs[t], kinds[t], me, shapes[t]), out_refs[t].at[me],
                                       local_sems.at[t]) for t in range(n_t)]
        remote = []
        for t in range(n_t):
            for r in range(1, N_DEV):
                px = 1 - x if r & 4 else x
                py = 1 - y if r & 2 else y
                pc = 1 - c if r & 1 else c
                remote.append(pltpu.make_async_remote_copy(
                    src_ref=_slab(g_refs[t], kinds[t], 4 * px + 2 * py + pc, shapes[t]), dst_ref=out_refs[t].at[me],
                    send_sem=send_sems.at[7 * t + r - 1], recv_sem=recv_sems.at[7 * t + r - 1],
                    device_id=(px, py, pc), device_id_type=MESH))
        return local, remote

    def first(g_refs, out_refs, sems):
        local, remote = copies(g_refs, out_refs, sems)
        for cp in local + remote:
            cp.start()

    def last(g_refs, out_refs, sems):
        local, remote = copies(g_refs, out_refs, sems)
        for cp in remote + local:
            cp.wait()

    return Hosted(
        list(grads), [jax.ShapeDtypeStruct((N_DEV,) + tuple(s), BF16) for s in shapes],
        [pltpu.SemaphoreType.DMA((7 * n_t,)), pltpu.SemaphoreType.DMA((7 * n_t,)), pltpu.SemaphoreType.DMA((n_t,))],
        first, last)


def adamw_layer(w, m, v, parts, layer, bufs, *, name):
    n_l, rows, cols = w.shape
    n = parts.shape[0]
    tr = min(rows, 256)

    def body(w_ref, m_ref, v_ref, g_ref, *rest):
        go_ref, d_ref, mo_ref, vo_ref = rest[-4:]
        g = g_ref[0].astype(F32)
        for k in range(1, n):
            g = g + g_ref[k].astype(F32)
        m_new = ADAM_B1 * m_ref[...] + (1.0 - ADAM_B1) * g
        v_new = ADAM_B2 * v_ref[...] + (1.0 - ADAM_B2) * (g * g)
        m_hat = m_new / (1.0 - ADAM_B1 ** ADAM_STEP)
        v_hat = v_new / (1.0 - ADAM_B2 ** ADAM_STEP)
        go_ref[...] = g
        d_ref[...] = -ADAM_LR * (m_hat / (jnp.sqrt(v_hat) + ADAM_EPS) + ADAM_WD * w_ref[...])
        mo_ref[...] = m_new
        vo_ref[...] = v_new

    blk = pl.BlockSpec((None, tr, cols), lambda i: (layer, i, 0))
    in_specs = [blk, blk, blk, pl.BlockSpec((n, None, tr, cols), lambda i: (0, 0, i, 0))]
    args = [w, m, v, parts]
    aliases = {}
    if bufs is not None:
        in_specs += [pl.BlockSpec(memory_space=pl.ANY)] * 4
        args += list(bufs)
        aliases = {4 + k: k for k in range(4)}
    return pl.pallas_call(
        body, name=name, grid=(rows // tr,), in_specs=in_specs, out_specs=[blk] * 4,
        out_shape=[jax.ShapeDtypeStruct((n_l, rows, cols), F32)] * 4, input_output_aliases=aliases,
        compiler_params=_params(1),
    )(*args)


BIG =("gla_w_in", "gla_w_out", "att_w_in", "att_w_out", "ff_w1", "ff_w2")
KIND = {"gla_w_in": "stk", "gla_w_out": "row", "att_w_in": "col", "att_w_out": "row", "ff_w1": "col", "ff_w2": "row"}


def _pack_small(arrs):
    parts = []
    for a in arrs:
        f = a.reshape(-1)
        parts.append(jnp.pad(f, (0, -f.shape[0] % 128)))
    flat = jnp.concatenate(parts)
    flat = jnp.pad(flat, (0, -flat.shape[0] % 1024))
    return flat.reshape(-1, 128)


def _unpack_small(packed, shapes):
    flat = packed.reshape(packed.shape[:-2] + (-1,))
    out, off = [], 0
    for shp in shapes:
        n = int(np.prod(shp))
        out.append(flat[..., off:off + n].reshape(packed.shape[:-2] + tuple(shp)))
        off += n + (-n % 128)
    return out


def _vec(a):
    return a.reshape(1, -1)


def _layer_weights(i):
    mixer = "gla" if i % 2 == 0 else "att"
    return [(f"{mixer}_w_in", i // 2), (f"{mixer}_w_out", i // 2), ("ff_w1", i), ("ff_w2", i)]


def _trunk(x, target, mods, sm, w, m, v):
    shard_bf = {n: w[n].astype(BF16) for n in BIG}

    def gather_of(names):
        return gather_plan([shard_bf[n] for n, _ in names], [KIND[n] for n, _ in names], [l for _, l in names])

    def gather_layer(i):
        return _layer_weights(i), gather_of(_layer_weights(i))

    wts = {}

    def keep_gathered(names, arrays):
        for (n, l), a in zip(names, arrays):
            if KIND[n] == "stk":
                a = a.transpose(1, 2, 0, 3).reshape(1, D, GLA_IN)
                a = jnp.pad(a, ((0, 0), (0, 0), (0, GLA_INP - GLA_IN)))
            wts[n, l] = a

    first_names, rest_names = _layer_weights(0)[:1], _layer_weights(0)[1:]
    keep_gathered(first_names, run_hosted(gather_of(first_names), name="gather_first"))

    rel_idx = _rel_index()
    saved = []
    for i in range(DEPTH):
        sh1, sc1, g1, sh2, sc2, g2 = [mods[i, k:k + 1] for k in range(6)]
        rec = {"x0": x}
        j = i // 2
        nxt_names, nxt_plan = gather_layer(i + 1) if i + 1 < DEPTH else (None, None)
        if i % 2 == 0:
            w2p = jnp.pad(sm["gla_w_gk2"][j], ((0, 128 - GLA_RANK), (0, 0)))
            bgk, gn = _vec(sm["gla_b_gk"][j]), _vec(sm["gla_g_norm"][j])
            if i == 0:
                proj, got0 = mm_nn(x, wts["gla_w_in", j], 0, pro="mod", p1=sc1, p2=sh1, tm=512, tn=GLA_INP,
                                   comm=gather_of(rest_names), name=f"gla_proj_{i}")
                keep_gathered(rest_names, got0)
            else:
                proj = mm_nn(x, wts["gla_w_in", j], 0, pro="mod", p1=sc1, p2=sh1, tm=512, tn=GLA_INP,
                             name=f"gla_proj_{i}")
            (og, states), got = gla_fwd(proj, w2p, bgk, gn, comm=nxt_plan, name=f"gla_core_{i}")
            y = mm_nn(og, wts["gla_w_out", j], 0, tm=1024, tn=1024, name=f"gla_out_{i}")
            rec.update(kind="gla", j=j, w2p=w2p, bgk=bgk, gn=gn, proj=proj, og=og, states=states)
        else:
            rel = sm["att_rel_bias"][j]
            rel_pad = jnp.pad(rel, ((0, 0), (0, REL_PAD - N_REL)), constant_values=NEG_INF)
            tiles = rel_bias_tiles(rel_pad, rel_idx, name=f"att_bias_{i}")
            tiles = tiles.reshape(ATT_H, len(REL_TILES), REL_TILE, REL_TILE)
            clip = jnp.broadcast_to(rel[:, 2 * MAX_REL][:, None, None], (ATT_H, 1, 128))
            qkv = mm_nn(x, wts["att_w_in", j], 0, pro="mod", p1=sc1, p2=sh1, bias=_vec(sm["att_b_in"][j]),
                        out_dtype=BF16, tm=512, tn=3 * D, name=f"att_proj_{i}")
            (o,), got = attn_fwd(qkv, tiles, clip, comm=nxt_plan, name=f"att_core_{i}")
            y = mm_nn(o, wts["att_w_out", j], 0, tm=1024, tn=1024, name=f"att_out_{i}")
            rec.update(kind="att", j=j, tiles=tiles, clip=clip, qkv=qkv, o=o)
        if nxt_plan is not None:
            keep_gathered(nxt_names, got)
        x1 = ln_fwd(x, y, g1, _vec(sm["ln_g"][i, 0]), _vec(sm["ln_b"][i, 0]), name=f"ln_mix_{i}")
        h = mm_nn(x1, wts["ff_w1", i], 0, pro="mod", p1=sc2, p2=sh2, out_dtype=BF16, tm=2048, tn=1024,
                  name=f"ff_up_{i}")
        y2 = mm_nn_ksplit(h, wts["ff_w2", i], 0, pro="relu2", tm=1024, tk=2048, name=f"ff_down_{i}")
        x2 = ln_fwd(x1, y2, g2, _vec(sm["ln_g"][i, 1]), _vec(sm["ln_b"][i, 1]), name=f"ln_ff_{i}")
        rec.update(y=y, x1=x1, h=h, y2=y2)
        saved.append(rec)
        x = x2

    dy, loss = loss_head(x, target, name="loss_head")

    gw = {}

    def wgrad(weight, layer, a, d, *, tn, tk=1024, tm=512, col_block0=0, **kw):
        gw[weight, layer] = mm_tn(a, d, tk=tk, tn=tn, tm=tm, out_buf=gw.get((weight, layer)),
                                  out_shape=wts[weight, layer].shape, col_block0=col_block0, **kw)

    def scatter_layer(units):
        grads = []
        for n, l in units:
            g = gw[n, l]
            if KIND[n] == "stk":
                g = g[:, :, :GLA_IN].reshape(1, D, N_DEV, GLA_IN // N_DEV).transpose(2, 0, 1, 3)
            grads.append(g)
        return scatter_plan(grads, [KIND[n] for n, _ in units], [(1,) + tuple(w[n].shape[1:]) for n, _ in units])

    results = {n: None for n in BIG}

    def update(units, parts):
        for (n, l), p in zip(units, parts):
            results[n] = adamw_layer(w[n], m[n], v[n], p, l, results[n], name=f"adamw_{n}_{l}")

    gs = {"ln_g": [[None, None] for _ in range(DEPTH)], "ln_b": [[None, None] for _ in range(DEPTH)],
          "gla_w_gk2": [None] * 2, "gla_b_gk": [None] * 2, "gla_g_norm": [None] * 2, "att_b_in": [None] * 2,
          "att_rel_bias": [None] * 2}
    dmods = [[None] * 6 for _ in range(DEPTH)]
    nxt = None
    nxt_slot = None
    for i in reversed(range(DEPTH)):
        rec = saved[i]
        sh1, sc1, g1, sh2, sc2, g2 = [mods[i, k:k + 1] for k in range(6)]
        x0, x1 = rec["x0"], rec["x1"]
        if nxt is None:
            dz2, acc = ln_bwd(x1, rec["y2"], g2, _vec(sm["ln_g"][i, 1]), dout=dy, name=f"ln_ff_bwd_{i}")
        else:
            dz2, acc = ln_bwd(x1, rec["y2"], g2, _vec(sm["ln_g"][i, 1]), nxt=nxt[:3] + (_vec(sm["ln_b"][i, 1]),),
                              name=f"ln_ff_bwd_{i}")
            dmods[nxt_slot[0]][nxt_slot[1]] = acc[3]
            dmods[nxt_slot[0]][nxt_slot[2]] = acc[4]
        gs["ln_g"][i][1], gs["ln_b"][i][1], dmods[i][5] = acc[0], acc[1], acc[2]
        dh = mm_nt([dz2], wts["ff_w2", i], 0, pro="scale", p1=g2, epi_h=rec["h"], out_dtype=BF16, tm=2048, tn=512,
                   name=f"ff_down_bwd_{i}")
        wgrad("ff_w2", i, rec["h"], dz2, pro="relu2", dscale=g2, tk=2048, tn=1024, name=f"ff_w2_grad_{i}")
        du2 = mm_nt([dh], wts["ff_w1", i], 0, tm=1024, tn=1024, name=f"ff_up_bwd_{i}")
        wgrad("ff_w1", i, x1, dh, pro="mod", p1=sc2, p2=sh2, tn=2048, name=f"ff_w1_grad_{i}")
        dz1, acc = ln_bwd(x0, rec["y"], g1, _vec(sm["ln_g"][i, 0]), nxt=(dz2, du2, sc2, _vec(sm["ln_b"][i, 0])),
                          name=f"ln_mix_bwd_{i}")
        dmods[i][4], dmods[i][3] = acc[3], acc[4]
        gs["ln_g"][i][0], gs["ln_b"][i][0], dmods[i][2] = acc[0], acc[1], acc[2]
        j = rec["j"]
        w_in, w_out = _layer_weights(i)[:2]
        if rec["kind"] == "gla":
            dog = mm_nt([dz1], wts[w_out], 0, pro="scale", p1=g1, tm=1024, tn=1024, name=f"gla_out_bwd_{i}")
            wgrad(*w_out, rec["og"], dz1, dscale=g1, tn=1024, name=f"gla_wout_grad_{i}")
        else:
            do = mm_nt([dz1], wts[w_out], 0, pro="scale", p1=g1, out_dtype=BF16, tm=1024, tn=1024,
                       name=f"att_out_bwd_{i}")
            wgrad(*w_out, rec["o"], dz1, dscale=g1, tn=1024, name=f"att_wout_grad_{i}")
        units = [("ff_w1", i), ("ff_w2", i), w_out] + ([_layer_weights(i + 1)[0]] if i + 1 < DEPTH else [])
        plan = scatter_layer(units)
        if rec["kind"] == "gla":
            (dproj, dw2p, dbgk, dgn), parts = gla_bwd(rec["proj"], dog, rec["states"], rec["w2p"], rec["bgk"],
                                                      rec["gn"], comm=plan, name=f"gla_core_bwd_{i}")
            gs["gla_w_gk2"][j], gs["gla_b_gk"][j], gs["gla_g_norm"][j] = dw2p[:GLA_RANK], dbgk[0], dgn[0]
            wgrad(*w_in, x0, dproj, pro="mod", p1=sc1, p2=sh1, tk=512, tn=GLA_INP, name=f"gla_win_grad_{i}")
            if i == 0:
                du1, last_parts = mm_nt([dproj], wts[w_in], 0, tm=1024, tn=1024, comm=scatter_layer([w_in]),
                                        name=f"gla_proj_bwd_{i}")
                update([w_in], last_parts)
            else:
                du1 = mm_nt([dproj], wts[w_in], 0, tm=1024, tn=1024, name=f"gla_proj_bwd_{i}")
        else:
            (dq, dk, dv, dtiles, dclip, sq, sk, sv), parts = attn_bwd(rec["qkv"], do, rec["tiles"], rec["clip"],
                                                                      comm=plan, name=f"att_core_bwd_{i}")
            drel = rel_bias_grad(dtiles.reshape(ATT_H, -1), dclip.reshape(ATT_H, 128), rel_idx,
                                 name=f"att_bias_grad_{i}")
            gs["att_rel_bias"][j] = drel[:, :N_REL]
            gs["att_b_in"][j] = jnp.concatenate([sq[0], sk[0], sv[0]])
            du1 = mm_nt([dq, dk, dv], wts[w_in], 0, tm=512, tn=1024, name=f"att_proj_bwd_{i}")
            for n, t in enumerate((dq, dk, dv)):
                wgrad(*w_in, x0, t, pro="mod", p1=sc1, p2=sh1, tn=1024, col_block0=n, name=f"att_win_grad_{i}_{n}")
        update(units, parts)
        nxt = (dz1, du1, sc1, x0)
        nxt_slot = (i, 1, 0)
    dx, acc = combine_final(nxt[0], nxt[1], nxt[2], nxt[3], name="grad_x")
    dmods[0][1], dmods[0][0] = acc[3], acc[4]
    dmods = jnp.stack([jnp.stack(r) for r in dmods])
    gs = {k: jnp.stack([jnp.stack(r) if isinstance(r, list) else r for r in v]) for k, v in gs.items()}
    return loss, dx, dmods, gs, results


WEIGHTS = ("w_ada", "b_ada", "ln_g", "ln_b", "gla_w_in", "gla_w_gk2", "gla_b_gk", "gla_g_norm", "gla_w_out",
           "att_w_in", "att_b_in", "att_rel_bias", "att_w_out", "ff_w1", "ff_w2")
SMALL_SHARDED = {"ln_g": 2, "ln_b": 2, "gla_w_gk2": 2, "gla_g_norm": 2, "att_b_in": 1}
SMALL_GRADS = ("ln_g", "ln_b", "gla_w_gk2", "gla_b_gk", "gla_g_norm", "att_b_in", "att_rel_bias")


def kernel(x, c, w_ada, b_ada, ln_g, ln_b, gla_w_in, gla_w_gk2, gla_b_gk, gla_g_norm, gla_w_out, att_w_in, att_b_in, att_rel_bias, att_w_out, ff_w1, ff_w2, loss_target, m_w_ada, m_b_ada, m_ln_g, m_ln_b, m_gla_w_in, m_gla_w_gk2, m_gla_b_gk, m_gla_g_norm, m_gla_w_out, m_att_w_in, m_att_b_in, m_att_rel_bias, m_att_w_out, m_ff_w1, m_ff_w2, v_w_ada, v_b_ada, v_ln_g, v_ln_b, v_gla_w_in, v_gla_w_gk2, v_gla_b_gk, v_gla_g_norm, v_gla_w_out, v_att_w_in, v_att_b_in, v_att_rel_bias, v_att_w_out, v_ff_w1, v_ff_w2):
    w = dict(w_ada=w_ada, b_ada=b_ada, ln_g=ln_g, ln_b=ln_b, gla_w_in=gla_w_in, gla_w_gk2=gla_w_gk2,
             gla_b_gk=gla_b_gk, gla_g_norm=gla_g_norm, gla_w_out=gla_w_out, att_w_in=att_w_in, att_b_in=att_b_in,
             att_rel_bias=att_rel_bias, att_w_out=att_w_out, ff_w1=ff_w1, ff_w2=ff_w2)
    m = dict(w_ada=m_w_ada, b_ada=m_b_ada, ln_g=m_ln_g, ln_b=m_ln_b, gla_w_in=m_gla_w_in, gla_w_gk2=m_gla_w_gk2,
             gla_b_gk=m_gla_b_gk, gla_g_norm=m_gla_g_norm, gla_w_out=m_gla_w_out, att_w_in=m_att_w_in,
             att_b_in=m_att_b_in, att_rel_bias=m_att_rel_bias, att_w_out=m_att_w_out, ff_w1=m_ff_w1, ff_w2=m_ff_w2)
    v = dict(w_ada=v_w_ada, b_ada=v_b_ada, ln_g=v_ln_g, ln_b=v_ln_b, gla_w_in=v_gla_w_in, gla_w_gk2=v_gla_w_gk2,
             gla_b_gk=v_gla_b_gk, gla_g_norm=v_gla_g_norm, gla_w_out=v_gla_w_out, att_w_in=v_att_w_in,
             att_b_in=v_att_b_in, att_rel_bias=v_att_rel_bias, att_w_out=v_att_w_out, ff_w1=v_ff_w1, ff_w2=v_ff_w2)
    xi, yi, ci = _my_place()
    me = 4 * xi + 2 * yi + ci

    small_names = tuple(SMALL_SHARDED)
    small_in = _pack_small([c] + [w[n] for n in small_names])
    small_all = all_gather(small_in, name="gather_small").reshape(N_DEV, -1, 128)
    parts = _unpack_small(small_all, [c.shape] + [w[n].shape for n in small_names])
    c_all = parts[0].reshape(N_DEV, D)
    sm = {"gla_b_gk": gla_b_gk, "att_rel_bias": att_rel_bias}
    for n, p in zip(small_names, parts[1:]):
        ax = SMALL_SHARDED[n]
        sm[n] = jnp.moveaxis(p, 0, ax).reshape(p.shape[1:ax + 1] + (N_DEV * p.shape[ax + 1],) + p.shape[ax + 2:])

    n_ada = w_ada.shape[2]
    mp = mods_partial(c_all, w_ada, name="mods_partial")
    mp_all = all_gather(mp.reshape(DEPTH * N_DEV, n_ada), name="gather_mods")
    mp_all = mp_all.reshape(N_DEV, DEPTH, N_DEV, n_ada)
    mods = lax.dynamic_index_in_dim(mp_all, me, axis=2, keepdims=False)
    mods = mods.transpose(1, 0, 2).reshape(DEPTH, 6 * D) + b_ada
    mods = mods.reshape(DEPTH, 6, D)

    loss, dx, dmods, gs, results = _trunk(x.reshape(x.shape[1:]), loss_target.reshape(x.shape[1:]), mods, sm, w, m, v)
    loss = lax.psum(loss[0, 0], ("x", "y", "c"))

    dm_flat = dmods.reshape(DEPTH, 6 * D)
    small_g = [dm_flat] + [gs[n].reshape(sm[n].shape) for n in SMALL_GRADS]
    small_shapes = [a.shape for a in small_g]
    sg_all = all_gather(_pack_small(small_g), name="gather_small_grads").reshape(N_DEV, -1, 128)
    summed = _unpack_small(sum_parts(sg_all, name="sum_small_grads"), small_shapes)
    g_full = dict(zip(("b_ada",) + SMALL_GRADS, summed))
    dm_all = _unpack_small(sg_all, small_shapes)[0]
    dm_mine = lax.dynamic_slice_in_dim(dm_all, me * n_ada, n_ada, axis=2).transpose(1, 0, 2)
    g_w_ada = w_ada_grad(c_all, dm_mine, name="w_ada_grad")

    results["w_ada"] = adamw_nd(w_ada, m_w_ada, v_w_ada, g_w_ada[None], name="adamw_w_ada")
    for n in ("b_ada",) + SMALL_GRADS:
        g = g_full[n]
        if n in SMALL_SHARDED:
            ax = SMALL_SHARDED[n]
            width = w[n].shape[ax]
            g = lax.dynamic_slice_in_dim(g, me * width, width, axis=ax)
        results[n] = adamw_nd(w[n], m[n], v[n], g[None], name=f"adamw_{n}")

    out = [loss, dx[None]]
    for k in range(4):
        out += [results[n][k] for n in WEIGHTS]
    return tuple(out)
```

```python
import numpy as np
import jax
import jax.numpy as jnp
from jax import lax
from jax.experimental import pallas as pl
from jax.experimental.pallas import tpu as pltpu

F32 = jnp.float32
BF16 = jnp.bfloat16
HIGHEST = lax.Precision.HIGHEST
MESH = pl.DeviceIdType.MESH

N_DEV = 8
D = 1024
DEPTH = 4
CHUNK = 64
ALPHA = (2.0 * DEPTH) ** 0.25
LN_EPS = 1e-5
RMS_EPS = 1e-6
NEG_INF = -1e30

GLA_H = 4
GLA_DKH = 128
GLA_DVH = 256
GLA_DK = GLA_H * GLA_DKH
GLA_DV = GLA_H * GLA_DVH
GLA_RANK = 16
GLA_IN = 2 * GLA_DK + 2 * GLA_DV + GLA_RANK
GLA_INP = 3200
GLA_TAU_INV = 1.0 / 16.0
GLA_SUB = 2

ATT_H = 16
ATT_HD = 64
ATT_QB = 256
ATT_KB = 3 * ATT_QB
LEFT = 8 * CHUNK
MAX_REL = 128
N_REL = 2 * MAX_REL + 1
REL_PAD = 384
REL_TILE = 128
REL_TILES = (3, 4)
D_FF = 4 * D

ADAM_LR = 0.001
ADAM_B1 = 0.9
ADAM_B2 = 0.999
ADAM_EPS = 1e-08
ADAM_WD = 0.01
ADAM_STEP = 10

VMEM_LIMIT = 48 * 1024 * 1024


def _params(n_axes):
    return pltpu.CompilerParams(dimension_semantics=("arbitrary",) * n_axes, vmem_limit_bytes=VMEM_LIMIT)


def _dot(a, b):
    return jnp.dot(a, b, preferred_element_type=F32)


def _dot_nt(a, b):
    return lax.dot_general(a, b, (((1,), (1,)), ((), ())), preferred_element_type=F32)


def _dot_tn(a, b):
    return lax.dot_general(a, b, (((0,), (0,)), ((), ())), preferred_element_type=F32)


def _bf(a):
    return a.astype(BF16)


def _prologue(kind, a, p1=None, p2=None):
    if kind == "mod":
        return a.astype(F32) * (1.0 + p1) + p2
    if kind == "scale":
        return a.astype(F32) * (1.0 + p1)
    if kind == "relu2":
        r = jnp.maximum(a, 0.0)
        return r * r
    return a


class Hosted:
    def __init__(self, inputs, out_shapes, sems, first, last):
        self.inputs, self.out_shapes, self.sems, self.first, self.last = inputs, out_shapes, sems, first, last


def _hbm_specs(n):
    return [pl.BlockSpec(memory_space=pltpu.HBM)] * n


def _call_hosting(body, comm, *, first, last, in_specs, out_specs, out_shape, scratch_shapes, args, **kw):
    if comm is None:
        return pl.pallas_call(body, in_specs=in_specs, out_specs=out_specs, out_shape=out_shape,
                              scratch_shapes=scratch_shapes, **kw)(*args), []
    n_in, n_out, n_scr = len(in_specs), len(out_specs), len(scratch_shapes)
    n_ci, n_co = len(comm.inputs), len(comm.out_shapes)

    def hosting(*refs):
        ins, ci = refs[:n_in], refs[n_in:n_in + n_ci]
        k = n_in + n_ci
        outs, co = refs[k:k + n_out], refs[k + n_out:k + n_out + n_co]
        k += n_out + n_co
        scr, cs = refs[k:k + n_scr], refs[k + n_scr:]

        @pl.when(first())
        def _():
            comm.first(ci, co, cs)

        body(*ins, *outs, *scr)

        @pl.when(last())
        def _():
            comm.last(ci, co, cs)

    res = pl.pallas_call(
        hosting, in_specs=list(in_specs) + _hbm_specs(n_ci), out_specs=list(out_specs) + _hbm_specs(n_co),
        out_shape=list(out_shape) + list(comm.out_shapes), scratch_shapes=list(scratch_shapes) + list(comm.sems),
        **kw)(*args, *comm.inputs)
    return res[:n_out], res[n_out:]


def run_hosted(comm, *, name):
    n_i, n_o = len(comm.inputs), len(comm.out_shapes)

    def body(*refs):
        ins, outs, sems = refs[:n_i], refs[n_i:n_i + n_o], refs[n_i + n_o:]
        comm.first(ins, outs, sems)
        comm.last(ins, outs, sems)

    return pl.pallas_call(body, name=name, out_shape=list(comm.out_shapes), in_specs=_hbm_specs(n_i),
                          out_specs=_hbm_specs(n_o), scratch_shapes=list(comm.sems))(*comm.inputs)


def mm_nn(a, b, layer, *, pro=None, p1=None, p2=None, bias=None, out_dtype=F32, tm, tn, comm=None, name):
    M, K = a.shape
    N = b.shape[2]
    tm = min(tm, M)
    n_p = {"mod": 2, "scale": 1}.get(pro, 0)
    has_bias = bias is not None
    direct = pro is None and a.dtype == BF16

    def body(*refs):
        a_ref, b_ref = refs[0], refs[1]
        p_refs = refs[2:2 + n_p]
        bias_ref = refs[2 + n_p] if has_bias else None
        if direct:
            o_ref = refs[-1]
            lhs = a_ref[...]
        else:
            o_ref, abf = refs[-2], refs[-1]

            @pl.when(pl.program_id(1) == 0)
            def _():
                abf[...] = _bf(_prologue(pro, a_ref[...], *[r[...] for r in p_refs]))

            lhs = abf[...]
        acc = _dot(lhs, b_ref[...])
        if has_bias:
            acc = acc + bias_ref[...]
        o_ref[...] = acc.astype(out_dtype)

    in_specs = [pl.BlockSpec((tm, K), lambda i, j: (i, 0)), pl.BlockSpec((None, K, tn), lambda i, j: (layer, 0, j))]
    args = [a, b]
    for p in (p1, p2)[:n_p]:
        in_specs.append(pl.BlockSpec((1, K), lambda i, j: (0, 0)))
        args.append(p)
    if has_bias:
        in_specs.append(pl.BlockSpec((1, tn), lambda i, j: (0, j)))
        args.append(bias)
    n_i, n_j = M // tm, N // tn
    (out,), got = _call_hosting(
        body, comm, first=lambda: (pl.program_id(0) == 0) & (pl.program_id(1) == 0),
        last=lambda: (pl.program_id(0) == n_i - 1) & (pl.program_id(1) == n_j - 1),
        name=name, grid=(n_i, n_j), in_specs=in_specs,
        out_specs=[pl.BlockSpec((tm, tn), lambda i, j: (i, j))],
        out_shape=[jax.ShapeDtypeStruct((M, N), out_dtype)],
        scratch_shapes=[] if direct else [pltpu.VMEM((tm, K), BF16)], compiler_params=_params(2), args=args)
    return out if comm is None else (out, got)


def mm_nt(a_parts, w, layer, *, pro=None, p1=None, epi_h=None, out_dtype=F32, tm, tn, comm=None, name):
    M = a_parts[0].shape[0]
    tm = min(tm, M)
    widths = [p.shape[1] for p in a_parts]
    Nw = sum(widths)
    Kw = w.shape[1]
    n_a = len(a_parts)
    has_p = pro == "scale"
    has_h = epi_h is not None
    direct = n_a == 1 and not has_p and a_parts[0].dtype == BF16

    def body(*refs):
        a_refs = refs[:n_a]
        w_ref = refs[n_a]
        k = n_a + 1
        p_ref = refs[k] if has_p else None
        k += int(has_p)
        h_ref = refs[k] if has_h else None
        if direct:
            o_ref = refs[-1]
            lhs = a_refs[0][...]
        else:
            o_ref, abf = refs[-2], refs[-1]

            @pl.when(pl.program_id(1) == 0)
            def _():
                off = 0
                for r, wd in zip(a_refs, widths):
                    av = r[...]
                    if has_p:
                        av = av.astype(F32) * (1.0 + p_ref[...])
                    abf[:, off:off + wd] = _bf(av)
                    off += wd

            lhs = abf[...]
        acc = _dot_nt(lhs, w_ref[...])
        if has_h:
            acc = acc * (2.0 * jnp.maximum(h_ref[...], 0.0))
        o_ref[...] = acc.astype(out_dtype)

    in_specs = [pl.BlockSpec((tm, wd), lambda i, j: (i, 0)) for wd in widths]
    in_specs.append(pl.BlockSpec((None, tn, Nw), lambda i, j: (layer, j, 0)))
    args = list(a_parts) + [w]
    if has_p:
        in_specs.append(pl.BlockSpec((1, Nw), lambda i, j: (0, 0)))
        args.append(p1)
    if has_h:
        in_specs.append(pl.BlockSpec((tm, tn), lambda i, j: (i, j)))
        args.append(epi_h)
    n_i, n_j = M // tm, Kw // tn
    (out,), got = _call_hosting(
        body, comm, first=lambda: (pl.program_id(0) == 0) & (pl.program_id(1) == 0),
        last=lambda: (pl.program_id(0) == n_i - 1) & (pl.program_id(1) == n_j - 1),
        name=name, grid=(n_i, n_j), in_specs=in_specs,
        out_specs=[pl.BlockSpec((tm, tn), lambda i, j: (i, j))],
        out_shape=[jax.ShapeDtypeStruct((M, Kw), out_dtype)],
        scratch_shapes=[] if direct else [pltpu.VMEM((tm, Nw), BF16)], compiler_params=_params(2), args=args)
    return out if comm is None else (out, got)


def mm_tn(a, d, *, pro=None, p1=None, p2=None, dscale=None, tk, tn, tm, out_buf, out_shape, col_block0=0, name):
    M, Kf = a.shape
    N = d.shape[1]
    n_p = {"mod": 2}.get(pro, 0)
    has_ds = dscale is not None
    has_buf = out_buf is not None
    n_m = M // tm

    def body(*refs):
        a_ref, d_ref = refs[0], refs[1]
        p_refs = refs[2:2 + n_p]
        ds_ref = refs[2 + n_p] if has_ds else None
        o_ref, acc = refs[-2], refs[-1]
        m = pl.program_id(2)

        @pl.when(m == 0)
        def _():
            acc[...] = jnp.zeros_like(acc)

        av = _prologue(pro, a_ref[...], *[r[...] for r in p_refs])
        dv = d_ref[...]
        if has_ds:
            dv = dv.astype(F32) * (1.0 + ds_ref[...])
        acc[...] += _dot_tn(_bf(av), _bf(dv))

        @pl.when(m == n_m - 1)
        def _():
            o_ref[...] = _bf(acc[...])

    in_specs = [pl.BlockSpec((tm, tk), lambda i, j, m: (m, i)), pl.BlockSpec((tm, tn), lambda i, j, m: (m, j))]
    args = [a, d]
    for p in (p1, p2)[:n_p]:
        in_specs.append(pl.BlockSpec((1, tk), lambda i, j, m: (0, i)))
        args.append(p)
    if has_ds:
        in_specs.append(pl.BlockSpec((1, tn), lambda i, j, m: (0, j)))
        args.append(dscale)
    aliases = {}
    if has_buf:
        in_specs.append(pl.BlockSpec(memory_space=pl.ANY))
        args.append(out_buf)
        aliases = {len(args) - 1: 0}
    return pl.pallas_call(
        body, name=name, grid=(Kf // tk, N // tn, n_m), in_specs=in_specs,
        out_specs=pl.BlockSpec((None, tk, tn), lambda i, j, m: (0, i, col_block0 + j)),
        out_shape=jax.ShapeDtypeStruct(out_shape, BF16), input_output_aliases=aliases,
        scratch_shapes=[pltpu.VMEM((tk, tn), F32)], compiler_params=_params(3),
    )(*args)


ROW_BLOCK = 512
ACC_ROWS = 8


def _ln_stats(z):
    mu = jnp.mean(z, axis=-1, keepdims=True)
    zc = z - mu
    var = jnp.mean(zc * zc, axis=-1, keepdims=True)
    return zc, lax.rsqrt(var + LN_EPS)


def ln_fwd(x, y, gate, lng, lnb, *, name):
    S = x.shape[0]

    def body(x_ref, y_ref, gt_ref, g_ref, b_ref, o_ref):
        z = ALPHA * x_ref[...] + (1.0 + gt_ref[...]) * y_ref[...]
        zc, rstd = _ln_stats(z)
        o_ref[...] = (zc * rstd) * g_ref[...] + b_ref[...]

    row = pl.BlockSpec((ROW_BLOCK, D), lambda i: (i, 0))
    vec = pl.BlockSpec((1, D), lambda i: (0, 0))
    return pl.pallas_call(
        body, name=name, grid=(S // ROW_BLOCK,), in_specs=[row, row, vec, vec, vec], out_specs=row,
        out_shape=jax.ShapeDtypeStruct((S, D), F32), compiler_params=_params(1),
    )(x, y, gate, lng, lnb)


def _add_colsum(acc_ref, r, val):
    acc_ref[r:r + 1, :] += jnp.sum(val, axis=0, keepdims=True)


def ln_bwd(x_in, y, gate, lng, *, dout=None, nxt=None, name):
    S = x_in.shape[0]
    has_next = nxt is not None

    def body(*refs):
        if has_next:
            dzn_ref, dun_ref, scn_ref, b_ref = refs[:4]
            k = 4
        else:
            do_ref = refs[0]
            k = 1
        x_ref, y_ref, gt_ref, g_ref = refs[k:k + 4]
        dz_ref, acc_ref = refs[k + 4:]

        @pl.when(pl.program_id(0) == 0)
        def _():
            acc_ref[...] = jnp.zeros_like(acc_ref)

        yv = y_ref[...]
        z = ALPHA * x_ref[...] + (1.0 + gt_ref[...]) * yv
        zc, rstd = _ln_stats(z)
        xhat = zc * rstd
        if has_next:
            du = dun_ref[...]
            dout_v = ALPHA * dzn_ref[...] + du * (1.0 + scn_ref[...])
            _add_colsum(acc_ref, 3, du * (xhat * g_ref[...] + b_ref[...]))
            _add_colsum(acc_ref, 4, du)
        else:
            dout_v = do_ref[...]
        _add_colsum(acc_ref, 0, dout_v * xhat)
        _add_colsum(acc_ref, 1, dout_v)
        dxh = dout_v * g_ref[...]
        m1 = jnp.mean(dxh, axis=-1, keepdims=True)
        m2 = jnp.mean(dxh * xhat, axis=-1, keepdims=True)
        dz = rstd * (dxh - m1 - xhat * m2)
        _add_colsum(acc_ref, 2, dz * yv)
        dz_ref[...] = dz

    row = pl.BlockSpec((ROW_BLOCK, D), lambda i: (i, 0))
    vec = pl.BlockSpec((1, D), lambda i: (0, 0))
    if has_next:
        in_specs = [row, row, vec, vec]
        args = list(nxt)
    else:
        in_specs = [row]
        args = [dout]
    in_specs += [row, row, vec, vec]
    args += [x_in, y, gate, lng]
    return pl.pallas_call(
        body, name=name, grid=(S // ROW_BLOCK,), in_specs=in_specs,
        out_specs=[row, pl.BlockSpec((ACC_ROWS, D), lambda i: (0, 0))],
        out_shape=[jax.ShapeDtypeStruct((S, D), F32), jax.ShapeDtypeStruct((ACC_ROWS, D), F32)],
        compiler_params=_params(1),
    )(*args)


def combine_final(dz, du, sc, x_in, *, name):
    S = dz.shape[0]

    def body(dz_ref, du_ref, sc_ref, x_ref, dx_ref, acc_ref):
        @pl.when(pl.program_id(0) == 0)
        def _():
            acc_ref[...] = jnp.zeros_like(acc_ref)

        du_v = du_ref[...]
        dx_ref[...] = ALPHA * dz_ref[...] + du_v * (1.0 + sc_ref[...])
        _add_colsum(acc_ref, 3, du_v * x_ref[...])
        _add_colsum(acc_ref, 4, du_v)

    row = pl.BlockSpec((ROW_BLOCK, D), lambda i: (i, 0))
    vec = pl.BlockSpec((1, D), lambda i: (0, 0))
    return pl.pallas_call(
        body, name=name, grid=(S // ROW_BLOCK,), in_specs=[row, row, vec, row],
        out_specs=[row, pl.BlockSpec((ACC_ROWS, D), lambda i: (0, 0))],
        out_shape=[jax.ShapeDtypeStruct((S, D), F32), jax.ShapeDtypeStruct((ACC_ROWS, D), F32)],
        compiler_params=_params(1),
    )(dz, du, sc, x_in)


def loss_head(y, t, *, name):
    S = y.shape[0]

    def body(y_ref, t_ref, dy_ref, l_ref):
        @pl.when(pl.program_id(0) == 0)
        def _():
            l_ref[...] = jnp.zeros_like(l_ref)

        e = y_ref[...] - t_ref[...]
        dy_ref[...] = e * (1.0 / D)
        per_tok = jnp.sum(e * e, axis=1, keepdims=True) * (1.0 / D)
        l_ref[...] += 0.5 * jnp.sum(per_tok, axis=0, keepdims=True)

    row = pl.BlockSpec((ROW_BLOCK, D), lambda i: (i, 0))
    return pl.pallas_call(
        body, name=name, grid=(S // ROW_BLOCK,), in_specs=[row, row],
        out_specs=[row, pl.BlockSpec((8, 128), lambda i: (0, 0))],
        out_shape=[jax.ShapeDtypeStruct((S, D), F32), jax.ShapeDtypeStruct((8, 128), F32)],
        compiler_params=_params(1),
    )(y, t)


def _log_sigmoid(x):
    return jnp.minimum(x, 0.0) - jnp.log(1.0 + jnp.exp(-jnp.abs(x)))


def _silu(x):
    return x * (1.0 / (1.0 + jnp.exp(-x)))


def _cumsum_steps(x):
    row = lax.broadcasted_iota(jnp.int32, x.shape, 0)
    step = 1
    while step < x.shape[0]:
        x = x + jnp.where(row >= step, pltpu.roll(x, step, 0), 0.0)
        step *= 2
    return x


@jax.custom_vjp
def _cumsum_rows(x):
    return _cumsum_steps(x)


def _cumsum_rows_fwd(x):
    return _cumsum_steps(x), None


def _cumsum_rows_bwd(_, g):
    return (jnp.sum(g, axis=0, keepdims=True) - _cumsum_steps(g) + g,)


_cumsum_rows.defvjp(_cumsum_rows_fwd, _cumsum_rows_bwd)


def _gla_chunk(q, k, v, g, gk, s0t, w2p, bgk, gn):
    C = q.shape[0]
    row = lax.broadcasted_iota(jnp.int32, (C, C), 0)
    col = lax.broadcasted_iota(jnp.int32, (C, C), 1)
    lower = row >= col
    la = _log_sigmoid(_dot(_bf(gk), _bf(w2p)) + bgk) * GLA_TAU_INV
    outs, states = [], []
    for h in range(GLA_H):
        ks = slice(h * GLA_DKH, (h + 1) * GLA_DKH)
        vs = slice(h * GLA_DVH, (h + 1) * GLA_DVH)
        qh = q[:, ks] * (GLA_DKH ** -0.5)
        kh, vh, gh, lah, s0 = k[:, ks], v[:, vs], g[:, vs], la[:, ks], s0t[h]
        cum = _cumsum_rows(lah)
        e_pos = jnp.exp(cum)
        e_neg = jnp.exp(-cum)
        q_f = qh * e_pos
        a_f = _dot_nt(_bf(q_f), _bf(kh * e_neg))
        a_b = _dot_nt(_bf(qh * e_neg), _bf(kh * e_pos))
        att = jnp.where(lower, a_f, a_b)
        o = _dot(_bf(att), _bf(vh)) + _dot_nt(_bf(q_f), _bf(s0))
        tot = jnp.sum(lah, axis=0, keepdims=True)
        k_end = kh * jnp.exp(tot - cum)
        states.append(s0 * jnp.exp(tot) + _dot_tn(_bf(vh), _bf(k_end)))
        on = o * lax.rsqrt(jnp.mean(o * o, axis=-1, keepdims=True) + RMS_EPS) * gn[:, vs]
        outs.append(on * _silu(gh))
    return jnp.concatenate(outs, axis=1), tuple(states)


def _gla_split(p):
    return (p[:, 0:GLA_DK], p[:, GLA_DK:2 * GLA_DK], p[:, 2 * GLA_DK:2 * GLA_DK + GLA_DV],
            p[:, 2 * GLA_DK + GLA_DV:2 * GLA_DK + 2 * GLA_DV], p[:, 2 * GLA_DK + 2 * GLA_DV:GLA_INP])


def gla_fwd(proj, w2p, bgk, gn, *, comm=None, name):
    S = proj.shape[0]
    n_c = S // CHUNK
    n_s = n_c // GLA_SUB
    rows = GLA_SUB * CHUNK

    def body(p_ref, w_ref, b_ref, gn_ref, o_ref, st_ref, st):
        @pl.when(pl.program_id(0) == 0)
        def _():
            st[...] = jnp.zeros_like(st)

        s = tuple(st[h] for h in range(GLA_H))
        for u in range(GLA_SUB):
            sub = slice(u * CHUNK, (u + 1) * CHUNK)
            for h in range(GLA_H):
                st_ref[u, h] = s[h]
            og, s = _gla_chunk(*_gla_split(p_ref[sub, :]), s, w_ref[...], b_ref[...], gn_ref[...])
            o_ref[sub, :] = _bf(og)
        for h in range(GLA_H):
            st[h] = s[h]

    full = lambda shape: pl.BlockSpec(shape, lambda i: (0,) * len(shape))
    return _call_hosting(
        body, comm, first=lambda: pl.program_id(0) == 0, last=lambda: pl.program_id(0) == n_s - 1,
        name=name, grid=(n_s,),
        in_specs=[pl.BlockSpec((rows, GLA_INP), lambda i: (i, 0)), full((128, GLA_DK)), full((1, GLA_DK)),
                  full((1, GLA_DV))],
        out_specs=[pl.BlockSpec((rows, GLA_DV), lambda i: (i, 0)),
                   pl.BlockSpec((GLA_SUB, GLA_H, GLA_DVH, GLA_DKH), lambda i: (i, 0, 0, 0))],
        out_shape=[jax.ShapeDtypeStruct((S, GLA_DV), BF16),
                   jax.ShapeDtypeStruct((n_c, GLA_H, GLA_DVH, GLA_DKH), F32)],
        scratch_shapes=[pltpu.VMEM((GLA_H, GLA_DVH, GLA_DKH), F32)], compiler_params=_params(1),
        args=(proj, w2p, bgk, gn))


def gla_bwd(proj, dog, states, w2p, bgk, gn, *, comm=None, name):
    S = proj.shape[0]
    n_c = S // CHUNK
    n_s = n_c // GLA_SUB
    rows = GLA_SUB * CHUNK

    def body(p_ref, dog_ref, st_ref, w_ref, b_ref, gn_ref, dp_ref, dw_ref, db_ref, dgn_ref, ds_ref):
        @pl.when(pl.program_id(0) == 0)
        def _():
            ds_ref[...] = jnp.zeros_like(ds_ref)
            dw_ref[...] = jnp.zeros_like(dw_ref)
            db_ref[...] = jnp.zeros_like(db_ref)
            dgn_ref[...] = jnp.zeros_like(dgn_ref)

        ds = tuple(ds_ref[h] for h in range(GLA_H))
        for u in reversed(range(GLA_SUB)):
            sub = slice(u * CHUNK, (u + 1) * CHUNK)
            q, k, v, g, gk = _gla_split(p_ref[sub, :])
            s0 = tuple(st_ref[u, h] for h in range(GLA_H))
            _, vjp = jax.vjp(_gla_chunk, q, k, v, g, gk, s0, w_ref[...], b_ref[...], gn_ref[...])
            dq, dk, dv, dg, dgk, ds, dw, db, dgn = vjp((dog_ref[sub, :], ds))
            dp_ref[sub, 0:GLA_DK] = _bf(dq)
            dp_ref[sub, GLA_DK:2 * GLA_DK] = _bf(dk)
            dp_ref[sub, 2 * GLA_DK:2 * GLA_DK + GLA_DV] = _bf(dv)
            dp_ref[sub, 2 * GLA_DK + GLA_DV:2 * GLA_DK + 2 * GLA_DV] = _bf(dg)
            dp_ref[sub, 2 * GLA_DK + 2 * GLA_DV:GLA_INP] = _bf(dgk)
            dw_ref[...] += dw
            db_ref[...] += db
            dgn_ref[...] += dgn
        for h in range(GLA_H):
            ds_ref[h] = ds[h]

    full = lambda shape: pl.BlockSpec(shape, lambda i: (0,) * len(shape))
    rev = lambda i: (n_s - 1 - i, 0)
    return _call_hosting(
        body, comm, first=lambda: pl.program_id(0) == 0, last=lambda: pl.program_id(0) == n_s - 1,
        name=name, grid=(n_s,),
        in_specs=[pl.BlockSpec((rows, GLA_INP), rev), pl.BlockSpec((rows, GLA_DV), rev),
                  pl.BlockSpec((GLA_SUB, GLA_H, GLA_DVH, GLA_DKH), lambda i: (n_s - 1 - i, 0, 0, 0)),
                  full((128, GLA_DK)), full((1, GLA_DK)), full((1, GLA_DV))],
        out_specs=[pl.BlockSpec((rows, GLA_INP), rev), full((128, GLA_DK)), full((1, GLA_DK)), full((1, GLA_DV))],
        out_shape=[jax.ShapeDtypeStruct((S, GLA_INP), BF16), jax.ShapeDtypeStruct((128, GLA_DK), F32),
                   jax.ShapeDtypeStruct((1, GLA_DK), F32), jax.ShapeDtypeStruct((1, GLA_DV), F32)],
        scratch_shapes=[pltpu.VMEM((GLA_H, GLA_DVH, GLA_DKH), F32)], compiler_params=_params(1),
        args=(proj, dog, states, w2p, bgk, gn))


def _rel_index():
    t = np.arange(REL_TILE)[:, None]
    j = np.arange(REL_TILE)[None, :]
    tiles = []
    for m in REL_TILES:
        chunks = (REL_TILE // CHUNK) * m + j // CHUNK - t // CHUNK
        band = (chunks >= 0) & (chunks <= LEFT // CHUNK)
        dist = LEFT - REL_TILE * m + t - j
        tiles.append(np.where(band, np.minimum(dist, MAX_REL) + MAX_REL, N_REL))
    return jnp.asarray(np.stack(tiles).reshape(1, -1).astype(np.int32))


REL_BLOCK = 2048


def _one_hot(idx_row):
    return (lax.broadcasted_iota(jnp.int32, (REL_PAD, idx_row.shape[1]), 0) == idx_row).astype(F32)


def rel_bias_tiles(rel_pad, idx, *, name):
    E = idx.shape[1]

    def body(r_ref, i_ref, o_ref):
        o_ref[...] = jnp.dot(r_ref[...], _one_hot(i_ref[...]), precision=HIGHEST, preferred_element_type=F32)

    return pl.pallas_call(
        body, name=name, grid=(E // REL_BLOCK,),
        in_specs=[pl.BlockSpec((ATT_H, REL_PAD), lambda i: (0, 0)), pl.BlockSpec((1, REL_BLOCK), lambda i: (0, i))],
        out_specs=pl.BlockSpec((ATT_H, REL_BLOCK), lambda i: (0, i)),
        out_shape=jax.ShapeDtypeStruct((ATT_H, E), F32), compiler_params=_params(1),
    )(rel_pad, idx)


def rel_bias_grad(dtiles_flat, dclip, idx, *, name):
    E = idx.shape[1]
    n_steps = E // REL_BLOCK

    def body(d_ref, c_ref, i_ref, o_ref):
        @pl.when(pl.program_id(0) == 0)
        def _():
            o_ref[...] = jnp.zeros_like(o_ref)

        o_ref[...] += lax.dot_general(d_ref[...], _one_hot(i_ref[...]), (((1,), (1,)), ((), ())),
                                      precision=HIGHEST, preferred_element_type=F32)

        @pl.when(pl.program_id(0) == n_steps - 1)
        def _():
            at_clip = lax.broadcasted_iota(jnp.int32, (1, REL_PAD), 1) == 2 * MAX_REL
            o_ref[...] += jnp.where(at_clip, jnp.sum(c_ref[...], axis=1, keepdims=True), 0.0)

    return pl.pallas_call(
        body, name=name, grid=(n_steps,),
        in_specs=[pl.BlockSpec((ATT_H, REL_BLOCK), lambda i: (0, i)), pl.BlockSpec((ATT_H, 128), lambda i: (0, 0)),
                  pl.BlockSpec((1, REL_BLOCK), lambda i: (0, i))],
        out_specs=pl.BlockSpec((ATT_H, REL_PAD), lambda i: (0, 0)),
        out_shape=jax.ShapeDtypeStruct((ATT_H, REL_PAD), F32), compiler_params=_params(1),
    )(dtiles_flat, dclip, idx)


def _attn_bias(tiles, clip):
    const = jnp.broadcast_to(clip, (REL_TILE, REL_TILE))
    zero = jnp.zeros((REL_TILE, REL_TILE), F32)
    rows = []
    for qt in range(ATT_QB // REL_TILE):
        blocks = []
        for kt in range(ATT_KB // REL_TILE):
            m = kt - qt
            if m in REL_TILES:
                blocks.append(tiles[REL_TILES.index(m)])
            elif 0 <= m < REL_TILES[0]:
                blocks.append(const)
            else:
                blocks.append(zero)
        rows.append(jnp.concatenate(blocks, axis=1))
    return jnp.concatenate(rows, axis=0)


def _attn_bias_grad(ds, dt_ref, dc_ref, a):
    tile = lambda qt, kt: ds[qt * REL_TILE:(qt + 1) * REL_TILE, kt * REL_TILE:(kt + 1) * REL_TILE]
    const = None
    sums = [None] * len(REL_TILES)
    for qt in range(ATT_QB // REL_TILE):
        for kt in range(ATT_KB // REL_TILE):
            m = kt - qt
            if m in REL_TILES:
                n = REL_TILES.index(m)
                sums[n] = tile(qt, kt) if sums[n] is None else sums[n] + tile(qt, kt)
            elif 0 <= m < REL_TILES[0]:
                const = tile(qt, kt) if const is None else const + tile(qt, kt)
    for n, v in enumerate(sums):
        dt_ref[a, n] += v
    dc_ref[a] += jnp.sum(const, axis=0, keepdims=True)


def _attn_head_lanes():
    lane = lax.broadcasted_iota(jnp.int32, (1, 2 * ATT_HD), 1)
    return [(lane >= a * ATT_HD) & (lane < (a + 1) * ATT_HD) for a in range(2)]


def _attn_band_bias(tiles, clip):
    j = lax.broadcasted_iota(jnp.int32, (ATT_QB, ATT_KB), 1)
    t = lax.broadcasted_iota(jnp.int32, (ATT_QB, ATT_KB), 0)
    shift = CHUNK.bit_length() - 1
    chunks = lax.shift_right_logical(j, shift) - lax.shift_right_logical(t, shift)
    band = (chunks >= 0) & (chunks <= LEFT // CHUNK)
    return jnp.where(band, _attn_bias(tiles, clip), NEG_INF)


def _attn_exp(qa, kb, bias, key_bias):
    s = _dot_nt(qa, kb) + bias + key_bias
    e = jnp.exp(s - jnp.max(s, axis=-1, keepdims=True))
    return e, jnp.sum(e, axis=-1, keepdims=True)


def _attn_specs():
    n_hp = ATT_H // 2
    q_spec = pl.BlockSpec((ATT_QB, 128), lambda hp, g: (g, hp))

    def win(col0, back):
        return pl.BlockSpec((ATT_QB, 128), lambda hp, g: (jnp.maximum(g - back, 0), col0 + hp))

    kv_specs = [win(n_hp, 2), win(n_hp, 1), win(n_hp, 0), win(2 * n_hp, 2), win(2 * n_hp, 1), win(2 * n_hp, 0)]
    tiles_spec = pl.BlockSpec((2, len(REL_TILES), REL_TILE, REL_TILE), lambda hp, g: (hp, 0, 0, 0))
    clip_spec = pl.BlockSpec((2, 1, 128), lambda hp, g: (hp, 0, 0))
    return q_spec, kv_specs, tiles_spec, clip_spec


def _attn_window(refs, g):
    kb = jnp.concatenate([_bf(r[...]) for r in refs[0:3]], axis=0)
    vb = jnp.concatenate([_bf(r[...]) for r in refs[3:6]], axis=0)
    j = lax.broadcasted_iota(jnp.int32, (1, ATT_KB), 1)
    return kb, vb, jnp.where(j + (g - 2) * ATT_QB >= 0, 0.0, NEG_INF)


def attn_fwd(qkv, tiles, clip, *, comm=None, name):
    S = qkv.shape[0]
    q_spec, kv_specs, tiles_spec, clip_spec = _attn_specs()

    def body(q_ref, *rest):
        kv_refs, t_ref, c_ref, o_ref, bias = rest[:6], rest[6], rest[7], rest[8], rest[9]
        g = pl.program_id(1)

        @pl.when(g == 0)
        def _():
            for a in range(2):
                bias[a * ATT_QB:(a + 1) * ATT_QB, :] = _attn_band_bias(t_ref[a], c_ref[a])

        kb, vb, key_bias = _attn_window(kv_refs, g)
        q = q_ref[...].astype(F32)
        out = jnp.zeros((ATT_QB, 2 * ATT_HD), F32)
        for a, lanes in enumerate(_attn_head_lanes()):
            mf = lanes.astype(F32)
            e, l = _attn_exp(_bf(q * (mf * ATT_HD ** -0.5)), kb, bias[a * ATT_QB:(a + 1) * ATT_QB, :], key_bias)
            out = out + _dot(_bf(e), vb) * (mf * (1.0 / l))
        o_ref[...] = _bf(out)

    n_hp, n_g = ATT_H // 2, S // ATT_QB
    return _call_hosting(
        body, comm, first=lambda: (pl.program_id(0) == 0) & (pl.program_id(1) == 0),
        last=lambda: (pl.program_id(0) == n_hp - 1) & (pl.program_id(1) == n_g - 1),
        name=name, grid=(n_hp, n_g), in_specs=[q_spec] + kv_specs + [tiles_spec, clip_spec],
        out_specs=[q_spec], out_shape=[jax.ShapeDtypeStruct((S, D), BF16)],
        scratch_shapes=[pltpu.VMEM((2 * ATT_QB, ATT_KB), F32)], compiler_params=_params(2),
        args=(*([qkv] * 7), tiles, clip))


def attn_bwd(qkv, do, tiles, clip, *, comm=None, name):
    S = qkv.shape[0]
    q_spec, kv_specs, tiles_spec, clip_spec = _attn_specs()
    col_spec = pl.BlockSpec((S, 128), lambda hp, g: (0, hp))
    sum_spec = pl.BlockSpec((1, 128), lambda hp, g: (0, hp))
    n_g = S // ATT_QB

    def body(q_ref, *rest):
        kv_refs, t_ref, c_ref, do_ref = rest[:6], rest[6], rest[7], rest[8]
        dq_ref, dk_ref, dv_ref, dt_ref, dc_ref, sq_ref, sk_ref, sv_ref, bias = rest[9:]
        g = pl.program_id(1)

        @pl.when(g == 0)
        def _():
            for a in range(2):
                bias[a * ATT_QB:(a + 1) * ATT_QB, :] = _attn_band_bias(t_ref[a], c_ref[a])
            dk_ref[...] = jnp.zeros_like(dk_ref)
            dv_ref[...] = jnp.zeros_like(dv_ref)
            dt_ref[...] = jnp.zeros_like(dt_ref)
            dc_ref[...] = jnp.zeros_like(dc_ref)
            sq_ref[...] = jnp.zeros_like(sq_ref)

        kb, vb, key_bias = _attn_window(kv_refs, g)
        q = q_ref[...].astype(F32)
        do = do_ref[...]
        lanes = _attn_head_lanes()
        mf = [m.astype(F32) * ATT_HD ** -0.5 for m in lanes]
        qs = _bf(jnp.concatenate([q * m for m in mf], axis=0))
        dos = jnp.concatenate([jnp.where(m, do, jnp.zeros_like(do)) for m in lanes], axis=0)
        e, l = _attn_exp(qs, kb, bias[...], key_bias)
        p = e * (1.0 / l)
        dp = _dot_nt(dos, vb)
        ds = p * (dp - jnp.sum(p * dp, axis=-1, keepdims=True))
        ds_b = _bf(ds)
        dq2 = _dot(ds_b, kb)
        dq = dq2[:ATT_QB] * mf[0] + dq2[ATT_QB:] * mf[1]
        dkw = _dot_tn(ds_b, qs)
        dvw = _dot_tn(_bf(p), dos)
        for a in range(2):
            _attn_bias_grad(ds[a * ATT_QB:(a + 1) * ATT_QB], dt_ref, dc_ref, a)
        dq_ref[...] = _bf(dq)
        sq_ref[...] += jnp.sum(dq, axis=0, keepdims=True)
        for blk in range(3):
            src = g - 2 + blk

            @pl.when(src >= 0)
            def _(blk=blk, src=src):
                rows = pl.ds(pl.multiple_of(src * ATT_QB, ATT_QB), ATT_QB)
                dk_ref[rows, :] += dkw[blk * ATT_QB:(blk + 1) * ATT_QB]
                dv_ref[rows, :] += dvw[blk * ATT_QB:(blk + 1) * ATT_QB]

        @pl.when(g == n_g - 1)
        def _():
            sk_ref[...] = jnp.sum(dk_ref[...], axis=0, keepdims=True)
            sv_ref[...] = jnp.sum(dv_ref[...], axis=0, keepdims=True)

    n_hp, n_g = ATT_H // 2, S // ATT_QB
    return _call_hosting(
        body, comm, first=lambda: (pl.program_id(0) == 0) & (pl.program_id(1) == 0),
        last=lambda: (pl.program_id(0) == n_hp - 1) & (pl.program_id(1) == n_g - 1),
        name=name, grid=(n_hp, n_g),
        in_specs=[q_spec] + kv_specs + [tiles_spec, clip_spec, q_spec],
        out_specs=[q_spec, col_spec, col_spec, tiles_spec, clip_spec] + [sum_spec] * 3,
        out_shape=[jax.ShapeDtypeStruct((S, D), BF16)] + [jax.ShapeDtypeStruct((S, D), F32)] * 2
        + [jax.ShapeDtypeStruct((ATT_H, len(REL_TILES), REL_TILE, REL_TILE), F32),
           jax.ShapeDtypeStruct((ATT_H, 1, 128), F32)] + [jax.ShapeDtypeStruct((1, D), F32)] * 3,
        scratch_shapes=[pltpu.VMEM((2 * ATT_QB, ATT_KB), F32)], compiler_params=_params(2),
        args=(*([qkv] * 7), tiles, clip, do))


def mods_partial(c_all, w_ada, *, name):
    n_l, _, n_c = w_ada.shape

    def body(c_ref, w_ref, o_ref):
        o_ref[...] = _dot(_bf(_silu(c_ref[...])), _bf(w_ref[...]))

    return pl.pallas_call(
        body, name=name, grid=(n_l,),
        in_specs=[pl.BlockSpec((N_DEV, D), lambda l: (0, 0)), pl.BlockSpec((None, D, n_c), lambda l: (l, 0, 0))],
        out_specs=pl.BlockSpec((None, N_DEV, n_c), lambda l: (l, 0, 0)),
        out_shape=jax.ShapeDtypeStruct((n_l, N_DEV, n_c), F32), compiler_params=_params(1),
    )(c_all, w_ada)


def w_ada_grad(c_all, dm, *, name):
    n_l, _, n_c = dm.shape

    def body(c_ref, d_ref, o_ref):
        o_ref[...] = lax.dot_general(_silu(c_ref[...]), d_ref[...], (((0,), (0,)), ((), ())),
                                     precision=HIGHEST, preferred_element_type=F32)

    return pl.pallas_call(
        body, name=name, grid=(n_l,),
        in_specs=[pl.BlockSpec((N_DEV, D), lambda l: (0, 0)), pl.BlockSpec((None, N_DEV, n_c), lambda l: (l, 0, 0))],
        out_specs=pl.BlockSpec((None, D, n_c), lambda l: (l, 0, 0)),
        out_shape=jax.ShapeDtypeStruct((n_l, D, n_c), F32), compiler_params=_params(1),
    )(c_all, dm)


def adamw(w, m, v, gparts, *, block_rows, name):
    R, C = w.shape
    n = gparts.shape[0]

    def body(w_ref, m_ref, v_ref, g_ref, go_ref, d_ref, mo_ref, vo_ref):
        g = g_ref[0].astype(F32)
        for k in range(1, n):
            g = g + g_ref[k].astype(F32)
        m_new = ADAM_B1 * m_ref[...] + (1.0 - ADAM_B1) * g
        v_new = ADAM_B2 * v_ref[...] + (1.0 - ADAM_B2) * (g * g)
        m_hat = m_new / (1.0 - ADAM_B1 ** ADAM_STEP)
        v_hat = v_new / (1.0 - ADAM_B2 ** ADAM_STEP)
        go_ref[...] = g
        d_ref[...] = -ADAM_LR * (m_hat / (jnp.sqrt(v_hat) + ADAM_EPS) + ADAM_WD * w_ref[...])
        mo_ref[...] = m_new
        vo_ref[...] = v_new

    blk = pl.BlockSpec((block_rows, C), lambda i: (i, 0))
    return pl.pallas_call(
        body, name=name, grid=(R // block_rows,),
        in_specs=[blk, blk, blk, pl.BlockSpec((n, block_rows, C), lambda i: (0, i, 0))],
        out_specs=[blk] * 4, out_shape=[jax.ShapeDtypeStruct((R, C), F32)] * 4, compiler_params=_params(1),
    )(w, m, v, gparts)


def adamw_nd(w, m, v, gparts, *, name):
    shape = w.shape
    two = (int(np.prod(shape[:-1])), shape[-1])
    rows = two[0]
    block_rows = rows
    for cand in (512, 256):
        if rows > cand and rows % cand == 0:
            block_rows = cand
            break
    outs = adamw(w.reshape(two), m.reshape(two), v.reshape(two), gparts.reshape((gparts.shape[0],) + two),
                 block_rows=block_rows, name=name)
    return [o.reshape(shape) for o in outs]


def sum_parts(parts, *, name):
    n, R, C = parts.shape

    def body(p_ref, o_ref):
        acc = p_ref[0]
        for k in range(1, n):
            acc = acc + p_ref[k]
        o_ref[...] = acc

    return pl.pallas_call(
        body, name=name, in_specs=[pl.BlockSpec((n, R, C), lambda: (0, 0, 0))],
        out_specs=pl.BlockSpec((R, C), lambda: (0, 0)), out_shape=jax.ShapeDtypeStruct((R, C), F32),
        compiler_params=pltpu.CompilerParams(vmem_limit_bytes=VMEM_LIMIT),
    )(parts)


def _my_place():
    return lax.axis_index("x"), lax.axis_index("y"), lax.axis_index("c")


def _full_shape(kind, shard):
    n_l, rows, cols = shard
    return {"col": (n_l, rows, N_DEV * cols), "row": (n_l, N_DEV * rows, cols), "stk": (N_DEV, n_l, rows, cols)}[kind]


def _slab(ref, kind, dev, shard):
    _, rows, cols = shard
    if kind == "col":
        return ref.at[:, :, pl.ds(pl.multiple_of(dev * cols, 128), cols)]
    if kind == "row":
        return ref.at[:, pl.ds(pl.multiple_of(dev * rows, 8), rows), :]
    return ref.at[dev]


def all_gather(x_shard, *, name):
    m_per, n = x_shard.shape

    def body(x_ref, out_ref, send_sems, recv_sems, local_sem):
        x, y, c = _my_place()
        me, sibling = (x, y, c), (x, y, 1 - c)
        chips = [(1 - x, y), (x, 1 - y), (1 - x, 1 - y)]

        def rows(px, py, pc):
            return out_ref.at[pl.ds((4 * px + 2 * py + pc) * m_per, m_per), :]

        def copy(k, block, to, src=None):
            return pltpu.make_async_remote_copy(
                src_ref=rows(*block) if src is None else src, dst_ref=rows(*block),
                send_sem=send_sems.at[k], recv_sem=recv_sems.at[k], device_id=to, device_id_type=MESH)

        mine = pltpu.make_async_copy(x_ref, rows(*me), local_sem)
        mine.start()
        first = [copy(0, me, sibling, src=x_ref)]
        first += [copy(1 + j, me, (*chip, c), src=x_ref) for j, chip in enumerate(chips)]
        for cp in first:
            cp.start()
        passed = [copy(4 + j, (*chip, c), sibling) for j, chip in enumerate(chips)]
        for j, chip in enumerate(chips):
            copy(1 + j, (*chip, c), me).wait_recv()
            passed[j].start()
        copy(0, sibling, me).wait_recv()
        for j, chip in enumerate(chips):
            copy(4 + j, (*chip, 1 - c), me).wait_recv()
        for cp in first + passed:
            cp.wait_send()
        mine.wait()

    return pl.pallas_call(
        body, name=name, out_shape=jax.ShapeDtypeStruct((N_DEV * m_per, n), x_shard.dtype),
        in_specs=[pl.BlockSpec(memory_space=pltpu.VMEM)], out_specs=pl.BlockSpec(memory_space=pltpu.VMEM),
        scratch_shapes=[pltpu.SemaphoreType.DMA((7,)), pltpu.SemaphoreType.DMA((7,)), pltpu.SemaphoreType.DMA],
        compiler_params=pltpu.CompilerParams(vmem_limit_bytes=VMEM_LIMIT),
    )(x_shard)


def gather_plan(shards, kinds, layers):
    n_t = len(shards)
    shapes = [(1,) + tuple(s.shape[1:]) for s in shards]

    def copies(x_refs, out_refs, sems):
        send_sems, recv_sems, local_sems = sems
        x, y, c = _my_place()
        me, sibling = (x, y, c), (x, y, 1 - c)
        chips = [(1 - x, y), (x, 1 - y), (1 - x, 1 - y)]
        own = [x_refs[t].at[pl.ds(layers[t], 1)] for t in range(n_t)]

        def slab(t, px, py, pc):
            return _slab(out_refs[t], kinds[t], 4 * px + 2 * py + pc, shapes[t])

        def copy(t, k, block, to, src=None):
            return pltpu.make_async_remote_copy(
                src_ref=slab(t, *block) if src is None else src, dst_ref=slab(t, *block),
                send_sem=send_sems.at[7 * t + k], recv_sem=recv_sems.at[7 * t + k], device_id=to,
                device_id_type=MESH)

        mine = [pltpu.make_async_copy(own[t], slab(t, *me), local_sems.at[t]) for t in range(n_t)]
        sends = []
        for t in range(n_t):
            sends.append(copy(t, 0, me, sibling, src=own[t]))
            sends += [copy(t, 1 + j, me, (*chip, c), src=own[t]) for j, chip in enumerate(chips)]
        return mine, sends, copy, me, sibling, chips, c

    def first(x_refs, out_refs, sems):
        mine, sends = copies(x_refs, out_refs, sems)[:2]
        for cp in mine + sends:
            cp.start()

    def last(x_refs, out_refs, sems):
        mine, sends, copy, me, sibling, chips, c = copies(x_refs, out_refs, sems)
        passed = []
        for j, chip in enumerate(chips):
            for t in range(n_t):
                copy(t, 1 + j, (*chip, c), me).wait_recv()
                passed.append(copy(t, 4 + j, (*chip, c), sibling))
                passed[-1].start()
        for t in range(n_t):
            copy(t, 0, sibling, me).wait_recv()
        for j, chip in enumerate(chips):
            for t in range(n_t):
                copy(t, 4 + j, (*chip, 1 - c), me).wait_recv()
        for cp in sends + passed:
            cp.wait_send()
        for cp in mine:
            cp.wait()

    return Hosted(
        list(shards), [jax.ShapeDtypeStruct(_full_shape(k, shp), s.dtype) for k, shp, s in zip(kinds, shapes, shards)],
        [pltpu.SemaphoreType.DMA((7 * n_t,)), pltpu.SemaphoreType.DMA((7 * n_t,)), pltpu.SemaphoreType.DMA((n_t,))],
        first, last)


def scatter_plan(grads, kinds, shapes):
    n_t = len(grads)

    def copies(g_refs, out_refs, sems):
        send_sems, recv_sems, local_sems = sems
        x, y, c = _my_place()
        me = 4 * x + 2 * y + c
        local = [pltpu.make_async_copy(_slab(g_refs[t], kinds[t], me, shapes[t]), out_refs[t].at[me],
                                       local_sems.at[t]) for t in range(n_t)]
        remote = []
        for t in range(n_t):
            for r in range(1, N_DEV):
                px = 1 - x if r & 4 else x
                py = 1 - y if r & 2 else y
                pc = 1 - c if r & 1 else c
                remote.append(pltpu.make_async_remote_copy(
                    src_ref=_slab(g_refs[t], kinds[t], 4 * px + 2 * py + pc, shapes[t]), dst_ref=out_refs[t].at[me],
                    send_sem=send_sems.at[7 * t + r - 1], recv_sem=recv_sems.at[7 * t + r - 1],
                    device_id=(px, py, pc), device_id_type=MESH))
        return local, remote

    def first(g_refs, out_refs, sems):
        local, remote = copies(g_refs, out_refs, sems)
        for cp in local + remote:
            cp.start()

    def last(g_refs, out_refs, sems):
        local, remote = copies(g_refs, out_refs, sems)
        for cp in remote + local:
            cp.wait()

    return Hosted(
        list(grads), [jax.ShapeDtypeStruct((N_DEV,) + tuple(s), BF16) for s in shapes],
        [pltpu.SemaphoreType.DMA((7 * n_t,)), pltpu.SemaphoreType.DMA((7 * n_t,)), pltpu.SemaphoreType.DMA((n_t,))],
        first, last)


def adamw_layer(w, m, v, parts, layer, bufs, *, name):
    n_l, rows, cols = w.shape
    n = parts.shape[0]
    tr = min(rows, 256)

    def body(w_ref, m_ref, v_ref, g_ref, *rest):
        go_ref, d_ref, mo_ref, vo_ref = rest[-4:]
        g = g_ref[0].astype(F32)
        for k in range(1, n):
            g = g + g_ref[k].astype(F32)
        m_new = ADAM_B1 * m_ref[...] + (1.0 - ADAM_B1) * g
        v_new = ADAM_B2 * v_ref[...] + (1.0 - ADAM_B2) * (g * g)
        m_hat = m_new / (1.0 - ADAM_B1 ** ADAM_STEP)
        v_hat = v_new / (1.0 - ADAM_B2 ** ADAM_STEP)
        go_ref[...] = g
        d_ref[...] = -ADAM_LR * (m_hat / (jnp.sqrt(v_hat) + ADAM_EPS) + ADAM_WD * w_ref[...])
        mo_ref[...] = m_new
        vo_ref[...] = v_new

    blk = pl.BlockSpec((None, tr, cols), lambda i: (layer, i, 0))
    in_specs = [blk, blk, blk, pl.BlockSpec((n, None, tr, cols), lambda i: (0, 0, i, 0))]
    args = [w, m, v, parts]
    aliases = {}
    if bufs is not None:
        in_specs += [pl.BlockSpec(memory_space=pl.ANY)] * 4
        args += list(bufs)
        aliases = {4 + k: k for k in range(4)}
    return pl.pallas_call(
        body, name=name, grid=(rows // tr,), in_specs=in_specs, out_specs=[blk] * 4,
        out_shape=[jax.ShapeDtypeStruct((n_l, rows, cols), F32)] * 4, input_output_aliases=aliases,
        compiler_params=_params(1),
    )(*args)


BIG =("gla_w_in", "gla_w_out", "att_w_in", "att_w_out", "ff_w1", "ff_w2")
KIND = {"gla_w_in": "stk", "gla_w_out": "row", "att_w_in": "col", "att_w_out": "row", "ff_w1": "col", "ff_w2": "row"}


def _pack_small(arrs):
    parts = []
    for a in arrs:
        f = a.reshape(-1)
        parts.append(jnp.pad(f, (0, -f.shape[0] % 128)))
    flat = jnp.concatenate(parts)
    flat = jnp.pad(flat, (0, -flat.shape[0] % 1024))
    return flat.reshape(-1, 128)


def _unpack_small(packed, shapes):
    flat = packed.reshape(packed.shape[:-2] + (-1,))
    out, off = [], 0
    for shp in shapes:
        n = int(np.prod(shp))
        out.append(flat[..., off:off + n].reshape(packed.shape[:-2] + tuple(shp)))
        off += n + (-n % 128)
    return out


def _vec(a):
    return a.reshape(1, -1)


def _layer_weights(i):
    mixer = "gla" if i % 2 == 0 else "att"
    return [(f"{mixer}_w_in", i // 2), (f"{mixer}_w_out", i // 2), ("ff_w1", i), ("ff_w2", i)]


def _trunk(x, target, mods, sm, w, m, v):
    shard_bf = {n: w[n].astype(BF16) for n in BIG}

    def gather_of(names):
        return gather_plan([shard_bf[n] for n, _ in names], [KIND[n] for n, _ in names], [l for _, l in names])

    def gather_layer(i):
        return _layer_weights(i), gather_of(_layer_weights(i))

    wts = {}

    def keep_gathered(names, arrays):
        for (n, l), a in zip(names, arrays):
            if KIND[n] == "stk":
                a = a.transpose(1, 2, 0, 3).reshape(1, D, GLA_IN)
                a = jnp.pad(a, ((0, 0), (0, 0), (0, GLA_INP - GLA_IN)))
            wts[n, l] = a

    first_names, rest_names = _layer_weights(0)[:1], _layer_weights(0)[1:]
    keep_gathered(first_names, run_hosted(gather_of(first_names), name="gather_first"))

    rel_idx = _rel_index()
    saved = []
    for i in range(DEPTH):
        sh1, sc1, g1, sh2, sc2, g2 = [mods[i, k:k + 1] for k in range(6)]
        rec = {"x0": x}
        j = i // 2
        nxt_names, nxt_plan = gather_layer(i + 1) if i + 1 < DEPTH else (None, None)
        if i % 2 == 0:
            w2p = jnp.pad(sm["gla_w_gk2"][j], ((0, 128 - GLA_RANK), (0, 0)))
            bgk, gn = _vec(sm["gla_b_gk"][j]), _vec(sm["gla_g_norm"][j])
            if i == 0:
                proj, got0 = mm_nn(x, wts["gla_w_in", j], 0, pro="mod", p1=sc1, p2=sh1, tm=512, tn=GLA_INP,
                                   comm=gather_of(rest_names), name=f"gla_proj_{i}")
                keep_gathered(rest_names, got0)
            else:
                proj = mm_nn(x, wts["gla_w_in", j], 0, pro="mod", p1=sc1, p2=sh1, tm=512, tn=GLA_INP,
                             name=f"gla_proj_{i}")
            (og, states), got = gla_fwd(proj, w2p, bgk, gn, comm=nxt_plan, name=f"gla_core_{i}")
            y = mm_nn(og, wts["gla_w_out", j], 0, tm=1024, tn=1024, name=f"gla_out_{i}")
            rec.update(kind="gla", j=j, w2p=w2p, bgk=bgk, gn=gn, proj=proj, og=og, states=states)
        else:
            rel = sm["att_rel_bias"][j]
            rel_pad = jnp.pad(rel, ((0, 0), (0, REL_PAD - N_REL)), constant_values=NEG_INF)
            tiles = rel_bias_tiles(rel_pad, rel_idx, name=f"att_bias_{i}")
            tiles = tiles.reshape(ATT_H, len(REL_TILES), REL_TILE, REL_TILE)
            clip = jnp.broadcast_to(rel[:, 2 * MAX_REL][:, None, None], (ATT_H, 1, 128))
            qkv = mm_nn(x, wts["att_w_in", j], 0, pro="mod", p1=sc1, p2=sh1, bias=_vec(sm["att_b_in"][j]),
                        out_dtype=BF16, tm=512, tn=3 * D, name=f"att_proj_{i}")
            (o,), got = attn_fwd(qkv, tiles, clip, comm=nxt_plan, name=f"att_core_{i}")
            y = mm_nn(o, wts["att_w_out", j], 0, tm=1024, tn=1024, name=f"att_out_{i}")
            rec.update(kind="att", j=j, tiles=tiles, clip=clip, qkv=qkv, o=o)
        if nxt_plan is not None:
            keep_gathered(nxt_names, got)
        x1 = ln_fwd(x, y, g1, _vec(sm["ln_g"][i, 0]), _vec(sm["ln_b"][i, 0]), name=f"ln_mix_{i}")
        h = mm_nn(x1, wts["ff_w1", i], 0, pro="mod", p1=sc2, p2=sh2, out_dtype=BF16, tm=512, tn=D_FF,
                  name=f"ff_up_{i}")
        y2 = mm_nn(h, wts["ff_w2", i], 0, pro="relu2", tm=512, tn=D, name=f"ff_down_{i}")
        x2 = ln_fwd(x1, y2, g2, _vec(sm["ln_g"][i, 1]), _vec(sm["ln_b"][i, 1]), name=f"ln_ff_{i}")
        rec.update(y=y, x1=x1, h=h, y2=y2)
        saved.append(rec)
        x = x2

    dy, loss = loss_head(x, target, name="loss_head")

    gw = {}

    def wgrad(weight, layer, a, d, *, tn, tk=1024, tm=512, col_block0=0, **kw):
        gw[weight, layer] = mm_tn(a, d, tk=tk, tn=tn, tm=tm, out_buf=gw.get((weight, layer)),
                                  out_shape=wts[weight, layer].shape, col_block0=col_block0, **kw)

    def scatter_layer(units):
        grads = []
        for n, l in units:
            g = gw[n, l]
            if KIND[n] == "stk":
                g = g[:, :, :GLA_IN].reshape(1, D, N_DEV, GLA_IN // N_DEV).transpose(2, 0, 1, 3)
            grads.append(g)
        return scatter_plan(grads, [KIND[n] for n, _ in units], [(1,) + tuple(w[n].shape[1:]) for n, _ in units])

    results = {n: None for n in BIG}

    def update(units, parts):
        for (n, l), p in zip(units, parts):
            results[n] = adamw_layer(w[n], m[n], v[n], p, l, results[n], name=f"adamw_{n}_{l}")

    gs = {"ln_g": [[None, None] for _ in range(DEPTH)], "ln_b": [[None, None] for _ in range(DEPTH)],
          "gla_w_gk2": [None] * 2, "gla_b_gk": [None] * 2, "gla_g_norm": [None] * 2, "att_b_in": [None] * 2,
          "att_rel_bias": [None] * 2}
    dmods = [[None] * 6 for _ in range(DEPTH)]
    nxt = None
    nxt_slot = None
    for i in reversed(range(DEPTH)):
        rec = saved[i]
        sh1, sc1, g1, sh2, sc2, g2 = [mods[i, k:k + 1] for k in range(6)]
        x0, x1 = rec["x0"], rec["x1"]
        if nxt is None:
            dz2, acc = ln_bwd(x1, rec["y2"], g2, _vec(sm["ln_g"][i, 1]), dout=dy, name=f"ln_ff_bwd_{i}")
        else:
            dz2, acc = ln_bwd(x1, rec["y2"], g2, _vec(sm["ln_g"][i, 1]), nxt=nxt[:3] + (_vec(sm["ln_b"][i, 1]),),
                              name=f"ln_ff_bwd_{i}")
            dmods[nxt_slot[0]][nxt_slot[1]] = acc[3]
            dmods[nxt_slot[0]][nxt_slot[2]] = acc[4]
        gs["ln_g"][i][1], gs["ln_b"][i][1], dmods[i][5] = acc[0], acc[1], acc[2]
        dh = mm_nt([dz2], wts["ff_w2", i], 0, pro="scale", p1=g2, epi_h=rec["h"], out_dtype=BF16, tm=512, tn=2048,
                   name=f"ff_down_bwd_{i}")
        wgrad("ff_w2", i, rec["h"], dz2, pro="relu2", dscale=g2, tk=2048, tn=1024, name=f"ff_w2_grad_{i}")
        du2 = mm_nt([dh], wts["ff_w1", i], 0, tm=1024, tn=1024, name=f"ff_up_bwd_{i}")
        wgrad("ff_w1", i, x1, dh, pro="mod", p1=sc2, p2=sh2, tn=2048, name=f"ff_w1_grad_{i}")
        dz1, acc = ln_bwd(x0, rec["y"], g1, _vec(sm["ln_g"][i, 0]), nxt=(dz2, du2, sc2, _vec(sm["ln_b"][i, 0])),
                          name=f"ln_mix_bwd_{i}")
        dmods[i][4], dmods[i][3] = acc[3], acc[4]
        gs["ln_g"][i][0], gs["ln_b"][i][0], dmods[i][2] = acc[0], acc[1], acc[2]
        j = rec["j"]
        w_in, w_out = _layer_weights(i)[:2]
        if rec["kind"] == "gla":
            dog = mm_nt([dz1], wts[w_out], 0, pro="scale", p1=g1, tm=1024, tn=1024, name=f"gla_out_bwd_{i}")
            wgrad(*w_out, rec["og"], dz1, dscale=g1, tn=1024, name=f"gla_wout_grad_{i}")
        else:
            do = mm_nt([dz1], wts[w_out], 0, pro="scale", p1=g1, out_dtype=BF16, tm=1024, tn=1024,
                       name=f"att_out_bwd_{i}")
            wgrad(*w_out, rec["o"], dz1, dscale=g1, tn=1024, name=f"att_wout_grad_{i}")
        units = [("ff_w1", i), ("ff_w2", i), w_out] + ([_layer_weights(i + 1)[0]] if i + 1 < DEPTH else [])
        plan = scatter_layer(units)
        if rec["kind"] == "gla":
            (dproj, dw2p, dbgk, dgn), parts = gla_bwd(rec["proj"], dog, rec["states"], rec["w2p"], rec["bgk"],
                                                      rec["gn"], comm=plan, name=f"gla_core_bwd_{i}")
            gs["gla_w_gk2"][j], gs["gla_b_gk"][j], gs["gla_g_norm"][j] = dw2p[:GLA_RANK], dbgk[0], dgn[0]
            wgrad(*w_in, x0, dproj, pro="mod", p1=sc1, p2=sh1, tk=512, tn=GLA_INP, name=f"gla_win_grad_{i}")
            if i == 0:
                du1, last_parts = mm_nt([dproj], wts[w_in], 0, tm=1024, tn=1024, comm=scatter_layer([w_in]),
                                        name=f"gla_proj_bwd_{i}")
                update([w_in], last_parts)
            else:
                du1 = mm_nt([dproj], wts[w_in], 0, tm=1024, tn=1024, name=f"gla_proj_bwd_{i}")
        else:
            (dq, dk, dv, dtiles, dclip, sq, sk, sv), parts = attn_bwd(rec["qkv"], do, rec["tiles"], rec["clip"],
                                                                      comm=plan, name=f"att_core_bwd_{i}")
            drel = rel_bias_grad(dtiles.reshape(ATT_H, -1), dclip.reshape(ATT_H, 128), rel_idx,
                                 name=f"att_bias_grad_{i}")
            gs["att_rel_bias"][j] = drel[:, :N_REL]
            gs["att_b_in"][j] = jnp.concatenate([sq[0], sk[0], sv[0]])
            du1 = mm_nt([dq, dk, dv], wts[w_in], 0, tm=512, tn=1024, name=f"att_proj_bwd_{i}")
            for n, t in enumerate((dq, dk, dv)):
                wgrad(*w_in, x0, t, pro="mod", p1=sc1, p2=sh1, tn=1024, col_block0=n, name=f"att_win_grad_{i}_{n}")
        update(units, parts)
        nxt = (dz1, du1, sc1, x0)
        nxt_slot = (i, 1, 0)
    dx, acc = combine_final(nxt[0], nxt[1], nxt[2], nxt[3], name="grad_x")
    dmods[0][1], dmods[0][0] = acc[3], acc[4]
    dmods = jnp.stack([jnp.stack(r) for r in dmods])
    gs = {k: jnp.stack([jnp.stack(r) if isinstance(r, list) else r for r in v]) for k, v in gs.items()}
    return loss, dx, dmods, gs, results


WEIGHTS = ("w_ada", "b_ada", "ln_g", "ln_b", "gla_w_in", "gla_w_gk2", "gla_b_gk", "gla_g_norm", "gla_w_out",
           "att_w_in", "att_b_in", "att_rel_bias", "att_w_out", "ff_w1", "ff_w2")
SMALL_SHARDED = {"ln_g": 2, "ln_b": 2, "gla_w_gk2": 2, "gla_g_norm": 2, "att_b_in": 1}
SMALL_GRADS = ("ln_g", "ln_b", "gla_w_gk2", "gla_b_gk", "gla_g_norm", "att_b_in", "att_rel_bias")


def kernel(x, c, w_ada, b_ada, ln_g, ln_b, gla_w_in, gla_w_gk2, gla_b_gk, gla_g_norm, gla_w_out, att_w_in, att_b_in, att_rel_bias, att_w_out, ff_w1, ff_w2, loss_target, m_w_ada, m_b_ada, m_ln_g, m_ln_b, m_gla_w_in, m_gla_w_gk2, m_gla_b_gk, m_gla_g_norm, m_gla_w_out, m_att_w_in, m_att_b_in, m_att_rel_bias, m_att_w_out, m_ff_w1, m_ff_w2, v_w_ada, v_b_ada, v_ln_g, v_ln_b, v_gla_w_in, v_gla_w_gk2, v_gla_b_gk, v_gla_g_norm, v_gla_w_out, v_att_w_in, v_att_b_in, v_att_rel_bias, v_att_w_out, v_ff_w1, v_ff_w2):
    w = dict(w_ada=w_ada, b_ada=b_ada, ln_g=ln_g, ln_b=ln_b, gla_w_in=gla_w_in, gla_w_gk2=gla_w_gk2,
             gla_b_gk=gla_b_gk, gla_g_norm=gla_g_norm, gla_w_out=gla_w_out, att_w_in=att_w_in, att_b_in=att_b_in,
             att_rel_bias=att_rel_bias, att_w_out=att_w_out, ff_w1=ff_w1, ff_w2=ff_w2)
    m = dict(w_ada=m_w_ada, b_ada=m_b_ada, ln_g=m_ln_g, ln_b=m_ln_b, gla_w_in=m_gla_w_in, gla_w_gk2=m_gla_w_gk2,
             gla_b_gk=m_gla_b_gk, gla_g_norm=m_gla_g_norm, gla_w_out=m_gla_w_out, att_w_in=m_att_w_in,
             att_b_in=m_att_b_in, att_rel_bias=m_att_rel_bias, att_w_out=m_att_w_out, ff_w1=m_ff_w1, ff_w2=m_ff_w2)
    v = dict(w_ada=v_w_ada, b_ada=v_b_ada, ln_g=v_ln_g, ln_b=v_ln_b, gla_w_in=v_gla_w_in, gla_w_gk2=v_gla_w_gk2,
             gla_b_gk=v_gla_b_gk, gla_g_norm=v_gla_g_norm, gla_w_out=v_gla_w_out, att_w_in=v_att_w_in,
             att_b_in=v_att_b_in, att_rel_bias=v_att_rel_bias, att_w_out=v_att_w_out, ff_w1=v_ff_w1, ff_w2=v_ff_w2)
    xi, yi, ci = _my_place()
    me = 4 * xi + 2 * yi + ci

    small_names = tuple(SMALL_SHARDED)
    small_in = _pack_small([c] + [w[n] for n in small_names])
    small_all = all_gather(small_in, name="gather_small").reshape(N_DEV, -1, 128)
    parts = _unpack_small(small_all, [c.shape] + [w[n].shape for n in small_names])
    c_all = parts[0].reshape(N_DEV, D)
    sm = {"gla_b_gk": gla_b_gk, "att_rel_bias": att_rel_bias}
    for n, p in zip(small_names, parts[1:]):
        ax = SMALL_SHARDED[n]
        sm[n] = jnp.moveaxis(p, 0, ax).reshape(p.shape[1:ax + 1] + (N_DEV * p.shape[ax + 1],) + p.shape[ax + 2:])

    n_ada = w_ada.shape[2]
    mp = mods_partial(c_all, w_ada, name="mods_partial")
    mp_all = all_gather(mp.reshape(DEPTH * N_DEV, n_ada), name="gather_mods")
    mp_all = mp_all.reshape(N_DEV, DEPTH, N_DEV, n_ada)
    mods = lax.dynamic_index_in_dim(mp_all, me, axis=2, keepdims=False)
    mods = mods.transpose(1, 0, 2).reshape(DEPTH, 6 * D) + b_ada
    mods = mods.reshape(DEPTH, 6, D)

    loss, dx, dmods, gs, results = _trunk(x.reshape(x.shape[1:]), loss_target.reshape(x.shape[1:]), mods, sm, w, m, v)
    loss = lax.psum(loss[0, 0], ("x", "y", "c"))

    dm_flat = dmods.reshape(DEPTH, 6 * D)
    small_g = [dm_flat] + [gs[n].reshape(sm[n].shape) for n in SMALL_GRADS]
    small_shapes = [a.shape for a in small_g]
    sg_all = all_gather(_pack_small(small_g), name="gather_small_grads").reshape(N_DEV, -1, 128)
    summed = _unpack_small(sum_parts(sg_all, name="sum_small_grads"), small_shapes)
    g_full = dict(zip(("b_ada",) + SMALL_GRADS, summed))
    dm_all = _unpack_small(sg_all, small_shapes)[0]
    dm_mine = lax.dynamic_slice_in_dim(dm_all, me * n_ada, n_ada, axis=2).transpose(1, 0, 2)
    g_w_ada = w_ada_grad(c_all, dm_mine, name="w_ada_grad")

    results["w_ada"] = adamw_nd(w_ada, m_w_ada, v_w_ada, g_w_ada[None], name="adamw_w_ada")
    for n in ("b_ada",) + SMALL_GRADS:
        g = g_full[n]
        if n in SMALL_SHARDED:
            ax = SMALL_SHARDED[n]
            width = w[n].shape[ax]
            g = lax.dynamic_slice_in_dim(g, me * width, width, axis=ax)
        results[n] = adamw_nd(w[n], m[n], v[n], g[None], name=f"adamw_{n}")

    out = [loss, dx[None]]
    for k in range(4):
        out += [results[n][k] for n in WEIGHTS]
    return tuple(out)
```

```python
import numpy as np
import jax
import jax.numpy as jnp
from jax import lax
from jax.experimental import pallas as pl
from jax.experimental.pallas import tpu as pltpu

F32 = jnp.float32
BF16 = jnp.bfloat16
HIGHEST = lax.Precision.HIGHEST
MESH = pl.DeviceIdType.MESH

N_DEV = 8
D = 1024
DEPTH = 4
CHUNK = 64
ALPHA = (2.0 * DEPTH) ** 0.25
LN_EPS = 1e-5
RMS_EPS = 1e-6
NEG_INF = -1e30

GLA_H = 4
GLA_DKH = 128
GLA_DVH = 256
GLA_DK = GLA_H * GLA_DKH
GLA_DV = GLA_H * GLA_DVH
GLA_RANK = 16
GLA_IN = 2 * GLA_DK + 2 * GLA_DV + GLA_RANK
GLA_INP = 3200
GLA_TAU_INV = 1.0 / 16.0
GLA_SUB = 2

ATT_H = 16
ATT_HD = 64
ATT_QB = 256
ATT_KB = 3 * ATT_QB
LEFT = 8 * CHUNK
MAX_REL = 128
N_REL = 2 * MAX_REL + 1
REL_PAD = 384
REL_TILE = 128
REL_TILES = (3, 4)
D_FF = 4 * D

ADAM_LR = 0.001
ADAM_B1 = 0.9
ADAM_B2 = 0.999
ADAM_EPS = 1e-08
ADAM_WD = 0.01
ADAM_STEP = 10

VMEM_LIMIT = 48 * 1024 * 1024


def _params(n_axes):
    return pltpu.CompilerParams(dimension_semantics=("arbitrary",) * n_axes, vmem_limit_bytes=VMEM_LIMIT)


def _dot(a, b):
    return jnp.dot(a, b, preferred_element_type=F32)


def _dot_nt(a, b):
    return lax.dot_general(a, b, (((1,), (1,)), ((), ())), preferred_element_type=F32)


def _dot_tn(a, b):
    return lax.dot_general(a, b, (((0,), (0,)), ((), ())), preferred_element_type=F32)


def _bf(a):
    return a.astype(BF16)


def _prologue(kind, a, p1=None, p2=None):
    if kind == "mod":
        return a.astype(F32) * (1.0 + p1) + p2
    if kind == "scale":
        return a.astype(F32) * (1.0 + p1)
    if kind == "relu2":
        r = jnp.maximum(a, 0.0)
        return r * r
    return a


class Hosted:
    def __init__(self, inputs, out_shapes, sems, first, last):
        self.inputs, self.out_shapes, self.sems, self.first, self.last = inputs, out_shapes, sems, first, last


def _hbm_specs(n):
    return [pl.BlockSpec(memory_space=pltpu.HBM)] * n


def _call_hosting(body, comm, *, first, last, in_specs, out_specs, out_shape, scratch_shapes, args, **kw):
    if comm is None:
        return pl.pallas_call(body, in_specs=in_specs, out_specs=out_specs, out_shape=out_shape,
                              scratch_shapes=scratch_shapes, **kw)(*args), []
    n_in, n_out, n_scr = len(in_specs), len(out_specs), len(scratch_shapes)
    n_ci, n_co = len(comm.inputs), len(comm.out_shapes)

    def hosting(*refs):
        ins, ci = refs[:n_in], refs[n_in:n_in + n_ci]
        k = n_in + n_ci
        outs, co = refs[k:k + n_out], refs[k + n_out:k + n_out + n_co]
        k += n_out + n_co
        scr, cs = refs[k:k + n_scr], refs[k + n_scr:]

        @pl.when(first())
        def _():
            comm.first(ci, co, cs)

        body(*ins, *outs, *scr)

        @pl.when(last())
        def _():
            comm.last(ci, co, cs)

    res = pl.pallas_call(
        hosting, in_specs=list(in_specs) + _hbm_specs(n_ci), out_specs=list(out_specs) + _hbm_specs(n_co),
        out_shape=list(out_shape) + list(comm.out_shapes), scratch_shapes=list(scratch_shapes) + list(comm.sems),
        **kw)(*args, *comm.inputs)
    return res[:n_out], res[n_out:]


def run_hosted(comm, *, name):
    n_i, n_o = len(comm.inputs), len(comm.out_shapes)

    def body(*refs):
        ins, outs, sems = refs[:n_i], refs[n_i:n_i + n_o], refs[n_i + n_o:]
        comm.first(ins, outs, sems)
        comm.last(ins, outs, sems)

    return pl.pallas_call(body, name=name, out_shape=list(comm.out_shapes), in_specs=_hbm_specs(n_i),
                          out_specs=_hbm_specs(n_o), scratch_shapes=list(comm.sems))(*comm.inputs)


def mm_nn(a, b, layer, *, pro=None, p1=None, p2=None, bias=None, out_dtype=F32, tm, tn, comm=None, name):
    M, K = a.shape
    N = b.shape[2]
    tm = min(tm, M)
    n_p = {"mod": 2, "scale": 1}.get(pro, 0)
    has_bias = bias is not None
    direct = pro is None and a.dtype == BF16

    def body(*refs):
        a_ref, b_ref = refs[0], refs[1]
        p_refs = refs[2:2 + n_p]
        bias_ref = refs[2 + n_p] if has_bias else None
        if direct:
            o_ref = refs[-1]
            lhs = a_ref[...]
        else:
            o_ref, abf = refs[-2], refs[-1]

            @pl.when(pl.program_id(1) == 0)
            def _():
                abf[...] = _bf(_prologue(pro, a_ref[...], *[r[...] for r in p_refs]))

            lhs = abf[...]
        acc = _dot(lhs, b_ref[...])
        if has_bias:
            acc = acc + bias_ref[...]
        o_ref[...] = acc.astype(out_dtype)

    in_specs = [pl.BlockSpec((tm, K), lambda i, j: (i, 0)), pl.BlockSpec((None, K, tn), lambda i, j: (layer, 0, j))]
    args = [a, b]
    for p in (p1, p2)[:n_p]:
        in_specs.append(pl.BlockSpec((1, K), lambda i, j: (0, 0)))
        args.append(p)
    if has_bias:
        in_specs.append(pl.BlockSpec((1, tn), lambda i, j: (0, j)))
        args.append(bias)
    n_i, n_j = M // tm, N // tn
    (out,), got = _call_hosting(
        body, comm, first=lambda: (pl.program_id(0) == 0) & (pl.program_id(1) == 0),
        last=lambda: (pl.program_id(0) == n_i - 1) & (pl.program_id(1) == n_j - 1),
        name=name, grid=(n_i, n_j), in_specs=in_specs,
        out_specs=[pl.BlockSpec((tm, tn), lambda i, j: (i, j))],
        out_shape=[jax.ShapeDtypeStruct((M, N), out_dtype)],
        scratch_shapes=[] if direct else [pltpu.VMEM((tm, K), BF16)], compiler_params=_params(2), args=args)
    return out if comm is None else (out, got)


def mm_nt(a_parts, w, layer, *, pro=None, p1=None, epi_h=None, out_dtype=F32, tm, tn, comm=None, name):
    M = a_parts[0].shape[0]
    tm = min(tm, M)
    widths = [p.shape[1] for p in a_parts]
    Nw = sum(widths)
    Kw = w.shape[1]
    n_a = len(a_parts)
    has_p = pro == "scale"
    has_h = epi_h is not None
    direct = n_a == 1 and not has_p and a_parts[0].dtype == BF16

    def body(*refs):
        a_refs = refs[:n_a]
        w_ref = refs[n_a]
        k = n_a + 1
        p_ref = refs[k] if has_p else None
        k += int(has_p)
        h_ref = refs[k] if has_h else None
        if direct:
            o_ref = refs[-1]
            lhs = a_refs[0][...]
        else:
            o_ref, abf = refs[-2], refs[-1]

            @pl.when(pl.program_id(1) == 0)
            def _():
                off = 0
                for r, wd in zip(a_refs, widths):
                    av = r[...]
                    if has_p:
                        av = av.astype(F32) * (1.0 + p_ref[...])
                    abf[:, off:off + wd] = _bf(av)
                    off += wd

            lhs = abf[...]
        acc = _dot_nt(lhs, w_ref[...])
        if has_h:
            acc = acc * (2.0 * jnp.maximum(h_ref[...], 0.0))
        o_ref[...] = acc.astype(out_dtype)

    in_specs = [pl.BlockSpec((tm, wd), lambda i, j: (i, 0)) for wd in widths]
    in_specs.append(pl.BlockSpec((None, tn, Nw), lambda i, j: (layer, j, 0)))
    args = list(a_parts) + [w]
    if has_p:
        in_specs.append(pl.BlockSpec((1, Nw), lambda i, j: (0, 0)))
        args.append(p1)
    if has_h:
        in_specs.append(pl.BlockSpec((tm, tn), lambda i, j: (i, j)))
        args.append(epi_h)
    n_i, n_j = M // tm, Kw // tn
    (out,), got = _call_hosting(
        body, comm, first=lambda: (pl.program_id(0) == 0) & (pl.program_id(1) == 0),
        last=lambda: (pl.program_id(0) == n_i - 1) & (pl.program_id(1) == n_j - 1),
        name=name, grid=(n_i, n_j), in_specs=in_specs,
        out_specs=[pl.BlockSpec((tm, tn), lambda i, j: (i, j))],
        out_shape=[jax.ShapeDtypeStruct((M, Kw), out_dtype)],
        scratch_shapes=[] if direct else [pltpu.VMEM((tm, Nw), BF16)], compiler_params=_params(2), args=args)
    return out if comm is None else (out, got)


def mm_tn(a, d, *, pro=None, p1=None, p2=None, dscale=None, tk, tn, tm, out_buf, out_shape, col_block0=0, name):
    M, Kf = a.shape
    N = d.shape[1]
    n_p = {"mod": 2}.get(pro, 0)
    has_ds = dscale is not None
    has_buf = out_buf is not None
    n_m = M // tm

    def body(*refs):
        a_ref, d_ref = refs[0], refs[1]
        p_refs = refs[2:2 + n_p]
        ds_ref = refs[2 + n_p] if has_ds else None
        o_ref, acc = refs[-2], refs[-1]
        m = pl.program_id(2)

        @pl.when(m == 0)
        def _():
            acc[...] = jnp.zeros_like(acc)

        av = _prologue(pro, a_ref[...], *[r[...] for r in p_refs])
        dv = d_ref[...]
        if has_ds:
            dv = dv.astype(F32) * (1.0 + ds_ref[...])
        acc[...] += _dot_tn(_bf(av), _bf(dv))

        @pl.when(m == n_m - 1)
        def _():
            o_ref[...] = _bf(acc[...])

    in_specs = [pl.BlockSpec((tm, tk), lambda i, j, m: (m, i)), pl.BlockSpec((tm, tn), lambda i, j, m: (m, j))]
    args = [a, d]
    for p in (p1, p2)[:n_p]:
        in_specs.append(pl.BlockSpec((1, tk), lambda i, j, m: (0, i)))
        args.append(p)
    if has_ds:
        in_specs.append(pl.BlockSpec((1, tn), lambda i, j, m: (0, j)))
        args.append(dscale)
    aliases = {}
    if has_buf:
        in_specs.append(pl.BlockSpec(memory_space=pl.ANY))
        args.append(out_buf)
        aliases = {len(args) - 1: 0}
    return pl.pallas_call(
        body, name=name, grid=(Kf // tk, N // tn, n_m), in_specs=in_specs,
        out_specs=pl.BlockSpec((None, tk, tn), lambda i, j, m: (0, i, col_block0 + j)),
        out_shape=jax.ShapeDtypeStruct(out_shape, BF16), input_output_aliases=aliases,
        scratch_shapes=[pltpu.VMEM((tk, tn), F32)], compiler_params=_params(3),
    )(*args)


ROW_BLOCK = 512
ACC_ROWS = 8


def _ln_stats(z):
    mu = jnp.mean(z, axis=-1, keepdims=True)
    zc = z - mu
    var = jnp.mean(zc * zc, axis=-1, keepdims=True)
    return zc, lax.rsqrt(var + LN_EPS)


def ln_fwd(x, y, gate, lng, lnb, *, name):
    S = x.shape[0]

    def body(x_ref, y_ref, gt_ref, g_ref, b_ref, o_ref):
        z = ALPHA * x_ref[...] + (1.0 + gt_ref[...]) * y_ref[...]
        zc, rstd = _ln_stats(z)
        o_ref[...] = (zc * rstd) * g_ref[...] + b_ref[...]

    row = pl.BlockSpec((ROW_BLOCK, D), lambda i: (i, 0))
    vec = pl.BlockSpec((1, D), lambda i: (0, 0))
    return pl.pallas_call(
        body, name=name, grid=(S // ROW_BLOCK,), in_specs=[row, row, vec, vec, vec], out_specs=row,
        out_shape=jax.ShapeDtypeStruct((S, D), F32), compiler_params=_params(1),
    )(x, y, gate, lng, lnb)


def _add_colsum(acc_ref, r, val):
    acc_ref[r:r + 1, :] += jnp.sum(val, axis=0, keepdims=True)


def ln_bwd(x_in, y, gate, lng, *, dout=None, nxt=None, name):
    S = x_in.shape[0]
    has_next = nxt is not None

    def body(*refs):
        if has_next:
            dzn_ref, dun_ref, scn_ref, b_ref = refs[:4]
            k = 4
        else:
            do_ref = refs[0]
            k = 1
        x_ref, y_ref, gt_ref, g_ref = refs[k:k + 4]
        dz_ref, acc_ref = refs[k + 4:]

        @pl.when(pl.program_id(0) == 0)
        def _():
            acc_ref[...] = jnp.zeros_like(acc_ref)

        yv = y_ref[...]
        z = ALPHA * x_ref[...] + (1.0 + gt_ref[...]) * yv
        zc, rstd = _ln_stats(z)
        xhat = zc * rstd
        if has_next:
            du = dun_ref[...]
            dout_v = ALPHA * dzn_ref[...] + du * (1.0 + scn_ref[...])
            _add_colsum(acc_ref, 3, du * (xhat * g_ref[...] + b_ref[...]))
            _add_colsum(acc_ref, 4, du)
        else:
            dout_v = do_ref[...]
        _add_colsum(acc_ref, 0, dout_v * xhat)
        _add_colsum(acc_ref, 1, dout_v)
        dxh = dout_v * g_ref[...]
        m1 = jnp.mean(dxh, axis=-1, keepdims=True)
        m2 = jnp.mean(dxh * xhat, axis=-1, keepdims=True)
        dz = rstd * (dxh - m1 - xhat * m2)
        _add_colsum(acc_ref, 2, dz * yv)
        dz_ref[...] = dz

    row = pl.BlockSpec((ROW_BLOCK, D), lambda i: (i, 0))
    vec = pl.BlockSpec((1, D), lambda i: (0, 0))
    if has_next:
        in_specs = [row, row, vec, vec]
        args = list(nxt)
    else:
        in_specs = [row]
        args = [dout]
    in_specs += [row, row, vec, vec]
    args += [x_in, y, gate, lng]
    return pl.pallas_call(
        body, name=name, grid=(S // ROW_BLOCK,), in_specs=in_specs,
        out_specs=[row, pl.BlockSpec((ACC_ROWS, D), lambda i: (0, 0))],
        out_shape=[jax.ShapeDtypeStruct((S, D), F32), jax.ShapeDtypeStruct((ACC_ROWS, D), F32)],
        compiler_params=_params(1),
    )(*args)


def combine_final(dz, du, sc, x_in, *, name):
    S = dz.shape[0]

    def body(dz_ref, du_ref, sc_ref, x_ref, dx_ref, acc_ref):
        @pl.when(pl.program_id(0) == 0)
        def _():
            acc_ref[...] = jnp.zeros_like(acc_ref)

        du_v = du_ref[...]
        dx_ref[...] = ALPHA * dz_ref[...] + du_v * (1.0 + sc_ref[...])
        _add_colsum(acc_ref, 3, du_v * x_ref[...])
        _add_colsum(acc_ref, 4, du_v)

    row = pl.BlockSpec((ROW_BLOCK, D), lambda i: (i, 0))
    vec = pl.BlockSpec((1, D), lambda i: (0, 0))
    return pl.pallas_call(
        body, name=name, grid=(S // ROW_BLOCK,), in_specs=[row, row, vec, row],
        out_specs=[row, pl.BlockSpec((ACC_ROWS, D), lambda i: (0, 0))],
        out_shape=[jax.ShapeDtypeStruct((S, D), F32), jax.ShapeDtypeStruct((ACC_ROWS, D), F32)],
        compiler_params=_params(1),
    )(dz, du, sc, x_in)


def loss_head(y, t, *, name):
    S = y.shape[0]

    def body(y_ref, t_ref, dy_ref, l_ref):
        @pl.when(pl.program_id(0) == 0)
        def _():
            l_ref[...] = jnp.zeros_like(l_ref)

        e = y_ref[...] - t_ref[...]
        dy_ref[...] = e * (1.0 / D)
        per_tok = jnp.sum(e * e, axis=1, keepdims=True) * (1.0 / D)
        l_ref[...] += 0.5 * jnp.sum(per_tok, axis=0, keepdims=True)

    row = pl.BlockSpec((ROW_BLOCK, D), lambda i: (i, 0))
    return pl.pallas_call(
        body, name=name, grid=(S // ROW_BLOCK,), in_specs=[row, row],
        out_specs=[row, pl.BlockSpec((8, 128), lambda i: (0, 0))],
        out_shape=[jax.ShapeDtypeStruct((S, D), F32), jax.ShapeDtypeStruct((8, 128), F32)],
        compiler_params=_params(1),
    )(y, t)


def _log_sigmoid(x):
    return jnp.minimum(x, 0.0) - jnp.log(1.0 + jnp.exp(-jnp.abs(x)))


def _silu(x):
    return x * (1.0 / (1.0 + jnp.exp(-x)))


def _cumsum_steps(x):
    row = lax.broadcasted_iota(jnp.int32, x.shape, 0)
    step = 1
    while step < x.shape[0]:
        x = x + jnp.where(row >= step, pltpu.roll(x, step, 0), 0.0)
        step *= 2
    return x


@jax.custom_vjp
def _cumsum_rows(x):
    return _cumsum_steps(x)


def _cumsum_rows_fwd(x):
    return _cumsum_steps(x), None


def _cumsum_rows_bwd(_, g):
    return (jnp.sum(g, axis=0, keepdims=True) - _cumsum_steps(g) + g,)


_cumsum_rows.defvjp(_cumsum_rows_fwd, _cumsum_rows_bwd)


def _gla_chunk(q, k, v, g, gk, s0t, w2p, bgk, gn):
    C = q.shape[0]
    row = lax.broadcasted_iota(jnp.int32, (C, C), 0)
    col = lax.broadcasted_iota(jnp.int32, (C, C), 1)
    lower = row >= col
    la = _log_sigmoid(_dot(_bf(gk), _bf(w2p)) + bgk) * GLA_TAU_INV
    outs, states = [], []
    for h in range(GLA_H):
        ks = slice(h * GLA_DKH, (h + 1) * GLA_DKH)
        vs = slice(h * GLA_DVH, (h + 1) * GLA_DVH)
        qh = q[:, ks] * (GLA_DKH ** -0.5)
        kh, vh, gh, lah, s0 = k[:, ks], v[:, vs], g[:, vs], la[:, ks], s0t[h]
        cum = _cumsum_rows(lah)
        e_pos = jnp.exp(cum)
        e_neg = jnp.exp(-cum)
        q_f = qh * e_pos
        a_f = _dot_nt(_bf(q_f), _bf(kh * e_neg))
        a_b = _dot_nt(_bf(qh * e_neg), _bf(kh * e_pos))
        att = jnp.where(lower, a_f, a_b)
        o = _dot(_bf(att), _bf(vh)) + _dot_nt(_bf(q_f), _bf(s0))
        tot = jnp.sum(lah, axis=0, keepdims=True)
        k_end = kh * jnp.exp(tot - cum)
        states.append(s0 * jnp.exp(tot) + _dot_tn(_bf(vh), _bf(k_end)))
        on = o * lax.rsqrt(jnp.mean(o * o, axis=-1, keepdims=True) + RMS_EPS) * gn[:, vs]
        outs.append(on * _silu(gh))
    return jnp.concatenate(outs, axis=1), tuple(states)


def _gla_split(p):
    return (p[:, 0:GLA_DK], p[:, GLA_DK:2 * GLA_DK], p[:, 2 * GLA_DK:2 * GLA_DK + GLA_DV],
            p[:, 2 * GLA_DK + GLA_DV:2 * GLA_DK + 2 * GLA_DV], p[:, 2 * GLA_DK + 2 * GLA_DV:GLA_INP])


def gla_fwd(proj, w2p, bgk, gn, *, comm=None, name):
    S = proj.shape[0]
    n_c = S // CHUNK
    n_s = n_c // GLA_SUB
    rows = GLA_SUB * CHUNK

    def body(p_ref, w_ref, b_ref, gn_ref, o_ref, st_ref, st):
        @pl.when(pl.program_id(0) == 0)
        def _():
            st[...] = jnp.zeros_like(st)

        s = tuple(st[h] for h in range(GLA_H))
        for u in range(GLA_SUB):
            sub = slice(u * CHUNK, (u + 1) * CHUNK)
            for h in range(GLA_H):
                st_ref[u, h] = s[h]
            og, s = _gla_chunk(*_gla_split(p_ref[sub, :]), s, w_ref[...], b_ref[...], gn_ref[...])
            o_ref[sub, :] = _bf(og)
        for h in range(GLA_H):
            st[h] = s[h]

    full = lambda shape: pl.BlockSpec(shape, lambda i: (0,) * len(shape))
    return _call_hosting(
        body, comm, first=lambda: pl.program_id(0) == 0, last=lambda: pl.program_id(0) == n_s - 1,
        name=name, grid=(n_s,),
        in_specs=[pl.BlockSpec((rows, GLA_INP), lambda i: (i, 0)), full((128, GLA_DK)), full((1, GLA_DK)),
                  full((1, GLA_DV))],
        out_specs=[pl.BlockSpec((rows, GLA_DV), lambda i: (i, 0)),
                   pl.BlockSpec((GLA_SUB, GLA_H, GLA_DVH, GLA_DKH), lambda i: (i, 0, 0, 0))],
        out_shape=[jax.ShapeDtypeStruct((S, GLA_DV), BF16),
                   jax.ShapeDtypeStruct((n_c, GLA_H, GLA_DVH, GLA_DKH), F32)],
        scratch_shapes=[pltpu.VMEM((GLA_H, GLA_DVH, GLA_DKH), F32)], compiler_params=_params(1),
        args=(proj, w2p, bgk, gn))


def gla_bwd(proj, dog, states, w2p, bgk, gn, *, comm=None, name):
    S = proj.shape[0]
    n_c = S // CHUNK
    n_s = n_c // GLA_SUB
    rows = GLA_SUB * CHUNK

    def body(p_ref, dog_ref, st_ref, w_ref, b_ref, gn_ref, dp_ref, dw_ref, db_ref, dgn_ref, ds_ref):
        @pl.when(pl.program_id(0) == 0)
        def _():
            ds_ref[...] = jnp.zeros_like(ds_ref)
            dw_ref[...] = jnp.zeros_like(dw_ref)
            db_ref[...] = jnp.zeros_like(db_ref)
            dgn_ref[...] = jnp.zeros_like(dgn_ref)

        ds = tuple(ds_ref[h] for h in range(GLA_H))
        for u in reversed(range(GLA_SUB)):
            sub = slice(u * CHUNK, (u + 1) * CHUNK)
            q, k, v, g, gk = _gla_split(p_ref[sub, :])
            s0 = tuple(st_ref[u, h] for h in range(GLA_H))
            _, vjp = jax.vjp(_gla_chunk, q, k, v, g, gk, s0, w_ref[...], b_ref[...], gn_ref[...])
            dq, dk, dv, dg, dgk, ds, dw, db, dgn = vjp((dog_ref[sub, :], ds))
            dp_ref[sub, 0:GLA_DK] = _bf(dq)
            dp_ref[sub, GLA_DK:2 * GLA_DK] = _bf(dk)
            dp_ref[sub, 2 * GLA_DK:2 * GLA_DK + GLA_DV] = _bf(dv)
            dp_ref[sub, 2 * GLA_DK + GLA_DV:2 * GLA_DK + 2 * GLA_DV] = _bf(dg)
            dp_ref[sub, 2 * GLA_DK + 2 * GLA_DV:GLA_INP] = _bf(dgk)
            dw_ref[...] += dw
            db_ref[...] += db
            dgn_ref[...] += dgn
        for h in range(GLA_H):
            ds_ref[h] = ds[h]

    full = lambda shape: pl.BlockSpec(shape, lambda i: (0,) * len(shape))
    rev = lambda i: (n_s - 1 - i, 0)
    return _call_hosting(
        body, comm, first=lambda: pl.program_id(0) == 0, last=lambda: pl.program_id(0) == n_s - 1,
        name=name, grid=(n_s,),
        in_specs=[pl.BlockSpec((rows, GLA_INP), rev), pl.BlockSpec((rows, GLA_DV), rev),
                  pl.BlockSpec((GLA_SUB, GLA_H, GLA_DVH, GLA_DKH), lambda i: (n_s - 1 - i, 0, 0, 0)),
                  full((128, GLA_DK)), full((1, GLA_DK)), full((1, GLA_DV))],
        out_specs=[pl.BlockSpec((rows, GLA_INP), rev), full((128, GLA_DK)), full((1, GLA_DK)), full((1, GLA_DV))],
        out_shape=[jax.ShapeDtypeStruct((S, GLA_INP), BF16), jax.ShapeDtypeStruct((128, GLA_DK), F32),
                   jax.ShapeDtypeStruct((1, GLA_DK), F32), jax.ShapeDtypeStruct((1, GLA_DV), F32)],
        scratch_shapes=[pltpu.VMEM((GLA_H, GLA_DVH, GLA_DKH), F32)], compiler_params=_params(1),
        args=(proj, dog, states, w2p, bgk, gn))


def _rel_index():
    t = np.arange(REL_TILE)[:, None]
    j = np.arange(REL_TILE)[None, :]
    tiles = []
    for m in REL_TILES:
        chunks = (REL_TILE // CHUNK) * m + j // CHUNK - t // CHUNK
        band = (chunks >= 0) & (chunks <= LEFT // CHUNK)
        dist = LEFT - REL_TILE * m + t - j
        tiles.append(np.where(band, np.minimum(dist, MAX_REL) + MAX_REL, N_REL))
    return jnp.asarray(np.stack(tiles).reshape(1, -1).astype(np.int32))


REL_BLOCK = 2048


def _one_hot(idx_row):
    return (lax.broadcasted_iota(jnp.int32, (REL_PAD, idx_row.shape[1]), 0) == idx_row).astype(F32)


def rel_bias_tiles(rel_pad, idx, *, name):
    E = idx.shape[1]

    def body(r_ref, i_ref, o_ref):
        o_ref[...] = jnp.dot(r_ref[...], _one_hot(i_ref[...]), precision=HIGHEST, preferred_element_type=F32)

    return pl.pallas_call(
        body, name=name, grid=(E // REL_BLOCK,),
        in_specs=[pl.BlockSpec((ATT_H, REL_PAD), lambda i: (0, 0)), pl.BlockSpec((1, REL_BLOCK), lambda i: (0, i))],
        out_specs=pl.BlockSpec((ATT_H, REL_BLOCK), lambda i: (0, i)),
        out_shape=jax.ShapeDtypeStruct((ATT_H, E), F32), compiler_params=_params(1),
    )(rel_pad, idx)


def rel_bias_grad(dtiles_flat, dclip, idx, *, name):
    E = idx.shape[1]
    n_steps = E // REL_BLOCK

    def body(d_ref, c_ref, i_ref, o_ref):
        @pl.when(pl.program_id(0) == 0)
        def _():
            o_ref[...] = jnp.zeros_like(o_ref)

        o_ref[...] += lax.dot_general(d_ref[...], _one_hot(i_ref[...]), (((1,), (1,)), ((), ())),
                                      precision=HIGHEST, preferred_element_type=F32)

        @pl.when(pl.program_id(0) == n_steps - 1)
        def _():
            at_clip = lax.broadcasted_iota(jnp.int32, (1, REL_PAD), 1) == 2 * MAX_REL
            o_ref[...] += jnp.where(at_clip, jnp.sum(c_ref[...], axis=1, keepdims=True), 0.0)

    return pl.pallas_call(
        body, name=name, grid=(n_steps,),
        in_specs=[pl.BlockSpec((ATT_H, REL_BLOCK), lambda i: (0, i)), pl.BlockSpec((ATT_H, 128), lambda i: (0, 0)),
                  pl.BlockSpec((1, REL_BLOCK), lambda i: (0, i))],
        out_specs=pl.BlockSpec((ATT_H, REL_PAD), lambda i: (0, 0)),
        out_shape=jax.ShapeDtypeStruct((ATT_H, REL_PAD), F32), compiler_params=_params(1),
    )(dtiles_flat, dclip, idx)


def _attn_bias(tiles, clip):
    const = jnp.broadcast_to(clip, (REL_TILE, REL_TILE))
    zero = jnp.zeros((REL_TILE, REL_TILE), F32)
    rows = []
    for qt in range(ATT_QB // REL_TILE):
        blocks = []
        for kt in range(ATT_KB // REL_TILE):
            m = kt - qt
            if m in REL_TILES:
                blocks.append(tiles[REL_TILES.index(m)])
            elif 0 <= m < REL_TILES[0]:
                blocks.append(const)
            else:
                blocks.append(zero)
        rows.append(jnp.concatenate(blocks, axis=1))
    return jnp.concatenate(rows, axis=0)


def _attn_bias_grad(ds, dt_ref, dc_ref, a):
    tile = lambda qt, kt: ds[qt * REL_TILE:(qt + 1) * REL_TILE, kt * REL_TILE:(kt + 1) * REL_TILE]
    const = None
    sums = [None] * len(REL_TILES)
    for qt in range(ATT_QB // REL_TILE):
        for kt in range(ATT_KB // REL_TILE):
            m = kt - qt
            if m in REL_TILES:
                n = REL_TILES.index(m)
                sums[n] = tile(qt, kt) if sums[n] is None else sums[n] + tile(qt, kt)
            elif 0 <= m < REL_TILES[0]:
                const = tile(qt, kt) if const is None else const + tile(qt, kt)
    for n, v in enumerate(sums):
        dt_ref[a, n] += v
    dc_ref[a] += jnp.sum(const, axis=0, keepdims=True)


def _attn_head_lanes():
    lane = lax.broadcasted_iota(jnp.int32, (1, 2 * ATT_HD), 1)
    return [(lane >= a * ATT_HD) & (lane < (a + 1) * ATT_HD) for a in range(2)]


def _attn_band_bias(tiles, clip):
    j = lax.broadcasted_iota(jnp.int32, (ATT_QB, ATT_KB), 1)
    t = lax.broadcasted_iota(jnp.int32, (ATT_QB, ATT_KB), 0)
    shift = CHUNK.bit_length() - 1
    chunks = lax.shift_right_logical(j, shift) - lax.shift_right_logical(t, shift)
    band = (chunks >= 0) & (chunks <= LEFT // CHUNK)
    return jnp.where(band, _attn_bias(tiles, clip), NEG_INF)


def _attn_exp(qa, kb, bias, key_bias):
    s = _dot_nt(qa, kb) + bias + key_bias
    e = jnp.exp(s - jnp.max(s, axis=-1, keepdims=True))
    return e, jnp.sum(e, axis=-1, keepdims=True)


def _attn_specs():
    n_hp = ATT_H // 2
    q_spec = pl.BlockSpec((ATT_QB, 128), lambda hp, g: (g, hp))

    def win(col0, back):
        return pl.BlockSpec((ATT_QB, 128), lambda hp, g: (jnp.maximum(g - back, 0), col0 + hp))

    kv_specs = [win(n_hp, 2), win(n_hp, 1), win(n_hp, 0), win(2 * n_hp, 2), win(2 * n_hp, 1), win(2 * n_hp, 0)]
    tiles_spec = pl.BlockSpec((2, len(REL_TILES), REL_TILE, REL_TILE), lambda hp, g: (hp, 0, 0, 0))
    clip_spec = pl.BlockSpec((2, 1, 128), lambda hp, g: (hp, 0, 0))
    return q_spec, kv_specs, tiles_spec, clip_spec


def _attn_window(refs, g):
    kb = jnp.concatenate([_bf(r[...]) for r in refs[0:3]], axis=0)
    vb = jnp.concatenate([_bf(r[...]) for r in refs[3:6]], axis=0)
    j = lax.broadcasted_iota(jnp.int32, (1, ATT_KB), 1)
    return kb, vb, jnp.where(j + (g - 2) * ATT_QB >= 0, 0.0, NEG_INF)


def attn_fwd(qkv, tiles, clip, *, comm=None, name):
    S = qkv.shape[0]
    q_spec, kv_specs, tiles_spec, clip_spec = _attn_specs()

    def body(q_ref, *rest):
        kv_refs, t_ref, c_ref, o_ref, bias = rest[:6], rest[6], rest[7], rest[8], rest[9]
        g = pl.program_id(1)

        @pl.when(g == 0)
        def _():
            for a in range(2):
                bias[a * ATT_QB:(a + 1) * ATT_QB, :] = _attn_band_bias(t_ref[a], c_ref[a])

        kb, vb, key_bias = _attn_window(kv_refs, g)
        q = q_ref[...].astype(F32)
        out = jnp.zeros((ATT_QB, 2 * ATT_HD), F32)
        for a, lanes in enumerate(_attn_head_lanes()):
            mf = lanes.astype(F32)
            e, l = _attn_exp(_bf(q * (mf * ATT_HD ** -0.5)), kb, bias[a * ATT_QB:(a + 1) * ATT_QB, :], key_bias)
            out = out + _dot(_bf(e), vb) * (mf * (1.0 / l))
        o_ref[...] = _bf(out)

    n_hp, n_g = ATT_H // 2, S // ATT_QB
    return _call_hosting(
        body, comm, first=lambda: (pl.program_id(0) == 0) & (pl.program_id(1) == 0),
        last=lambda: (pl.program_id(0) == n_hp - 1) & (pl.program_id(1) == n_g - 1),
        name=name, grid=(n_hp, n_g), in_specs=[q_spec] + kv_specs + [tiles_spec, clip_spec],
        out_specs=[q_spec], out_shape=[jax.ShapeDtypeStruct((S, D), BF16)],
        scratch_shapes=[pltpu.VMEM((2 * ATT_QB, ATT_KB), F32)], compiler_params=_params(2),
        args=(*([qkv] * 7), tiles, clip))


def attn_bwd(qkv, do, tiles, clip, *, comm=None, name):
    S = qkv.shape[0]
    q_spec, kv_specs, tiles_spec, clip_spec = _attn_specs()
    col_spec = pl.BlockSpec((S, 128), lambda hp, g: (0, hp))
    sum_spec = pl.BlockSpec((1, 128), lambda hp, g: (0, hp))
    n_g = S // ATT_QB

    def body(q_ref, *rest):
        kv_refs, t_ref, c_ref, do_ref = rest[:6], rest[6], rest[7], rest[8]
        dq_ref, dk_ref, dv_ref, dt_ref, dc_ref, sq_ref, sk_ref, sv_ref, bias = rest[9:]
        g = pl.program_id(1)

        @pl.when(g == 0)
        def _():
            for a in range(2):
                bias[a * ATT_QB:(a + 1) * ATT_QB, :] = _attn_band_bias(t_ref[a], c_ref[a])
            dk_ref[...] = jnp.zeros_like(dk_ref)
            dv_ref[...] = jnp.zeros_like(dv_ref)
            dt_ref[...] = jnp.zeros_like(dt_ref)
            dc_ref[...] = jnp.zeros_like(dc_ref)
            sq_ref[...] = jnp.zeros_like(sq_ref)

        kb, vb, key_bias = _attn_window(kv_refs, g)
        q = q_ref[...].astype(F32)
        do = do_ref[...]
        lanes = _attn_head_lanes()
        mf = [m.astype(F32) * ATT_HD ** -0.5 for m in lanes]
        qs = _bf(jnp.concatenate([q * m for m in mf], axis=0))
        dos = jnp.concatenate([jnp.where(m, do, jnp.zeros_like(do)) for m in lanes], axis=0)
        e, l = _attn_exp(qs, kb, bias[...], key_bias)
        p = e * (1.0 / l)
        dp = _dot_nt(dos, vb)
        ds = p * (dp - jnp.sum(p * dp, axis=-1, keepdims=True))
        ds_b = _bf(ds)
        dq2 = _dot(ds_b, kb)
        dq = dq2[:ATT_QB] * mf[0] + dq2[ATT_QB:] * mf[1]
        dkw = _dot_tn(ds_b, qs)
        dvw = _dot_tn(_bf(p), dos)
        for a in range(2):
            _attn_bias_grad(ds[a * ATT_QB:(a + 1) * ATT_QB], dt_ref, dc_ref, a)
        dq_ref[...] = _bf(dq)
        sq_ref[...] += jnp.sum(dq, axis=0, keepdims=True)
        for blk in range(3):
            src = g - 2 + blk

            @pl.when(src >= 0)
            def _(blk=blk, src=src):
                rows = pl.ds(pl.multiple_of(src * ATT_QB, ATT_QB), ATT_QB)
                dk_ref[rows, :] += dkw[blk * ATT_QB:(blk + 1) * ATT_QB]
                dv_ref[rows, :] += dvw[blk * ATT_QB:(blk + 1) * ATT_QB]

        @pl.when(g == n_g - 1)
        def _():
            sk_ref[...] = jnp.sum(dk_ref[...], axis=0, keepdims=True)
            sv_ref[...] = jnp.sum(dv_ref[...], axis=0, keepdims=True)

    n_hp, n_g = ATT_H // 2, S // ATT_QB
    return _call_hosting(
        body, comm, first=lambda: (pl.program_id(0) == 0) & (pl.program_id(1) == 0),
        last=lambda: (pl.program_id(0) == n_hp - 1) & (pl.program_id(1) == n_g - 1),
        name=name, grid=(n_hp, n_g),
        in_specs=[q_spec] + kv_specs + [tiles_spec, clip_spec, q_spec],
        out_specs=[q_spec, col_spec, col_spec, tiles_spec, clip_spec] + [sum_spec] * 3,
        out_shape=[jax.ShapeDtypeStruct((S, D), BF16)] + [jax.ShapeDtypeStruct((S, D), F32)] * 2
        + [jax.ShapeDtypeStruct((ATT_H, len(REL_TILES), REL_TILE, REL_TILE), F32),
           jax.ShapeDtypeStruct((ATT_H, 1, 128), F32)] + [jax.ShapeDtypeStruct((1, D), F32)] * 3,
        scratch_shapes=[pltpu.VMEM((2 * ATT_QB, ATT_KB), F32)], compiler_params=_params(2),
        args=(*([qkv] * 7), tiles, clip, do))


def mods_partial(c_all, w_ada, *, name):
    n_l, _, n_c = w_ada.shape

    def body(c_ref, w_ref, o_ref):
        o_ref[...] = _dot(_bf(_silu(c_ref[...])), _bf(w_ref[...]))

    return pl.pallas_call(
        body, name=name, grid=(n_l,),
        in_specs=[pl.BlockSpec((N_DEV, D), lambda l: (0, 0)), pl.BlockSpec((None, D, n_c), lambda l: (l, 0, 0))],
        out_specs=pl.BlockSpec((None, N_DEV, n_c), lambda l: (l, 0, 0)),
        out_shape=jax.ShapeDtypeStruct((n_l, N_DEV, n_c), F32), compiler_params=_params(1),
    )(c_all, w_ada)


def w_ada_grad(c_all, dm, *, name):
    n_l, _, n_c = dm.shape

    def body(c_ref, d_ref, o_ref):
        o_ref[...] = lax.dot_general(_silu(c_ref[...]), d_ref[...], (((0,), (0,)), ((), ())),
                                     precision=HIGHEST, preferred_element_type=F32)

    return pl.pallas_call(
        body, name=name, grid=(n_l,),
        in_specs=[pl.BlockSpec((N_DEV, D), lambda l: (0, 0)), pl.BlockSpec((None, N_DEV, n_c), lambda l: (l, 0, 0))],
        out_specs=pl.BlockSpec((None, D, n_c), lambda l: (l, 0, 0)),
        out_shape=jax.ShapeDtypeStruct((n_l, D, n_c), F32), compiler_params=_params(1),
    )(c_all, dm)


def adamw(w, m, v, gparts, *, block_rows, name):
    R, C = w.shape
    n = gparts.shape[0]

    def body(w_ref, m_ref, v_ref, g_ref, go_ref, d_ref, mo_ref, vo_ref):
        g = g_ref[0].astype(F32)
        for k in range(1, n):
            g = g + g_ref[k].astype(F32)
        m_new = ADAM_B1 * m_ref[...] + (1.0 - ADAM_B1) * g
        v_new = ADAM_B2 * v_ref[...] + (1.0 - ADAM_B2) * (g * g)
        m_hat = m_new / (1.0 - ADAM_B1 ** ADAM_STEP)
        v_hat = v_new / (1.0 - ADAM_B2 ** ADAM_STEP)
        go_ref[...] = g
        d_ref[...] = -ADAM_LR * (m_hat / (jnp.sqrt(v_hat) + ADAM_EPS) + ADAM_WD * w_ref[...])
        mo_ref[...] = m_new
        vo_ref[...] = v_new

    blk = pl.BlockSpec((block_rows, C), lambda i: (i, 0))
    return pl.pallas_call(
        body, name=name, grid=(R // block_rows,),
        in_specs=[blk, blk, blk, pl.BlockSpec((n, block_rows, C), lambda i: (0, i, 0))],
        out_specs=[blk] * 4, out_shape=[jax.ShapeDtypeStruct((R, C), F32)] * 4, compiler_params=_params(1),
    )(w, m, v, gparts)


def adamw_nd(w, m, v, gparts, *, name):
    shape = w.shape
    two = (int(np.prod(shape[:-1])), shape[-1])
    rows = two[0]
    block_rows = rows
    for cand in (512, 256):
        if rows > cand and rows % cand == 0:
            block_rows = cand
            break
    outs = adamw(w.reshape(two), m.reshape(two), v.reshape(two), gparts.reshape((gparts.shape[0],) + two),
                 block_rows=block_rows, name=name)
    return [o.reshape(shape) for o in outs]


def sum_parts(parts, *, name):
    n, R, C = parts.shape

    def body(p_ref, o_ref):
        acc = p_ref[0]
        for k in range(1, n):
            acc = acc + p_ref[k]
        o_ref[...] = acc

    return pl.pallas_call(
        body, name=name, in_specs=[pl.BlockSpec((n, R, C), lambda: (0, 0, 0))],
        out_specs=pl.BlockSpec((R, C), lambda: (0, 0)), out_shape=jax.ShapeDtypeStruct((R, C), F32),
        compiler_params=pltpu.CompilerParams(vmem_limit_bytes=VMEM_LIMIT),
    )(parts)


def _my_place():
    return lax.axis_index("x"), lax.axis_index("y"), lax.axis_index("c")


def _full_shape(kind, shard):
    n_l, rows, cols = shard
    return {"col": (n_l, rows, N_DEV * cols), "row": (n_l, N_DEV * rows, cols), "stk": (N_DEV, n_l, rows, cols)}[kind]


def _slab(ref, kind, dev, shard):
    _, rows, cols = shard
    if kind == "col":
        return ref.at[:, :, pl.ds(pl.multiple_of(dev * cols, 128), cols)]
    if kind == "row":
        return ref.at[:, pl.ds(pl.multiple_of(dev * rows, 8), rows), :]
    return ref.at[dev]


def all_gather(x_shard, *, name):
    m_per, n = x_shard.shape

    def body(x_ref, out_ref, send_sems, recv_sems, local_sem):
        x, y, c = _my_place()
        me, sibling = (x, y, c), (x, y, 1 - c)
        chips = [(1 - x, y), (x, 1 - y), (1 - x, 1 - y)]

        def rows(px, py, pc):
            return out_ref.at[pl.ds((4 * px + 2 * py + pc) * m_per, m_per), :]

        def copy(k, block, to, src=None):
            return pltpu.make_async_remote_copy(
                src_ref=rows(*block) if src is None else src, dst_ref=rows(*block),
                send_sem=send_sems.at[k], recv_sem=recv_sems.at[k], device_id=to, device_id_type=MESH)

        mine = pltpu.make_async_copy(x_ref, rows(*me), local_sem)
        mine.start()
        first = [copy(0, me, sibling, src=x_ref)]
        first += [copy(1 + j, me, (*chip, c), src=x_ref) for j, chip in enumerate(chips)]
        for cp in first:
            cp.start()
        passed = [copy(4 + j, (*chip, c), sibling) for j, chip in enumerate(chips)]
        for j, chip in enumerate(chips):
            copy(1 + j, (*chip, c), me).wait_recv()
            passed[j].start()
        copy(0, sibling, me).wait_recv()
        for j, chip in enumerate(chips):
            copy(4 + j, (*chip, 1 - c), me).wait_recv()
        for cp in first + passed:
            cp.wait_send()
        mine.wait()

    return pl.pallas_call(
        body, name=name, out_shape=jax.ShapeDtypeStruct((N_DEV * m_per, n), x_shard.dtype),
        in_specs=[pl.BlockSpec(memory_space=pltpu.VMEM)], out_specs=pl.BlockSpec(memory_space=pltpu.VMEM),
        scratch_shapes=[pltpu.SemaphoreType.DMA((7,)), pltpu.SemaphoreType.DMA((7,)), pltpu.SemaphoreType.DMA],
        compiler_params=pltpu.CompilerParams(vmem_limit_bytes=VMEM_LIMIT),
    )(x_shard)


def gather_plan(shards, kinds, layers):
    n_t = len(shards)
    shapes = [(1,) + tuple(s.shape[1:]) for s in shards]

    def copies(x_refs, out_refs, sems):
        send_sems, recv_sems, local_sems = sems
        x, y, c = _my_place()
        me, sibling = (x, y, c), (x, y, 1 - c)
        chips = [(1 - x, y), (x, 1 - y), (1 - x, 1 - y)]
        own = [x_refs[t].at[pl.ds(layers[t], 1)] for t in range(n_t)]

        def slab(t, px, py, pc):
            return _slab(out_refs[t], kinds[t], 4 * px + 2 * py + pc, shapes[t])

        def copy(t, k, block, to, src=None):
            return pltpu.make_async_remote_copy(
                src_ref=slab(t, *block) if src is None else src, dst_ref=slab(t, *block),
                send_sem=send_sems.at[7 * t + k], recv_sem=recv_sems.at[7 * t + k], device_id=to,
                device_id_type=MESH)

        mine = [pltpu.make_async_copy(own[t], slab(t, *me), local_sems.at[t]) for t in range(n_t)]
        sends = []
        for t in range(n_t):
            sends.append(copy(t, 0, me, sibling, src=own[t]))
            sends += [copy(t, 1 + j, me, (*chip, c), src=own[t]) for j, chip in enumerate(chips)]
        return mine, sends, copy, me, sibling, chips, c

    def first(x_refs, out_refs, sems):
        mine, sends = copies(x_refs, out_refs, sems)[:2]
        for cp in mine + sends:
            cp.start()

    def last(x_refs, out_refs, sems):
        mine, sends, copy, me, sibling, chips, c = copies(x_refs, out_refs, sems)
        passed = []
        for j, chip in enumerate(chips):
            for t in range(n_t):
                copy(t, 1 + j, (*chip, c), me).wait_recv()
                passed.append(copy(t, 4 + j, (*chip, c), sibling))
                passed[-1].start()
        for t in range(n_t):
            copy(t, 0, sibling, me).wait_recv()
        for j, chip in enumerate(chips):
            for t in range(n_t):
                copy(t, 4 + j, (*chip, 1 - c), me).wait_recv()
        for cp in sends + passed:
            cp.wait_send()
        for cp in mine:
            cp.wait()

    return Hosted(
        list(shards), [jax.ShapeDtypeStruct(_full_shape(k, shp), s.dtype) for k, shp, s in zip(kinds, shapes, shards)],
        [pltpu.SemaphoreType.DMA((7 * n_t,)), pltpu.SemaphoreType.DMA((7 * n_t,)), pltpu.SemaphoreType.DMA((n_t,))],
        first, last)


def scatter_plan(grads, kinds, shapes):
    n_t = len(grads)

    def copies(g_refs, out_refs, sems):
        send_sems, recv_sems, local_sems = sems
        x, y, c = _my_place()
        me = 4 * x + 2 * y + c
        local = [pltpu.make_async_copy(_slab(g_refs[t], kinds[t], me, shapes[t]), out_refs[t].at[me],
                                       local_sems.at[t]) for t in range(n_t)]
        remote = []
        for t in range(n_t):
            for r in range(1, N_DEV):
                px = 1 - x if r & 4 else x
                py = 1 - y if r & 2 else y
                pc = 1 - c if r & 1 else c
                remote.append(pltpu.make_async_remote_copy(
                    src_ref=_slab(g_refs[t], kinds[t], 4 * px + 2 * py + pc, shapes[t]), dst_ref=out_refs[t].at[me],
                    send_sem=send_sems.at[7 * t + r - 1], recv_sem=recv_sems.at[7 * t + r - 1],
                    device_id=(px, py, pc), device_id_type=MESH))
        return local, remote

    def first(g_refs, out_refs, sems):
        local, remote = copies(g_refs, out_refs, sems)
        for cp in local + remote:
            cp.start()

    def last(g_refs, out_refs, sems):
        local, remote = copies(g_refs, out_refs, sems)
        for cp in remote + local:
            cp.wait()

    return Hosted(
        list(grads), [jax.ShapeDtypeStruct((N_DEV,) + tuple(s), BF16) for s in shapes],
        [pltpu.SemaphoreType.DMA((7 * n_t,)), pltpu.SemaphoreType.DMA((7 * n_t,)), pltpu.SemaphoreType.DMA((n_t,))],
        first, last)


def adamw_layer(w, m, v, parts, layer, bufs, *, name):
    n_l, rows, cols = w.shape
    n = parts.shape[0]
    tr = min(rows, 256)

    def body(w_ref, m_ref, v_ref, g_ref, *rest):
        go_ref, d_ref, mo_ref, vo_ref = rest[-4:]
        g = g_ref[0].astype(F32)
        for k in range(1, n):
            g = g + g_ref[k].astype(F32)
        m_new = ADAM_B1 * m_ref[...] + (1.0 - ADAM_B1) * g
        v_new = ADAM_B2 * v_ref[...] + (1.0 - ADAM_B2) * (g * g)
        m_hat = m_new / (1.0 - ADAM_B1 ** ADAM_STEP)
        v_hat = v_new / (1.0 - ADAM_B2 ** ADAM_STEP)
        go_ref[...] = g
        d_ref[...] = -ADAM_LR * (m_hat / (jnp.sqrt(v_hat) + ADAM_EPS) + ADAM_WD * w_ref[...])
        mo_ref[...] = m_new
        vo_ref[...] = v_new

    blk = pl.BlockSpec((None, tr, cols), lambda i: (layer, i, 0))
    in_specs = [blk, blk, blk, pl.BlockSpec((n, None, tr, cols), lambda i: (0, 0, i, 0))]
    args = [w, m, v, parts]
    aliases = {}
    if bufs is not None:
        in_specs += [pl.BlockSpec(memory_space=pl.ANY)] * 4
        args += list(bufs)
        aliases = {4 + k: k for k in range(4)}
    return pl.pallas_call(
        body, name=name, grid=(rows // tr,), in_specs=in_specs, out_specs=[blk] * 4,
        out_shape=[jax.ShapeDtypeStruct((n_l, rows, cols), F32)] * 4, input_output_aliases=aliases,
        compiler_params=_params(1),
    )(*args)


BIG =("gla_w_in", "gla_w_out", "att_w_in", "att_w_out", "ff_w1", "ff_w2")
KIND = {"gla_w_in": "stk", "gla_w_out": "row", "att_w_in": "col", "att_w_out": "row", "ff_w1": "col", "ff_w2": "row"}


def _pack_small(arrs):
    parts = []
    for a in arrs:
        f = a.reshape(-1)
        parts.append(jnp.pad(f, (0, -f.shape[0] % 128)))
    flat = jnp.concatenate(parts)
    flat = jnp.pad(flat, (0, -flat.shape[0] % 1024))
    return flat.reshape(-1, 128)


def _unpack_small(packed, shapes):
    flat = packed.reshape(packed.shape[:-2] + (-1,))
    out, off = [], 0
    for shp in shapes:
        n = int(np.prod(shp))
        out.append(flat[..., off:off + n].reshape(packed.shape[:-2] + tuple(shp)))
        off += n + (-n % 128)
    return out


def _vec(a):
    return a.reshape(1, -1)


def _layer_weights(i):
    mixer = "gla" if i % 2 == 0 else "att"
    return [(f"{mixer}_w_in", i // 2), (f"{mixer}_w_out", i // 2), ("ff_w1", i), ("ff_w2", i)]


def _trunk(x, target, mods, sm, w, m, v):
    shard_bf = {n: w[n].astype(BF16) for n in BIG}

    def gather_of(names):
        return gather_plan([shard_bf[n] for n, _ in names], [KIND[n] for n, _ in names], [l for _, l in names])

    def gather_under_core(i):
        names = _layer_weights(i)[2:] + (_layer_weights(i + 1)[:2] if i + 1 < DEPTH else [])
        return names, gather_of(names)

    wts = {}

    def keep_gathered(names, arrays):
        for (n, l), a in zip(names, arrays):
            if KIND[n] == "stk":
                a = a.transpose(1, 2, 0, 3).reshape(1, D, GLA_IN)
                a = jnp.pad(a, ((0, 0), (0, 0), (0, GLA_INP - GLA_IN)))
            wts[n, l] = a

    first_names, rest_names = _layer_weights(0)[:1], _layer_weights(0)[1:2]
    keep_gathered(first_names, run_hosted(gather_of(first_names), name="gather_first"))

    rel_idx = _rel_index()
    saved = []
    for i in range(DEPTH):
        sh1, sc1, g1, sh2, sc2, g2 = [mods[i, k:k + 1] for k in range(6)]
        rec = {"x0": x}
        j = i // 2
        nxt_names, nxt_plan = gather_under_core(i)
        if i % 2 == 0:
            w2p = jnp.pad(sm["gla_w_gk2"][j], ((0, 128 - GLA_RANK), (0, 0)))
            bgk, gn = _vec(sm["gla_b_gk"][j]), _vec(sm["gla_g_norm"][j])
            if i == 0:
                proj, got0 = mm_nn(x, wts["gla_w_in", j], 0, pro="mod", p1=sc1, p2=sh1, tm=512, tn=GLA_INP,
                                   comm=gather_of(rest_names), name=f"gla_proj_{i}")
                keep_gathered(rest_names, got0)
            else:
                proj = mm_nn(x, wts["gla_w_in", j], 0, pro="mod", p1=sc1, p2=sh1, tm=512, tn=GLA_INP,
                             name=f"gla_proj_{i}")
            (og, states), got = gla_fwd(proj, w2p, bgk, gn, comm=nxt_plan, name=f"gla_core_{i}")
            y = mm_nn(og, wts["gla_w_out", j], 0, tm=1024, tn=1024, name=f"gla_out_{i}")
            rec.update(kind="gla", j=j, w2p=w2p, bgk=bgk, gn=gn, proj=proj, og=og, states=states)
        else:
            rel = sm["att_rel_bias"][j]
            rel_pad = jnp.pad(rel, ((0, 0), (0, REL_PAD - N_REL)), constant_values=NEG_INF)
            tiles = rel_bias_tiles(rel_pad, rel_idx, name=f"att_bias_{i}")
            tiles = tiles.reshape(ATT_H, len(REL_TILES), REL_TILE, REL_TILE)
            clip = jnp.broadcast_to(rel[:, 2 * MAX_REL][:, None, None], (ATT_H, 1, 128))
            qkv = mm_nn(x, wts["att_w_in", j], 0, pro="mod", p1=sc1, p2=sh1, bias=_vec(sm["att_b_in"][j]),
                        out_dtype=BF16, tm=512, tn=3 * D, name=f"att_proj_{i}")
            (o,), got = attn_fwd(qkv, tiles, clip, comm=nxt_plan, name=f"att_core_{i}")
            y = mm_nn(o, wts["att_w_out", j], 0, tm=1024, tn=1024, name=f"att_out_{i}")
            rec.update(kind="att", j=j, tiles=tiles, clip=clip, qkv=qkv, o=o)
        keep_gathered(nxt_names, got)
        x1 = ln_fwd(x, y, g1, _vec(sm["ln_g"][i, 0]), _vec(sm["ln_b"][i, 0]), name=f"ln_mix_{i}")
        h = mm_nn(x1, wts["ff_w1", i], 0, pro="mod", p1=sc2, p2=sh2, out_dtype=BF16, tm=512, tn=D_FF,
                  name=f"ff_up_{i}")
        y2 = mm_nn(h, wts["ff_w2", i], 0, pro="relu2", tm=512, tn=D, name=f"ff_down_{i}")
        x2 = ln_fwd(x1, y2, g2, _vec(sm["ln_g"][i, 1]), _vec(sm["ln_b"][i, 1]), name=f"ln_ff_{i}")
        rec.update(y=y, x1=x1, h=h, y2=y2)
        saved.append(rec)
        x = x2

    dy, loss = loss_head(x, target, name="loss_head")

    gw = {}

    def wgrad(weight, layer, a, d, *, tn, tk=1024, tm=512, col_block0=0, **kw):
        gw[weight, layer] = mm_tn(a, d, tk=tk, tn=tn, tm=tm, out_buf=gw.get((weight, layer)),
                                  out_shape=wts[weight, layer].shape, col_block0=col_block0, **kw)

    def scatter_layer(units):
        grads = []
        for n, l in units:
            g = gw[n, l]
            if KIND[n] == "stk":
                g = g[:, :, :GLA_IN].reshape(1, D, N_DEV, GLA_IN // N_DEV).transpose(2, 0, 1, 3)
            grads.append(g)
        return scatter_plan(grads, [KIND[n] for n, _ in units], [(1,) + tuple(w[n].shape[1:]) for n, _ in units])

    results = {n: None for n in BIG}

    def update(units, parts):
        for (n, l), p in zip(units, parts):
            results[n] = adamw_layer(w[n], m[n], v[n], p, l, results[n], name=f"adamw_{n}_{l}")

    gs = {"ln_g": [[None, None] for _ in range(DEPTH)], "ln_b": [[None, None] for _ in range(DEPTH)],
          "gla_w_gk2": [None] * 2, "gla_b_gk": [None] * 2, "gla_g_norm": [None] * 2, "att_b_in": [None] * 2,
          "att_rel_bias": [None] * 2}
    dmods = [[None] * 6 for _ in range(DEPTH)]
    nxt = None
    nxt_slot = None
    for i in reversed(range(DEPTH)):
        rec = saved[i]
        sh1, sc1, g1, sh2, sc2, g2 = [mods[i, k:k + 1] for k in range(6)]
        x0, x1 = rec["x0"], rec["x1"]
        if nxt is None:
            dz2, acc = ln_bwd(x1, rec["y2"], g2, _vec(sm["ln_g"][i, 1]), dout=dy, name=f"ln_ff_bwd_{i}")
        else:
            dz2, acc = ln_bwd(x1, rec["y2"], g2, _vec(sm["ln_g"][i, 1]), nxt=nxt[:3] + (_vec(sm["ln_b"][i, 1]),),
                              name=f"ln_ff_bwd_{i}")
            dmods[nxt_slot[0]][nxt_slot[1]] = acc[3]
            dmods[nxt_slot[0]][nxt_slot[2]] = acc[4]
        gs["ln_g"][i][1], gs["ln_b"][i][1], dmods[i][5] = acc[0], acc[1], acc[2]
        dh = mm_nt([dz2], wts["ff_w2", i], 0, pro="scale", p1=g2, epi_h=rec["h"], out_dtype=BF16, tm=2048, tn=512,
                   name=f"ff_down_bwd_{i}")
        wgrad("ff_w2", i, rec["h"], dz2, pro="relu2", dscale=g2, tk=2048, tn=1024, name=f"ff_w2_grad_{i}")
        du2 = mm_nt([dh], wts["ff_w1", i], 0, tm=1024, tn=1024, name=f"ff_up_bwd_{i}")
        wgrad("ff_w1", i, x1, dh, pro="mod", p1=sc2, p2=sh2, tn=2048, name=f"ff_w1_grad_{i}")
        dz1, acc = ln_bwd(x0, rec["y"], g1, _vec(sm["ln_g"][i, 0]), nxt=(dz2, du2, sc2, _vec(sm["ln_b"][i, 0])),
                          name=f"ln_mix_bwd_{i}")
        dmods[i][4], dmods[i][3] = acc[3], acc[4]
        gs["ln_g"][i][0], gs["ln_b"][i][0], dmods[i][2] = acc[0], acc[1], acc[2]
        j = rec["j"]
        w_in, w_out = _layer_weights(i)[:2]
        if rec["kind"] == "gla":
            dog = mm_nt([dz1], wts[w_out], 0, pro="scale", p1=g1, tm=1024, tn=1024, name=f"gla_out_bwd_{i}")
            wgrad(*w_out, rec["og"], dz1, dscale=g1, tn=1024, name=f"gla_wout_grad_{i}")
        else:
            do = mm_nt([dz1], wts[w_out], 0, pro="scale", p1=g1, out_dtype=BF16, tm=1024, tn=1024,
                       name=f"att_out_bwd_{i}")
            wgrad(*w_out, rec["o"], dz1, dscale=g1, tn=1024, name=f"att_wout_grad_{i}")
        units = [("ff_w1", i), ("ff_w2", i), w_out] + ([_layer_weights(i + 1)[0]] if i + 1 < DEPTH else [])
        plan = scatter_layer(units)
        if rec["kind"] == "gla":
            (dproj, dw2p, dbgk, dgn), parts = gla_bwd(rec["proj"], dog, rec["states"], rec["w2p"], rec["bgk"],
                                                      rec["gn"], comm=plan, name=f"gla_core_bwd_{i}")
            gs["gla_w_gk2"][j], gs["gla_b_gk"][j], gs["gla_g_norm"][j] = dw2p[:GLA_RANK], dbgk[0], dgn[0]
            wgrad(*w_in, x0, dproj, pro="mod", p1=sc1, p2=sh1, tk=512, tn=GLA_INP, name=f"gla_win_grad_{i}")
            if i == 0:
                du1, last_parts = mm_nt([dproj], wts[w_in], 0, tm=1024, tn=1024, comm=scatter_layer([w_in]),
                                        name=f"gla_proj_bwd_{i}")
                update([w_in], last_parts)
            else:
                du1 = mm_nt([dproj], wts[w_in], 0, tm=1024, tn=1024, name=f"gla_proj_bwd_{i}")
        else:
            (dq, dk, dv, dtiles, dclip, sq, sk, sv), parts = attn_bwd(rec["qkv"], do, rec["tiles"], rec["clip"],
                                                                      comm=plan, name=f"att_core_bwd_{i}")
            drel = rel_bias_grad(dtiles.reshape(ATT_H, -1), dclip.reshape(ATT_H, 128), rel_idx,
                                 name=f"att_bias_grad_{i}")
            gs["att_rel_bias"][j] = drel[:, :N_REL]
            gs["att_b_in"][j] = jnp.concatenate([sq[0], sk[0], sv[0]])
            du1 = mm_nt([dq, dk, dv], wts[w_in], 0, tm=512, tn=1024, name=f"att_proj_bwd_{i}")
            for n, t in enumerate((dq, dk, dv)):
                wgrad(*w_in, x0, t, pro="mod", p1=sc1, p2=sh1, tn=1024, col_block0=n, name=f"att_win_grad_{i}_{n}")
        update(units, parts)
        nxt = (dz1, du1, sc1, x0)
        nxt_slot = (i, 1, 0)
    dx, acc = combine_final(nxt[0], nxt[1], nxt[2], nxt[3], name="grad_x")
    dmods[0][1], dmods[0][0] = acc[3], acc[4]
    dmods = jnp.stack([jnp.stack(r) for r in dmods])
    gs = {k: jnp.stack([jnp.stack(r) if isinstance(r, list) else r for r in v]) for k, v in gs.items()}
    return loss, dx, dmods, gs, results


WEIGHTS = ("w_ada", "b_ada", "ln_g", "ln_b", "gla_w_in", "gla_w_gk2", "gla_b_gk", "gla_g_norm", "gla_w_out",
           "att_w_in", "att_b_in", "att_rel_bias", "att_w_out", "ff_w1", "ff_w2")
SMALL_SHARDED = {"ln_g": 2, "ln_b": 2, "gla_w_gk2": 2, "gla_g_norm": 2, "att_b_in": 1}
SMALL_GRADS = ("ln_g", "ln_b", "gla_w_gk2", "gla_b_gk", "gla_g_norm", "att_b_in", "att_rel_bias")


def kernel(x, c, w_ada, b_ada, ln_g, ln_b, gla_w_in, gla_w_gk2, gla_b_gk, gla_g_norm, gla_w_out, att_w_in, att_b_in, att_rel_bias, att_w_out, ff_w1, ff_w2, loss_target, m_w_ada, m_b_ada, m_ln_g, m_ln_b, m_gla_w_in, m_gla_w_gk2, m_gla_b_gk, m_gla_g_norm, m_gla_w_out, m_att_w_in, m_att_b_in, m_att_rel_bias, m_att_w_out, m_ff_w1, m_ff_w2, v_w_ada, v_b_ada, v_ln_g, v_ln_b, v_gla_w_in, v_gla_w_gk2, v_gla_b_gk, v_gla_g_norm, v_gla_w_out, v_att_w_in, v_att_b_in, v_att_rel_bias, v_att_w_out, v_ff_w1, v_ff_w2):
    w = dict(w_ada=w_ada, b_ada=b_ada, ln_g=ln_g, ln_b=ln_b, gla_w_in=gla_w_in, gla_w_gk2=gla_w_gk2,
             gla_b_gk=gla_b_gk, gla_g_norm=gla_g_norm, gla_w_out=gla_w_out, att_w_in=att_w_in, att_b_in=att_b_in,
             att_rel_bias=att_rel_bias, att_w_out=att_w_out, ff_w1=ff_w1, ff_w2=ff_w2)
    m = dict(w_ada=m_w_ada, b_ada=m_b_ada, ln_g=m_ln_g, ln_b=m_ln_b, gla_w_in=m_gla_w_in, gla_w_gk2=m_gla_w_gk2,
             gla_b_gk=m_gla_b_gk, gla_g_norm=m_gla_g_norm, gla_w_out=m_gla_w_out, att_w_in=m_att_w_in,
             att_b_in=m_att_b_in, att_rel_bias=m_att_rel_bias, att_w_out=m_att_w_out, ff_w1=m_ff_w1, ff_w2=m_ff_w2)
    v = dict(w_ada=v_w_ada, b_ada=v_b_ada, ln_g=v_ln_g, ln_b=v_ln_b, gla_w_in=v_gla_w_in, gla_w_gk2=v_gla_w_gk2,
             gla_b_gk=v_gla_b_gk, gla_g_norm=v_gla_g_norm, gla_w_out=v_gla_w_out, att_w_in=v_att_w_in,
             att_b_in=v_att_b_in, att_rel_bias=v_att_rel_bias, att_w_out=v_att_w_out, ff_w1=v_ff_w1, ff_w2=v_ff_w2)
    xi, yi, ci = _my_place()
    me = 4 * xi + 2 * yi + ci

    small_names = tuple(SMALL_SHARDED)
    small_in = _pack_small([c] + [w[n] for n in small_names])
    small_all = all_gather(small_in, name="gather_small").reshape(N_DEV, -1, 128)
    parts = _unpack_small(small_all, [c.shape] + [w[n].shape for n in small_names])
    c_all = parts[0].reshape(N_DEV, D)
    sm = {"gla_b_gk": gla_b_gk, "att_rel_bias": att_rel_bias}
    for n, p in zip(small_names, parts[1:]):
        ax = SMALL_SHARDED[n]
        sm[n] = jnp.moveaxis(p, 0, ax).reshape(p.shape[1:ax + 1] + (N_DEV * p.shape[ax + 1],) + p.shape[ax + 2:])

    n_ada = w_ada.shape[2]
    mp = mods_partial(c_all, w_ada, name="mods_partial")
    mp_all = all_gather(mp.reshape(DEPTH * N_DEV, n_ada), name="gather_mods")
    mp_all = mp_all.reshape(N_DEV, DEPTH, N_DEV, n_ada)
    mods = lax.dynamic_index_in_dim(mp_all, me, axis=2, keepdims=False)
    mods = mods.transpose(1, 0, 2).reshape(DEPTH, 6 * D) + b_ada
    mods = mods.reshape(DEPTH, 6, D)

    loss, dx, dmods, gs, results = _trunk(x.reshape(x.shape[1:]), loss_target.reshape(x.shape[1:]), mods, sm, w, m, v)
    loss = lax.psum(loss[0, 0], ("x", "y", "c"))

    dm_flat = dmods.reshape(DEPTH, 6 * D)
    small_g = [dm_flat] + [gs[n].reshape(sm[n].shape) for n in SMALL_GRADS]
    small_shapes = [a.shape for a in small_g]
    sg_all = all_gather(_pack_small(small_g), name="gather_small_grads").reshape(N_DEV, -1, 128)
    summed = _unpack_small(sum_parts(sg_all, name="sum_small_grads"), small_shapes)
    g_full = dict(zip(("b_ada",) + SMALL_GRADS, summed))
    dm_all = _unpack_small(sg_all, small_shapes)[0]
    dm_mine = lax.dynamic_slice_in_dim(dm_all, me * n_ada, n_ada, axis=2).transpose(1, 0, 2)
    g_w_ada = w_ada_grad(c_all, dm_mine, name="w_ada_grad")

    results["w_ada"] = adamw_nd(w_ada, m_w_ada, v_w_ada, g_w_ada[None], name="adamw_w_ada")
    for n in ("b_ada",) + SMALL_GRADS:
        g = g_full[n]
        if n in SMALL_SHARDED:
            ax = SMALL_SHARDED[n]
            width = w[n].shape[ax]
            g = lax.dynamic_slice_in_dim(g, me * width, width, axis=ax)
        results[n] = adamw_nd(w[n], m[n], v[n], g[None], name=f"adamw_{n}")

    out = [loss, dx[None]]
    for k in range(4):
        out += [results[n][k] for n in WEIGHTS]
    return tuple(out)
```

```python
import numpy as np
import jax
import jax.numpy as jnp
from jax import lax
from jax.experimental import pallas as pl
from jax.experimental.pallas import tpu as pltpu

F32 = jnp.float32
BF16 = jnp.bfloat16
HIGHEST = lax.Precision.HIGHEST
MESH = pl.DeviceIdType.MESH

N_DEV = 8
D = 1024
DEPTH = 4
CHUNK = 64
ALPHA = (2.0 * DEPTH) ** 0.25
LN_EPS = 1e-5
RMS_EPS = 1e-6
NEG_INF = -1e30

GLA_H = 4
GLA_DKH = 128
GLA_DVH = 256
GLA_DK = GLA_H * GLA_DKH
GLA_DV = GLA_H * GLA_DVH
GLA_RANK = 16
GLA_IN = 2 * GLA_DK + 2 * GLA_DV + GLA_RANK
GLA_INP = 3200
GLA_TAU_INV = 1.0 / 16.0
GLA_SUB = 2

ATT_H = 16
ATT_HD = 64
ATT_QB = 256
ATT_KB = 3 * ATT_QB
LEFT = 8 * CHUNK
MAX_REL = 128
N_REL = 2 * MAX_REL + 1
REL_PAD = 384
REL_TILE = 128
REL_TILES = (3, 4)
D_FF = 4 * D

ADAM_LR = 0.001
ADAM_B1 = 0.9
ADAM_B2 = 0.999
ADAM_EPS = 1e-08
ADAM_WD = 0.01
ADAM_STEP = 10

VMEM_LIMIT = 48 * 1024 * 1024


def _params(n_axes):
    return pltpu.CompilerParams(dimension_semantics=("arbitrary",) * n_axes, vmem_limit_bytes=VMEM_LIMIT)


def _dot(a, b):
    return jnp.dot(a, b, preferred_element_type=F32)


def _dot_nt(a, b):
    return lax.dot_general(a, b, (((1,), (1,)), ((), ())), preferred_element_type=F32)


def _dot_tn(a, b):
    return lax.dot_general(a, b, (((0,), (0,)), ((), ())), preferred_element_type=F32)


def _bf(a):
    return a.astype(BF16)


def _prologue(kind, a, p1=None, p2=None):
    if kind == "mod":
        return a.astype(F32) * (1.0 + p1) + p2
    if kind == "scale":
        return a.astype(F32) * (1.0 + p1)
    if kind == "relu2":
        r = jnp.maximum(a, 0.0)
        return r * r
    return a


class Hosted:
    def __init__(self, inputs, out_shapes, sems, first, last, mid=None, last_after_mid=None):
        self.inputs, self.out_shapes, self.sems, self.first, self.last = inputs, out_shapes, sems, first, last
        self.mid, self.last_after_mid = mid, last_after_mid


def _hbm_specs(n):
    return [pl.BlockSpec(memory_space=pltpu.HBM)] * n


def _call_hosting(body, comm, *, first, last, in_specs, out_specs, out_shape, scratch_shapes, args, mid=None, **kw):
    if comm is None:
        return pl.pallas_call(body, in_specs=in_specs, out_specs=out_specs, out_shape=out_shape,
                              scratch_shapes=scratch_shapes, **kw)(*args), []
    n_in, n_out, n_scr = len(in_specs), len(out_specs), len(scratch_shapes)
    n_ci, n_co = len(comm.inputs), len(comm.out_shapes)

    def hosting(*refs):
        ins, ci = refs[:n_in], refs[n_in:n_in + n_ci]
        k = n_in + n_ci
        outs, co = refs[k:k + n_out], refs[k + n_out:k + n_out + n_co]
        k += n_out + n_co
        scr, cs = refs[k:k + n_scr], refs[k + n_scr:]

        @pl.when(first())
        def _():
            comm.first(ci, co, cs)

        body(*ins, *outs, *scr)
        split = mid is not None and comm.mid is not None
        if split:
            @pl.when(mid())
            def _():
                comm.mid(ci, co, cs)

        @pl.when(last())
        def _():
            (comm.last_after_mid if split else comm.last)(ci, co, cs)

    res = pl.pallas_call(
        hosting, in_specs=list(in_specs) + _hbm_specs(n_ci), out_specs=list(out_specs) + _hbm_specs(n_co),
        out_shape=list(out_shape) + list(comm.out_shapes), scratch_shapes=list(scratch_shapes) + list(comm.sems),
        **kw)(*args, *comm.inputs)
    return res[:n_out], res[n_out:]


def run_hosted(comm, *, name):
    n_i, n_o = len(comm.inputs), len(comm.out_shapes)

    def body(*refs):
        ins, outs, sems = refs[:n_i], refs[n_i:n_i + n_o], refs[n_i + n_o:]
        comm.first(ins, outs, sems)
        comm.last(ins, outs, sems)

    return pl.pallas_call(body, name=name, out_shape=list(comm.out_shapes), in_specs=_hbm_specs(n_i),
                          out_specs=_hbm_specs(n_o), scratch_shapes=list(comm.sems))(*comm.inputs)


def mm_nn(a, b, layer, *, pro=None, p1=None, p2=None, bias=None, out_dtype=F32, tm, tn, comm=None, name):
    M, K = a.shape
    N = b.shape[2]
    tm = min(tm, M)
    n_p = {"mod": 2, "scale": 1}.get(pro, 0)
    has_bias = bias is not None
    direct = pro is None and a.dtype == BF16

    def body(*refs):
        a_ref, b_ref = refs[0], refs[1]
        p_refs = refs[2:2 + n_p]
        bias_ref = refs[2 + n_p] if has_bias else None
        if direct:
            o_ref = refs[-1]
            lhs = a_ref[...]
        else:
            o_ref, abf = refs[-2], refs[-1]

            @pl.when(pl.program_id(1) == 0)
            def _():
                abf[...] = _bf(_prologue(pro, a_ref[...], *[r[...] for r in p_refs]))

            lhs = abf[...]
        acc = _dot(lhs, b_ref[...])
        if has_bias:
            acc = acc + bias_ref[...]
        o_ref[...] = acc.astype(out_dtype)

    in_specs = [pl.BlockSpec((tm, K), lambda i, j: (i, 0)), pl.BlockSpec((None, K, tn), lambda i, j: (layer, 0, j))]
    args = [a, b]
    for p in (p1, p2)[:n_p]:
        in_specs.append(pl.BlockSpec((1, K), lambda i, j: (0, 0)))
        args.append(p)
    if has_bias:
        in_specs.append(pl.BlockSpec((1, tn), lambda i, j: (0, j)))
        args.append(bias)
    n_i, n_j = M // tm, N // tn
    (out,), got = _call_hosting(
        body, comm, first=lambda: (pl.program_id(0) == 0) & (pl.program_id(1) == 0),
        last=lambda: (pl.program_id(0) == n_i - 1) & (pl.program_id(1) == n_j - 1),
        name=name, grid=(n_i, n_j), in_specs=in_specs,
        out_specs=[pl.BlockSpec((tm, tn), lambda i, j: (i, j))],
        out_shape=[jax.ShapeDtypeStruct((M, N), out_dtype)],
        scratch_shapes=[] if direct else [pltpu.VMEM((tm, K), BF16)], compiler_params=_params(2), args=args)
    return out if comm is None else (out, got)


def mm_nt(a_parts, w, layer, *, pro=None, p1=None, epi_h=None, out_dtype=F32, tm, tn, comm=None, name):
    M = a_parts[0].shape[0]
    tm = min(tm, M)
    widths = [p.shape[1] for p in a_parts]
    Nw = sum(widths)
    Kw = w.shape[1]
    n_a = len(a_parts)
    has_p = pro == "scale"
    has_h = epi_h is not None
    direct = n_a == 1 and not has_p and a_parts[0].dtype == BF16

    def body(*refs):
        a_refs = refs[:n_a]
        w_ref = refs[n_a]
        k = n_a + 1
        p_ref = refs[k] if has_p else None
        k += int(has_p)
        h_ref = refs[k] if has_h else None
        if direct:
            o_ref = refs[-1]
            lhs = a_refs[0][...]
        else:
            o_ref, abf = refs[-2], refs[-1]

            @pl.when(pl.program_id(1) == 0)
            def _():
                off = 0
                for r, wd in zip(a_refs, widths):
                    av = r[...]
                    if has_p:
                        av = av.astype(F32) * (1.0 + p_ref[...])
                    abf[:, off:off + wd] = _bf(av)
                    off += wd

            lhs = abf[...]
        acc = _dot_nt(lhs, w_ref[...])
        if has_h:
            acc = acc * (2.0 * jnp.maximum(h_ref[...], 0.0))
        o_ref[...] = acc.astype(out_dtype)

    in_specs = [pl.BlockSpec((tm, wd), lambda i, j: (i, 0)) for wd in widths]
    in_specs.append(pl.BlockSpec((None, tn, Nw), lambda i, j: (layer, j, 0)))
    args = list(a_parts) + [w]
    if has_p:
        in_specs.append(pl.BlockSpec((1, Nw), lambda i, j: (0, 0)))
        args.append(p1)
    if has_h:
        in_specs.append(pl.BlockSpec((tm, tn), lambda i, j: (i, j)))
        args.append(epi_h)
    n_i, n_j = M // tm, Kw // tn
    (out,), got = _call_hosting(
        body, comm, first=lambda: (pl.program_id(0) == 0) & (pl.program_id(1) == 0),
        last=lambda: (pl.program_id(0) == n_i - 1) & (pl.program_id(1) == n_j - 1),
        name=name, grid=(n_i, n_j), in_specs=in_specs,
        out_specs=[pl.BlockSpec((tm, tn), lambda i, j: (i, j))],
        out_shape=[jax.ShapeDtypeStruct((M, Kw), out_dtype)],
        scratch_shapes=[] if direct else [pltpu.VMEM((tm, Nw), BF16)], compiler_params=_params(2), args=args)
    return out if comm is None else (out, got)


def mm_tn(a, d, *, pro=None, p1=None, p2=None, dscale=None, tk, tn, tm, out_buf, out_shape, col_block0=0, name):
    M, Kf = a.shape
    N = d.shape[1]
    n_p = {"mod": 2}.get(pro, 0)
    has_ds = dscale is not None
    has_buf = out_buf is not None
    n_m = M // tm

    def body(*refs):
        a_ref, d_ref = refs[0], refs[1]
        p_refs = refs[2:2 + n_p]
        ds_ref = refs[2 + n_p] if has_ds else None
        o_ref, acc = refs[-2], refs[-1]
        m = pl.program_id(2)

        @pl.when(m == 0)
        def _():
            acc[...] = jnp.zeros_like(acc)

        av = _prologue(pro, a_ref[...], *[r[...] for r in p_refs])
        dv = d_ref[...]
        if has_ds:
            dv = dv.astype(F32) * (1.0 + ds_ref[...])
        acc[...] += _dot_tn(_bf(av), _bf(dv))

        @pl.when(m == n_m - 1)
        def _():
            o_ref[...] = _bf(acc[...])

    in_specs = [pl.BlockSpec((tm, tk), lambda i, j, m: (m, i)), pl.BlockSpec((tm, tn), lambda i, j, m: (m, j))]
    args = [a, d]
    for p in (p1, p2)[:n_p]:
        in_specs.append(pl.BlockSpec((1, tk), lambda i, j, m: (0, i)))
        args.append(p)
    if has_ds:
        in_specs.append(pl.BlockSpec((1, tn), lambda i, j, m: (0, j)))
        args.append(dscale)
    aliases = {}
    if has_buf:
        in_specs.append(pl.BlockSpec(memory_space=pl.ANY))
        args.append(out_buf)
        aliases = {len(args) - 1: 0}
    return pl.pallas_call(
        body, name=name, grid=(Kf // tk, N // tn, n_m), in_specs=in_specs,
        out_specs=pl.BlockSpec((None, tk, tn), lambda i, j, m: (0, i, col_block0 + j)),
        out_shape=jax.ShapeDtypeStruct(out_shape, BF16), input_output_aliases=aliases,
        scratch_shapes=[pltpu.VMEM((tk, tn), F32)], compiler_params=_params(3),
    )(*args)


ROW_BLOCK = 512
ACC_ROWS = 8


def _ln_stats(z):
    mu = jnp.mean(z, axis=-1, keepdims=True)
    zc = z - mu
    var = jnp.mean(zc * zc, axis=-1, keepdims=True)
    return zc, lax.rsqrt(var + LN_EPS)


def ln_fwd(x, y, gate, lng, lnb, *, name):
    S = x.shape[0]

    def body(x_ref, y_ref, gt_ref, g_ref, b_ref, o_ref):
        z = ALPHA * x_ref[...] + (1.0 + gt_ref[...]) * y_ref[...]
        zc, rstd = _ln_stats(z)
        o_ref[...] = (zc * rstd) * g_ref[...] + b_ref[...]

    row = pl.BlockSpec((ROW_BLOCK, D), lambda i: (i, 0))
    vec = pl.BlockSpec((1, D), lambda i: (0, 0))
    return pl.pallas_call(
        body, name=name, grid=(S // ROW_BLOCK,), in_specs=[row, row, vec, vec, vec], out_specs=row,
        out_shape=jax.ShapeDtypeStruct((S, D), F32), compiler_params=_params(1),
    )(x, y, gate, lng, lnb)


def _add_colsum(acc_ref, r, val):
    acc_ref[r:r + 1, :] += jnp.sum(val, axis=0, keepdims=True)


def ln_bwd(x_in, y, gate, lng, *, loss=None, nxt=None, name):
    S = x_in.shape[0]
    has_next = nxt is not None

    def body(*refs):
        if has_next:
            dzn_ref, dun_ref, scn_ref, b_ref = refs[:4]
            k = 4
        else:
            t_ref, b_ref = refs[:2]
            k = 2
        x_ref, y_ref, gt_ref, g_ref = refs[k:k + 4]
        dz_ref, acc_ref = refs[k + 4:k + 6]

        @pl.when(pl.program_id(0) == 0)
        def _():
            acc_ref[...] = jnp.zeros_like(acc_ref)
            if not has_next:
                refs[k + 6][...] = jnp.zeros_like(refs[k + 6])

        yv = y_ref[...]
        z = ALPHA * x_ref[...] + (1.0 + gt_ref[...]) * yv
        zc, rstd = _ln_stats(z)
        xhat = zc * rstd
        if has_next:
            du = dun_ref[...]
            dout_v = ALPHA * dzn_ref[...] + du * (1.0 + scn_ref[...])
            _add_colsum(acc_ref, 3, du * (xhat * g_ref[...] + b_ref[...]))
            _add_colsum(acc_ref, 4, du)
        else:
            e = (xhat * g_ref[...] + b_ref[...]) - t_ref[...]
            dout_v = e * (1.0 / D)
            per_tok = jnp.sum(e * e, axis=1, keepdims=True) * (1.0 / D)
            refs[k + 6][...] += 0.5 * jnp.sum(per_tok, axis=0, keepdims=True)
        _add_colsum(acc_ref, 0, dout_v * xhat)
        _add_colsum(acc_ref, 1, dout_v)
        dxh = dout_v * g_ref[...]
        m1 = jnp.mean(dxh, axis=-1, keepdims=True)
        m2 = jnp.mean(dxh * xhat, axis=-1, keepdims=True)
        dz = rstd * (dxh - m1 - xhat * m2)
        _add_colsum(acc_ref, 2, dz * yv)
        dz_ref[...] = dz

    row = pl.BlockSpec((ROW_BLOCK, D), lambda i: (i, 0))
    vec = pl.BlockSpec((1, D), lambda i: (0, 0))
    if has_next:
        in_specs = [row, row, vec, vec]
        args = list(nxt)
    else:
        in_specs = [row, vec]
        args = list(loss)
    in_specs += [row, row, vec, vec]
    args += [x_in, y, gate, lng]
    out_specs = [row, pl.BlockSpec((ACC_ROWS, D), lambda i: (0, 0))]
    out_shape = [jax.ShapeDtypeStruct((S, D), F32), jax.ShapeDtypeStruct((ACC_ROWS, D), F32)]
    if not has_next:
        out_specs.append(pl.BlockSpec((8, 128), lambda i: (0, 0)))
        out_shape.append(jax.ShapeDtypeStruct((8, 128), F32))
    return pl.pallas_call(
        body, name=name, grid=(S // ROW_BLOCK,), in_specs=in_specs, out_specs=out_specs, out_shape=out_shape,
        compiler_params=_params(1),
    )(*args)


def combine_final(dz, du, sc, x_in, *, name):
    S = dz.shape[0]

    def body(dz_ref, du_ref, sc_ref, x_ref, dx_ref, acc_ref):
        @pl.when(pl.program_id(0) == 0)
        def _():
            acc_ref[...] = jnp.zeros_like(acc_ref)

        du_v = du_ref[...]
        dx_ref[...] = ALPHA * dz_ref[...] + du_v * (1.0 + sc_ref[...])
        _add_colsum(acc_ref, 3, du_v * x_ref[...])
        _add_colsum(acc_ref, 4, du_v)

    row = pl.BlockSpec((ROW_BLOCK, D), lambda i: (i, 0))
    vec = pl.BlockSpec((1, D), lambda i: (0, 0))
    return pl.pallas_call(
        body, name=name, grid=(S // ROW_BLOCK,), in_specs=[row, row, vec, row],
        out_specs=[row, pl.BlockSpec((ACC_ROWS, D), lambda i: (0, 0))],
        out_shape=[jax.ShapeDtypeStruct((S, D), F32), jax.ShapeDtypeStruct((ACC_ROWS, D), F32)],
        compiler_params=_params(1),
    )(dz, du, sc, x_in)


def _log_sigmoid(x):
    return jnp.minimum(x, 0.0) - jnp.log(1.0 + jnp.exp(-jnp.abs(x)))


def _silu(x):
    return x * (1.0 / (1.0 + jnp.exp(-x)))


def _cumsum_steps(x):
    row = lax.broadcasted_iota(jnp.int32, x.shape, 0)
    step = 1
    while step < x.shape[0]:
        x = x + jnp.where(row >= step, pltpu.roll(x, step, 0), 0.0)
        step *= 2
    return x


@jax.custom_vjp
def _cumsum_rows(x):
    return _cumsum_steps(x)


def _cumsum_rows_fwd(x):
    return _cumsum_steps(x), None


def _cumsum_rows_bwd(_, g):
    return (jnp.sum(g, axis=0, keepdims=True) - _cumsum_steps(g) + g,)


_cumsum_rows.defvjp(_cumsum_rows_fwd, _cumsum_rows_bwd)


def _gla_chunk(q, k, v, g, gk, s0t, w2p, bgk, gn):
    C = q.shape[0]
    row = lax.broadcasted_iota(jnp.int32, (C, C), 0)
    col = lax.broadcasted_iota(jnp.int32, (C, C), 1)
    lower = row >= col
    la = _log_sigmoid(_dot(_bf(gk), _bf(w2p)) + bgk) * GLA_TAU_INV
    outs, states = [], []
    for h in range(GLA_H):
        ks = slice(h * GLA_DKH, (h + 1) * GLA_DKH)
        vs = slice(h * GLA_DVH, (h + 1) * GLA_DVH)
        qh = q[:, ks] * (GLA_DKH ** -0.5)
        kh, vh, gh, lah, s0 = k[:, ks], v[:, vs], g[:, vs], la[:, ks], s0t[h]
        cum = _cumsum_rows(lah)
        e_pos = jnp.exp(cum)
        e_neg = jnp.exp(-cum)
        q_f = qh * e_pos
        a_f = _dot_nt(_bf(q_f), _bf(kh * e_neg))
        a_b = _dot_nt(_bf(qh * e_neg), _bf(kh * e_pos))
        att = jnp.where(lower, a_f, a_b)
        o = _dot(_bf(att), _bf(vh)) + _dot_nt(_bf(q_f), _bf(s0))
        tot = jnp.sum(lah, axis=0, keepdims=True)
        k_end = kh * jnp.exp(tot - cum)
        states.append(s0 * jnp.exp(tot) + _dot_tn(_bf(vh), _bf(k_end)))
        on = o * lax.rsqrt(jnp.mean(o * o, axis=-1, keepdims=True) + RMS_EPS) * gn[:, vs]
        outs.append(on * _silu(gh))
    return jnp.concatenate(outs, axis=1), tuple(states)


def _gla_split(p):
    return (p[:, 0:GLA_DK], p[:, GLA_DK:2 * GLA_DK], p[:, 2 * GLA_DK:2 * GLA_DK + GLA_DV],
            p[:, 2 * GLA_DK + GLA_DV:2 * GLA_DK + 2 * GLA_DV], p[:, 2 * GLA_DK + 2 * GLA_DV:GLA_INP])


def gla_fwd(proj, w2p, bgk, gn, *, comm=None, name):
    S = proj.shape[0]
    n_c = S // CHUNK
    n_s = n_c // GLA_SUB
    rows = GLA_SUB * CHUNK

    def body(p_ref, w_ref, b_ref, gn_ref, o_ref, st_ref, st):
        @pl.when(pl.program_id(0) == 0)
        def _():
            st[...] = jnp.zeros_like(st)

        s = tuple(st[h] for h in range(GLA_H))
        for u in range(GLA_SUB):
            sub = slice(u * CHUNK, (u + 1) * CHUNK)
            for h in range(GLA_H):
                st_ref[u, h] = s[h]
            og, s = _gla_chunk(*_gla_split(p_ref[sub, :]), s, w_ref[...], b_ref[...], gn_ref[...])
            o_ref[sub, :] = _bf(og)
        for h in range(GLA_H):
            st[h] = s[h]

    full = lambda shape: pl.BlockSpec(shape, lambda i: (0,) * len(shape))
    return _call_hosting(
        body, comm, first=lambda: pl.program_id(0) == 0, last=lambda: pl.program_id(0) == n_s - 1,
        name=name, grid=(n_s,),
        in_specs=[pl.BlockSpec((rows, GLA_INP), lambda i: (i, 0)), full((128, GLA_DK)), full((1, GLA_DK)),
                  full((1, GLA_DV))],
        out_specs=[pl.BlockSpec((rows, GLA_DV), lambda i: (i, 0)),
                   pl.BlockSpec((GLA_SUB, GLA_H, GLA_DVH, GLA_DKH), lambda i: (i, 0, 0, 0))],
        out_shape=[jax.ShapeDtypeStruct((S, GLA_DV), BF16),
                   jax.ShapeDtypeStruct((n_c, GLA_H, GLA_DVH, GLA_DKH), F32)],
        scratch_shapes=[pltpu.VMEM((GLA_H, GLA_DVH, GLA_DKH), F32)], compiler_params=_params(1),
        args=(proj, w2p, bgk, gn))


def gla_bwd(proj, dog, states, w2p, bgk, gn, *, comm=None, name):
    S = proj.shape[0]
    n_c = S // CHUNK
    n_s = n_c // GLA_SUB
    rows = GLA_SUB * CHUNK

    def body(p_ref, dog_ref, st_ref, w_ref, b_ref, gn_ref, dp_ref, dw_ref, db_ref, dgn_ref, ds_ref):
        @pl.when(pl.program_id(0) == 0)
        def _():
            ds_ref[...] = jnp.zeros_like(ds_ref)
            dw_ref[...] = jnp.zeros_like(dw_ref)
            db_ref[...] = jnp.zeros_like(db_ref)
            dgn_ref[...] = jnp.zeros_like(dgn_ref)

        ds = tuple(ds_ref[h] for h in range(GLA_H))
        for u in reversed(range(GLA_SUB)):
            sub = slice(u * CHUNK, (u + 1) * CHUNK)
            q, k, v, g, gk = _gla_split(p_ref[sub, :])
            s0 = tuple(st_ref[u, h] for h in range(GLA_H))
            _, vjp = jax.vjp(_gla_chunk, q, k, v, g, gk, s0, w_ref[...], b_ref[...], gn_ref[...])
            dq, dk, dv, dg, dgk, ds, dw, db, dgn = vjp((dog_ref[sub, :], ds))
            dp_ref[sub, 0:GLA_DK] = _bf(dq)
            dp_ref[sub, GLA_DK:2 * GLA_DK] = _bf(dk)
            dp_ref[sub, 2 * GLA_DK:2 * GLA_DK + GLA_DV] = _bf(dv)
            dp_ref[sub, 2 * GLA_DK + GLA_DV:2 * GLA_DK + 2 * GLA_DV] = _bf(dg)
            dp_ref[sub, 2 * GLA_DK + 2 * GLA_DV:GLA_INP] = _bf(dgk)
            dw_ref[...] += dw
            db_ref[...] += db
            dgn_ref[...] += dgn
        for h in range(GLA_H):
            ds_ref[h] = ds[h]

    full = lambda shape: pl.BlockSpec(shape, lambda i: (0,) * len(shape))
    rev = lambda i: (n_s - 1 - i, 0)
    return _call_hosting(
        body, comm, first=lambda: pl.program_id(0) == 0, last=lambda: pl.program_id(0) == n_s - 1,
        name=name, grid=(n_s,),
        in_specs=[pl.BlockSpec((rows, GLA_INP), rev), pl.BlockSpec((rows, GLA_DV), rev),
                  pl.BlockSpec((GLA_SUB, GLA_H, GLA_DVH, GLA_DKH), lambda i: (n_s - 1 - i, 0, 0, 0)),
                  full((128, GLA_DK)), full((1, GLA_DK)), full((1, GLA_DV))],
        out_specs=[pl.BlockSpec((rows, GLA_INP), rev), full((128, GLA_DK)), full((1, GLA_DK)), full((1, GLA_DV))],
        out_shape=[jax.ShapeDtypeStruct((S, GLA_INP), BF16), jax.ShapeDtypeStruct((128, GLA_DK), F32),
                   jax.ShapeDtypeStruct((1, GLA_DK), F32), jax.ShapeDtypeStruct((1, GLA_DV), F32)],
        scratch_shapes=[pltpu.VMEM((GLA_H, GLA_DVH, GLA_DKH), F32)], compiler_params=_params(1),
        args=(proj, dog, states, w2p, bgk, gn))


def _rel_index():
    t = np.arange(REL_TILE)[:, None]
    j = np.arange(REL_TILE)[None, :]
    tiles = []
    for m in REL_TILES:
        chunks = (REL_TILE // CHUNK) * m + j // CHUNK - t // CHUNK
        band = (chunks >= 0) & (chunks <= LEFT // CHUNK)
        dist = LEFT - REL_TILE * m + t - j
        tiles.append(np.where(band, np.minimum(dist, MAX_REL) + MAX_REL, N_REL))
    return jnp.asarray(np.stack(tiles).reshape(1, -1).astype(np.int32))


REL_BLOCK = 2048


def _one_hot(idx_row):
    return (lax.broadcasted_iota(jnp.int32, (REL_PAD, idx_row.shape[1]), 0) == idx_row).astype(F32)


def rel_bias_tiles(rel_pad, idx, *, name):
    E = idx.shape[1]

    def body(r_ref, i_ref, o_ref):
        o_ref[...] = jnp.dot(r_ref[...], _one_hot(i_ref[...]), precision=HIGHEST, preferred_element_type=F32)

    return pl.pallas_call(
        body, name=name, grid=(E // REL_BLOCK,),
        in_specs=[pl.BlockSpec((ATT_H, REL_PAD), lambda i: (0, 0)), pl.BlockSpec((1, REL_BLOCK), lambda i: (0, i))],
        out_specs=pl.BlockSpec((ATT_H, REL_BLOCK), lambda i: (0, i)),
        out_shape=jax.ShapeDtypeStruct((ATT_H, E), F32), compiler_params=_params(1),
    )(rel_pad, idx)


def rel_bias_grad(dtiles_flat, dclip, idx, *, name):
    E = idx.shape[1]
    n_steps = E // REL_BLOCK

    def body(d_ref, c_ref, i_ref, o_ref):
        @pl.when(pl.program_id(0) == 0)
        def _():
            o_ref[...] = jnp.zeros_like(o_ref)

        o_ref[...] += lax.dot_general(d_ref[...], _one_hot(i_ref[...]), (((1,), (1,)), ((), ())),
                                      precision=HIGHEST, preferred_element_type=F32)

        @pl.when(pl.program_id(0) == n_steps - 1)
        def _():
            at_clip = lax.broadcasted_iota(jnp.int32, (1, REL_PAD), 1) == 2 * MAX_REL
            o_ref[...] += jnp.where(at_clip, jnp.sum(c_ref[...], axis=1, keepdims=True), 0.0)

    return pl.pallas_call(
        body, name=name, grid=(n_steps,),
        in_specs=[pl.BlockSpec((ATT_H, REL_BLOCK), lambda i: (0, i)), pl.BlockSpec((ATT_H, 128), lambda i: (0, 0)),
                  pl.BlockSpec((1, REL_BLOCK), lambda i: (0, i))],
        out_specs=pl.BlockSpec((ATT_H, REL_PAD), lambda i: (0, 0)),
        out_shape=jax.ShapeDtypeStruct((ATT_H, REL_PAD), F32), compiler_params=_params(1),
    )(dtiles_flat, dclip, idx)


def _attn_bias(tiles, clip):
    const = jnp.broadcast_to(clip, (REL_TILE, REL_TILE))
    zero = jnp.zeros((REL_TILE, REL_TILE), F32)
    rows = []
    for qt in range(ATT_QB // REL_TILE):
        blocks = []
        for kt in range(ATT_KB // REL_TILE):
            m = kt - qt
            if m in REL_TILES:
                blocks.append(tiles[REL_TILES.index(m)])
            elif 0 <= m < REL_TILES[0]:
                blocks.append(const)
            else:
                blocks.append(zero)
        rows.append(jnp.concatenate(blocks, axis=1))
    return jnp.concatenate(rows, axis=0)


def _attn_bias_grad(ds, dt_ref, dc_ref, a):
    tile = lambda qt, kt: ds[qt * REL_TILE:(qt + 1) * REL_TILE, kt * REL_TILE:(kt + 1) * REL_TILE]
    const = None
    sums = [None] * len(REL_TILES)
    for qt in range(ATT_QB // REL_TILE):
        for kt in range(ATT_KB // REL_TILE):
            m = kt - qt
            if m in REL_TILES:
                n = REL_TILES.index(m)
                sums[n] = tile(qt, kt) if sums[n] is None else sums[n] + tile(qt, kt)
            elif 0 <= m < REL_TILES[0]:
                const = tile(qt, kt) if const is None else const + tile(qt, kt)
    for n, v in enumerate(sums):
        dt_ref[a, n] += v
    dc_ref[a] += jnp.sum(const, axis=0, keepdims=True)


def _attn_head_lanes():
    lane = lax.broadcasted_iota(jnp.int32, (1, 2 * ATT_HD), 1)
    return [(lane >= a * ATT_HD) & (lane < (a + 1) * ATT_HD) for a in range(2)]


def _attn_band_bias(tiles, clip):
    j = lax.broadcasted_iota(jnp.int32, (ATT_QB, ATT_KB), 1)
    t = lax.broadcasted_iota(jnp.int32, (ATT_QB, ATT_KB), 0)
    shift = CHUNK.bit_length() - 1
    chunks = lax.shift_right_logical(j, shift) - lax.shift_right_logical(t, shift)
    band = (chunks >= 0) & (chunks <= LEFT // CHUNK)
    return jnp.where(band, _attn_bias(tiles, clip), NEG_INF)


def _attn_exp(qa, kb, bias, key_bias):
    s = _dot_nt(qa, kb) + bias + key_bias
    e = jnp.exp(s - jnp.max(s, axis=-1, keepdims=True))
    return e, jnp.sum(e, axis=-1, keepdims=True)


def _attn_specs():
    n_hp = ATT_H // 2
    q_spec = pl.BlockSpec((ATT_QB, 128), lambda hp, g: (g, hp))

    def win(col0, back):
        return pl.BlockSpec((ATT_QB, 128), lambda hp, g: (jnp.maximum(g - back, 0), col0 + hp))

    kv_specs = [win(n_hp, 2), win(n_hp, 1), win(n_hp, 0), win(2 * n_hp, 2), win(2 * n_hp, 1), win(2 * n_hp, 0)]
    tiles_spec = pl.BlockSpec((2, len(REL_TILES), REL_TILE, REL_TILE), lambda hp, g: (hp, 0, 0, 0))
    clip_spec = pl.BlockSpec((2, 1, 128), lambda hp, g: (hp, 0, 0))
    return q_spec, kv_specs, tiles_spec, clip_spec


def _attn_window(refs, g):
    kb = jnp.concatenate([_bf(r[...]) for r in refs[0:3]], axis=0)
    vb = jnp.concatenate([_bf(r[...]) for r in refs[3:6]], axis=0)
    j = lax.broadcasted_iota(jnp.int32, (1, ATT_KB), 1)
    return kb, vb, jnp.where(j + (g - 2) * ATT_QB >= 0, 0.0, NEG_INF)


def attn_fwd(qkv, tiles, clip, *, comm=None, name):
    S = qkv.shape[0]
    q_spec, kv_specs, tiles_spec, clip_spec = _attn_specs()

    def body(q_ref, *rest):
        kv_refs, t_ref, c_ref, o_ref, bias = rest[:6], rest[6], rest[7], rest[8], rest[9]
        g = pl.program_id(1)

        @pl.when(g == 0)
        def _():
            for a in range(2):
                bias[a * ATT_QB:(a + 1) * ATT_QB, :] = _attn_band_bias(t_ref[a], c_ref[a])

        kb, vb, key_bias = _attn_window(kv_refs, g)
        q = q_ref[...].astype(F32)
        out = jnp.zeros((ATT_QB, 2 * ATT_HD), F32)
        for a, lanes in enumerate(_attn_head_lanes()):
            mf = lanes.astype(F32)
            e, l = _attn_exp(_bf(q * (mf * ATT_HD ** -0.5)), kb, bias[a * ATT_QB:(a + 1) * ATT_QB, :], key_bias)
            out = out + _dot(_bf(e), vb) * (mf * (1.0 / l))
        o_ref[...] = _bf(out)

    n_hp, n_g = ATT_H // 2, S // ATT_QB
    return _call_hosting(
        body, comm, first=lambda: (pl.program_id(0) == 0) & (pl.program_id(1) == 0),
        last=lambda: (pl.program_id(0) == n_hp - 1) & (pl.program_id(1) == n_g - 1),
        mid=lambda: (pl.program_id(0) == n_hp // 2 + 1) & (pl.program_id(1) == 0),
        name=name, grid=(n_hp, n_g), in_specs=[q_spec] + kv_specs + [tiles_spec, clip_spec],
        out_specs=[q_spec], out_shape=[jax.ShapeDtypeStruct((S, D), BF16)],
        scratch_shapes=[pltpu.VMEM((2 * ATT_QB, ATT_KB), F32)], compiler_params=_params(2),
        args=(*([qkv] * 7), tiles, clip))


def attn_bwd(qkv, do, tiles, clip, *, comm=None, name):
    S = qkv.shape[0]
    q_spec, kv_specs, tiles_spec, clip_spec = _attn_specs()
    col_spec = pl.BlockSpec((S, 128), lambda hp, g: (0, hp))
    sum_spec = pl.BlockSpec((1, 128), lambda hp, g: (0, hp))
    n_g = S // ATT_QB

    def body(q_ref, *rest):
        kv_refs, t_ref, c_ref, do_ref = rest[:6], rest[6], rest[7], rest[8]
        dq_ref, dk_ref, dv_ref, dt_ref, dc_ref, sq_ref, sk_ref, sv_ref, bias = rest[9:]
        g = pl.program_id(1)

        @pl.when(g == 0)
        def _():
            for a in range(2):
                bias[a * ATT_QB:(a + 1) * ATT_QB, :] = _attn_band_bias(t_ref[a], c_ref[a])
            dk_ref[...] = jnp.zeros_like(dk_ref)
            dv_ref[...] = jnp.zeros_like(dv_ref)
            dt_ref[...] = jnp.zeros_like(dt_ref)
            dc_ref[...] = jnp.zeros_like(dc_ref)
            sq_ref[...] = jnp.zeros_like(sq_ref)

        kb, vb, key_bias = _attn_window(kv_refs, g)
        q = q_ref[...].astype(F32)
        do = do_ref[...]
        lanes = _attn_head_lanes()
        mf = [m.astype(F32) * ATT_HD ** -0.5 for m in lanes]
        qs = _bf(jnp.concatenate([q * m for m in mf], axis=0))
        dos = jnp.concatenate([jnp.where(m, do, jnp.zeros_like(do)) for m in lanes], axis=0)
        e, l = _attn_exp(qs, kb, bias[...], key_bias)
        p = e * (1.0 / l)
        dp = _dot_nt(dos, vb)
        ds = p * (dp - jnp.sum(p * dp, axis=-1, keepdims=True))
        ds_b = _bf(ds)
        dq2 = _dot(ds_b, kb)
        dq = dq2[:ATT_QB] * mf[0] + dq2[ATT_QB:] * mf[1]
        dkw = _dot_tn(ds_b, qs)
        dvw = _dot_tn(_bf(p), dos)
        for a in range(2):
            _attn_bias_grad(ds[a * ATT_QB:(a + 1) * ATT_QB], dt_ref, dc_ref, a)
        dq_ref[...] = _bf(dq)
        sq_ref[...] += jnp.sum(dq, axis=0, keepdims=True)
        for blk in range(3):
            src = g - 2 + blk

            @pl.when(src >= 0)
            def _(blk=blk, src=src):
                rows = pl.ds(pl.multiple_of(src * ATT_QB, ATT_QB), ATT_QB)
                dk_ref[rows, :] += dkw[blk * ATT_QB:(blk + 1) * ATT_QB]
                dv_ref[rows, :] += dvw[blk * ATT_QB:(blk + 1) * ATT_QB]

        @pl.when(g == n_g - 1)
        def _():
            sk_ref[...] = jnp.sum(dk_ref[...], axis=0, keepdims=True)
            sv_ref[...] = jnp.sum(dv_ref[...], axis=0, keepdims=True)

    n_hp, n_g = ATT_H // 2, S // ATT_QB
    return _call_hosting(
        body, comm, first=lambda: (pl.program_id(0) == 0) & (pl.program_id(1) == 0),
        last=lambda: (pl.program_id(0) == n_hp - 1) & (pl.program_id(1) == n_g - 1),
        name=name, grid=(n_hp, n_g),
        in_specs=[q_spec] + kv_specs + [tiles_spec, clip_spec, q_spec],
        out_specs=[q_spec, col_spec, col_spec, tiles_spec, clip_spec] + [sum_spec] * 3,
        out_shape=[jax.ShapeDtypeStruct((S, D), BF16)] + [jax.ShapeDtypeStruct((S, D), F32)] * 2
        + [jax.ShapeDtypeStruct((ATT_H, len(REL_TILES), REL_TILE, REL_TILE), F32),
           jax.ShapeDtypeStruct((ATT_H, 1, 128), F32)] + [jax.ShapeDtypeStruct((1, D), F32)] * 3,
        scratch_shapes=[pltpu.VMEM((2 * ATT_QB, ATT_KB), F32)], compiler_params=_params(2),
        args=(*([qkv] * 7), tiles, clip, do))


def mods_partial(c_all, w_ada, *, name):
    n_l, _, n_c = w_ada.shape

    def body(c_ref, w_ref, o_ref):
        o_ref[...] = _dot(_bf(_silu(c_ref[...])), _bf(w_ref[...]))

    return pl.pallas_call(
        body, name=name, grid=(n_l,),
        in_specs=[pl.BlockSpec((N_DEV, D), lambda l: (0, 0)), pl.BlockSpec((None, D, n_c), lambda l: (l, 0, 0))],
        out_specs=pl.BlockSpec((None, N_DEV, n_c), lambda l: (l, 0, 0)),
        out_shape=jax.ShapeDtypeStruct((n_l, N_DEV, n_c), F32), compiler_params=_params(1),
    )(c_all, w_ada)


def w_ada_grad(c_all, dm, *, name):
    n_l, _, n_c = dm.shape

    def body(c_ref, d_ref, o_ref):
        o_ref[...] = lax.dot_general(_silu(c_ref[...]), d_ref[...], (((0,), (0,)), ((), ())),
                                     precision=HIGHEST, preferred_element_type=F32)

    return pl.pallas_call(
        body, name=name, grid=(n_l,),
        in_specs=[pl.BlockSpec((N_DEV, D), lambda l: (0, 0)), pl.BlockSpec((None, N_DEV, n_c), lambda l: (l, 0, 0))],
        out_specs=pl.BlockSpec((None, D, n_c), lambda l: (l, 0, 0)),
        out_shape=jax.ShapeDtypeStruct((n_l, D, n_c), F32), compiler_params=_params(1),
    )(c_all, dm)


def adamw(w, m, v, gparts, *, block_rows, name):
    R, C = w.shape
    n = gparts.shape[0]

    def body(w_ref, m_ref, v_ref, g_ref, go_ref, d_ref, mo_ref, vo_ref):
        g = g_ref[0].astype(F32)
        for k in range(1, n):
            g = g + g_ref[k].astype(F32)
        m_new = ADAM_B1 * m_ref[...] + (1.0 - ADAM_B1) * g
        v_new = ADAM_B2 * v_ref[...] + (1.0 - ADAM_B2) * (g * g)
        m_hat = m_new / (1.0 - ADAM_B1 ** ADAM_STEP)
        v_hat = v_new / (1.0 - ADAM_B2 ** ADAM_STEP)
        go_ref[...] = g
        d_ref[...] = -ADAM_LR * (m_hat / (jnp.sqrt(v_hat) + ADAM_EPS) + ADAM_WD * w_ref[...])
        mo_ref[...] = m_new
        vo_ref[...] = v_new

    blk = pl.BlockSpec((block_rows, C), lambda i: (i, 0))
    return pl.pallas_call(
        body, name=name, grid=(R // block_rows,),
        in_specs=[blk, blk, blk, pl.BlockSpec((n, block_rows, C), lambda i: (0, i, 0))],
        out_specs=[blk] * 4, out_shape=[jax.ShapeDtypeStruct((R, C), F32)] * 4, compiler_params=_params(1),
    )(w, m, v, gparts)


def adamw_nd(w, m, v, gparts, *, name):
    shape = w.shape
    two = (int(np.prod(shape[:-1])), shape[-1])
    rows = two[0]
    block_rows = rows
    for cand in (512, 256):
        if rows > cand and rows % cand == 0:
            block_rows = cand
            break
    outs = adamw(w.reshape(two), m.reshape(two), v.reshape(two), gparts.reshape((gparts.shape[0],) + two),
                 block_rows=block_rows, name=name)
    return [o.reshape(shape) for o in outs]


def sum_parts(parts, *, name):
    n, R, C = parts.shape

    def body(p_ref, o_ref):
        acc = p_ref[0]
        for k in range(1, n):
            acc = acc + p_ref[k]
        o_ref[...] = acc

    return pl.pallas_call(
        body, name=name, in_specs=[pl.BlockSpec((n, R, C), lambda: (0, 0, 0))],
        out_specs=pl.BlockSpec((R, C), lambda: (0, 0)), out_shape=jax.ShapeDtypeStruct((R, C), F32),
        compiler_params=pltpu.CompilerParams(vmem_limit_bytes=VMEM_LIMIT),
    )(parts)


def _my_place():
    return lax.axis_index("x"), lax.axis_index("y"), lax.axis_index("c")


def _full_shape(kind, shard):
    n_l, rows, cols = shard
    return {"col": (n_l, rows, N_DEV * cols), "row": (n_l, N_DEV * rows, cols), "stk": (N_DEV, n_l, rows, cols)}[kind]


def _slab(ref, kind, dev, shard):
    _, rows, cols = shard
    if kind == "col":
        return ref.at[:, :, pl.ds(pl.multiple_of(dev * cols, 128), cols)]
    if kind == "row":
        return ref.at[:, pl.ds(pl.multiple_of(dev * rows, 8), rows), :]
    return ref.at[dev]


def all_gather(x_shard, *, name):
    m_per, n = x_shard.shape

    def body(x_ref, out_ref, send_sems, recv_sems, local_sem):
        x, y, c = _my_place()
        me, sibling = (x, y, c), (x, y, 1 - c)
        chips = [(1 - x, y), (x, 1 - y), (1 - x, 1 - y)]

        def rows(px, py, pc):
            return out_ref.at[pl.ds((4 * px + 2 * py + pc) * m_per, m_per), :]

        def copy(k, block, to, src=None):
            return pltpu.make_async_remote_copy(
                src_ref=rows(*block) if src is None else src, dst_ref=rows(*block),
                send_sem=send_sems.at[k], recv_sem=recv_sems.at[k], device_id=to, device_id_type=MESH)

        mine = pltpu.make_async_copy(x_ref, rows(*me), local_sem)
        mine.start()
        first = [copy(0, me, sibling, src=x_ref)]
        first += [copy(1 + j, me, (*chip, c), src=x_ref) for j, chip in enumerate(chips)]
        for cp in first:
            cp.start()
        passed = [copy(4 + j, (*chip, c), sibling) for j, chip in enumerate(chips)]
        for j, chip in enumerate(chips):
            copy(1 + j, (*chip, c), me).wait_recv()
            passed[j].start()
        copy(0, sibling, me).wait_recv()
        for j, chip in enumerate(chips):
            copy(4 + j, (*chip, 1 - c), me).wait_recv()
        for cp in first + passed:
            cp.wait_send()
        mine.wait()

    return pl.pallas_call(
        body, name=name, out_shape=jax.ShapeDtypeStruct((N_DEV * m_per, n), x_shard.dtype),
        in_specs=[pl.BlockSpec(memory_space=pltpu.VMEM)], out_specs=pl.BlockSpec(memory_space=pltpu.VMEM),
        scratch_shapes=[pltpu.SemaphoreType.DMA((7,)), pltpu.SemaphoreType.DMA((7,)), pltpu.SemaphoreType.DMA],
        compiler_params=pltpu.CompilerParams(vmem_limit_bytes=VMEM_LIMIT),
    )(x_shard)


def gather_plan(shards, kinds, layers):
    n_t = len(shards)
    shapes = [(1,) + tuple(s.shape[1:]) for s in shards]

    def copies(x_refs, out_refs, sems):
        send_sems, recv_sems, local_sems = sems
        x, y, c = _my_place()
        me, sibling = (x, y, c), (x, y, 1 - c)
        chips = [(1 - x, y), (x, 1 - y), (1 - x, 1 - y)]
        own = [x_refs[t].at[pl.ds(layers[t], 1)] for t in range(n_t)]

        def slab(t, px, py, pc):
            return _slab(out_refs[t], kinds[t], 4 * px + 2 * py + pc, shapes[t])

        def copy(t, k, block, to, src=None):
            return pltpu.make_async_remote_copy(
                src_ref=slab(t, *block) if src is None else src, dst_ref=slab(t, *block),
                send_sem=send_sems.at[7 * t + k], recv_sem=recv_sems.at[7 * t + k], device_id=to,
                device_id_type=MESH)

        mine = [pltpu.make_async_copy(own[t], slab(t, *me), local_sems.at[t]) for t in range(n_t)]
        sends = []
        for t in range(n_t):
            sends.append(copy(t, 0, me, sibling, src=own[t]))
            sends += [copy(t, 1 + j, me, (*chip, c), src=own[t]) for j, chip in enumerate(chips)]
        return mine, sends, copy, me, sibling, chips, c

    def first(x_refs, out_refs, sems):
        mine, sends = copies(x_refs, out_refs, sems)[:2]
        for cp in mine + sends:
            cp.start()

    def forward(x_refs, out_refs, sems):
        _, _, copy, me, sibling, chips, c = copies(x_refs, out_refs, sems)
        for j, chip in enumerate(chips):
            for t in range(n_t):
                copy(t, 1 + j, (*chip, c), me).wait_recv()
                copy(t, 4 + j, (*chip, c), sibling).start()

    def finish(x_refs, out_refs, sems):
        mine, sends, copy, me, sibling, chips, c = copies(x_refs, out_refs, sems)
        passed = [copy(t, 4 + j, (*chip, c), sibling) for j, chip in enumerate(chips) for t in range(n_t)]
        for t in range(n_t):
            copy(t, 0, sibling, me).wait_recv()
        for j, chip in enumerate(chips):
            for t in range(n_t):
                copy(t, 4 + j, (*chip, 1 - c), me).wait_recv()
        for cp in sends + passed:
            cp.wait_send()
        for cp in mine:
            cp.wait()

    def last(x_refs, out_refs, sems):
        forward(x_refs, out_refs, sems)
        finish(x_refs, out_refs, sems)

    return Hosted(
        list(shards), [jax.ShapeDtypeStruct(_full_shape(k, shp), s.dtype) for k, shp, s in zip(kinds, shapes, shards)],
        [pltpu.SemaphoreType.DMA((7 * n_t,)), pltpu.SemaphoreType.DMA((7 * n_t,)), pltpu.SemaphoreType.DMA((n_t,))],
        first, last, mid=forward, last_after_mid=finish)


def scatter_plan(grads, kinds, shapes):
    n_t = len(grads)

    def copies(g_refs, out_refs, sems):
        send_sems, recv_sems, local_sems = sems
        x, y, c = _my_place()
        me = 4 * x + 2 * y + c
        local = [pltpu.make_async_copy(_slab(g_refs[t], kinds[t], me, shapes[t]), out_refs[t].at[me],
                                       local_sems.at[t]) for t in range(n_t)]
        remote = []
        for t in range(n_t):
            for r in range(1, N_DEV):
                px = 1 - x if r & 4 else x
                py = 1 - y if r & 2 else y
                pc = 1 - c if r & 1 else c
                remote.append(pltpu.make_async_remote_copy(
                    src_ref=_slab(g_refs[t], kinds[t], 4 * px + 2 * py + pc, shapes[t]), dst_ref=out_refs[t].at[me],
                    send_sem=send_sems.at[7 * t + r - 1], recv_sem=recv_sems.at[7 * t + r - 1],
                    device_id=(px, py, pc), device_id_type=MESH))
        return local, remote

    def first(g_refs, out_refs, sems):
        local, remote = copies(g_refs, out_refs, sems)
        for cp in local + remote:
            cp.start()

    def last(g_refs, out_refs, sems):
        local, remote = copies(g_refs, out_refs, sems)
        for cp in remote + local:
            cp.wait()

    return Hosted(
        list(grads), [jax.ShapeDtypeStruct((N_DEV,) + tuple(s), BF16) for s in shapes],
        [pltpu.SemaphoreType.DMA((7 * n_t,)), pltpu.SemaphoreType.DMA((7 * n_t,)), pltpu.SemaphoreType.DMA((n_t,))],
        first, last)


def adamw_layer(w, m, v, parts, layer, bufs, *, name):
    n_l, rows, cols = w.shape
    n = parts.shape[0]
    tr = min(rows, 256)

    def body(w_ref, m_ref, v_ref, g_ref, *rest):
        go_ref, d_ref, mo_ref, vo_ref = rest[-4:]
        g = g_ref[0].astype(F32)
        for k in range(1, n):
            g = g + g_ref[k].astype(F32)
        m_new = ADAM_B1 * m_ref[...] + (1.0 - ADAM_B1) * g
        v_new = ADAM_B2 * v_ref[...] + (1.0 - ADAM_B2) * (g * g)
        m_hat = m_new / (1.0 - ADAM_B1 ** ADAM_STEP)
        v_hat = v_new / (1.0 - ADAM_B2 ** ADAM_STEP)
        go_ref[...] = g
        d_ref[...] = -ADAM_LR * (m_hat / (jnp.sqrt(v_hat) + ADAM_EPS) + ADAM_WD * w_ref[...])
        mo_ref[...] = m_new
        vo_ref[...] = v_new

    blk = pl.BlockSpec((None, tr, cols), lambda i: (layer, i, 0))
    in_specs = [blk, blk, blk, pl.BlockSpec((n, None, tr, cols), lambda i: (0, 0, i, 0))]
    args = [w, m, v, parts]
    aliases = {}
    if bufs is not None:
        in_specs += [pl.BlockSpec(memory_space=pl.ANY)] * 4
        args += list(bufs)
        aliases = {4 + k: k for k in range(4)}
    return pl.pallas_call(
        body, name=name, grid=(rows // tr,), in_specs=in_specs, out_specs=[blk] * 4,
        out_shape=[jax.ShapeDtypeStruct((n_l, rows, cols), F32)] * 4, input_output_aliases=aliases,
        compiler_params=_params(1),
    )(*args)


BIG =("gla_w_in", "gla_w_out", "att_w_in", "att_w_out", "ff_w1", "ff_w2")
KIND = {"gla_w_in": "stk", "gla_w_out": "row", "att_w_in": "col", "att_w_out": "row", "ff_w1": "col", "ff_w2": "row"}


def _pack_small(arrs):
    parts = []
    for a in arrs:
        f = a.reshape(-1)
        parts.append(jnp.pad(f, (0, -f.shape[0] % 128)))
    flat = jnp.concatenate(parts)
    flat = jnp.pad(flat, (0, -flat.shape[0] % 1024))
    return flat.reshape(-1, 128)


def _unpack_small(packed, shapes):
    flat = packed.reshape(packed.shape[:-2] + (-1,))
    out, off = [], 0
    for shp in shapes:
        n = int(np.prod(shp))
        out.append(flat[..., off:off + n].reshape(packed.shape[:-2] + tuple(shp)))
        off += n + (-n % 128)
    return out


def _vec(a):
    return a.reshape(1, -1)


def _layer_weights(i):
    mixer = "gla" if i % 2 == 0 else "att"
    return [(f"{mixer}_w_in", i // 2), (f"{mixer}_w_out", i // 2), ("ff_w1", i), ("ff_w2", i)]


def _trunk(x, target, mods, sm, w, m, v):
    shard_bf = {n: w[n].astype(BF16) for n in BIG}

    def gather_of(names):
        return gather_plan([shard_bf[n] for n, _ in names], [KIND[n] for n, _ in names], [l for _, l in names])

    def gather_under_core(i):
        names = _layer_weights(i)[2:] + (_layer_weights(i + 1)[:2] if i + 1 < DEPTH else [])
        return names, gather_of(names)

    wts = {}

    def keep_gathered(names, arrays):
        for (n, l), a in zip(names, arrays):
            if KIND[n] == "stk":
                a = a.transpose(1, 2, 0, 3).reshape(1, D, GLA_IN)
                a = jnp.pad(a, ((0, 0), (0, 0), (0, GLA_INP - GLA_IN)))
            wts[n, l] = a

    first_names, rest_names = _layer_weights(0)[:1], _layer_weights(0)[1:2]
    keep_gathered(first_names, run_hosted(gather_of(first_names), name="gather_first"))

    rel_idx = _rel_index()
    saved = []
    for i in range(DEPTH):
        sh1, sc1, g1, sh2, sc2, g2 = [mods[i, k:k + 1] for k in range(6)]
        rec = {"x0": x}
        j = i // 2
        nxt_names, nxt_plan = gather_under_core(i)
        if i % 2 == 0:
            w2p = jnp.pad(sm["gla_w_gk2"][j], ((0, 128 - GLA_RANK), (0, 0)))
            bgk, gn = _vec(sm["gla_b_gk"][j]), _vec(sm["gla_g_norm"][j])
            if i == 0:
                proj, got0 = mm_nn(x, wts["gla_w_in", j], 0, pro="mod", p1=sc1, p2=sh1, tm=512, tn=GLA_INP,
                                   comm=gather_of(rest_names), name=f"gla_proj_{i}")
                keep_gathered(rest_names, got0)
            else:
                proj = mm_nn(x, wts["gla_w_in", j], 0, pro="mod", p1=sc1, p2=sh1, tm=512, tn=GLA_INP,
                             name=f"gla_proj_{i}")
            (og, states), got = gla_fwd(proj, w2p, bgk, gn, comm=nxt_plan, name=f"gla_core_{i}")
            y = mm_nn(og, wts["gla_w_out", j], 0, tm=1024, tn=1024, name=f"gla_out_{i}")
            rec.update(kind="gla", j=j, w2p=w2p, bgk=bgk, gn=gn, proj=proj, og=og, states=states)
        else:
            rel = sm["att_rel_bias"][j]
            rel_pad = jnp.pad(rel, ((0, 0), (0, REL_PAD - N_REL)), constant_values=NEG_INF)
            tiles = rel_bias_tiles(rel_pad, rel_idx, name=f"att_bias_{i}")
            tiles = tiles.reshape(ATT_H, len(REL_TILES), REL_TILE, REL_TILE)
            clip = jnp.broadcast_to(rel[:, 2 * MAX_REL][:, None, None], (ATT_H, 1, 128))
            qkv = mm_nn(x, wts["att_w_in", j], 0, pro="mod", p1=sc1, p2=sh1, bias=_vec(sm["att_b_in"][j]),
                        out_dtype=BF16, tm=512, tn=3 * D, name=f"att_proj_{i}")
            (o,), got = attn_fwd(qkv, tiles, clip, comm=nxt_plan, name=f"att_core_{i}")
            y = mm_nn(o, wts["att_w_out", j], 0, tm=1024, tn=1024, name=f"att_out_{i}")
            rec.update(kind="att", j=j, tiles=tiles, clip=clip, qkv=qkv, o=o)
        keep_gathered(nxt_names, got)
        x1 = ln_fwd(x, y, g1, _vec(sm["ln_g"][i, 0]), _vec(sm["ln_b"][i, 0]), name=f"ln_mix_{i}")
        h = mm_nn(x1, wts["ff_w1", i], 0, pro="mod", p1=sc2, p2=sh2, out_dtype=BF16, tm=512, tn=D_FF,
                  name=f"ff_up_{i}")
        y2 = mm_nn(h, wts["ff_w2", i], 0, pro="relu2", tm=512, tn=D, name=f"ff_down_{i}")
        rec.update(y=y, x1=x1, h=h, y2=y2)
        saved.append(rec)
        if i + 1 < DEPTH:
            x = ln_fwd(x1, y2, g2, _vec(sm["ln_g"][i, 1]), _vec(sm["ln_b"][i, 1]), name=f"ln_ff_{i}")

    gw = {}

    def wgrad(weight, layer, a, d, *, tn, tk=1024, tm=512, col_block0=0, **kw):
        gw[weight, layer] = mm_tn(a, d, tk=tk, tn=tn, tm=tm, out_buf=gw.get((weight, layer)),
                                  out_shape=wts[weight, layer].shape, col_block0=col_block0, **kw)

    def scatter_layer(units):
        grads = []
        for n, l in units:
            g = gw[n, l]
            if KIND[n] == "stk":
                g = g[:, :, :GLA_IN].reshape(1, D, N_DEV, GLA_IN // N_DEV).transpose(2, 0, 1, 3)
            grads.append(g)
        return scatter_plan(grads, [KIND[n] for n, _ in units], [(1,) + tuple(w[n].shape[1:]) for n, _ in units])

    results = {n: None for n in BIG}

    def update(units, parts):
        for (n, l), p in zip(units, parts):
            results[n] = adamw_layer(w[n], m[n], v[n], p, l, results[n], name=f"adamw_{n}_{l}")

    gs = {"ln_g": [[None, None] for _ in range(DEPTH)], "ln_b": [[None, None] for _ in range(DEPTH)],
          "gla_w_gk2": [None] * 2, "gla_b_gk": [None] * 2, "gla_g_norm": [None] * 2, "att_b_in": [None] * 2,
          "att_rel_bias": [None] * 2}
    dmods = [[None] * 6 for _ in range(DEPTH)]
    nxt = None
    nxt_slot = None
    for i in reversed(range(DEPTH)):
        rec = saved[i]
        sh1, sc1, g1, sh2, sc2, g2 = [mods[i, k:k + 1] for k in range(6)]
        x0, x1 = rec["x0"], rec["x1"]
        if nxt is None:
            dz2, acc, loss = ln_bwd(x1, rec["y2"], g2, _vec(sm["ln_g"][i, 1]),
                                    loss=(target, _vec(sm["ln_b"][i, 1])), name=f"ln_ff_bwd_{i}")
        else:
            dz2, acc = ln_bwd(x1, rec["y2"], g2, _vec(sm["ln_g"][i, 1]), nxt=nxt[:3] + (_vec(sm["ln_b"][i, 1]),),
                              name=f"ln_ff_bwd_{i}")
            dmods[nxt_slot[0]][nxt_slot[1]] = acc[3]
            dmods[nxt_slot[0]][nxt_slot[2]] = acc[4]
        gs["ln_g"][i][1], gs["ln_b"][i][1], dmods[i][5] = acc[0], acc[1], acc[2]
        dh = mm_nt([dz2], wts["ff_w2", i], 0, pro="scale", p1=g2, epi_h=rec["h"], out_dtype=BF16, tm=2048, tn=512,
                   name=f"ff_down_bwd_{i}")
        wgrad("ff_w2", i, rec["h"], dz2, pro="relu2", dscale=g2, tk=2048, tn=1024, name=f"ff_w2_grad_{i}")
        du2 = mm_nt([dh], wts["ff_w1", i], 0, tm=1024, tn=1024, name=f"ff_up_bwd_{i}")
        wgrad("ff_w1", i, x1, dh, pro="mod", p1=sc2, p2=sh2, tn=2048, name=f"ff_w1_grad_{i}")
        dz1, acc = ln_bwd(x0, rec["y"], g1, _vec(sm["ln_g"][i, 0]), nxt=(dz2, du2, sc2, _vec(sm["ln_b"][i, 0])),
                          name=f"ln_mix_bwd_{i}")
        dmods[i][4], dmods[i][3] = acc[3], acc[4]
        gs["ln_g"][i][0], gs["ln_b"][i][0], dmods[i][2] = acc[0], acc[1], acc[2]
        j = rec["j"]
        w_in, w_out = _layer_weights(i)[:2]
        if rec["kind"] == "gla":
            dog = mm_nt([dz1], wts[w_out], 0, pro="scale", p1=g1, tm=1024, tn=1024, name=f"gla_out_bwd_{i}")
            wgrad(*w_out, rec["og"], dz1, dscale=g1, tn=1024, name=f"gla_wout_grad_{i}")
        else:
            do = mm_nt([dz1], wts[w_out], 0, pro="scale", p1=g1, out_dtype=BF16, tm=1024, tn=1024,
                       name=f"att_out_bwd_{i}")
            wgrad(*w_out, rec["o"], dz1, dscale=g1, tn=1024, name=f"att_wout_grad_{i}")
        units = [("ff_w1", i), ("ff_w2", i), w_out] + ([_layer_weights(i + 1)[0]] if i + 1 < DEPTH else [])
        plan = scatter_layer(units)
        if rec["kind"] == "gla":
            (dproj, dw2p, dbgk, dgn), parts = gla_bwd(rec["proj"], dog, rec["states"], rec["w2p"], rec["bgk"],
                                                      rec["gn"], comm=plan, name=f"gla_core_bwd_{i}")
            gs["gla_w_gk2"][j], gs["gla_b_gk"][j], gs["gla_g_norm"][j] = dw2p[:GLA_RANK], dbgk[0], dgn[0]
            wgrad(*w_in, x0, dproj, pro="mod", p1=sc1, p2=sh1, tk=512, tn=GLA_INP, name=f"gla_win_grad_{i}")
            if i == 0:
                du1, last_parts = mm_nt([dproj], wts[w_in], 0, tm=1024, tn=1024, comm=scatter_layer([w_in]),
                                        name=f"gla_proj_bwd_{i}")
                update([w_in], last_parts)
            else:
                du1 = mm_nt([dproj], wts[w_in], 0, tm=1024, tn=1024, name=f"gla_proj_bwd_{i}")
        else:
            (dq, dk, dv, dtiles, dclip, sq, sk, sv), parts = attn_bwd(rec["qkv"], do, rec["tiles"], rec["clip"],
                                                                      comm=plan, name=f"att_core_bwd_{i}")
            drel = rel_bias_grad(dtiles.reshape(ATT_H, -1), dclip.reshape(ATT_H, 128), rel_idx,
                                 name=f"att_bias_grad_{i}")
            gs["att_rel_bias"][j] = drel[:, :N_REL]
            gs["att_b_in"][j] = jnp.concatenate([sq[0], sk[0], sv[0]])
            du1 = mm_nt([dq, dk, dv], wts[w_in], 0, tm=512, tn=1024, name=f"att_proj_bwd_{i}")
            for n, t in enumerate((dq, dk, dv)):
                wgrad(*w_in, x0, t, pro="mod", p1=sc1, p2=sh1, tn=1024, col_block0=n, name=f"att_win_grad_{i}_{n}")
        update(units, parts)
        nxt = (dz1, du1, sc1, x0)
        nxt_slot = (i, 1, 0)
    dx, acc = combine_final(nxt[0], nxt[1], nxt[2], nxt[3], name="grad_x")
    dmods[0][1], dmods[0][0] = acc[3], acc[4]
    dmods = jnp.stack([jnp.stack(r) for r in dmods])
    gs = {k: jnp.stack([jnp.stack(r) if isinstance(r, list) else r for r in v]) for k, v in gs.items()}
    return loss, dx, dmods, gs, results


WEIGHTS = ("w_ada", "b_ada", "ln_g", "ln_b", "gla_w_in", "gla_w_gk2", "gla_b_gk", "gla_g_norm", "gla_w_out",
           "att_w_in", "att_b_in", "att_rel_bias", "att_w_out", "ff_w1", "ff_w2")
SMALL_SHARDED = {"ln_g": 2, "ln_b": 2, "gla_w_gk2": 2, "gla_g_norm": 2, "att_b_in": 1}
SMALL_GRADS = ("ln_g", "ln_b", "gla_w_gk2", "gla_b_gk", "gla_g_norm", "att_b_in", "att_rel_bias")


def kernel(x, c, w_ada, b_ada, ln_g, ln_b, gla_w_in, gla_w_gk2, gla_b_gk, gla_g_norm, gla_w_out, att_w_in, att_b_in, att_rel_bias, att_w_out, ff_w1, ff_w2, loss_target, m_w_ada, m_b_ada, m_ln_g, m_ln_b, m_gla_w_in, m_gla_w_gk2, m_gla_b_gk, m_gla_g_norm, m_gla_w_out, m_att_w_in, m_att_b_in, m_att_rel_bias, m_att_w_out, m_ff_w1, m_ff_w2, v_w_ada, v_b_ada, v_ln_g, v_ln_b, v_gla_w_in, v_gla_w_gk2, v_gla_b_gk, v_gla_g_norm, v_gla_w_out, v_att_w_in, v_att_b_in, v_att_rel_bias, v_att_w_out, v_ff_w1, v_ff_w2):
    w = dict(w_ada=w_ada, b_ada=b_ada, ln_g=ln_g, ln_b=ln_b, gla_w_in=gla_w_in, gla_w_gk2=gla_w_gk2,
             gla_b_gk=gla_b_gk, gla_g_norm=gla_g_norm, gla_w_out=gla_w_out, att_w_in=att_w_in, att_b_in=att_b_in,
             att_rel_bias=att_rel_bias, att_w_out=att_w_out, ff_w1=ff_w1, ff_w2=ff_w2)
    m = dict(w_ada=m_w_ada, b_ada=m_b_ada, ln_g=m_ln_g, ln_b=m_ln_b, gla_w_in=m_gla_w_in, gla_w_gk2=m_gla_w_gk2,
             gla_b_gk=m_gla_b_gk, gla_g_norm=m_gla_g_norm, gla_w_out=m_gla_w_out, att_w_in=m_att_w_in,
             att_b_in=m_att_b_in, att_rel_bias=m_att_rel_bias, att_w_out=m_att_w_out, ff_w1=m_ff_w1, ff_w2=m_ff_w2)
    v = dict(w_ada=v_w_ada, b_ada=v_b_ada, ln_g=v_ln_g, ln_b=v_ln_b, gla_w_in=v_gla_w_in, gla_w_gk2=v_gla_w_gk2,
             gla_b_gk=v_gla_b_gk, gla_g_norm=v_gla_g_norm, gla_w_out=v_gla_w_out, att_w_in=v_att_w_in,
             att_b_in=v_att_b_in, att_rel_bias=v_att_rel_bias, att_w_out=v_att_w_out, ff_w1=v_ff_w1, ff_w2=v_ff_w2)
    xi, yi, ci = _my_place()
    me = 4 * xi + 2 * yi + ci

    small_names = tuple(SMALL_SHARDED)
    small_in = _pack_small([c] + [w[n] for n in small_names])
    small_all = all_gather(small_in, name="gather_small").reshape(N_DEV, -1, 128)
    parts = _unpack_small(small_all, [c.shape] + [w[n].shape for n in small_names])
    c_all = parts[0].reshape(N_DEV, D)
    sm = {"gla_b_gk": gla_b_gk, "att_rel_bias": att_rel_bias}
    for n, p in zip(small_names, parts[1:]):
        ax = SMALL_SHARDED[n]
        sm[n] = jnp.moveaxis(p, 0, ax).reshape(p.shape[1:ax + 1] + (N_DEV * p.shape[ax + 1],) + p.shape[ax + 2:])

    n_ada = w_ada.shape[2]
    mp = mods_partial(c_all, w_ada, name="mods_partial")
    mp_all = all_gather(mp.reshape(DEPTH * N_DEV, n_ada), name="gather_mods")
    mp_all = mp_all.reshape(N_DEV, DEPTH, N_DEV, n_ada)
    mods = lax.dynamic_index_in_dim(mp_all, me, axis=2, keepdims=False)
    mods = mods.transpose(1, 0, 2).reshape(DEPTH, 6 * D) + b_ada
    mods = mods.reshape(DEPTH, 6, D)

    loss, dx, dmods, gs, results = _trunk(x.reshape(x.shape[1:]), loss_target.reshape(x.shape[1:]), mods, sm, w, m, v)
    loss = lax.psum(loss[0, 0], ("x", "y", "c"))

    dm_flat = dmods.reshape(DEPTH, 6 * D)
    small_g = [dm_flat] + [gs[n].reshape(sm[n].shape) for n in SMALL_GRADS]
    small_shapes = [a.shape for a in small_g]
    sg_all = all_gather(_pack_small(small_g), name="gather_small_grads").reshape(N_DEV, -1, 128)
    summed = _unpack_small(sum_parts(sg_all, name="sum_small_grads"), small_shapes)
    g_full = dict(zip(("b_ada",) + SMALL_GRADS, summed))
    dm_all = _unpack_small(sg_all, small_shapes)[0]
    dm_mine = lax.dynamic_slice_in_dim(dm_all, me * n_ada, n_ada, axis=2).transpose(1, 0, 2)
    g_w_ada = w_ada_grad(c_all, dm_mine, name="w_ada_grad")

    results["w_ada"] = adamw_nd(w_ada, m_w_ada, v_w_ada, g_w_ada[None], name="adamw_w_ada")
    for n in ("b_ada",) + SMALL_GRADS:
        g = g_full[n]
        if n in SMALL_SHARDED:
            ax = SMALL_SHARDED[n]
            width = w[n].shape[ax]
            g = lax.dynamic_slice_in_dim(g, me * width, width, axis=ax)
        results[n] = adamw_nd(w[n], m[n], v[n], g[None], name=f"adamw_{n}")

    out = [loss, dx[None]]
    for k in range(4):
        out += [results[n][k] for n in WEIGHTS]
    return tuple(out)
```

```python
import numpy as np
import jax
import jax.numpy as jnp
from jax import lax
from jax.experimental import pallas as pl
from jax.experimental.pallas import tpu as pltpu

F32 = jnp.float32
BF16 = jnp.bfloat16
HIGHEST = lax.Precision.HIGHEST
MESH = pl.DeviceIdType.MESH

N_DEV = 8
D = 1024
DEPTH = 4
CHUNK = 64
ALPHA = (2.0 * DEPTH) ** 0.25
LN_EPS = 1e-5
RMS_EPS = 1e-6
NEG_INF = -1e30

GLA_H = 4
GLA_DKH = 128
GLA_DVH = 256
GLA_DK = GLA_H * GLA_DKH
GLA_DV = GLA_H * GLA_DVH
GLA_RANK = 16
GLA_IN = 2 * GLA_DK + 2 * GLA_DV + GLA_RANK
GLA_INP = 3200
GLA_TAU_INV = 1.0 / 16.0
GLA_SUB = 2

ATT_H = 16
ATT_HD = 64
ATT_QB = 256
ATT_KB = 3 * ATT_QB
LEFT = 8 * CHUNK
MAX_REL = 128
N_REL = 2 * MAX_REL + 1
REL_PAD = 384
REL_TILE = 128
REL_TILES = (3, 4)
D_FF = 4 * D

ADAM_LR = 0.001
ADAM_B1 = 0.9
ADAM_B2 = 0.999
ADAM_EPS = 1e-08
ADAM_WD = 0.01
ADAM_STEP = 10

VMEM_LIMIT = 48 * 1024 * 1024


def _params(n_axes):
    return pltpu.CompilerParams(dimension_semantics=("arbitrary",) * n_axes, vmem_limit_bytes=VMEM_LIMIT)


def _dot(a, b):
    return jnp.dot(a, b, preferred_element_type=F32)


def _dot_nt(a, b):
    return lax.dot_general(a, b, (((1,), (1,)), ((), ())), preferred_element_type=F32)


def _dot_tn(a, b):
    return lax.dot_general(a, b, (((0,), (0,)), ((), ())), preferred_element_type=F32)


def _bf(a):
    return a.astype(BF16)


def _prologue(kind, a, p1=None, p2=None):
    if kind == "mod":
        return a.astype(F32) * (1.0 + p1) + p2
    if kind == "scale":
        return a.astype(F32) * (1.0 + p1)
    if kind == "relu2":
        r = jnp.maximum(a, 0.0)
        return r * r
    return a


class Hosted:
    def __init__(self, inputs, out_shapes, sems, first, last, mid=None, last_after_mid=None):
        self.inputs, self.out_shapes, self.sems, self.first, self.last = inputs, out_shapes, sems, first, last
        self.mid, self.last_after_mid = mid, last_after_mid


def _hbm_specs(n):
    return [pl.BlockSpec(memory_space=pltpu.HBM)] * n


def _call_hosting(body, comm, *, first, last, in_specs, out_specs, out_shape, scratch_shapes, args, mid=None, **kw):
    if comm is None:
        return pl.pallas_call(body, in_specs=in_specs, out_specs=out_specs, out_shape=out_shape,
                              scratch_shapes=scratch_shapes, **kw)(*args), []
    n_in, n_out, n_scr = len(in_specs), len(out_specs), len(scratch_shapes)
    n_ci, n_co = len(comm.inputs), len(comm.out_shapes)

    def hosting(*refs):
        ins, ci = refs[:n_in], refs[n_in:n_in + n_ci]
        k = n_in + n_ci
        outs, co = refs[k:k + n_out], refs[k + n_out:k + n_out + n_co]
        k += n_out + n_co
        scr, cs = refs[k:k + n_scr], refs[k + n_scr:]

        @pl.when(first())
        def _():
            comm.first(ci, co, cs)

        body(*ins, *outs, *scr)
        split = mid is not None and comm.mid is not None
        if split:
            @pl.when(mid())
            def _():
                comm.mid(ci, co, cs)

        @pl.when(last())
        def _():
            (comm.last_after_mid if split else comm.last)(ci, co, cs)

    res = pl.pallas_call(
        hosting, in_specs=list(in_specs) + _hbm_specs(n_ci), out_specs=list(out_specs) + _hbm_specs(n_co),
        out_shape=list(out_shape) + list(comm.out_shapes), scratch_shapes=list(scratch_shapes) + list(comm.sems),
        **kw)(*args, *comm.inputs)
    return res[:n_out], res[n_out:]


def run_hosted(comm, *, name):
    n_i, n_o = len(comm.inputs), len(comm.out_shapes)

    def body(*refs):
        ins, outs, sems = refs[:n_i], refs[n_i:n_i + n_o], refs[n_i + n_o:]
        comm.first(ins, outs, sems)
        comm.last(ins, outs, sems)

    return pl.pallas_call(body, name=name, out_shape=list(comm.out_shapes), in_specs=_hbm_specs(n_i),
                          out_specs=_hbm_specs(n_o), scratch_shapes=list(comm.sems))(*comm.inputs)


def mm_nn(a, b, layer, *, pro=None, p1=None, p2=None, bias=None, out_dtype=F32, tm, tn, comm=None, name):
    M, K = a.shape
    N = b.shape[2]
    tm = min(tm, M)
    n_p = {"mod": 2, "scale": 1}.get(pro, 0)
    has_bias = bias is not None
    direct = pro is None and a.dtype == BF16

    def body(*refs):
        a_ref, b_ref = refs[0], refs[1]
        p_refs = refs[2:2 + n_p]
        bias_ref = refs[2 + n_p] if has_bias else None
        if direct:
            o_ref = refs[-1]
            lhs = a_ref[...]
        else:
            o_ref, abf = refs[-2], refs[-1]

            @pl.when(pl.program_id(1) == 0)
            def _():
                abf[...] = _bf(_prologue(pro, a_ref[...], *[r[...] for r in p_refs]))

            lhs = abf[...]
        acc = _dot(lhs, b_ref[...])
        if has_bias:
            acc = acc + bias_ref[...]
        o_ref[...] = acc.astype(out_dtype)

    in_specs = [pl.BlockSpec((tm, K), lambda i, j: (i, 0)), pl.BlockSpec((None, K, tn), lambda i, j: (layer, 0, j))]
    args = [a, b]
    for p in (p1, p2)[:n_p]:
        in_specs.append(pl.BlockSpec((1, K), lambda i, j: (0, 0)))
        args.append(p)
    if has_bias:
        in_specs.append(pl.BlockSpec((1, tn), lambda i, j: (0, j)))
        args.append(bias)
    n_i, n_j = M // tm, N // tn
    (out,), got = _call_hosting(
        body, comm, first=lambda: (pl.program_id(0) == 0) & (pl.program_id(1) == 0),
        last=lambda: (pl.program_id(0) == n_i - 1) & (pl.program_id(1) == n_j - 1),
        name=name, grid=(n_i, n_j), in_specs=in_specs,
        out_specs=[pl.BlockSpec((tm, tn), lambda i, j: (i, j))],
        out_shape=[jax.ShapeDtypeStruct((M, N), out_dtype)],
        scratch_shapes=[] if direct else [pltpu.VMEM((tm, K), BF16)], compiler_params=_params(2), args=args)
    return out if comm is None else (out, got)


def mm_nt(a_parts, w, layer, *, pro=None, p1=None, epi_h=None, out_dtype=F32, tm, tn, comm=None, name):
    M = a_parts[0].shape[0]
    tm = min(tm, M)
    widths = [p.shape[1] for p in a_parts]
    Nw = sum(widths)
    Kw = w.shape[1]
    n_a = len(a_parts)
    has_p = pro == "scale"
    has_h = epi_h is not None
    direct = n_a == 1 and not has_p and a_parts[0].dtype == BF16

    def body(*refs):
        a_refs = refs[:n_a]
        w_ref = refs[n_a]
        k = n_a + 1
        p_ref = refs[k] if has_p else None
        k += int(has_p)
        h_ref = refs[k] if has_h else None
        if direct:
            o_ref = refs[-1]
            lhs = a_refs[0][...]
        else:
            o_ref, abf = refs[-2], refs[-1]

            @pl.when(pl.program_id(1) == 0)
            def _():
                off = 0
                for r, wd in zip(a_refs, widths):
                    av = r[...]
                    if has_p:
                        av = av.astype(F32) * (1.0 + p_ref[...])
                    abf[:, off:off + wd] = _bf(av)
                    off += wd

            lhs = abf[...]
        acc = _dot_nt(lhs, w_ref[...])
        if has_h:
            acc = acc * (2.0 * jnp.maximum(h_ref[...], 0.0))
        o_ref[...] = acc.astype(out_dtype)

    in_specs = [pl.BlockSpec((tm, wd), lambda i, j: (i, 0)) for wd in widths]
    in_specs.append(pl.BlockSpec((None, tn, Nw), lambda i, j: (layer, j, 0)))
    args = list(a_parts) + [w]
    if has_p:
        in_specs.append(pl.BlockSpec((1, Nw), lambda i, j: (0, 0)))
        args.append(p1)
    if has_h:
        in_specs.append(pl.BlockSpec((tm, tn), lambda i, j: (i, j)))
        args.append(epi_h)
    n_i, n_j = M // tm, Kw // tn
    (out,), got = _call_hosting(
        body, comm, first=lambda: (pl.program_id(0) == 0) & (pl.program_id(1) == 0),
        last=lambda: (pl.program_id(0) == n_i - 1) & (pl.program_id(1) == n_j - 1),
        name=name, grid=(n_i, n_j), in_specs=in_specs,
        out_specs=[pl.BlockSpec((tm, tn), lambda i, j: (i, j))],
        out_shape=[jax.ShapeDtypeStruct((M, Kw), out_dtype)],
        scratch_shapes=[] if direct else [pltpu.VMEM((tm, Nw), BF16)], compiler_params=_params(2), args=args)
    return out if comm is None else (out, got)


def mm_tn(a, d, *, pro=None, p1=None, p2=None, dscale=None, tk, tn, tm, out_buf, out_shape, col_block0=0, name):
    M, Kf = a.shape
    N = d.shape[1]
    n_p = {"mod": 2}.get(pro, 0)
    has_ds = dscale is not None
    has_buf = out_buf is not None
    n_m = M // tm

    def body(*refs):
        a_ref, d_ref = refs[0], refs[1]
        p_refs = refs[2:2 + n_p]
        ds_ref = refs[2 + n_p] if has_ds else None
        o_ref, acc = refs[-2], refs[-1]
        m = pl.program_id(2)

        @pl.when(m == 0)
        def _():
            acc[...] = jnp.zeros_like(acc)

        av = _prologue(pro, a_ref[...], *[r[...] for r in p_refs])
        dv = d_ref[...]
        if has_ds:
            dv = dv.astype(F32) * (1.0 + ds_ref[...])
        acc[...] += _dot_tn(_bf(av), _bf(dv))

        @pl.when(m == n_m - 1)
        def _():
            o_ref[...] = _bf(acc[...])

    in_specs = [pl.BlockSpec((tm, tk), lambda i, j, m: (m, i)), pl.BlockSpec((tm, tn), lambda i, j, m: (m, j))]
    args = [a, d]
    for p in (p1, p2)[:n_p]:
        in_specs.append(pl.BlockSpec((1, tk), lambda i, j, m: (0, i)))
        args.append(p)
    if has_ds:
        in_specs.append(pl.BlockSpec((1, tn), lambda i, j, m: (0, j)))
        args.append(dscale)
    aliases = {}
    if has_buf:
        in_specs.append(pl.BlockSpec(memory_space=pl.ANY))
        args.append(out_buf)
        aliases = {len(args) - 1: 0}
    return pl.pallas_call(
        body, name=name, grid=(Kf // tk, N // tn, n_m), in_specs=in_specs,
        out_specs=pl.BlockSpec((None, tk, tn), lambda i, j, m: (0, i, col_block0 + j)),
        out_shape=jax.ShapeDtypeStruct(out_shape, BF16), input_output_aliases=aliases,
        scratch_shapes=[pltpu.VMEM((tk, tn), F32)], compiler_params=_params(3),
    )(*args)


ROW_BLOCK = 512
ACC_ROWS = 8


def _ln_stats(z):
    mu = jnp.mean(z, axis=-1, keepdims=True)
    zc = z - mu
    var = jnp.mean(zc * zc, axis=-1, keepdims=True)
    return zc, lax.rsqrt(var + LN_EPS)


def ln_fwd(x, y, gate, lng, lnb, *, name):
    S = x.shape[0]

    def body(x_ref, y_ref, gt_ref, g_ref, b_ref, o_ref):
        z = ALPHA * x_ref[...] + (1.0 + gt_ref[...]) * y_ref[...]
        zc, rstd = _ln_stats(z)
        o_ref[...] = (zc * rstd) * g_ref[...] + b_ref[...]

    row = pl.BlockSpec((ROW_BLOCK, D), lambda i: (i, 0))
    vec = pl.BlockSpec((1, D), lambda i: (0, 0))
    return pl.pallas_call(
        body, name=name, grid=(S // ROW_BLOCK,), in_specs=[row, row, vec, vec, vec], out_specs=row,
        out_shape=jax.ShapeDtypeStruct((S, D), F32), compiler_params=_params(1),
    )(x, y, gate, lng, lnb)


def _add_colsum(acc_ref, r, val):
    acc_ref[r:r + 1, :] += jnp.sum(val, axis=0, keepdims=True)


def ln_bwd(x_in, y, gate, lng, *, loss=None, nxt=None, name):
    S = x_in.shape[0]
    has_next = nxt is not None

    def body(*refs):
        if has_next:
            dzn_ref, dun_ref, scn_ref, b_ref = refs[:4]
            k = 4
        else:
            t_ref, b_ref = refs[:2]
            k = 2
        x_ref, y_ref, gt_ref, g_ref = refs[k:k + 4]
        dz_ref, acc_ref, dzs_ref = refs[k + 4:k + 7]
        loss_ref = None if has_next else refs[k + 7]

        @pl.when(pl.program_id(0) == 0)
        def _():
            acc_ref[...] = jnp.zeros_like(acc_ref)
            if not has_next:
                loss_ref[...] = jnp.zeros_like(loss_ref)

        yv = y_ref[...]
        z = ALPHA * x_ref[...] + (1.0 + gt_ref[...]) * yv
        zc, rstd = _ln_stats(z)
        xhat = zc * rstd
        if has_next:
            du = dun_ref[...]
            dout_v = ALPHA * dzn_ref[...] + du * (1.0 + scn_ref[...])
            _add_colsum(acc_ref, 3, du * (xhat * g_ref[...] + b_ref[...]))
            _add_colsum(acc_ref, 4, du)
        else:
            e = (xhat * g_ref[...] + b_ref[...]) - t_ref[...]
            dout_v = e * (1.0 / D)
            per_tok = jnp.sum(e * e, axis=1, keepdims=True) * (1.0 / D)
            loss_ref[...] += 0.5 * jnp.sum(per_tok, axis=0, keepdims=True)
        _add_colsum(acc_ref, 0, dout_v * xhat)
        _add_colsum(acc_ref, 1, dout_v)
        dxh = dout_v * g_ref[...]
        m1 = jnp.mean(dxh, axis=-1, keepdims=True)
        m2 = jnp.mean(dxh * xhat, axis=-1, keepdims=True)
        dz = rstd * (dxh - m1 - xhat * m2)
        _add_colsum(acc_ref, 2, dz * yv)
        dz_ref[...] = dz
        dzs_ref[...] = _bf(dz * (1.0 + gt_ref[...]))

    row = pl.BlockSpec((ROW_BLOCK, D), lambda i: (i, 0))
    vec = pl.BlockSpec((1, D), lambda i: (0, 0))
    if has_next:
        in_specs = [row, row, vec, vec]
        args = list(nxt)
    else:
        in_specs = [row, vec]
        args = list(loss)
    in_specs += [row, row, vec, vec]
    args += [x_in, y, gate, lng]
    out_specs = [row, pl.BlockSpec((ACC_ROWS, D), lambda i: (0, 0)), row]
    out_shape = [jax.ShapeDtypeStruct((S, D), F32), jax.ShapeDtypeStruct((ACC_ROWS, D), F32),
                 jax.ShapeDtypeStruct((S, D), BF16)]
    if not has_next:
        out_specs.append(pl.BlockSpec((8, 128), lambda i: (0, 0)))
        out_shape.append(jax.ShapeDtypeStruct((8, 128), F32))
    return pl.pallas_call(
        body, name=name, grid=(S // ROW_BLOCK,), in_specs=in_specs, out_specs=out_specs, out_shape=out_shape,
        compiler_params=_params(1),
    )(*args)


def combine_final(dz, du, sc, x_in, *, name):
    S = dz.shape[0]

    def body(dz_ref, du_ref, sc_ref, x_ref, dx_ref, acc_ref):
        @pl.when(pl.program_id(0) == 0)
        def _():
            acc_ref[...] = jnp.zeros_like(acc_ref)

        du_v = du_ref[...]
        dx_ref[...] = ALPHA * dz_ref[...] + du_v * (1.0 + sc_ref[...])
        _add_colsum(acc_ref, 3, du_v * x_ref[...])
        _add_colsum(acc_ref, 4, du_v)

    row = pl.BlockSpec((ROW_BLOCK, D), lambda i: (i, 0))
    vec = pl.BlockSpec((1, D), lambda i: (0, 0))
    return pl.pallas_call(
        body, name=name, grid=(S // ROW_BLOCK,), in_specs=[row, row, vec, row],
        out_specs=[row, pl.BlockSpec((ACC_ROWS, D), lambda i: (0, 0))],
        out_shape=[jax.ShapeDtypeStruct((S, D), F32), jax.ShapeDtypeStruct((ACC_ROWS, D), F32)],
        compiler_params=_params(1),
    )(dz, du, sc, x_in)


def _log_sigmoid(x):
    return jnp.minimum(x, 0.0) - jnp.log(1.0 + jnp.exp(-jnp.abs(x)))


def _silu(x):
    return x * (1.0 / (1.0 + jnp.exp(-x)))


def _cumsum_steps(x):
    row = lax.broadcasted_iota(jnp.int32, x.shape, 0)
    step = 1
    while step < x.shape[0]:
        x = x + jnp.where(row >= step, pltpu.roll(x, step, 0), 0.0)
        step *= 2
    return x


@jax.custom_vjp
def _cumsum_rows(x):
    return _cumsum_steps(x)


def _cumsum_rows_fwd(x):
    return _cumsum_steps(x), None


def _cumsum_rows_bwd(_, g):
    return (jnp.sum(g, axis=0, keepdims=True) - _cumsum_steps(g) + g,)


_cumsum_rows.defvjp(_cumsum_rows_fwd, _cumsum_rows_bwd)


def _gla_chunk(q, k, v, g, gk, s0t, w2p, bgk, gn):
    C = q.shape[0]
    row = lax.broadcasted_iota(jnp.int32, (C, C), 0)
    col = lax.broadcasted_iota(jnp.int32, (C, C), 1)
    lower = row >= col
    la = _log_sigmoid(_dot(_bf(gk), _bf(w2p)) + bgk) * GLA_TAU_INV
    outs, states = [], []
    for h in range(GLA_H):
        ks = slice(h * GLA_DKH, (h + 1) * GLA_DKH)
        vs = slice(h * GLA_DVH, (h + 1) * GLA_DVH)
        qh = q[:, ks] * (GLA_DKH ** -0.5)
        kh, vh, gh, lah, s0 = k[:, ks], v[:, vs], g[:, vs], la[:, ks], s0t[h]
        cum = _cumsum_rows(lah)
        e_pos = jnp.exp(cum)
        e_neg = jnp.exp(-cum)
        q_f = qh * e_pos
        a_f = _dot_nt(_bf(q_f), _bf(kh * e_neg))
        a_b = _dot_nt(_bf(qh * e_neg), _bf(kh * e_pos))
        att = jnp.where(lower, a_f, a_b)
        o = _dot(_bf(att), _bf(vh)) + _dot_nt(_bf(q_f), _bf(s0))
        tot = jnp.sum(lah, axis=0, keepdims=True)
        k_end = kh * jnp.exp(tot - cum)
        states.append(s0 * jnp.exp(tot) + _dot_tn(_bf(vh), _bf(k_end)))
        on = o * lax.rsqrt(jnp.mean(o * o, axis=-1, keepdims=True) + RMS_EPS) * gn[:, vs]
        outs.append(on * _silu(gh))
    return jnp.concatenate(outs, axis=1), tuple(states)


def _gla_split(p):
    return (p[:, 0:GLA_DK], p[:, GLA_DK:2 * GLA_DK], p[:, 2 * GLA_DK:2 * GLA_DK + GLA_DV],
            p[:, 2 * GLA_DK + GLA_DV:2 * GLA_DK + 2 * GLA_DV], p[:, 2 * GLA_DK + 2 * GLA_DV:GLA_INP])


def gla_fwd(proj, w2p, bgk, gn, *, comm=None, name):
    S = proj.shape[0]
    n_c = S // CHUNK
    n_s = n_c // GLA_SUB
    rows = GLA_SUB * CHUNK

    def body(p_ref, w_ref, b_ref, gn_ref, o_ref, st_ref, st):
        @pl.when(pl.program_id(0) == 0)
        def _():
            st[...] = jnp.zeros_like(st)

        s = tuple(st[h] for h in range(GLA_H))
        for u in range(GLA_SUB):
            sub = slice(u * CHUNK, (u + 1) * CHUNK)
            for h in range(GLA_H):
                st_ref[u, h] = s[h]
            og, s = _gla_chunk(*_gla_split(p_ref[sub, :]), s, w_ref[...], b_ref[...], gn_ref[...])
            o_ref[sub, :] = _bf(og)
        for h in range(GLA_H):
            st[h] = s[h]

    full = lambda shape: pl.BlockSpec(shape, lambda i: (0,) * len(shape))
    return _call_hosting(
        body, comm, first=lambda: pl.program_id(0) == 0, last=lambda: pl.program_id(0) == n_s - 1,
        name=name, grid=(n_s,),
        in_specs=[pl.BlockSpec((rows, GLA_INP), lambda i: (i, 0)), full((128, GLA_DK)), full((1, GLA_DK)),
                  full((1, GLA_DV))],
        out_specs=[pl.BlockSpec((rows, GLA_DV), lambda i: (i, 0)),
                   pl.BlockSpec((GLA_SUB, GLA_H, GLA_DVH, GLA_DKH), lambda i: (i, 0, 0, 0))],
        out_shape=[jax.ShapeDtypeStruct((S, GLA_DV), BF16),
                   jax.ShapeDtypeStruct((n_c, GLA_H, GLA_DVH, GLA_DKH), F32)],
        scratch_shapes=[pltpu.VMEM((GLA_H, GLA_DVH, GLA_DKH), F32)], compiler_params=_params(1),
        args=(proj, w2p, bgk, gn))


def gla_bwd(proj, dog, states, w2p, bgk, gn, *, comm=None, name):
    S = proj.shape[0]
    n_c = S // CHUNK
    n_s = n_c // GLA_SUB
    rows = GLA_SUB * CHUNK

    def body(p_ref, dog_ref, st_ref, w_ref, b_ref, gn_ref, dp_ref, dw_ref, db_ref, dgn_ref, ds_ref):
        @pl.when(pl.program_id(0) == 0)
        def _():
            ds_ref[...] = jnp.zeros_like(ds_ref)
            dw_ref[...] = jnp.zeros_like(dw_ref)
            db_ref[...] = jnp.zeros_like(db_ref)
            dgn_ref[...] = jnp.zeros_like(dgn_ref)

        ds = tuple(ds_ref[h] for h in range(GLA_H))
        for u in reversed(range(GLA_SUB)):
            sub = slice(u * CHUNK, (u + 1) * CHUNK)
            q, k, v, g, gk = _gla_split(p_ref[sub, :])
            s0 = tuple(st_ref[u, h] for h in range(GLA_H))
            _, vjp = jax.vjp(_gla_chunk, q, k, v, g, gk, s0, w_ref[...], b_ref[...], gn_ref[...])
            dq, dk, dv, dg, dgk, ds, dw, db, dgn = vjp((dog_ref[sub, :], ds))
            dp_ref[sub, 0:GLA_DK] = _bf(dq)
            dp_ref[sub, GLA_DK:2 * GLA_DK] = _bf(dk)
            dp_ref[sub, 2 * GLA_DK:2 * GLA_DK + GLA_DV] = _bf(dv)
            dp_ref[sub, 2 * GLA_DK + GLA_DV:2 * GLA_DK + 2 * GLA_DV] = _bf(dg)
            dp_ref[sub, 2 * GLA_DK + 2 * GLA_DV:GLA_INP] = _bf(dgk)
            dw_ref[...] += dw
            db_ref[...] += db
            dgn_ref[...] += dgn
        for h in range(GLA_H):
            ds_ref[h] = ds[h]

    full = lambda shape: pl.BlockSpec(shape, lambda i: (0,) * len(shape))
    rev = lambda i: (n_s - 1 - i, 0)
    return _call_hosting(
        body, comm, first=lambda: pl.program_id(0) == 0, last=lambda: pl.program_id(0) == n_s - 1,
        name=name, grid=(n_s,),
        in_specs=[pl.BlockSpec((rows, GLA_INP), rev), pl.BlockSpec((rows, GLA_DV), rev),
                  pl.BlockSpec((GLA_SUB, GLA_H, GLA_DVH, GLA_DKH), lambda i: (n_s - 1 - i, 0, 0, 0)),
                  full((128, GLA_DK)), full((1, GLA_DK)), full((1, GLA_DV))],
        out_specs=[pl.BlockSpec((rows, GLA_INP), rev), full((128, GLA_DK)), full((1, GLA_DK)), full((1, GLA_DV))],
        out_shape=[jax.ShapeDtypeStruct((S, GLA_INP), BF16), jax.ShapeDtypeStruct((128, GLA_DK), F32),
                   jax.ShapeDtypeStruct((1, GLA_DK), F32), jax.ShapeDtypeStruct((1, GLA_DV), F32)],
        scratch_shapes=[pltpu.VMEM((GLA_H, GLA_DVH, GLA_DKH), F32)], compiler_params=_params(1),
        args=(proj, dog, states, w2p, bgk, gn))


def _rel_index():
    t = np.arange(REL_TILE)[:, None]
    j = np.arange(REL_TILE)[None, :]
    tiles = []
    for m in REL_TILES:
        chunks = (REL_TILE // CHUNK) * m + j // CHUNK - t // CHUNK
        band = (chunks >= 0) & (chunks <= LEFT // CHUNK)
        dist = LEFT - REL_TILE * m + t - j
        tiles.append(np.where(band, np.minimum(dist, MAX_REL) + MAX_REL, N_REL))
    return jnp.asarray(np.stack(tiles).reshape(1, -1).astype(np.int32))


REL_BLOCK = 2048


def _one_hot(idx_row):
    return (lax.broadcasted_iota(jnp.int32, (REL_PAD, idx_row.shape[1]), 0) == idx_row).astype(F32)


def rel_bias_tiles(rel_pad, idx, *, name):
    E = idx.shape[1]

    def body(r_ref, i_ref, o_ref):
        o_ref[...] = jnp.dot(r_ref[...], _one_hot(i_ref[...]), precision=HIGHEST, preferred_element_type=F32)

    return pl.pallas_call(
        body, name=name, grid=(E // REL_BLOCK,),
        in_specs=[pl.BlockSpec((ATT_H, REL_PAD), lambda i: (0, 0)), pl.BlockSpec((1, REL_BLOCK), lambda i: (0, i))],
        out_specs=pl.BlockSpec((ATT_H, REL_BLOCK), lambda i: (0, i)),
        out_shape=jax.ShapeDtypeStruct((ATT_H, E), F32), compiler_params=_params(1),
    )(rel_pad, idx)


def rel_bias_grad(dtiles_flat, dclip, idx, *, name):
    E = idx.shape[1]
    n_steps = E // REL_BLOCK

    def body(d_ref, c_ref, i_ref, o_ref):
        @pl.when(pl.program_id(0) == 0)
        def _():
            o_ref[...] = jnp.zeros_like(o_ref)

        o_ref[...] += lax.dot_general(d_ref[...], _one_hot(i_ref[...]), (((1,), (1,)), ((), ())),
                                      precision=HIGHEST, preferred_element_type=F32)

        @pl.when(pl.program_id(0) == n_steps - 1)
        def _():
            at_clip = lax.broadcasted_iota(jnp.int32, (1, REL_PAD), 1) == 2 * MAX_REL
            o_ref[...] += jnp.where(at_clip, jnp.sum(c_ref[...], axis=1, keepdims=True), 0.0)

    return pl.pallas_call(
        body, name=name, grid=(n_steps,),
        in_specs=[pl.BlockSpec((ATT_H, REL_BLOCK), lambda i: (0, i)), pl.BlockSpec((ATT_H, 128), lambda i: (0, 0)),
                  pl.BlockSpec((1, REL_BLOCK), lambda i: (0, i))],
        out_specs=pl.BlockSpec((ATT_H, REL_PAD), lambda i: (0, 0)),
        out_shape=jax.ShapeDtypeStruct((ATT_H, REL_PAD), F32), compiler_params=_params(1),
    )(dtiles_flat, dclip, idx)


def _attn_bias(tiles, clip):
    const = jnp.broadcast_to(clip, (REL_TILE, REL_TILE))
    zero = jnp.zeros((REL_TILE, REL_TILE), F32)
    rows = []
    for qt in range(ATT_QB // REL_TILE):
        blocks = []
        for kt in range(ATT_KB // REL_TILE):
            m = kt - qt
            if m in REL_TILES:
                blocks.append(tiles[REL_TILES.index(m)])
            elif 0 <= m < REL_TILES[0]:
                blocks.append(const)
            else:
                blocks.append(zero)
        rows.append(jnp.concatenate(blocks, axis=1))
    return jnp.concatenate(rows, axis=0)


def _attn_bias_grad(ds, dt_ref, dc_ref, a):
    tile = lambda qt, kt: ds[qt * REL_TILE:(qt + 1) * REL_TILE, kt * REL_TILE:(kt + 1) * REL_TILE]
    const = None
    sums = [None] * len(REL_TILES)
    for qt in range(ATT_QB // REL_TILE):
        for kt in range(ATT_KB // REL_TILE):
            m = kt - qt
            if m in REL_TILES:
                n = REL_TILES.index(m)
                sums[n] = tile(qt, kt) if sums[n] is None else sums[n] + tile(qt, kt)
            elif 0 <= m < REL_TILES[0]:
                const = tile(qt, kt) if const is None else const + tile(qt, kt)
    for n, v in enumerate(sums):
        dt_ref[a, n] += v
    dc_ref[a] += jnp.sum(const, axis=0, keepdims=True)


def _attn_head_lanes():
    lane = lax.broadcasted_iota(jnp.int32, (1, 2 * ATT_HD), 1)
    return [(lane >= a * ATT_HD) & (lane < (a + 1) * ATT_HD) for a in range(2)]


def _attn_band_bias(tiles, clip):
    j = lax.broadcasted_iota(jnp.int32, (ATT_QB, ATT_KB), 1)
    t = lax.broadcasted_iota(jnp.int32, (ATT_QB, ATT_KB), 0)
    shift = CHUNK.bit_length() - 1
    chunks = lax.shift_right_logical(j, shift) - lax.shift_right_logical(t, shift)
    band = (chunks >= 0) & (chunks <= LEFT // CHUNK)
    return jnp.where(band, _attn_bias(tiles, clip), NEG_INF)


def _attn_exp(qa, kb, bias, key_bias):
    s = _dot_nt(qa, kb) + bias + key_bias
    e = jnp.exp(s - jnp.max(s, axis=-1, keepdims=True))
    return e, jnp.sum(e, axis=-1, keepdims=True)


def _attn_specs():
    n_hp = ATT_H // 2
    q_spec = pl.BlockSpec((ATT_QB, 128), lambda hp, g: (g, hp))

    def win(col0, back):
        return pl.BlockSpec((ATT_QB, 128), lambda hp, g: (jnp.maximum(g - back, 0), col0 + hp))

    kv_specs = [win(n_hp, 2), win(n_hp, 1), win(n_hp, 0), win(2 * n_hp, 2), win(2 * n_hp, 1), win(2 * n_hp, 0)]
    tiles_spec = pl.BlockSpec((2, len(REL_TILES), REL_TILE, REL_TILE), lambda hp, g: (hp, 0, 0, 0))
    clip_spec = pl.BlockSpec((2, 1, 128), lambda hp, g: (hp, 0, 0))
    return q_spec, kv_specs, tiles_spec, clip_spec


def _attn_window(refs, g):
    kb = jnp.concatenate([_bf(r[...]) for r in refs[0:3]], axis=0)
    vb = jnp.concatenate([_bf(r[...]) for r in refs[3:6]], axis=0)
    j = lax.broadcasted_iota(jnp.int32, (1, ATT_KB), 1)
    return kb, vb, jnp.where(j + (g - 2) * ATT_QB >= 0, 0.0, NEG_INF)


def attn_fwd(qkv, tiles, clip, *, comm=None, name):
    S = qkv.shape[0]
    q_spec, kv_specs, tiles_spec, clip_spec = _attn_specs()

    def body(q_ref, *rest):
        kv_refs, t_ref, c_ref, o_ref, bias = rest[:6], rest[6], rest[7], rest[8], rest[9]
        g = pl.program_id(1)

        @pl.when(g == 0)
        def _():
            for a in range(2):
                bias[a * ATT_QB:(a + 1) * ATT_QB, :] = _attn_band_bias(t_ref[a], c_ref[a])

        kb, vb, key_bias = _attn_window(kv_refs, g)
        q = q_ref[...].astype(F32)
        out = jnp.zeros((ATT_QB, 2 * ATT_HD), F32)
        for a, lanes in enumerate(_attn_head_lanes()):
            mf = lanes.astype(F32)
            e, l = _attn_exp(_bf(q * (mf * ATT_HD ** -0.5)), kb, bias[a * ATT_QB:(a + 1) * ATT_QB, :], key_bias)
            out = out + _dot(_bf(e), vb) * (mf * (1.0 / l))
        o_ref[...] = _bf(out)

    n_hp, n_g = ATT_H // 2, S // ATT_QB
    return _call_hosting(
        body, comm, first=lambda: (pl.program_id(0) == 0) & (pl.program_id(1) == 0),
        last=lambda: (pl.program_id(0) == n_hp - 1) & (pl.program_id(1) == n_g - 1),
        mid=lambda: (pl.program_id(0) == n_hp // 2 + 1) & (pl.program_id(1) == 0),
        name=name, grid=(n_hp, n_g), in_specs=[q_spec] + kv_specs + [tiles_spec, clip_spec],
        out_specs=[q_spec], out_shape=[jax.ShapeDtypeStruct((S, D), BF16)],
        scratch_shapes=[pltpu.VMEM((2 * ATT_QB, ATT_KB), F32)], compiler_params=_params(2),
        args=(*([qkv] * 7), tiles, clip))


def attn_bwd(qkv, do, tiles, clip, *, comm=None, name):
    S = qkv.shape[0]
    q_spec, kv_specs, tiles_spec, clip_spec = _attn_specs()
    col_spec = pl.BlockSpec((S, 128), lambda hp, g: (0, hp))
    sum_spec = pl.BlockSpec((1, 128), lambda hp, g: (0, hp))
    n_g = S // ATT_QB

    def body(q_ref, *rest):
        kv_refs, t_ref, c_ref, do_ref = rest[:6], rest[6], rest[7], rest[8]
        dq_ref, dk_ref, dv_ref, dt_ref, dc_ref, sq_ref, sk_ref, sv_ref, bias = rest[9:]
        g = pl.program_id(1)

        @pl.when(g == 0)
        def _():
            for a in range(2):
                bias[a * ATT_QB:(a + 1) * ATT_QB, :] = _attn_band_bias(t_ref[a], c_ref[a])
            dk_ref[...] = jnp.zeros_like(dk_ref)
            dv_ref[...] = jnp.zeros_like(dv_ref)
            dt_ref[...] = jnp.zeros_like(dt_ref)
            dc_ref[...] = jnp.zeros_like(dc_ref)
            sq_ref[...] = jnp.zeros_like(sq_ref)

        kb, vb, key_bias = _attn_window(kv_refs, g)
        q = q_ref[...].astype(F32)
        do = do_ref[...]
        lanes = _attn_head_lanes()
        mf = [m.astype(F32) * ATT_HD ** -0.5 for m in lanes]
        qs = _bf(jnp.concatenate([q * m for m in mf], axis=0))
        dos = jnp.concatenate([jnp.where(m, do, jnp.zeros_like(do)) for m in lanes], axis=0)
        e, l = _attn_exp(qs, kb, bias[...], key_bias)
        p = e * (1.0 / l)
        dp = _dot_nt(dos, vb)
        ds = p * (dp - jnp.sum(p * dp, axis=-1, keepdims=True))
        ds_b = _bf(ds)
        dq2 = _dot(ds_b, kb)
        dq = dq2[:ATT_QB] * mf[0] + dq2[ATT_QB:] * mf[1]
        dkw = _dot_tn(ds_b, qs)
        dvw = _dot_tn(_bf(p), dos)
        for a in range(2):
            _attn_bias_grad(ds[a * ATT_QB:(a + 1) * ATT_QB], dt_ref, dc_ref, a)
        dq_ref[...] = _bf(dq)
        sq_ref[...] += jnp.sum(dq, axis=0, keepdims=True)
        for blk in range(3):
            src = g - 2 + blk

            @pl.when(src >= 0)
            def _(blk=blk, src=src):
                rows = pl.ds(pl.multiple_of(src * ATT_QB, ATT_QB), ATT_QB)
                dk_ref[rows, :] += dkw[blk * ATT_QB:(blk + 1) * ATT_QB]
                dv_ref[rows, :] += dvw[blk * ATT_QB:(blk + 1) * ATT_QB]

        @pl.when(g == n_g - 1)
        def _():
            sk_ref[...] = jnp.sum(dk_ref[...], axis=0, keepdims=True)
            sv_ref[...] = jnp.sum(dv_ref[...], axis=0, keepdims=True)

    n_hp, n_g = ATT_H // 2, S // ATT_QB
    return _call_hosting(
        body, comm, first=lambda: (pl.program_id(0) == 0) & (pl.program_id(1) == 0),
        last=lambda: (pl.program_id(0) == n_hp - 1) & (pl.program_id(1) == n_g - 1),
        name=name, grid=(n_hp, n_g),
        in_specs=[q_spec] + kv_specs + [tiles_spec, clip_spec, q_spec],
        out_specs=[q_spec, col_spec, col_spec, tiles_spec, clip_spec] + [sum_spec] * 3,
        out_shape=[jax.ShapeDtypeStruct((S, D), BF16)] + [jax.ShapeDtypeStruct((S, D), F32)] * 2
        + [jax.ShapeDtypeStruct((ATT_H, len(REL_TILES), REL_TILE, REL_TILE), F32),
           jax.ShapeDtypeStruct((ATT_H, 1, 128), F32)] + [jax.ShapeDtypeStruct((1, D), F32)] * 3,
        scratch_shapes=[pltpu.VMEM((2 * ATT_QB, ATT_KB), F32)], compiler_params=_params(2),
        args=(*([qkv] * 7), tiles, clip, do))


def mods_partial(c_all, w_ada, *, name):
    n_l, _, n_c = w_ada.shape

    def body(c_ref, w_ref, o_ref):
        o_ref[...] = _dot(_bf(_silu(c_ref[...])), _bf(w_ref[...]))

    return pl.pallas_call(
        body, name=name, grid=(n_l,),
        in_specs=[pl.BlockSpec((N_DEV, D), lambda l: (0, 0)), pl.BlockSpec((None, D, n_c), lambda l: (l, 0, 0))],
        out_specs=pl.BlockSpec((None, N_DEV, n_c), lambda l: (l, 0, 0)),
        out_shape=jax.ShapeDtypeStruct((n_l, N_DEV, n_c), F32), compiler_params=_params(1),
    )(c_all, w_ada)


def w_ada_grad(c_all, dm, *, name):
    n_l, _, n_c = dm.shape

    def body(c_ref, d_ref, o_ref):
        o_ref[...] = lax.dot_general(_silu(c_ref[...]), d_ref[...], (((0,), (0,)), ((), ())),
                                     precision=HIGHEST, preferred_element_type=F32)

    return pl.pallas_call(
        body, name=name, grid=(n_l,),
        in_specs=[pl.BlockSpec((N_DEV, D), lambda l: (0, 0)), pl.BlockSpec((None, N_DEV, n_c), lambda l: (l, 0, 0))],
        out_specs=pl.BlockSpec((None, D, n_c), lambda l: (l, 0, 0)),
        out_shape=jax.ShapeDtypeStruct((n_l, D, n_c), F32), compiler_params=_params(1),
    )(c_all, dm)


def adamw(w, m, v, gparts, *, block_rows, name):
    R, C = w.shape
    n = gparts.shape[0]

    def body(w_ref, m_ref, v_ref, g_ref, go_ref, d_ref, mo_ref, vo_ref):
        g = g_ref[0].astype(F32)
        for k in range(1, n):
            g = g + g_ref[k].astype(F32)
        m_new = ADAM_B1 * m_ref[...] + (1.0 - ADAM_B1) * g
        v_new = ADAM_B2 * v_ref[...] + (1.0 - ADAM_B2) * (g * g)
        m_hat = m_new / (1.0 - ADAM_B1 ** ADAM_STEP)
        v_hat = v_new / (1.0 - ADAM_B2 ** ADAM_STEP)
        go_ref[...] = g
        d_ref[...] = -ADAM_LR * (m_hat / (jnp.sqrt(v_hat) + ADAM_EPS) + ADAM_WD * w_ref[...])
        mo_ref[...] = m_new
        vo_ref[...] = v_new

    blk = pl.BlockSpec((block_rows, C), lambda i: (i, 0))
    return pl.pallas_call(
        body, name=name, grid=(R // block_rows,),
        in_specs=[blk, blk, blk, pl.BlockSpec((n, block_rows, C), lambda i: (0, i, 0))],
        out_specs=[blk] * 4, out_shape=[jax.ShapeDtypeStruct((R, C), F32)] * 4, compiler_params=_params(1),
    )(w, m, v, gparts)


def adamw_nd(w, m, v, gparts, *, name):
    shape = w.shape
    two = (int(np.prod(shape[:-1])), shape[-1])
    rows = two[0]
    block_rows = rows
    for cand in (512, 256):
        if rows > cand and rows % cand == 0:
            block_rows = cand
            break
    outs = adamw(w.reshape(two), m.reshape(two), v.reshape(two), gparts.reshape((gparts.shape[0],) + two),
                 block_rows=block_rows, name=name)
    return [o.reshape(shape) for o in outs]


def sum_parts(parts, *, name):
    n, R, C = parts.shape

    def body(p_ref, o_ref):
        acc = p_ref[0]
        for k in range(1, n):
            acc = acc + p_ref[k]
        o_ref[...] = acc

    return pl.pallas_call(
        body, name=name, in_specs=[pl.BlockSpec((n, R, C), lambda: (0, 0, 0))],
        out_specs=pl.BlockSpec((R, C), lambda: (0, 0)), out_shape=jax.ShapeDtypeStruct((R, C), F32),
        compiler_params=pltpu.CompilerParams(vmem_limit_bytes=VMEM_LIMIT),
    )(parts)


def _my_place():
    return lax.axis_index("x"), lax.axis_index("y"), lax.axis_index("c")


def _full_shape(kind, shard):
    n_l, rows, cols = shard
    return {"col": (n_l, rows, N_DEV * cols), "row": (n_l, N_DEV * rows, cols), "stk": (N_DEV, n_l, rows, cols)}[kind]


def _slab(ref, kind, dev, shard):
    _, rows, cols = shard
    if kind == "col":
        return ref.at[:, :, pl.ds(pl.multiple_of(dev * cols, 128), cols)]
    if kind == "row":
        return ref.at[:, pl.ds(pl.multiple_of(dev * rows, 8), rows), :]
    return ref.at[dev]


def all_gather(x_shard, *, name):
    m_per, n = x_shard.shape

    def body(x_ref, out_ref, send_sems, recv_sems, local_sem):
        x, y, c = _my_place()
        me, sibling = (x, y, c), (x, y, 1 - c)
        chips = [(1 - x, y), (x, 1 - y), (1 - x, 1 - y)]

        def rows(px, py, pc):
            return out_ref.at[pl.ds((4 * px + 2 * py + pc) * m_per, m_per), :]

        def copy(k, block, to, src=None):
            return pltpu.make_async_remote_copy(
                src_ref=rows(*block) if src is None else src, dst_ref=rows(*block),
                send_sem=send_sems.at[k], recv_sem=recv_sems.at[k], device_id=to, device_id_type=MESH)

        mine = pltpu.make_async_copy(x_ref, rows(*me), local_sem)
        mine.start()
        first = [copy(0, me, sibling, src=x_ref)]
        first += [copy(1 + j, me, (*chip, c), src=x_ref) for j, chip in enumerate(chips)]
        for cp in first:
            cp.start()
        passed = [copy(4 + j, (*chip, c), sibling) for j, chip in enumerate(chips)]
        for j, chip in enumerate(chips):
            copy(1 + j, (*chip, c), me).wait_recv()
            passed[j].start()
        copy(0, sibling, me).wait_recv()
        for j, chip in enumerate(chips):
            copy(4 + j, (*chip, 1 - c), me).wait_recv()
        for cp in first + passed:
            cp.wait_send()
        mine.wait()

    return pl.pallas_call(
        body, name=name, out_shape=jax.ShapeDtypeStruct((N_DEV * m_per, n), x_shard.dtype),
        in_specs=[pl.BlockSpec(memory_space=pltpu.VMEM)], out_specs=pl.BlockSpec(memory_space=pltpu.VMEM),
        scratch_shapes=[pltpu.SemaphoreType.DMA((7,)), pltpu.SemaphoreType.DMA((7,)), pltpu.SemaphoreType.DMA],
        compiler_params=pltpu.CompilerParams(vmem_limit_bytes=VMEM_LIMIT),
    )(x_shard)


def gather_plan(shards, kinds, layers):
    n_t = len(shards)
    shapes = [(1,) + tuple(s.shape[1:]) for s in shards]

    def copies(x_refs, out_refs, sems):
        send_sems, recv_sems, local_sems = sems
        x, y, c = _my_place()
        me, sibling = (x, y, c), (x, y, 1 - c)
        chips = [(1 - x, y), (x, 1 - y), (1 - x, 1 - y)]
        own = [x_refs[t].at[pl.ds(layers[t], 1)] for t in range(n_t)]

        def slab(t, px, py, pc):
            return _slab(out_refs[t], kinds[t], 4 * px + 2 * py + pc, shapes[t])

        def copy(t, k, block, to, src=None):
            return pltpu.make_async_remote_copy(
                src_ref=slab(t, *block) if src is None else src, dst_ref=slab(t, *block),
                send_sem=send_sems.at[7 * t + k], recv_sem=recv_sems.at[7 * t + k], device_id=to,
                device_id_type=MESH)

        mine = [pltpu.make_async_copy(own[t], slab(t, *me), local_sems.at[t]) for t in range(n_t)]
        sends = []
        for t in range(n_t):
            sends.append(copy(t, 0, me, sibling, src=own[t]))
            sends += [copy(t, 1 + j, me, (*chip, c), src=own[t]) for j, chip in enumerate(chips)]
        return mine, sends, copy, me, sibling, chips, c

    def first(x_refs, out_refs, sems):
        mine, sends = copies(x_refs, out_refs, sems)[:2]
        for cp in mine + sends:
            cp.start()

    def forward(x_refs, out_refs, sems):
        _, _, copy, me, sibling, chips, c = copies(x_refs, out_refs, sems)
        for j, chip in enumerate(chips):
            for t in range(n_t):
                copy(t, 1 + j, (*chip, c), me).wait_recv()
                copy(t, 4 + j, (*chip, c), sibling).start()

    def finish(x_refs, out_refs, sems):
        mine, sends, copy, me, sibling, chips, c = copies(x_refs, out_refs, sems)
        passed = [copy(t, 4 + j, (*chip, c), sibling) for j, chip in enumerate(chips) for t in range(n_t)]
        for t in range(n_t):
            copy(t, 0, sibling, me).wait_recv()
        for j, chip in enumerate(chips):
            for t in range(n_t):
                copy(t, 4 + j, (*chip, 1 - c), me).wait_recv()
        for cp in sends + passed:
            cp.wait_send()
        for cp in mine:
            cp.wait()

    def last(x_refs, out_refs, sems):
        forward(x_refs, out_refs, sems)
        finish(x_refs, out_refs, sems)

    return Hosted(
        list(shards), [jax.ShapeDtypeStruct(_full_shape(k, shp), s.dtype) for k, shp, s in zip(kinds, shapes, shards)],
        [pltpu.SemaphoreType.DMA((7 * n_t,)), pltpu.SemaphoreType.DMA((7 * n_t,)), pltpu.SemaphoreType.DMA((n_t,))],
        first, last, mid=forward, last_after_mid=finish)


def scatter_plan(grads, kinds, shapes):
    n_t = len(grads)

    def copies(g_refs, out_refs, sems):
        send_sems, recv_sems, local_sems = sems
        x, y, c = _my_place()
        me = 4 * x + 2 * y + c
        local = [pltpu.make_async_copy(_slab(g_refs[t], kinds[t], me, shapes[t]), out_refs[t].at[me],
                                       local_sems.at[t]) for t in range(n_t)]
        remote = []
        for t in range(n_t):
            for r in range(1, N_DEV):
                px = 1 - x if r & 4 else x
                py = 1 - y if r & 2 else y
                pc = 1 - c if r & 1 else c
                remote.append(pltpu.make_async_remote_copy(
                    src_ref=_slab(g_refs[t], kinds[t], 4 * px + 2 * py + pc, shapes[t]), dst_ref=out_refs[t].at[me],
                    send_sem=send_sems.at[7 * t + r - 1], recv_sem=recv_sems.at[7 * t + r - 1],
                    device_id=(px, py, pc), device_id_type=MESH))
        return local, remote

    def first(g_refs, out_refs, sems):
        local, remote = copies(g_refs, out_refs, sems)
        for cp in local + remote:
            cp.start()

    def last(g_refs, out_refs, sems):
        local, remote = copies(g_refs, out_refs, sems)
        for cp in remote + local:
            cp.wait()

    return Hosted(
        list(grads), [jax.ShapeDtypeStruct((N_DEV,) + tuple(s), BF16) for s in shapes],
        [pltpu.SemaphoreType.DMA((7 * n_t,)), pltpu.SemaphoreType.DMA((7 * n_t,)), pltpu.SemaphoreType.DMA((n_t,))],
        first, last)


def adamw_layer(w, m, v, parts, layer, bufs, *, name):
    n_l, rows, cols = w.shape
    n = parts.shape[0]
    tr = min(rows, 256)

    def body(w_ref, m_ref, v_ref, g_ref, *rest):
        go_ref, d_ref, mo_ref, vo_ref = rest[-4:]
        g = g_ref[0].astype(F32)
        for k in range(1, n):
            g = g + g_ref[k].astype(F32)
        m_new = ADAM_B1 * m_ref[...] + (1.0 - ADAM_B1) * g
        v_new = ADAM_B2 * v_ref[...] + (1.0 - ADAM_B2) * (g * g)
        m_hat = m_new / (1.0 - ADAM_B1 ** ADAM_STEP)
        v_hat = v_new / (1.0 - ADAM_B2 ** ADAM_STEP)
        go_ref[...] = g
        d_ref[...] = -ADAM_LR * (m_hat / (jnp.sqrt(v_hat) + ADAM_EPS) + ADAM_WD * w_ref[...])
        mo_ref[...] = m_new
        vo_ref[...] = v_new

    blk = pl.BlockSpec((None, tr, cols), lambda i: (layer, i, 0))
    in_specs = [blk, blk, blk, pl.BlockSpec((n, None, tr, cols), lambda i: (0, 0, i, 0))]
    args = [w, m, v, parts]
    aliases = {}
    if bufs is not None:
        in_specs += [pl.BlockSpec(memory_space=pl.ANY)] * 4
        args += list(bufs)
        aliases = {4 + k: k for k in range(4)}
    return pl.pallas_call(
        body, name=name, grid=(rows // tr,), in_specs=in_specs, out_specs=[blk] * 4,
        out_shape=[jax.ShapeDtypeStruct((n_l, rows, cols), F32)] * 4, input_output_aliases=aliases,
        compiler_params=_params(1),
    )(*args)


BIG =("gla_w_in", "gla_w_out", "att_w_in", "att_w_out", "ff_w1", "ff_w2")
KIND = {"gla_w_in": "stk", "gla_w_out": "row", "att_w_in": "col", "att_w_out": "row", "ff_w1": "col", "ff_w2": "row"}


def _pack_small(arrs):
    parts = []
    for a in arrs:
        f = a.reshape(-1)
        parts.append(jnp.pad(f, (0, -f.shape[0] % 128)))
    flat = jnp.concatenate(parts)
    flat = jnp.pad(flat, (0, -flat.shape[0] % 1024))
    return flat.reshape(-1, 128)


def _unpack_small(packed, shapes):
    flat = packed.reshape(packed.shape[:-2] + (-1,))
    out, off = [], 0
    for shp in shapes:
        n = int(np.prod(shp))
        out.append(flat[..., off:off + n].reshape(packed.shape[:-2] + tuple(shp)))
        off += n + (-n % 128)
    return out


def _vec(a):
    return a.reshape(1, -1)


def _layer_weights(i):
    mixer = "gla" if i % 2 == 0 else "att"
    return [(f"{mixer}_w_in", i // 2), (f"{mixer}_w_out", i // 2), ("ff_w1", i), ("ff_w2", i)]


def _trunk(x, target, mods, sm, w, m, v):
    shard_bf = {n: w[n].astype(BF16) for n in BIG}

    def gather_of(names):
        return gather_plan([shard_bf[n] for n, _ in names], [KIND[n] for n, _ in names], [l for _, l in names])

    def gather_under_core(i):
        names = _layer_weights(i)[2:] + (_layer_weights(i + 1)[:2] if i + 1 < DEPTH else [])
        return names, gather_of(names)

    wts = {}

    def keep_gathered(names, arrays):
        for (n, l), a in zip(names, arrays):
            if KIND[n] == "stk":
                a = a.transpose(1, 2, 0, 3).reshape(1, D, GLA_IN)
                a = jnp.pad(a, ((0, 0), (0, 0), (0, GLA_INP - GLA_IN)))
            wts[n, l] = a

    first_names, rest_names = _layer_weights(0)[:1], _layer_weights(0)[1:2]
    keep_gathered(first_names, run_hosted(gather_of(first_names), name="gather_first"))

    rel_idx = _rel_index()
    saved = []
    for i in range(DEPTH):
        sh1, sc1, g1, sh2, sc2, g2 = [mods[i, k:k + 1] for k in range(6)]
        rec = {"x0": x}
        j = i // 2
        nxt_names, nxt_plan = gather_under_core(i)
        if i % 2 == 0:
            w2p = jnp.pad(sm["gla_w_gk2"][j], ((0, 128 - GLA_RANK), (0, 0)))
            bgk, gn = _vec(sm["gla_b_gk"][j]), _vec(sm["gla_g_norm"][j])
            if i == 0:
                proj, got0 = mm_nn(x, wts["gla_w_in", j], 0, pro="mod", p1=sc1, p2=sh1, tm=512, tn=GLA_INP,
                                   comm=gather_of(rest_names), name=f"gla_proj_{i}")
                keep_gathered(rest_names, got0)
            else:
                proj = mm_nn(x, wts["gla_w_in", j], 0, pro="mod", p1=sc1, p2=sh1, tm=512, tn=GLA_INP,
                             name=f"gla_proj_{i}")
            (og, states), got = gla_fwd(proj, w2p, bgk, gn, comm=nxt_plan, name=f"gla_core_{i}")
            y = mm_nn(og, wts["gla_w_out", j], 0, tm=1024, tn=1024, name=f"gla_out_{i}")
            rec.update(kind="gla", j=j, w2p=w2p, bgk=bgk, gn=gn, proj=proj, og=og, states=states)
        else:
            rel = sm["att_rel_bias"][j]
            rel_pad = jnp.pad(rel, ((0, 0), (0, REL_PAD - N_REL)), constant_values=NEG_INF)
            tiles = rel_bias_tiles(rel_pad, rel_idx, name=f"att_bias_{i}")
            tiles = tiles.reshape(ATT_H, len(REL_TILES), REL_TILE, REL_TILE)
            clip = jnp.broadcast_to(rel[:, 2 * MAX_REL][:, None, None], (ATT_H, 1, 128))
            qkv = mm_nn(x, wts["att_w_in", j], 0, pro="mod", p1=sc1, p2=sh1, bias=_vec(sm["att_b_in"][j]),
                        out_dtype=BF16, tm=512, tn=3 * D, name=f"att_proj_{i}")
            (o,), got = attn_fwd(qkv, tiles, clip, comm=nxt_plan, name=f"att_core_{i}")
            y = mm_nn(o, wts["att_w_out", j], 0, tm=1024, tn=1024, name=f"att_out_{i}")
            rec.update(kind="att", j=j, tiles=tiles, clip=clip, qkv=qkv, o=o)
        keep_gathered(nxt_names, got)
        x1 = ln_fwd(x, y, g1, _vec(sm["ln_g"][i, 0]), _vec(sm["ln_b"][i, 0]), name=f"ln_mix_{i}")
        h = mm_nn(x1, wts["ff_w1", i], 0, pro="mod", p1=sc2, p2=sh2, out_dtype=BF16, tm=512, tn=D_FF,
                  name=f"ff_up_{i}")
        y2 = mm_nn(h, wts["ff_w2", i], 0, pro="relu2", tm=512, tn=D, name=f"ff_down_{i}")
        rec.update(y=y, x1=x1, h=h, y2=y2)
        saved.append(rec)
        if i + 1 < DEPTH:
            x = ln_fwd(x1, y2, g2, _vec(sm["ln_g"][i, 1]), _vec(sm["ln_b"][i, 1]), name=f"ln_ff_{i}")

    gw = {}

    def wgrad(weight, layer, a, d, *, tn, tk=1024, tm=512, col_block0=0, **kw):
        gw[weight, layer] = mm_tn(a, d, tk=tk, tn=tn, tm=tm, out_buf=gw.get((weight, layer)),
                                  out_shape=wts[weight, layer].shape, col_block0=col_block0, **kw)

    def scatter_layer(units):
        grads = []
        for n, l in units:
            g = gw[n, l]
            if KIND[n] == "stk":
                g = g[:, :, :GLA_IN].reshape(1, D, N_DEV, GLA_IN // N_DEV).transpose(2, 0, 1, 3)
            grads.append(g)
        return scatter_plan(grads, [KIND[n] for n, _ in units], [(1,) + tuple(w[n].shape[1:]) for n, _ in units])

    results = {n: None for n in BIG}

    def update(units, parts):
        for (n, l), p in zip(units, parts):
            results[n] = adamw_layer(w[n], m[n], v[n], p, l, results[n], name=f"adamw_{n}_{l}")

    gs = {"ln_g": [[None, None] for _ in range(DEPTH)], "ln_b": [[None, None] for _ in range(DEPTH)],
          "gla_w_gk2": [None] * 2, "gla_b_gk": [None] * 2, "gla_g_norm": [None] * 2, "att_b_in": [None] * 2,
          "att_rel_bias": [None] * 2}
    dmods = [[None] * 6 for _ in range(DEPTH)]
    nxt = None
    nxt_slot = None
    for i in reversed(range(DEPTH)):
        rec = saved[i]
        sh1, sc1, g1, sh2, sc2, g2 = [mods[i, k:k + 1] for k in range(6)]
        x0, x1 = rec["x0"], rec["x1"]
        if nxt is None:
            dz2, acc, dzs2, loss = ln_bwd(x1, rec["y2"], g2, _vec(sm["ln_g"][i, 1]),
                                          loss=(target, _vec(sm["ln_b"][i, 1])), name=f"ln_ff_bwd_{i}")
        else:
            dz2, acc, dzs2 = ln_bwd(x1, rec["y2"], g2, _vec(sm["ln_g"][i, 1]),
                                    nxt=nxt[:3] + (_vec(sm["ln_b"][i, 1]),), name=f"ln_ff_bwd_{i}")
            dmods[nxt_slot[0]][nxt_slot[1]] = acc[3]
            dmods[nxt_slot[0]][nxt_slot[2]] = acc[4]
        gs["ln_g"][i][1], gs["ln_b"][i][1], dmods[i][5] = acc[0], acc[1], acc[2]
        dh = mm_nt([dzs2], wts["ff_w2", i], 0, epi_h=rec["h"], out_dtype=BF16, tm=1024, tn=1024,
                   name=f"ff_down_bwd_{i}")
        wgrad("ff_w2", i, rec["h"], dzs2, pro="relu2", tk=2048, tn=1024, name=f"ff_w2_grad_{i}")
        du2 = mm_nt([dh], wts["ff_w1", i], 0, tm=1024, tn=1024, name=f"ff_up_bwd_{i}")
        wgrad("ff_w1", i, x1, dh, pro="mod", p1=sc2, p2=sh2, tn=2048, name=f"ff_w1_grad_{i}")
        dz1, acc, dzs1 = ln_bwd(x0, rec["y"], g1, _vec(sm["ln_g"][i, 0]),
                                nxt=(dz2, du2, sc2, _vec(sm["ln_b"][i, 0])), name=f"ln_mix_bwd_{i}")
        dmods[i][4], dmods[i][3] = acc[3], acc[4]
        gs["ln_g"][i][0], gs["ln_b"][i][0], dmods[i][2] = acc[0], acc[1], acc[2]
        j = rec["j"]
        w_in, w_out = _layer_weights(i)[:2]
        if rec["kind"] == "gla":
            dog = mm_nt([dzs1], wts[w_out], 0, tm=1024, tn=1024, name=f"gla_out_bwd_{i}")
            wgrad(*w_out, rec["og"], dzs1, tn=1024, name=f"gla_wout_grad_{i}")
        else:
            do = mm_nt([dzs1], wts[w_out], 0, out_dtype=BF16, tm=1024, tn=1024, name=f"att_out_bwd_{i}")
            wgrad(*w_out, rec["o"], dzs1, tn=1024, name=f"att_wout_grad_{i}")
        units = [("ff_w1", i), ("ff_w2", i), w_out] + ([_layer_weights(i + 1)[0]] if i + 1 < DEPTH else [])
        plan = scatter_layer(units)
        if rec["kind"] == "gla":
            (dproj, dw2p, dbgk, dgn), parts = gla_bwd(rec["proj"], dog, rec["states"], rec["w2p"], rec["bgk"],
                                                      rec["gn"], comm=plan, name=f"gla_core_bwd_{i}")
            gs["gla_w_gk2"][j], gs["gla_b_gk"][j], gs["gla_g_norm"][j] = dw2p[:GLA_RANK], dbgk[0], dgn[0]
            wgrad(*w_in, x0, dproj, pro="mod", p1=sc1, p2=sh1, tk=512, tn=GLA_INP, name=f"gla_win_grad_{i}")
            if i == 0:
                du1, last_parts = mm_nt([dproj], wts[w_in], 0, tm=1024, tn=1024, comm=scatter_layer([w_in]),
                                        name=f"gla_proj_bwd_{i}")
                update([w_in], last_parts)
            else:
                du1 = mm_nt([dproj], wts[w_in], 0, tm=1024, tn=1024, name=f"gla_proj_bwd_{i}")
        else:
            (dq, dk, dv, dtiles, dclip, sq, sk, sv), parts = attn_bwd(rec["qkv"], do, rec["tiles"], rec["clip"],
                                                                      comm=plan, name=f"att_core_bwd_{i}")
            drel = rel_bias_grad(dtiles.reshape(ATT_H, -1), dclip.reshape(ATT_H, 128), rel_idx,
                                 name=f"att_bias_grad_{i}")
            gs["att_rel_bias"][j] = drel[:, :N_REL]
            gs["att_b_in"][j] = jnp.concatenate([sq[0], sk[0], sv[0]])
            du1 = mm_nt([dq, dk, dv], wts[w_in], 0, tm=512, tn=1024, name=f"att_proj_bwd_{i}")
            for n, t in enumerate((dq, dk, dv)):
                wgrad(*w_in, x0, t, pro="mod", p1=sc1, p2=sh1, tn=1024, col_block0=n, name=f"att_win_grad_{i}_{n}")
        update(units, parts)
        nxt = (dz1, du1, sc1, x0)
        nxt_slot = (i, 1, 0)
    dx, acc = combine_final(nxt[0], nxt[1], nxt[2], nxt[3], name="grad_x")
    dmods[0][1], dmods[0][0] = acc[3], acc[4]
    dmods = jnp.stack([jnp.stack(r) for r in dmods])
    gs = {k: jnp.stack([jnp.stack(r) if isinstance(r, list) else r for r in v]) for k, v in gs.items()}
    return loss, dx, dmods, gs, results


WEIGHTS = ("w_ada", "b_ada", "ln_g", "ln_b", "gla_w_in", "gla_w_gk2", "gla_b_gk", "gla_g_norm", "gla_w_out",
           "att_w_in", "att_b_in", "att_rel_bias", "att_w_out", "ff_w1", "ff_w2")
SMALL_SHARDED = {"ln_g": 2, "ln_b": 2, "gla_w_gk2": 2, "gla_g_norm": 2, "att_b_in": 1}
SMALL_GRADS = ("ln_g", "ln_b", "gla_w_gk2", "gla_b_gk", "gla_g_norm", "att_b_in", "att_rel_bias")


def kernel(x, c, w_ada, b_ada, ln_g, ln_b, gla_w_in, gla_w_gk2, gla_b_gk, gla_g_norm, gla_w_out, att_w_in, att_b_in, att_rel_bias, att_w_out, ff_w1, ff_w2, loss_target, m_w_ada, m_b_ada, m_ln_g, m_ln_b, m_gla_w_in, m_gla_w_gk2, m_gla_b_gk, m_gla_g_norm, m_gla_w_out, m_att_w_in, m_att_b_in, m_att_rel_bias, m_att_w_out, m_ff_w1, m_ff_w2, v_w_ada, v_b_ada, v_ln_g, v_ln_b, v_gla_w_in, v_gla_w_gk2, v_gla_b_gk, v_gla_g_norm, v_gla_w_out, v_att_w_in, v_att_b_in, v_att_rel_bias, v_att_w_out, v_ff_w1, v_ff_w2):
    w = dict(w_ada=w_ada, b_ada=b_ada, ln_g=ln_g, ln_b=ln_b, gla_w_in=gla_w_in, gla_w_gk2=gla_w_gk2,
             gla_b_gk=gla_b_gk, gla_g_norm=gla_g_norm, gla_w_out=gla_w_out, att_w_in=att_w_in, att_b_in=att_b_in,
             att_rel_bias=att_rel_bias, att_w_out=att_w_out, ff_w1=ff_w1, ff_w2=ff_w2)
    m = dict(w_ada=m_w_ada, b_ada=m_b_ada, ln_g=m_ln_g, ln_b=m_ln_b, gla_w_in=m_gla_w_in, gla_w_gk2=m_gla_w_gk2,
             gla_b_gk=m_gla_b_gk, gla_g_norm=m_gla_g_norm, gla_w_out=m_gla_w_out, att_w_in=m_att_w_in,
             att_b_in=m_att_b_in, att_rel_bias=m_att_rel_bias, att_w_out=m_att_w_out, ff_w1=m_ff_w1, ff_w2=m_ff_w2)
    v = dict(w_ada=v_w_ada, b_ada=v_b_ada, ln_g=v_ln_g, ln_b=v_ln_b, gla_w_in=v_gla_w_in, gla_w_gk2=v_gla_w_gk2,
             gla_b_gk=v_gla_b_gk, gla_g_norm=v_gla_g_norm, gla_w_out=v_gla_w_out, att_w_in=v_att_w_in,
             att_b_in=v_att_b_in, att_rel_bias=v_att_rel_bias, att_w_out=v_att_w_out, ff_w1=v_ff_w1, ff_w2=v_ff_w2)
    xi, yi, ci = _my_place()
    me = 4 * xi + 2 * yi + ci

    small_names = tuple(SMALL_SHARDED)
    small_in = _pack_small([c] + [w[n] for n in small_names])
    small_all = all_gather(small_in, name="gather_small").reshape(N_DEV, -1, 128)
    parts = _unpack_small(small_all, [c.shape] + [w[n].shape for n in small_names])
    c_all = parts[0].reshape(N_DEV, D)
    sm = {"gla_b_gk": gla_b_gk, "att_rel_bias": att_rel_bias}
    for n, p in zip(small_names, parts[1:]):
        ax = SMALL_SHARDED[n]
        sm[n] = jnp.moveaxis(p, 0, ax).reshape(p.shape[1:ax + 1] + (N_DEV * p.shape[ax + 1],) + p.shape[ax + 2:])

    n_ada = w_ada.shape[2]
    mp = mods_partial(c_all, w_ada, name="mods_partial")
    mp_all = all_gather(mp.reshape(DEPTH * N_DEV, n_ada), name="gather_mods")
    mp_all = mp_all.reshape(N_DEV, DEPTH, N_DEV, n_ada)
    mods = lax.dynamic_index_in_dim(mp_all, me, axis=2, keepdims=False)
    mods = mods.transpose(1, 0, 2).reshape(DEPTH, 6 * D) + b_ada
    mods = mods.reshape(DEPTH, 6, D)

    loss, dx, dmods, gs, results = _trunk(x.reshape(x.shape[1:]), loss_target.reshape(x.shape[1:]), mods, sm, w, m, v)
    loss = lax.psum(loss[0, 0], ("x", "y", "c"))

    dm_flat = dmods.reshape(DEPTH, 6 * D)
    small_g = [dm_flat] + [gs[n].reshape(sm[n].shape) for n in SMALL_GRADS]
    small_shapes = [a.shape for a in small_g]
    sg_all = all_gather(_pack_small(small_g), name="gather_small_grads").reshape(N_DEV, -1, 128)
    summed = _unpack_small(sum_parts(sg_all, name="sum_small_grads"), small_shapes)
    g_full = dict(zip(("b_ada",) + SMALL_GRADS, summed))
    dm_all = _unpack_small(sg_all, small_shapes)[0]
    dm_mine = lax.dynamic_slice_in_dim(dm_all, me * n_ada, n_ada, axis=2).transpose(1, 0, 2)
    g_w_ada = w_ada_grad(c_all, dm_mine, name="w_ada_grad")

    results["w_ada"] = adamw_nd(w_ada, m_w_ada, v_w_ada, g_w_ada[None], name="adamw_w_ada")
    for n in ("b_ada",) + SMALL_GRADS:
        g = g_full[n]
        if n in SMALL_SHARDED:
            ax = SMALL_SHARDED[n]
            width = w[n].shape[ax]
            g = lax.dynamic_slice_in_dim(g, me * width, width, axis=ax)
        results[n] = adamw_nd(w[n], m[n], v[n], g[None], name=f"adamw_{n}")

    out = [loss, dx[None]]
    for k in range(4):
        out += [results[n][k] for n in WEIGHTS]
    return tuple(out)
```

```python
import numpy as np
import jax
import jax.numpy as jnp
from jax import lax
from jax.experimental import pallas as pl
from jax.experimental.pallas import tpu as pltpu

F32 = jnp.float32
BF16 = jnp.bfloat16
HIGHEST = lax.Precision.HIGHEST
MESH = pl.DeviceIdType.MESH

N_DEV = 8
D = 1024
DEPTH = 4
CHUNK = 64
ALPHA = (2.0 * DEPTH) ** 0.25
LN_EPS = 1e-5
RMS_EPS = 1e-6
NEG_INF = -1e30

GLA_H = 4
GLA_DKH = 128
GLA_DVH = 256
GLA_DK = GLA_H * GLA_DKH
GLA_DV = GLA_H * GLA_DVH
GLA_RANK = 16
GLA_IN = 2 * GLA_DK + 2 * GLA_DV + GLA_RANK
GLA_INP = 3200
GLA_TAU_INV = 1.0 / 16.0
GLA_SUB = 2

ATT_H = 16
ATT_HD = 64
ATT_QB = 256
ATT_KB = 3 * ATT_QB
LEFT = 8 * CHUNK
MAX_REL = 128
N_REL = 2 * MAX_REL + 1
REL_PAD = 384
REL_TILE = 128
REL_TILES = (3, 4)
D_FF = 4 * D

ADAM_LR = 0.001
ADAM_B1 = 0.9
ADAM_B2 = 0.999
ADAM_EPS = 1e-08
ADAM_WD = 0.01
ADAM_STEP = 10

VMEM_LIMIT = 48 * 1024 * 1024


def _params(n_axes):
    return pltpu.CompilerParams(dimension_semantics=("arbitrary",) * n_axes, vmem_limit_bytes=VMEM_LIMIT)


def _dot(a, b):
    return jnp.dot(a, b, preferred_element_type=F32)


def _dot_nt(a, b):
    return lax.dot_general(a, b, (((1,), (1,)), ((), ())), preferred_element_type=F32)


def _dot_tn(a, b):
    return lax.dot_general(a, b, (((0,), (0,)), ((), ())), preferred_element_type=F32)


def _bf(a):
    return a.astype(BF16)


def _prologue(kind, a, p1=None, p2=None):
    if kind == "mod":
        return a.astype(F32) * (1.0 + p1) + p2
    if kind == "relu2":
        r = jnp.maximum(a, 0.0)
        return r * r
    return a


class Hosted:
    def __init__(self, inputs, out_shapes, sems, first, last, mid=None, last_after_mid=None):
        self.inputs, self.out_shapes, self.sems, self.first, self.last = inputs, out_shapes, sems, first, last
        self.mid, self.last_after_mid = mid, last_after_mid


def _hbm_specs(n):
    return [pl.BlockSpec(memory_space=pltpu.HBM)] * n


def _call_hosting(body, comm, *, first, last, in_specs, out_specs, out_shape, scratch_shapes, args, mid=None, **kw):
    if comm is None:
        return pl.pallas_call(body, in_specs=in_specs, out_specs=out_specs, out_shape=out_shape,
                              scratch_shapes=scratch_shapes, **kw)(*args), []
    n_in, n_out, n_scr = len(in_specs), len(out_specs), len(scratch_shapes)
    n_ci, n_co = len(comm.inputs), len(comm.out_shapes)

    def hosting(*refs):
        ins, ci = refs[:n_in], refs[n_in:n_in + n_ci]
        k = n_in + n_ci
        outs, co = refs[k:k + n_out], refs[k + n_out:k + n_out + n_co]
        k += n_out + n_co
        scr, cs = refs[k:k + n_scr], refs[k + n_scr:]

        @pl.when(first())
        def _():
            comm.first(ci, co, cs)

        body(*ins, *outs, *scr)
        split = mid is not None and comm.mid is not None
        if split:
            @pl.when(mid())
            def _():
                comm.mid(ci, co, cs)

        @pl.when(last())
        def _():
            (comm.last_after_mid if split else comm.last)(ci, co, cs)

    res = pl.pallas_call(
        hosting, in_specs=list(in_specs) + _hbm_specs(n_ci), out_specs=list(out_specs) + _hbm_specs(n_co),
        out_shape=list(out_shape) + list(comm.out_shapes), scratch_shapes=list(scratch_shapes) + list(comm.sems),
        **kw)(*args, *comm.inputs)
    return res[:n_out], res[n_out:]


def run_hosted(comm, *, name):
    n_i, n_o = len(comm.inputs), len(comm.out_shapes)

    def body(*refs):
        ins, outs, sems = refs[:n_i], refs[n_i:n_i + n_o], refs[n_i + n_o:]
        comm.first(ins, outs, sems)
        comm.last(ins, outs, sems)

    return pl.pallas_call(body, name=name, out_shape=list(comm.out_shapes), in_specs=_hbm_specs(n_i),
                          out_specs=_hbm_specs(n_o), scratch_shapes=list(comm.sems))(*comm.inputs)


def mm_nn(a, b, layer, *, pro=None, p1=None, p2=None, bias=None, out_dtype=F32, tm, tn, comm=None, name):
    M, K = a.shape
    N = b.shape[2]
    tm = min(tm, M)
    n_p = {"mod": 2}.get(pro, 0)
    has_bias = bias is not None
    direct = pro is None and a.dtype == BF16

    def body(*refs):
        a_ref, b_ref = refs[0], refs[1]
        p_refs = refs[2:2 + n_p]
        bias_ref = refs[2 + n_p] if has_bias else None
        if direct:
            o_ref = refs[-1]
            lhs = a_ref[...]
        else:
            o_ref, abf = refs[-2], refs[-1]

            @pl.when(pl.program_id(1) == 0)
            def _():
                abf[...] = _bf(_prologue(pro, a_ref[...], *[r[...] for r in p_refs]))

            lhs = abf[...]
        acc = _dot(lhs, b_ref[...])
        if has_bias:
            acc = acc + bias_ref[...]
        o_ref[...] = acc.astype(out_dtype)

    in_specs = [pl.BlockSpec((tm, K), lambda i, j: (i, 0)), pl.BlockSpec((None, K, tn), lambda i, j: (layer, 0, j))]
    args = [a, b]
    for p in (p1, p2)[:n_p]:
        in_specs.append(pl.BlockSpec((1, K), lambda i, j: (0, 0)))
        args.append(p)
    if has_bias:
        in_specs.append(pl.BlockSpec((1, tn), lambda i, j: (0, j)))
        args.append(bias)
    n_i, n_j = M // tm, N // tn
    (out,), got = _call_hosting(
        body, comm, first=lambda: (pl.program_id(0) == 0) & (pl.program_id(1) == 0),
        last=lambda: (pl.program_id(0) == n_i - 1) & (pl.program_id(1) == n_j - 1),
        name=name, grid=(n_i, n_j), in_specs=in_specs,
        out_specs=[pl.BlockSpec((tm, tn), lambda i, j: (i, j))],
        out_shape=[jax.ShapeDtypeStruct((M, N), out_dtype)],
        scratch_shapes=[] if direct else [pltpu.VMEM((tm, K), BF16)], compiler_params=_params(2), args=args)
    return out if comm is None else (out, got)


def mm_nt(a_parts, w, layer, *, epi_h=None, out_dtype=F32, tm, tn, comm=None, name):
    M = a_parts[0].shape[0]
    tm = min(tm, M)
    widths = [p.shape[1] for p in a_parts]
    Nw = sum(widths)
    Kw = w.shape[1]
    n_a = len(a_parts)
    has_h = epi_h is not None
    direct = n_a == 1 and a_parts[0].dtype == BF16

    def body(*refs):
        a_refs = refs[:n_a]
        w_ref = refs[n_a]
        h_ref = refs[n_a + 1] if has_h else None
        if direct:
            o_ref = refs[-1]
            lhs = a_refs[0][...]
        else:
            o_ref, abf = refs[-2], refs[-1]

            @pl.when(pl.program_id(1) == 0)
            def _():
                off = 0
                for r, wd in zip(a_refs, widths):
                    abf[:, off:off + wd] = _bf(r[...])
                    off += wd

            lhs = abf[...]
        acc = _dot_nt(lhs, w_ref[...])
        if has_h:
            acc = acc * (2.0 * jnp.maximum(h_ref[...], 0.0))
        o_ref[...] = acc.astype(out_dtype)

    in_specs = [pl.BlockSpec((tm, wd), lambda i, j: (i, 0)) for wd in widths]
    in_specs.append(pl.BlockSpec((None, tn, Nw), lambda i, j: (layer, j, 0)))
    args = list(a_parts) + [w]
    if has_h:
        in_specs.append(pl.BlockSpec((tm, tn), lambda i, j: (i, j)))
        args.append(epi_h)
    n_i, n_j = M // tm, Kw // tn
    (out,), got = _call_hosting(
        body, comm, first=lambda: (pl.program_id(0) == 0) & (pl.program_id(1) == 0),
        last=lambda: (pl.program_id(0) == n_i - 1) & (pl.program_id(1) == n_j - 1),
        name=name, grid=(n_i, n_j), in_specs=in_specs,
        out_specs=[pl.BlockSpec((tm, tn), lambda i, j: (i, j))],
        out_shape=[jax.ShapeDtypeStruct((M, Kw), out_dtype)],
        scratch_shapes=[] if direct else [pltpu.VMEM((tm, Nw), BF16)], compiler_params=_params(2), args=args)
    return out if comm is None else (out, got)


def mm_tn(a, d, *, pro=None, p1=None, p2=None, tk, tn, tm, out_buf, out_shape, col_block0=0, name):
    M, Kf = a.shape
    N = d.shape[1]
    n_p = {"mod": 2}.get(pro, 0)
    has_buf = out_buf is not None
    n_m = M // tm

    def body(*refs):
        a_ref, d_ref = refs[0], refs[1]
        p_refs = refs[2:2 + n_p]
        o_ref, acc = refs[-2], refs[-1]
        m = pl.program_id(2)

        @pl.when(m == 0)
        def _():
            acc[...] = jnp.zeros_like(acc)

        av = _prologue(pro, a_ref[...], *[r[...] for r in p_refs])
        acc[...] += _dot_tn(_bf(av), _bf(d_ref[...]))

        @pl.when(m == n_m - 1)
        def _():
            o_ref[...] = _bf(acc[...])

    in_specs = [pl.BlockSpec((tm, tk), lambda i, j, m: (m, i)), pl.BlockSpec((tm, tn), lambda i, j, m: (m, j))]
    args = [a, d]
    for p in (p1, p2)[:n_p]:
        in_specs.append(pl.BlockSpec((1, tk), lambda i, j, m: (0, i)))
        args.append(p)
    aliases = {}
    if has_buf:
        in_specs.append(pl.BlockSpec(memory_space=pl.ANY))
        args.append(out_buf)
        aliases = {len(args) - 1: 0}
    return pl.pallas_call(
        body, name=name, grid=(Kf // tk, N // tn, n_m), in_specs=in_specs,
        out_specs=pl.BlockSpec((None, tk, tn), lambda i, j, m: (0, i, col_block0 + j)),
        out_shape=jax.ShapeDtypeStruct(out_shape, BF16), input_output_aliases=aliases,
        scratch_shapes=[pltpu.VMEM((tk, tn), F32)], compiler_params=_params(3),
    )(*args)


ROW_BLOCK = 512
ACC_ROWS = 8


def _ln_stats(z):
    mu = jnp.mean(z, axis=-1, keepdims=True)
    zc = z - mu
    var = jnp.mean(zc * zc, axis=-1, keepdims=True)
    return zc, lax.rsqrt(var + LN_EPS)


def ln_fwd(x, y, gate, lng, lnb, *, name):
    S = x.shape[0]

    def body(x_ref, y_ref, gt_ref, g_ref, b_ref, o_ref):
        z = ALPHA * x_ref[...] + (1.0 + gt_ref[...]) * y_ref[...]
        zc, rstd = _ln_stats(z)
        o_ref[...] = (zc * rstd) * g_ref[...] + b_ref[...]

    row = pl.BlockSpec((ROW_BLOCK, D), lambda i: (i, 0))
    vec = pl.BlockSpec((1, D), lambda i: (0, 0))
    return pl.pallas_call(
        body, name=name, grid=(S // ROW_BLOCK,), in_specs=[row, row, vec, vec, vec], out_specs=row,
        out_shape=jax.ShapeDtypeStruct((S, D), F32), compiler_params=_params(1),
    )(x, y, gate, lng, lnb)


def _add_colsum(acc_ref, r, val):
    acc_ref[r:r + 1, :] += jnp.sum(val, axis=0, keepdims=True)


def ln_bwd(x_in, y, gate, lng, *, loss=None, nxt=None, name):
    S = x_in.shape[0]
    has_next = nxt is not None

    def body(*refs):
        if has_next:
            dzn_ref, dun_ref, scn_ref, b_ref = refs[:4]
            k = 4
        else:
            t_ref, b_ref = refs[:2]
            k = 2
        x_ref, y_ref, gt_ref, g_ref = refs[k:k + 4]
        dz_ref, acc_ref, dzs_ref = refs[k + 4:k + 7]
        loss_ref = None if has_next else refs[k + 7]

        @pl.when(pl.program_id(0) == 0)
        def _():
            acc_ref[...] = jnp.zeros_like(acc_ref)
            if not has_next:
                loss_ref[...] = jnp.zeros_like(loss_ref)

        yv = y_ref[...]
        z = ALPHA * x_ref[...] + (1.0 + gt_ref[...]) * yv
        zc, rstd = _ln_stats(z)
        xhat = zc * rstd
        if has_next:
            du = dun_ref[...]
            dout_v = ALPHA * dzn_ref[...] + du * (1.0 + scn_ref[...])
            _add_colsum(acc_ref, 3, du * (xhat * g_ref[...] + b_ref[...]))
            _add_colsum(acc_ref, 4, du)
        else:
            e = (xhat * g_ref[...] + b_ref[...]) - t_ref[...]
            dout_v = e * (1.0 / D)
            per_tok = jnp.sum(e * e, axis=1, keepdims=True) * (1.0 / D)
            loss_ref[...] += 0.5 * jnp.sum(per_tok, axis=0, keepdims=True)
        _add_colsum(acc_ref, 0, dout_v * xhat)
        _add_colsum(acc_ref, 1, dout_v)
        dxh = dout_v * g_ref[...]
        m1 = jnp.mean(dxh, axis=-1, keepdims=True)
        m2 = jnp.mean(dxh * xhat, axis=-1, keepdims=True)
        dz = rstd * (dxh - m1 - xhat * m2)
        _add_colsum(acc_ref, 2, dz * yv)
        dz_ref[...] = dz
        dzs_ref[...] = _bf(dz * (1.0 + gt_ref[...]))

    row = pl.BlockSpec((ROW_BLOCK, D), lambda i: (i, 0))
    vec = pl.BlockSpec((1, D), lambda i: (0, 0))
    if has_next:
        in_specs = [row, row, vec, vec]
        args = list(nxt)
    else:
        in_specs = [row, vec]
        args = list(loss)
    in_specs += [row, row, vec, vec]
    args += [x_in, y, gate, lng]
    out_specs = [row, pl.BlockSpec((ACC_ROWS, D), lambda i: (0, 0)), row]
    out_shape = [jax.ShapeDtypeStruct((S, D), F32), jax.ShapeDtypeStruct((ACC_ROWS, D), F32),
                 jax.ShapeDtypeStruct((S, D), BF16)]
    if not has_next:
        out_specs.append(pl.BlockSpec((8, 128), lambda i: (0, 0)))
        out_shape.append(jax.ShapeDtypeStruct((8, 128), F32))
    return pl.pallas_call(
        body, name=name, grid=(S // ROW_BLOCK,), in_specs=in_specs, out_specs=out_specs, out_shape=out_shape,
        compiler_params=_params(1),
    )(*args)


def combine_final(dz, du, sc, x_in, *, name):
    S = dz.shape[0]

    def body(dz_ref, du_ref, sc_ref, x_ref, dx_ref, acc_ref):
        @pl.when(pl.program_id(0) == 0)
        def _():
            acc_ref[...] = jnp.zeros_like(acc_ref)

        du_v = du_ref[...]
        dx_ref[...] = ALPHA * dz_ref[...] + du_v * (1.0 + sc_ref[...])
        _add_colsum(acc_ref, 3, du_v * x_ref[...])
        _add_colsum(acc_ref, 4, du_v)

    row = pl.BlockSpec((ROW_BLOCK, D), lambda i: (i, 0))
    vec = pl.BlockSpec((1, D), lambda i: (0, 0))
    return pl.pallas_call(
        body, name=name, grid=(S // ROW_BLOCK,), in_specs=[row, row, vec, row],
        out_specs=[row, pl.BlockSpec((ACC_ROWS, D), lambda i: (0, 0))],
        out_shape=[jax.ShapeDtypeStruct((S, D), F32), jax.ShapeDtypeStruct((ACC_ROWS, D), F32)],
        compiler_params=_params(1),
    )(dz, du, sc, x_in)


def _log_sigmoid(x):
    return jnp.minimum(x, 0.0) - jnp.log(1.0 + jnp.exp(-jnp.abs(x)))


def _silu(x):
    return x * (1.0 / (1.0 + jnp.exp(-x)))


def _cumsum_steps(x):
    row = lax.broadcasted_iota(jnp.int32, x.shape, 0)
    step = 1
    while step < x.shape[0]:
        x = x + jnp.where(row >= step, pltpu.roll(x, step, 0), 0.0)
        step *= 2
    return x


@jax.custom_vjp
def _cumsum_rows(x):
    return _cumsum_steps(x)


def _cumsum_rows_fwd(x):
    return _cumsum_steps(x), None


def _cumsum_rows_bwd(_, g):
    return (jnp.sum(g, axis=0, keepdims=True) - _cumsum_steps(g) + g,)


_cumsum_rows.defvjp(_cumsum_rows_fwd, _cumsum_rows_bwd)


def _gla_chunk(q, k, v, g, gk, s0t, w2p, bgk, gn):
    C = q.shape[0]
    row = lax.broadcasted_iota(jnp.int32, (C, C), 0)
    col = lax.broadcasted_iota(jnp.int32, (C, C), 1)
    lower = row >= col
    la = _log_sigmoid(_dot(_bf(gk), _bf(w2p)) + bgk) * GLA_TAU_INV
    outs, states = [], []
    for h in range(GLA_H):
        ks = slice(h * GLA_DKH, (h + 1) * GLA_DKH)
        vs = slice(h * GLA_DVH, (h + 1) * GLA_DVH)
        qh = q[:, ks] * (GLA_DKH ** -0.5)
        kh, vh, gh, lah, s0 = k[:, ks], v[:, vs], g[:, vs], la[:, ks], s0t[h]
        cum = _cumsum_rows(lah)
        e_pos = jnp.exp(cum)
        e_neg = jnp.exp(-cum)
        q_f = qh * e_pos
        a_f = _dot_nt(_bf(q_f), _bf(kh * e_neg))
        a_b = _dot_nt(_bf(qh * e_neg), _bf(kh * e_pos))
        att = jnp.where(lower, a_f, a_b)
        o = _dot(_bf(att), _bf(vh)) + _dot_nt(_bf(q_f), _bf(s0))
        tot = jnp.sum(lah, axis=0, keepdims=True)
        k_end = kh * jnp.exp(tot - cum)
        states.append(s0 * jnp.exp(tot) + _dot_tn(_bf(vh), _bf(k_end)))
        on = o * lax.rsqrt(jnp.mean(o * o, axis=-1, keepdims=True) + RMS_EPS) * gn[:, vs]
        outs.append(on * _silu(gh))
    return jnp.concatenate(outs, axis=1), tuple(states)


def _gla_split(p):
    return (p[:, 0:GLA_DK], p[:, GLA_DK:2 * GLA_DK], p[:, 2 * GLA_DK:2 * GLA_DK + GLA_DV],
            p[:, 2 * GLA_DK + GLA_DV:2 * GLA_DK + 2 * GLA_DV], p[:, 2 * GLA_DK + 2 * GLA_DV:GLA_INP])


def gla_fwd(proj, w2p, bgk, gn, *, comm=None, name):
    S = proj.shape[0]
    n_c = S // CHUNK
    n_s = n_c // GLA_SUB
    rows = GLA_SUB * CHUNK

    def body(p_ref, w_ref, b_ref, gn_ref, o_ref, st_ref, st):
        @pl.when(pl.program_id(0) == 0)
        def _():
            st[...] = jnp.zeros_like(st)

        s = tuple(st[h] for h in range(GLA_H))
        for u in range(GLA_SUB):
            sub = slice(u * CHUNK, (u + 1) * CHUNK)
            for h in range(GLA_H):
                st_ref[u, h] = s[h]
            og, s = _gla_chunk(*_gla_split(p_ref[sub, :]), s, w_ref[...], b_ref[...], gn_ref[...])
            o_ref[sub, :] = _bf(og)
        for h in range(GLA_H):
            st[h] = s[h]

    full = lambda shape: pl.BlockSpec(shape, lambda i: (0,) * len(shape))
    return _call_hosting(
        body, comm, first=lambda: pl.program_id(0) == 0, last=lambda: pl.program_id(0) == n_s - 1,
        name=name, grid=(n_s,),
        in_specs=[pl.BlockSpec((rows, GLA_INP), lambda i: (i, 0)), full((128, GLA_DK)), full((1, GLA_DK)),
                  full((1, GLA_DV))],
        out_specs=[pl.BlockSpec((rows, GLA_DV), lambda i: (i, 0)),
                   pl.BlockSpec((GLA_SUB, GLA_H, GLA_DVH, GLA_DKH), lambda i: (i, 0, 0, 0))],
        out_shape=[jax.ShapeDtypeStruct((S, GLA_DV), BF16),
                   jax.ShapeDtypeStruct((n_c, GLA_H, GLA_DVH, GLA_DKH), F32)],
        scratch_shapes=[pltpu.VMEM((GLA_H, GLA_DVH, GLA_DKH), F32)], compiler_params=_params(1),
        args=(proj, w2p, bgk, gn))


def gla_bwd(proj, dog, states, w2p, bgk, gn, *, comm=None, name):
    S = proj.shape[0]
    n_c = S // CHUNK
    n_s = n_c // GLA_SUB
    rows = GLA_SUB * CHUNK

    def body(p_ref, dog_ref, st_ref, w_ref, b_ref, gn_ref, dp_ref, dw_ref, db_ref, dgn_ref, ds_ref):
        @pl.when(pl.program_id(0) == 0)
        def _():
            ds_ref[...] = jnp.zeros_like(ds_ref)
            dw_ref[...] = jnp.zeros_like(dw_ref)
            db_ref[...] = jnp.zeros_like(db_ref)
            dgn_ref[...] = jnp.zeros_like(dgn_ref)

        ds = tuple(ds_ref[h] for h in range(GLA_H))
        for u in reversed(range(GLA_SUB)):
            sub = slice(u * CHUNK, (u + 1) * CHUNK)
            q, k, v, g, gk = _gla_split(p_ref[sub, :])
            s0 = tuple(st_ref[u, h] for h in range(GLA_H))
            _, vjp = jax.vjp(_gla_chunk, q, k, v, g, gk, s0, w_ref[...], b_ref[...], gn_ref[...])
            dq, dk, dv, dg, dgk, ds, dw, db, dgn = vjp((dog_ref[sub, :], ds))
            dp_ref[sub, 0:GLA_DK] = _bf(dq)
            dp_ref[sub, GLA_DK:2 * GLA_DK] = _bf(dk)
            dp_ref[sub, 2 * GLA_DK:2 * GLA_DK + GLA_DV] = _bf(dv)
            dp_ref[sub, 2 * GLA_DK + GLA_DV:2 * GLA_DK + 2 * GLA_DV] = _bf(dg)
            dp_ref[sub, 2 * GLA_DK + 2 * GLA_DV:GLA_INP] = _bf(dgk)
            dw_ref[...] += dw
            db_ref[...] += db
            dgn_ref[...] += dgn
        for h in range(GLA_H):
            ds_ref[h] = ds[h]

    full = lambda shape: pl.BlockSpec(shape, lambda i: (0,) * len(shape))
    rev = lambda i: (n_s - 1 - i, 0)
    return _call_hosting(
        body, comm, first=lambda: pl.program_id(0) == 0, last=lambda: pl.program_id(0) == n_s - 1,
        name=name, grid=(n_s,),
        in_specs=[pl.BlockSpec((rows, GLA_INP), rev), pl.BlockSpec((rows, GLA_DV), rev),
                  pl.BlockSpec((GLA_SUB, GLA_H, GLA_DVH, GLA_DKH), lambda i: (n_s - 1 - i, 0, 0, 0)),
                  full((128, GLA_DK)), full((1, GLA_DK)), full((1, GLA_DV))],
        out_specs=[pl.BlockSpec((rows, GLA_INP), rev), full((128, GLA_DK)), full((1, GLA_DK)), full((1, GLA_DV))],
        out_shape=[jax.ShapeDtypeStruct((S, GLA_INP), BF16), jax.ShapeDtypeStruct((128, GLA_DK), F32),
                   jax.ShapeDtypeStruct((1, GLA_DK), F32), jax.ShapeDtypeStruct((1, GLA_DV), F32)],
        scratch_shapes=[pltpu.VMEM((GLA_H, GLA_DVH, GLA_DKH), F32)], compiler_params=_params(1),
        args=(proj, dog, states, w2p, bgk, gn))


def _rel_index():
    t = np.arange(REL_TILE)[:, None]
    j = np.arange(REL_TILE)[None, :]
    tiles = []
    for m in REL_TILES:
        chunks = (REL_TILE // CHUNK) * m + j // CHUNK - t // CHUNK
        band = (chunks >= 0) & (chunks <= LEFT // CHUNK)
        dist = LEFT - REL_TILE * m + t - j
        tiles.append(np.where(band, np.minimum(dist, MAX_REL) + MAX_REL, N_REL))
    return jnp.asarray(np.stack(tiles).reshape(1, -1).astype(np.int32))


REL_BLOCK = 2048


def _one_hot(idx_row):
    return (lax.broadcasted_iota(jnp.int32, (REL_PAD, idx_row.shape[1]), 0) == idx_row).astype(F32)


def rel_bias_tiles(rel_pad, idx, *, name):
    E = idx.shape[1]

    def body(r_ref, i_ref, o_ref):
        o_ref[...] = jnp.dot(r_ref[...], _one_hot(i_ref[...]), precision=HIGHEST, preferred_element_type=F32)

    return pl.pallas_call(
        body, name=name, grid=(E // REL_BLOCK,),
        in_specs=[pl.BlockSpec((ATT_H, REL_PAD), lambda i: (0, 0)), pl.BlockSpec((1, REL_BLOCK), lambda i: (0, i))],
        out_specs=pl.BlockSpec((ATT_H, REL_BLOCK), lambda i: (0, i)),
        out_shape=jax.ShapeDtypeStruct((ATT_H, E), F32), compiler_params=_params(1),
    )(rel_pad, idx)


def rel_bias_grad(dtiles_flat, dclip, idx, *, name):
    E = idx.shape[1]
    n_steps = E // REL_BLOCK

    def body(d_ref, c_ref, i_ref, o_ref):
        @pl.when(pl.program_id(0) == 0)
        def _():
            o_ref[...] = jnp.zeros_like(o_ref)

        o_ref[...] += lax.dot_general(d_ref[...], _one_hot(i_ref[...]), (((1,), (1,)), ((), ())),
                                      precision=HIGHEST, preferred_element_type=F32)

        @pl.when(pl.program_id(0) == n_steps - 1)
        def _():
            at_clip = lax.broadcasted_iota(jnp.int32, (1, REL_PAD), 1) == 2 * MAX_REL
            o_ref[...] += jnp.where(at_clip, jnp.sum(c_ref[...], axis=1, keepdims=True), 0.0)

    return pl.pallas_call(
        body, name=name, grid=(n_steps,),
        in_specs=[pl.BlockSpec((ATT_H, REL_BLOCK), lambda i: (0, i)), pl.BlockSpec((ATT_H, 128), lambda i: (0, 0)),
                  pl.BlockSpec((1, REL_BLOCK), lambda i: (0, i))],
        out_specs=pl.BlockSpec((ATT_H, REL_PAD), lambda i: (0, 0)),
        out_shape=jax.ShapeDtypeStruct((ATT_H, REL_PAD), F32), compiler_params=_params(1),
    )(dtiles_flat, dclip, idx)


def _attn_bias(tiles, clip):
    const = jnp.broadcast_to(clip, (REL_TILE, REL_TILE))
    zero = jnp.zeros((REL_TILE, REL_TILE), F32)
    rows = []
    for qt in range(ATT_QB // REL_TILE):
        blocks = []
        for kt in range(ATT_KB // REL_TILE):
            m = kt - qt
            if m in REL_TILES:
                blocks.append(tiles[REL_TILES.index(m)])
            elif 0 <= m < REL_TILES[0]:
                blocks.append(const)
            else:
                blocks.append(zero)
        rows.append(jnp.concatenate(blocks, axis=1))
    return jnp.concatenate(rows, axis=0)


def _attn_bias_grad(ds, dt_ref, dc_ref, a):
    tile = lambda qt, kt: ds[qt * REL_TILE:(qt + 1) * REL_TILE, kt * REL_TILE:(kt + 1) * REL_TILE]
    const = None
    sums = [None] * len(REL_TILES)
    for qt in range(ATT_QB // REL_TILE):
        for kt in range(ATT_KB // REL_TILE):
            m = kt - qt
            if m in REL_TILES:
                n = REL_TILES.index(m)
                sums[n] = tile(qt, kt) if sums[n] is None else sums[n] + tile(qt, kt)
            elif 0 <= m < REL_TILES[0]:
                const = tile(qt, kt) if const is None else const + tile(qt, kt)
    for n, v in enumerate(sums):
        dt_ref[a, n] += v
    dc_ref[a] += jnp.sum(const, axis=0, keepdims=True)


def _attn_head_lanes():
    lane = lax.broadcasted_iota(jnp.int32, (1, 2 * ATT_HD), 1)
    return [(lane >= a * ATT_HD) & (lane < (a + 1) * ATT_HD) for a in range(2)]


def _attn_band_bias(tiles, clip):
    j = lax.broadcasted_iota(jnp.int32, (ATT_QB, ATT_KB), 1)
    t = lax.broadcasted_iota(jnp.int32, (ATT_QB, ATT_KB), 0)
    shift = CHUNK.bit_length() - 1
    chunks = lax.shift_right_logical(j, shift) - lax.shift_right_logical(t, shift)
    band = (chunks >= 0) & (chunks <= LEFT // CHUNK)
    return jnp.where(band, _attn_bias(tiles, clip), NEG_INF)


def _attn_exp(qa, kb, bias, key_bias):
    s = _dot_nt(qa, kb) + bias + key_bias
    e = jnp.exp(s - jnp.max(s, axis=-1, keepdims=True))
    return e, jnp.sum(e, axis=-1, keepdims=True)


def _attn_specs():
    n_hp = ATT_H // 2
    q_spec = pl.BlockSpec((ATT_QB, 128), lambda hp, g: (g, hp))

    def win(col0, back):
        return pl.BlockSpec((ATT_QB, 128), lambda hp, g: (jnp.maximum(g - back, 0), col0 + hp))

    kv_specs = [win(n_hp, 2), win(n_hp, 1), win(n_hp, 0), win(2 * n_hp, 2), win(2 * n_hp, 1), win(2 * n_hp, 0)]
    tiles_spec = pl.BlockSpec((2, len(REL_TILES), REL_TILE, REL_TILE), lambda hp, g: (hp, 0, 0, 0))
    clip_spec = pl.BlockSpec((2, 1, 128), lambda hp, g: (hp, 0, 0))
    return q_spec, kv_specs, tiles_spec, clip_spec


def _attn_window(refs, g):
    kb = jnp.concatenate([_bf(r[...]) for r in refs[0:3]], axis=0)
    vb = jnp.concatenate([_bf(r[...]) for r in refs[3:6]], axis=0)
    j = lax.broadcasted_iota(jnp.int32, (1, ATT_KB), 1)
    return kb, vb, jnp.where(j + (g - 2) * ATT_QB >= 0, 0.0, NEG_INF)


def attn_fwd(qkv, tiles, clip, *, comm=None, name):
    S = qkv.shape[0]
    q_spec, kv_specs, tiles_spec, clip_spec = _attn_specs()

    def body(q_ref, *rest):
        kv_refs, t_ref, c_ref, o_ref, bias = rest[:6], rest[6], rest[7], rest[8], rest[9]
        g = pl.program_id(1)

        @pl.when(g == 0)
        def _():
            for a in range(2):
                bias[a * ATT_QB:(a + 1) * ATT_QB, :] = _attn_band_bias(t_ref[a], c_ref[a])

        kb, vb, key_bias = _attn_window(kv_refs, g)
        q = q_ref[...].astype(F32)
        out = jnp.zeros((ATT_QB, 2 * ATT_HD), F32)
        for a, lanes in enumerate(_attn_head_lanes()):
            mf = lanes.astype(F32)
            e, l = _attn_exp(_bf(q * (mf * ATT_HD ** -0.5)), kb, bias[a * ATT_QB:(a + 1) * ATT_QB, :], key_bias)
            out = out + _dot(_bf(e), vb) * (mf * (1.0 / l))
        o_ref[...] = _bf(out)

    n_hp, n_g = ATT_H // 2, S // ATT_QB
    return _call_hosting(
        body, comm, first=lambda: (pl.program_id(0) == 0) & (pl.program_id(1) == 0),
        last=lambda: (pl.program_id(0) == n_hp - 1) & (pl.program_id(1) == n_g - 1),
        mid=lambda: (pl.program_id(0) == n_hp // 2 + 1) & (pl.program_id(1) == 0),
        name=name, grid=(n_hp, n_g), in_specs=[q_spec] + kv_specs + [tiles_spec, clip_spec],
        out_specs=[q_spec], out_shape=[jax.ShapeDtypeStruct((S, D), BF16)],
        scratch_shapes=[pltpu.VMEM((2 * ATT_QB, ATT_KB), F32)], compiler_params=_params(2),
        args=(*([qkv] * 7), tiles, clip))


def attn_bwd(qkv, do, tiles, clip, *, comm=None, name):
    S = qkv.shape[0]
    q_spec, kv_specs, tiles_spec, clip_spec = _attn_specs()
    col_spec = pl.BlockSpec((S, 128), lambda hp, g: (0, hp))
    sum_spec = pl.BlockSpec((1, 128), lambda hp, g: (0, hp))
    n_g = S // ATT_QB

    def body(q_ref, *rest):
        kv_refs, t_ref, c_ref, do_ref = rest[:6], rest[6], rest[7], rest[8]
        dq_ref, dk_ref, dv_ref, dt_ref, dc_ref, sq_ref, sk_ref, sv_ref, bias = rest[9:]
        g = pl.program_id(1)

        @pl.when(g == 0)
        def _():
            for a in range(2):
                bias[a * ATT_QB:(a + 1) * ATT_QB, :] = _attn_band_bias(t_ref[a], c_ref[a])
            dk_ref[...] = jnp.zeros_like(dk_ref)
            dv_ref[...] = jnp.zeros_like(dv_ref)
            dt_ref[...] = jnp.zeros_like(dt_ref)
            dc_ref[...] = jnp.zeros_like(dc_ref)
            sq_ref[...] = jnp.zeros_like(sq_ref)

        kb, vb, key_bias = _attn_window(kv_refs, g)
        q = q_ref[...].astype(F32)
        do = do_ref[...]
        lanes = _attn_head_lanes()
        mf = [m.astype(F32) * ATT_HD ** -0.5 for m in lanes]
        qs = _bf(jnp.concatenate([q * m for m in mf], axis=0))
        dos = jnp.concatenate([jnp.where(m, do, jnp.zeros_like(do)) for m in lanes], axis=0)
        e, l = _attn_exp(qs, kb, bias[...], key_bias)
        p = e * (1.0 / l)
        dp = _dot_nt(dos, vb)
        ds = p * (dp - jnp.sum(p * dp, axis=-1, keepdims=True))
        ds_b = _bf(ds)
        dq2 = _dot(ds_b, kb)
        dq = dq2[:ATT_QB] * mf[0] + dq2[ATT_QB:] * mf[1]
        dkw = _dot_tn(ds_b, qs)
        dvw = _dot_tn(_bf(p), dos)
        for a in range(2):
            _attn_bias_grad(ds[a * ATT_QB:(a + 1) * ATT_QB], dt_ref, dc_ref, a)
        dq_ref[...] = _bf(dq)
        sq_ref[...] += jnp.sum(dq, axis=0, keepdims=True)
        for blk in range(3):
            src = g - 2 + blk

            @pl.when(src >= 0)
            def _(blk=blk, src=src):
                rows = pl.ds(pl.multiple_of(src * ATT_QB, ATT_QB), ATT_QB)
                dk_ref[rows, :] += dkw[blk * ATT_QB:(blk + 1) * ATT_QB]
                dv_ref[rows, :] += dvw[blk * ATT_QB:(blk + 1) * ATT_QB]

        @pl.when(g == n_g - 1)
        def _():
            sk_ref[...] = jnp.sum(dk_ref[...], axis=0, keepdims=True)
            sv_ref[...] = jnp.sum(dv_ref[...], axis=0, keepdims=True)

    n_hp, n_g = ATT_H // 2, S // ATT_QB
    return _call_hosting(
        body, comm, first=lambda: (pl.program_id(0) == 0) & (pl.program_id(1) == 0),
        last=lambda: (pl.program_id(0) == n_hp - 1) & (pl.program_id(1) == n_g - 1),
        name=name, grid=(n_hp, n_g),
        in_specs=[q_spec] + kv_specs + [tiles_spec, clip_spec, q_spec],
        out_specs=[q_spec, col_spec, col_spec, tiles_spec, clip_spec] + [sum_spec] * 3,
        out_shape=[jax.ShapeDtypeStruct((S, D), BF16)] + [jax.ShapeDtypeStruct((S, D), F32)] * 2
        + [jax.ShapeDtypeStruct((ATT_H, len(REL_TILES), REL_TILE, REL_TILE), F32),
           jax.ShapeDtypeStruct((ATT_H, 1, 128), F32)] + [jax.ShapeDtypeStruct((1, D), F32)] * 3,
        scratch_shapes=[pltpu.VMEM((2 * ATT_QB, ATT_KB), F32)], compiler_params=_params(2),
        args=(*([qkv] * 7), tiles, clip, do))


def mods_partial(c_all, w_ada, *, name):
    n_l, _, n_c = w_ada.shape

    def body(c_ref, w_ref, o_ref):
        o_ref[...] = _dot(_bf(_silu(c_ref[...])), _bf(w_ref[...]))

    return pl.pallas_call(
        body, name=name, grid=(n_l,),
        in_specs=[pl.BlockSpec((N_DEV, D), lambda l: (0, 0)), pl.BlockSpec((None, D, n_c), lambda l: (l, 0, 0))],
        out_specs=pl.BlockSpec((None, N_DEV, n_c), lambda l: (l, 0, 0)),
        out_shape=jax.ShapeDtypeStruct((n_l, N_DEV, n_c), F32), compiler_params=_params(1),
    )(c_all, w_ada)


def w_ada_grad(c_all, dm, *, name):
    n_l, _, n_c = dm.shape

    def body(c_ref, d_ref, o_ref):
        o_ref[...] = lax.dot_general(_silu(c_ref[...]), d_ref[...], (((0,), (0,)), ((), ())),
                                     precision=HIGHEST, preferred_element_type=F32)

    return pl.pallas_call(
        body, name=name, grid=(n_l,),
        in_specs=[pl.BlockSpec((N_DEV, D), lambda l: (0, 0)), pl.BlockSpec((None, N_DEV, n_c), lambda l: (l, 0, 0))],
        out_specs=pl.BlockSpec((None, D, n_c), lambda l: (l, 0, 0)),
        out_shape=jax.ShapeDtypeStruct((n_l, D, n_c), F32), compiler_params=_params(1),
    )(c_all, dm)


def adamw(w, m, v, gparts, *, block_rows, name):
    R, C = w.shape
    n = gparts.shape[0]

    def body(w_ref, m_ref, v_ref, g_ref, go_ref, d_ref, mo_ref, vo_ref):
        g = g_ref[0].astype(F32)
        for k in range(1, n):
            g = g + g_ref[k].astype(F32)
        m_new = ADAM_B1 * m_ref[...] + (1.0 - ADAM_B1) * g
        v_new = ADAM_B2 * v_ref[...] + (1.0 - ADAM_B2) * (g * g)
        m_hat = m_new / (1.0 - ADAM_B1 ** ADAM_STEP)
        v_hat = v_new / (1.0 - ADAM_B2 ** ADAM_STEP)
        go_ref[...] = g
        d_ref[...] = -ADAM_LR * (m_hat / (jnp.sqrt(v_hat) + ADAM_EPS) + ADAM_WD * w_ref[...])
        mo_ref[...] = m_new
        vo_ref[...] = v_new

    blk = pl.BlockSpec((block_rows, C), lambda i: (i, 0))
    return pl.pallas_call(
        body, name=name, grid=(R // block_rows,),
        in_specs=[blk, blk, blk, pl.BlockSpec((n, block_rows, C), lambda i: (0, i, 0))],
        out_specs=[blk] * 4, out_shape=[jax.ShapeDtypeStruct((R, C), F32)] * 4, compiler_params=_params(1),
    )(w, m, v, gparts)


def adamw_nd(w, m, v, gparts, *, name):
    shape = w.shape
    two = (int(np.prod(shape[:-1])), shape[-1])
    rows = two[0]
    block_rows = rows
    for cand in (512, 256):
        if rows > cand and rows % cand == 0:
            block_rows = cand
            break
    outs = adamw(w.reshape(two), m.reshape(two), v.reshape(two), gparts.reshape((gparts.shape[0],) + two),
                 block_rows=block_rows, name=name)
    return [o.reshape(shape) for o in outs]


def sum_parts(parts, *, name):
    n, R, C = parts.shape

    def body(p_ref, o_ref):
        acc = p_ref[0]
        for k in range(1, n):
            acc = acc + p_ref[k]
        o_ref[...] = acc

    return pl.pallas_call(
        body, name=name, in_specs=[pl.BlockSpec((n, R, C), lambda: (0, 0, 0))],
        out_specs=pl.BlockSpec((R, C), lambda: (0, 0)), out_shape=jax.ShapeDtypeStruct((R, C), F32),
        compiler_params=pltpu.CompilerParams(vmem_limit_bytes=VMEM_LIMIT),
    )(parts)


def _my_place():
    return lax.axis_index("x"), lax.axis_index("y"), lax.axis_index("c")


def _full_shape(kind, shard):
    n_l, rows, cols = shard
    return {"col": (n_l, rows, N_DEV * cols), "row": (n_l, N_DEV * rows, cols), "stk": (N_DEV, n_l, rows, cols)}[kind]


def _slab(ref, kind, dev, shard):
    _, rows, cols = shard
    if kind == "col":
        return ref.at[:, :, pl.ds(pl.multiple_of(dev * cols, 128), cols)]
    if kind == "row":
        return ref.at[:, pl.ds(pl.multiple_of(dev * rows, 8), rows), :]
    return ref.at[dev]


def all_gather(x_shard, *, name):
    m_per, n = x_shard.shape

    def body(x_ref, out_ref, send_sems, recv_sems, local_sem):
        x, y, c = _my_place()
        me, sibling = (x, y, c), (x, y, 1 - c)
        chips = [(1 - x, y), (x, 1 - y), (1 - x, 1 - y)]

        def rows(px, py, pc):
            return out_ref.at[pl.ds((4 * px + 2 * py + pc) * m_per, m_per), :]

        def copy(k, block, to, src=None):
            return pltpu.make_async_remote_copy(
                src_ref=rows(*block) if src is None else src, dst_ref=rows(*block),
                send_sem=send_sems.at[k], recv_sem=recv_sems.at[k], device_id=to, device_id_type=MESH)

        mine = pltpu.make_async_copy(x_ref, rows(*me), local_sem)
        mine.start()
        first = [copy(0, me, sibling, src=x_ref)]
        first += [copy(1 + j, me, (*chip, c), src=x_ref) for j, chip in enumerate(chips)]
        for cp in first:
            cp.start()
        passed = [copy(4 + j, (*chip, c), sibling) for j, chip in enumerate(chips)]
        for j, chip in enumerate(chips):
            copy(1 + j, (*chip, c), me).wait_recv()
            passed[j].start()
        copy(0, sibling, me).wait_recv()
        for j, chip in enumerate(chips):
            copy(4 + j, (*chip, 1 - c), me).wait_recv()
        for cp in first + passed:
            cp.wait_send()
        mine.wait()

    return pl.pallas_call(
        body, name=name, out_shape=jax.ShapeDtypeStruct((N_DEV * m_per, n), x_shard.dtype),
        in_specs=[pl.BlockSpec(memory_space=pltpu.VMEM)], out_specs=pl.BlockSpec(memory_space=pltpu.VMEM),
        scratch_shapes=[pltpu.SemaphoreType.DMA((7,)), pltpu.SemaphoreType.DMA((7,)), pltpu.SemaphoreType.DMA],
        compiler_params=pltpu.CompilerParams(vmem_limit_bytes=VMEM_LIMIT),
    )(x_shard)


def gather_plan(shards, kinds, layers):
    n_t = len(shards)
    shapes = [(1,) + tuple(s.shape[1:]) for s in shards]

    def copies(x_refs, out_refs, sems):
        send_sems, recv_sems, local_sems = sems
        x, y, c = _my_place()
        me, sibling = (x, y, c), (x, y, 1 - c)
        chips = [(1 - x, y), (x, 1 - y), (1 - x, 1 - y)]
        own = [x_refs[t].at[pl.ds(layers[t], 1)] for t in range(n_t)]

        def slab(t, px, py, pc):
            return _slab(out_refs[t], kinds[t], 4 * px + 2 * py + pc, shapes[t])

        def copy(t, k, block, to, src=None):
            return pltpu.make_async_remote_copy(
                src_ref=slab(t, *block) if src is None else src, dst_ref=slab(t, *block),
                send_sem=send_sems.at[7 * t + k], recv_sem=recv_sems.at[7 * t + k], device_id=to,
                device_id_type=MESH)

        mine = [pltpu.make_async_copy(own[t], slab(t, *me), local_sems.at[t]) for t in range(n_t)]
        sends = []
        for t in range(n_t):
            sends.append(copy(t, 0, me, sibling, src=own[t]))
            sends += [copy(t, 1 + j, me, (*chip, c), src=own[t]) for j, chip in enumerate(chips)]
        return mine, sends, copy, me, sibling, chips, c

    def first(x_refs, out_refs, sems):
        mine, sends = copies(x_refs, out_refs, sems)[:2]
        for cp in mine + sends:
            cp.start()

    def forward(x_refs, out_refs, sems):
        _, _, copy, me, sibling, chips, c = copies(x_refs, out_refs, sems)
        for j, chip in enumerate(chips):
            for t in range(n_t):
                copy(t, 1 + j, (*chip, c), me).wait_recv()
                copy(t, 4 + j, (*chip, c), sibling).start()

    def finish(x_refs, out_refs, sems):
        mine, sends, copy, me, sibling, chips, c = copies(x_refs, out_refs, sems)
        passed = [copy(t, 4 + j, (*chip, c), sibling) for j, chip in enumerate(chips) for t in range(n_t)]
        for t in range(n_t):
            copy(t, 0, sibling, me).wait_recv()
        for j, chip in enumerate(chips):
            for t in range(n_t):
                copy(t, 4 + j, (*chip, 1 - c), me).wait_recv()
        for cp in sends + passed:
            cp.wait_send()
        for cp in mine:
            cp.wait()

    def last(x_refs, out_refs, sems):
        forward(x_refs, out_refs, sems)
        finish(x_refs, out_refs, sems)

    return Hosted(
        list(shards), [jax.ShapeDtypeStruct(_full_shape(k, shp), s.dtype) for k, shp, s in zip(kinds, shapes, shards)],
        [pltpu.SemaphoreType.DMA((7 * n_t,)), pltpu.SemaphoreType.DMA((7 * n_t,)), pltpu.SemaphoreType.DMA((n_t,))],
        first, last, mid=forward, last_after_mid=finish)


def scatter_plan(grads, kinds, shapes):
    n_t = len(grads)

    def copies(g_refs, out_refs, sems):
        send_sems, recv_sems, local_sems = sems
        x, y, c = _my_place()
        me = 4 * x + 2 * y + c
        local = [pltpu.make_async_copy(_slab(g_refs[t], kinds[t], me, shapes[t]), out_refs[t].at[me],
                                       local_sems.at[t]) for t in range(n_t)]
        remote = []
        for t in range(n_t):
            for r in range(1, N_DEV):
                px = 1 - x if r & 4 else x
                py = 1 - y if r & 2 else y
                pc = 1 - c if r & 1 else c
                remote.append(pltpu.make_async_remote_copy(
                    src_ref=_slab(g_refs[t], kinds[t], 4 * px + 2 * py + pc, shapes[t]), dst_ref=out_refs[t].at[me],
                    send_sem=send_sems.at[7 * t + r - 1], recv_sem=recv_sems.at[7 * t + r - 1],
                    device_id=(px, py, pc), device_id_type=MESH))
        return local, remote

    def first(g_refs, out_refs, sems):
        local, remote = copies(g_refs, out_refs, sems)
        for cp in local + remote:
            cp.start()

    def last(g_refs, out_refs, sems):
        local, remote = copies(g_refs, out_refs, sems)
        for cp in remote + local:
            cp.wait()

    return Hosted(
        list(grads), [jax.ShapeDtypeStruct((N_DEV,) + tuple(s), BF16) for s in shapes],
        [pltpu.SemaphoreType.DMA((7 * n_t,)), pltpu.SemaphoreType.DMA((7 * n_t,)), pltpu.SemaphoreType.DMA((n_t,))],
        first, last)


def adamw_layer(w, m, v, parts, layer, bufs, *, name):
    n_l, rows, cols = w.shape
    n = parts.shape[0]
    tr = min(rows, 256)

    def body(w_ref, m_ref, v_ref, g_ref, *rest):
        go_ref, d_ref, mo_ref, vo_ref = rest[-4:]
        g = g_ref[0].astype(F32)
        for k in range(1, n):
            g = g + g_ref[k].astype(F32)
        m_new = ADAM_B1 * m_ref[...] + (1.0 - ADAM_B1) * g
        v_new = ADAM_B2 * v_ref[...] + (1.0 - ADAM_B2) * (g * g)
        m_hat = m_new / (1.0 - ADAM_B1 ** ADAM_STEP)
        v_hat = v_new / (1.0 - ADAM_B2 ** ADAM_STEP)
        go_ref[...] = g
        d_ref[...] = -ADAM_LR * (m_hat / (jnp.sqrt(v_hat) + ADAM_EPS) + ADAM_WD * w_ref[...])
        mo_ref[...] = m_new
        vo_ref[...] = v_new

    blk = pl.BlockSpec((None, tr, cols), lambda i: (layer, i, 0))
    in_specs = [blk, blk, blk, pl.BlockSpec((n, None, tr, cols), lambda i: (0, 0, i, 0))]
    args = [w, m, v, parts]
    aliases = {}
    if bufs is not None:
        in_specs += [pl.BlockSpec(memory_space=pl.ANY)] * 4
        args += list(bufs)
        aliases = {4 + k: k for k in range(4)}
    return pl.pallas_call(
        body, name=name, grid=(rows // tr,), in_specs=in_specs, out_specs=[blk] * 4,
        out_shape=[jax.ShapeDtypeStruct((n_l, rows, cols), F32)] * 4, input_output_aliases=aliases,
        compiler_params=_params(1),
    )(*args)


BIG =("gla_w_in", "gla_w_out", "att_w_in", "att_w_out", "ff_w1", "ff_w2")
KIND = {"gla_w_in": "stk", "gla_w_out": "row", "att_w_in": "col", "att_w_out": "row", "ff_w1": "col", "ff_w2": "row"}


def _pack_small(arrs):
    parts = []
    for a in arrs:
        f = a.reshape(-1)
        parts.append(jnp.pad(f, (0, -f.shape[0] % 128)))
    flat = jnp.concatenate(parts)
    flat = jnp.pad(flat, (0, -flat.shape[0] % 1024))
    return flat.reshape(-1, 128)


def _unpack_small(packed, shapes):
    flat = packed.reshape(packed.shape[:-2] + (-1,))
    out, off = [], 0
    for shp in shapes:
        n = int(np.prod(shp))
        out.append(flat[..., off:off + n].reshape(packed.shape[:-2] + tuple(shp)))
        off += n + (-n % 128)
    return out


def _vec(a):
    return a.reshape(1, -1)


def _layer_weights(i):
    mixer = "gla" if i % 2 == 0 else "att"
    return [(f"{mixer}_w_in", i // 2), (f"{mixer}_w_out", i // 2), ("ff_w1", i), ("ff_w2", i)]


def _trunk(x, target, mods, sm, w, m, v):
    shard_bf = {n: w[n].astype(BF16) for n in BIG}

    def gather_of(names):
        return gather_plan([shard_bf[n] for n, _ in names], [KIND[n] for n, _ in names], [l for _, l in names])

    def gather_under_core(i):
        names = _layer_weights(i)[2:] + (_layer_weights(i + 1)[:2] if i + 1 < DEPTH else [])
        return names, gather_of(names)

    wts = {}

    def keep_gathered(names, arrays):
        for (n, l), a in zip(names, arrays):
            if KIND[n] == "stk":
                a = a.transpose(1, 2, 0, 3).reshape(1, D, GLA_IN)
                a = jnp.pad(a, ((0, 0), (0, 0), (0, GLA_INP - GLA_IN)))
            wts[n, l] = a

    first_names, rest_names = _layer_weights(0)[:1], _layer_weights(0)[1:2]
    keep_gathered(first_names, run_hosted(gather_of(first_names), name="gather_first"))

    rel_idx = _rel_index()
    saved = []
    for i in range(DEPTH):
        sh1, sc1, g1, sh2, sc2, g2 = [mods[i, k:k + 1] for k in range(6)]
        rec = {"x0": x}
        j = i // 2
        nxt_names, nxt_plan = gather_under_core(i)
        if i % 2 == 0:
            w2p = jnp.pad(sm["gla_w_gk2"][j], ((0, 128 - GLA_RANK), (0, 0)))
            bgk, gn = _vec(sm["gla_b_gk"][j]), _vec(sm["gla_g_norm"][j])
            if i == 0:
                proj, got0 = mm_nn(x, wts["gla_w_in", j], 0, pro="mod", p1=sc1, p2=sh1, tm=512, tn=GLA_INP,
                                   comm=gather_of(rest_names), name=f"gla_proj_{i}")
                keep_gathered(rest_names, got0)
            else:
                proj = mm_nn(x, wts["gla_w_in", j], 0, pro="mod", p1=sc1, p2=sh1, tm=512, tn=GLA_INP,
                             name=f"gla_proj_{i}")
            (og, states), got = gla_fwd(proj, w2p, bgk, gn, comm=nxt_plan, name=f"gla_core_{i}")
            y = mm_nn(og, wts["gla_w_out", j], 0, tm=1024, tn=1024, name=f"gla_out_{i}")
            rec.update(kind="gla", j=j, w2p=w2p, bgk=bgk, gn=gn, proj=proj, og=og, states=states)
        else:
            rel = sm["att_rel_bias"][j]
            rel_pad = jnp.pad(rel, ((0, 0), (0, REL_PAD - N_REL)), constant_values=NEG_INF)
            tiles = rel_bias_tiles(rel_pad, rel_idx, name=f"att_bias_{i}")
            tiles = tiles.reshape(ATT_H, len(REL_TILES), REL_TILE, REL_TILE)
            clip = jnp.broadcast_to(rel[:, 2 * MAX_REL][:, None, None], (ATT_H, 1, 128))
            qkv = mm_nn(x, wts["att_w_in", j], 0, pro="mod", p1=sc1, p2=sh1, bias=_vec(sm["att_b_in"][j]),
                        out_dtype=BF16, tm=512, tn=3 * D, name=f"att_proj_{i}")
            (o,), got = attn_fwd(qkv, tiles, clip, comm=nxt_plan, name=f"att_core_{i}")
            y = mm_nn(o, wts["att_w_out", j], 0, tm=1024, tn=1024, name=f"att_out_{i}")
            rec.update(kind="att", j=j, tiles=tiles, clip=clip, qkv=qkv, o=o)
        keep_gathered(nxt_names, got)
        x1 = ln_fwd(x, y, g1, _vec(sm["ln_g"][i, 0]), _vec(sm["ln_b"][i, 0]), name=f"ln_mix_{i}")
        h = mm_nn(x1, wts["ff_w1", i], 0, pro="mod", p1=sc2, p2=sh2, out_dtype=BF16, tm=512, tn=D_FF,
                  name=f"ff_up_{i}")
        y2 = mm_nn(h, wts["ff_w2", i], 0, pro="relu2", tm=512, tn=D, name=f"ff_down_{i}")
        rec.update(y=y, x1=x1, h=h, y2=y2)
        saved.append(rec)
        if i + 1 < DEPTH:
            x = ln_fwd(x1, y2, g2, _vec(sm["ln_g"][i, 1]), _vec(sm["ln_b"][i, 1]), name=f"ln_ff_{i}")

    gw = {}

    def wgrad(weight, layer, a, d, *, tn, tk=1024, tm=512, col_block0=0, **kw):
        gw[weight, layer] = mm_tn(a, d, tk=tk, tn=tn, tm=tm, out_buf=gw.get((weight, layer)),
                                  out_shape=wts[weight, layer].shape, col_block0=col_block0, **kw)

    def scatter_layer(units):
        grads = []
        for n, l in units:
            g = gw[n, l]
            if KIND[n] == "stk":
                g = g[:, :, :GLA_IN].reshape(1, D, N_DEV, GLA_IN // N_DEV).transpose(2, 0, 1, 3)
            grads.append(g)
        return scatter_plan(grads, [KIND[n] for n, _ in units], [(1,) + tuple(w[n].shape[1:]) for n, _ in units])

    results = {n: None for n in BIG}

    def update(units, parts):
        for (n, l), p in zip(units, parts):
            results[n] = adamw_layer(w[n], m[n], v[n], p, l, results[n], name=f"adamw_{n}_{l}")

    gs = {"ln_g": [[None, None] for _ in range(DEPTH)], "ln_b": [[None, None] for _ in range(DEPTH)],
          "gla_w_gk2": [None] * 2, "gla_b_gk": [None] * 2, "gla_g_norm": [None] * 2, "att_b_in": [None] * 2,
          "att_rel_bias": [None] * 2}
    dmods = [[None] * 6 for _ in range(DEPTH)]
    nxt = None
    nxt_slot = None
    for i in reversed(range(DEPTH)):
        rec = saved[i]
        sh1, sc1, g1, sh2, sc2, g2 = [mods[i, k:k + 1] for k in range(6)]
        x0, x1 = rec["x0"], rec["x1"]
        if nxt is None:
            dz2, acc, dzs2, loss = ln_bwd(x1, rec["y2"], g2, _vec(sm["ln_g"][i, 1]),
                                          loss=(target, _vec(sm["ln_b"][i, 1])), name=f"ln_ff_bwd_{i}")
        else:
            dz2, acc, dzs2 = ln_bwd(x1, rec["y2"], g2, _vec(sm["ln_g"][i, 1]),
                                    nxt=nxt[:3] + (_vec(sm["ln_b"][i, 1]),), name=f"ln_ff_bwd_{i}")
            dmods[nxt_slot[0]][nxt_slot[1]] = acc[3]
            dmods[nxt_slot[0]][nxt_slot[2]] = acc[4]
        gs["ln_g"][i][1], gs["ln_b"][i][1], dmods[i][5] = acc[0], acc[1], acc[2]
        dh = mm_nt([dzs2], wts["ff_w2", i], 0, epi_h=rec["h"], out_dtype=BF16, tm=1024, tn=1024,
                   name=f"ff_down_bwd_{i}")
        wgrad("ff_w2", i, rec["h"], dzs2, pro="relu2", tk=2048, tn=1024, name=f"ff_w2_grad_{i}")
        du2 = mm_nt([dh], wts["ff_w1", i], 0, tm=1024, tn=1024, name=f"ff_up_bwd_{i}")
        wgrad("ff_w1", i, x1, dh, pro="mod", p1=sc2, p2=sh2, tn=2048, name=f"ff_w1_grad_{i}")
        dz1, acc, dzs1 = ln_bwd(x0, rec["y"], g1, _vec(sm["ln_g"][i, 0]),
                                nxt=(dz2, du2, sc2, _vec(sm["ln_b"][i, 0])), name=f"ln_mix_bwd_{i}")
        dmods[i][4], dmods[i][3] = acc[3], acc[4]
        gs["ln_g"][i][0], gs["ln_b"][i][0], dmods[i][2] = acc[0], acc[1], acc[2]
        j = rec["j"]
        w_in, w_out = _layer_weights(i)[:2]
        if rec["kind"] == "gla":
            dog = mm_nt([dzs1], wts[w_out], 0, tm=1024, tn=1024, name=f"gla_out_bwd_{i}")
            wgrad(*w_out, rec["og"], dzs1, tn=1024, name=f"gla_wout_grad_{i}")
        else:
            do = mm_nt([dzs1], wts[w_out], 0, out_dtype=BF16, tm=1024, tn=1024, name=f"att_out_bwd_{i}")
            wgrad(*w_out, rec["o"], dzs1, tn=1024, name=f"att_wout_grad_{i}")
        units = [("ff_w1", i), ("ff_w2", i), w_out] + ([_layer_weights(i + 1)[0]] if i + 1 < DEPTH else [])
        plan = scatter_layer(units)
        if rec["kind"] == "gla":
            (dproj, dw2p, dbgk, dgn), parts = gla_bwd(rec["proj"], dog, rec["states"], rec["w2p"], rec["bgk"],
                                                      rec["gn"], comm=plan, name=f"gla_core_bwd_{i}")
            gs["gla_w_gk2"][j], gs["gla_b_gk"][j], gs["gla_g_norm"][j] = dw2p[:GLA_RANK], dbgk[0], dgn[0]
            wgrad(*w_in, x0, dproj, pro="mod", p1=sc1, p2=sh1, tk=512, tn=GLA_INP, name=f"gla_win_grad_{i}")
            if i == 0:
                du1, last_parts = mm_nt([dproj], wts[w_in], 0, tm=1024, tn=1024, comm=scatter_layer([w_in]),
                                        name=f"gla_proj_bwd_{i}")
                update([w_in], last_parts)
            else:
                du1 = mm_nt([dproj], wts[w_in], 0, tm=1024, tn=1024, name=f"gla_proj_bwd_{i}")
        else:
            (dq, dk, dv, dtiles, dclip, sq, sk, sv), parts = attn_bwd(rec["qkv"], do, rec["tiles"], rec["clip"],
                                                                      comm=plan, name=f"att_core_bwd_{i}")
            drel = rel_bias_grad(dtiles.reshape(ATT_H, -1), dclip.reshape(ATT_H, 128), rel_idx,
                                 name=f"att_bias_grad_{i}")
            gs["att_rel_bias"][j] = drel[:, :N_REL]
            gs["att_b_in"][j] = jnp.concatenate([sq[0], sk[0], sv[0]])
            du1 = mm_nt([dq, dk, dv], wts[w_in], 0, tm=512, tn=1024, name=f"att_proj_bwd_{i}")
            for n, t in enumerate((dq, dk, dv)):
                wgrad(*w_in, x0, t, pro="mod", p1=sc1, p2=sh1, tn=1024, col_block0=n, name=f"att_win_grad_{i}_{n}")
        update(units, parts)
        nxt = (dz1, du1, sc1, x0)
        nxt_slot = (i, 1, 0)
    dx, acc = combine_final(nxt[0], nxt[1], nxt[2], nxt[3], name="grad_x")
    dmods[0][1], dmods[0][0] = acc[3], acc[4]
    dmods = jnp.stack([jnp.stack(r) for r in dmods])
    gs = {k: jnp.stack([jnp.stack(r) if isinstance(r, list) else r for r in v]) for k, v in gs.items()}
    return loss, dx, dmods, gs, results


WEIGHTS = ("w_ada", "b_ada", "ln_g", "ln_b", "gla_w_in", "gla_w_gk2", "gla_b_gk", "gla_g_norm", "gla_w_out",
           "att_w_in", "att_b_in", "att_rel_bias", "att_w_out", "ff_w1", "ff_w2")
SMALL_SHARDED = {"ln_g": 2, "ln_b": 2, "gla_w_gk2": 2, "gla_g_norm": 2, "att_b_in": 1}
SMALL_GRADS = ("ln_g", "ln_b", "gla_w_gk2", "gla_b_gk", "gla_g_norm", "att_b_in", "att_rel_bias")


def kernel(x, c, w_ada, b_ada, ln_g, ln_b, gla_w_in, gla_w_gk2, gla_b_gk, gla_g_norm, gla_w_out, att_w_in, att_b_in, att_rel_bias, att_w_out, ff_w1, ff_w2, loss_target, m_w_ada, m_b_ada, m_ln_g, m_ln_b, m_gla_w_in, m_gla_w_gk2, m_gla_b_gk, m_gla_g_norm, m_gla_w_out, m_att_w_in, m_att_b_in, m_att_rel_bias, m_att_w_out, m_ff_w1, m_ff_w2, v_w_ada, v_b_ada, v_ln_g, v_ln_b, v_gla_w_in, v_gla_w_gk2, v_gla_b_gk, v_gla_g_norm, v_gla_w_out, v_att_w_in, v_att_b_in, v_att_rel_bias, v_att_w_out, v_ff_w1, v_ff_w2):
    w = dict(w_ada=w_ada, b_ada=b_ada, ln_g=ln_g, ln_b=ln_b, gla_w_in=gla_w_in, gla_w_gk2=gla_w_gk2,
             gla_b_gk=gla_b_gk, gla_g_norm=gla_g_norm, gla_w_out=gla_w_out, att_w_in=att_w_in, att_b_in=att_b_in,
             att_rel_bias=att_rel_bias, att_w_out=att_w_out, ff_w1=ff_w1, ff_w2=ff_w2)
    m = dict(w_ada=m_w_ada, b_ada=m_b_ada, ln_g=m_ln_g, ln_b=m_ln_b, gla_w_in=m_gla_w_in, gla_w_gk2=m_gla_w_gk2,
             gla_b_gk=m_gla_b_gk, gla_g_norm=m_gla_g_norm, gla_w_out=m_gla_w_out, att_w_in=m_att_w_in,
             att_b_in=m_att_b_in, att_rel_bias=m_att_rel_bias, att_w_out=m_att_w_out, ff_w1=m_ff_w1, ff_w2=m_ff_w2)
    v = dict(w_ada=v_w_ada, b_ada=v_b_ada, ln_g=v_ln_g, ln_b=v_ln_b, gla_w_in=v_gla_w_in, gla_w_gk2=v_gla_w_gk2,
             gla_b_gk=v_gla_b_gk, gla_g_norm=v_gla_g_norm, gla_w_out=v_gla_w_out, att_w_in=v_att_w_in,
             att_b_in=v_att_b_in, att_rel_bias=v_att_rel_bias, att_w_out=v_att_w_out, ff_w1=v_ff_w1, ff_w2=v_ff_w2)
    xi, yi, ci = _my_place()
    me = 4 * xi + 2 * yi + ci

    small_names = tuple(SMALL_SHARDED)
    small_in = _pack_small([c] + [w[n] for n in small_names])
    small_all = all_gather(small_in, name="gather_small").reshape(N_DEV, -1, 128)
    parts = _unpack_small(small_all, [c.shape] + [w[n].shape for n in small_names])
    c_all = parts[0].reshape(N_DEV, D)
    sm = {"gla_b_gk": gla_b_gk, "att_rel_bias": att_rel_bias}
    for n, p in zip(small_names, parts[1:]):
        ax = SMALL_SHARDED[n]
        sm[n] = jnp.moveaxis(p, 0, ax).reshape(p.shape[1:ax + 1] + (N_DEV * p.shape[ax + 1],) + p.shape[ax + 2:])

    n_ada = w_ada.shape[2]
    mp = mods_partial(c_all, w_ada, name="mods_partial")
    mp_all = all_gather(mp.reshape(DEPTH * N_DEV, n_ada), name="gather_mods")
    mp_all = mp_all.reshape(N_DEV, DEPTH, N_DEV, n_ada)
    mods = lax.dynamic_index_in_dim(mp_all, me, axis=2, keepdims=False)
    mods = mods.transpose(1, 0, 2).reshape(DEPTH, 6 * D) + b_ada
    mods = mods.reshape(DEPTH, 6, D)

    loss, dx, dmods, gs, results = _trunk(x.reshape(x.shape[1:]), loss_target.reshape(x.shape[1:]), mods, sm, w, m, v)
    loss = lax.psum(loss[0, 0], ("x", "y", "c"))

    dm_flat = dmods.reshape(DEPTH, 6 * D)
    small_g = [dm_flat] + [gs[n].reshape(sm[n].shape) for n in SMALL_GRADS]
    small_shapes = [a.shape for a in small_g]
    sg_all = all_gather(_pack_small(small_g), name="gather_small_grads").reshape(N_DEV, -1, 128)
    summed = _unpack_small(sum_parts(sg_all, name="sum_small_grads"), small_shapes)
    g_full = dict(zip(("b_ada",) + SMALL_GRADS, summed))
    dm_all = _unpack_small(sg_all, small_shapes)[0]
    dm_mine = lax.dynamic_slice_in_dim(dm_all, me * n_ada, n_ada, axis=2).transpose(1, 0, 2)
    g_w_ada = w_ada_grad(c_all, dm_mine, name="w_ada_grad")

    results["w_ada"] = adamw_nd(w_ada, m_w_ada, v_w_ada, g_w_ada[None], name="adamw_w_ada")
    for n in ("b_ada",) + SMALL_GRADS:
        g = g_full[n]
        if n in SMALL_SHARDED:
            ax = SMALL_SHARDED[n]
            width = w[n].shape[ax]
            g = lax.dynamic_slice_in_dim(g, me * width, width, axis=ax)
        results[n] = adamw_nd(w[n], m[n], v[n], g[None], name=f"adamw_{n}")

    out = [loss, dx[None]]
    for k in range(4):
        out += [results[n][k] for n in WEIGHTS]
    return tuple(out)
```

```python
import numpy as np
import jax
import jax.numpy as jnp
from jax import lax
from jax.experimental import pallas as pl
from jax.experimental.pallas import tpu as pltpu

F32 = jnp.float32
BF16 = jnp.bfloat16
HIGHEST = lax.Precision.HIGHEST
MESH = pl.DeviceIdType.MESH

N_DEV = 8
D = 1024
DEPTH = 4
CHUNK = 64
ALPHA = (2.0 * DEPTH) ** 0.25
LN_EPS = 1e-5
RMS_EPS = 1e-6
NEG_INF = -1e30

GLA_H = 4
GLA_DKH = 128
GLA_DVH = 256
GLA_DK = GLA_H * GLA_DKH
GLA_DV = GLA_H * GLA_DVH
GLA_RANK = 16
GLA_IN = 2 * GLA_DK + 2 * GLA_DV + GLA_RANK
GLA_INP = 3200
GLA_TAU_INV = 1.0 / 16.0
GLA_SUB = 2

ATT_H = 16
ATT_HD = 64
ATT_QB = 256
ATT_KB = 3 * ATT_QB
LEFT = 8 * CHUNK
MAX_REL = 128
N_REL = 2 * MAX_REL + 1
REL_PAD = 384
REL_TILE = 128
REL_TILES = (3, 4)
D_FF = 4 * D

ADAM_LR = 0.001
ADAM_B1 = 0.9
ADAM_B2 = 0.999
ADAM_EPS = 1e-08
ADAM_WD = 0.01
ADAM_STEP = 10

VMEM_LIMIT = 48 * 1024 * 1024


def _params(n_axes):
    return pltpu.CompilerParams(dimension_semantics=("arbitrary",) * n_axes, vmem_limit_bytes=VMEM_LIMIT)


def _dot(a, b):
    return jnp.dot(a, b, preferred_element_type=F32)


def _dot_nt(a, b):
    return lax.dot_general(a, b, (((1,), (1,)), ((), ())), preferred_element_type=F32)


def _dot_tn(a, b):
    return lax.dot_general(a, b, (((0,), (0,)), ((), ())), preferred_element_type=F32)


def _bf(a):
    return a.astype(BF16)


def _prologue(kind, a, p1=None, p2=None):
    if kind == "mod":
        return a.astype(F32) * (1.0 + p1) + p2
    if kind == "relu2":
        r = jnp.maximum(a, 0.0)
        return r * r
    return a


class Hosted:
    def __init__(self, inputs, out_shapes, sems, first, last, mid=None, last_after_mid=None):
        self.inputs, self.out_shapes, self.sems, self.first, self.last = inputs, out_shapes, sems, first, last
        self.mid, self.last_after_mid = mid, last_after_mid


def _hbm_specs(n):
    return [pl.BlockSpec(memory_space=pltpu.HBM)] * n


def _call_hosting(body, comm, *, first, last, in_specs, out_specs, out_shape, scratch_shapes, args, mid=None, **kw):
    if comm is None:
        return pl.pallas_call(body, in_specs=in_specs, out_specs=out_specs, out_shape=out_shape,
                              scratch_shapes=scratch_shapes, **kw)(*args), []
    n_in, n_out, n_scr = len(in_specs), len(out_specs), len(scratch_shapes)
    n_ci, n_co = len(comm.inputs), len(comm.out_shapes)

    def hosting(*refs):
        ins, ci = refs[:n_in], refs[n_in:n_in + n_ci]
        k = n_in + n_ci
        outs, co = refs[k:k + n_out], refs[k + n_out:k + n_out + n_co]
        k += n_out + n_co
        scr, cs = refs[k:k + n_scr], refs[k + n_scr:]

        @pl.when(first())
        def _():
            comm.first(ci, co, cs)

        body(*ins, *outs, *scr)
        split = mid is not None and comm.mid is not None
        if split:
            @pl.when(mid())
            def _():
                comm.mid(ci, co, cs)

        @pl.when(last())
        def _():
            (comm.last_after_mid if split else comm.last)(ci, co, cs)

    res = pl.pallas_call(
        hosting, in_specs=list(in_specs) + _hbm_specs(n_ci), out_specs=list(out_specs) + _hbm_specs(n_co),
        out_shape=list(out_shape) + list(comm.out_shapes), scratch_shapes=list(scratch_shapes) + list(comm.sems),
        **kw)(*args, *comm.inputs)
    return res[:n_out], res[n_out:]


def run_hosted(comm, *, name):
    n_i, n_o = len(comm.inputs), len(comm.out_shapes)

    def body(*refs):
        ins, outs, sems = refs[:n_i], refs[n_i:n_i + n_o], refs[n_i + n_o:]
        comm.first(ins, outs, sems)
        comm.last(ins, outs, sems)

    return pl.pallas_call(body, name=name, out_shape=list(comm.out_shapes), in_specs=_hbm_specs(n_i),
                          out_specs=_hbm_specs(n_o), scratch_shapes=list(comm.sems))(*comm.inputs)


def mm_nn(a, b, layer, *, pro=None, p1=None, p2=None, bias=None, out_dtype=F32, tm, tn, comm=None, name):
    M, K = a.shape
    N = b.shape[2]
    tm = min(tm, M)
    n_p = {"mod": 2}.get(pro, 0)
    has_bias = bias is not None
    direct = pro is None and a.dtype == BF16

    def body(*refs):
        a_ref, b_ref = refs[0], refs[1]
        p_refs = refs[2:2 + n_p]
        bias_ref = refs[2 + n_p] if has_bias else None
        if direct:
            o_ref = refs[-1]
            lhs = a_ref[...]
        else:
            o_ref, abf = refs[-2], refs[-1]

            @pl.when(pl.program_id(1) == 0)
            def _():
                abf[...] = _bf(_prologue(pro, a_ref[...], *[r[...] for r in p_refs]))

            lhs = abf[...]
        acc = _dot(lhs, b_ref[...])
        if has_bias:
            acc = acc + bias_ref[...]
        o_ref[...] = acc.astype(out_dtype)

    in_specs = [pl.BlockSpec((tm, K), lambda i, j: (i, 0)), pl.BlockSpec((None, K, tn), lambda i, j: (layer, 0, j))]
    args = [a, b]
    for p in (p1, p2)[:n_p]:
        in_specs.append(pl.BlockSpec((1, K), lambda i, j: (0, 0)))
        args.append(p)
    if has_bias:
        in_specs.append(pl.BlockSpec((1, tn), lambda i, j: (0, j)))
        args.append(bias)
    n_i, n_j = M // tm, N // tn
    (out,), got = _call_hosting(
        body, comm, first=lambda: (pl.program_id(0) == 0) & (pl.program_id(1) == 0),
        last=lambda: (pl.program_id(0) == n_i - 1) & (pl.program_id(1) == n_j - 1),
        name=name, grid=(n_i, n_j), in_specs=in_specs,
        out_specs=[pl.BlockSpec((tm, tn), lambda i, j: (i, j))],
        out_shape=[jax.ShapeDtypeStruct((M, N), out_dtype)],
        scratch_shapes=[] if direct else [pltpu.VMEM((tm, K), BF16)], compiler_params=_params(2), args=args)
    return out if comm is None else (out, got)


def mm_nt(a_parts, w, layer, *, epi_h=None, out_dtype=F32, tm, tn, comm=None, name):
    M = a_parts[0].shape[0]
    tm = min(tm, M)
    widths = [p.shape[1] for p in a_parts]
    Nw = sum(widths)
    Kw = w.shape[1]
    n_a = len(a_parts)
    has_h = epi_h is not None
    direct = n_a == 1 and a_parts[0].dtype == BF16

    def body(*refs):
        a_refs = refs[:n_a]
        w_ref = refs[n_a]
        h_ref = refs[n_a + 1] if has_h else None
        if direct:
            o_ref = refs[-1]
            lhs = a_refs[0][...]
        else:
            o_ref, abf = refs[-2], refs[-1]

            @pl.when(pl.program_id(1) == 0)
            def _():
                off = 0
                for r, wd in zip(a_refs, widths):
                    abf[:, off:off + wd] = _bf(r[...])
                    off += wd

            lhs = abf[...]
        acc = _dot_nt(lhs, w_ref[...])
        if has_h:
            acc = acc * (2.0 * jnp.maximum(h_ref[...], 0.0))
        o_ref[...] = acc.astype(out_dtype)

    in_specs = [pl.BlockSpec((tm, wd), lambda i, j: (i, 0)) for wd in widths]
    in_specs.append(pl.BlockSpec((None, tn, Nw), lambda i, j: (layer, j, 0)))
    args = list(a_parts) + [w]
    if has_h:
        in_specs.append(pl.BlockSpec((tm, tn), lambda i, j: (i, j)))
        args.append(epi_h)
    n_i, n_j = M // tm, Kw // tn
    (out,), got = _call_hosting(
        body, comm, first=lambda: (pl.program_id(0) == 0) & (pl.program_id(1) == 0),
        last=lambda: (pl.program_id(0) == n_i - 1) & (pl.program_id(1) == n_j - 1),
        name=name, grid=(n_i, n_j), in_specs=in_specs,
        out_specs=[pl.BlockSpec((tm, tn), lambda i, j: (i, j))],
        out_shape=[jax.ShapeDtypeStruct((M, Kw), out_dtype)],
        scratch_shapes=[] if direct else [pltpu.VMEM((tm, Nw), BF16)], compiler_params=_params(2), args=args)
    return out if comm is None else (out, got)


def mm_tn(a, d, *, pro=None, p1=None, p2=None, tk, tn, tm, out_buf, out_shape, col_block0=0, name):
    M, Kf = a.shape
    N = d.shape[1]
    n_p = {"mod": 2}.get(pro, 0)
    has_buf = out_buf is not None
    n_m = M // tm

    def body(*refs):
        a_ref, d_ref = refs[0], refs[1]
        p_refs = refs[2:2 + n_p]
        o_ref, acc = refs[-2], refs[-1]
        m = pl.program_id(2)

        @pl.when(m == 0)
        def _():
            acc[...] = jnp.zeros_like(acc)

        av = _prologue(pro, a_ref[...], *[r[...] for r in p_refs])
        acc[...] += _dot_tn(_bf(av), _bf(d_ref[...]))

        @pl.when(m == n_m - 1)
        def _():
            o_ref[...] = _bf(acc[...])

    in_specs = [pl.BlockSpec((tm, tk), lambda i, j, m: (m, i)), pl.BlockSpec((tm, tn), lambda i, j, m: (m, j))]
    args = [a, d]
    for p in (p1, p2)[:n_p]:
        in_specs.append(pl.BlockSpec((1, tk), lambda i, j, m: (0, i)))
        args.append(p)
    aliases = {}
    if has_buf:
        in_specs.append(pl.BlockSpec(memory_space=pl.ANY))
        args.append(out_buf)
        aliases = {len(args) - 1: 0}
    return pl.pallas_call(
        body, name=name, grid=(Kf // tk, N // tn, n_m), in_specs=in_specs,
        out_specs=pl.BlockSpec((None, tk, tn), lambda i, j, m: (0, i, col_block0 + j)),
        out_shape=jax.ShapeDtypeStruct(out_shape, BF16), input_output_aliases=aliases,
        scratch_shapes=[pltpu.VMEM((tk, tn), F32)], compiler_params=_params(3),
    )(*args)


ROW_BLOCK = 512
ACC_ROWS = 8


def _ln_stats(z):
    mu = jnp.mean(z, axis=-1, keepdims=True)
    zc = z - mu
    var = jnp.mean(zc * zc, axis=-1, keepdims=True)
    return zc, lax.rsqrt(var + LN_EPS)


def ln_fwd(x, y, gate, lng, lnb, *, name):
    S = x.shape[0]

    def body(x_ref, y_ref, gt_ref, g_ref, b_ref, o_ref):
        z = ALPHA * x_ref[...] + (1.0 + gt_ref[...]) * y_ref[...]
        zc, rstd = _ln_stats(z)
        o_ref[...] = (zc * rstd) * g_ref[...] + b_ref[...]

    row = pl.BlockSpec((ROW_BLOCK, D), lambda i: (i, 0))
    vec = pl.BlockSpec((1, D), lambda i: (0, 0))
    return pl.pallas_call(
        body, name=name, grid=(S // ROW_BLOCK,), in_specs=[row, row, vec, vec, vec], out_specs=row,
        out_shape=jax.ShapeDtypeStruct((S, D), F32), compiler_params=_params(1),
    )(x, y, gate, lng, lnb)


def _add_colsum(acc_ref, r, val):
    acc_ref[r:r + 1, :] += jnp.sum(val, axis=0, keepdims=True)


def ln_bwd(x_in, y, gate, lng, *, loss=None, nxt=None, name):
    S = x_in.shape[0]
    has_next = nxt is not None

    def body(*refs):
        if has_next:
            dzn_ref, dun_ref, scn_ref, b_ref = refs[:4]
            k = 4
        else:
            t_ref, b_ref = refs[:2]
            k = 2
        x_ref, y_ref, gt_ref, g_ref = refs[k:k + 4]
        dz_ref, acc_ref, dzs_ref = refs[k + 4:k + 7]
        loss_ref = None if has_next else refs[k + 7]

        @pl.when(pl.program_id(0) == 0)
        def _():
            acc_ref[...] = jnp.zeros_like(acc_ref)
            if not has_next:
                loss_ref[...] = jnp.zeros_like(loss_ref)

        yv = y_ref[...]
        z = ALPHA * x_ref[...] + (1.0 + gt_ref[...]) * yv
        zc, rstd = _ln_stats(z)
        xhat = zc * rstd
        if has_next:
            du = dun_ref[...]
            dout_v = ALPHA * dzn_ref[...] + du * (1.0 + scn_ref[...])
            _add_colsum(acc_ref, 3, du * (xhat * g_ref[...] + b_ref[...]))
            _add_colsum(acc_ref, 4, du)
        else:
            e = (xhat * g_ref[...] + b_ref[...]) - t_ref[...]
            dout_v = e * (1.0 / D)
            per_tok = jnp.sum(e * e, axis=1, keepdims=True) * (1.0 / D)
            loss_ref[...] += 0.5 * jnp.sum(per_tok, axis=0, keepdims=True)
        _add_colsum(acc_ref, 0, dout_v * xhat)
        _add_colsum(acc_ref, 1, dout_v)
        dxh = dout_v * g_ref[...]
        m1 = jnp.mean(dxh, axis=-1, keepdims=True)
        m2 = jnp.mean(dxh * xhat, axis=-1, keepdims=True)
        dz = rstd * (dxh - m1 - xhat * m2)
        _add_colsum(acc_ref, 2, dz * yv)
        dz_ref[...] = dz
        dzs_ref[...] = _bf(dz * (1.0 + gt_ref[...]))

    row = pl.BlockSpec((ROW_BLOCK, D), lambda i: (i, 0))
    vec = pl.BlockSpec((1, D), lambda i: (0, 0))
    if has_next:
        in_specs = [row, row, vec, vec]
        args = list(nxt)
    else:
        in_specs = [row, vec]
        args = list(loss)
    in_specs += [row, row, vec, vec]
    args += [x_in, y, gate, lng]
    out_specs = [row, pl.BlockSpec((ACC_ROWS, D), lambda i: (0, 0)), row]
    out_shape = [jax.ShapeDtypeStruct((S, D), F32), jax.ShapeDtypeStruct((ACC_ROWS, D), F32),
                 jax.ShapeDtypeStruct((S, D), BF16)]
    if not has_next:
        out_specs.append(pl.BlockSpec((8, 128), lambda i: (0, 0)))
        out_shape.append(jax.ShapeDtypeStruct((8, 128), F32))
    return pl.pallas_call(
        body, name=name, grid=(S // ROW_BLOCK,), in_specs=in_specs, out_specs=out_specs, out_shape=out_shape,
        compiler_params=_params(1),
    )(*args)


def combine_final(dz, du, sc, x_in, *, name):
    S = dz.shape[0]

    def body(dz_ref, du_ref, sc_ref, x_ref, dx_ref, acc_ref):
        @pl.when(pl.program_id(0) == 0)
        def _():
            acc_ref[...] = jnp.zeros_like(acc_ref)

        du_v = du_ref[...]
        dx_ref[...] = ALPHA * dz_ref[...] + du_v * (1.0 + sc_ref[...])
        _add_colsum(acc_ref, 3, du_v * x_ref[...])
        _add_colsum(acc_ref, 4, du_v)

    row = pl.BlockSpec((ROW_BLOCK, D), lambda i: (i, 0))
    vec = pl.BlockSpec((1, D), lambda i: (0, 0))
    return pl.pallas_call(
        body, name=name, grid=(S // ROW_BLOCK,), in_specs=[row, row, vec, row],
        out_specs=[row, pl.BlockSpec((ACC_ROWS, D), lambda i: (0, 0))],
        out_shape=[jax.ShapeDtypeStruct((S, D), F32), jax.ShapeDtypeStruct((ACC_ROWS, D), F32)],
        compiler_params=_params(1),
    )(dz, du, sc, x_in)


def _log_sigmoid(x):
    return jnp.minimum(x, 0.0) - jnp.log(1.0 + jnp.exp(-jnp.abs(x)))


def _silu(x):
    return x * (1.0 / (1.0 + jnp.exp(-x)))


def _cumsum_steps(x):
    row = lax.broadcasted_iota(jnp.int32, x.shape, 0)
    step = 1
    while step < x.shape[0]:
        x = x + jnp.where(row >= step, pltpu.roll(x, step, 0), 0.0)
        step *= 2
    return x


@jax.custom_vjp
def _cumsum_rows(x):
    return _cumsum_steps(x)


def _cumsum_rows_fwd(x):
    return _cumsum_steps(x), None


def _cumsum_rows_bwd(_, g):
    return (jnp.sum(g, axis=0, keepdims=True) - _cumsum_steps(g) + g,)


_cumsum_rows.defvjp(_cumsum_rows_fwd, _cumsum_rows_bwd)


def _gla_chunk(q, k, v, g, gk, s0t, w2p, bgk, gn):
    C = q.shape[0]
    row = lax.broadcasted_iota(jnp.int32, (C, C), 0)
    col = lax.broadcasted_iota(jnp.int32, (C, C), 1)
    lower = row >= col
    la = _log_sigmoid(_dot(_bf(gk), _bf(w2p)) + bgk) * GLA_TAU_INV
    outs, states = [], []
    for h in range(GLA_H):
        ks = slice(h * GLA_DKH, (h + 1) * GLA_DKH)
        vs = slice(h * GLA_DVH, (h + 1) * GLA_DVH)
        qh = q[:, ks] * (GLA_DKH ** -0.5)
        kh, vh, gh, lah, s0 = k[:, ks], v[:, vs], g[:, vs], la[:, ks], s0t[h]
        cum = _cumsum_rows(lah)
        e_pos = jnp.exp(cum)
        e_neg = jnp.exp(-cum)
        q_f = qh * e_pos
        a_f = _dot_nt(_bf(q_f), _bf(kh * e_neg))
        a_b = _dot_nt(_bf(qh * e_neg), _bf(kh * e_pos))
        att = jnp.where(lower, a_f, a_b)
        o = _dot(_bf(att), _bf(vh)) + _dot_nt(_bf(q_f), _bf(s0))
        tot = jnp.sum(lah, axis=0, keepdims=True)
        k_end = kh * jnp.exp(tot - cum)
        states.append(s0 * jnp.exp(tot) + _dot_tn(_bf(vh), _bf(k_end)))
        on = o * lax.rsqrt(jnp.mean(o * o, axis=-1, keepdims=True) + RMS_EPS) * gn[:, vs]
        outs.append(on * _silu(gh))
    return jnp.concatenate(outs, axis=1), tuple(states)


def _gla_split(p):
    return (p[:, 0:GLA_DK], p[:, GLA_DK:2 * GLA_DK], p[:, 2 * GLA_DK:2 * GLA_DK + GLA_DV],
            p[:, 2 * GLA_DK + GLA_DV:2 * GLA_DK + 2 * GLA_DV], p[:, 2 * GLA_DK + 2 * GLA_DV:GLA_INP])


def gla_fwd(proj, w2p, bgk, gn, *, comm=None, name):
    S = proj.shape[0]
    n_c = S // CHUNK
    n_s = n_c // GLA_SUB
    rows = GLA_SUB * CHUNK

    def body(p_ref, w_ref, b_ref, gn_ref, o_ref, st_ref, st):
        @pl.when(pl.program_id(0) == 0)
        def _():
            st[...] = jnp.zeros_like(st)

        s = tuple(st[h] for h in range(GLA_H))
        for u in range(GLA_SUB):
            sub = slice(u * CHUNK, (u + 1) * CHUNK)
            for h in range(GLA_H):
                st_ref[u, h] = s[h]
            og, s = _gla_chunk(*_gla_split(p_ref[sub, :]), s, w_ref[...], b_ref[...], gn_ref[...])
            o_ref[sub, :] = _bf(og)
        for h in range(GLA_H):
            st[h] = s[h]

    full = lambda shape: pl.BlockSpec(shape, lambda i: (0,) * len(shape))
    return _call_hosting(
        body, comm, first=lambda: pl.program_id(0) == 0, last=lambda: pl.program_id(0) == n_s - 1,
        name=name, grid=(n_s,),
        in_specs=[pl.BlockSpec((rows, GLA_INP), lambda i: (i, 0)), full((128, GLA_DK)), full((1, GLA_DK)),
                  full((1, GLA_DV))],
        out_specs=[pl.BlockSpec((rows, GLA_DV), lambda i: (i, 0)),
                   pl.BlockSpec((GLA_SUB, GLA_H, GLA_DVH, GLA_DKH), lambda i: (i, 0, 0, 0))],
        out_shape=[jax.ShapeDtypeStruct((S, GLA_DV), BF16),
                   jax.ShapeDtypeStruct((n_c, GLA_H, GLA_DVH, GLA_DKH), F32)],
        scratch_shapes=[pltpu.VMEM((GLA_H, GLA_DVH, GLA_DKH), F32)], compiler_params=_params(1),
        args=(proj, w2p, bgk, gn))


def gla_bwd(proj, dog, states, w2p, bgk, gn, *, comm=None, name):
    S = proj.shape[0]
    n_c = S // CHUNK
    n_s = n_c // GLA_SUB
    rows = GLA_SUB * CHUNK

    def body(p_ref, dog_ref, st_ref, w_ref, b_ref, gn_ref, dp_ref, dw_ref, db_ref, dgn_ref, ds_ref):
        @pl.when(pl.program_id(0) == 0)
        def _():
            ds_ref[...] = jnp.zeros_like(ds_ref)
            dw_ref[...] = jnp.zeros_like(dw_ref)
            db_ref[...] = jnp.zeros_like(db_ref)
            dgn_ref[...] = jnp.zeros_like(dgn_ref)

        ds = tuple(ds_ref[h] for h in range(GLA_H))
        for u in reversed(range(GLA_SUB)):
            sub = slice(u * CHUNK, (u + 1) * CHUNK)
            q, k, v, g, gk = _gla_split(p_ref[sub, :])
            s0 = tuple(st_ref[u, h] for h in range(GLA_H))
            _, vjp = jax.vjp(_gla_chunk, q, k, v, g, gk, s0, w_ref[...], b_ref[...], gn_ref[...])
            dq, dk, dv, dg, dgk, ds, dw, db, dgn = vjp((dog_ref[sub, :], ds))
            dp_ref[sub, 0:GLA_DK] = _bf(dq)
            dp_ref[sub, GLA_DK:2 * GLA_DK] = _bf(dk)
            dp_ref[sub, 2 * GLA_DK:2 * GLA_DK + GLA_DV] = _bf(dv)
            dp_ref[sub, 2 * GLA_DK + GLA_DV:2 * GLA_DK + 2 * GLA_DV] = _bf(dg)
            dp_ref[sub, 2 * GLA_DK + 2 * GLA_DV:GLA_INP] = _bf(dgk)
            dw_ref[...] += dw
            db_ref[...] += db
            dgn_ref[...] += dgn
        for h in range(GLA_H):
            ds_ref[h] = ds[h]

    full = lambda shape: pl.BlockSpec(shape, lambda i: (0,) * len(shape))
    rev = lambda i: (n_s - 1 - i, 0)
    return _call_hosting(
        body, comm, first=lambda: pl.program_id(0) == 0, last=lambda: pl.program_id(0) == n_s - 1,
        name=name, grid=(n_s,),
        in_specs=[pl.BlockSpec((rows, GLA_INP), rev), pl.BlockSpec((rows, GLA_DV), rev),
                  pl.BlockSpec((GLA_SUB, GLA_H, GLA_DVH, GLA_DKH), lambda i: (n_s - 1 - i, 0, 0, 0)),
                  full((128, GLA_DK)), full((1, GLA_DK)), full((1, GLA_DV))],
        out_specs=[pl.BlockSpec((rows, GLA_INP), rev), full((128, GLA_DK)), full((1, GLA_DK)), full((1, GLA_DV))],
        out_shape=[jax.ShapeDtypeStruct((S, GLA_INP), BF16), jax.ShapeDtypeStruct((128, GLA_DK), F32),
                   jax.ShapeDtypeStruct((1, GLA_DK), F32), jax.ShapeDtypeStruct((1, GLA_DV), F32)],
        scratch_shapes=[pltpu.VMEM((GLA_H, GLA_DVH, GLA_DKH), F32)], compiler_params=_params(1),
        args=(proj, dog, states, w2p, bgk, gn))


def _rel_index():
    t = np.arange(REL_TILE)[:, None]
    j = np.arange(REL_TILE)[None, :]
    tiles = []
    for m in REL_TILES:
        chunks = (REL_TILE // CHUNK) * m + j // CHUNK - t // CHUNK
        band = (chunks >= 0) & (chunks <= LEFT // CHUNK)
        dist = LEFT - REL_TILE * m + t - j
        tiles.append(np.where(band, np.minimum(dist, MAX_REL) + MAX_REL, N_REL))
    return jnp.asarray(np.stack(tiles).reshape(1, -1).astype(np.int32))


REL_BLOCK = 2048


def _one_hot(idx_row):
    return (lax.broadcasted_iota(jnp.int32, (REL_PAD, idx_row.shape[1]), 0) == idx_row).astype(F32)


def rel_bias_tiles(rel_pad, idx, *, name):
    E = idx.shape[1]

    def body(r_ref, i_ref, o_ref):
        o_ref[...] = jnp.dot(r_ref[...], _one_hot(i_ref[...]), precision=HIGHEST, preferred_element_type=F32)

    return pl.pallas_call(
        body, name=name, grid=(E // REL_BLOCK,),
        in_specs=[pl.BlockSpec((ATT_H, REL_PAD), lambda i: (0, 0)), pl.BlockSpec((1, REL_BLOCK), lambda i: (0, i))],
        out_specs=pl.BlockSpec((ATT_H, REL_BLOCK), lambda i: (0, i)),
        out_shape=jax.ShapeDtypeStruct((ATT_H, E), F32), compiler_params=_params(1),
    )(rel_pad, idx)


def rel_bias_grad(dtiles_flat, dclip, idx, *, name):
    E = idx.shape[1]
    n_steps = E // REL_BLOCK

    def body(d_ref, c_ref, i_ref, o_ref):
        @pl.when(pl.program_id(0) == 0)
        def _():
            o_ref[...] = jnp.zeros_like(o_ref)

        o_ref[...] += lax.dot_general(d_ref[...], _one_hot(i_ref[...]), (((1,), (1,)), ((), ())),
                                      precision=HIGHEST, preferred_element_type=F32)

        @pl.when(pl.program_id(0) == n_steps - 1)
        def _():
            at_clip = lax.broadcasted_iota(jnp.int32, (1, REL_PAD), 1) == 2 * MAX_REL
            o_ref[...] += jnp.where(at_clip, jnp.sum(c_ref[...], axis=1, keepdims=True), 0.0)

    return pl.pallas_call(
        body, name=name, grid=(n_steps,),
        in_specs=[pl.BlockSpec((ATT_H, REL_BLOCK), lambda i: (0, i)), pl.BlockSpec((ATT_H, 128), lambda i: (0, 0)),
                  pl.BlockSpec((1, REL_BLOCK), lambda i: (0, i))],
        out_specs=pl.BlockSpec((ATT_H, REL_PAD), lambda i: (0, 0)),
        out_shape=jax.ShapeDtypeStruct((ATT_H, REL_PAD), F32), compiler_params=_params(1),
    )(dtiles_flat, dclip, idx)


def _attn_bias(tiles, clip):
    const = jnp.broadcast_to(clip, (REL_TILE, REL_TILE))
    zero = jnp.zeros((REL_TILE, REL_TILE), F32)
    rows = []
    for qt in range(ATT_QB // REL_TILE):
        blocks = []
        for kt in range(ATT_KB // REL_TILE):
            m = kt - qt
            if m in REL_TILES:
                blocks.append(tiles[REL_TILES.index(m)])
            elif 0 <= m < REL_TILES[0]:
                blocks.append(const)
            else:
                blocks.append(zero)
        rows.append(jnp.concatenate(blocks, axis=1))
    return jnp.concatenate(rows, axis=0)


def _attn_bias_grad(ds, dt_ref, dc_ref, a):
    tile = lambda qt, kt: ds[qt * REL_TILE:(qt + 1) * REL_TILE, kt * REL_TILE:(kt + 1) * REL_TILE]
    const = None
    sums = [None] * len(REL_TILES)
    for qt in range(ATT_QB // REL_TILE):
        for kt in range(ATT_KB // REL_TILE):
            m = kt - qt
            if m in REL_TILES:
                n = REL_TILES.index(m)
                sums[n] = tile(qt, kt) if sums[n] is None else sums[n] + tile(qt, kt)
            elif 0 <= m < REL_TILES[0]:
                const = tile(qt, kt) if const is None else const + tile(qt, kt)
    for n, v in enumerate(sums):
        dt_ref[a, n] += v
    dc_ref[a] += jnp.sum(const, axis=0, keepdims=True)


def _attn_head_lanes():
    lane = lax.broadcasted_iota(jnp.int32, (1, 2 * ATT_HD), 1)
    return [(lane >= a * ATT_HD) & (lane < (a + 1) * ATT_HD) for a in range(2)]


def _attn_band_bias(tiles, clip):
    j = lax.broadcasted_iota(jnp.int32, (ATT_QB, ATT_KB), 1)
    t = lax.broadcasted_iota(jnp.int32, (ATT_QB, ATT_KB), 0)
    shift = CHUNK.bit_length() - 1
    chunks = lax.shift_right_logical(j, shift) - lax.shift_right_logical(t, shift)
    band = (chunks >= 0) & (chunks <= LEFT // CHUNK)
    return jnp.where(band, _attn_bias(tiles, clip), NEG_INF)


def _attn_exp(qa, kb, bias):
    s = _dot_nt(qa, kb) + bias
    e = jnp.exp(s - jnp.max(s, axis=-1, keepdims=True))
    return e, jnp.sum(e, axis=-1, keepdims=True)


def _attn_specs():
    n_hp = ATT_H // 2
    q_spec = pl.BlockSpec((ATT_QB, 128), lambda hp, g: (g, hp))

    def win(col0, back):
        return pl.BlockSpec((ATT_QB, 128), lambda hp, g: (jnp.maximum(g - back, 0), col0 + hp))

    kv_specs = [win(n_hp, 2), win(n_hp, 1), win(n_hp, 0), win(2 * n_hp, 2), win(2 * n_hp, 1), win(2 * n_hp, 0)]
    tiles_spec = pl.BlockSpec((2, len(REL_TILES), REL_TILE, REL_TILE), lambda hp, g: (hp, 0, 0, 0))
    clip_spec = pl.BlockSpec((2, 1, 128), lambda hp, g: (hp, 0, 0))
    return q_spec, kv_specs, tiles_spec, clip_spec


def _attn_window(refs, g):
    kb = jnp.concatenate([_bf(r[...]) for r in refs[0:3]], axis=0)
    vb = jnp.concatenate([_bf(r[...]) for r in refs[3:6]], axis=0)
    j = lax.broadcasted_iota(jnp.int32, (1, ATT_KB), 1)
    return kb, vb, jnp.where(j + (g - 2) * ATT_QB >= 0, 0.0, NEG_INF)


def attn_fwd(qkv, tiles, clip, *, comm=None, name):
    S = qkv.shape[0]
    q_spec, kv_specs, tiles_spec, clip_spec = _attn_specs()

    def body(q_ref, *rest):
        kv_refs, t_ref, c_ref, o_ref, bias = rest[:6], rest[6], rest[7], rest[8], rest[9]
        g = pl.program_id(1)

        kb, vb, key_bias = _attn_window(kv_refs, g)

        @pl.when(g <= 2)
        def _():
            for a in range(2):
                bias[a * ATT_QB:(a + 1) * ATT_QB, :] = _attn_band_bias(t_ref[a], c_ref[a]) + key_bias

        q = q_ref[...].astype(F32)
        out = jnp.zeros((ATT_QB, 2 * ATT_HD), F32)
        for a, lanes in enumerate(_attn_head_lanes()):
            mf = lanes.astype(F32)
            e, l = _attn_exp(_bf(q * (mf * ATT_HD ** -0.5)), kb, bias[a * ATT_QB:(a + 1) * ATT_QB, :])
            out = out + _dot(_bf(e), vb) * (mf * (1.0 / l))
        o_ref[...] = _bf(out)

    n_hp, n_g = ATT_H // 2, S // ATT_QB
    return _call_hosting(
        body, comm, first=lambda: (pl.program_id(0) == 0) & (pl.program_id(1) == 0),
        last=lambda: (pl.program_id(0) == n_hp - 1) & (pl.program_id(1) == n_g - 1),
        mid=lambda: (pl.program_id(0) == n_hp // 2 + 1) & (pl.program_id(1) == 0),
        name=name, grid=(n_hp, n_g), in_specs=[q_spec] + kv_specs + [tiles_spec, clip_spec],
        out_specs=[q_spec], out_shape=[jax.ShapeDtypeStruct((S, D), BF16)],
        scratch_shapes=[pltpu.VMEM((2 * ATT_QB, ATT_KB), F32)], compiler_params=_params(2),
        args=(*([qkv] * 7), tiles, clip))


def attn_bwd(qkv, do, tiles, clip, *, comm=None, name):
    S = qkv.shape[0]
    q_spec, kv_specs, tiles_spec, clip_spec = _attn_specs()
    col_spec = pl.BlockSpec((S, 128), lambda hp, g: (0, hp))
    sum_spec = pl.BlockSpec((1, 128), lambda hp, g: (0, hp))
    n_g = S // ATT_QB

    def body(q_ref, *rest):
        kv_refs, t_ref, c_ref, do_ref = rest[:6], rest[6], rest[7], rest[8]
        dq_ref, dk_ref, dv_ref, dt_ref, dc_ref, sq_ref, sk_ref, sv_ref, bias = rest[9:]
        g = pl.program_id(1)

        @pl.when(g == 0)
        def _():
            dk_ref[...] = jnp.zeros_like(dk_ref)
            dv_ref[...] = jnp.zeros_like(dv_ref)
            dt_ref[...] = jnp.zeros_like(dt_ref)
            dc_ref[...] = jnp.zeros_like(dc_ref)
            sq_ref[...] = jnp.zeros_like(sq_ref)

        kb, vb, key_bias = _attn_window(kv_refs, g)

        @pl.when(g <= 2)
        def _():
            for a in range(2):
                bias[a * ATT_QB:(a + 1) * ATT_QB, :] = _attn_band_bias(t_ref[a], c_ref[a]) + key_bias

        q = q_ref[...].astype(F32)
        do = do_ref[...]
        lanes = _attn_head_lanes()
        mf = [m.astype(F32) * ATT_HD ** -0.5 for m in lanes]
        qs = _bf(jnp.concatenate([q * m for m in mf], axis=0))
        dos = jnp.concatenate([jnp.where(m, do, jnp.zeros_like(do)) for m in lanes], axis=0)
        e, l = _attn_exp(qs, kb, bias[...])
        p = e * (1.0 / l)
        dp = _dot_nt(dos, vb)
        ds = p * (dp - jnp.sum(p * dp, axis=-1, keepdims=True))
        ds_b = _bf(ds)
        dq2 = _dot(ds_b, kb)
        dq = dq2[:ATT_QB] * mf[0] + dq2[ATT_QB:] * mf[1]
        dkw = _dot_tn(ds_b, qs)
        dvw = _dot_tn(_bf(p), dos)
        for a in range(2):
            _attn_bias_grad(ds[a * ATT_QB:(a + 1) * ATT_QB], dt_ref, dc_ref, a)
        dq_ref[...] = _bf(dq)
        sq_ref[...] += jnp.sum(dq, axis=0, keepdims=True)
        for blk in range(3):
            src = g - 2 + blk

            @pl.when(src >= 0)
            def _(blk=blk, src=src):
                rows = pl.ds(pl.multiple_of(src * ATT_QB, ATT_QB), ATT_QB)
                dk_ref[rows, :] += dkw[blk * ATT_QB:(blk + 1) * ATT_QB]
                dv_ref[rows, :] += dvw[blk * ATT_QB:(blk + 1) * ATT_QB]

        @pl.when(g == n_g - 1)
        def _():
            sk_ref[...] = jnp.sum(dk_ref[...], axis=0, keepdims=True)
            sv_ref[...] = jnp.sum(dv_ref[...], axis=0, keepdims=True)

    n_hp, n_g = ATT_H // 2, S // ATT_QB
    return _call_hosting(
        body, comm, first=lambda: (pl.program_id(0) == 0) & (pl.program_id(1) == 0),
        last=lambda: (pl.program_id(0) == n_hp - 1) & (pl.program_id(1) == n_g - 1),
        name=name, grid=(n_hp, n_g),
        in_specs=[q_spec] + kv_specs + [tiles_spec, clip_spec, q_spec],
        out_specs=[q_spec, col_spec, col_spec, tiles_spec, clip_spec] + [sum_spec] * 3,
        out_shape=[jax.ShapeDtypeStruct((S, D), BF16)] + [jax.ShapeDtypeStruct((S, D), F32)] * 2
        + [jax.ShapeDtypeStruct((ATT_H, len(REL_TILES), REL_TILE, REL_TILE), F32),
           jax.ShapeDtypeStruct((ATT_H, 1, 128), F32)] + [jax.ShapeDtypeStruct((1, D), F32)] * 3,
        scratch_shapes=[pltpu.VMEM((2 * ATT_QB, ATT_KB), F32)], compiler_params=_params(2),
        args=(*([qkv] * 7), tiles, clip, do))


def mods_partial(c_all, w_ada, *, name):
    n_l, _, n_c = w_ada.shape

    def body(c_ref, w_ref, o_ref):
        o_ref[...] = _dot(_bf(_silu(c_ref[...])), _bf(w_ref[...]))

    return pl.pallas_call(
        body, name=name, grid=(n_l,),
        in_specs=[pl.BlockSpec((N_DEV, D), lambda l: (0, 0)), pl.BlockSpec((None, D, n_c), lambda l: (l, 0, 0))],
        out_specs=pl.BlockSpec((None, N_DEV, n_c), lambda l: (l, 0, 0)),
        out_shape=jax.ShapeDtypeStruct((n_l, N_DEV, n_c), F32), compiler_params=_params(1),
    )(c_all, w_ada)


def w_ada_grad(c_all, dm, *, name):
    n_l, _, n_c = dm.shape

    def body(c_ref, d_ref, o_ref):
        o_ref[...] = lax.dot_general(_silu(c_ref[...]), d_ref[...], (((0,), (0,)), ((), ())),
                                     precision=HIGHEST, preferred_element_type=F32)

    return pl.pallas_call(
        body, name=name, grid=(n_l,),
        in_specs=[pl.BlockSpec((N_DEV, D), lambda l: (0, 0)), pl.BlockSpec((None, N_DEV, n_c), lambda l: (l, 0, 0))],
        out_specs=pl.BlockSpec((None, D, n_c), lambda l: (l, 0, 0)),
        out_shape=jax.ShapeDtypeStruct((n_l, D, n_c), F32), compiler_params=_params(1),
    )(c_all, dm)


def adamw(w, m, v, gparts, *, block_rows, name):
    R, C = w.shape
    n = gparts.shape[0]

    def body(w_ref, m_ref, v_ref, g_ref, go_ref, d_ref, mo_ref, vo_ref):
        g = g_ref[0].astype(F32)
        for k in range(1, n):
            g = g + g_ref[k].astype(F32)
        m_new = ADAM_B1 * m_ref[...] + (1.0 - ADAM_B1) * g
        v_new = ADAM_B2 * v_ref[...] + (1.0 - ADAM_B2) * (g * g)
        m_hat = m_new / (1.0 - ADAM_B1 ** ADAM_STEP)
        v_hat = v_new / (1.0 - ADAM_B2 ** ADAM_STEP)
        go_ref[...] = g
        d_ref[...] = -ADAM_LR * (m_hat / (jnp.sqrt(v_hat) + ADAM_EPS) + ADAM_WD * w_ref[...])
        mo_ref[...] = m_new
        vo_ref[...] = v_new

    blk = pl.BlockSpec((block_rows, C), lambda i: (i, 0))
    return pl.pallas_call(
        body, name=name, grid=(R // block_rows,),
        in_specs=[blk, blk, blk, pl.BlockSpec((n, block_rows, C), lambda i: (0, i, 0))],
        out_specs=[blk] * 4, out_shape=[jax.ShapeDtypeStruct((R, C), F32)] * 4, compiler_params=_params(1),
    )(w, m, v, gparts)


def adamw_nd(w, m, v, gparts, *, name):
    shape = w.shape
    two = (int(np.prod(shape[:-1])), shape[-1])
    rows = two[0]
    block_rows = rows
    for cand in (512, 256):
        if rows > cand and rows % cand == 0:
            block_rows = cand
            break
    outs = adamw(w.reshape(two), m.reshape(two), v.reshape(two), gparts.reshape((gparts.shape[0],) + two),
                 block_rows=block_rows, name=name)
    return [o.reshape(shape) for o in outs]


def sum_parts(parts, *, name):
    n, R, C = parts.shape

    def body(p_ref, o_ref):
        acc = p_ref[0]
        for k in range(1, n):
            acc = acc + p_ref[k]
        o_ref[...] = acc

    return pl.pallas_call(
        body, name=name, in_specs=[pl.BlockSpec((n, R, C), lambda: (0, 0, 0))],
        out_specs=pl.BlockSpec((R, C), lambda: (0, 0)), out_shape=jax.ShapeDtypeStruct((R, C), F32),
        compiler_params=pltpu.CompilerParams(vmem_limit_bytes=VMEM_LIMIT),
    )(parts)


def _my_place():
    return lax.axis_index("x"), lax.axis_index("y"), lax.axis_index("c")


def _full_shape(kind, shard):
    n_l, rows, cols = shard
    return {"col": (n_l, rows, N_DEV * cols), "row": (n_l, N_DEV * rows, cols), "stk": (N_DEV, n_l, rows, cols)}[kind]


def _slab(ref, kind, dev, shard):
    _, rows, cols = shard
    if kind == "col":
        return ref.at[:, :, pl.ds(pl.multiple_of(dev * cols, 128), cols)]
    if kind == "row":
        return ref.at[:, pl.ds(pl.multiple_of(dev * rows, 8), rows), :]
    return ref.at[dev]


def all_gather(x_shard, *, name):
    m_per, n = x_shard.shape

    def body(x_ref, out_ref, send_sems, recv_sems, local_sem):
        x, y, c = _my_place()
        me, sibling = (x, y, c), (x, y, 1 - c)
        chips = [(1 - x, y), (x, 1 - y), (1 - x, 1 - y)]

        def rows(px, py, pc):
            return out_ref.at[pl.ds((4 * px + 2 * py + pc) * m_per, m_per), :]

        def copy(k, block, to, src=None):
            return pltpu.make_async_remote_copy(
                src_ref=rows(*block) if src is None else src, dst_ref=rows(*block),
                send_sem=send_sems.at[k], recv_sem=recv_sems.at[k], device_id=to, device_id_type=MESH)

        mine = pltpu.make_async_copy(x_ref, rows(*me), local_sem)
        mine.start()
        first = [copy(0, me, sibling, src=x_ref)]
        first += [copy(1 + j, me, (*chip, c), src=x_ref) for j, chip in enumerate(chips)]
        for cp in first:
            cp.start()
        passed = [copy(4 + j, (*chip, c), sibling) for j, chip in enumerate(chips)]
        for j, chip in enumerate(chips):
            copy(1 + j, (*chip, c), me).wait_recv()
            passed[j].start()
        copy(0, sibling, me).wait_recv()
        for j, chip in enumerate(chips):
            copy(4 + j, (*chip, 1 - c), me).wait_recv()
        for cp in first + passed:
            cp.wait_send()
        mine.wait()

    return pl.pallas_call(
        body, name=name, out_shape=jax.ShapeDtypeStruct((N_DEV * m_per, n), x_shard.dtype),
        in_specs=[pl.BlockSpec(memory_space=pltpu.VMEM)], out_specs=pl.BlockSpec(memory_space=pltpu.VMEM),
        scratch_shapes=[pltpu.SemaphoreType.DMA((7,)), pltpu.SemaphoreType.DMA((7,)), pltpu.SemaphoreType.DMA],
        compiler_params=pltpu.CompilerParams(vmem_limit_bytes=VMEM_LIMIT),
    )(x_shard)


def gather_plan(shards, kinds, layers):
    n_t = len(shards)
    shapes = [(1,) + tuple(s.shape[1:]) for s in shards]

    def copies(x_refs, out_refs, sems):
        send_sems, recv_sems, local_sems = sems
        x, y, c = _my_place()
        me, sibling = (x, y, c), (x, y, 1 - c)
        chips = [(1 - x, y), (x, 1 - y), (1 - x, 1 - y)]
        own = [x_refs[t].at[pl.ds(layers[t], 1)] for t in range(n_t)]

        def slab(t, px, py, pc):
            return _slab(out_refs[t], kinds[t], 4 * px + 2 * py + pc, shapes[t])

        def copy(t, k, block, to, src=None):
            return pltpu.make_async_remote_copy(
                src_ref=slab(t, *block) if src is None else src, dst_ref=slab(t, *block),
                send_sem=send_sems.at[7 * t + k], recv_sem=recv_sems.at[7 * t + k], device_id=to,
                device_id_type=MESH)

        mine = [pltpu.make_async_copy(own[t], slab(t, *me), local_sems.at[t]) for t in range(n_t)]
        sends = []
        for t in range(n_t):
            sends.append(copy(t, 0, me, sibling, src=own[t]))
            sends += [copy(t, 1 + j, me, (*chip, c), src=own[t]) for j, chip in enumerate(chips)]
        return mine, sends, copy, me, sibling, chips, c

    def first(x_refs, out_refs, sems):
        mine, sends = copies(x_refs, out_refs, sems)[:2]
        for cp in mine + sends:
            cp.start()

    def forward(x_refs, out_refs, sems):
        _, _, copy, me, sibling, chips, c = copies(x_refs, out_refs, sems)
        for j, chip in enumerate(chips):
            for t in range(n_t):
                copy(t, 1 + j, (*chip, c), me).wait_recv()
                copy(t, 4 + j, (*chip, c), sibling).start()

    def finish(x_refs, out_refs, sems):
        mine, sends, copy, me, sibling, chips, c = copies(x_refs, out_refs, sems)
        passed = [copy(t, 4 + j, (*chip, c), sibling) for j, chip in enumerate(chips) for t in range(n_t)]
        for t in range(n_t):
            copy(t, 0, sibling, me).wait_recv()
        for j, chip in enumerate(chips):
            for t in range(n_t):
                copy(t, 4 + j, (*chip, 1 - c), me).wait_recv()
        for cp in sends + passed:
            cp.wait_send()
        for cp in mine:
            cp.wait()

    def last(x_refs, out_refs, sems):
        forward(x_refs, out_refs, sems)
        finish(x_refs, out_refs, sems)

    return Hosted(
        list(shards), [jax.ShapeDtypeStruct(_full_shape(k, shp), s.dtype) for k, shp, s in zip(kinds, shapes, shards)],
        [pltpu.SemaphoreType.DMA((7 * n_t,)), pltpu.SemaphoreType.DMA((7 * n_t,)), pltpu.SemaphoreType.DMA((n_t,))],
        first, last, mid=forward, last_after_mid=finish)


def scatter_plan(grads, kinds, shapes):
    n_t = len(grads)

    def copies(g_refs, out_refs, sems):
        send_sems, recv_sems, local_sems = sems
        x, y, c = _my_place()
        me = 4 * x + 2 * y + c
        local = [pltpu.make_async_copy(_slab(g_refs[t], kinds[t], me, shapes[t]), out_refs[t].at[me],
                                       local_sems.at[t]) for t in range(n_t)]
        remote = []
        for t in range(n_t):
            for r in range(1, N_DEV):
                px = 1 - x if r & 4 else x
                py = 1 - y if r & 2 else y
                pc = 1 - c if r & 1 else c
                remote.append(pltpu.make_async_remote_copy(
                    src_ref=_slab(g_refs[t], kinds[t], 4 * px + 2 * py + pc, shapes[t]), dst_ref=out_refs[t].at[me],
                    send_sem=send_sems.at[7 * t + r - 1], recv_sem=recv_sems.at[7 * t + r - 1],
                    device_id=(px, py, pc), device_id_type=MESH))
        return local, remote

    def first(g_refs, out_refs, sems):
        local, remote = copies(g_refs, out_refs, sems)
        for cp in local + remote:
            cp.start()

    def last(g_refs, out_refs, sems):
        local, remote = copies(g_refs, out_refs, sems)
        for cp in remote + local:
            cp.wait()

    return Hosted(
        list(grads), [jax.ShapeDtypeStruct((N_DEV,) + tuple(s), BF16) for s in shapes],
        [pltpu.SemaphoreType.DMA((7 * n_t,)), pltpu.SemaphoreType.DMA((7 * n_t,)), pltpu.SemaphoreType.DMA((n_t,))],
        first, last)


def adamw_layer(w, m, v, parts, layer, bufs, *, name):
    n_l, rows, cols = w.shape
    n = parts.shape[0]
    tr = min(rows, 256)

    def body(w_ref, m_ref, v_ref, g_ref, *rest):
        go_ref, d_ref, mo_ref, vo_ref = rest[-4:]
        g = g_ref[0].astype(F32)
        for k in range(1, n):
            g = g + g_ref[k].astype(F32)
        m_new = ADAM_B1 * m_ref[...] + (1.0 - ADAM_B1) * g
        v_new = ADAM_B2 * v_ref[...] + (1.0 - ADAM_B2) * (g * g)
        m_hat = m_new / (1.0 - ADAM_B1 ** ADAM_STEP)
        v_hat = v_new / (1.0 - ADAM_B2 ** ADAM_STEP)
        go_ref[...] = g
        d_ref[...] = -ADAM_LR * (m_hat / (jnp.sqrt(v_hat) + ADAM_EPS) + ADAM_WD * w_ref[...])
        mo_ref[...] = m_new
        vo_ref[...] = v_new

    blk = pl.BlockSpec((None, tr, cols), lambda i: (layer, i, 0))
    in_specs = [blk, blk, blk, pl.BlockSpec((n, None, tr, cols), lambda i: (0, 0, i, 0))]
    args = [w, m, v, parts]
    aliases = {}
    if bufs is not None:
        in_specs += [pl.BlockSpec(memory_space=pl.ANY)] * 4
        args += list(bufs)
        aliases = {4 + k: k for k in range(4)}
    return pl.pallas_call(
        body, name=name, grid=(rows // tr,), in_specs=in_specs, out_specs=[blk] * 4,
        out_shape=[jax.ShapeDtypeStruct((n_l, rows, cols), F32)] * 4, input_output_aliases=aliases,
        compiler_params=_params(1),
    )(*args)


BIG =("gla_w_in", "gla_w_out", "att_w_in", "att_w_out", "ff_w1", "ff_w2")
KIND = {"gla_w_in": "stk", "gla_w_out": "row", "att_w_in": "col", "att_w_out": "row", "ff_w1": "col", "ff_w2": "row"}


def _pack_small(arrs):
    parts = []
    for a in arrs:
        f = a.reshape(-1)
        parts.append(jnp.pad(f, (0, -f.shape[0] % 128)))
    flat = jnp.concatenate(parts)
    flat = jnp.pad(flat, (0, -flat.shape[0] % 1024))
    return flat.reshape(-1, 128)


def _unpack_small(packed, shapes):
    flat = packed.reshape(packed.shape[:-2] + (-1,))
    out, off = [], 0
    for shp in shapes:
        n = int(np.prod(shp))
        out.append(flat[..., off:off + n].reshape(packed.shape[:-2] + tuple(shp)))
        off += n + (-n % 128)
    return out


def _vec(a):
    return a.reshape(1, -1)


def _layer_weights(i):
    mixer = "gla" if i % 2 == 0 else "att"
    return [(f"{mixer}_w_in", i // 2), (f"{mixer}_w_out", i // 2), ("ff_w1", i), ("ff_w2", i)]


def _trunk(x, target, mods, sm, w, m, v):
    shard_bf = {n: w[n].astype(BF16) for n in BIG}

    def gather_of(names):
        return gather_plan([shard_bf[n] for n, _ in names], [KIND[n] for n, _ in names], [l for _, l in names])

    def gather_under_core(i):
        names = _layer_weights(i)[2:] + (_layer_weights(i + 1)[:2] if i + 1 < DEPTH else [])
        return names, gather_of(names)

    wts = {}

    def keep_gathered(names, arrays):
        for (n, l), a in zip(names, arrays):
            if KIND[n] == "stk":
                a = a.transpose(1, 2, 0, 3).reshape(1, D, GLA_IN)
                a = jnp.pad(a, ((0, 0), (0, 0), (0, GLA_INP - GLA_IN)))
            wts[n, l] = a

    first_names, rest_names = _layer_weights(0)[:1], _layer_weights(0)[1:2]
    keep_gathered(first_names, run_hosted(gather_of(first_names), name="gather_first"))

    rel_idx = _rel_index()
    saved = []
    for i in range(DEPTH):
        sh1, sc1, g1, sh2, sc2, g2 = [mods[i, k:k + 1] for k in range(6)]
        rec = {"x0": x}
        j = i // 2
        nxt_names, nxt_plan = gather_under_core(i)
        if i % 2 == 0:
            w2p = jnp.pad(sm["gla_w_gk2"][j], ((0, 128 - GLA_RANK), (0, 0)))
            bgk, gn = _vec(sm["gla_b_gk"][j]), _vec(sm["gla_g_norm"][j])
            if i == 0:
                proj, got0 = mm_nn(x, wts["gla_w_in", j], 0, pro="mod", p1=sc1, p2=sh1, tm=512, tn=GLA_INP,
                                   comm=gather_of(rest_names), name=f"gla_proj_{i}")
                keep_gathered(rest_names, got0)
            else:
                proj = mm_nn(x, wts["gla_w_in", j], 0, pro="mod", p1=sc1, p2=sh1, tm=512, tn=GLA_INP,
                             name=f"gla_proj_{i}")
            (og, states), got = gla_fwd(proj, w2p, bgk, gn, comm=nxt_plan, name=f"gla_core_{i}")
            y = mm_nn(og, wts["gla_w_out", j], 0, tm=1024, tn=1024, name=f"gla_out_{i}")
            rec.update(kind="gla", j=j, w2p=w2p, bgk=bgk, gn=gn, proj=proj, og=og, states=states)
        else:
            rel = sm["att_rel_bias"][j]
            rel_pad = jnp.pad(rel, ((0, 0), (0, REL_PAD - N_REL)), constant_values=NEG_INF)
            tiles = rel_bias_tiles(rel_pad, rel_idx, name=f"att_bias_{i}")
            tiles = tiles.reshape(ATT_H, len(REL_TILES), REL_TILE, REL_TILE)
            clip = jnp.broadcast_to(rel[:, 2 * MAX_REL][:, None, None], (ATT_H, 1, 128))
            qkv = mm_nn(x, wts["att_w_in", j], 0, pro="mod", p1=sc1, p2=sh1, bias=_vec(sm["att_b_in"][j]),
                        out_dtype=BF16, tm=512, tn=3 * D, name=f"att_proj_{i}")
            (o,), got = attn_fwd(qkv, tiles, clip, comm=nxt_plan, name=f"att_core_{i}")
            y = mm_nn(o, wts["att_w_out", j], 0, tm=1024, tn=1024, name=f"att_out_{i}")
            rec.update(kind="att", j=j, tiles=tiles, clip=clip, qkv=qkv, o=o)
        keep_gathered(nxt_names, got)
        x1 = ln_fwd(x, y, g1, _vec(sm["ln_g"][i, 0]), _vec(sm["ln_b"][i, 0]), name=f"ln_mix_{i}")
        h = mm_nn(x1, wts["ff_w1", i], 0, pro="mod", p1=sc2, p2=sh2, out_dtype=BF16, tm=512, tn=D_FF,
                  name=f"ff_up_{i}")
        y2 = mm_nn(h, wts["ff_w2", i], 0, pro="relu2", tm=512, tn=D, name=f"ff_down_{i}")
        rec.update(y=y, x1=x1, h=h, y2=y2)
        saved.append(rec)
        if i + 1 < DEPTH:
            x = ln_fwd(x1, y2, g2, _vec(sm["ln_g"][i, 1]), _vec(sm["ln_b"][i, 1]), name=f"ln_ff_{i}")

    gw = {}

    def wgrad(weight, layer, a, d, *, tn, tk=1024, tm=512, col_block0=0, **kw):
        gw[weight, layer] = mm_tn(a, d, tk=tk, tn=tn, tm=tm, out_buf=gw.get((weight, layer)),
                                  out_shape=wts[weight, layer].shape, col_block0=col_block0, **kw)

    def scatter_layer(units):
        grads = []
        for n, l in units:
            g = gw[n, l]
            if KIND[n] == "stk":
                g = g[:, :, :GLA_IN].reshape(1, D, N_DEV, GLA_IN // N_DEV).transpose(2, 0, 1, 3)
            grads.append(g)
        return scatter_plan(grads, [KIND[n] for n, _ in units], [(1,) + tuple(w[n].shape[1:]) for n, _ in units])

    results = {n: None for n in BIG}

    def update(units, parts):
        for (n, l), p in zip(units, parts):
            results[n] = adamw_layer(w[n], m[n], v[n], p, l, results[n], name=f"adamw_{n}_{l}")

    gs = {"ln_g": [[None, None] for _ in range(DEPTH)], "ln_b": [[None, None] for _ in range(DEPTH)],
          "gla_w_gk2": [None] * 2, "gla_b_gk": [None] * 2, "gla_g_norm": [None] * 2, "att_b_in": [None] * 2,
          "att_rel_bias": [None] * 2}
    dmods = [[None] * 6 for _ in range(DEPTH)]
    nxt = None
    nxt_slot = None
    for i in reversed(range(DEPTH)):
        rec = saved[i]
        sh1, sc1, g1, sh2, sc2, g2 = [mods[i, k:k + 1] for k in range(6)]
        x0, x1 = rec["x0"], rec["x1"]
        if nxt is None:
            dz2, acc, dzs2, loss = ln_bwd(x1, rec["y2"], g2, _vec(sm["ln_g"][i, 1]),
                                          loss=(target, _vec(sm["ln_b"][i, 1])), name=f"ln_ff_bwd_{i}")
        else:
            dz2, acc, dzs2 = ln_bwd(x1, rec["y2"], g2, _vec(sm["ln_g"][i, 1]),
                                    nxt=nxt[:3] + (_vec(sm["ln_b"][i, 1]),), name=f"ln_ff_bwd_{i}")
            dmods[nxt_slot[0]][nxt_slot[1]] = acc[3]
            dmods[nxt_slot[0]][nxt_slot[2]] = acc[4]
        gs["ln_g"][i][1], gs["ln_b"][i][1], dmods[i][5] = acc[0], acc[1], acc[2]
        dh = mm_nt([dzs2], wts["ff_w2", i], 0, epi_h=rec["h"], out_dtype=BF16, tm=1024, tn=1024,
                   name=f"ff_down_bwd_{i}")
        wgrad("ff_w2", i, rec["h"], dzs2, pro="relu2", tk=2048, tn=1024, name=f"ff_w2_grad_{i}")
        du2 = mm_nt([dh], wts["ff_w1", i], 0, tm=1024, tn=1024, name=f"ff_up_bwd_{i}")
        wgrad("ff_w1", i, x1, dh, pro="mod", p1=sc2, p2=sh2, tn=2048, name=f"ff_w1_grad_{i}")
        dz1, acc, dzs1 = ln_bwd(x0, rec["y"], g1, _vec(sm["ln_g"][i, 0]),
                                nxt=(dz2, du2, sc2, _vec(sm["ln_b"][i, 0])), name=f"ln_mix_bwd_{i}")
        dmods[i][4], dmods[i][3] = acc[3], acc[4]
        gs["ln_g"][i][0], gs["ln_b"][i][0], dmods[i][2] = acc[0], acc[1], acc[2]
        j = rec["j"]
        w_in, w_out = _layer_weights(i)[:2]
        if rec["kind"] == "gla":
            dog = mm_nt([dzs1], wts[w_out], 0, tm=1024, tn=1024, name=f"gla_out_bwd_{i}")
            wgrad(*w_out, rec["og"], dzs1, tn=1024, name=f"gla_wout_grad_{i}")
        else:
            do = mm_nt([dzs1], wts[w_out], 0, out_dtype=BF16, tm=1024, tn=1024, name=f"att_out_bwd_{i}")
            wgrad(*w_out, rec["o"], dzs1, tn=1024, name=f"att_wout_grad_{i}")
        units = [("ff_w1", i), ("ff_w2", i), w_out] + ([_layer_weights(i + 1)[0]] if i + 1 < DEPTH else [])
        plan = scatter_layer(units)
        if rec["kind"] == "gla":
            (dproj, dw2p, dbgk, dgn), parts = gla_bwd(rec["proj"], dog, rec["states"], rec["w2p"], rec["bgk"],
                                                      rec["gn"], comm=plan, name=f"gla_core_bwd_{i}")
            gs["gla_w_gk2"][j], gs["gla_b_gk"][j], gs["gla_g_norm"][j] = dw2p[:GLA_RANK], dbgk[0], dgn[0]
            wgrad(*w_in, x0, dproj, pro="mod", p1=sc1, p2=sh1, tk=512, tn=GLA_INP, name=f"gla_win_grad_{i}")
            if i == 0:
                du1, last_parts = mm_nt([dproj], wts[w_in], 0, tm=1024, tn=1024, comm=scatter_layer([w_in]),
                                        name=f"gla_proj_bwd_{i}")
                update([w_in], last_parts)
            else:
                du1 = mm_nt([dproj], wts[w_in], 0, tm=1024, tn=1024, name=f"gla_proj_bwd_{i}")
        else:
            (dq, dk, dv, dtiles, dclip, sq, sk, sv), parts = attn_bwd(rec["qkv"], do, rec["tiles"], rec["clip"],
                                                                      comm=plan, name=f"att_core_bwd_{i}")
            drel = rel_bias_grad(dtiles.reshape(ATT_H, -1), dclip.reshape(ATT_H, 128), rel_idx,
                                 name=f"att_bias_grad_{i}")
            gs["att_rel_bias"][j] = drel[:, :N_REL]
            gs["att_b_in"][j] = jnp.concatenate([sq[0], sk[0], sv[0]])
            du1 = mm_nt([dq, dk, dv], wts[w_in], 0, tm=512, tn=1024, name=f"att_proj_bwd_{i}")
            for n, t in enumerate((dq, dk, dv)):
                wgrad(*w_in, x0, t, pro="mod", p1=sc1, p2=sh1, tn=1024, col_block0=n, name=f"att_win_grad_{i}_{n}")
        update(units, parts)
        nxt = (dz1, du1, sc1, x0)
        nxt_slot = (i, 1, 0)
    dx, acc = combine_final(nxt[0], nxt[1], nxt[2], nxt[3], name="grad_x")
    dmods[0][1], dmods[0][0] = acc[3], acc[4]
    dmods = jnp.stack([jnp.stack(r) for r in dmods])
    gs = {k: jnp.stack([jnp.stack(r) if isinstance(r, list) else r for r in v]) for k, v in gs.items()}
    return loss, dx, dmods, gs, results


WEIGHTS = ("w_ada", "b_ada", "ln_g", "ln_b", "gla_w_in", "gla_w_gk2", "gla_b_gk", "gla_g_norm", "gla_w_out",
           "att_w_in", "att_b_in", "att_rel_bias", "att_w_out", "ff_w1", "ff_w2")
SMALL_SHARDED = {"ln_g": 2, "ln_b": 2, "gla_w_gk2": 2, "gla_g_norm": 2, "att_b_in": 1}
SMALL_GRADS = ("ln_g", "ln_b", "gla_w_gk2", "gla_b_gk", "gla_g_norm", "att_b_in", "att_rel_bias")


def kernel(x, c, w_ada, b_ada, ln_g, ln_b, gla_w_in, gla_w_gk2, gla_b_gk, gla_g_norm, gla_w_out, att_w_in, att_b_in, att_rel_bias, att_w_out, ff_w1, ff_w2, loss_target, m_w_ada, m_b_ada, m_ln_g, m_ln_b, m_gla_w_in, m_gla_w_gk2, m_gla_b_gk, m_gla_g_norm, m_gla_w_out, m_att_w_in, m_att_b_in, m_att_rel_bias, m_att_w_out, m_ff_w1, m_ff_w2, v_w_ada, v_b_ada, v_ln_g, v_ln_b, v_gla_w_in, v_gla_w_gk2, v_gla_b_gk, v_gla_g_norm, v_gla_w_out, v_att_w_in, v_att_b_in, v_att_rel_bias, v_att_w_out, v_ff_w1, v_ff_w2):
    w = dict(w_ada=w_ada, b_ada=b_ada, ln_g=ln_g, ln_b=ln_b, gla_w_in=gla_w_in, gla_w_gk2=gla_w_gk2,
             gla_b_gk=gla_b_gk, gla_g_norm=gla_g_norm, gla_w_out=gla_w_out, att_w_in=att_w_in, att_b_in=att_b_in,
             att_rel_bias=att_rel_bias, att_w_out=att_w_out, ff_w1=ff_w1, ff_w2=ff_w2)
    m = dict(w_ada=m_w_ada, b_ada=m_b_ada, ln_g=m_ln_g, ln_b=m_ln_b, gla_w_in=m_gla_w_in, gla_w_gk2=m_gla_w_gk2,
             gla_b_gk=m_gla_b_gk, gla_g_norm=m_gla_g_norm, gla_w_out=m_gla_w_out, att_w_in=m_att_w_in,
             att_b_in=m_att_b_in, att_rel_bias=m_att_rel_bias, att_w_out=m_att_w_out, ff_w1=m_ff_w1, ff_w2=m_ff_w2)
    v = dict(w_ada=v_w_ada, b_ada=v_b_ada, ln_g=v_ln_g, ln_b=v_ln_b, gla_w_in=v_gla_w_in, gla_w_gk2=v_gla_w_gk2,
             gla_b_gk=v_gla_b_gk, gla_g_norm=v_gla_g_norm, gla_w_out=v_gla_w_out, att_w_in=v_att_w_in,
             att_b_in=v_att_b_in, att_rel_bias=v_att_rel_bias, att_w_out=v_att_w_out, ff_w1=v_ff_w1, ff_w2=v_ff_w2)
    xi, yi, ci = _my_place()
    me = 4 * xi + 2 * yi + ci

    small_names = tuple(SMALL_SHARDED)
    small_in = _pack_small([c] + [w[n] for n in small_names])
    small_all = all_gather(small_in, name="gather_small").reshape(N_DEV, -1, 128)
    parts = _unpack_small(small_all, [c.shape] + [w[n].shape for n in small_names])
    c_all = parts[0].reshape(N_DEV, D)
    sm = {"gla_b_gk": gla_b_gk, "att_rel_bias": att_rel_bias}
    for n, p in zip(small_names, parts[1:]):
        ax = SMALL_SHARDED[n]
        sm[n] = jnp.moveaxis(p, 0, ax).reshape(p.shape[1:ax + 1] + (N_DEV * p.shape[ax + 1],) + p.shape[ax + 2:])

    n_ada = w_ada.shape[2]
    mp = mods_partial(c_all, w_ada, name="mods_partial")
    mp_all = all_gather(mp.reshape(DEPTH * N_DEV, n_ada), name="gather_mods")
    mp_all = mp_all.reshape(N_DEV, DEPTH, N_DEV, n_ada)
    mods = lax.dynamic_index_in_dim(mp_all, me, axis=2, keepdims=False)
    mods = mods.transpose(1, 0, 2).reshape(DEPTH, 6 * D) + b_ada
    mods = mods.reshape(DEPTH, 6, D)

    loss, dx, dmods, gs, results = _trunk(x.reshape(x.shape[1:]), loss_target.reshape(x.shape[1:]), mods, sm, w, m, v)
    loss = lax.psum(loss[0, 0], ("x", "y", "c"))

    dm_flat = dmods.reshape(DEPTH, 6 * D)
    small_g = [dm_flat] + [gs[n].reshape(sm[n].shape) for n in SMALL_GRADS]
    small_shapes = [a.shape for a in small_g]
    sg_all = all_gather(_pack_small(small_g), name="gather_small_grads").reshape(N_DEV, -1, 128)
    summed = _unpack_small(sum_parts(sg_all, name="sum_small_grads"), small_shapes)
    g_full = dict(zip(("b_ada",) + SMALL_GRADS, summed))
    dm_all = _unpack_small(sg_all, small_shapes)[0]
    dm_mine = lax.dynamic_slice_in_dim(dm_all, me * n_ada, n_ada, axis=2).transpose(1, 0, 2)
    g_w_ada = w_ada_grad(c_all, dm_mine, name="w_ada_grad")

    results["w_ada"] = adamw_nd(w_ada, m_w_ada, v_w_ada, g_w_ada[None], name="adamw_w_ada")
    for n in ("b_ada",) + SMALL_GRADS:
        g = g_full[n]
        if n in SMALL_SHARDED:
            ax = SMALL_SHARDED[n]
            width = w[n].shape[ax]
            g = lax.dynamic_slice_in_dim(g, me * width, width, axis=ax)
        results[n] = adamw_nd(w[n], m[n], v[n], g[None], name=f"adamw_{n}")

    out = [loss, dx[None]]
    for k in range(4):
        out += [results[n][k] for n in WEIGHTS]
    return tuple(out)
```

```python
import numpy as np
import jax
import jax.numpy as jnp
from jax import lax
from jax.experimental import pallas as pl
from jax.experimental.pallas import tpu as pltpu

F32 = jnp.float32
BF16 = jnp.bfloat16
HIGHEST = lax.Precision.HIGHEST
MESH = pl.DeviceIdType.MESH

N_DEV = 8
D = 1024
DEPTH = 4
CHUNK = 64
ALPHA = (2.0 * DEPTH) ** 0.25
LN_EPS = 1e-5
RMS_EPS = 1e-6
NEG_INF = -1e30

GLA_H = 4
GLA_DKH = 128
GLA_DVH = 256
GLA_DK = GLA_H * GLA_DKH
GLA_DV = GLA_H * GLA_DVH
GLA_RANK = 16
GLA_IN = 2 * GLA_DK + 2 * GLA_DV + GLA_RANK
GLA_INP = 3200
GLA_TAU_INV = 1.0 / 16.0
GLA_SUB = 1

ATT_H = 16
ATT_HD = 64
ATT_QB = 256
ATT_KB = 3 * ATT_QB
LEFT = 8 * CHUNK
MAX_REL = 128
N_REL = 2 * MAX_REL + 1
REL_PAD = 384
REL_TILE = 128
REL_TILES = (3, 4)
D_FF = 4 * D

ADAM_LR = 0.001
ADAM_B1 = 0.9
ADAM_B2 = 0.999
ADAM_EPS = 1e-08
ADAM_WD = 0.01
ADAM_STEP = 10

VMEM_LIMIT = 48 * 1024 * 1024


def _params(n_axes):
    return pltpu.CompilerParams(dimension_semantics=("arbitrary",) * n_axes, vmem_limit_bytes=VMEM_LIMIT)


def _dot(a, b):
    return jnp.dot(a, b, preferred_element_type=F32)


def _dot_nt(a, b):
    return lax.dot_general(a, b, (((1,), (1,)), ((), ())), preferred_element_type=F32)


def _dot_tn(a, b):
    return lax.dot_general(a, b, (((0,), (0,)), ((), ())), preferred_element_type=F32)


def _bf(a):
    return a.astype(BF16)


def _prologue(kind, a, p1=None, p2=None):
    if kind == "mod":
        return a.astype(F32) * (1.0 + p1) + p2
    if kind == "relu2":
        r = jnp.maximum(a, 0.0)
        return r * r
    return a


class Hosted:
    def __init__(self, inputs, out_shapes, sems, first, last, mid=None, last_after_mid=None):
        self.inputs, self.out_shapes, self.sems, self.first, self.last = inputs, out_shapes, sems, first, last
        self.mid, self.last_after_mid = mid, last_after_mid


def _hbm_specs(n):
    return [pl.BlockSpec(memory_space=pltpu.HBM)] * n


def _call_hosting(body, comm, *, first, last, in_specs, out_specs, out_shape, scratch_shapes, args, mid=None, **kw):
    if comm is None:
        return pl.pallas_call(body, in_specs=in_specs, out_specs=out_specs, out_shape=out_shape,
                              scratch_shapes=scratch_shapes, **kw)(*args), []
    n_in, n_out, n_scr = len(in_specs), len(out_specs), len(scratch_shapes)
    n_ci, n_co = len(comm.inputs), len(comm.out_shapes)

    def hosting(*refs):
        ins, ci = refs[:n_in], refs[n_in:n_in + n_ci]
        k = n_in + n_ci
        outs, co = refs[k:k + n_out], refs[k + n_out:k + n_out + n_co]
        k += n_out + n_co
        scr, cs = refs[k:k + n_scr], refs[k + n_scr:]

        @pl.when(first())
        def _():
            comm.first(ci, co, cs)

        body(*ins, *outs, *scr)
        split = mid is not None and comm.mid is not None
        if split:
            @pl.when(mid())
            def _():
                comm.mid(ci, co, cs)

        @pl.when(last())
        def _():
            (comm.last_after_mid if split else comm.last)(ci, co, cs)

    res = pl.pallas_call(
        hosting, in_specs=list(in_specs) + _hbm_specs(n_ci), out_specs=list(out_specs) + _hbm_specs(n_co),
        out_shape=list(out_shape) + list(comm.out_shapes), scratch_shapes=list(scratch_shapes) + list(comm.sems),
        **kw)(*args, *comm.inputs)
    return res[:n_out], res[n_out:]


def run_hosted(comm, *, name):
    n_i, n_o = len(comm.inputs), len(comm.out_shapes)

    def body(*refs):
        ins, outs, sems = refs[:n_i], refs[n_i:n_i + n_o], refs[n_i + n_o:]
        comm.first(ins, outs, sems)
        comm.last(ins, outs, sems)

    return pl.pallas_call(body, name=name, out_shape=list(comm.out_shapes), in_specs=_hbm_specs(n_i),
                          out_specs=_hbm_specs(n_o), scratch_shapes=list(comm.sems))(*comm.inputs)


def mm_nn(a, b, layer, *, pro=None, p1=None, p2=None, bias=None, out_dtype=F32, tm, tn, comm=None, name):
    M, K = a.shape
    N = b.shape[2]
    tm = min(tm, M)
    n_p = {"mod": 2}.get(pro, 0)
    has_bias = bias is not None
    direct = pro is None and a.dtype == BF16

    def body(*refs):
        a_ref, b_ref = refs[0], refs[1]
        p_refs = refs[2:2 + n_p]
        bias_ref = refs[2 + n_p] if has_bias else None
        if direct:
            o_ref = refs[-1]
            lhs = a_ref[...]
        else:
            o_ref, abf = refs[-2], refs[-1]

            @pl.when(pl.program_id(1) == 0)
            def _():
                abf[...] = _bf(_prologue(pro, a_ref[...], *[r[...] for r in p_refs]))

            lhs = abf[...]
        acc = _dot(lhs, b_ref[...])
        if has_bias:
            acc = acc + bias_ref[...]
        o_ref[...] = acc.astype(out_dtype)

    in_specs = [pl.BlockSpec((tm, K), lambda i, j: (i, 0)), pl.BlockSpec((None, K, tn), lambda i, j: (layer, 0, j))]
    args = [a, b]
    for p in (p1, p2)[:n_p]:
        in_specs.append(pl.BlockSpec((1, K), lambda i, j: (0, 0)))
        args.append(p)
    if has_bias:
        in_specs.append(pl.BlockSpec((1, tn), lambda i, j: (0, j)))
        args.append(bias)
    n_i, n_j = M // tm, N // tn
    (out,), got = _call_hosting(
        body, comm, first=lambda: (pl.program_id(0) == 0) & (pl.program_id(1) == 0),
        last=lambda: (pl.program_id(0) == n_i - 1) & (pl.program_id(1) == n_j - 1),
        name=name, grid=(n_i, n_j), in_specs=in_specs,
        out_specs=[pl.BlockSpec((tm, tn), lambda i, j: (i, j))],
        out_shape=[jax.ShapeDtypeStruct((M, N), out_dtype)],
        scratch_shapes=[] if direct else [pltpu.VMEM((tm, K), BF16)], compiler_params=_params(2), args=args)
    return out if comm is None else (out, got)


def mm_nt(a_parts, w, layer, *, epi_h=None, out_dtype=F32, tm, tn, comm=None, name):
    M = a_parts[0].shape[0]
    tm = min(tm, M)
    widths = [p.shape[1] for p in a_parts]
    Nw = sum(widths)
    Kw = w.shape[1]
    n_a = len(a_parts)
    has_h = epi_h is not None
    direct = n_a == 1 and a_parts[0].dtype == BF16

    def body(*refs):
        a_refs = refs[:n_a]
        w_ref = refs[n_a]
        h_ref = refs[n_a + 1] if has_h else None
        if direct:
            o_ref = refs[-1]
            lhs = a_refs[0][...]
        else:
            o_ref, abf = refs[-2], refs[-1]

            @pl.when(pl.program_id(1) == 0)
            def _():
                off = 0
                for r, wd in zip(a_refs, widths):
                    abf[:, off:off + wd] = _bf(r[...])
                    off += wd

            lhs = abf[...]
        acc = _dot_nt(lhs, w_ref[...])
        if has_h:
            acc = acc * (2.0 * jnp.maximum(h_ref[...], 0.0))
        o_ref[...] = acc.astype(out_dtype)

    in_specs = [pl.BlockSpec((tm, wd), lambda i, j: (i, 0)) for wd in widths]
    in_specs.append(pl.BlockSpec((None, tn, Nw), lambda i, j: (layer, j, 0)))
    args = list(a_parts) + [w]
    if has_h:
        in_specs.append(pl.BlockSpec((tm, tn), lambda i, j: (i, j)))
        args.append(epi_h)
    n_i, n_j = M // tm, Kw // tn
    (out,), got = _call_hosting(
        body, comm, first=lambda: (pl.program_id(0) == 0) & (pl.program_id(1) == 0),
        last=lambda: (pl.program_id(0) == n_i - 1) & (pl.program_id(1) == n_j - 1),
        name=name, grid=(n_i, n_j), in_specs=in_specs,
        out_specs=[pl.BlockSpec((tm, tn), lambda i, j: (i, j))],
        out_shape=[jax.ShapeDtypeStruct((M, Kw), out_dtype)],
        scratch_shapes=[] if direct else [pltpu.VMEM((tm, Nw), BF16)], compiler_params=_params(2), args=args)
    return out if comm is None else (out, got)


def mm_tn(a, d, *, pro=None, p1=None, p2=None, tk, tn, tm, out_buf, out_shape, col_block0=0, name):
    M, Kf = a.shape
    N = d.shape[1]
    n_p = {"mod": 2}.get(pro, 0)
    has_buf = out_buf is not None
    n_m = M // tm

    def body(*refs):
        a_ref, d_ref = refs[0], refs[1]
        p_refs = refs[2:2 + n_p]
        o_ref, acc = refs[-2], refs[-1]
        m = pl.program_id(2)

        @pl.when(m == 0)
        def _():
            acc[...] = jnp.zeros_like(acc)

        av = _prologue(pro, a_ref[...], *[r[...] for r in p_refs])
        acc[...] += _dot_tn(_bf(av), _bf(d_ref[...]))

        @pl.when(m == n_m - 1)
        def _():
            o_ref[...] = _bf(acc[...])

    in_specs = [pl.BlockSpec((tm, tk), lambda i, j, m: (m, i)), pl.BlockSpec((tm, tn), lambda i, j, m: (m, j))]
    args = [a, d]
    for p in (p1, p2)[:n_p]:
        in_specs.append(pl.BlockSpec((1, tk), lambda i, j, m: (0, i)))
        args.append(p)
    aliases = {}
    if has_buf:
        in_specs.append(pl.BlockSpec(memory_space=pl.ANY))
        args.append(out_buf)
        aliases = {len(args) - 1: 0}
    return pl.pallas_call(
        body, name=name, grid=(Kf // tk, N // tn, n_m), in_specs=in_specs,
        out_specs=pl.BlockSpec((None, tk, tn), lambda i, j, m: (0, i, col_block0 + j)),
        out_shape=jax.ShapeDtypeStruct(out_shape, BF16), input_output_aliases=aliases,
        scratch_shapes=[pltpu.VMEM((tk, tn), F32)], compiler_params=_params(3),
    )(*args)


ROW_BLOCK = 512
ACC_ROWS = 8


def _ln_stats(z):
    mu = jnp.mean(z, axis=-1, keepdims=True)
    zc = z - mu
    var = jnp.mean(zc * zc, axis=-1, keepdims=True)
    return zc, lax.rsqrt(var + LN_EPS)


def ln_fwd(x, y, gate, lng, lnb, *, name):
    S = x.shape[0]

    def body(x_ref, y_ref, gt_ref, g_ref, b_ref, o_ref):
        z = ALPHA * x_ref[...] + (1.0 + gt_ref[...]) * y_ref[...]
        zc, rstd = _ln_stats(z)
        o_ref[...] = (zc * rstd) * g_ref[...] + b_ref[...]

    row = pl.BlockSpec((ROW_BLOCK, D), lambda i: (i, 0))
    vec = pl.BlockSpec((1, D), lambda i: (0, 0))
    return pl.pallas_call(
        body, name=name, grid=(S // ROW_BLOCK,), in_specs=[row, row, vec, vec, vec], out_specs=row,
        out_shape=jax.ShapeDtypeStruct((S, D), F32), compiler_params=_params(1),
    )(x, y, gate, lng, lnb)


def _add_colsum(acc_ref, r, val):
    acc_ref[r:r + 1, :] += jnp.sum(val, axis=0, keepdims=True)


def ln_bwd(x_in, y, gate, lng, *, loss=None, nxt=None, name):
    S = x_in.shape[0]
    has_next = nxt is not None

    def body(*refs):
        if has_next:
            dzn_ref, dun_ref, scn_ref, b_ref = refs[:4]
            k = 4
        else:
            t_ref, b_ref = refs[:2]
            k = 2
        x_ref, y_ref, gt_ref, g_ref = refs[k:k + 4]
        dz_ref, acc_ref, dzs_ref = refs[k + 4:k + 7]
        loss_ref = None if has_next else refs[k + 7]

        @pl.when(pl.program_id(0) == 0)
        def _():
            acc_ref[...] = jnp.zeros_like(acc_ref)
            if not has_next:
                loss_ref[...] = jnp.zeros_like(loss_ref)

        yv = y_ref[...]
        z = ALPHA * x_ref[...] + (1.0 + gt_ref[...]) * yv
        zc, rstd = _ln_stats(z)
        xhat = zc * rstd
        if has_next:
            du = dun_ref[...]
            dout_v = ALPHA * dzn_ref[...] + du * (1.0 + scn_ref[...])
            _add_colsum(acc_ref, 3, du * (xhat * g_ref[...] + b_ref[...]))
            _add_colsum(acc_ref, 4, du)
        else:
            e = (xhat * g_ref[...] + b_ref[...]) - t_ref[...]
            dout_v = e * (1.0 / D)
            per_tok = jnp.sum(e * e, axis=1, keepdims=True) * (1.0 / D)
            loss_ref[...] += 0.5 * jnp.sum(per_tok, axis=0, keepdims=True)
        _add_colsum(acc_ref, 0, dout_v * xhat)
        _add_colsum(acc_ref, 1, dout_v)
        dxh = dout_v * g_ref[...]
        m1 = jnp.mean(dxh, axis=-1, keepdims=True)
        m2 = jnp.mean(dxh * xhat, axis=-1, keepdims=True)
        dz = rstd * (dxh - m1 - xhat * m2)
        _add_colsum(acc_ref, 2, dz * yv)
        dz_ref[...] = dz
        dzs_ref[...] = _bf(dz * (1.0 + gt_ref[...]))

    row = pl.BlockSpec((ROW_BLOCK, D), lambda i: (i, 0))
    vec = pl.BlockSpec((1, D), lambda i: (0, 0))
    if has_next:
        in_specs = [row, row, vec, vec]
        args = list(nxt)
    else:
        in_specs = [row, vec]
        args = list(loss)
    in_specs += [row, row, vec, vec]
    args += [x_in, y, gate, lng]
    out_specs = [row, pl.BlockSpec((ACC_ROWS, D), lambda i: (0, 0)), row]
    out_shape = [jax.ShapeDtypeStruct((S, D), F32), jax.ShapeDtypeStruct((ACC_ROWS, D), F32),
                 jax.ShapeDtypeStruct((S, D), BF16)]
    if not has_next:
        out_specs.append(pl.BlockSpec((8, 128), lambda i: (0, 0)))
        out_shape.append(jax.ShapeDtypeStruct((8, 128), F32))
    return pl.pallas_call(
        body, name=name, grid=(S // ROW_BLOCK,), in_specs=in_specs, out_specs=out_specs, out_shape=out_shape,
        compiler_params=_params(1),
    )(*args)


def combine_final(dz, du, sc, x_in, *, name):
    S = dz.shape[0]

    def body(dz_ref, du_ref, sc_ref, x_ref, dx_ref, acc_ref):
        @pl.when(pl.program_id(0) == 0)
        def _():
            acc_ref[...] = jnp.zeros_like(acc_ref)

        du_v = du_ref[...]
        dx_ref[...] = ALPHA * dz_ref[...] + du_v * (1.0 + sc_ref[...])
        _add_colsum(acc_ref, 3, du_v * x_ref[...])
        _add_colsum(acc_ref, 4, du_v)

    row = pl.BlockSpec((ROW_BLOCK, D), lambda i: (i, 0))
    vec = pl.BlockSpec((1, D), lambda i: (0, 0))
    return pl.pallas_call(
        body, name=name, grid=(S // ROW_BLOCK,), in_specs=[row, row, vec, row],
        out_specs=[row, pl.BlockSpec((ACC_ROWS, D), lambda i: (0, 0))],
        out_shape=[jax.ShapeDtypeStruct((S, D), F32), jax.ShapeDtypeStruct((ACC_ROWS, D), F32)],
        compiler_params=_params(1),
    )(dz, du, sc, x_in)


def _log_sigmoid(x):
    return jnp.minimum(x, 0.0) - jnp.log(1.0 + jnp.exp(-jnp.abs(x)))


def _silu(x):
    return x * (1.0 / (1.0 + jnp.exp(-x)))


def _cumsum_steps(x):
    row = lax.broadcasted_iota(jnp.int32, x.shape, 0)
    step = 1
    while step < x.shape[0]:
        x = x + jnp.where(row >= step, pltpu.roll(x, step, 0), 0.0)
        step *= 2
    return x


@jax.custom_vjp
def _cumsum_rows(x):
    return _cumsum_steps(x)


def _cumsum_rows_fwd(x):
    return _cumsum_steps(x), None


def _cumsum_rows_bwd(_, g):
    return (jnp.sum(g, axis=0, keepdims=True) - _cumsum_steps(g) + g,)


_cumsum_rows.defvjp(_cumsum_rows_fwd, _cumsum_rows_bwd)


def _gla_chunk(q, k, v, g, gk, s0t, w2p, bgk, gn):
    C = q.shape[0]
    row = lax.broadcasted_iota(jnp.int32, (C, C), 0)
    col = lax.broadcasted_iota(jnp.int32, (C, C), 1)
    lower = row >= col
    la = _log_sigmoid(_dot(_bf(gk), _bf(w2p)) + bgk) * GLA_TAU_INV
    outs, states = [], []
    for h in range(GLA_H):
        ks = slice(h * GLA_DKH, (h + 1) * GLA_DKH)
        vs = slice(h * GLA_DVH, (h + 1) * GLA_DVH)
        qh = q[:, ks] * (GLA_DKH ** -0.5)
        kh, vh, gh, lah, s0 = k[:, ks], v[:, vs], g[:, vs], la[:, ks], s0t[h]
        cum = _cumsum_rows(lah)
        e_pos = jnp.exp(cum)
        e_neg = jnp.exp(-cum)
        q_f = qh * e_pos
        a_f = _dot_nt(_bf(q_f), _bf(kh * e_neg))
        a_b = _dot_nt(_bf(qh * e_neg), _bf(kh * e_pos))
        att = jnp.where(lower, a_f, a_b)
        o = _dot(_bf(att), _bf(vh)) + _dot_nt(_bf(q_f), _bf(s0))
        tot = jnp.sum(lah, axis=0, keepdims=True)
        k_end = kh * jnp.exp(tot - cum)
        states.append(s0 * jnp.exp(tot) + _dot_tn(_bf(vh), _bf(k_end)))
        on = o * lax.rsqrt(jnp.mean(o * o, axis=-1, keepdims=True) + RMS_EPS) * gn[:, vs]
        outs.append(on * _silu(gh))
    return jnp.concatenate(outs, axis=1), tuple(states)


def _gla_split(p):
    return (p[:, 0:GLA_DK], p[:, GLA_DK:2 * GLA_DK], p[:, 2 * GLA_DK:2 * GLA_DK + GLA_DV],
            p[:, 2 * GLA_DK + GLA_DV:2 * GLA_DK + 2 * GLA_DV], p[:, 2 * GLA_DK + 2 * GLA_DV:GLA_INP])


def gla_fwd(proj, w2p, bgk, gn, *, comm=None, name):
    S = proj.shape[0]
    n_c = S // CHUNK
    n_s = n_c // GLA_SUB
    rows = GLA_SUB * CHUNK

    def body(p_ref, w_ref, b_ref, gn_ref, o_ref, st_ref, st):
        @pl.when(pl.program_id(0) == 0)
        def _():
            st[...] = jnp.zeros_like(st)

        s = tuple(st[h] for h in range(GLA_H))
        for u in range(GLA_SUB):
            sub = slice(u * CHUNK, (u + 1) * CHUNK)
            for h in range(GLA_H):
                st_ref[u, h] = s[h]
            og, s = _gla_chunk(*_gla_split(p_ref[sub, :]), s, w_ref[...], b_ref[...], gn_ref[...])
            o_ref[sub, :] = _bf(og)
        for h in range(GLA_H):
            st[h] = s[h]

    full = lambda shape: pl.BlockSpec(shape, lambda i: (0,) * len(shape))
    return _call_hosting(
        body, comm, first=lambda: pl.program_id(0) == 0, last=lambda: pl.program_id(0) == n_s - 1,
        name=name, grid=(n_s,),
        in_specs=[pl.BlockSpec((rows, GLA_INP), lambda i: (i, 0)), full((128, GLA_DK)), full((1, GLA_DK)),
                  full((1, GLA_DV))],
        out_specs=[pl.BlockSpec((rows, GLA_DV), lambda i: (i, 0)),
                   pl.BlockSpec((GLA_SUB, GLA_H, GLA_DVH, GLA_DKH), lambda i: (i, 0, 0, 0))],
        out_shape=[jax.ShapeDtypeStruct((S, GLA_DV), BF16),
                   jax.ShapeDtypeStruct((n_c, GLA_H, GLA_DVH, GLA_DKH), F32)],
        scratch_shapes=[pltpu.VMEM((GLA_H, GLA_DVH, GLA_DKH), F32)], compiler_params=_params(1),
        args=(proj, w2p, bgk, gn))


def gla_bwd(proj, dog, states, w2p, bgk, gn, *, comm=None, name):
    S = proj.shape[0]
    n_c = S // CHUNK
    n_s = n_c // GLA_SUB
    rows = GLA_SUB * CHUNK

    def body(p_ref, dog_ref, st_ref, w_ref, b_ref, gn_ref, dp_ref, dw_ref, db_ref, dgn_ref, ds_ref):
        @pl.when(pl.program_id(0) == 0)
        def _():
            ds_ref[...] = jnp.zeros_like(ds_ref)
            dw_ref[...] = jnp.zeros_like(dw_ref)
            db_ref[...] = jnp.zeros_like(db_ref)
            dgn_ref[...] = jnp.zeros_like(dgn_ref)

        ds = tuple(ds_ref[h] for h in range(GLA_H))
        for u in reversed(range(GLA_SUB)):
            sub = slice(u * CHUNK, (u + 1) * CHUNK)
            q, k, v, g, gk = _gla_split(p_ref[sub, :])
            s0 = tuple(st_ref[u, h] for h in range(GLA_H))
            _, vjp = jax.vjp(_gla_chunk, q, k, v, g, gk, s0, w_ref[...], b_ref[...], gn_ref[...])
            dq, dk, dv, dg, dgk, ds, dw, db, dgn = vjp((dog_ref[sub, :], ds))
            dp_ref[sub, 0:GLA_DK] = _bf(dq)
            dp_ref[sub, GLA_DK:2 * GLA_DK] = _bf(dk)
            dp_ref[sub, 2 * GLA_DK:2 * GLA_DK + GLA_DV] = _bf(dv)
            dp_ref[sub, 2 * GLA_DK + GLA_DV:2 * GLA_DK + 2 * GLA_DV] = _bf(dg)
            dp_ref[sub, 2 * GLA_DK + 2 * GLA_DV:GLA_INP] = _bf(dgk)
            dw_ref[...] += dw
            db_ref[...] += db
            dgn_ref[...] += dgn
        for h in range(GLA_H):
            ds_ref[h] = ds[h]

    full = lambda shape: pl.BlockSpec(shape, lambda i: (0,) * len(shape))
    rev = lambda i: (n_s - 1 - i, 0)
    return _call_hosting(
        body, comm, first=lambda: pl.program_id(0) == 0, last=lambda: pl.program_id(0) == n_s - 1,
        name=name, grid=(n_s,),
        in_specs=[pl.BlockSpec((rows, GLA_INP), rev), pl.BlockSpec((rows, GLA_DV), rev),
                  pl.BlockSpec((GLA_SUB, GLA_H, GLA_DVH, GLA_DKH), lambda i: (n_s - 1 - i, 0, 0, 0)),
                  full((128, GLA_DK)), full((1, GLA_DK)), full((1, GLA_DV))],
        out_specs=[pl.BlockSpec((rows, GLA_INP), rev), full((128, GLA_DK)), full((1, GLA_DK)), full((1, GLA_DV))],
        out_shape=[jax.ShapeDtypeStruct((S, GLA_INP), BF16), jax.ShapeDtypeStruct((128, GLA_DK), F32),
                   jax.ShapeDtypeStruct((1, GLA_DK), F32), jax.ShapeDtypeStruct((1, GLA_DV), F32)],
        scratch_shapes=[pltpu.VMEM((GLA_H, GLA_DVH, GLA_DKH), F32)], compiler_params=_params(1),
        args=(proj, dog, states, w2p, bgk, gn))


def _rel_index():
    t = np.arange(REL_TILE)[:, None]
    j = np.arange(REL_TILE)[None, :]
    tiles = []
    for m in REL_TILES:
        chunks = (REL_TILE // CHUNK) * m + j // CHUNK - t // CHUNK
        band = (chunks >= 0) & (chunks <= LEFT // CHUNK)
        dist = LEFT - REL_TILE * m + t - j
        tiles.append(np.where(band, np.minimum(dist, MAX_REL) + MAX_REL, N_REL))
    return jnp.asarray(np.stack(tiles).reshape(1, -1).astype(np.int32))


REL_BLOCK = 2048


def _one_hot(idx_row):
    return (lax.broadcasted_iota(jnp.int32, (REL_PAD, idx_row.shape[1]), 0) == idx_row).astype(F32)


def rel_bias_tiles(rel_pad, idx, *, name):
    E = idx.shape[1]

    def body(r_ref, i_ref, o_ref):
        o_ref[...] = jnp.dot(r_ref[...], _one_hot(i_ref[...]), precision=HIGHEST, preferred_element_type=F32)

    return pl.pallas_call(
        body, name=name, grid=(E // REL_BLOCK,),
        in_specs=[pl.BlockSpec((ATT_H, REL_PAD), lambda i: (0, 0)), pl.BlockSpec((1, REL_BLOCK), lambda i: (0, i))],
        out_specs=pl.BlockSpec((ATT_H, REL_BLOCK), lambda i: (0, i)),
        out_shape=jax.ShapeDtypeStruct((ATT_H, E), F32), compiler_params=_params(1),
    )(rel_pad, idx)


def rel_bias_grad(dtiles_flat, dclip, idx, *, name):
    E = idx.shape[1]
    n_steps = E // REL_BLOCK

    def body(d_ref, c_ref, i_ref, o_ref):
        @pl.when(pl.program_id(0) == 0)
        def _():
            o_ref[...] = jnp.zeros_like(o_ref)

        o_ref[...] += lax.dot_general(d_ref[...], _one_hot(i_ref[...]), (((1,), (1,)), ((), ())),
                                      precision=HIGHEST, preferred_element_type=F32)

        @pl.when(pl.program_id(0) == n_steps - 1)
        def _():
            at_clip = lax.broadcasted_iota(jnp.int32, (1, REL_PAD), 1) == 2 * MAX_REL
            o_ref[...] += jnp.where(at_clip, jnp.sum(c_ref[...], axis=1, keepdims=True), 0.0)

    return pl.pallas_call(
        body, name=name, grid=(n_steps,),
        in_specs=[pl.BlockSpec((ATT_H, REL_BLOCK), lambda i: (0, i)), pl.BlockSpec((ATT_H, 128), lambda i: (0, 0)),
                  pl.BlockSpec((1, REL_BLOCK), lambda i: (0, i))],
        out_specs=pl.BlockSpec((ATT_H, REL_PAD), lambda i: (0, 0)),
        out_shape=jax.ShapeDtypeStruct((ATT_H, REL_PAD), F32), compiler_params=_params(1),
    )(dtiles_flat, dclip, idx)


def _attn_bias(tiles, clip):
    const = jnp.broadcast_to(clip, (REL_TILE, REL_TILE))
    zero = jnp.zeros((REL_TILE, REL_TILE), F32)
    rows = []
    for qt in range(ATT_QB // REL_TILE):
        blocks = []
        for kt in range(ATT_KB // REL_TILE):
            m = kt - qt
            if m in REL_TILES:
                blocks.append(tiles[REL_TILES.index(m)])
            elif 0 <= m < REL_TILES[0]:
                blocks.append(const)
            else:
                blocks.append(zero)
        rows.append(jnp.concatenate(blocks, axis=1))
    return jnp.concatenate(rows, axis=0)


def _attn_bias_grad(ds, dt_ref, dc_ref, a):
    tile = lambda qt, kt: ds[qt * REL_TILE:(qt + 1) * REL_TILE, kt * REL_TILE:(kt + 1) * REL_TILE]
    const = None
    sums = [None] * len(REL_TILES)
    for qt in range(ATT_QB // REL_TILE):
        for kt in range(ATT_KB // REL_TILE):
            m = kt - qt
            if m in REL_TILES:
                n = REL_TILES.index(m)
                sums[n] = tile(qt, kt) if sums[n] is None else sums[n] + tile(qt, kt)
            elif 0 <= m < REL_TILES[0]:
                const = tile(qt, kt) if const is None else const + tile(qt, kt)
    for n, v in enumerate(sums):
        dt_ref[a, n] += v
    dc_ref[a] += jnp.sum(const, axis=0, keepdims=True)


def _attn_head_lanes():
    lane = lax.broadcasted_iota(jnp.int32, (1, 2 * ATT_HD), 1)
    return [(lane >= a * ATT_HD) & (lane < (a + 1) * ATT_HD) for a in range(2)]


def _attn_band_bias(tiles, clip):
    j = lax.broadcasted_iota(jnp.int32, (ATT_QB, ATT_KB), 1)
    t = lax.broadcasted_iota(jnp.int32, (ATT_QB, ATT_KB), 0)
    shift = CHUNK.bit_length() - 1
    chunks = lax.shift_right_logical(j, shift) - lax.shift_right_logical(t, shift)
    band = (chunks >= 0) & (chunks <= LEFT // CHUNK)
    return jnp.where(band, _attn_bias(tiles, clip), NEG_INF)


def _attn_exp(qa, kb, bias, key_bias):
    s = _dot_nt(qa, kb) + bias + key_bias
    e = jnp.exp(s - jnp.max(s, axis=-1, keepdims=True))
    return e, jnp.sum(e, axis=-1, keepdims=True)


def _attn_specs():
    n_hp = ATT_H // 2
    q_spec = pl.BlockSpec((ATT_QB, 128), lambda hp, g: (g, hp))

    def win(col0, back):
        return pl.BlockSpec((ATT_QB, 128), lambda hp, g: (jnp.maximum(g - back, 0), col0 + hp))

    kv_specs = [win(n_hp, 2), win(n_hp, 1), win(n_hp, 0), win(2 * n_hp, 2), win(2 * n_hp, 1), win(2 * n_hp, 0)]
    tiles_spec = pl.BlockSpec((2, len(REL_TILES), REL_TILE, REL_TILE), lambda hp, g: (hp, 0, 0, 0))
    clip_spec = pl.BlockSpec((2, 1, 128), lambda hp, g: (hp, 0, 0))
    return q_spec, kv_specs, tiles_spec, clip_spec


def _attn_window(refs, g):
    kb = jnp.concatenate([_bf(r[...]) for r in refs[0:3]], axis=0)
    vb = jnp.concatenate([_bf(r[...]) for r in refs[3:6]], axis=0)
    j = lax.broadcasted_iota(jnp.int32, (1, ATT_KB), 1)
    return kb, vb, jnp.where(j + (g - 2) * ATT_QB >= 0, 0.0, NEG_INF)


def attn_fwd(qkv, tiles, clip, *, comm=None, name):
    S = qkv.shape[0]
    q_spec, kv_specs, tiles_spec, clip_spec = _attn_specs()

    def body(q_ref, *rest):
        kv_refs, t_ref, c_ref, o_ref, bias = rest[:6], rest[6], rest[7], rest[8], rest[9]
        g = pl.program_id(1)

        @pl.when(g == 0)
        def _():
            for a in range(2):
                bias[a * ATT_QB:(a + 1) * ATT_QB, :] = _attn_band_bias(t_ref[a], c_ref[a])

        kb, vb, key_bias = _attn_window(kv_refs, g)
        q = q_ref[...].astype(F32)
        out = jnp.zeros((ATT_QB, 2 * ATT_HD), F32)
        for a, lanes in enumerate(_attn_head_lanes()):
            mf = lanes.astype(F32)
            e, l = _attn_exp(_bf(q * (mf * ATT_HD ** -0.5)), kb, bias[a * ATT_QB:(a + 1) * ATT_QB, :], key_bias)
            out = out + _dot(_bf(e), vb) * (mf * (1.0 / l))
        o_ref[...] = _bf(out)

    n_hp, n_g = ATT_H // 2, S // ATT_QB
    return _call_hosting(
        body, comm, first=lambda: (pl.program_id(0) == 0) & (pl.program_id(1) == 0),
        last=lambda: (pl.program_id(0) == n_hp - 1) & (pl.program_id(1) == n_g - 1),
        mid=lambda: (pl.program_id(0) == n_hp // 2 + 1) & (pl.program_id(1) == 0),
        name=name, grid=(n_hp, n_g), in_specs=[q_spec] + kv_specs + [tiles_spec, clip_spec],
        out_specs=[q_spec], out_shape=[jax.ShapeDtypeStruct((S, D), BF16)],
        scratch_shapes=[pltpu.VMEM((2 * ATT_QB, ATT_KB), F32)], compiler_params=_params(2),
        args=(*([qkv] * 7), tiles, clip))


def attn_bwd(qkv, do, tiles, clip, *, comm=None, name):
    S = qkv.shape[0]
    q_spec, kv_specs, tiles_spec, clip_spec = _attn_specs()
    col_spec = pl.BlockSpec((S, 128), lambda hp, g: (0, hp))
    sum_spec = pl.BlockSpec((1, 128), lambda hp, g: (0, hp))
    n_g = S // ATT_QB

    def body(q_ref, *rest):
        kv_refs, t_ref, c_ref, do_ref = rest[:6], rest[6], rest[7], rest[8]
        dq_ref, dk_ref, dv_ref, dt_ref, dc_ref, sq_ref, sk_ref, sv_ref, bias = rest[9:]
        g = pl.program_id(1)

        @pl.when(g == 0)
        def _():
            for a in range(2):
                bias[a * ATT_QB:(a + 1) * ATT_QB, :] = _attn_band_bias(t_ref[a], c_ref[a])
            dk_ref[...] = jnp.zeros_like(dk_ref)
            dv_ref[...] = jnp.zeros_like(dv_ref)
            dt_ref[...] = jnp.zeros_like(dt_ref)
            dc_ref[...] = jnp.zeros_like(dc_ref)
            sq_ref[...] = jnp.zeros_like(sq_ref)

        kb, vb, key_bias = _attn_window(kv_refs, g)
        q = q_ref[...].astype(F32)
        do = do_ref[...]
        lanes = _attn_head_lanes()
        mf = [m.astype(F32) * ATT_HD ** -0.5 for m in lanes]
        qs = _bf(jnp.concatenate([q * m for m in mf], axis=0))
        dos = jnp.concatenate([jnp.where(m, do, jnp.zeros_like(do)) for m in lanes], axis=0)
        e, l = _attn_exp(qs, kb, bias[...], key_bias)
        p = e * (1.0 / l)
        dp = _dot_nt(dos, vb)
        ds = p * (dp - jnp.sum(p * dp, axis=-1, keepdims=True))
        ds_b = _bf(ds)
        dq2 = _dot(ds_b, kb)
        dq = dq2[:ATT_QB] * mf[0] + dq2[ATT_QB:] * mf[1]
        dkw = _dot_tn(ds_b, qs)
        dvw = _dot_tn(_bf(p), dos)
        for a in range(2):
            _attn_bias_grad(ds[a * ATT_QB:(a + 1) * ATT_QB], dt_ref, dc_ref, a)
        dq_ref[...] = _bf(dq)
        sq_ref[...] += jnp.sum(dq, axis=0, keepdims=True)
        for blk in range(3):
            src = g - 2 + blk

            @pl.when(src >= 0)
            def _(blk=blk, src=src):
                rows = pl.ds(pl.multiple_of(src * ATT_QB, ATT_QB), ATT_QB)
                dk_ref[rows, :] += dkw[blk * ATT_QB:(blk + 1) * ATT_QB]
                dv_ref[rows, :] += dvw[blk * ATT_QB:(blk + 1) * ATT_QB]

        @pl.when(g == n_g - 1)
        def _():
            sk_ref[...] = jnp.sum(dk_ref[...], axis=0, keepdims=True)
            sv_ref[...] = jnp.sum(dv_ref[...], axis=0, keepdims=True)

    n_hp, n_g = ATT_H // 2, S // ATT_QB
    return _call_hosting(
        body, comm, first=lambda: (pl.program_id(0) == 0) & (pl.program_id(1) == 0),
        last=lambda: (pl.program_id(0) == n_hp - 1) & (pl.program_id(1) == n_g - 1),
        name=name, grid=(n_hp, n_g),
        in_specs=[q_spec] + kv_specs + [tiles_spec, clip_spec, q_spec],
        out_specs=[q_spec, col_spec, col_spec, tiles_spec, clip_spec] + [sum_spec] * 3,
        out_shape=[jax.ShapeDtypeStruct((S, D), BF16)] + [jax.ShapeDtypeStruct((S, D), F32)] * 2
        + [jax.ShapeDtypeStruct((ATT_H, len(REL_TILES), REL_TILE, REL_TILE), F32),
           jax.ShapeDtypeStruct((ATT_H, 1, 128), F32)] + [jax.ShapeDtypeStruct((1, D), F32)] * 3,
        scratch_shapes=[pltpu.VMEM((2 * ATT_QB, ATT_KB), F32)], compiler_params=_params(2),
        args=(*([qkv] * 7), tiles, clip, do))


def mods_partial(c_all, w_ada, *, name):
    n_l, _, n_c = w_ada.shape

    def body(c_ref, w_ref, o_ref):
        o_ref[...] = _dot(_bf(_silu(c_ref[...])), _bf(w_ref[...]))

    return pl.pallas_call(
        body, name=name, grid=(n_l,),
        in_specs=[pl.BlockSpec((N_DEV, D), lambda l: (0, 0)), pl.BlockSpec((None, D, n_c), lambda l: (l, 0, 0))],
        out_specs=pl.BlockSpec((None, N_DEV, n_c), lambda l: (l, 0, 0)),
        out_shape=jax.ShapeDtypeStruct((n_l, N_DEV, n_c), F32), compiler_params=_params(1),
    )(c_all, w_ada)


def w_ada_grad(c_all, dm, *, name):
    n_l, _, n_c = dm.shape

    def body(c_ref, d_ref, o_ref):
        o_ref[...] = lax.dot_general(_silu(c_ref[...]), d_ref[...], (((0,), (0,)), ((), ())),
                                     precision=HIGHEST, preferred_element_type=F32)

    return pl.pallas_call(
        body, name=name, grid=(n_l,),
        in_specs=[pl.BlockSpec((N_DEV, D), lambda l: (0, 0)), pl.BlockSpec((None, N_DEV, n_c), lambda l: (l, 0, 0))],
        out_specs=pl.BlockSpec((None, D, n_c), lambda l: (l, 0, 0)),
        out_shape=jax.ShapeDtypeStruct((n_l, D, n_c), F32), compiler_params=_params(1),
    )(c_all, dm)


def adamw(w, m, v, gparts, *, block_rows, name):
    R, C = w.shape
    n = gparts.shape[0]

    def body(w_ref, m_ref, v_ref, g_ref, go_ref, d_ref, mo_ref, vo_ref):
        g = g_ref[0].astype(F32)
        for k in range(1, n):
            g = g + g_ref[k].astype(F32)
        m_new = ADAM_B1 * m_ref[...] + (1.0 - ADAM_B1) * g
        v_new = ADAM_B2 * v_ref[...] + (1.0 - ADAM_B2) * (g * g)
        m_hat = m_new / (1.0 - ADAM_B1 ** ADAM_STEP)
        v_hat = v_new / (1.0 - ADAM_B2 ** ADAM_STEP)
        go_ref[...] = g
        d_ref[...] = -ADAM_LR * (m_hat / (jnp.sqrt(v_hat) + ADAM_EPS) + ADAM_WD * w_ref[...])
        mo_ref[...] = m_new
        vo_ref[...] = v_new

    blk = pl.BlockSpec((block_rows, C), lambda i: (i, 0))
    return pl.pallas_call(
        body, name=name, grid=(R // block_rows,),
        in_specs=[blk, blk, blk, pl.BlockSpec((n, block_rows, C), lambda i: (0, i, 0))],
        out_specs=[blk] * 4, out_shape=[jax.ShapeDtypeStruct((R, C), F32)] * 4, compiler_params=_params(1),
    )(w, m, v, gparts)


def adamw_nd(w, m, v, gparts, *, name):
    shape = w.shape
    two = (int(np.prod(shape[:-1])), shape[-1])
    rows = two[0]
    block_rows = rows
    for cand in (512, 256):
        if rows > cand and rows % cand == 0:
            block_rows = cand
            break
    outs = adamw(w.reshape(two), m.reshape(two), v.reshape(two), gparts.reshape((gparts.shape[0],) + two),
                 block_rows=block_rows, name=name)
    return [o.reshape(shape) for o in outs]


def sum_parts(parts, *, name):
    n, R, C = parts.shape

    def body(p_ref, o_ref):
        acc = p_ref[0]
        for k in range(1, n):
            acc = acc + p_ref[k]
        o_ref[...] = acc

    return pl.pallas_call(
        body, name=name, in_specs=[pl.BlockSpec((n, R, C), lambda: (0, 0, 0))],
        out_specs=pl.BlockSpec((R, C), lambda: (0, 0)), out_shape=jax.ShapeDtypeStruct((R, C), F32),
        compiler_params=pltpu.CompilerParams(vmem_limit_bytes=VMEM_LIMIT),
    )(parts)


def _my_place():
    return lax.axis_index("x"), lax.axis_index("y"), lax.axis_index("c")


def _full_shape(kind, shard):
    n_l, rows, cols = shard
    return {"col": (n_l, rows, N_DEV * cols), "row": (n_l, N_DEV * rows, cols), "stk": (N_DEV, n_l, rows, cols)}[kind]


def _slab(ref, kind, dev, shard):
    _, rows, cols = shard
    if kind == "col":
        return ref.at[:, :, pl.ds(pl.multiple_of(dev * cols, 128), cols)]
    if kind == "row":
        return ref.at[:, pl.ds(pl.multiple_of(dev * rows, 8), rows), :]
    return ref.at[dev]


def all_gather(x_shard, *, name):
    m_per, n = x_shard.shape

    def body(x_ref, out_ref, send_sems, recv_sems, local_sem):
        x, y, c = _my_place()
        me, sibling = (x, y, c), (x, y, 1 - c)
        chips = [(1 - x, y), (x, 1 - y), (1 - x, 1 - y)]

        def rows(px, py, pc):
            return out_ref.at[pl.ds((4 * px + 2 * py + pc) * m_per, m_per), :]

        def copy(k, block, to, src=None):
            return pltpu.make_async_remote_copy(
                src_ref=rows(*block) if src is None else src, dst_ref=rows(*block),
                send_sem=send_sems.at[k], recv_sem=recv_sems.at[k], device_id=to, device_id_type=MESH)

        mine = pltpu.make_async_copy(x_ref, rows(*me), local_sem)
        mine.start()
        first = [copy(0, me, sibling, src=x_ref)]
        first += [copy(1 + j, me, (*chip, c), src=x_ref) for j, chip in enumerate(chips)]
        for cp in first:
            cp.start()
        passed = [copy(4 + j, (*chip, c), sibling) for j, chip in enumerate(chips)]
        for j, chip in enumerate(chips):
            copy(1 + j, (*chip, c), me).wait_recv()
            passed[j].start()
        copy(0, sibling, me).wait_recv()
        for j, chip in enumerate(chips):
            copy(4 + j, (*chip, 1 - c), me).wait_recv()
        for cp in first + passed:
            cp.wait_send()
        mine.wait()

    return pl.pallas_call(
        body, name=name, out_shape=jax.ShapeDtypeStruct((N_DEV * m_per, n), x_shard.dtype),
        in_specs=[pl.BlockSpec(memory_space=pltpu.VMEM)], out_specs=pl.BlockSpec(memory_space=pltpu.VMEM),
        scratch_shapes=[pltpu.SemaphoreType.DMA((7,)), pltpu.SemaphoreType.DMA((7,)), pltpu.SemaphoreType.DMA],
        compiler_params=pltpu.CompilerParams(vmem_limit_bytes=VMEM_LIMIT),
    )(x_shard)


def gather_plan(shards, kinds, layers):
    n_t = len(shards)
    shapes = [(1,) + tuple(s.shape[1:]) for s in shards]

    def copies(x_refs, out_refs, sems):
        send_sems, recv_sems, local_sems = sems
        x, y, c = _my_place()
        me, sibling = (x, y, c), (x, y, 1 - c)
        chips = [(1 - x, y), (x, 1 - y), (1 - x, 1 - y)]
        own = [x_refs[t].at[pl.ds(layers[t], 1)] for t in range(n_t)]

        def slab(t, px, py, pc):
            return _slab(out_refs[t], kinds[t], 4 * px + 2 * py + pc, shapes[t])

        def copy(t, k, block, to, src=None):
            return pltpu.make_async_remote_copy(
                src_ref=slab(t, *block) if src is None else src, dst_ref=slab(t, *block),
                send_sem=send_sems.at[7 * t + k], recv_sem=recv_sems.at[7 * t + k], device_id=to,
                device_id_type=MESH)

        mine = [pltpu.make_async_copy(own[t], slab(t, *me), local_sems.at[t]) for t in range(n_t)]
        sends = []
        for t in range(n_t):
            sends.append(copy(t, 0, me, sibling, src=own[t]))
            sends += [copy(t, 1 + j, me, (*chip, c), src=own[t]) for j, chip in enumerate(chips)]
        return mine, sends, copy, me, sibling, chips, c

    def first(x_refs, out_refs, sems):
        mine, sends = copies(x_refs, out_refs, sems)[:2]
        for cp in mine + sends:
            cp.start()

    def forward(x_refs, out_refs, sems):
        _, _, copy, me, sibling, chips, c = copies(x_refs, out_refs, sems)
        for j, chip in enumerate(chips):
            for t in range(n_t):
                copy(t, 1 + j, (*chip, c), me).wait_recv()
                copy(t, 4 + j, (*chip, c), sibling).start()

    def finish(x_refs, out_refs, sems):
        mine, sends, copy, me, sibling, chips, c = copies(x_refs, out_refs, sems)
        passed = [copy(t, 4 + j, (*chip, c), sibling) for j, chip in enumerate(chips) for t in range(n_t)]
        for t in range(n_t):
            copy(t, 0, sibling, me).wait_recv()
        for j, chip in enumerate(chips):
            for t in range(n_t):
                copy(t, 4 + j, (*chip, 1 - c), me).wait_recv()
        for cp in sends + passed:
            cp.wait_send()
        for cp in mine:
            cp.wait()

    def last(x_refs, out_refs, sems):
        forward(x_refs, out_refs, sems)
        finish(x_refs, out_refs, sems)

    return Hosted(
        list(shards), [jax.ShapeDtypeStruct(_full_shape(k, shp), s.dtype) for k, shp, s in zip(kinds, shapes, shards)],
        [pltpu.SemaphoreType.DMA((7 * n_t,)), pltpu.SemaphoreType.DMA((7 * n_t,)), pltpu.SemaphoreType.DMA((n_t,))],
        first, last, mid=forward, last_after_mid=finish)


def scatter_plan(grads, kinds, shapes):
    n_t = len(grads)

    def copies(g_refs, out_refs, sems):
        send_sems, recv_sems, local_sems = sems
        x, y, c = _my_place()
        me = 4 * x + 2 * y + c
        local = [pltpu.make_async_copy(_slab(g_refs[t], kinds[t], me, shapes[t]), out_refs[t].at[me],
                                       local_sems.at[t]) for t in range(n_t)]
        remote = []
        for t in range(n_t):
            for r in range(1, N_DEV):
                px = 1 - x if r & 4 else x
                py = 1 - y if r & 2 else y
                pc = 1 - c if r & 1 else c
                remote.append(pltpu.make_async_remote_copy(
                    src_ref=_slab(g_refs[t], kinds[t], 4 * px + 2 * py + pc, shapes[t]), dst_ref=out_refs[t].at[me],
                    send_sem=send_sems.at[7 * t + r - 1], recv_sem=recv_sems.at[7 * t + r - 1],
                    device_id=(px, py, pc), device_id_type=MESH))
        return local, remote

    def first(g_refs, out_refs, sems):
        local, remote = copies(g_refs, out_refs, sems)
        for cp in local + remote:
            cp.start()

    def last(g_refs, out_refs, sems):
        local, remote = copies(g_refs, out_refs, sems)
        for cp in remote + local:
            cp.wait()

    return Hosted(
        list(grads), [jax.ShapeDtypeStruct((N_DEV,) + tuple(s), BF16) for s in shapes],
        [pltpu.SemaphoreType.DMA((7 * n_t,)), pltpu.SemaphoreType.DMA((7 * n_t,)), pltpu.SemaphoreType.DMA((n_t,))],
        first, last)


def adamw_layer(w, m, v, parts, layer, bufs, *, name):
    n_l, rows, cols = w.shape
    n = parts.shape[0]
    tr = min(rows, 256)

    def body(w_ref, m_ref, v_ref, g_ref, *rest):
        go_ref, d_ref, mo_ref, vo_ref = rest[-4:]
        g = g_ref[0].astype(F32)
        for k in range(1, n):
            g = g + g_ref[k].astype(F32)
        m_new = ADAM_B1 * m_ref[...] + (1.0 - ADAM_B1) * g
        v_new = ADAM_B2 * v_ref[...] + (1.0 - ADAM_B2) * (g * g)
        m_hat = m_new / (1.0 - ADAM_B1 ** ADAM_STEP)
        v_hat = v_new / (1.0 - ADAM_B2 ** ADAM_STEP)
        go_ref[...] = g
        d_ref[...] = -ADAM_LR * (m_hat / (jnp.sqrt(v_hat) + ADAM_EPS) + ADAM_WD * w_ref[...])
        mo_ref[...] = m_new
        vo_ref[...] = v_new

    blk = pl.BlockSpec((None, tr, cols), lambda i: (layer, i, 0))
    in_specs = [blk, blk, blk, pl.BlockSpec((n, None, tr, cols), lambda i: (0, 0, i, 0))]
    args = [w, m, v, parts]
    aliases = {}
    if bufs is not None:
        in_specs += [pl.BlockSpec(memory_space=pl.ANY)] * 4
        args += list(bufs)
        aliases = {4 + k: k for k in range(4)}
    return pl.pallas_call(
        body, name=name, grid=(rows // tr,), in_specs=in_specs, out_specs=[blk] * 4,
        out_shape=[jax.ShapeDtypeStruct((n_l, rows, cols), F32)] * 4, input_output_aliases=aliases,
        compiler_params=_params(1),
    )(*args)


BIG =("gla_w_in", "gla_w_out", "att_w_in", "att_w_out", "ff_w1", "ff_w2")
KIND = {"gla_w_in": "stk", "gla_w_out": "row", "att_w_in": "col", "att_w_out": "row", "ff_w1": "col", "ff_w2": "row"}


def _pack_small(arrs):
    parts = []
    for a in arrs:
        f = a.reshape(-1)
        parts.append(jnp.pad(f, (0, -f.shape[0] % 128)))
    flat = jnp.concatenate(parts)
    flat = jnp.pad(flat, (0, -flat.shape[0] % 1024))
    return flat.reshape(-1, 128)


def _unpack_small(packed, shapes):
    flat = packed.reshape(packed.shape[:-2] + (-1,))
    out, off = [], 0
    for shp in shapes:
        n = int(np.prod(shp))
        out.append(flat[..., off:off + n].reshape(packed.shape[:-2] + tuple(shp)))
        off += n + (-n % 128)
    return out


def _vec(a):
    return a.reshape(1, -1)


def _layer_weights(i):
    mixer = "gla" if i % 2 == 0 else "att"
    return [(f"{mixer}_w_in", i // 2), (f"{mixer}_w_out", i // 2), ("ff_w1", i), ("ff_w2", i)]


def _trunk(x, target, mods, sm, w, m, v):
    shard_bf = {n: w[n].astype(BF16) for n in BIG}

    def gather_of(names):
        return gather_plan([shard_bf[n] for n, _ in names], [KIND[n] for n, _ in names], [l for _, l in names])

    def gather_under_core(i):
        names = _layer_weights(i)[2:] + (_layer_weights(i + 1)[:2] if i + 1 < DEPTH else [])
        return names, gather_of(names)

    wts = {}

    def keep_gathered(names, arrays):
        for (n, l), a in zip(names, arrays):
            if KIND[n] == "stk":
                a = a.transpose(1, 2, 0, 3).reshape(1, D, GLA_IN)
                a = jnp.pad(a, ((0, 0), (0, 0), (0, GLA_INP - GLA_IN)))
            wts[n, l] = a

    first_names, rest_names = _layer_weights(0)[:1], _layer_weights(0)[1:2]
    keep_gathered(first_names, run_hosted(gather_of(first_names), name="gather_first"))

    rel_idx = _rel_index()
    saved = []
    for i in range(DEPTH):
        sh1, sc1, g1, sh2, sc2, g2 = [mods[i, k:k + 1] for k in range(6)]
        rec = {"x0": x}
        j = i // 2
        nxt_names, nxt_plan = gather_under_core(i)
        if i % 2 == 0:
            w2p = jnp.pad(sm["gla_w_gk2"][j], ((0, 128 - GLA_RANK), (0, 0)))
            bgk, gn = _vec(sm["gla_b_gk"][j]), _vec(sm["gla_g_norm"][j])
            if i == 0:
                proj, got0 = mm_nn(x, wts["gla_w_in", j], 0, pro="mod", p1=sc1, p2=sh1, tm=512, tn=GLA_INP,
                                   comm=gather_of(rest_names), name=f"gla_proj_{i}")
                keep_gathered(rest_names, got0)
            else:
                proj = mm_nn(x, wts["gla_w_in", j], 0, pro="mod", p1=sc1, p2=sh1, tm=512, tn=GLA_INP,
                             name=f"gla_proj_{i}")
            (og, states), got = gla_fwd(proj, w2p, bgk, gn, comm=nxt_plan, name=f"gla_core_{i}")
            y = mm_nn(og, wts["gla_w_out", j], 0, tm=1024, tn=1024, name=f"gla_out_{i}")
            rec.update(kind="gla", j=j, w2p=w2p, bgk=bgk, gn=gn, proj=proj, og=og, states=states)
        else:
            rel = sm["att_rel_bias"][j]
            rel_pad = jnp.pad(rel, ((0, 0), (0, REL_PAD - N_REL)), constant_values=NEG_INF)
            tiles = rel_bias_tiles(rel_pad, rel_idx, name=f"att_bias_{i}")
            tiles = tiles.reshape(ATT_H, len(REL_TILES), REL_TILE, REL_TILE)
            clip = jnp.broadcast_to(rel[:, 2 * MAX_REL][:, None, None], (ATT_H, 1, 128))
            qkv = mm_nn(x, wts["att_w_in", j], 0, pro="mod", p1=sc1, p2=sh1, bias=_vec(sm["att_b_in"][j]),
                        out_dtype=BF16, tm=512, tn=3 * D, name=f"att_proj_{i}")
            (o,), got = attn_fwd(qkv, tiles, clip, comm=nxt_plan, name=f"att_core_{i}")
            y = mm_nn(o, wts["att_w_out", j], 0, tm=1024, tn=1024, name=f"att_out_{i}")
            rec.update(kind="att", j=j, tiles=tiles, clip=clip, qkv=qkv, o=o)
        keep_gathered(nxt_names, got)
        x1 = ln_fwd(x, y, g1, _vec(sm["ln_g"][i, 0]), _vec(sm["ln_b"][i, 0]), name=f"ln_mix_{i}")
        h = mm_nn(x1, wts["ff_w1", i], 0, pro="mod", p1=sc2, p2=sh2, out_dtype=BF16, tm=512, tn=D_FF,
                  name=f"ff_up_{i}")
        y2 = mm_nn(h, wts["ff_w2", i], 0, pro="relu2", tm=512, tn=D, name=f"ff_down_{i}")
        rec.update(y=y, x1=x1, h=h, y2=y2)
        saved.append(rec)
        if i + 1 < DEPTH:
            x = ln_fwd(x1, y2, g2, _vec(sm["ln_g"][i, 1]), _vec(sm["ln_b"][i, 1]), name=f"ln_ff_{i}")

    gw = {}

    def wgrad(weight, layer, a, d, *, tn, tk=1024, tm=512, col_block0=0, **kw):
        gw[weight, layer] = mm_tn(a, d, tk=tk, tn=tn, tm=tm, out_buf=gw.get((weight, layer)),
                                  out_shape=wts[weight, layer].shape, col_block0=col_block0, **kw)

    def scatter_layer(units):
        grads = []
        for n, l in units:
            g = gw[n, l]
            if KIND[n] == "stk":
                g = g[:, :, :GLA_IN].reshape(1, D, N_DEV, GLA_IN // N_DEV).transpose(2, 0, 1, 3)
            grads.append(g)
        return scatter_plan(grads, [KIND[n] for n, _ in units], [(1,) + tuple(w[n].shape[1:]) for n, _ in units])

    results = {n: None for n in BIG}

    def update(units, parts):
        for (n, l), p in zip(units, parts):
            results[n] = adamw_layer(w[n], m[n], v[n], p, l, results[n], name=f"adamw_{n}_{l}")

    gs = {"ln_g": [[None, None] for _ in range(DEPTH)], "ln_b": [[None, None] for _ in range(DEPTH)],
          "gla_w_gk2": [None] * 2, "gla_b_gk": [None] * 2, "gla_g_norm": [None] * 2, "att_b_in": [None] * 2,
          "att_rel_bias": [None] * 2}
    dmods = [[None] * 6 for _ in range(DEPTH)]
    nxt = None
    nxt_slot = None
    for i in reversed(range(DEPTH)):
        rec = saved[i]
        sh1, sc1, g1, sh2, sc2, g2 = [mods[i, k:k + 1] for k in range(6)]
        x0, x1 = rec["x0"], rec["x1"]
        if nxt is None:
            dz2, acc, dzs2, loss = ln_bwd(x1, rec["y2"], g2, _vec(sm["ln_g"][i, 1]),
                                          loss=(target, _vec(sm["ln_b"][i, 1])), name=f"ln_ff_bwd_{i}")
        else:
            dz2, acc, dzs2 = ln_bwd(x1, rec["y2"], g2, _vec(sm["ln_g"][i, 1]),
                                    nxt=nxt[:3] + (_vec(sm["ln_b"][i, 1]),), name=f"ln_ff_bwd_{i}")
            dmods[nxt_slot[0]][nxt_slot[1]] = acc[3]
            dmods[nxt_slot[0]][nxt_slot[2]] = acc[4]
        gs["ln_g"][i][1], gs["ln_b"][i][1], dmods[i][5] = acc[0], acc[1], acc[2]
        dh = mm_nt([dzs2], wts["ff_w2", i], 0, epi_h=rec["h"], out_dtype=BF16, tm=1024, tn=1024,
                   name=f"ff_down_bwd_{i}")
        wgrad("ff_w2", i, rec["h"], dzs2, pro="relu2", tk=2048, tn=1024, name=f"ff_w2_grad_{i}")
        du2 = mm_nt([dh], wts["ff_w1", i], 0, tm=1024, tn=1024, name=f"ff_up_bwd_{i}")
        wgrad("ff_w1", i, x1, dh, pro="mod", p1=sc2, p2=sh2, tn=2048, name=f"ff_w1_grad_{i}")
        dz1, acc, dzs1 = ln_bwd(x0, rec["y"], g1, _vec(sm["ln_g"][i, 0]),
                                nxt=(dz2, du2, sc2, _vec(sm["ln_b"][i, 0])), name=f"ln_mix_bwd_{i}")
        dmods[i][4], dmods[i][3] = acc[3], acc[4]
        gs["ln_g"][i][0], gs["ln_b"][i][0], dmods[i][2] = acc[0], acc[1], acc[2]
        j = rec["j"]
        w_in, w_out = _layer_weights(i)[:2]
        if rec["kind"] == "gla":
            dog = mm_nt([dzs1], wts[w_out], 0, tm=1024, tn=1024, name=f"gla_out_bwd_{i}")
            wgrad(*w_out, rec["og"], dzs1, tn=1024, name=f"gla_wout_grad_{i}")
        else:
            do = mm_nt([dzs1], wts[w_out], 0, out_dtype=BF16, tm=1024, tn=1024, name=f"att_out_bwd_{i}")
            wgrad(*w_out, rec["o"], dzs1, tn=1024, name=f"att_wout_grad_{i}")
        units = [("ff_w1", i), ("ff_w2", i), w_out] + ([_layer_weights(i + 1)[0]] if i + 1 < DEPTH else [])
        plan = scatter_layer(units)
        if rec["kind"] == "gla":
            (dproj, dw2p, dbgk, dgn), parts = gla_bwd(rec["proj"], dog, rec["states"], rec["w2p"], rec["bgk"],
                                                      rec["gn"], comm=plan, name=f"gla_core_bwd_{i}")
            gs["gla_w_gk2"][j], gs["gla_b_gk"][j], gs["gla_g_norm"][j] = dw2p[:GLA_RANK], dbgk[0], dgn[0]
            wgrad(*w_in, x0, dproj, pro="mod", p1=sc1, p2=sh1, tk=512, tn=GLA_INP, name=f"gla_win_grad_{i}")
            if i == 0:
                du1, last_parts = mm_nt([dproj], wts[w_in], 0, tm=1024, tn=1024, comm=scatter_layer([w_in]),
                                        name=f"gla_proj_bwd_{i}")
                update([w_in], last_parts)
            else:
                du1 = mm_nt([dproj], wts[w_in], 0, tm=1024, tn=1024, name=f"gla_proj_bwd_{i}")
        else:
            (dq, dk, dv, dtiles, dclip, sq, sk, sv), parts = attn_bwd(rec["qkv"], do, rec["tiles"], rec["clip"],
                                                                      comm=plan, name=f"att_core_bwd_{i}")
            drel = rel_bias_grad(dtiles.reshape(ATT_H, -1), dclip.reshape(ATT_H, 128), rel_idx,
                                 name=f"att_bias_grad_{i}")
            gs["att_rel_bias"][j] = drel[:, :N_REL]
            gs["att_b_in"][j] = jnp.concatenate([sq[0], sk[0], sv[0]])
            du1 = mm_nt([dq, dk, dv], wts[w_in], 0, tm=512, tn=1024, name=f"att_proj_bwd_{i}")
            for n, t in enumerate((dq, dk, dv)):
                wgrad(*w_in, x0, t, pro="mod", p1=sc1, p2=sh1, tn=1024, col_block0=n, name=f"att_win_grad_{i}_{n}")
        update(units, parts)
        nxt = (dz1, du1, sc1, x0)
        nxt_slot = (i, 1, 0)
    dx, acc = combine_final(nxt[0], nxt[1], nxt[2], nxt[3], name="grad_x")
    dmods[0][1], dmods[0][0] = acc[3], acc[4]
    dmods = jnp.stack([jnp.stack(r) for r in dmods])
    gs = {k: jnp.stack([jnp.stack(r) if isinstance(r, list) else r for r in v]) for k, v in gs.items()}
    return loss, dx, dmods, gs, results


WEIGHTS = ("w_ada", "b_ada", "ln_g", "ln_b", "gla_w_in", "gla_w_gk2", "gla_b_gk", "gla_g_norm", "gla_w_out",
           "att_w_in", "att_b_in", "att_rel_bias", "att_w_out", "ff_w1", "ff_w2")
SMALL_SHARDED = {"ln_g": 2, "ln_b": 2, "gla_w_gk2": 2, "gla_g_norm": 2, "att_b_in": 1}
SMALL_GRADS = ("ln_g", "ln_b", "gla_w_gk2", "gla_b_gk", "gla_g_norm", "att_b_in", "att_rel_bias")


def kernel(x, c, w_ada, b_ada, ln_g, ln_b, gla_w_in, gla_w_gk2, gla_b_gk, gla_g_norm, gla_w_out, att_w_in, att_b_in, att_rel_bias, att_w_out, ff_w1, ff_w2, loss_target, m_w_ada, m_b_ada, m_ln_g, m_ln_b, m_gla_w_in, m_gla_w_gk2, m_gla_b_gk, m_gla_g_norm, m_gla_w_out, m_att_w_in, m_att_b_in, m_att_rel_bias, m_att_w_out, m_ff_w1, m_ff_w2, v_w_ada, v_b_ada, v_ln_g, v_ln_b, v_gla_w_in, v_gla_w_gk2, v_gla_b_gk, v_gla_g_norm, v_gla_w_out, v_att_w_in, v_att_b_in, v_att_rel_bias, v_att_w_out, v_ff_w1, v_ff_w2):
    w = dict(w_ada=w_ada, b_ada=b_ada, ln_g=ln_g, ln_b=ln_b, gla_w_in=gla_w_in, gla_w_gk2=gla_w_gk2,
             gla_b_gk=gla_b_gk, gla_g_norm=gla_g_norm, gla_w_out=gla_w_out, att_w_in=att_w_in, att_b_in=att_b_in,
             att_rel_bias=att_rel_bias, att_w_out=att_w_out, ff_w1=ff_w1, ff_w2=ff_w2)
    m = dict(w_ada=m_w_ada, b_ada=m_b_ada, ln_g=m_ln_g, ln_b=m_ln_b, gla_w_in=m_gla_w_in, gla_w_gk2=m_gla_w_gk2,
             gla_b_gk=m_gla_b_gk, gla_g_norm=m_gla_g_norm, gla_w_out=m_gla_w_out, att_w_in=m_att_w_in,
             att_b_in=m_att_b_in, att_rel_bias=m_att_rel_bias, att_w_out=m_att_w_out, ff_w1=m_ff_w1, ff_w2=m_ff_w2)
    v = dict(w_ada=v_w_ada, b_ada=v_b_ada, ln_g=v_ln_g, ln_b=v_ln_b, gla_w_in=v_gla_w_in, gla_w_gk2=v_gla_w_gk2,
             gla_b_gk=v_gla_b_gk, gla_g_norm=v_gla_g_norm, gla_w_out=v_gla_w_out, att_w_in=v_att_w_in,
             att_b_in=v_att_b_in, att_rel_bias=v_att_rel_bias, att_w_out=v_att_w_out, ff_w1=v_ff_w1, ff_w2=v_ff_w2)
    xi, yi, ci = _my_place()
    me = 4 * xi + 2 * yi + ci

    small_names = tuple(SMALL_SHARDED)
    small_in = _pack_small([c] + [w[n] for n in small_names])
    small_all = all_gather(small_in, name="gather_small").reshape(N_DEV, -1, 128)
    parts = _unpack_small(small_all, [c.shape] + [w[n].shape for n in small_names])
    c_all = parts[0].reshape(N_DEV, D)
    sm = {"gla_b_gk": gla_b_gk, "att_rel_bias": att_rel_bias}
    for n, p in zip(small_names, parts[1:]):
        ax = SMALL_SHARDED[n]
        sm[n] = jnp.moveaxis(p, 0, ax).reshape(p.shape[1:ax + 1] + (N_DEV * p.shape[ax + 1],) + p.shape[ax + 2:])

    n_ada = w_ada.shape[2]
    mp = mods_partial(c_all, w_ada, name="mods_partial")
    mp_all = all_gather(mp.reshape(DEPTH * N_DEV, n_ada), name="gather_mods")
    mp_all = mp_all.reshape(N_DEV, DEPTH, N_DEV, n_ada)
    mods = lax.dynamic_index_in_dim(mp_all, me, axis=2, keepdims=False)
    mods = mods.transpose(1, 0, 2).reshape(DEPTH, 6 * D) + b_ada
    mods = mods.reshape(DEPTH, 6, D)

    loss, dx, dmods, gs, results = _trunk(x.reshape(x.shape[1:]), loss_target.reshape(x.shape[1:]), mods, sm, w, m, v)
    loss = lax.psum(loss[0, 0], ("x", "y", "c"))

    dm_flat = dmods.reshape(DEPTH, 6 * D)
    small_g = [dm_flat] + [gs[n].reshape(sm[n].shape) for n in SMALL_GRADS]
    small_shapes = [a.shape for a in small_g]
    sg_all = all_gather(_pack_small(small_g), name="gather_small_grads").reshape(N_DEV, -1, 128)
    summed = _unpack_small(sum_parts(sg_all, name="sum_small_grads"), small_shapes)
    g_full = dict(zip(("b_ada",) + SMALL_GRADS, summed))
    dm_all = _unpack_small(sg_all, small_shapes)[0]
    dm_mine = lax.dynamic_slice_in_dim(dm_all, me * n_ada, n_ada, axis=2).transpose(1, 0, 2)
    g_w_ada = w_ada_grad(c_all, dm_mine, name="w_ada_grad")

    results["w_ada"] = adamw_nd(w_ada, m_w_ada, v_w_ada, g_w_ada[None], name="adamw_w_ada")
    for n in ("b_ada",) + SMALL_GRADS:
        g = g_full[n]
        if n in SMALL_SHARDED:
            ax = SMALL_SHARDED[n]
            width = w[n].shape[ax]
            g = lax.dynamic_slice_in_dim(g, me * width, width, axis=ax)
        results[n] = adamw_nd(w[n], m[n], v[n], g[None], name=f"adamw_{n}")

    out = [loss, dx[None]]
    for k in range(4):
        out += [results[n][k] for n in WEIGHTS]
    return tuple(out)
```
